```python
import math
import jax, jax.numpy as jnp
from jax import lax
import numpy as np

D_MODEL = 1024
BATCH = 8
SEQ = 4096
DEPTH = 2

N_MIXERS = 2
N_A = (DEPTH + 1) // 2
N_B = DEPTH // 2
HG_HEAD_DIM = 128
HG_HEADS = D_MODEL // HG_HEAD_DIM
HG_DIM = HG_HEADS * HG_HEAD_DIM
HG_CHUNK = 64
D_INNER = 2 * D_MODEL
SSM_HEAD_DIM = 64
SSM_HEADS = D_INNER // SSM_HEAD_DIM
SSM_GROUPS = 8
SSM_HPG = SSM_HEADS // SSM_GROUPS
SSM_STATE = 128
SSM_CONV = 5
SSD_CHUNK = 128
GN = SSM_GROUPS * SSM_STATE
CONV_DIM = D_INNER + 2 * GN
B_PROJ = 2 * D_INNER + 2 * GN + 2 * SSM_HEADS
D_FF = ((8 * D_MODEL // 3 + 255) // 256) * 256
FFN_CONV = 3
EPS = 1e-6

kernel_name = "hgrn2_mamba2_convglu_bidir_hybrid"


def rmsnorm(x, w):
    xf = x.astype(jnp.float32)
    y = xf * lax.rsqrt(jnp.mean(xf * xf, axis=-1, keepdims=True) + EPS)
    return (y * w.astype(jnp.float32)).astype(x.dtype)


def group_rmsnorm(x, w, groups):
    shp = x.shape
    xf = x.astype(jnp.float32).reshape(*shp[:-1], groups, shp[-1] // groups)
    y = xf * lax.rsqrt(jnp.mean(xf * xf, axis=-1, keepdims=True) + EPS)
    y = y.reshape(shp) * w.astype(jnp.float32)
    return y.astype(x.dtype)


def dwconv_centred(x, w, b):
    ch = x.shape[-1]
    y = lax.conv_general_dilated(x, w[:, None, :].astype(x.dtype), window_strides=(1,), padding='SAME',
                                 dimension_numbers=('NWC', 'WIO', 'NWC'), feature_group_count=ch)
    return y + b.astype(x.dtype)


def flip(t):
    return jnp.flip(t, axis=1)


def gla_chunked(q, k, v, logf):
    bsz, seq, h, dk = q.shape
    dv = v.shape[-1]
    nc = seq // HG_CHUNK

    def chunks(t):
        return t.reshape(bsz, nc, HG_CHUNK, h, t.shape[-1]).transpose(1, 0, 3, 2, 4)

    lower = jnp.tril(jnp.ones((HG_CHUNK, HG_CHUNK), dtype=bool))

    def step(state, inp):
        qc, kc, vc, gc = inp
        b = jnp.cumsum(gc, axis=2)
        o_inter = jnp.einsum('bhtk,bhkv->bhtv', qc * jnp.exp(b), state)
        rel = jnp.where(lower[:, :, None], b[:, :, :, None, :] - b[:, :, None, :, :], -jnp.inf)
        att = jnp.einsum('bhtsk,bhsk->bhts', qc[:, :, :, None, :] * jnp.exp(rel), kc)
        o = o_inter + jnp.einsum('bhts,bhsv->bhtv', att, vc)
        b_last = b[:, :, -1:, :]
        state = (jnp.exp(b_last)[:, :, 0, :, None] * state
                 + jnp.einsum('bhsk,bhsv->bhkv', kc * jnp.exp(b_last - b), vc))
        return state, o

    state0 = jnp.zeros((bsz, h, dk, dv), q.dtype)
    _, o = lax.scan(step, state0, (chunks(q), chunks(k), chunks(v), chunks(logf)))
    return o.transpose(1, 0, 3, 2, 4).reshape(bsz, seq, h, dv)


def hgrn2_mixer(u, w_in, lb, norm_w, w_out):
    bsz, seq, _ = u.shape
    q, f_fw, f_bw, iv, g = jnp.split(u @ w_in, 5, axis=-1)

    def heads(t):
        return t.reshape(bsz, seq, HG_HEADS, HG_HEAD_DIM)

    def gate(fr):
        f = lb + (1.0 - lb) * jax.nn.sigmoid(fr.astype(jnp.float32))
        return heads(jnp.log(f)).astype(u.dtype), heads(1.0 - f).astype(u.dtype)

    q = heads(jax.nn.silu(q))
    iv = heads(iv)
    logf_fw, k_fw = gate(f_fw)
    logf_bw, k_bw = gate(f_bw)
    o = gla_chunked(q, k_fw, iv, logf_fw) + flip(gla_chunked(flip(q), flip(k_bw), flip(iv), flip(logf_bw)))
    o = rmsnorm(o, norm_w) * jax.nn.silu(heads(g))
    return o.reshape(bsz, seq, HG_DIM) @ w_out


def ssd_chunked(x, dt, a, bm, cm):
    bsz, seq = x.shape[:2]
    nc = seq // SSD_CHUNK
    c = SSD_CHUNK
    xd = (x * dt[..., None]).reshape(bsz, nc, c, SSM_GROUPS, SSM_HPG, SSM_HEAD_DIM)
    la = (dt * a).reshape(bsz, nc, c, SSM_GROUPS, SSM_HPG).transpose(0, 3, 4, 1, 2)
    a_cum = jnp.cumsum(la, axis=-1)
    bc = bm.reshape(bsz, nc, c, SSM_GROUPS, SSM_STATE)
    cc = cm.reshape(bsz, nc, c, SSM_GROUPS, SSM_STATE)
    lower = jnp.tril(jnp.ones((c, c), dtype=bool))
    lmat = jnp.exp(jnp.where(lower, a_cum[..., :, None] - a_cum[..., None, :], -jnp.inf))
    cb = jnp.einsum('bclgn,bcsgn->bgcls', cc, bc)
    y_diag = jnp.einsum('bgjcls,bcsgjp->bclgjp', cb[:, :, None] * lmat, xd)
    decay_s = jnp.exp(a_cum[..., -1:] - a_cum).transpose(0, 3, 4, 1, 2)[..., None]
    states = jnp.einsum('bcsgn,bcsgjp->bcgjpn', bc, xd * decay_s)
    a_last = a_cum[..., -1]
    a_cs = jnp.cumsum(a_last, axis=-1)
    a_excl = a_cs - a_last
    before = jnp.tril(jnp.ones((nc, nc), dtype=bool), k=-1)
    w = jnp.exp(jnp.where(before, a_excl[..., :, None] - a_cs[..., None, :], -jnp.inf))
    h_in = jnp.einsum('bgjzc,bcgjpn->bzgjpn', w, states)
    y_off = (jnp.einsum('bzlgn,bzgjpn->bzlgjp', cc, h_in)
             * jnp.exp(a_cum).transpose(0, 3, 4, 1, 2)[..., None])
    return (y_diag + y_off).reshape(bsz, seq, SSM_GROUPS, SSM_HPG, SSM_HEAD_DIM)


def mamba2_mixer(u, w_in, conv_w, conv_b, dt_bias, a_log, d_skip, norm_w, w_out):
    bsz, seq, _ = u.shape
    z, xbc, dt_raw = jnp.split(u @ w_in, [D_INNER, D_INNER + CONV_DIM], axis=-1)
    xbc = jax.nn.silu(dwconv_centred(xbc, conv_w, conv_b))
    xs, bm, cm = jnp.split(xbc, [D_INNER, D_INNER + GN], axis=-1)
    xs = xs.reshape(bsz, seq, SSM_GROUPS, SSM_HPG, SSM_HEAD_DIM)
    bm = bm.reshape(bsz, seq, SSM_GROUPS, SSM_STATE)
    cm = cm.reshape(bsz, seq, SSM_GROUPS, SSM_STATE)
    dt = jax.nn.softplus(dt_raw.reshape(bsz, seq, 2, SSM_HEADS) + dt_bias)
    a = -jnp.exp(a_log).reshape(2, SSM_GROUPS, SSM_HPG)
    dt_fw = dt[:, :, 0].reshape(bsz, seq, SSM_GROUPS, SSM_HPG)
    dt_bw = dt[:, :, 1].reshape(bsz, seq, SSM_GROUPS, SSM_HPG)
    y = (ssd_chunked(xs, dt_fw, a[0], bm, cm)
         + flip(ssd_chunked(flip(xs), flip(dt_bw), a[1], flip(bm), flip(cm)))
         + xs * d_skip.reshape(SSM_GROUPS, SSM_HPG, 1))
    y = y.reshape(bsz, seq, D_INNER) * jax.nn.silu(z)
    y = group_rmsnorm(y, norm_w, SSM_GROUPS)
    return y @ w_out


def conv_glu(u, w_in, conv_w, conv_b, w_out):
    gate, val = jnp.split(u @ w_in, 2, axis=-1)
    return (jax.nn.silu(dwconv_centred(gate, conv_w, conv_b)) * val) @ w_out


def _fwd_setup_inputs(seed: int = 0) -> dict:
    key = jax.random.key(seed)
    ks = jax.random.split(key, 20)

    def nrm(k, shape, scale):
        return scale * jax.random.normal(k, shape, jnp.float32)

    dt = jnp.exp(jax.random.uniform(ks[10], (N_B, 2, SSM_HEADS), jnp.float32)
                 * (math.log(0.1) - math.log(1e-3)) + math.log(1e-3))
    return {
        "x": nrm(ks[0], (BATCH, SEQ, D_MODEL), 1.0),
        "norm1_w": 1.0 + nrm(ks[1], (DEPTH, D_MODEL), 0.02),
        "norm2_w": 1.0 + nrm(ks[2], (DEPTH, D_MODEL), 0.02),
        "a_w_in": nrm(ks[3], (N_A, D_MODEL, 5 * HG_DIM), D_MODEL ** -0.5),
        "a_lb_logits": nrm(ks[4], (DEPTH + 1, HG_DIM), 0.1),
        "a_norm_w": 1.0 + nrm(ks[5], (N_A, HG_HEAD_DIM), 0.02),
        "a_w_out": nrm(ks[6], (N_A, HG_DIM, D_MODEL), HG_DIM ** -0.5),
        "b_w_in": nrm(ks[7], (N_B, D_MODEL, B_PROJ), D_MODEL ** -0.5),
        "b_conv_w": nrm(ks[8], (N_B, SSM_CONV, CONV_DIM), SSM_CONV ** -0.5),
        "b_conv_b": nrm(ks[9], (N_B, CONV_DIM), 0.02),
        "b_dt_bias": dt + jnp.log(-jnp.expm1(-dt)),
        "b_a_log": jnp.log(jax.random.uniform(ks[11], (N_B, 2, SSM_HEADS), jnp.float32, 1.0, 16.0)),
        "b_d_skip": 1.0 + nrm(ks[12], (N_B, SSM_HEADS), 0.02),
        "b_norm_w": 1.0 + nrm(ks[13], (N_B, D_INNER), 0.02),
        "b_w_out": nrm(ks[14], (N_B, D_INNER, D_MODEL), D_INNER ** -0.5),
        "ffn_w_in": nrm(ks[15], (DEPTH, D_MODEL, 2 * D_FF), D_MODEL ** -0.5),
        "ffn_conv_w": nrm(ks[16], (DEPTH, FFN_CONV, D_FF), FFN_CONV ** -0.5),
        "ffn_conv_b": nrm(ks[17], (DEPTH, D_FF), 0.02),
        "ffn_w_out": nrm(ks[18], (DEPTH, D_FF, D_MODEL), D_FF ** -0.5),
        "final_norm_w": 1.0 + nrm(ks[19], (D_MODEL,), 0.02),
    }


def _fwd_reference(x, norm1_w, norm2_w, a_w_in, a_lb_logits, a_norm_w, a_w_out, b_w_in, b_conv_w, b_conv_b,
              b_dt_bias, b_a_log, b_d_skip, b_norm_w, b_w_out, ffn_w_in, ffn_conv_w, ffn_conv_b, ffn_w_out,
              final_norm_w):
    lower_bounds = jnp.cumsum(jax.nn.softmax(a_lb_logits.astype(jnp.float32), axis=0), axis=0)
    h = x
    for i in range(DEPTH):
        u = rmsnorm(h, norm1_w[i])
        j = i // N_MIXERS
        if i % N_MIXERS == 0:
            h = h + hgrn2_mixer(u, a_w_in[j], lower_bounds[i], a_norm_w[j], a_w_out[j])
        else:
            h = h + mamba2_mixer(u, b_w_in[j], b_conv_w[j], b_conv_b[j], b_dt_bias[j], b_a_log[j],
                                 b_d_skip[j], b_norm_w[j], b_w_out[j])
        h = h + conv_glu(rmsnorm(h, norm2_w[i]), ffn_w_in[i], ffn_conv_w[i], ffn_conv_b[i], ffn_w_out[i])
    return rmsnorm(h, final_norm_w)


import jax as _jax
import jax.numpy as _jnp

TWIN_FORMAT = 'train_step'
FWD_PARAMS = ['x', 'norm1_w', 'norm2_w', 'a_w_in', 'a_lb_logits', 'a_norm_w', 'a_w_out', 'b_w_in', 'b_conv_w', 'b_conv_b', 'b_dt_bias', 'b_a_log', 'b_d_skip', 'b_norm_w', 'b_w_out', 'ffn_w_in', 'ffn_conv_w', 'ffn_conv_b', 'ffn_w_out', 'final_norm_w']
TWIN_WEIGHTS = ['norm1_w', 'norm2_w', 'a_w_in', 'a_lb_logits', 'a_norm_w', 'a_w_out', 'b_w_in', 'b_conv_w', 'b_conv_b', 'b_dt_bias', 'b_a_log', 'b_d_skip', 'b_norm_w', 'b_w_out', 'ffn_w_in', 'ffn_conv_w', 'ffn_conv_b', 'ffn_w_out', 'final_norm_w']
TWIN_DIFF_INPUT = 'x'
TWIN_INPUTS = ['x', 'norm1_w', 'norm2_w', 'a_w_in', 'a_lb_logits', 'a_norm_w', 'a_w_out', 'b_w_in', 'b_conv_w', 'b_conv_b', 'b_dt_bias', 'b_a_log', 'b_d_skip', 'b_norm_w', 'b_w_out', 'ffn_w_in', 'ffn_conv_w', 'ffn_conv_b', 'ffn_w_out', 'final_norm_w', 'loss_target', 'm_norm1_w', 'm_norm2_w', 'm_a_w_in', 'm_a_lb_logits', 'm_a_norm_w', 'm_a_w_out', 'm_b_w_in', 'm_b_conv_w', 'm_b_conv_b', 'm_b_dt_bias', 'm_b_a_log', 'm_b_d_skip', 'm_b_norm_w', 'm_b_w_out', 'm_ffn_w_in', 'm_ffn_conv_w', 'm_ffn_conv_b', 'm_ffn_w_out', 'm_final_norm_w', 'v_norm1_w', 'v_norm2_w', 'v_a_w_in', 'v_a_lb_logits', 'v_a_norm_w', 'v_a_w_out', 'v_b_w_in', 'v_b_conv_w', 'v_b_conv_b', 'v_b_dt_bias', 'v_b_a_log', 'v_b_d_skip', 'v_b_norm_w', 'v_b_w_out', 'v_ffn_w_in', 'v_ffn_conv_w', 'v_ffn_conv_b', 'v_ffn_w_out', 'v_final_norm_w']
TWIN_OUTPUTS = ['loss', 'grad_x', 'grad_norm1_w', 'grad_norm2_w', 'grad_a_w_in', 'grad_a_lb_logits', 'grad_a_norm_w', 'grad_a_w_out', 'grad_b_w_in', 'grad_b_conv_w', 'grad_b_conv_b', 'grad_b_dt_bias', 'grad_b_a_log', 'grad_b_d_skip', 'grad_b_norm_w', 'grad_b_w_out', 'grad_ffn_w_in', 'grad_ffn_conv_w', 'grad_ffn_conv_b', 'grad_ffn_w_out', 'grad_final_norm_w', 'delta_norm1_w', 'delta_norm2_w', 'delta_a_w_in', 'delta_a_lb_logits', 'delta_a_norm_w', 'delta_a_w_out', 'delta_b_w_in', 'delta_b_conv_w', 'delta_b_conv_b', 'delta_b_dt_bias', 'delta_b_a_log', 'delta_b_d_skip', 'delta_b_norm_w', 'delta_b_w_out', 'delta_ffn_w_in', 'delta_ffn_conv_w', 'delta_ffn_conv_b', 'delta_ffn_w_out', 'delta_final_norm_w', 'new_m_norm1_w', 'new_m_norm2_w', 'new_m_a_w_in', 'new_m_a_lb_logits', 'new_m_a_norm_w', 'new_m_a_w_out', 'new_m_b_w_in', 'new_m_b_conv_w', 'new_m_b_conv_b', 'new_m_b_dt_bias', 'new_m_b_a_log', 'new_m_b_d_skip', 'new_m_b_norm_w', 'new_m_b_w_out', 'new_m_ffn_w_in', 'new_m_ffn_conv_w', 'new_m_ffn_conv_b', 'new_m_ffn_w_out', 'new_m_final_norm_w', 'new_v_norm1_w', 'new_v_norm2_w', 'new_v_a_w_in', 'new_v_a_lb_logits', 'new_v_a_norm_w', 'new_v_a_w_out', 'new_v_b_w_in', 'new_v_b_conv_w', 'new_v_b_conv_b', 'new_v_b_dt_bias', 'new_v_b_a_log', 'new_v_b_d_skip', 'new_v_b_norm_w', 'new_v_b_w_out', 'new_v_ffn_w_in', 'new_v_ffn_conv_w', 'new_v_ffn_conv_b', 'new_v_ffn_w_out', 'new_v_final_norm_w']
TWIN_LEAF_KINDS = {'loss': 'loss', 'grad_x': 'grad_x', 'grad_norm1_w': 'grad_w', 'grad_norm2_w': 'grad_w', 'grad_a_w_in': 'grad_w', 'grad_a_lb_logits': 'grad_w', 'grad_a_norm_w': 'grad_w', 'grad_a_w_out': 'grad_w', 'grad_b_w_in': 'grad_w', 'grad_b_conv_w': 'grad_w', 'grad_b_conv_b': 'grad_w', 'grad_b_dt_bias': 'grad_w', 'grad_b_a_log': 'grad_w', 'grad_b_d_skip': 'grad_w', 'grad_b_norm_w': 'grad_w', 'grad_b_w_out': 'grad_w', 'grad_ffn_w_in': 'grad_w', 'grad_ffn_conv_w': 'grad_w', 'grad_ffn_conv_b': 'grad_w', 'grad_ffn_w_out': 'grad_w', 'grad_final_norm_w': 'grad_w', 'delta_norm1_w': 'delta_w', 'delta_norm2_w': 'delta_w', 'delta_a_w_in': 'delta_w', 'delta_a_lb_logits': 'delta_w', 'delta_a_norm_w': 'delta_w', 'delta_a_w_out': 'delta_w', 'delta_b_w_in': 'delta_w', 'delta_b_conv_w': 'delta_w', 'delta_b_conv_b': 'delta_w', 'delta_b_dt_bias': 'delta_w', 'delta_b_a_log': 'delta_w', 'delta_b_d_skip': 'delta_w', 'delta_b_norm_w': 'delta_w', 'delta_b_w_out': 'delta_w', 'delta_ffn_w_in': 'delta_w', 'delta_ffn_conv_w': 'delta_w', 'delta_ffn_conv_b': 'delta_w', 'delta_ffn_w_out': 'delta_w', 'delta_final_norm_w': 'delta_w', 'new_m_norm1_w': 'new_m', 'new_m_norm2_w': 'new_m', 'new_m_a_w_in': 'new_m', 'new_m_a_lb_logits': 'new_m', 'new_m_a_norm_w': 'new_m', 'new_m_a_w_out': 'new_m', 'new_m_b_w_in': 'new_m', 'new_m_b_conv_w': 'new_m', 'new_m_b_conv_b': 'new_m', 'new_m_b_dt_bias': 'new_m', 'new_m_b_a_log': 'new_m', 'new_m_b_d_skip': 'new_m', 'new_m_b_norm_w': 'new_m', 'new_m_b_w_out': 'new_m', 'new_m_ffn_w_in': 'new_m', 'new_m_ffn_conv_w': 'new_m', 'new_m_ffn_conv_b': 'new_m', 'new_m_ffn_w_out': 'new_m', 'new_m_final_norm_w': 'new_m', 'new_v_norm1_w': 'new_v', 'new_v_norm2_w': 'new_v', 'new_v_a_w_in': 'new_v', 'new_v_a_lb_logits': 'new_v', 'new_v_a_norm_w': 'new_v', 'new_v_a_w_out': 'new_v', 'new_v_b_w_in': 'new_v', 'new_v_b_conv_w': 'new_v', 'new_v_b_conv_b': 'new_v', 'new_v_b_dt_bias': 'new_v', 'new_v_b_a_log': 'new_v', 'new_v_b_d_skip': 'new_v', 'new_v_b_norm_w': 'new_v', 'new_v_b_w_out': 'new_v', 'new_v_ffn_w_in': 'new_v', 'new_v_ffn_conv_w': 'new_v', 'new_v_ffn_conv_b': 'new_v', 'new_v_ffn_w_out': 'new_v', 'new_v_final_norm_w': 'new_v'}


def _forward(args):
    return _fwd_reference(*[args[k] for k in FWD_PARAMS])


def _output_shape():
    out = _jax.eval_shape(lambda: _forward(_fwd_setup_inputs(0)))
    return out.shape, out.dtype

N_MICROBATCH = 1
ADAM_LR = 0.001
ADAM_B1 = 0.9
ADAM_B2 = 0.999
ADAM_EPS = 1e-08
ADAM_WD = 0.01
ADAM_STEP = 10
PER_EXAMPLE_BATCH_AXIS = {'x': 0, 'loss_target': 0}
SHARED_INPUTS = []
_WEIGHT_DTYPES = {'norm1_w': _jnp.float32, 'norm2_w': _jnp.float32, 'a_w_in': _jnp.float32, 'a_lb_logits': _jnp.float32, 'a_norm_w': _jnp.float32, 'a_w_out': _jnp.float32, 'b_w_in': _jnp.float32, 'b_conv_w': _jnp.float32, 'b_conv_b': _jnp.float32, 'b_dt_bias': _jnp.float32, 'b_a_log': _jnp.float32, 'b_d_skip': _jnp.float32, 'b_norm_w': _jnp.float32, 'b_w_out': _jnp.float32, 'ffn_w_in': _jnp.float32, 'ffn_conv_w': _jnp.float32, 'ffn_conv_b': _jnp.float32, 'ffn_w_out': _jnp.float32, 'final_norm_w': _jnp.float32}
MOMENT_SCALE = {'norm1_w': 1.777250e-01, 'norm2_w': 1.252842e-01, 'a_w_in': 8.356249e-02, 'a_lb_logits': 5.201023e-03, 'a_norm_w': 3.739153e-01, 'a_w_out': 1.290856e-01, 'b_w_in': 6.918134e-02, 'b_conv_w': 5.965678e-02, 'b_conv_b': 1.008657e-01, 'b_dt_bias': 1.340237e-01, 'b_a_log': 1.810957e-01, 'b_d_skip': 3.033220e-01, 'b_norm_w': 8.312851e-02, 'b_w_out': 1.141220e-01, 'ffn_w_in': 5.295477e-02, 'ffn_conv_w': 5.443578e-02, 'ffn_conv_b': 5.066098e-02, 'ffn_w_out': 8.615333e-02, 'final_norm_w': 3.197507e+01}


def _to_microbatches(a, axis):
    t = _jnp.moveaxis(a, axis, 0)
    t = t.reshape((N_MICROBATCH, t.shape[0] // N_MICROBATCH) + t.shape[1:])
    return _jnp.moveaxis(t, 1, axis + 1)


def setup_inputs(seed: int = 0) -> dict:
    inp = _fwd_setup_inputs(seed)
    key = _jax.random.fold_in(_jax.random.key(seed), 7919)
    shape, _ = _output_shape()
    out = dict(inp)
    out["loss_target"] = _jax.random.normal(_jax.random.fold_in(key, 0), shape, _jnp.float32)
    for i, name in enumerate(TWIN_WEIGHTS):
        w = inp[name].astype(_jnp.float32)
        if MOMENT_SCALE is None:
            s = _jnp.sqrt(_jnp.mean(_jnp.square(w)) + 1e-30)
        else:
            s = MOMENT_SCALE[name]
        km, kv = _jax.random.split(_jax.random.fold_in(key, i + 1))
        out[name] = w
        out["m_" + name] = s * _jax.random.normal(km, w.shape, _jnp.float32)
        out["v_" + name] = (s * s) * _jax.random.uniform(kv, w.shape, _jnp.float32, 0.5, 1.5)
    if N_MICROBATCH > 1:
        for name, axis in PER_EXAMPLE_BATCH_AXIS.items():
            out[name] = _to_microbatches(out[name], axis)
    return {'x': out['x'], 'norm1_w': out['norm1_w'], 'norm2_w': out['norm2_w'], 'a_w_in': out['a_w_in'], 'a_lb_logits': out['a_lb_logits'], 'a_norm_w': out['a_norm_w'], 'a_w_out': out['a_w_out'], 'b_w_in': out['b_w_in'], 'b_conv_w': out['b_conv_w'], 'b_conv_b': out['b_conv_b'], 'b_dt_bias': out['b_dt_bias'], 'b_a_log': out['b_a_log'], 'b_d_skip': out['b_d_skip'], 'b_norm_w': out['b_norm_w'], 'b_w_out': out['b_w_out'], 'ffn_w_in': out['ffn_w_in'], 'ffn_conv_w': out['ffn_conv_w'], 'ffn_conv_b': out['ffn_conv_b'], 'ffn_w_out': out['ffn_w_out'], 'final_norm_w': out['final_norm_w'], 'loss_target': out['loss_target'], 'm_norm1_w': out['m_norm1_w'], 'm_norm2_w': out['m_norm2_w'], 'm_a_w_in': out['m_a_w_in'], 'm_a_lb_logits': out['m_a_lb_logits'], 'm_a_norm_w': out['m_a_norm_w'], 'm_a_w_out': out['m_a_w_out'], 'm_b_w_in': out['m_b_w_in'], 'm_b_conv_w': out['m_b_conv_w'], 'm_b_conv_b': out['m_b_conv_b'], 'm_b_dt_bias': out['m_b_dt_bias'], 'm_b_a_log': out['m_b_a_log'], 'm_b_d_skip': out['m_b_d_skip'], 'm_b_norm_w': out['m_b_norm_w'], 'm_b_w_out': out['m_b_w_out'], 'm_ffn_w_in': out['m_ffn_w_in'], 'm_ffn_conv_w': out['m_ffn_conv_w'], 'm_ffn_conv_b': out['m_ffn_conv_b'], 'm_ffn_w_out': out['m_ffn_w_out'], 'm_final_norm_w': out['m_final_norm_w'], 'v_norm1_w': out['v_norm1_w'], 'v_norm2_w': out['v_norm2_w'], 'v_a_w_in': out['v_a_w_in'], 'v_a_lb_logits': out['v_a_lb_logits'], 'v_a_norm_w': out['v_a_norm_w'], 'v_a_w_out': out['v_a_w_out'], 'v_b_w_in': out['v_b_w_in'], 'v_b_conv_w': out['v_b_conv_w'], 'v_b_conv_b': out['v_b_conv_b'], 'v_b_dt_bias': out['v_b_dt_bias'], 'v_b_a_log': out['v_b_a_log'], 'v_b_d_skip': out['v_b_d_skip'], 'v_b_norm_w': out['v_b_norm_w'], 'v_b_w_out': out['v_b_w_out'], 'v_ffn_w_in': out['v_ffn_w_in'], 'v_ffn_conv_w': out['v_ffn_conv_w'], 'v_ffn_conv_b': out['v_ffn_conv_b'], 'v_ffn_w_out': out['v_ffn_w_out'], 'v_final_norm_w': out['v_final_norm_w']}


def _loss(weights, diff, rest, loss_target):
    with _jax.named_scope("forward"):
        args = {**rest, TWIN_DIFF_INPUT: diff, **{k: w.astype(_WEIGHT_DTYPES[k]) for k, w in weights.items()}}
        y = _forward(args)
    with _jax.named_scope("loss_head"):
        err = _jnp.square(y.astype(_jnp.float32) - loss_target)
        return 0.5 * _jnp.sum(_jnp.mean(err, axis=-1)) if err.ndim else 0.5 * err


def _adamw(w, g, m, v):
    m = ADAM_B1 * m + (1.0 - ADAM_B1) * g
    v = ADAM_B2 * v + (1.0 - ADAM_B2) * _jnp.square(g)
    m_hat = m / (1.0 - ADAM_B1 ** ADAM_STEP)
    v_hat = v / (1.0 - ADAM_B2 ** ADAM_STEP)
    delta = -ADAM_LR * (m_hat / (_jnp.sqrt(v_hat) + ADAM_EPS) + ADAM_WD * w)
    return delta, m, v


def reference(x, norm1_w, norm2_w, a_w_in, a_lb_logits, a_norm_w, a_w_out, b_w_in, b_conv_w, b_conv_b, b_dt_bias, b_a_log, b_d_skip, b_norm_w, b_w_out, ffn_w_in, ffn_conv_w, ffn_conv_b, ffn_w_out, final_norm_w, loss_target, m_norm1_w, m_norm2_w, m_a_w_in, m_a_lb_logits, m_a_norm_w, m_a_w_out, m_b_w_in, m_b_conv_w, m_b_conv_b, m_b_dt_bias, m_b_a_log, m_b_d_skip, m_b_norm_w, m_b_w_out, m_ffn_w_in, m_ffn_conv_w, m_ffn_conv_b, m_ffn_w_out, m_final_norm_w, v_norm1_w, v_norm2_w, v_a_w_in, v_a_lb_logits, v_a_norm_w, v_a_w_out, v_b_w_in, v_b_conv_w, v_b_conv_b, v_b_dt_bias, v_b_a_log, v_b_d_skip, v_b_norm_w, v_b_w_out, v_ffn_w_in, v_ffn_conv_w, v_ffn_conv_b, v_ffn_w_out, v_final_norm_w):
    given = dict(x=x, norm1_w=norm1_w, norm2_w=norm2_w, a_w_in=a_w_in, a_lb_logits=a_lb_logits, a_norm_w=a_norm_w, a_w_out=a_w_out, b_w_in=b_w_in, b_conv_w=b_conv_w, b_conv_b=b_conv_b, b_dt_bias=b_dt_bias, b_a_log=b_a_log, b_d_skip=b_d_skip, b_norm_w=b_norm_w, b_w_out=b_w_out, ffn_w_in=ffn_w_in, ffn_conv_w=ffn_conv_w, ffn_conv_b=ffn_conv_b, ffn_w_out=ffn_w_out, final_norm_w=final_norm_w, loss_target=loss_target, m_norm1_w=m_norm1_w, m_norm2_w=m_norm2_w, m_a_w_in=m_a_w_in, m_a_lb_logits=m_a_lb_logits, m_a_norm_w=m_a_norm_w, m_a_w_out=m_a_w_out, m_b_w_in=m_b_w_in, m_b_conv_w=m_b_conv_w, m_b_conv_b=m_b_conv_b, m_b_dt_bias=m_b_dt_bias, m_b_a_log=m_b_a_log, m_b_d_skip=m_b_d_skip, m_b_norm_w=m_b_norm_w, m_b_w_out=m_b_w_out, m_ffn_w_in=m_ffn_w_in, m_ffn_conv_w=m_ffn_conv_w, m_ffn_conv_b=m_ffn_conv_b, m_ffn_w_out=m_ffn_w_out, m_final_norm_w=m_final_norm_w, v_norm1_w=v_norm1_w, v_norm2_w=v_norm2_w, v_a_w_in=v_a_w_in, v_a_lb_logits=v_a_lb_logits, v_a_norm_w=v_a_norm_w, v_a_w_out=v_a_w_out, v_b_w_in=v_b_w_in, v_b_conv_w=v_b_conv_w, v_b_conv_b=v_b_conv_b, v_b_dt_bias=v_b_dt_bias, v_b_a_log=v_b_a_log, v_b_d_skip=v_b_d_skip, v_b_norm_w=v_b_norm_w, v_b_w_out=v_b_w_out, v_ffn_w_in=v_ffn_w_in, v_ffn_conv_w=v_ffn_conv_w, v_ffn_conv_b=v_ffn_conv_b, v_ffn_w_out=v_ffn_w_out, v_final_norm_w=v_final_norm_w)
    weights = {n: given[n] for n in TWIN_WEIGHTS}
    shared = {n: given[n] for n in SHARED_INPUTS}
    per_example = {n: given[n] for n in ['x']}
    grad_fn = _jax.value_and_grad(_loss, argnums=(0, 1))

    def one_microbatch(ex, loss_target):
        ex = dict(ex)
        diff = ex.pop(TWIN_DIFF_INPUT)
        return grad_fn(weights, diff, {**shared, **ex}, loss_target)

    if N_MICROBATCH == 1:
        loss, (grad_w, grad_x) = one_microbatch(per_example, given["loss_target"])
    else:
        def body(carry, xs):
            loss_sum, grad_sum = carry
            l_k, (gw_k, gx_k) = one_microbatch(xs[0], xs[1])
            with _jax.named_scope("update"):
                return (loss_sum + l_k, _jax.tree.map(_jnp.add, grad_sum, gw_k)), gx_k

        init = (_jnp.zeros((), _jnp.float32), _jax.tree.map(_jnp.zeros_like, weights))
        (loss, grad_w), grad_x = _jax.lax.scan(body, init, (per_example, given["loss_target"]))
    with _jax.named_scope("update"):
        delta_w, new_m, new_v = {}, {}, {}
        for n in TWIN_WEIGHTS:
            delta_w[n], new_m[n], new_v[n] = _adamw(weights[n], grad_w[n], given["m_" + n], given["v_" + n])
    return (loss, grad_x, *[grad_w[n] for n in TWIN_WEIGHTS], *[delta_w[n] for n in TWIN_WEIGHTS],
            *[new_m[n] for n in TWIN_WEIGHTS], *[new_v[n] for n in TWIN_WEIGHTS])
```

```python
import functools

import jax
import jax.numpy as jnp
from jax import lax
from jax.experimental import pallas as pl
from jax.experimental.pallas import tpu as pltpu

F32 = jnp.float32
BF16 = jnp.bfloat16
HIGHEST = lax.Precision.HIGHEST
MESH = pl.DeviceIdType.MESH

N_DEV = 8
EPS = 1e-6
D_MODEL = 1024
HG_HEADS = 8
HG_HEAD_DIM = 128
D_INNER = 2048
SSM_HEAD_DIM = 64
SSM_GROUPS = 8
SSM_HPG = 4
SSM_STATE = 128
GN = SSM_GROUPS * SSM_STATE
CONV_DIM = D_INNER + 2 * GN
D_FF = 2816
SSM_CONV = 5
FFN_CONV = 3

ADAM_LR = 0.001
ADAM_B1 = 0.9
ADAM_B2 = 0.999
ADAM_EPS = 1e-08
ADAM_WD = 0.01
ADAM_STEP = 10

LANES = 128
ROW_TILE = 256
COL_TILE = 128
GLA_CHUNK = 16
GLA_BLOCK = 512
SSD_CHUNK = 128
SSD_BLOCK = 512
PACK_W = 1024
VMEM_LIMIT = 56 * 1024 * 1024

NT_DIMS = (((1,), (1,)), ((), ()))
TN_DIMS = (((0,), (0,)), ((), ()))


def _cparams(*sem):
    return pltpu.CompilerParams(dimension_semantics=sem, vmem_limit_bytes=VMEM_LIMIT)


def _rms(x, w):
    return x * lax.rsqrt(jnp.mean(x * x, axis=-1, keepdims=True) + EPS) * w


def _row_kernel(body_fn, rows, params, row_outs, acc_outs, *, name, tile=ROW_TILE):
    L = rows[0][0].shape[0]
    tile = min(tile, L)
    n_in = len(rows) + len(params)
    n_ro = len(row_outs)

    def body(*refs):
        outs = body_fn(*[r[...] for r in refs[:n_in]])
        for ref, o in zip(refs[n_in:n_in + n_ro], outs[:n_ro]):
            ref[...] = o.astype(ref.dtype)
        first = pl.program_id(0) == 0
        for ref, o in zip(refs[n_in + n_ro:], outs[n_ro:]):
            @pl.when(first)
            def _(ref=ref):
                ref[...] = jnp.zeros(ref.shape, ref.dtype)
            ref[...] += o

    in_specs = [pl.BlockSpec((tile, w), lambda i, cb=cb: (i, cb)) for _, w, cb in rows]
    in_specs += [pl.BlockSpec(p.shape, lambda i: (0, 0)) for p in params]
    out_specs = [pl.BlockSpec((tile, w), lambda i: (i, 0)) for w, _ in row_outs]
    out_specs += [pl.BlockSpec(s, lambda i: (0, 0)) for s in acc_outs]
    out_shape = [jax.ShapeDtypeStruct((L, w), dt) for w, dt in row_outs]
    out_shape += [jax.ShapeDtypeStruct(s, F32) for s in acc_outs]
    return pl.pallas_call(
        body, name=name, grid=(L // tile,), in_specs=in_specs, out_specs=out_specs, out_shape=out_shape,
        compiler_params=_cparams("arbitrary" if acc_outs else "parallel"),
    )(*[a for a, _, _ in rows], *params)


def _col_kernel(body_fn, cols, params, col_outs, par_outs, *, name, n_tiles):
    L = cols[0][0].shape[0]
    n_in = len(cols) + len(params)
    width = n_tiles * COL_TILE

    def body(*refs):
        outs = body_fn(*[r[...] for r in refs[:n_in]])
        for ref, o in zip(refs[n_in:], outs):
            ref[...] = o.astype(ref.dtype)

    in_specs = [pl.BlockSpec((L, COL_TILE), lambda j, cb=cb: (0, cb + j)) for _, cb in cols]
    in_specs += [pl.BlockSpec((p.shape[0], COL_TILE), lambda j, cb=cb: (0, cb + j)) for p, cb in params]
    out_specs = [pl.BlockSpec((L, COL_TILE), lambda j: (0, j)) for _ in col_outs]
    out_specs += [pl.BlockSpec((k, COL_TILE), lambda j: (0, j)) for k in par_outs]
    out_shape = [jax.ShapeDtypeStruct((L, width), dt) for dt in col_outs]
    out_shape += [jax.ShapeDtypeStruct((k, width), F32) for k in par_outs]
    return pl.pallas_call(
        body, name=name, grid=(n_tiles,), in_specs=in_specs, out_specs=out_specs, out_shape=out_shape,
        compiler_params=_cparams("parallel"),
    )(*[a for a, _ in cols], *[p for p, _ in params])


def _pick(n, options):
    for t in options:
        if n % t == 0:
            return t
    return n


def _matmul(a, b, *, name, nt=False, add=None, out_dtype=F32):
    M, K = a.shape
    N = b.shape[0] if nt else b.shape[1]
    tm = _pick(M, (512, 256, 128))
    tn = _pick(N, (512, 256, 128))

    def body(*refs):
        a_ref, b_ref = refs[0], refs[1]
        o_ref = refs[-1]
        if nt:
            acc = lax.dot_general(a_ref[...], b_ref[...], NT_DIMS, preferred_element_type=F32)
        else:
            acc = jnp.dot(a_ref[...], b_ref[...], preferred_element_type=F32)
        if add is not None:
            acc = acc + refs[2][...]
        o_ref[...] = acc.astype(o_ref.dtype)

    in_specs = [pl.BlockSpec((tm, K), lambda i, j: (i, 0)),
                pl.BlockSpec((tn, K), lambda i, j: (j, 0)) if nt else pl.BlockSpec((K, tn), lambda i, j: (0, j))]
    args = [a, b]
    if add is not None:
        in_specs.append(pl.BlockSpec((tm, tn), lambda i, j: (i, j)))
        args.append(add)
    return pl.pallas_call(
        body, name=name, grid=(M // tm, N // tn), in_specs=in_specs,
        out_specs=pl.BlockSpec((tm, tn), lambda i, j: (i, j)),
        out_shape=jax.ShapeDtypeStruct((M, N), out_dtype),
        compiler_params=_cparams("parallel", "parallel"),
    )(*args)


def _hgrn2_pre_fn(q_raw, ffw_raw, fbw_raw, lb_logits):
    lb = jax.nn.softmax(lb_logits, axis=0)[0:1]

    def gate(fr):
        f = lb + (1.0 - lb) * jax.nn.sigmoid(fr)
        return jnp.log(f), 1.0 - f

    lf_fw, k_fw = gate(ffw_raw)
    lf_bw, k_bw = gate(fbw_raw)
    return jax.nn.silu(q_raw), lf_fw, k_fw, lf_bw, k_bw


def _hgrn2_post_fn(o, g, norm_w):
    outs = []
    for h in range(HG_HEADS):
        sl = slice(h * HG_HEAD_DIM, (h + 1) * HG_HEAD_DIM)
        outs.append(_rms(o[:, sl], norm_w) * jax.nn.silu(g[:, sl]))
    return jnp.concatenate(outs, axis=1)


def _mamba_post_fn(y, z, norm_w):
    outs = []
    gw = D_INNER // SSM_GROUPS
    for gi in range(SSM_GROUPS):
        sl = slice(gi * gw, (gi + 1) * gw)
        outs.append(_rms(y[:, sl] * jax.nn.silu(z[:, sl]), norm_w[:, sl]))
    return jnp.concatenate(outs, axis=1)


def _shift_rows_impl(x, d):
    if d == 0:
        return x
    n = x.shape[0]
    t = lax.broadcasted_iota(jnp.int32, x.shape, 0)
    rolled = pltpu.roll(x, d % n, 0)
    return jnp.where((t >= d) if d > 0 else (t < n + d), rolled, 0.0)


@functools.partial(jax.custom_vjp, nondiff_argnums=(1,))
def _shift_rows(x, d):
    return _shift_rows_impl(x, d)


_shift_rows.defvjp(lambda x, d: (_shift_rows_impl(x, d), None), lambda d, _, g: (_shift_rows_impl(g, -d),))


def _dwconv(x, w, b):
    taps = w.shape[0]
    c = (taps - 1) // 2
    y = b + w[0:1, :] * _shift_rows(x, c)
    for k in range(1, taps):
        y = y + w[k:k + 1, :] * _shift_rows(x, c - k)
    return y


def _ffn_mid_fn(gate, val, w, b):
    return jax.nn.silu(_dwconv(gate, w, b)) * val


def _mamba_conv_fn(xbc, w, b):
    return jax.nn.silu(_dwconv(xbc, w, b))


def _gla_consts(rev):
    c = GLA_CHUNK
    r2 = lax.broadcasted_iota(jnp.int32, (c, c), 0)
    c2 = lax.broadcasted_iota(jnp.int32, (c, c), 1)
    t3 = lax.broadcasted_iota(jnp.int32, (c, c, HG_HEAD_DIM), 0)
    s3 = lax.broadcasted_iota(jnp.int32, (c, c, HG_HEAD_DIM), 1)
    if rev:
        return (r2 <= c2).astype(F32), t3 <= s3
    return (r2 >= c2).astype(F32), t3 >= s3


def _gla_chunk(st, q, k, v, g, *, rev, consts):
    tri, mask3 = consts
    c = q.shape[0]
    b = jnp.dot(tri, g, precision=HIGHEST, preferred_element_type=F32)
    o = lax.dot_general((q * jnp.exp(b)).astype(BF16), st.astype(BF16), NT_DIMS, preferred_element_type=F32)
    e = jnp.exp(jnp.where(mask3, b[:, None, :] - b[None, :, :], -jnp.inf))
    att = jnp.sum(q[:, None, :] * k[None, :, :] * e, axis=-1)
    o = o + jnp.dot(att.astype(BF16), v.astype(BF16), preferred_element_type=F32)
    b_end = b[0:1] if rev else b[c - 1:c]
    kd = (k * jnp.exp(b_end - b)).astype(BF16)
    st_new = st * jnp.exp(b_end) + lax.dot_general(v.astype(BF16), kd, TN_DIMS, preferred_element_type=F32)
    return st_new, o


def _gla_fwd(q, k, v, v_cb, g, *, rev, add, name):
    L = q.shape[0]
    blk = min(GLA_BLOCK, L)
    nblk, nsub = L // blk, blk // GLA_CHUNK
    hd = HG_HEAD_DIM

    def body(*refs):
        q_ref, k_ref, v_ref, g_ref = refs[:4]
        add_ref = refs[4] if add is not None else None
        o_ref, st_out, st_scr = refs[-3:]
        consts = _gla_consts(rev)

        @pl.when(pl.program_id(1) == 0)
        def _():
            st_scr[...] = jnp.zeros(st_scr.shape, F32)

        st_out[...] = st_scr[...]

        def step(i, carry):
            sl = pl.ds(pl.multiple_of((nsub - 1 - i if rev else i) * GLA_CHUNK, GLA_CHUNK), GLA_CHUNK)
            st_new, o = _gla_chunk(st_scr[...], q_ref[sl, :], k_ref[sl, :], v_ref[sl, :], g_ref[sl, :],
                                   rev=rev, consts=consts)
            st_scr[...] = st_new
            if add_ref is not None:
                o = o + add_ref[sl, :]
            o_ref[sl, :] = o
            return carry

        lax.fori_loop(0, nsub, step, 0)

    def pos(j):
        return nblk - 1 - j if rev else j

    spec = pl.BlockSpec((blk, hd), lambda h, j: (pos(j), h))
    in_specs = [spec, spec, pl.BlockSpec((blk, hd), lambda h, j: (pos(j), v_cb + h)), spec]
    args = [q, k, v, g]
    if add is not None:
        in_specs.append(spec)
        args.append(add)
    return pl.pallas_call(
        body, name=name, grid=(HG_HEADS, nblk), in_specs=in_specs,
        out_specs=[spec, pl.BlockSpec((None, None, hd, hd), lambda h, j: (h, j, 0, 0))],
        out_shape=[jax.ShapeDtypeStruct((L, HG_HEADS * hd), F32),
                   jax.ShapeDtypeStruct((HG_HEADS, nblk, hd, hd), F32)],
        scratch_shapes=[pltpu.VMEM((hd, hd), F32)],
        compiler_params=_cparams("parallel", "arbitrary"),
    )(*args)


def _gla_bwd(q, k, v, v_cb, g, do, states, *, rev, adds, name):
    L = q.shape[0]
    blk = min(GLA_BLOCK, L)
    nblk, nsub = L // blk, blk // GLA_CHUNK
    hd = HG_HEAD_DIM
    n_add = 0 if adds is None else 2

    def body(*refs):
        q_ref, k_ref, v_ref, g_ref, do_ref, st_in = refs[:6]
        add_refs = refs[6:6 + n_add]
        dq_ref, dk_ref, dv_ref, dg_ref, dst_scr, sub_scr = refs[6 + n_add:]
        consts = _gla_consts(rev)
        chunk = functools.partial(_gla_chunk, rev=rev, consts=consts)

        @pl.when(pl.program_id(1) == 0)
        def _():
            dst_scr[...] = jnp.zeros(dst_scr.shape, F32)

        def rows(i):
            return pl.ds(pl.multiple_of((nsub - 1 - i if rev else i) * GLA_CHUNK, GLA_CHUNK), GLA_CHUNK)

        def replay(i, st):
            sub_scr[i] = st
            sl = rows(i)
            return chunk(st, q_ref[sl, :], k_ref[sl, :], v_ref[sl, :], g_ref[sl, :])[0]

        lax.fori_loop(0, nsub, replay, st_in[...])

        def step(n, carry):
            i = nsub - 1 - n
            sl = rows(i)
            _, vjp = jax.vjp(chunk, sub_scr[i], q_ref[sl, :], k_ref[sl, :], v_ref[sl, :], g_ref[sl, :])
            dst, dq, dk, dv, dg = vjp((dst_scr[...], do_ref[sl, :]))
            dst_scr[...] = dst
            if n_add:
                dq = dq + add_refs[0][sl, :]
                dv = dv + add_refs[1][sl, :]
            dq_ref[sl, :] = dq
            dk_ref[sl, :] = dk
            dv_ref[sl, :] = dv
            dg_ref[sl, :] = dg
            return carry

        lax.fori_loop(0, nsub, step, 0)

    def pos(j):
        p = nblk - 1 - j
        return nblk - 1 - p if rev else p

    spec = pl.BlockSpec((blk, hd), lambda h, j: (pos(j), h))
    in_specs = [spec, spec, pl.BlockSpec((blk, hd), lambda h, j: (pos(j), v_cb + h)), spec, spec,
                pl.BlockSpec((None, None, hd, hd), lambda h, j: (h, nblk - 1 - j, 0, 0))]
    args = [q, k, v, g, do, states]
    if adds is not None:
        in_specs += [spec, spec]
        args += list(adds)
    out = jax.ShapeDtypeStruct((L, HG_HEADS * hd), F32)
    return pl.pallas_call(
        body, name=name, grid=(HG_HEADS, nblk), in_specs=in_specs,
        out_specs=[spec] * 4, out_shape=[out] * 4,
        scratch_shapes=[pltpu.VMEM((hd, hd), F32), pltpu.VMEM((nsub, hd, hd), F32)],
        compiler_params=_cparams("parallel", "arbitrary"),
    )(*args)


def _ssd_consts(rev):
    c = SSD_CHUNK
    gw = SSM_HPG * SSM_HEAD_DIM
    r2 = lax.broadcasted_iota(jnp.int32, (c, c), 0)
    c2 = lax.broadcasted_iota(jnp.int32, (c, c), 1)
    low = (r2 <= c2) if rev else (r2 >= c2)
    lane = lax.broadcasted_iota(jnp.int32, (1, gw), 1)
    return low.astype(F32), low, r2 == c2, lane, lane >> 6


def _expand_heads(v4, head_of_lane):
    lane4 = lax.broadcasted_iota(jnp.int32, v4.shape, 1)
    out = None
    for j in range(SSM_HPG):
        col = jnp.sum(jnp.where(lane4 == j, v4, 0.0), axis=-1, keepdims=True)
        term = jnp.where(head_of_lane == j, col, 0.0)
        out = term if out is None else out + term
    return out


def _ssd_chunk(st, x, bm, cm, dtr, bias, alog, dsk, *, rev, consts, skip):
    tri, low, eye, lane, head_of_lane = consts
    c = x.shape[0]
    dt_l = _expand_heads(jax.nn.softplus(dtr + bias), head_of_lane)
    a_l = _expand_heads(-jnp.exp(alog), head_of_lane)
    acum = jnp.dot(tri, dt_l * a_l, precision=HIGHEST, preferred_element_type=F32)
    xd = x * dt_l
    cb = lax.dot_general(cm.astype(BF16), bm.astype(BF16), NT_DIMS, preferred_element_type=F32)
    y = jnp.dot(cm.astype(BF16), st.astype(BF16), preferred_element_type=F32) * jnp.exp(acum)
    for j in range(SSM_HPG):
        acol = jnp.sum(jnp.where(lane == j * SSM_HEAD_DIM, acum, 0.0), axis=-1, keepdims=True)
        ab = jnp.broadcast_to(acol, (c, c))
        arow = jnp.sum(jnp.where(eye, ab, 0.0), axis=0, keepdims=True)
        lmat = jnp.exp(jnp.where(low, ab - arow, -jnp.inf))
        xj = jnp.where(head_of_lane == j, xd, 0.0)
        y = y + jnp.dot((cb * lmat).astype(BF16), xj.astype(BF16), preferred_element_type=F32)
    a_end = acum[0:1] if rev else acum[c - 1:c]
    xdec = (xd * jnp.exp(a_end - acum)).astype(BF16)
    st_new = st * jnp.exp(a_end) + lax.dot_general(bm.astype(BF16), xdec, TN_DIMS, preferred_element_type=F32)
    if skip:
        y = y + x * _expand_heads(dsk, head_of_lane)
    return st_new, y


def _ssd_specs(L, rev):
    blk = min(SSD_BLOCK, L)
    nblk = L // blk
    gw = SSM_HPG * SSM_HEAD_DIM

    def pos(j):
        return nblk - 1 - j if rev else j

    return blk, nblk, gw, pos


def _ssd_fwd(xbc, dtr, bias, alog, dsk, *, rev, skip, add, name):
    L = xbc.shape[0]
    blk, nblk, gw, pos = _ssd_specs(L, rev)
    nsub = blk // SSD_CHUNK
    n = SSM_STATE

    def body(*refs):
        x_ref, b_ref, c_ref, dt_ref, bias_ref, alog_ref, dsk_ref = refs[:7]
        add_ref = refs[7] if add is not None else None
        y_ref, st_out, st_scr = refs[-3:]
        consts = _ssd_consts(rev)

        @pl.when(pl.program_id(1) == 0)
        def _():
            st_scr[...] = jnp.zeros(st_scr.shape, F32)

        st_out[...] = st_scr[...]

        def step(i, carry):
            sl = pl.ds(pl.multiple_of((nsub - 1 - i if rev else i) * SSD_CHUNK, SSD_CHUNK), SSD_CHUNK)
            st_new, y = _ssd_chunk(st_scr[...], x_ref[sl, :], b_ref[sl, :], c_ref[sl, :], dt_ref[sl, :],
                                   bias_ref[...], alog_ref[...], dsk_ref[...], rev=rev, consts=consts, skip=skip)
            st_scr[...] = st_new
            if add_ref is not None:
                y = y + add_ref[sl, :]
            y_ref[sl, :] = y
            return carry

        lax.fori_loop(0, nsub, step, 0)

    b0 = D_INNER // n
    yspec = pl.BlockSpec((blk, gw), lambda g, j: (pos(j), g))
    pspec = pl.BlockSpec((None, 1, SSM_HPG), lambda g, j: (g, 0, 0))
    in_specs = [yspec,
                pl.BlockSpec((blk, n), lambda g, j: (pos(j), b0 + g)),
                pl.BlockSpec((blk, n), lambda g, j: (pos(j), b0 + SSM_GROUPS + g)),
                pl.BlockSpec((None, blk, SSM_HPG), lambda g, j: (g, pos(j), 0)),
                pspec, pspec, pspec]
    args = [xbc, xbc, xbc, dtr, bias, alog, dsk]
    if add is not None:
        in_specs.append(yspec)
        args.append(add)
    return pl.pallas_call(
        body, name=name, grid=(SSM_GROUPS, nblk), in_specs=in_specs,
        out_specs=[yspec, pl.BlockSpec((None, None, n, gw), lambda g, j: (g, j, 0, 0))],
        out_shape=[jax.ShapeDtypeStruct((L, D_INNER), F32),
                   jax.ShapeDtypeStruct((SSM_GROUPS, nblk, n, gw), F32)],
        scratch_shapes=[pltpu.VMEM((n, gw), F32)],
        compiler_params=_cparams("parallel", "arbitrary"),
    )(*args)


def _ssd_bwd(xbc, dtr, bias, alog, dsk, dy, states, *, rev, skip, adds, name):
    L = xbc.shape[0]
    blk, nblk, gw, _ = _ssd_specs(L, rev)
    nsub = blk // SSD_CHUNK
    n = SSM_STATE
    n_add = 0 if adds is None else 3

    def body(*refs):
        x_ref, b_ref, c_ref, dt_ref, bias_ref, alog_ref, dsk_ref, dy_ref, st_in = refs[:9]
        add_refs = refs[9:9 + n_add]
        dx_ref, db_ref, dc_ref, ddt_ref, dbias_ref, dalog_ref, ddsk_ref, dst_scr, sub_scr = refs[9 + n_add:]
        consts = _ssd_consts(rev)
        chunk = functools.partial(_ssd_chunk, rev=rev, consts=consts, skip=skip)

        @pl.when(pl.program_id(1) == 0)
        def _():
            dst_scr[...] = jnp.zeros(dst_scr.shape, F32)
            dbias_ref[...] = jnp.zeros(dbias_ref.shape, F32)
            dalog_ref[...] = jnp.zeros(dalog_ref.shape, F32)
            ddsk_ref[...] = jnp.zeros(ddsk_ref.shape, F32)

        def rows(i):
            return pl.ds(pl.multiple_of((nsub - 1 - i if rev else i) * SSD_CHUNK, SSD_CHUNK), SSD_CHUNK)

        def operands(i):
            sl = rows(i)
            return (x_ref[sl, :], b_ref[sl, :], c_ref[sl, :], dt_ref[sl, :], bias_ref[...], alog_ref[...], dsk_ref[...])

        def replay(i, st):
            sub_scr[i] = st
            return chunk(st, *operands(i))[0]

        lax.fori_loop(0, nsub, replay, st_in[...])

        def step(k, carry):
            i = nsub - 1 - k
            sl = rows(i)
            _, vjp = jax.vjp(chunk, sub_scr[i], *operands(i))
            dst, dx, db, dc, ddt, dbias, dalog, ddsk = vjp((dst_scr[...], dy_ref[sl, :]))
            dst_scr[...] = dst
            if n_add:
                dx = dx + add_refs[0][sl, :]
                db = db + add_refs[1][sl, :]
                dc = dc + add_refs[2][sl, :]
            dx_ref[sl, :] = dx
            db_ref[sl, :] = db
            dc_ref[sl, :] = dc
            ddt_ref[sl, :] = ddt
            dbias_ref[...] += dbias
            dalog_ref[...] += dalog
            ddsk_ref[...] += ddsk
            return carry

        lax.fori_loop(0, nsub, step, 0)

    def pos(j):
        p = nblk - 1 - j
        return nblk - 1 - p if rev else p

    b0 = D_INNER // n
    xspec = pl.BlockSpec((blk, gw), lambda g, j: (pos(j), g))
    nspec = pl.BlockSpec((blk, n), lambda g, j: (pos(j), g))
    dtspec = pl.BlockSpec((None, blk, SSM_HPG), lambda g, j: (g, pos(j), 0))
    pspec = pl.BlockSpec((None, 1, SSM_HPG), lambda g, j: (g, 0, 0))
    in_specs = [xspec,
                pl.BlockSpec((blk, n), lambda g, j: (pos(j), b0 + g)),
                pl.BlockSpec((blk, n), lambda g, j: (pos(j), b0 + SSM_GROUPS + g)),
                dtspec, pspec, pspec, pspec, xspec,
                pl.BlockSpec((None, None, n, gw), lambda g, j: (g, nblk - 1 - j, 0, 0))]
    args = [xbc, xbc, xbc, dtr, bias, alog, dsk, dy, states]
    if adds is not None:
        in_specs += [xspec, nspec, nspec]
        args += list(adds)
    par = jax.ShapeDtypeStruct((SSM_GROUPS, 1, SSM_HPG), F32)
    return pl.pallas_call(
        body, name=name, grid=(SSM_GROUPS, nblk), in_specs=in_specs,
        out_specs=[xspec, nspec, nspec, dtspec, pspec, pspec, pspec],
        out_shape=[jax.ShapeDtypeStruct((L, D_INNER), F32), jax.ShapeDtypeStruct((L, GN), F32),
                   jax.ShapeDtypeStruct((L, GN), F32), jax.ShapeDtypeStruct((SSM_GROUPS, L, SSM_HPG), F32),
                   par, par, par],
        scratch_shapes=[pltpu.VMEM((n, gw), F32), pltpu.VMEM((nsub, n, gw), F32)],
        compiler_params=_cparams("parallel", "arbitrary"),
    )(*args)


def _norm_fwd(h, w, name):
    d = h.shape[1]
    return _row_kernel(lambda hv, wv: (_rms(hv, wv),), [(h, d, 0)], [w], [(d, BF16)], [], name=name)[0]


def _norm_bwd(h, w, du, dh_in, name):
    d = h.shape[1]

    def fn(hv, duv, dhv, wv):
        _, vjp = jax.vjp(_rms, hv, wv)
        dh, dw = vjp(duv.astype(F32))
        dh = dh + dhv
        return dh, dh, dw

    return _row_kernel(fn, [(h, d, 0), (du, d, 0), (dh_in, d, 0)], [w], [(d, F32), (d, BF16)], [(1, d)], name=name)


def _ffn_fwd(h, norm_w, w_gate, w_val, w_out, conv_w, conv_b, tag):
    u = _norm_fwd(h, norm_w, f"{tag}_norm")
    pg = _matmul(u, w_gate, name=f"{tag}_in_gate")
    pv = _matmul(u, w_val, name=f"{tag}_in_val")
    y = _col_kernel(lambda g, v, w, b: (_ffn_mid_fn(g, v, w, b),), [(pg, 0), (pv, 0)], [(conv_w, 0), (conv_b, 0)],
                    [BF16], [], name=f"{tag}_mid", n_tiles=D_FF // COL_TILE)[0]
    h_out = _matmul(y, w_out, add=h, name=f"{tag}_out")
    return h_out, (u, pg, pv, y)


def _ffn_bwd(h, dh, dh16, saved, norm_w, w_gate, w_val, w_out, conv_w, conv_b, tag):
    u, pg, pv, y = saved
    dy = _matmul(dh16, w_out, nt=True, name=f"{tag}_out_dx")
    dw_out = _matmul(y.T, dh16, name=f"{tag}_out_dw")

    def fn(g, v, ct, w, b):
        _, vjp = jax.vjp(_ffn_mid_fn, g, v, w, b)
        return vjp(ct)

    dpg, dpv, dcw, dcb = _col_kernel(fn, [(pg, 0), (pv, 0), (dy, 0)], [(conv_w, 0), (conv_b, 0)], [BF16, BF16],
                                     [FFN_CONV, 1], name=f"{tag}_mid_bwd", n_tiles=D_FF // COL_TILE)
    du = _matmul(dpg, w_gate, nt=True, name=f"{tag}_gate_dx")
    du = _matmul(dpv, w_val, nt=True, add=du, out_dtype=BF16, name=f"{tag}_val_dx")
    ut = u.T
    dw_gate = _matmul(ut, dpg, name=f"{tag}_gate_dw")
    dw_val = _matmul(ut, dpv, name=f"{tag}_val_dw")
    dh, dh16, dnw = _norm_bwd(h, norm_w, du, dh, f"{tag}_norm_bwd")
    return dh, dh16, dict(w_in=jnp.concatenate([dw_gate, dw_val], axis=1), w_out=dw_out, conv_w=dcw, conv_b=dcb, norm=dnw)


def _hgrn2_fwd(h, norm_w, w_in, lb_logits, a_norm_w, w_out):
    d = D_MODEL
    u = _norm_fwd(h, norm_w, "hg_norm")
    pa = _matmul(u, w_in, name="hg_in")
    qs, lf_fw, k_fw, lf_bw, k_bw = _row_kernel(
        _hgrn2_pre_fn, [(pa, d, 0), (pa, d, 1), (pa, d, 2)], [lb_logits], [(d, F32)] * 5, [], name="hg_pre")
    o_fw, st_fw = _gla_fwd(qs, k_fw, pa, 3 * HG_HEADS, lf_fw, rev=False, add=None, name="hg_gla_fw")
    o, st_bw = _gla_fwd(qs, k_bw, pa, 3 * HG_HEADS, lf_bw, rev=True, add=o_fw, name="hg_gla_bw")
    y = _row_kernel(lambda ov, gv, wv: (_hgrn2_post_fn(ov, gv, wv),), [(o, d, 0), (pa, d, 4)], [a_norm_w],
                    [(d, BF16)], [], name="hg_post")[0]
    h_out = _matmul(y, w_out, add=h, name="hg_out")
    return h_out, (u, pa, qs, lf_fw, k_fw, lf_bw, k_bw, st_fw, st_bw, o, y)


def _hgrn2_bwd(h, dh, dh16, saved, norm_w, w_in, lb_logits, a_norm_w, w_out):
    d = D_MODEL
    u, pa, qs, lf_fw, k_fw, lf_bw, k_bw, st_fw, st_bw, o, y = saved
    dy = _matmul(dh16, w_out, nt=True, name="hg_out_dx")
    dw_out = _matmul(y.T, dh16, name="hg_out_dw")

    def post_bwd(ov, gv, ct, wv):
        _, vjp = jax.vjp(_hgrn2_post_fn, ov, gv, wv)
        return vjp(ct)

    do, dg, dnw = _row_kernel(post_bwd, [(o, d, 0), (pa, d, 4), (dy, d, 0)], [a_norm_w], [(d, F32), (d, F32)],
                              [(1, HG_HEAD_DIM)], name="hg_post_bwd")
    dq1, dk_fw, dv1, dlf_fw = _gla_bwd(qs, k_fw, pa, 3 * HG_HEADS, lf_fw, do, st_fw, rev=False, adds=None,
                                       name="hg_gla_fw_bwd")
    dqs, dk_bw, dv, dlf_bw = _gla_bwd(qs, k_bw, pa, 3 * HG_HEADS, lf_bw, do, st_bw, rev=True, adds=(dq1, dv1),
                                      name="hg_gla_bw_bwd")

    def pre_bwd(qr, fr, br, c0, c1, c2, c3, c4, dvv, dgv, lbl):
        _, vjp = jax.vjp(_hgrn2_pre_fn, qr, fr, br, lbl)
        dq, df, db, dlbl = vjp((c0, c1, c2, c3, c4))
        return jnp.concatenate([dq, df, db, dvv, dgv], axis=1), dlbl

    rows = [(pa, d, 0), (pa, d, 1), (pa, d, 2), (dqs, d, 0), (dlf_fw, d, 0), (dk_fw, d, 0), (dlf_bw, d, 0),
            (dk_bw, d, 0), (dv, d, 0), (dg, d, 0)]
    dpa, dlbl = _row_kernel(pre_bwd, rows, [lb_logits], [(5 * d, BF16)], [lb_logits.shape], name="hg_pre_bwd")
    du = _matmul(dpa, w_in, nt=True, out_dtype=BF16, name="hg_in_dx")
    dw_in = _matmul(u.T, dpa, name="hg_in_dw")
    dh, dh16, dn1 = _norm_bwd(h, norm_w, du, dh, "hg_norm_bwd")
    return dh, dh16, dict(w_in=dw_in, w_out=dw_out, lb=dlbl, a_norm=dnw, norm=dn1)


def _group_params(p):
    return p.reshape(SSM_GROUPS, 1, SSM_HPG)


def _mamba_fwd(h, norm_w, w_z, w_xbc, w_dt, conv_w, conv_b, dt_bias, a_log, d_skip, b_norm_w, w_out):
    L = h.shape[0]
    u = _norm_fwd(h, norm_w, "mb_norm")
    z = _matmul(u, w_z, name="mb_in_z")
    xbc_raw = _matmul(u, w_xbc, name="mb_in_xbc")
    dt_raw = _matmul(u, w_dt, name="mb_in_dt")
    xbc = _col_kernel(lambda xv, w, b: (_mamba_conv_fn(xv, w, b),), [(xbc_raw, 0)], [(conv_w, 0), (conv_b, 0)],
                      [F32], [], name="mb_conv", n_tiles=CONV_DIM // COL_TILE)[0]
    dtr = dt_raw.reshape(L, 2, SSM_GROUPS, SSM_HPG).transpose(1, 2, 0, 3)
    bias, alog = dt_bias.reshape(2, -1), a_log.reshape(2, -1)
    dsk = _group_params(d_skip.reshape(-1))
    y_fw, st_fw = _ssd_fwd(xbc, dtr[0], _group_params(bias[0]), _group_params(alog[0]), dsk, rev=False, skip=True,
                           add=None, name="mb_ssd_fw")
    ysum, st_bw = _ssd_fwd(xbc, dtr[1], _group_params(bias[1]), _group_params(alog[1]), dsk, rev=True, skip=False,
                           add=y_fw, name="mb_ssd_bw")
    y = _row_kernel(lambda yv, zv, wv: (_mamba_post_fn(yv, zv, wv),), [(ysum, D_INNER, 0), (z, D_INNER, 0)],
                    [b_norm_w], [(D_INNER, BF16)], [], name="mb_post")[0]
    h_out = _matmul(y, w_out, add=h, name="mb_out")
    return h_out, (u, z, xbc_raw, xbc, dtr, st_fw, st_bw, ysum, y)


def _mamba_bwd(h, dh, dh16, saved, norm_w, w_z, w_xbc, w_dt, conv_w, conv_b, dt_bias, a_log, d_skip, b_norm_w, w_out):
    L = h.shape[0]
    u, z, xbc_raw, xbc, dtr, st_fw, st_bw, ysum, y = saved
    dy = _matmul(dh16, w_out, nt=True, name="mb_out_dx")
    dw_out = _matmul(y.T, dh16, name="mb_out_dw")

    def post_bwd(yv, zv, ct, wv):
        _, vjp = jax.vjp(_mamba_post_fn, yv, zv, wv)
        return vjp(ct)

    dys, dz, dbn = _row_kernel(post_bwd, [(ysum, D_INNER, 0), (z, D_INNER, 0), (dy, D_INNER, 0)], [b_norm_w],
                               [(D_INNER, F32), (D_INNER, BF16)], [(1, D_INNER)], name="mb_post_bwd")
    bias, alog = dt_bias.reshape(2, -1), a_log.reshape(2, -1)
    dsk = _group_params(d_skip.reshape(-1))
    dx1, db1, dc1, ddt_fw, dbias_fw, dalog_fw, ddsk = _ssd_bwd(
        xbc, dtr[0], _group_params(bias[0]), _group_params(alog[0]), dsk, dys, st_fw, rev=False, skip=True,
        adds=None, name="mb_ssd_fw_bwd")
    dx, db, dc, ddt_bw, dbias_bw, dalog_bw, _ = _ssd_bwd(
        xbc, dtr[1], _group_params(bias[1]), _group_params(alog[1]), dsk, dys, st_bw, rev=True, skip=False,
        adds=(dx1, db1, dc1), name="mb_ssd_bw_bwd")

    def conv_bwd(n_tiles, ct, first):
        def fn(xv, ctv, w, b):
            _, vjp = jax.vjp(_mamba_conv_fn, xv, w, b)
            return vjp(ctv)
        return _col_kernel(fn, [(xbc_raw, first), (ct, 0)], [(conv_w, first), (conv_b, first)], [BF16],
                           [SSM_CONV, 1], name=f"mb_conv_bwd_{first}", n_tiles=n_tiles)

    nx, nb = D_INNER // COL_TILE, GN // COL_TILE
    parts = [conv_bwd(nx, dx, 0), conv_bwd(nb, db, nx), conv_bwd(nb, dc, nx + nb)]
    dxbc = jnp.concatenate([p[0] for p in parts], axis=1)
    dcw = jnp.concatenate([p[1] for p in parts], axis=1)
    dcb = jnp.concatenate([p[2] for p in parts], axis=1)
    ddt = jnp.stack([ddt_fw, ddt_bw]).transpose(2, 0, 1, 3).reshape(L, 2 * SSM_GROUPS * SSM_HPG).astype(BF16)
    du = _matmul(dz, w_z, nt=True, name="mb_z_dx")
    du = _matmul(dxbc, w_xbc, nt=True, add=du, name="mb_xbc_dx")
    du = _matmul(ddt, w_dt, nt=True, add=du, out_dtype=BF16, name="mb_dt_dx")
    ut = u.T
    dw_in = jnp.concatenate([_matmul(ut, dz, name="mb_z_dw"), _matmul(ut, dxbc, name="mb_xbc_dw"),
                             _matmul(ut, ddt, name="mb_dt_dw")], axis=1)
    dh, dh16, dn1 = _norm_bwd(h, norm_w, du, dh, "mb_norm_bwd")
    grads = dict(w_in=dw_in, w_out=dw_out, conv_w=dcw, conv_b=dcb, b_norm=dbn, norm=dn1,
                 dt_bias=jnp.stack([dbias_fw, dbias_bw]).reshape(1, 2, -1),
                 a_log=jnp.stack([dalog_fw, dalog_bw]).reshape(1, 2, -1), d_skip=ddsk.reshape(1, -1))
    return dh, dh16, grads


def _loss_head(h, target, w):
    d = h.shape[1]

    def fn(hv, tv, wv):
        def loss(hv, wv):
            err = _rms(hv, wv) - tv
            return 0.5 * jnp.sum(jnp.mean(err * err, axis=-1, keepdims=True), axis=0, keepdims=True)
        val, vjp = jax.vjp(loss, hv, wv)
        dh, dw = vjp(jnp.ones((1, 1), F32))
        return dh, dh, val, dw

    return _row_kernel(fn, [(h, d, 0), (target, d, 0)], [w], [(d, F32), (d, BF16)], [(1, 1), (1, d)], name="loss_head")


def _local_step(x, target, w):
    g_w, v_w = w["ffn_w_in"][:, :, :D_FF], w["ffn_w_in"][:, :, D_FF:]
    wz, wxbc, wdt = (w["b_w_in"][0, :, :D_INNER], w["b_w_in"][0, :, D_INNER:D_INNER + CONV_DIM],
                     w["b_w_in"][0, :, D_INNER + CONV_DIM:])
    ffn = [(w["norm2_w"][i:i + 1], g_w[i], v_w[i], w["ffn_w_out"][i], w["ffn_conv_w"][i], w["ffn_conv_b"][i:i + 1])
           for i in range(2)]
    hg = (w["norm1_w"][0:1], w["a_w_in"][0], w["a_lb_logits"], w["a_norm_w"], w["a_w_out"][0])
    mb = (w["norm1_w"][1:2], wz, wxbc, wdt, w["b_conv_w"][0], w["b_conv_b"], w["b_dt_bias"], w["b_a_log"],
          w["b_d_skip"], w["b_norm_w"], w["b_w_out"][0])

    h0 = x
    h1, s_hg = _hgrn2_fwd(h0, *hg)
    h2, s_f0 = _ffn_fwd(h1, *ffn[0], "ffn0")
    h3, s_mb = _mamba_fwd(h2, *mb)
    h4, s_f1 = _ffn_fwd(h3, *ffn[1], "ffn1")
    dh, dh16, loss, d_final = _loss_head(h4, target, w["final_norm_w"].reshape(1, -1))
    dh, dh16, g_f1 = _ffn_bwd(h3, dh, dh16, s_f1, *ffn[1], "ffn1")
    dh, dh16, g_mb = _mamba_bwd(h2, dh, dh16, s_mb, *mb)
    dh, dh16, g_f0 = _ffn_bwd(h1, dh, dh16, s_f0, *ffn[0], "ffn0")
    dh, dh16, g_hg = _hgrn2_bwd(h0, dh, dh16, s_hg, *hg)
    grads = dict(
        norm1_w=jnp.concatenate([g_hg["norm"], g_mb["norm"]], axis=0),
        norm2_w=jnp.concatenate([g_f0["norm"], g_f1["norm"]], axis=0),
        a_w_in=g_hg["w_in"][None], a_lb_logits=g_hg["lb"], a_norm_w=g_hg["a_norm"], a_w_out=g_hg["w_out"][None],
        b_w_in=g_mb["w_in"][None], b_conv_w=g_mb["conv_w"][None], b_conv_b=g_mb["conv_b"],
        b_dt_bias=g_mb["dt_bias"], b_a_log=g_mb["a_log"], b_d_skip=g_mb["d_skip"], b_norm_w=g_mb["b_norm"],
        b_w_out=g_mb["w_out"][None],
        ffn_w_in=jnp.stack([g_f0["w_in"], g_f1["w_in"]]), ffn_conv_w=jnp.stack([g_f0["conv_w"], g_f1["conv_w"]]),
        ffn_conv_b=jnp.concatenate([g_f0["conv_b"], g_f1["conv_b"]], axis=0),
        ffn_w_out=jnp.stack([g_f0["w_out"], g_f1["w_out"]]), final_norm_w=d_final.reshape(-1),
    )
    return loss, dh, grads


def _mesh_pos():
    return lax.axis_index("x"), lax.axis_index("y"), lax.axis_index("c")


def _all_gather(shard, name):
    m_per, n = shard.shape

    def body(x_ref, out_ref, send_sems, recv_sems, local_sem):
        x, y, c = _mesh_pos()
        me, sibling = (x, y, c), (x, y, 1 - c)
        chips = [(1 - x, y), (x, 1 - y), (1 - x, 1 - y)]

        def rows(px, py, pc):
            return out_ref.at[pl.ds((4 * px + 2 * py + pc) * m_per, m_per), :]

        def copy(k, block, to, src=None):
            return pltpu.make_async_remote_copy(
                src_ref=rows(*block) if src is None else src, dst_ref=rows(*block),
                send_sem=send_sems.at[k], recv_sem=recv_sems.at[k], device_id=to, device_id_type=MESH)

        mine = pltpu.make_async_copy(x_ref, rows(*me), local_sem)
        mine.start()
        first = [copy(0, me, sibling, src=x_ref)]
        first += [copy(1 + j, me, (*chip, c), src=x_ref) for j, chip in enumerate(chips)]
        for cp in first:
            cp.start()
        passed = [copy(4 + j, (*chip, c), sibling) for j, chip in enumerate(chips)]
        for j, chip in enumerate(chips):
            copy(1 + j, (*chip, c), me).wait_recv()
            passed[j].start()
        copy(0, sibling, me).wait_recv()
        for j, chip in enumerate(chips):
            copy(4 + j, (*chip, 1 - c), me).wait_recv()
        for cp in first + passed:
            cp.wait_send()
        mine.wait()

    return pl.pallas_call(
        body, name=name, out_shape=jax.ShapeDtypeStruct((N_DEV * m_per, n), shard.dtype),
        in_specs=[pl.BlockSpec(memory_space=pl.ANY)], out_specs=pl.BlockSpec(memory_space=pl.ANY),
        scratch_shapes=[pltpu.SemaphoreType.DMA((7,)), pltpu.SemaphoreType.DMA((7,)), pltpu.SemaphoreType.DMA(())],
    )(shard)


def _exchange_parts(parts, name):
    def body(p_ref, got_ref, send_sems, recv_sems, local_sem):
        x, y, c = _mesh_pos()
        me = 4 * x + 2 * y + c
        mine = pltpu.make_async_copy(p_ref.at[me], got_ref.at[me], local_sem)
        mine.start()
        copies = []
        for k in range(1, N_DEV):
            px = 1 - x if k & 4 else x
            py = 1 - y if k & 2 else y
            pc = 1 - c if k & 1 else c
            cp = pltpu.make_async_remote_copy(
                src_ref=p_ref.at[4 * px + 2 * py + pc], dst_ref=got_ref.at[me],
                send_sem=send_sems.at[k - 1], recv_sem=recv_sems.at[k - 1], device_id=(px, py, pc), device_id_type=MESH)
            cp.start()
            copies.append(cp)
        for cp in copies:
            cp.wait()
        mine.wait()

    return pl.pallas_call(
        body, name=name, out_shape=jax.ShapeDtypeStruct(parts.shape, parts.dtype),
        in_specs=[pl.BlockSpec(memory_space=pl.ANY)], out_specs=pl.BlockSpec(memory_space=pl.ANY),
        scratch_shapes=[pltpu.SemaphoreType.DMA((7,)), pltpu.SemaphoreType.DMA((7,)), pltpu.SemaphoreType.DMA(())],
    )(parts)


def _adamw(got, w, m, v, *, name, tile):
    rows, width = w.shape
    c1 = 1.0 / (1.0 - ADAM_B1 ** ADAM_STEP)
    c2 = 1.0 / (1.0 - ADAM_B2 ** ADAM_STEP)

    def body(got_ref, w_ref, m_ref, v_ref, g_ref, d_ref, nm_ref, nv_ref):
        g = got_ref[0]
        for s in range(1, N_DEV):
            g = g + got_ref[s]
        m_new = ADAM_B1 * m_ref[...] + (1.0 - ADAM_B1) * g
        v_new = ADAM_B2 * v_ref[...] + (1.0 - ADAM_B2) * (g * g)
        g_ref[...] = g
        nm_ref[...] = m_new
        nv_ref[...] = v_new
        d_ref[...] = -ADAM_LR * ((m_new * c1) / (jnp.sqrt(v_new * c2) + ADAM_EPS) + ADAM_WD * w_ref[...])

    spec = pl.BlockSpec((tile, width), lambda i: (i, 0))
    out = jax.ShapeDtypeStruct((rows, width), F32)
    return pl.pallas_call(
        body, name=name, grid=(rows // tile,),
        in_specs=[pl.BlockSpec((N_DEV, tile, width), lambda i: (0, i, 0)), spec, spec, spec],
        out_specs=[spec] * 4, out_shape=[out] * 4, compiler_params=_cparams("parallel"),
    )(got, w, m, v)


SHARDED = dict(a_w_in=(2, True), a_w_out=(1, True), b_w_in=(2, True), b_w_out=(1, True), ffn_w_in=(2, True),
               ffn_w_out=(1, True), b_conv_w=(2, False), b_conv_b=(1, False), b_norm_w=(1, False), ffn_conv_w=(2, False))
REPLICATED = ("norm1_w", "norm2_w", "a_lb_logits", "a_norm_w", "b_dt_bias", "b_a_log", "b_d_skip", "ffn_conv_b",
              "final_norm_w")
WEIGHTS = ("norm1_w", "norm2_w", "a_w_in", "a_lb_logits", "a_norm_w", "a_w_out", "b_w_in", "b_conv_w", "b_conv_b",
           "b_dt_bias", "b_a_log", "b_d_skip", "b_norm_w", "b_w_out", "ffn_w_in", "ffn_conv_w", "ffn_conv_b",
           "ffn_w_out", "final_norm_w")


def _pad_rows(flat, multiple):
    n = flat.shape[-1]
    per = PACK_W * multiple
    total = -(-n // per) * per
    flat = jnp.pad(flat, [(0, 0)] * (flat.ndim - 1) + [(0, total - n)])
    return flat.reshape(*flat.shape[:-1], total // PACK_W, PACK_W)


def _to_parts(full, axis):
    shp = full.shape
    t = full.reshape(*shp[:axis], N_DEV, shp[axis] // N_DEV, *shp[axis + 1:])
    return jnp.moveaxis(t, axis, 0).reshape(N_DEV, -1)


def _from_parts(parts, local_shape, axis):
    t = jnp.moveaxis(parts.reshape(N_DEV, *local_shape), 0, axis)
    shp = list(local_shape)
    shp[axis] *= N_DEV
    return t.reshape(shp)


def _gather_weights(local):
    big = [n for n, (_, mm) in SHARDED.items() if mm]
    small = [n for n, (_, mm) in SHARDED.items() if not mm]
    flat16 = jnp.concatenate([local[n].astype(BF16).reshape(-1) for n in big])
    flat32 = jnp.concatenate([local[n].reshape(-1) for n in small])
    bits = lax.bitcast_convert_type(flat32, BF16).reshape(-1)
    pack = _pad_rows(jnp.concatenate([flat16, bits]), 16)
    got = _all_gather(pack, "gather_weights").reshape(N_DEV, -1)
    whole, off = {}, 0
    for n in big:
        size = local[n].size
        whole[n] = _from_parts(got[:, off:off + size], local[n].shape, SHARDED[n][0])
        off += size
    for n in small:
        size = local[n].size
        part = lax.bitcast_convert_type(got[:, off:off + 2 * size].reshape(N_DEV, size, 2), F32)
        whole[n] = _from_parts(part, local[n].shape, SHARDED[n][0])
        off += 2 * size
    return whole


def _pack_local(tree):
    return _pad_rows(jnp.concatenate([tree[n].reshape(-1) for n in SHARDED]), 8)


def _unpack_local(pack, like):
    flat, out, off = pack.reshape(-1), {}, 0
    for n in SHARDED:
        out[n] = flat[off:off + like[n].size].reshape(like[n].shape)
        off += like[n].size
    return out


def _pack_small(tree, extra):
    flat = jnp.concatenate([tree[n].reshape(-1) for n in REPLICATED] + [extra.reshape(-1)])
    return _pad_rows(flat, 8)


def _unpack_small(pack, like):
    flat, out, off = pack.reshape(-1), {}, 0
    for n in REPLICATED:
        out[n] = flat[off:off + like[n].size].reshape(like[n].shape)
        off += like[n].size
    return out, flat[off]


def kernel(x, norm1_w, norm2_w, a_w_in, a_lb_logits, a_norm_w, a_w_out, b_w_in, b_conv_w, b_conv_b, b_dt_bias, b_a_log, b_d_skip, b_norm_w, b_w_out, ffn_w_in, ffn_conv_w, ffn_conv_b, ffn_w_out, final_norm_w, loss_target, m_norm1_w, m_norm2_w, m_a_w_in, m_a_lb_logits, m_a_norm_w, m_a_w_out, m_b_w_in, m_b_conv_w, m_b_conv_b, m_b_dt_bias, m_b_a_log, m_b_d_skip, m_b_norm_w, m_b_w_out, m_ffn_w_in, m_ffn_conv_w, m_ffn_conv_b, m_ffn_w_out, m_final_norm_w, v_norm1_w, v_norm2_w, v_a_w_in, v_a_lb_logits, v_a_norm_w, v_a_w_out, v_b_w_in, v_b_conv_w, v_b_conv_b, v_b_dt_bias, v_b_a_log, v_b_d_skip, v_b_norm_w, v_b_w_out, v_ffn_w_in, v_ffn_conv_w, v_ffn_conv_b, v_ffn_w_out, v_final_norm_w):
    given = dict(locals())
    local = {n: given[n] for n in WEIGHTS}
    mom = {n: given["m_" + n] for n in WEIGHTS}
    var = {n: given["v_" + n] for n in WEIGHTS}

    whole = dict(local)
    whole.update(_gather_weights(local))
    loss, grad_x, grads = _local_step(x[0], loss_target[0], whole)

    parts = _pad_rows(jnp.concatenate([_to_parts(grads[n], SHARDED[n][0]) for n in SHARDED], axis=1), 8)
    got = _exchange_parts(parts, "exchange_grads")
    g, dlt, nm, nv = _adamw(got, _pack_local(local), _pack_local(mom), _pack_local(var), name="adamw_sharded",
                            tile=parts.shape[1] // 10 if parts.shape[1] % 80 == 0 else 8)
    out_g, out_d, out_m, out_v = (_unpack_local(t, local) for t in (g, dlt, nm, nv))

    small = _pack_small(grads, loss)
    rows = small.shape[0]
    got_s = _all_gather(small, "gather_small").reshape(N_DEV, rows, PACK_W)
    zero = jnp.zeros((1,), F32)
    g, dlt, nm, nv = _adamw(got_s, _pack_small(local, zero), _pack_small(mom, zero), _pack_small(var, zero),
                            name="adamw_replicated", tile=rows)
    (rep_g, total), (rep_d, _), (rep_m, _), (rep_v, _) = (_unpack_small(t, local) for t in (g, dlt, nm, nv))
    out_g.update(rep_g)
    out_d.update(rep_d)
    out_m.update(rep_m)
    out_v.update(rep_v)

    return (total, grad_x[None], *[out_g[n] for n in WEIGHTS], *[out_d[n] for n in WEIGHTS],
            *[out_m[n] for n in WEIGHTS], *[out_v[n] for n in WEIGHTS])
```

```python
import functools

import jax
import jax.numpy as jnp
from jax import lax
from jax.experimental import pallas as pl
from jax.experimental.pallas import tpu as pltpu

F32 = jnp.float32
BF16 = jnp.bfloat16
HIGHEST = lax.Precision.HIGHEST
MESH = pl.DeviceIdType.MESH

N_DEV = 8
EPS = 1e-6
D_MODEL = 1024
HG_HEADS = 8
HG_HEAD_DIM = 128
D_INNER = 2048
SSM_HEAD_DIM = 64
SSM_GROUPS = 8
SSM_HPG = 4
SSM_STATE = 128
GN = SSM_GROUPS * SSM_STATE
CONV_DIM = D_INNER + 2 * GN
D_FF = 2816
SSM_CONV = 5
FFN_CONV = 3

ADAM_LR = 0.001
ADAM_B1 = 0.9
ADAM_B2 = 0.999
ADAM_EPS = 1e-08
ADAM_WD = 0.01
ADAM_STEP = 10

LANES = 128
ROW_TILE = 256
COL_TILE = 128
GLA_CHUNK = 16
GLA_BLOCK = 512
GLA_HEADS_PER_STEP = 4
SSD_CHUNK = 128
SSD_BLOCK = 512
PACK_W = 1024
VMEM_LIMIT = 56 * 1024 * 1024

NT_DIMS = (((1,), (1,)), ((), ()))
TN_DIMS = (((0,), (0,)), ((), ()))


def _cparams(*sem):
    return pltpu.CompilerParams(dimension_semantics=sem, vmem_limit_bytes=VMEM_LIMIT)


def _rms(x, w):
    return x * lax.rsqrt(jnp.mean(x * x, axis=-1, keepdims=True) + EPS) * w


def _row_kernel(body_fn, rows, params, row_outs, acc_outs, *, name, tile=ROW_TILE):
    L = rows[0][0].shape[0]
    tile = min(tile, L)
    n_in = len(rows) + len(params)
    n_ro = len(row_outs)

    def body(*refs):
        outs = body_fn(*[r[...] for r in refs[:n_in]])
        for ref, o in zip(refs[n_in:n_in + n_ro], outs[:n_ro]):
            ref[...] = o.astype(ref.dtype)
        first = pl.program_id(0) == 0
        for ref, o in zip(refs[n_in + n_ro:], outs[n_ro:]):
            @pl.when(first)
            def _(ref=ref):
                ref[...] = jnp.zeros(ref.shape, ref.dtype)
            ref[...] += o

    in_specs = [pl.BlockSpec((tile, w), lambda i, cb=cb: (i, cb)) for _, w, cb in rows]
    in_specs += [pl.BlockSpec(p.shape, lambda i: (0, 0)) for p in params]
    out_specs = [pl.BlockSpec((tile, w), lambda i: (i, 0)) for w, _ in row_outs]
    out_specs += [pl.BlockSpec(s, lambda i: (0, 0)) for s in acc_outs]
    out_shape = [jax.ShapeDtypeStruct((L, w), dt) for w, dt in row_outs]
    out_shape += [jax.ShapeDtypeStruct(s, F32) for s in acc_outs]
    return pl.pallas_call(
        body, name=name, grid=(L // tile,), in_specs=in_specs, out_specs=out_specs, out_shape=out_shape,
        compiler_params=_cparams("arbitrary" if acc_outs else "parallel"),
    )(*[a for a, _, _ in rows], *params)


def _col_kernel(body_fn, cols, params, col_outs, par_outs, *, name, n_tiles):
    L = cols[0][0].shape[0]
    n_in = len(cols) + len(params)
    width = n_tiles * COL_TILE

    def body(*refs):
        outs = body_fn(*[r[...] for r in refs[:n_in]])
        for ref, o in zip(refs[n_in:], outs):
            ref[...] = o.astype(ref.dtype)

    in_specs = [pl.BlockSpec((L, COL_TILE), lambda j, cb=cb: (0, cb + j)) for _, cb in cols]
    in_specs += [pl.BlockSpec((p.shape[0], COL_TILE), lambda j, cb=cb: (0, cb + j)) for p, cb in params]
    out_specs = [pl.BlockSpec((L, COL_TILE), lambda j: (0, j)) for _ in col_outs]
    out_specs += [pl.BlockSpec((k, COL_TILE), lambda j: (0, j)) for k in par_outs]
    out_shape = [jax.ShapeDtypeStruct((L, width), dt) for dt in col_outs]
    out_shape += [jax.ShapeDtypeStruct((k, width), F32) for k in par_outs]
    return pl.pallas_call(
        body, name=name, grid=(n_tiles,), in_specs=in_specs, out_specs=out_specs, out_shape=out_shape,
        compiler_params=_cparams("parallel"),
    )(*[a for a, _ in cols], *[p for p, _ in params])


def _pick(n, options):
    for t in options:
        if n % t == 0:
            return t
    return n


def _matmul(a, b, *, name, nt=False, add=None, out_dtype=F32):
    M, K = a.shape
    N = b.shape[0] if nt else b.shape[1]
    tm = _pick(M, (512, 256, 128))
    tn = _pick(N, (512, 256, 128))

    def body(*refs):
        a_ref, b_ref = refs[0], refs[1]
        o_ref = refs[-1]
        if nt:
            acc = lax.dot_general(a_ref[...], b_ref[...], NT_DIMS, preferred_element_type=F32)
        else:
            acc = jnp.dot(a_ref[...], b_ref[...], preferred_element_type=F32)
        if add is not None:
            acc = acc + refs[2][...]
        o_ref[...] = acc.astype(o_ref.dtype)

    in_specs = [pl.BlockSpec((tm, K), lambda i, j: (i, 0)),
                pl.BlockSpec((tn, K), lambda i, j: (j, 0)) if nt else pl.BlockSpec((K, tn), lambda i, j: (0, j))]
    args = [a, b]
    if add is not None:
        in_specs.append(pl.BlockSpec((tm, tn), lambda i, j: (i, j)))
        args.append(add)
    return pl.pallas_call(
        body, name=name, grid=(M // tm, N // tn), in_specs=in_specs,
        out_specs=pl.BlockSpec((tm, tn), lambda i, j: (i, j)),
        out_shape=jax.ShapeDtypeStruct((M, N), out_dtype),
        compiler_params=_cparams("parallel", "parallel"),
    )(*args)


def _hgrn2_pre_fn(q_raw, ffw_raw, fbw_raw, lb_logits):
    lb = jax.nn.softmax(lb_logits, axis=0)[0:1]

    def gate(fr):
        f = lb + (1.0 - lb) * jax.nn.sigmoid(fr)
        return jnp.log(f), 1.0 - f

    lf_fw, k_fw = gate(ffw_raw)
    lf_bw, k_bw = gate(fbw_raw)
    return jax.nn.silu(q_raw), lf_fw, k_fw, lf_bw, k_bw


def _hgrn2_post_fn(o, g, norm_w):
    outs = []
    for h in range(HG_HEADS):
        sl = slice(h * HG_HEAD_DIM, (h + 1) * HG_HEAD_DIM)
        outs.append(_rms(o[:, sl], norm_w) * jax.nn.silu(g[:, sl]))
    return jnp.concatenate(outs, axis=1)


def _mamba_post_fn(y, z, norm_w):
    outs = []
    gw = D_INNER // SSM_GROUPS
    for gi in range(SSM_GROUPS):
        sl = slice(gi * gw, (gi + 1) * gw)
        outs.append(_rms(y[:, sl] * jax.nn.silu(z[:, sl]), norm_w[:, sl]))
    return jnp.concatenate(outs, axis=1)


def _shift_rows_impl(x, d):
    if d == 0:
        return x
    n = x.shape[0]
    t = lax.broadcasted_iota(jnp.int32, x.shape, 0)
    rolled = pltpu.roll(x, d % n, 0)
    return jnp.where((t >= d) if d > 0 else (t < n + d), rolled, 0.0)


@functools.partial(jax.custom_vjp, nondiff_argnums=(1,))
def _shift_rows(x, d):
    return _shift_rows_impl(x, d)


_shift_rows.defvjp(lambda x, d: (_shift_rows_impl(x, d), None), lambda d, _, g: (_shift_rows_impl(g, -d),))


def _dwconv(x, w, b):
    taps = w.shape[0]
    c = (taps - 1) // 2
    y = b + w[0:1, :] * _shift_rows(x, c)
    for k in range(1, taps):
        y = y + w[k:k + 1, :] * _shift_rows(x, c - k)
    return y


def _ffn_mid_fn(gate, val, w, b):
    return jax.nn.silu(_dwconv(gate, w, b)) * val


def _mamba_conv_fn(xbc, w, b):
    return jax.nn.silu(_dwconv(xbc, w, b))


def _gla_consts(rev):
    c = GLA_CHUNK
    r2 = lax.broadcasted_iota(jnp.int32, (c, c), 0)
    c2 = lax.broadcasted_iota(jnp.int32, (c, c), 1)
    t3 = lax.broadcasted_iota(jnp.int32, (c, c, HG_HEAD_DIM), 0)
    s3 = lax.broadcasted_iota(jnp.int32, (c, c, HG_HEAD_DIM), 1)
    if rev:
        return (r2 <= c2).astype(F32), t3 <= s3
    return (r2 >= c2).astype(F32), t3 >= s3


def _gla_chunk(st, q, k, v, g, *, rev, consts):
    tri, mask3 = consts
    c = q.shape[0]
    b = jnp.dot(tri, g, precision=HIGHEST, preferred_element_type=F32)
    o = lax.dot_general((q * jnp.exp(b)).astype(BF16), st.astype(BF16), NT_DIMS, preferred_element_type=F32)
    e = jnp.exp(jnp.where(mask3, b[:, None, :] - b[None, :, :], -jnp.inf))
    att = jnp.sum(q[:, None, :] * k[None, :, :] * e, axis=-1)
    o = o + jnp.dot(att.astype(BF16), v.astype(BF16), preferred_element_type=F32)
    b_end = b[0:1] if rev else b[c - 1:c]
    kd = (k * jnp.exp(b_end - b)).astype(BF16)
    st_new = st * jnp.exp(b_end) + lax.dot_general(v.astype(BF16), kd, TN_DIMS, preferred_element_type=F32)
    return st_new, o


def _gla_fwd(q, k, v, v_cb, g, *, rev, add, name):
    L = q.shape[0]
    blk = min(GLA_BLOCK, L)
    nblk, nsub = L // blk, blk // GLA_CHUNK
    hd, hps = HG_HEAD_DIM, GLA_HEADS_PER_STEP
    wide = hd * hps

    def body(*refs):
        q_ref, k_ref, v_ref, g_ref = refs[:4]
        add_ref = refs[4] if add is not None else None
        o_ref, st_out, st_scr = refs[-3:]
        consts = _gla_consts(rev)

        @pl.when(pl.program_id(1) == 0)
        def _():
            st_scr[...] = jnp.zeros(st_scr.shape, F32)

        st_out[...] = st_scr[...]

        def step(i, carry):
            sl = pl.ds(pl.multiple_of((nsub - 1 - i if rev else i) * GLA_CHUNK, GLA_CHUNK), GLA_CHUNK)
            for hi in range(hps):
                ln = slice(hi * hd, (hi + 1) * hd)
                st_new, o = _gla_chunk(st_scr[hi], q_ref[sl, ln], k_ref[sl, ln], v_ref[sl, ln], g_ref[sl, ln],
                                       rev=rev, consts=consts)
                st_scr[hi] = st_new
                if add_ref is not None:
                    o = o + add_ref[sl, ln]
                o_ref[sl, ln] = o
            return carry

        lax.fori_loop(0, nsub, step, 0)

    def pos(j):
        return nblk - 1 - j if rev else j

    spec = pl.BlockSpec((blk, wide), lambda h, j: (pos(j), h))
    in_specs = [spec, spec, pl.BlockSpec((blk, wide), lambda h, j: (pos(j), v_cb // hps + h)), spec]
    args = [q, k, v, g]
    if add is not None:
        in_specs.append(spec)
        args.append(add)
    return pl.pallas_call(
        body, name=name, grid=(HG_HEADS // hps, nblk), in_specs=in_specs,
        out_specs=[spec, pl.BlockSpec((hps, None, hd, hd), lambda h, j: (h, j, 0, 0))],
        out_shape=[jax.ShapeDtypeStruct((L, HG_HEADS * hd), F32),
                   jax.ShapeDtypeStruct((HG_HEADS, nblk, hd, hd), F32)],
        scratch_shapes=[pltpu.VMEM((hps, hd, hd), F32)],
        compiler_params=_cparams("parallel", "arbitrary"),
    )(*args)


def _gla_bwd(q, k, v, v_cb, g, do, states, *, rev, adds, name):
    L = q.shape[0]
    blk = min(GLA_BLOCK, L)
    nblk, nsub = L // blk, blk // GLA_CHUNK
    hd, hps = HG_HEAD_DIM, GLA_HEADS_PER_STEP
    wide = hd * hps
    n_add = 0 if adds is None else 2

    def body(*refs):
        q_ref, k_ref, v_ref, g_ref, do_ref, st_in = refs[:6]
        add_refs = refs[6:6 + n_add]
        dq_ref, dk_ref, dv_ref, dg_ref, dst_scr, sub_scr = refs[6 + n_add:]
        consts = _gla_consts(rev)
        chunk = functools.partial(_gla_chunk, rev=rev, consts=consts)

        @pl.when(pl.program_id(1) == 0)
        def _():
            dst_scr[...] = jnp.zeros(dst_scr.shape, F32)

        def rows(i):
            return pl.ds(pl.multiple_of((nsub - 1 - i if rev else i) * GLA_CHUNK, GLA_CHUNK), GLA_CHUNK)

        for hi in range(hps):
            sub_scr[hi] = st_in[hi]

        def replay(i, carry):
            sl = rows(i)
            for hi in range(hps):
                ln = slice(hi * hd, (hi + 1) * hd)
                sub_scr[(i + 1) * hps + hi] = chunk(sub_scr[i * hps + hi], q_ref[sl, ln], k_ref[sl, ln],
                                                    v_ref[sl, ln], g_ref[sl, ln])[0]
            return carry

        lax.fori_loop(0, nsub - 1, replay, 0)

        def step(n, carry):
            i = nsub - 1 - n
            sl = rows(i)
            for hi in range(hps):
                ln = slice(hi * hd, (hi + 1) * hd)
                _, vjp = jax.vjp(chunk, sub_scr[i * hps + hi], q_ref[sl, ln], k_ref[sl, ln], v_ref[sl, ln],
                                 g_ref[sl, ln])
                dst, dq, dk, dv, dg = vjp((dst_scr[hi], do_ref[sl, ln]))
                dst_scr[hi] = dst
                if n_add:
                    dq = dq + add_refs[0][sl, ln]
                    dv = dv + add_refs[1][sl, ln]
                dq_ref[sl, ln] = dq
                dk_ref[sl, ln] = dk
                dv_ref[sl, ln] = dv
                dg_ref[sl, ln] = dg
            return carry

        lax.fori_loop(0, nsub, step, 0)

    def pos(j):
        p = nblk - 1 - j
        return nblk - 1 - p if rev else p

    spec = pl.BlockSpec((blk, wide), lambda h, j: (pos(j), h))
    in_specs = [spec, spec, pl.BlockSpec((blk, wide), lambda h, j: (pos(j), v_cb // hps + h)), spec, spec,
                pl.BlockSpec((hps, None, hd, hd), lambda h, j: (h, nblk - 1 - j, 0, 0))]
    args = [q, k, v, g, do, states]
    if adds is not None:
        in_specs += [spec, spec]
        args += list(adds)
    out = jax.ShapeDtypeStruct((L, HG_HEADS * hd), F32)
    return pl.pallas_call(
        body, name=name, grid=(HG_HEADS // hps, nblk), in_specs=in_specs,
        out_specs=[spec] * 4, out_shape=[out] * 4,
        scratch_shapes=[pltpu.VMEM((hps, hd, hd), F32), pltpu.VMEM((nsub * hps, hd, hd), F32)],
        compiler_params=_cparams("parallel", "arbitrary"),
    )(*args)


def _ssd_consts(rev):
    c = SSD_CHUNK
    gw = SSM_HPG * SSM_HEAD_DIM
    r2 = lax.broadcasted_iota(jnp.int32, (c, c), 0)
    c2 = lax.broadcasted_iota(jnp.int32, (c, c), 1)
    low = (r2 <= c2) if rev else (r2 >= c2)
    lane = lax.broadcasted_iota(jnp.int32, (1, gw), 1)
    return low.astype(F32), low, r2 == c2, lane, lane >> 6


def _expand_heads(v4, head_of_lane):
    lane4 = lax.broadcasted_iota(jnp.int32, v4.shape, 1)
    out = None
    for j in range(SSM_HPG):
        col = jnp.sum(jnp.where(lane4 == j, v4, 0.0), axis=-1, keepdims=True)
        term = jnp.where(head_of_lane == j, col, 0.0)
        out = term if out is None else out + term
    return out


def _ssd_chunk(st, x, bm, cm, dtr, bias, alog, dsk, *, rev, consts, skip):
    tri, low, eye, lane, head_of_lane = consts
    c = x.shape[0]
    dt_l = _expand_heads(jax.nn.softplus(dtr + bias), head_of_lane)
    a_l = _expand_heads(-jnp.exp(alog), head_of_lane)
    acum = jnp.dot(tri, dt_l * a_l, precision=HIGHEST, preferred_element_type=F32)
    xd = x * dt_l
    cb = lax.dot_general(cm.astype(BF16), bm.astype(BF16), NT_DIMS, preferred_element_type=F32)
    y = jnp.dot(cm.astype(BF16), st.astype(BF16), preferred_element_type=F32) * jnp.exp(acum)
    for j in range(SSM_HPG):
        acol = jnp.sum(jnp.where(lane == j * SSM_HEAD_DIM, acum, 0.0), axis=-1, keepdims=True)
        ab = jnp.broadcast_to(acol, (c, c))
        arow = jnp.sum(jnp.where(eye, ab, 0.0), axis=0, keepdims=True)
        lmat = jnp.exp(jnp.where(low, ab - arow, -jnp.inf))
        xj = jnp.where(head_of_lane == j, xd, 0.0)
        y = y + jnp.dot((cb * lmat).astype(BF16), xj.astype(BF16), preferred_element_type=F32)
    a_end = acum[0:1] if rev else acum[c - 1:c]
    xdec = (xd * jnp.exp(a_end - acum)).astype(BF16)
    st_new = st * jnp.exp(a_end) + lax.dot_general(bm.astype(BF16), xdec, TN_DIMS, preferred_element_type=F32)
    if skip:
        y = y + x * _expand_heads(dsk, head_of_lane)
    return st_new, y


def _ssd_specs(L, rev):
    blk = min(SSD_BLOCK, L)
    nblk = L // blk
    gw = SSM_HPG * SSM_HEAD_DIM

    def pos(j):
        return nblk - 1 - j if rev else j

    return blk, nblk, gw, pos


def _ssd_fwd(xbc, dtr, bias, alog, dsk, *, rev, skip, add, name):
    L = xbc.shape[0]
    blk, nblk, gw, pos = _ssd_specs(L, rev)
    nsub = blk // SSD_CHUNK
    n = SSM_STATE

    def body(*refs):
        x_ref, b_ref, c_ref, dt_ref, bias_ref, alog_ref, dsk_ref = refs[:7]
        add_ref = refs[7] if add is not None else None
        y_ref, st_out, st_scr = refs[-3:]
        consts = _ssd_consts(rev)

        @pl.when(pl.program_id(1) == 0)
        def _():
            st_scr[...] = jnp.zeros(st_scr.shape, F32)

        st_out[...] = st_scr[...]

        def step(i, carry):
            sl = pl.ds(pl.multiple_of((nsub - 1 - i if rev else i) * SSD_CHUNK, SSD_CHUNK), SSD_CHUNK)
            st_new, y = _ssd_chunk(st_scr[...], x_ref[sl, :], b_ref[sl, :], c_ref[sl, :], dt_ref[sl, :],
                                   bias_ref[...], alog_ref[...], dsk_ref[...], rev=rev, consts=consts, skip=skip)
            st_scr[...] = st_new
            if add_ref is not None:
                y = y + add_ref[sl, :]
            y_ref[sl, :] = y
            return carry

        lax.fori_loop(0, nsub, step, 0)

    b0 = D_INNER // n
    yspec = pl.BlockSpec((blk, gw), lambda g, j: (pos(j), g))
    pspec = pl.BlockSpec((None, 1, SSM_HPG), lambda g, j: (g, 0, 0))
    in_specs = [yspec,
                pl.BlockSpec((blk, n), lambda g, j: (pos(j), b0 + g)),
                pl.BlockSpec((blk, n), lambda g, j: (pos(j), b0 + SSM_GROUPS + g)),
                pl.BlockSpec((None, blk, SSM_HPG), lambda g, j: (g, pos(j), 0)),
                pspec, pspec, pspec]
    args = [xbc, xbc, xbc, dtr, bias, alog, dsk]
    if add is not None:
        in_specs.append(yspec)
        args.append(add)
    return pl.pallas_call(
        body, name=name, grid=(SSM_GROUPS, nblk), in_specs=in_specs,
        out_specs=[yspec, pl.BlockSpec((None, None, n, gw), lambda g, j: (g, j, 0, 0))],
        out_shape=[jax.ShapeDtypeStruct((L, D_INNER), F32),
                   jax.ShapeDtypeStruct((SSM_GROUPS, nblk, n, gw), F32)],
        scratch_shapes=[pltpu.VMEM((n, gw), F32)],
        compiler_params=_cparams("parallel", "arbitrary"),
    )(*args)


def _ssd_bwd(xbc, dtr, bias, alog, dsk, dy, states, *, rev, skip, adds, name):
    L = xbc.shape[0]
    blk, nblk, gw, _ = _ssd_specs(L, rev)
    nsub = blk // SSD_CHUNK
    n = SSM_STATE
    n_add = 0 if adds is None else 3

    def body(*refs):
        x_ref, b_ref, c_ref, dt_ref, bias_ref, alog_ref, dsk_ref, dy_ref, st_in = refs[:9]
        add_refs = refs[9:9 + n_add]
        dx_ref, db_ref, dc_ref, ddt_ref, dbias_ref, dalog_ref, ddsk_ref, dst_scr, sub_scr = refs[9 + n_add:]
        consts = _ssd_consts(rev)
        chunk = functools.partial(_ssd_chunk, rev=rev, consts=consts, skip=skip)

        @pl.when(pl.program_id(1) == 0)
        def _():
            dst_scr[...] = jnp.zeros(dst_scr.shape, F32)
            dbias_ref[...] = jnp.zeros(dbias_ref.shape, F32)
            dalog_ref[...] = jnp.zeros(dalog_ref.shape, F32)
            ddsk_ref[...] = jnp.zeros(ddsk_ref.shape, F32)

        def rows(i):
            return pl.ds(pl.multiple_of((nsub - 1 - i if rev else i) * SSD_CHUNK, SSD_CHUNK), SSD_CHUNK)

        def operands(i):
            sl = rows(i)
            return (x_ref[sl, :], b_ref[sl, :], c_ref[sl, :], dt_ref[sl, :], bias_ref[...], alog_ref[...], dsk_ref[...])

        def replay(i, st):
            sub_scr[i] = st
            return chunk(st, *operands(i))[0]

        lax.fori_loop(0, nsub, replay, st_in[...])

        def step(k, carry):
            i = nsub - 1 - k
            sl = rows(i)
            _, vjp = jax.vjp(chunk, sub_scr[i], *operands(i))
            dst, dx, db, dc, ddt, dbias, dalog, ddsk = vjp((dst_scr[...], dy_ref[sl, :]))
            dst_scr[...] = dst
            if n_add:
                dx = dx + add_refs[0][sl, :]
                db = db + add_refs[1][sl, :]
                dc = dc + add_refs[2][sl, :]
            dx_ref[sl, :] = dx
            db_ref[sl, :] = db
            dc_ref[sl, :] = dc
            ddt_ref[sl, :] = ddt
            dbias_ref[...] += dbias
            dalog_ref[...] += dalog
            ddsk_ref[...] += ddsk
            return carry

        lax.fori_loop(0, nsub, step, 0)

    def pos(j):
        p = nblk - 1 - j
        return nblk - 1 - p if rev else p

    b0 = D_INNER // n
    xspec = pl.BlockSpec((blk, gw), lambda g, j: (pos(j), g))
    nspec = pl.BlockSpec((blk, n), lambda g, j: (pos(j), g))
    dtspec = pl.BlockSpec((None, blk, SSM_HPG), lambda g, j: (g, pos(j), 0))
    pspec = pl.BlockSpec((None, 1, SSM_HPG), lambda g, j: (g, 0, 0))
    in_specs = [xspec,
                pl.BlockSpec((blk, n), lambda g, j: (pos(j), b0 + g)),
                pl.BlockSpec((blk, n), lambda g, j: (pos(j), b0 + SSM_GROUPS + g)),
                dtspec, pspec, pspec, pspec, xspec,
                pl.BlockSpec((None, None, n, gw), lambda g, j: (g, nblk - 1 - j, 0, 0))]
    args = [xbc, xbc, xbc, dtr, bias, alog, dsk, dy, states]
    if adds is not None:
        in_specs += [xspec, nspec, nspec]
        args += list(adds)
    par = jax.ShapeDtypeStruct((SSM_GROUPS, 1, SSM_HPG), F32)
    return pl.pallas_call(
        body, name=name, grid=(SSM_GROUPS, nblk), in_specs=in_specs,
        out_specs=[xspec, nspec, nspec, dtspec, pspec, pspec, pspec],
        out_shape=[jax.ShapeDtypeStruct((L, D_INNER), F32), jax.ShapeDtypeStruct((L, GN), F32),
                   jax.ShapeDtypeStruct((L, GN), F32), jax.ShapeDtypeStruct((SSM_GROUPS, L, SSM_HPG), F32),
                   par, par, par],
        scratch_shapes=[pltpu.VMEM((n, gw), F32), pltpu.VMEM((nsub, n, gw), F32)],
        compiler_params=_cparams("parallel", "arbitrary"),
    )(*args)


def _norm_fwd(h, w, name):
    d = h.shape[1]
    return _row_kernel(lambda hv, wv: (_rms(hv, wv),), [(h, d, 0)], [w], [(d, BF16)], [], name=name)[0]


def _norm_bwd(h, w, du, dh_in, name):
    d = h.shape[1]

    def fn(hv, duv, dhv, wv):
        _, vjp = jax.vjp(_rms, hv, wv)
        dh, dw = vjp(duv.astype(F32))
        dh = dh + dhv
        return dh, dh, dw

    return _row_kernel(fn, [(h, d, 0), (du, d, 0), (dh_in, d, 0)], [w], [(d, F32), (d, BF16)], [(1, d)], name=name)


def _ffn_fwd(h, norm_w, w_gate, w_val, w_out, conv_w, conv_b, tag):
    u = _norm_fwd(h, norm_w, f"{tag}_norm")
    pg = _matmul(u, w_gate, name=f"{tag}_in_gate")
    pv = _matmul(u, w_val, name=f"{tag}_in_val")
    y = _col_kernel(lambda g, v, w, b: (_ffn_mid_fn(g, v, w, b),), [(pg, 0), (pv, 0)], [(conv_w, 0), (conv_b, 0)],
                    [BF16], [], name=f"{tag}_mid", n_tiles=D_FF // COL_TILE)[0]
    h_out = _matmul(y, w_out, add=h, name=f"{tag}_out")
    return h_out, (u, pg, pv, y)


def _ffn_bwd(h, dh, dh16, saved, norm_w, w_gate, w_val, w_out, conv_w, conv_b, tag):
    u, pg, pv, y = saved
    dy = _matmul(dh16, w_out, nt=True, name=f"{tag}_out_dx")
    dw_out = _matmul(y.T, dh16, name=f"{tag}_out_dw")

    def fn(g, v, ct, w, b):
        _, vjp = jax.vjp(_ffn_mid_fn, g, v, w, b)
        return vjp(ct)

    dpg, dpv, dcw, dcb = _col_kernel(fn, [(pg, 0), (pv, 0), (dy, 0)], [(conv_w, 0), (conv_b, 0)], [BF16, BF16],
                                     [FFN_CONV, 1], name=f"{tag}_mid_bwd", n_tiles=D_FF // COL_TILE)
    du = _matmul(dpg, w_gate, nt=True, name=f"{tag}_gate_dx")
    du = _matmul(dpv, w_val, nt=True, add=du, out_dtype=BF16, name=f"{tag}_val_dx")
    ut = u.T
    dw_gate = _matmul(ut, dpg, name=f"{tag}_gate_dw")
    dw_val = _matmul(ut, dpv, name=f"{tag}_val_dw")
    dh, dh16, dnw = _norm_bwd(h, norm_w, du, dh, f"{tag}_norm_bwd")
    return dh, dh16, dict(w_in=jnp.concatenate([dw_gate, dw_val], axis=1), w_out=dw_out, conv_w=dcw, conv_b=dcb, norm=dnw)


def _hgrn2_fwd(h, norm_w, w_in, lb_logits, a_norm_w, w_out):
    d = D_MODEL
    u = _norm_fwd(h, norm_w, "hg_norm")
    pa = _matmul(u, w_in, name="hg_in")
    qs, lf_fw, k_fw, lf_bw, k_bw = _row_kernel(
        _hgrn2_pre_fn, [(pa, d, 0), (pa, d, 1), (pa, d, 2)], [lb_logits], [(d, F32)] * 5, [], name="hg_pre")
    o_fw, st_fw = _gla_fwd(qs, k_fw, pa, 3 * HG_HEADS, lf_fw, rev=False, add=None, name="hg_gla_fw")
    o, st_bw = _gla_fwd(qs, k_bw, pa, 3 * HG_HEADS, lf_bw, rev=True, add=o_fw, name="hg_gla_bw")
    y = _row_kernel(lambda ov, gv, wv: (_hgrn2_post_fn(ov, gv, wv),), [(o, d, 0), (pa, d, 4)], [a_norm_w],
                    [(d, BF16)], [], name="hg_post")[0]
    h_out = _matmul(y, w_out, add=h, name="hg_out")
    return h_out, (u, pa, qs, lf_fw, k_fw, lf_bw, k_bw, st_fw, st_bw, o, y)


def _hgrn2_bwd(h, dh, dh16, saved, norm_w, w_in, lb_logits, a_norm_w, w_out):
    d = D_MODEL
    u, pa, qs, lf_fw, k_fw, lf_bw, k_bw, st_fw, st_bw, o, y = saved
    dy = _matmul(dh16, w_out, nt=True, name="hg_out_dx")
    dw_out = _matmul(y.T, dh16, name="hg_out_dw")

    def post_bwd(ov, gv, ct, wv):
        _, vjp = jax.vjp(_hgrn2_post_fn, ov, gv, wv)
        return vjp(ct)

    do, dg, dnw = _row_kernel(post_bwd, [(o, d, 0), (pa, d, 4), (dy, d, 0)], [a_norm_w], [(d, F32), (d, F32)],
                              [(1, HG_HEAD_DIM)], name="hg_post_bwd")
    dq1, dk_fw, dv1, dlf_fw = _gla_bwd(qs, k_fw, pa, 3 * HG_HEADS, lf_fw, do, st_fw, rev=False, adds=None,
                                       name="hg_gla_fw_bwd")
    dqs, dk_bw, dv, dlf_bw = _gla_bwd(qs, k_bw, pa, 3 * HG_HEADS, lf_bw, do, st_bw, rev=True, adds=(dq1, dv1),
                                      name="hg_gla_bw_bwd")

    def pre_bwd(qr, fr, br, c0, c1, c2, c3, c4, dvv, dgv, lbl):
        _, vjp = jax.vjp(_hgrn2_pre_fn, qr, fr, br, lbl)
        dq, df, db, dlbl = vjp((c0, c1, c2, c3, c4))
        return jnp.concatenate([dq, df, db, dvv, dgv], axis=1), dlbl

    rows = [(pa, d, 0), (pa, d, 1), (pa, d, 2), (dqs, d, 0), (dlf_fw, d, 0), (dk_fw, d, 0), (dlf_bw, d, 0),
            (dk_bw, d, 0), (dv, d, 0), (dg, d, 0)]
    dpa, dlbl = _row_kernel(pre_bwd, rows, [lb_logits], [(5 * d, BF16)], [lb_logits.shape], name="hg_pre_bwd")
    du = _matmul(dpa, w_in, nt=True, out_dtype=BF16, name="hg_in_dx")
    dw_in = _matmul(u.T, dpa, name="hg_in_dw")
    dh, dh16, dn1 = _norm_bwd(h, norm_w, du, dh, "hg_norm_bwd")
    return dh, dh16, dict(w_in=dw_in, w_out=dw_out, lb=dlbl, a_norm=dnw, norm=dn1)


def _group_params(p):
    return p.reshape(SSM_GROUPS, 1, SSM_HPG)


def _mamba_fwd(h, norm_w, w_z, w_xbc, w_dt, conv_w, conv_b, dt_bias, a_log, d_skip, b_norm_w, w_out):
    L = h.shape[0]
    u = _norm_fwd(h, norm_w, "mb_norm")
    z = _matmul(u, w_z, name="mb_in_z")
    xbc_raw = _matmul(u, w_xbc, name="mb_in_xbc")
    dt_raw = _matmul(u, w_dt, name="mb_in_dt")
    xbc = _col_kernel(lambda xv, w, b: (_mamba_conv_fn(xv, w, b),), [(xbc_raw, 0)], [(conv_w, 0), (conv_b, 0)],
                      [F32], [], name="mb_conv", n_tiles=CONV_DIM // COL_TILE)[0]
    dtr = dt_raw.reshape(L, 2, SSM_GROUPS, SSM_HPG).transpose(1, 2, 0, 3)
    bias, alog = dt_bias.reshape(2, -1), a_log.reshape(2, -1)
    dsk = _group_params(d_skip.reshape(-1))
    y_fw, st_fw = _ssd_fwd(xbc, dtr[0], _group_params(bias[0]), _group_params(alog[0]), dsk, rev=False, skip=True,
                           add=None, name="mb_ssd_fw")
    ysum, st_bw = _ssd_fwd(xbc, dtr[1], _group_params(bias[1]), _group_params(alog[1]), dsk, rev=True, skip=False,
                           add=y_fw, name="mb_ssd_bw")
    y = _row_kernel(lambda yv, zv, wv: (_mamba_post_fn(yv, zv, wv),), [(ysum, D_INNER, 0), (z, D_INNER, 0)],
                    [b_norm_w], [(D_INNER, BF16)], [], name="mb_post")[0]
    h_out = _matmul(y, w_out, add=h, name="mb_out")
    return h_out, (u, z, xbc_raw, xbc, dtr, st_fw, st_bw, ysum, y)


def _mamba_bwd(h, dh, dh16, saved, norm_w, w_z, w_xbc, w_dt, conv_w, conv_b, dt_bias, a_log, d_skip, b_norm_w, w_out):
    L = h.shape[0]
    u, z, xbc_raw, xbc, dtr, st_fw, st_bw, ysum, y = saved
    dy = _matmul(dh16, w_out, nt=True, name="mb_out_dx")
    dw_out = _matmul(y.T, dh16, name="mb_out_dw")

    def post_bwd(yv, zv, ct, wv):
        _, vjp = jax.vjp(_mamba_post_fn, yv, zv, wv)
        return vjp(ct)

    dys, dz, dbn = _row_kernel(post_bwd, [(ysum, D_INNER, 0), (z, D_INNER, 0), (dy, D_INNER, 0)], [b_norm_w],
                               [(D_INNER, F32), (D_INNER, BF16)], [(1, D_INNER)], name="mb_post_bwd")
    bias, alog = dt_bias.reshape(2, -1), a_log.reshape(2, -1)
    dsk = _group_params(d_skip.reshape(-1))
    dx1, db1, dc1, ddt_fw, dbias_fw, dalog_fw, ddsk = _ssd_bwd(
        xbc, dtr[0], _group_params(bias[0]), _group_params(alog[0]), dsk, dys, st_fw, rev=False, skip=True,
        adds=None, name="mb_ssd_fw_bwd")
    dx, db, dc, ddt_bw, dbias_bw, dalog_bw, _ = _ssd_bwd(
        xbc, dtr[1], _group_params(bias[1]), _group_params(alog[1]), dsk, dys, st_bw, rev=True, skip=False,
        adds=(dx1, db1, dc1), name="mb_ssd_bw_bwd")

    def conv_bwd(n_tiles, ct, first):
        def fn(xv, ctv, w, b):
            _, vjp = jax.vjp(_mamba_conv_fn, xv, w, b)
            return vjp(ctv)
        return _col_kernel(fn, [(xbc_raw, first), (ct, 0)], [(conv_w, first), (conv_b, first)], [BF16],
                           [SSM_CONV, 1], name=f"mb_conv_bwd_{first}", n_tiles=n_tiles)

    nx, nb = D_INNER // COL_TILE, GN // COL_TILE
    parts = [conv_bwd(nx, dx, 0), conv_bwd(nb, db, nx), conv_bwd(nb, dc, nx + nb)]
    dxbc = jnp.concatenate([p[0] for p in parts], axis=1)
    dcw = jnp.concatenate([p[1] for p in parts], axis=1)
    dcb = jnp.concatenate([p[2] for p in parts], axis=1)
    ddt = jnp.stack([ddt_fw, ddt_bw]).transpose(2, 0, 1, 3).reshape(L, 2 * SSM_GROUPS * SSM_HPG).astype(BF16)
    du = _matmul(dz, w_z, nt=True, name="mb_z_dx")
    du = _matmul(dxbc, w_xbc, nt=True, add=du, name="mb_xbc_dx")
    du = _matmul(ddt, w_dt, nt=True, add=du, out_dtype=BF16, name="mb_dt_dx")
    ut = u.T
    dw_in = jnp.concatenate([_matmul(ut, dz, name="mb_z_dw"), _matmul(ut, dxbc, name="mb_xbc_dw"),
                             _matmul(ut, ddt, name="mb_dt_dw")], axis=1)
    dh, dh16, dn1 = _norm_bwd(h, norm_w, du, dh, "mb_norm_bwd")
    grads = dict(w_in=dw_in, w_out=dw_out, conv_w=dcw, conv_b=dcb, b_norm=dbn, norm=dn1,
                 dt_bias=jnp.stack([dbias_fw, dbias_bw]).reshape(1, 2, -1),
                 a_log=jnp.stack([dalog_fw, dalog_bw]).reshape(1, 2, -1), d_skip=ddsk.reshape(1, -1))
    return dh, dh16, grads


def _loss_head(h, target, w):
    d = h.shape[1]

    def fn(hv, tv, wv):
        def loss(hv, wv):
            err = _rms(hv, wv) - tv
            return 0.5 * jnp.sum(jnp.mean(err * err, axis=-1, keepdims=True), axis=0, keepdims=True)
        val, vjp = jax.vjp(loss, hv, wv)
        dh, dw = vjp(jnp.ones((1, 1), F32))
        return dh, dh, val, dw

    return _row_kernel(fn, [(h, d, 0), (target, d, 0)], [w], [(d, F32), (d, BF16)], [(1, 1), (1, d)], name="loss_head")


def _local_step(x, target, w):
    g_w, v_w = w["ffn_w_in"][:, :, :D_FF], w["ffn_w_in"][:, :, D_FF:]
    wz, wxbc, wdt = (w["b_w_in"][0, :, :D_INNER], w["b_w_in"][0, :, D_INNER:D_INNER + CONV_DIM],
                     w["b_w_in"][0, :, D_INNER + CONV_DIM:])
    ffn = [(w["norm2_w"][i:i + 1], g_w[i], v_w[i], w["ffn_w_out"][i], w["ffn_conv_w"][i], w["ffn_conv_b"][i:i + 1])
           for i in range(2)]
    hg = (w["norm1_w"][0:1], w["a_w_in"][0], w["a_lb_logits"], w["a_norm_w"], w["a_w_out"][0])
    mb = (w["norm1_w"][1:2], wz, wxbc, wdt, w["b_conv_w"][0], w["b_conv_b"], w["b_dt_bias"], w["b_a_log"],
          w["b_d_skip"], w["b_norm_w"], w["b_w_out"][0])

    h0 = x
    h1, s_hg = _hgrn2_fwd(h0, *hg)
    h2, s_f0 = _ffn_fwd(h1, *ffn[0], "ffn0")
    h3, s_mb = _mamba_fwd(h2, *mb)
    h4, s_f1 = _ffn_fwd(h3, *ffn[1], "ffn1")
    dh, dh16, loss, d_final = _loss_head(h4, target, w["final_norm_w"].reshape(1, -1))
    dh, dh16, g_f1 = _ffn_bwd(h3, dh, dh16, s_f1, *ffn[1], "ffn1")
    dh, dh16, g_mb = _mamba_bwd(h2, dh, dh16, s_mb, *mb)
    dh, dh16, g_f0 = _ffn_bwd(h1, dh, dh16, s_f0, *ffn[0], "ffn0")
    dh, dh16, g_hg = _hgrn2_bwd(h0, dh, dh16, s_hg, *hg)
    grads = dict(
        norm1_w=jnp.concatenate([g_hg["norm"], g_mb["norm"]], axis=0),
        norm2_w=jnp.concatenate([g_f0["norm"], g_f1["norm"]], axis=0),
        a_w_in=g_hg["w_in"][None], a_lb_logits=g_hg["lb"], a_norm_w=g_hg["a_norm"], a_w_out=g_hg["w_out"][None],
        b_w_in=g_mb["w_in"][None], b_conv_w=g_mb["conv_w"][None], b_conv_b=g_mb["conv_b"],
        b_dt_bias=g_mb["dt_bias"], b_a_log=g_mb["a_log"], b_d_skip=g_mb["d_skip"], b_norm_w=g_mb["b_norm"],
        b_w_out=g_mb["w_out"][None],
        ffn_w_in=jnp.stack([g_f0["w_in"], g_f1["w_in"]]), ffn_conv_w=jnp.stack([g_f0["conv_w"], g_f1["conv_w"]]),
        ffn_conv_b=jnp.concatenate([g_f0["conv_b"], g_f1["conv_b"]], axis=0),
        ffn_w_out=jnp.stack([g_f0["w_out"], g_f1["w_out"]]), final_norm_w=d_final.reshape(-1),
    )
    return loss, dh, grads


def _mesh_pos():
    return lax.axis_index("x"), lax.axis_index("y"), lax.axis_index("c")


N_COPIES = N_DEV - 1


def _comm_call(body, ins, out_shape, name):
    n = len(ins)
    hbm = pl.BlockSpec(memory_space=pl.ANY)
    return pl.pallas_call(
        body, name=name, out_shape=out_shape, in_specs=[hbm] * n, out_specs=[hbm] * n,
        scratch_shapes=[pltpu.SemaphoreType.DMA((n * N_COPIES,)), pltpu.SemaphoreType.DMA((n * N_COPIES,)),
                        pltpu.SemaphoreType.DMA((n,))],
    )(*ins)


def _all_gather(shards, name):
    n = len(shards)

    def body(*refs):
        x_refs, out_refs = refs[:n], refs[n:2 * n]
        send_sems, recv_sems, local_sems = refs[2 * n:]
        x, y, c = _mesh_pos()
        me, sibling = (x, y, c), (x, y, 1 - c)
        chips = [(1 - x, y), (x, 1 - y), (1 - x, 1 - y)]

        def copy(w, k, block, to, own=False):
            px, py, pc = block
            dst = out_refs[w].at[4 * px + 2 * py + pc]
            return pltpu.make_async_remote_copy(
                src_ref=x_refs[w] if own else dst, dst_ref=dst, send_sem=send_sems.at[w * N_COPIES + k],
                recv_sem=recv_sems.at[w * N_COPIES + k], device_id=to, device_id_type=MESH)

        mine = [pltpu.make_async_copy(x_refs[w], out_refs[w].at[4 * x + 2 * y + c], local_sems.at[w]) for w in range(n)]
        for cp in mine:
            cp.start()
        first = [copy(w, 1 + j, me, (*chip, c), own=True) for j, chip in enumerate(chips) for w in range(n)]
        first += [copy(w, 0, me, sibling, own=True) for w in range(n)]
        for cp in first:
            cp.start()
        passed = []
        for j, chip in enumerate(chips):
            for w in range(n):
                copy(w, 1 + j, (*chip, c), me).wait_recv()
                passed.append(copy(w, 4 + j, (*chip, c), sibling))
                passed[-1].start()
        for w in range(n):
            copy(w, 0, sibling, me).wait_recv()
        for j, chip in enumerate(chips):
            for w in range(n):
                copy(w, 4 + j, (*chip, 1 - c), me).wait_recv()
        for cp in first + passed:
            cp.wait_send()
        for cp in mine:
            cp.wait()

    out_shape = [jax.ShapeDtypeStruct((N_DEV, *s.shape), s.dtype) for s in shards]
    return _comm_call(body, shards, out_shape, name)


def _exchange_parts(parts, name):
    n = len(parts)

    def body(*refs):
        p_refs, got_refs = refs[:n], refs[n:2 * n]
        send_sems, recv_sems, local_sems = refs[2 * n:]
        x, y, c = _mesh_pos()
        me = 4 * x + 2 * y + c
        mine = [pltpu.make_async_copy(p_refs[w].at[me], got_refs[w].at[me], local_sems.at[w]) for w in range(n)]
        for cp in mine:
            cp.start()
        copies = []
        for k in (4, 2, 6, 5, 3, 7, 1):
            px = 1 - x if k & 4 else x
            py = 1 - y if k & 2 else y
            pc = 1 - c if k & 1 else c
            for w in range(n):
                cp = pltpu.make_async_remote_copy(
                    src_ref=p_refs[w].at[4 * px + 2 * py + pc], dst_ref=got_refs[w].at[me],
                    send_sem=send_sems.at[w * N_COPIES + k - 1], recv_sem=recv_sems.at[w * N_COPIES + k - 1],
                    device_id=(px, py, pc), device_id_type=MESH)
                cp.start()
                copies.append(cp)
        for cp in copies:
            cp.wait()
        for cp in mine:
            cp.wait()

    out_shape = [jax.ShapeDtypeStruct(p.shape, p.dtype) for p in parts]
    return _comm_call(body, parts, out_shape, name)


def _adamw(got, w, m, v, *, name, tile):
    rows, width = w.shape
    c1 = 1.0 / (1.0 - ADAM_B1 ** ADAM_STEP)
    c2 = 1.0 / (1.0 - ADAM_B2 ** ADAM_STEP)

    def body(got_ref, w_ref, m_ref, v_ref, g_ref, d_ref, nm_ref, nv_ref):
        g = got_ref[0]
        for s in range(1, N_DEV):
            g = g + got_ref[s]
        m_new = ADAM_B1 * m_ref[...] + (1.0 - ADAM_B1) * g
        v_new = ADAM_B2 * v_ref[...] + (1.0 - ADAM_B2) * (g * g)
        g_ref[...] = g
        nm_ref[...] = m_new
        nv_ref[...] = v_new
        d_ref[...] = -ADAM_LR * ((m_new * c1) / (jnp.sqrt(v_new * c2) + ADAM_EPS) + ADAM_WD * w_ref[...])

    spec = pl.BlockSpec((tile, width), lambda i: (i, 0))
    out = jax.ShapeDtypeStruct((rows, width), F32)
    return pl.pallas_call(
        body, name=name, grid=(rows // tile,),
        in_specs=[pl.BlockSpec((N_DEV, tile, width), lambda i: (0, i, 0)), spec, spec, spec],
        out_specs=[spec] * 4, out_shape=[out] * 4, compiler_params=_cparams("parallel"),
    )(got, w, m, v)


SHARDED = dict(a_w_in=(2, True), a_w_out=(1, True), b_w_in=(2, True), b_w_out=(1, True), ffn_w_in=(2, True),
               ffn_w_out=(1, True), b_conv_w=(2, False), b_conv_b=(1, False), b_norm_w=(1, False), ffn_conv_w=(2, False))
REPLICATED = ("norm1_w", "norm2_w", "a_lb_logits", "a_norm_w", "b_dt_bias", "b_a_log", "b_d_skip", "ffn_conv_b",
              "final_norm_w")
WEIGHTS = ("norm1_w", "norm2_w", "a_w_in", "a_lb_logits", "a_norm_w", "a_w_out", "b_w_in", "b_conv_w", "b_conv_b",
           "b_dt_bias", "b_a_log", "b_d_skip", "b_norm_w", "b_w_out", "ffn_w_in", "ffn_conv_w", "ffn_conv_b",
           "ffn_w_out", "final_norm_w")


def _pad_rows(flat, multiple):
    n = flat.shape[-1]
    per = PACK_W * multiple
    total = -(-n // per) * per
    flat = jnp.pad(flat, [(0, 0)] * (flat.ndim - 1) + [(0, total - n)])
    return flat.reshape(*flat.shape[:-1], total // PACK_W, PACK_W)


def _to_parts(full, axis):
    shp = full.shape
    t = full.reshape(*shp[:axis], N_DEV, shp[axis] // N_DEV, *shp[axis + 1:])
    return jnp.moveaxis(t, axis, 0)


def _from_parts(parts, axis):
    t = jnp.moveaxis(parts, 0, axis)
    shp = t.shape
    return t.reshape(*shp[:axis], shp[axis] * shp[axis + 1], *shp[axis + 2:])


BIG = tuple(n for n, (_, mm) in SHARDED.items() if mm)
SMALL = tuple(n for n, (_, mm) in SHARDED.items() if not mm)
SMALL_W = 512


def _small_rows(tree, lead):
    rows = []
    for n in SMALL:
        t = tree[n]
        t = t.reshape(*lead, -1, t.shape[-1])
        rows.append(jnp.pad(t, [(0, 0)] * (t.ndim - 1) + [(0, SMALL_W - t.shape[-1])]))
    buf = jnp.concatenate(rows, axis=-2)
    return jnp.pad(buf, [(0, 0)] * (buf.ndim - 2) + [(0, 16 - buf.shape[-2]), (0, 0)])


def _small_unrows(buf, like, lead):
    out, r = {}, 0
    for n in SMALL:
        shp = like[n].shape
        k = like[n].size // shp[-1]
        out[n] = buf[..., r:r + k, :shp[-1]].reshape(*lead, *shp)
        r += k
    return out


def _gather_weights(local):
    shards = [local[n].astype(BF16) for n in BIG] + [_small_rows(local, ())]
    got = _all_gather(shards, "gather_weights")
    whole = {n: _from_parts(g, SHARDED[n][0]) for n, g in zip(BIG, got)}
    small = _small_unrows(got[-1], local, (N_DEV,))
    whole.update({n: _from_parts(small[n], SHARDED[n][0]) for n in SMALL})
    return whole


def _pack_small(tree, extra):
    flat = jnp.concatenate([tree[n].reshape(-1) for n in REPLICATED] + [extra.reshape(-1)])
    return _pad_rows(flat, 8)


def _unpack_small(pack, like):
    flat, out, off = pack.reshape(-1), {}, 0
    for n in REPLICATED:
        out[n] = flat[off:off + like[n].size].reshape(like[n].shape)
        off += like[n].size
    return out, flat[off]


def kernel(x, norm1_w, norm2_w, a_w_in, a_lb_logits, a_norm_w, a_w_out, b_w_in, b_conv_w, b_conv_b, b_dt_bias, b_a_log, b_d_skip, b_norm_w, b_w_out, ffn_w_in, ffn_conv_w, ffn_conv_b, ffn_w_out, final_norm_w, loss_target, m_norm1_w, m_norm2_w, m_a_w_in, m_a_lb_logits, m_a_norm_w, m_a_w_out, m_b_w_in, m_b_conv_w, m_b_conv_b, m_b_dt_bias, m_b_a_log, m_b_d_skip, m_b_norm_w, m_b_w_out, m_ffn_w_in, m_ffn_conv_w, m_ffn_conv_b, m_ffn_w_out, m_final_norm_w, v_norm1_w, v_norm2_w, v_a_w_in, v_a_lb_logits, v_a_norm_w, v_a_w_out, v_b_w_in, v_b_conv_w, v_b_conv_b, v_b_dt_bias, v_b_a_log, v_b_d_skip, v_b_norm_w, v_b_w_out, v_ffn_w_in, v_ffn_conv_w, v_ffn_conv_b, v_ffn_w_out, v_final_norm_w):
    given = dict(locals())
    local = {n: given[n] for n in WEIGHTS}
    mom = {n: given["m_" + n] for n in WEIGHTS}
    var = {n: given["v_" + n] for n in WEIGHTS}

    whole = dict(local)
    whole.update(_gather_weights(local))
    loss, grad_x, grads = _local_step(x[0], loss_target[0], whole)

    parts = {n: _to_parts(grads[n], SHARDED[n][0]) for n in SHARDED}
    sent = [parts[n].reshape(N_DEV, -1, parts[n].shape[-1]) for n in BIG] + [_small_rows(parts, (N_DEV,))]
    got = _exchange_parts(sent, "exchange_grads")
    out_g, out_d, out_m, out_v = {}, {}, {}, {}

    def update(got_n, trees, shapes, name):
        w2, m2, v2 = (t.reshape(got_n.shape[1:]) for t in trees)
        res = _adamw(got_n, w2, m2, v2, name=name, tile=_pick(got_n.shape[1], (256, 176, 128, 16)))
        return [r.reshape(shapes) for r in res]

    for n, got_n in zip(BIG, got):
        out_g[n], out_d[n], out_m[n], out_v[n] = update(got_n, (local[n], mom[n], var[n]), local[n].shape, "adamw_" + n)
    res = update(got[-1], [_small_rows(t, ()) for t in (local, mom, var)], (16, SMALL_W), "adamw_small")
    for out, r in zip((out_g, out_d, out_m, out_v), res):
        out.update(_small_unrows(r, local, ()))

    small = _pack_small(grads, loss)
    rows = small.shape[0]
    got_s = _all_gather([small], "gather_small")[0]
    zero = jnp.zeros((1,), F32)
    g, dlt, nm, nv = _adamw(got_s, _pack_small(local, zero), _pack_small(mom, zero), _pack_small(var, zero),
                            name="adamw_replicated", tile=rows)
    (rep_g, total), (rep_d, _), (rep_m, _), (rep_v, _) = (_unpack_small(t, local) for t in (g, dlt, nm, nv))
    out_g.update(rep_g)
    out_d.update(rep_d)
    out_m.update(rep_m)
    out_v.update(rep_v)

    return (total, grad_x[None], *[out_g[n] for n in WEIGHTS], *[out_d[n] for n in WEIGHTS],
            *[out_m[n] for n in WEIGHTS], *[out_v[n] for n in WEIGHTS])
```

```python
import functools

import jax
import jax.numpy as jnp
from jax import lax
from jax.experimental import pallas as pl
from jax.experimental.pallas import tpu as pltpu

F32 = jnp.float32
BF16 = jnp.bfloat16
HIGHEST = lax.Precision.HIGHEST
MESH = pl.DeviceIdType.MESH

N_DEV = 8
EPS = 1e-6
D_MODEL = 1024
HG_HEADS = 8
HG_HEAD_DIM = 128
D_INNER = 2048
SSM_HEAD_DIM = 64
SSM_GROUPS = 8
SSM_HPG = 4
SSM_STATE = 128
GN = SSM_GROUPS * SSM_STATE
CONV_DIM = D_INNER + 2 * GN
D_FF = 2816
SSM_CONV = 5
FFN_CONV = 3

ADAM_LR = 0.001
ADAM_B1 = 0.9
ADAM_B2 = 0.999
ADAM_EPS = 1e-08
ADAM_WD = 0.01
ADAM_STEP = 10

LANES = 128
ROW_TILE = 256
COL_TILE = 128
GLA_CHUNK = 16
GLA_BLOCK = 512
GLA_HEADS_PER_STEP = 4
SSD_CHUNK = 128
SSD_BLOCK = 512
PACK_W = 1024
VMEM_LIMIT = 56 * 1024 * 1024

NT_DIMS = (((1,), (1,)), ((), ()))
TN_DIMS = (((0,), (0,)), ((), ()))


def _cparams(*sem):
    return pltpu.CompilerParams(dimension_semantics=sem, vmem_limit_bytes=VMEM_LIMIT)


def _rms(x, w):
    return x * lax.rsqrt(jnp.mean(x * x, axis=-1, keepdims=True) + EPS) * w


def _row_kernel(body_fn, rows, params, row_outs, acc_outs, *, name, tile=ROW_TILE):
    L = rows[0][0].shape[0]
    tile = min(tile, L)
    n_in = len(rows) + len(params)
    n_ro = len(row_outs)

    def body(*refs):
        outs = body_fn(*[r[...] for r in refs[:n_in]])
        for ref, o in zip(refs[n_in:n_in + n_ro], outs[:n_ro]):
            ref[...] = o.astype(ref.dtype)
        first = pl.program_id(0) == 0
        for ref, o in zip(refs[n_in + n_ro:], outs[n_ro:]):
            @pl.when(first)
            def _(ref=ref):
                ref[...] = jnp.zeros(ref.shape, ref.dtype)
            ref[...] += o

    in_specs = [pl.BlockSpec((tile, w), lambda i, cb=cb: (i, cb)) for _, w, cb in rows]
    in_specs += [pl.BlockSpec(p.shape, lambda i: (0, 0)) for p in params]
    out_specs = [pl.BlockSpec((tile, w), lambda i: (i, 0)) for w, _ in row_outs]
    out_specs += [pl.BlockSpec(s, lambda i: (0, 0)) for s in acc_outs]
    out_shape = [jax.ShapeDtypeStruct((L, w), dt) for w, dt in row_outs]
    out_shape += [jax.ShapeDtypeStruct(s, F32) for s in acc_outs]
    return pl.pallas_call(
        body, name=name, grid=(L // tile,), in_specs=in_specs, out_specs=out_specs, out_shape=out_shape,
        compiler_params=_cparams("arbitrary" if acc_outs else "parallel"),
    )(*[a for a, _, _ in rows], *params)


def _col_kernel(body_fn, cols, params, col_outs, par_outs, *, name, n_tiles):
    L = cols[0][0].shape[0]
    n_in = len(cols) + len(params)
    width = n_tiles * COL_TILE

    def body(*refs):
        outs = body_fn(*[r[...] for r in refs[:n_in]])
        for ref, o in zip(refs[n_in:], outs):
            ref[...] = o.astype(ref.dtype)

    in_specs = [pl.BlockSpec((L, COL_TILE), lambda j, cb=cb: (0, cb + j)) for _, cb in cols]
    in_specs += [pl.BlockSpec((p.shape[0], COL_TILE), lambda j, cb=cb: (0, cb + j)) for p, cb in params]
    out_specs = [pl.BlockSpec((L, COL_TILE), lambda j: (0, j)) for _ in col_outs]
    out_specs += [pl.BlockSpec((k, COL_TILE), lambda j: (0, j)) for k in par_outs]
    out_shape = [jax.ShapeDtypeStruct((L, width), dt) for dt in col_outs]
    out_shape += [jax.ShapeDtypeStruct((k, width), F32) for k in par_outs]
    return pl.pallas_call(
        body, name=name, grid=(n_tiles,), in_specs=in_specs, out_specs=out_specs, out_shape=out_shape,
        compiler_params=_cparams("parallel"),
    )(*[a for a, _ in cols], *[p for p, _ in params])


def _pick(n, options):
    for t in options:
        if n % t == 0:
            return t
    return n


MATMUL_VMEM = 40 * 1024 * 1024


def _matmul_tiles(M, N, K, out_bytes):
    best = None
    for tm in (1024, 512, 256, 128, M):
        for tn in (1408, 1024, 512, 256, 128, N):
            if M % tm or N % tn:
                continue
            if 2 * (2 * K * (tm + tn) + out_bytes * tm * tn) > MATMUL_VMEM:
                continue
            if best is None or tm * tn > best[0] * best[1]:
                best = (tm, tn)
    return best


def _matmul(a, b, *, name, nt=False, add=None, out_dtype=F32):
    M, K = a.shape
    N = b.shape[0] if nt else b.shape[1]
    tm, tn = _matmul_tiles(M, N, K, 4 * (1 + (add is not None)) if out_dtype == F32 else 2 + 4 * (add is not None))

    def body(*refs):
        a_ref, b_ref = refs[0], refs[1]
        o_ref = refs[-1]
        if nt:
            acc = lax.dot_general(a_ref[...], b_ref[...], NT_DIMS, preferred_element_type=F32)
        else:
            acc = jnp.dot(a_ref[...], b_ref[...], preferred_element_type=F32)
        if add is not None:
            acc = acc + refs[2][...]
        o_ref[...] = acc.astype(o_ref.dtype)

    in_specs = [pl.BlockSpec((tm, K), lambda i, j: (i, 0)),
                pl.BlockSpec((tn, K), lambda i, j: (j, 0)) if nt else pl.BlockSpec((K, tn), lambda i, j: (0, j))]
    args = [a, b]
    if add is not None:
        in_specs.append(pl.BlockSpec((tm, tn), lambda i, j: (i, j)))
        args.append(add)
    return pl.pallas_call(
        body, name=name, grid=(M // tm, N // tn), in_specs=in_specs,
        out_specs=pl.BlockSpec((tm, tn), lambda i, j: (i, j)),
        out_shape=jax.ShapeDtypeStruct((M, N), out_dtype),
        compiler_params=_cparams("parallel", "parallel"),
    )(*args)


def _hgrn2_pre_fn(q_raw, ffw_raw, fbw_raw, lb_logits):
    lb = jax.nn.softmax(lb_logits, axis=0)[0:1]

    def gate(fr):
        f = lb + (1.0 - lb) * jax.nn.sigmoid(fr)
        return jnp.log(f), 1.0 - f

    lf_fw, k_fw = gate(ffw_raw)
    lf_bw, k_bw = gate(fbw_raw)
    return jax.nn.silu(q_raw), lf_fw, k_fw, lf_bw, k_bw


def _hgrn2_post_fn(o, g, norm_w):
    outs = []
    for h in range(HG_HEADS):
        sl = slice(h * HG_HEAD_DIM, (h + 1) * HG_HEAD_DIM)
        outs.append(_rms(o[:, sl], norm_w) * jax.nn.silu(g[:, sl]))
    return jnp.concatenate(outs, axis=1)


def _mamba_post_fn(y, z, norm_w):
    outs = []
    gw = D_INNER // SSM_GROUPS
    for gi in range(SSM_GROUPS):
        sl = slice(gi * gw, (gi + 1) * gw)
        outs.append(_rms(y[:, sl] * jax.nn.silu(z[:, sl]), norm_w[:, sl]))
    return jnp.concatenate(outs, axis=1)


def _shift_rows_impl(x, d):
    if d == 0:
        return x
    n = x.shape[0]
    t = lax.broadcasted_iota(jnp.int32, x.shape, 0)
    rolled = pltpu.roll(x, d % n, 0)
    return jnp.where((t >= d) if d > 0 else (t < n + d), rolled, 0.0)


@functools.partial(jax.custom_vjp, nondiff_argnums=(1,))
def _shift_rows(x, d):
    return _shift_rows_impl(x, d)


_shift_rows.defvjp(lambda x, d: (_shift_rows_impl(x, d), None), lambda d, _, g: (_shift_rows_impl(g, -d),))


def _dwconv(x, w, b):
    taps = w.shape[0]
    c = (taps - 1) // 2
    y = b + w[0:1, :] * _shift_rows(x, c)
    for k in range(1, taps):
        y = y + w[k:k + 1, :] * _shift_rows(x, c - k)
    return y


def _ffn_mid_fn(gate, val, w, b):
    return jax.nn.silu(_dwconv(gate, w, b)) * val


def _mamba_conv_fn(xbc, w, b):
    return jax.nn.silu(_dwconv(xbc, w, b))


def _gla_consts(rev):
    return lax.broadcasted_iota(jnp.int32, (GLA_CHUNK, HG_HEAD_DIM), 0)


def _segment_cumsum(x, seg, rev):
    n = x.shape[0]
    t = lax.broadcasted_iota(jnp.int32, x.shape, 0) & (seg - 1)
    s = 1
    while s < seg:
        if rev:
            x = x + jnp.where(t < seg - s, pltpu.roll(x, n - s, 0), 0.0)
        else:
            x = x + jnp.where(t >= s, pltpu.roll(x, s, 0), 0.0)
        s *= 2
    return x


@functools.partial(jax.custom_vjp, nondiff_argnums=(1, 2))
def _segment_cumsum_diff(x, seg, rev):
    return _segment_cumsum(x, seg, rev)


_segment_cumsum_diff.defvjp(lambda x, seg, rev: (_segment_cumsum(x, seg, rev), None),
                            lambda seg, rev, _, g: (_segment_cumsum(g, seg, not rev),))


def _gla_chunk(st, q, k, v, b, *, rev, consts):
    row = consts
    c = q.shape[0]
    o = lax.dot_general((q * jnp.exp(b)).astype(BF16), st.astype(BF16), NT_DIMS, preferred_element_type=F32)
    for s in range(c):
        e = jnp.exp(jnp.where((row <= s) if rev else (row >= s), b - b[s:s + 1], -jnp.inf))
        o = o + jnp.sum(q * (k[s:s + 1] * e), axis=-1, keepdims=True) * v[s:s + 1]
    b_end = b[0:1] if rev else b[c - 1:c]
    kd = (k * jnp.exp(b_end - b)).astype(BF16)
    st_new = st * jnp.exp(b_end) + lax.dot_general(v.astype(BF16), kd, TN_DIMS, preferred_element_type=F32)
    return st_new, o


def _gla_fwd(q, k, v, v_cb, g, *, rev, add, name):
    L = q.shape[0]
    blk = min(GLA_BLOCK, L)
    nblk, nsub = L // blk, blk // GLA_CHUNK
    hd, hps = HG_HEAD_DIM, GLA_HEADS_PER_STEP
    wide = hd * hps

    def body(*refs):
        q_ref, k_ref, v_ref, g_ref = refs[:4]
        add_ref = refs[4] if add is not None else None
        o_ref, st_out, st_scr, b_scr = refs[-4:]
        consts = _gla_consts(rev)

        @pl.when(pl.program_id(1) == 0)
        def _():
            st_scr[...] = jnp.zeros(st_scr.shape, F32)

        st_out[...] = st_scr[...]
        b_scr[...] = _segment_cumsum(g_ref[...], GLA_CHUNK, rev)

        def step(i, carry):
            sl = pl.ds(pl.multiple_of((nsub - 1 - i if rev else i) * GLA_CHUNK, GLA_CHUNK), GLA_CHUNK)
            lanes = [slice(hi * hd, (hi + 1) * hd) for hi in range(hps)]
            ins = [(st_scr[hi], q_ref[sl, ln], k_ref[sl, ln], v_ref[sl, ln], b_scr[sl, ln])
                   for hi, ln in enumerate(lanes)]
            adds = [add_ref[sl, ln] for ln in lanes] if add_ref is not None else None
            outs = [_gla_chunk(*args, rev=rev, consts=consts) for args in ins]
            for hi, ln in enumerate(lanes):
                st_scr[hi] = outs[hi][0]
                o_ref[sl, ln] = outs[hi][1] if adds is None else outs[hi][1] + adds[hi]
            return carry

        lax.fori_loop(0, nsub, step, 0)

    def pos(j):
        return nblk - 1 - j if rev else j

    spec = pl.BlockSpec((blk, wide), lambda h, j: (pos(j), h))
    in_specs = [spec, spec, pl.BlockSpec((blk, wide), lambda h, j: (pos(j), v_cb // hps + h)), spec]
    args = [q, k, v, g]
    if add is not None:
        in_specs.append(spec)
        args.append(add)
    return pl.pallas_call(
        body, name=name, grid=(HG_HEADS // hps, nblk), in_specs=in_specs,
        out_specs=[spec, pl.BlockSpec((hps, None, hd, hd), lambda h, j: (h, j, 0, 0))],
        out_shape=[jax.ShapeDtypeStruct((L, HG_HEADS * hd), F32),
                   jax.ShapeDtypeStruct((HG_HEADS, nblk, hd, hd), F32)],
        scratch_shapes=[pltpu.VMEM((hps, hd, hd), F32), pltpu.VMEM((blk, wide), F32)],
        compiler_params=_cparams("parallel", "arbitrary"),
    )(*args)


def _gla_bwd(q, k, v, v_cb, g, do, states, *, rev, adds, name):
    L = q.shape[0]
    blk = min(GLA_BLOCK, L)
    nblk, nsub = L // blk, blk // GLA_CHUNK
    hd, hps = HG_HEAD_DIM, GLA_HEADS_PER_STEP
    wide = hd * hps
    n_add = 0 if adds is None else 2

    def body(*refs):
        q_ref, k_ref, v_ref, g_ref, do_ref, st_in = refs[:6]
        add_refs = refs[6:6 + n_add]
        dq_ref, dk_ref, dv_ref, dg_ref, dst_scr, sub_scr, b_scr, db_scr = refs[6 + n_add:]
        consts = _gla_consts(rev)
        chunk = functools.partial(_gla_chunk, rev=rev, consts=consts)

        @pl.when(pl.program_id(1) == 0)
        def _():
            dst_scr[...] = jnp.zeros(dst_scr.shape, F32)

        b_scr[...] = _segment_cumsum(g_ref[...], GLA_CHUNK, rev)

        def rows(i):
            return pl.ds(pl.multiple_of((nsub - 1 - i if rev else i) * GLA_CHUNK, GLA_CHUNK), GLA_CHUNK)

        for hi in range(hps):
            sub_scr[hi] = st_in[hi]

        def replay(i, carry):
            sl = rows(i)
            lanes = [slice(hi * hd, (hi + 1) * hd) for hi in range(hps)]
            ins = [(sub_scr[i * hps + hi], q_ref[sl, ln], k_ref[sl, ln], v_ref[sl, ln], b_scr[sl, ln])
                   for hi, ln in enumerate(lanes)]
            outs = [chunk(*args)[0] for args in ins]
            for hi in range(hps):
                sub_scr[(i + 1) * hps + hi] = outs[hi]
            return carry

        lax.fori_loop(0, nsub - 1, replay, 0)

        def step(n, carry):
            i = nsub - 1 - n
            sl = rows(i)
            lanes = [slice(hi * hd, (hi + 1) * hd) for hi in range(hps)]
            ins = [(sub_scr[i * hps + hi], q_ref[sl, ln], k_ref[sl, ln], v_ref[sl, ln], b_scr[sl, ln])
                   for hi, ln in enumerate(lanes)]
            cts = [(dst_scr[hi], do_ref[sl, ln]) for hi, ln in enumerate(lanes)]
            adds = [(add_refs[0][sl, ln], add_refs[1][sl, ln]) for ln in lanes] if n_add else None
            outs = [jax.vjp(chunk, *args)[1](ct) for args, ct in zip(ins, cts)]
            for hi, ln in enumerate(lanes):
                dst, dq, dk, dv, db = outs[hi]
                dst_scr[hi] = dst
                dq_ref[sl, ln] = dq if adds is None else dq + adds[hi][0]
                dk_ref[sl, ln] = dk
                dv_ref[sl, ln] = dv if adds is None else dv + adds[hi][1]
                db_scr[sl, ln] = db
            return carry

        lax.fori_loop(0, nsub, step, 0)
        dg_ref[...] = _segment_cumsum(db_scr[...], GLA_CHUNK, not rev)

    def pos(j):
        p = nblk - 1 - j
        return nblk - 1 - p if rev else p

    spec = pl.BlockSpec((blk, wide), lambda h, j: (pos(j), h))
    in_specs = [spec, spec, pl.BlockSpec((blk, wide), lambda h, j: (pos(j), v_cb // hps + h)), spec, spec,
                pl.BlockSpec((hps, None, hd, hd), lambda h, j: (h, nblk - 1 - j, 0, 0))]
    args = [q, k, v, g, do, states]
    if adds is not None:
        in_specs += [spec, spec]
        args += list(adds)
    out = jax.ShapeDtypeStruct((L, HG_HEADS * hd), F32)
    return pl.pallas_call(
        body, name=name, grid=(HG_HEADS // hps, nblk), in_specs=in_specs,
        out_specs=[spec] * 4, out_shape=[out] * 4,
        scratch_shapes=[pltpu.VMEM((hps, hd, hd), F32), pltpu.VMEM((nsub * hps, hd, hd), F32),
                        pltpu.VMEM((blk, wide), F32), pltpu.VMEM((blk, wide), F32)],
        compiler_params=_cparams("parallel", "arbitrary"),
    )(*args)


def _ssd_consts(rev):
    c = SSD_CHUNK
    gw = SSM_HPG * SSM_HEAD_DIM
    r2 = lax.broadcasted_iota(jnp.int32, (c, c), 0)
    c2 = lax.broadcasted_iota(jnp.int32, (c, c), 1)
    low = (r2 <= c2) if rev else (r2 >= c2)
    lane = lax.broadcasted_iota(jnp.int32, (1, gw), 1)
    return low, r2 == c2, lane, lane >> 6


def _expand_heads(v4, head_of_lane):
    lane4 = lax.broadcasted_iota(jnp.int32, v4.shape, 1)
    out = None
    for j in range(SSM_HPG):
        col = jnp.sum(jnp.where(lane4 == j, v4, 0.0), axis=-1, keepdims=True)
        term = jnp.where(head_of_lane == j, col, 0.0)
        out = term if out is None else out + term
    return out


def _ssd_chunk(st, x, bm, cm, dtr, bias, alog, dsk, *, rev, consts, skip):
    low, eye, lane, head_of_lane = consts
    c = x.shape[0]
    dt_l = _expand_heads(jax.nn.softplus(dtr + bias), head_of_lane)
    a_l = _expand_heads(-jnp.exp(alog), head_of_lane)
    acum = _segment_cumsum_diff(dt_l * a_l, c, rev)
    xd = x * dt_l
    cb = lax.dot_general(cm.astype(BF16), bm.astype(BF16), NT_DIMS, preferred_element_type=F32)
    y = jnp.dot(cm.astype(BF16), st.astype(BF16), preferred_element_type=F32) * jnp.exp(acum)
    for j in range(SSM_HPG):
        acol = jnp.sum(jnp.where(lane == j * SSM_HEAD_DIM, acum, 0.0), axis=-1, keepdims=True)
        ab = jnp.broadcast_to(acol, (c, c))
        arow = jnp.sum(jnp.where(eye, ab, 0.0), axis=0, keepdims=True)
        lmat = jnp.exp(jnp.where(low, ab - arow, -jnp.inf))
        xj = jnp.where(head_of_lane == j, xd, 0.0)
        y = y + jnp.dot((cb * lmat).astype(BF16), xj.astype(BF16), preferred_element_type=F32)
    a_end = acum[0:1] if rev else acum[c - 1:c]
    xdec = (xd * jnp.exp(a_end - acum)).astype(BF16)
    st_new = st * jnp.exp(a_end) + lax.dot_general(bm.astype(BF16), xdec, TN_DIMS, preferred_element_type=F32)
    if skip:
        y = y + x * _expand_heads(dsk, head_of_lane)
    return st_new, y


def _ssd_specs(L, rev):
    blk = min(SSD_BLOCK, L)
    nblk = L // blk
    gw = SSM_HPG * SSM_HEAD_DIM

    def pos(j):
        return nblk - 1 - j if rev else j

    return blk, nblk, gw, pos


def _ssd_fwd(xbc, dtr, bias, alog, dsk, *, rev, skip, add, name):
    L = xbc.shape[0]
    blk, nblk, gw, pos = _ssd_specs(L, rev)
    nsub = blk // SSD_CHUNK
    n = SSM_STATE

    def body(*refs):
        x_ref, b_ref, c_ref, dt_ref, bias_ref, alog_ref, dsk_ref = refs[:7]
        add_ref = refs[7] if add is not None else None
        y_ref, st_out, st_scr = refs[-3:]
        consts = _ssd_consts(rev)

        @pl.when(pl.program_id(1) == 0)
        def _():
            st_scr[...] = jnp.zeros(st_scr.shape, F32)

        st_out[...] = st_scr[...]

        def step(i, carry):
            sl = pl.ds(pl.multiple_of((nsub - 1 - i if rev else i) * SSD_CHUNK, SSD_CHUNK), SSD_CHUNK)
            st_new, y = _ssd_chunk(st_scr[...], x_ref[sl, :], b_ref[sl, :], c_ref[sl, :], dt_ref[sl, :],
                                   bias_ref[...], alog_ref[...], dsk_ref[...], rev=rev, consts=consts, skip=skip)
            st_scr[...] = st_new
            if add_ref is not None:
                y = y + add_ref[sl, :]
            y_ref[sl, :] = y
            return carry

        lax.fori_loop(0, nsub, step, 0)

    b0 = D_INNER // n
    yspec = pl.BlockSpec((blk, gw), lambda g, j: (pos(j), g))
    pspec = pl.BlockSpec((None, 1, SSM_HPG), lambda g, j: (g, 0, 0))
    in_specs = [yspec,
                pl.BlockSpec((blk, n), lambda g, j: (pos(j), b0 + g)),
                pl.BlockSpec((blk, n), lambda g, j: (pos(j), b0 + SSM_GROUPS + g)),
                pl.BlockSpec((None, blk, SSM_HPG), lambda g, j: (g, pos(j), 0)),
                pspec, pspec, pspec]
    args = [xbc, xbc, xbc, dtr, bias, alog, dsk]
    if add is not None:
        in_specs.append(yspec)
        args.append(add)
    return pl.pallas_call(
        body, name=name, grid=(SSM_GROUPS, nblk), in_specs=in_specs,
        out_specs=[yspec, pl.BlockSpec((None, None, n, gw), lambda g, j: (g, j, 0, 0))],
        out_shape=[jax.ShapeDtypeStruct((L, D_INNER), F32),
                   jax.ShapeDtypeStruct((SSM_GROUPS, nblk, n, gw), F32)],
        scratch_shapes=[pltpu.VMEM((n, gw), F32)],
        compiler_params=_cparams("parallel", "arbitrary"),
    )(*args)


def _ssd_bwd(xbc, dtr, bias, alog, dsk, dy, states, *, rev, skip, adds, name):
    L = xbc.shape[0]
    blk, nblk, gw, _ = _ssd_specs(L, rev)
    nsub = blk // SSD_CHUNK
    n = SSM_STATE
    n_add = 0 if adds is None else 3

    def body(*refs):
        x_ref, b_ref, c_ref, dt_ref, bias_ref, alog_ref, dsk_ref, dy_ref, st_in = refs[:9]
        add_refs = refs[9:9 + n_add]
        dx_ref, db_ref, dc_ref, ddt_ref, dbias_ref, dalog_ref, ddsk_ref, dst_scr, sub_scr = refs[9 + n_add:]
        consts = _ssd_consts(rev)
        chunk = functools.partial(_ssd_chunk, rev=rev, consts=consts, skip=skip)

        @pl.when(pl.program_id(1) == 0)
        def _():
            dst_scr[...] = jnp.zeros(dst_scr.shape, F32)
            dbias_ref[...] = jnp.zeros(dbias_ref.shape, F32)
            dalog_ref[...] = jnp.zeros(dalog_ref.shape, F32)
            ddsk_ref[...] = jnp.zeros(ddsk_ref.shape, F32)

        def rows(i):
            return pl.ds(pl.multiple_of((nsub - 1 - i if rev else i) * SSD_CHUNK, SSD_CHUNK), SSD_CHUNK)

        def operands(i):
            sl = rows(i)
            return (x_ref[sl, :], b_ref[sl, :], c_ref[sl, :], dt_ref[sl, :], bias_ref[...], alog_ref[...], dsk_ref[...])

        def replay(i, st):
            sub_scr[i] = st
            return chunk(st, *operands(i))[0]

        lax.fori_loop(0, nsub, replay, st_in[...])

        def step(k, carry):
            i = nsub - 1 - k
            sl = rows(i)
            _, vjp = jax.vjp(chunk, sub_scr[i], *operands(i))
            dst, dx, db, dc, ddt, dbias, dalog, ddsk = vjp((dst_scr[...], dy_ref[sl, :]))
            dst_scr[...] = dst
            if n_add:
                dx = dx + add_refs[0][sl, :]
                db = db + add_refs[1][sl, :]
                dc = dc + add_refs[2][sl, :]
            dx_ref[sl, :] = dx
            db_ref[sl, :] = db
            dc_ref[sl, :] = dc
            ddt_ref[sl, :] = ddt
            dbias_ref[...] += dbias
            dalog_ref[...] += dalog
            ddsk_ref[...] += ddsk
            return carry

        lax.fori_loop(0, nsub, step, 0)

    def pos(j):
        p = nblk - 1 - j
        return nblk - 1 - p if rev else p

    b0 = D_INNER // n
    xspec = pl.BlockSpec((blk, gw), lambda g, j: (pos(j), g))
    nspec = pl.BlockSpec((blk, n), lambda g, j: (pos(j), g))
    dtspec = pl.BlockSpec((None, blk, SSM_HPG), lambda g, j: (g, pos(j), 0))
    pspec = pl.BlockSpec((None, 1, SSM_HPG), lambda g, j: (g, 0, 0))
    in_specs = [xspec,
                pl.BlockSpec((blk, n), lambda g, j: (pos(j), b0 + g)),
                pl.BlockSpec((blk, n), lambda g, j: (pos(j), b0 + SSM_GROUPS + g)),
                dtspec, pspec, pspec, pspec, xspec,
                pl.BlockSpec((None, None, n, gw), lambda g, j: (g, nblk - 1 - j, 0, 0))]
    args = [xbc, xbc, xbc, dtr, bias, alog, dsk, dy, states]
    if adds is not None:
        in_specs += [xspec, nspec, nspec]
        args += list(adds)
    par = jax.ShapeDtypeStruct((SSM_GROUPS, 1, SSM_HPG), F32)
    return pl.pallas_call(
        body, name=name, grid=(SSM_GROUPS, nblk), in_specs=in_specs,
        out_specs=[xspec, nspec, nspec, dtspec, pspec, pspec, pspec],
        out_shape=[jax.ShapeDtypeStruct((L, D_INNER), F32), jax.ShapeDtypeStruct((L, GN), F32),
                   jax.ShapeDtypeStruct((L, GN), F32), jax.ShapeDtypeStruct((SSM_GROUPS, L, SSM_HPG), F32),
                   par, par, par],
        scratch_shapes=[pltpu.VMEM((n, gw), F32), pltpu.VMEM((nsub, n, gw), F32)],
        compiler_params=_cparams("parallel", "arbitrary"),
    )(*args)


def _norm_fwd(h, w, name):
    d = h.shape[1]
    return _row_kernel(lambda hv, wv: (_rms(hv, wv),), [(h, d, 0)], [w], [(d, BF16)], [], name=name)[0]


def _norm_bwd(h, w, du, dh_in, name):
    d = h.shape[1]

    def fn(hv, duv, dhv, wv):
        _, vjp = jax.vjp(_rms, hv, wv)
        dh, dw = vjp(duv.astype(F32))
        dh = dh + dhv
        return dh, dh, dw

    return _row_kernel(fn, [(h, d, 0), (du, d, 0), (dh_in, d, 0)], [w], [(d, F32), (d, BF16)], [(1, d)], name=name)


def _ffn_fwd(h, norm_w, w_gate, w_val, w_out, conv_w, conv_b, tag):
    u = _norm_fwd(h, norm_w, f"{tag}_norm")
    pg = _matmul(u, w_gate, name=f"{tag}_in_gate")
    pv = _matmul(u, w_val, name=f"{tag}_in_val")
    y = _col_kernel(lambda g, v, w, b: (_ffn_mid_fn(g, v, w, b),), [(pg, 0), (pv, 0)], [(conv_w, 0), (conv_b, 0)],
                    [BF16], [], name=f"{tag}_mid", n_tiles=D_FF // COL_TILE)[0]
    h_out = _matmul(y, w_out, add=h, name=f"{tag}_out")
    return h_out, (u, pg, pv, y)


def _ffn_bwd(h, dh, dh16, saved, norm_w, w_gate, w_val, w_out, conv_w, conv_b, tag):
    u, pg, pv, y = saved
    dy = _matmul(dh16, w_out, nt=True, name=f"{tag}_out_dx")
    dw_out = _matmul(y.T, dh16, name=f"{tag}_out_dw")

    def fn(g, v, ct, w, b):
        _, vjp = jax.vjp(_ffn_mid_fn, g, v, w, b)
        return vjp(ct)

    dpg, dpv, dcw, dcb = _col_kernel(fn, [(pg, 0), (pv, 0), (dy, 0)], [(conv_w, 0), (conv_b, 0)], [BF16, BF16],
                                     [FFN_CONV, 1], name=f"{tag}_mid_bwd", n_tiles=D_FF // COL_TILE)
    du = _matmul(dpg, w_gate, nt=True, name=f"{tag}_gate_dx")
    du = _matmul(dpv, w_val, nt=True, add=du, out_dtype=BF16, name=f"{tag}_val_dx")
    ut = u.T
    dw_gate = _matmul(ut, dpg, name=f"{tag}_gate_dw")
    dw_val = _matmul(ut, dpv, name=f"{tag}_val_dw")
    dh, dh16, dnw = _norm_bwd(h, norm_w, du, dh, f"{tag}_norm_bwd")
    return dh, dh16, dict(w_in=jnp.concatenate([dw_gate, dw_val], axis=1), w_out=dw_out, conv_w=dcw, conv_b=dcb, norm=dnw)


def _hgrn2_fwd(h, norm_w, w_in, lb_logits, a_norm_w, w_out):
    d = D_MODEL
    u = _norm_fwd(h, norm_w, "hg_norm")
    pa = _matmul(u, w_in, name="hg_in")
    qs, lf_fw, k_fw, lf_bw, k_bw = _row_kernel(
        _hgrn2_pre_fn, [(pa, d, 0), (pa, d, 1), (pa, d, 2)], [lb_logits], [(d, F32)] * 5, [], name="hg_pre")
    o_fw, st_fw = _gla_fwd(qs, k_fw, pa, 3 * HG_HEADS, lf_fw, rev=False, add=None, name="hg_gla_fw")
    o, st_bw = _gla_fwd(qs, k_bw, pa, 3 * HG_HEADS, lf_bw, rev=True, add=o_fw, name="hg_gla_bw")
    y = _row_kernel(lambda ov, gv, wv: (_hgrn2_post_fn(ov, gv, wv),), [(o, d, 0), (pa, d, 4)], [a_norm_w],
                    [(d, BF16)], [], name="hg_post")[0]
    h_out = _matmul(y, w_out, add=h, name="hg_out")
    return h_out, (u, pa, qs, lf_fw, k_fw, lf_bw, k_bw, st_fw, st_bw, o, y)


def _hgrn2_bwd(h, dh, dh16, saved, norm_w, w_in, lb_logits, a_norm_w, w_out):
    d = D_MODEL
    u, pa, qs, lf_fw, k_fw, lf_bw, k_bw, st_fw, st_bw, o, y = saved
    dy = _matmul(dh16, w_out, nt=True, name="hg_out_dx")
    dw_out = _matmul(y.T, dh16, name="hg_out_dw")

    def post_bwd(ov, gv, ct, wv):
        _, vjp = jax.vjp(_hgrn2_post_fn, ov, gv, wv)
        return vjp(ct)

    do, dg, dnw = _row_kernel(post_bwd, [(o, d, 0), (pa, d, 4), (dy, d, 0)], [a_norm_w], [(d, F32), (d, F32)],
                              [(1, HG_HEAD_DIM)], name="hg_post_bwd")
    dq1, dk_fw, dv1, dlf_fw = _gla_bwd(qs, k_fw, pa, 3 * HG_HEADS, lf_fw, do, st_fw, rev=False, adds=None,
                                       name="hg_gla_fw_bwd")
    dqs, dk_bw, dv, dlf_bw = _gla_bwd(qs, k_bw, pa, 3 * HG_HEADS, lf_bw, do, st_bw, rev=True, adds=(dq1, dv1),
                                      name="hg_gla_bw_bwd")

    def pre_bwd(qr, fr, br, c0, c1, c2, c3, c4, dvv, dgv, lbl):
        _, vjp = jax.vjp(_hgrn2_pre_fn, qr, fr, br, lbl)
        dq, df, db, dlbl = vjp((c0, c1, c2, c3, c4))
        return jnp.concatenate([dq, df, db, dvv, dgv], axis=1), dlbl

    rows = [(pa, d, 0), (pa, d, 1), (pa, d, 2), (dqs, d, 0), (dlf_fw, d, 0), (dk_fw, d, 0), (dlf_bw, d, 0),
            (dk_bw, d, 0), (dv, d, 0), (dg, d, 0)]
    dpa, dlbl = _row_kernel(pre_bwd, rows, [lb_logits], [(5 * d, BF16)], [lb_logits.shape], name="hg_pre_bwd")
    du = _matmul(dpa, w_in, nt=True, out_dtype=BF16, name="hg_in_dx")
    dw_in = _matmul(u.T, dpa, name="hg_in_dw")
    dh, dh16, dn1 = _norm_bwd(h, norm_w, du, dh, "hg_norm_bwd")
    return dh, dh16, dict(w_in=dw_in, w_out=dw_out, lb=dlbl, a_norm=dnw, norm=dn1)


def _group_params(p):
    return p.reshape(SSM_GROUPS, 1, SSM_HPG)


def _mamba_fwd(h, norm_w, w_z, w_xbc, w_dt, conv_w, conv_b, dt_bias, a_log, d_skip, b_norm_w, w_out):
    L = h.shape[0]
    u = _norm_fwd(h, norm_w, "mb_norm")
    z = _matmul(u, w_z, name="mb_in_z")
    xbc_raw = _matmul(u, w_xbc, name="mb_in_xbc")
    dt_raw = _matmul(u, w_dt, name="mb_in_dt")
    xbc = _col_kernel(lambda xv, w, b: (_mamba_conv_fn(xv, w, b),), [(xbc_raw, 0)], [(conv_w, 0), (conv_b, 0)],
                      [F32], [], name="mb_conv", n_tiles=CONV_DIM // COL_TILE)[0]
    dtr = dt_raw.reshape(L, 2, SSM_GROUPS, SSM_HPG).transpose(1, 2, 0, 3)
    bias, alog = dt_bias.reshape(2, -1), a_log.reshape(2, -1)
    dsk = _group_params(d_skip.reshape(-1))
    y_fw, st_fw = _ssd_fwd(xbc, dtr[0], _group_params(bias[0]), _group_params(alog[0]), dsk, rev=False, skip=True,
                           add=None, name="mb_ssd_fw")
    ysum, st_bw = _ssd_fwd(xbc, dtr[1], _group_params(bias[1]), _group_params(alog[1]), dsk, rev=True, skip=False,
                           add=y_fw, name="mb_ssd_bw")
    y = _row_kernel(lambda yv, zv, wv: (_mamba_post_fn(yv, zv, wv),), [(ysum, D_INNER, 0), (z, D_INNER, 0)],
                    [b_norm_w], [(D_INNER, BF16)], [], name="mb_post")[0]
    h_out = _matmul(y, w_out, add=h, name="mb_out")
    return h_out, (u, z, xbc_raw, xbc, dtr, st_fw, st_bw, ysum, y)


def _mamba_bwd(h, dh, dh16, saved, norm_w, w_z, w_xbc, w_dt, conv_w, conv_b, dt_bias, a_log, d_skip, b_norm_w, w_out):
    L = h.shape[0]
    u, z, xbc_raw, xbc, dtr, st_fw, st_bw, ysum, y = saved
    dy = _matmul(dh16, w_out, nt=True, name="mb_out_dx")
    dw_out = _matmul(y.T, dh16, name="mb_out_dw")

    def post_bwd(yv, zv, ct, wv):
        _, vjp = jax.vjp(_mamba_post_fn, yv, zv, wv)
        return vjp(ct)

    dys, dz, dbn = _row_kernel(post_bwd, [(ysum, D_INNER, 0), (z, D_INNER, 0), (dy, D_INNER, 0)], [b_norm_w],
                               [(D_INNER, F32), (D_INNER, BF16)], [(1, D_INNER)], name="mb_post_bwd")
    bias, alog = dt_bias.reshape(2, -1), a_log.reshape(2, -1)
    dsk = _group_params(d_skip.reshape(-1))
    dx1, db1, dc1, ddt_fw, dbias_fw, dalog_fw, ddsk = _ssd_bwd(
        xbc, dtr[0], _group_params(bias[0]), _group_params(alog[0]), dsk, dys, st_fw, rev=False, skip=True,
        adds=None, name="mb_ssd_fw_bwd")
    dx, db, dc, ddt_bw, dbias_bw, dalog_bw, _ = _ssd_bwd(
        xbc, dtr[1], _group_params(bias[1]), _group_params(alog[1]), dsk, dys, st_bw, rev=True, skip=False,
        adds=(dx1, db1, dc1), name="mb_ssd_bw_bwd")

    def conv_bwd(n_tiles, ct, first):
        def fn(xv, ctv, w, b):
            _, vjp = jax.vjp(_mamba_conv_fn, xv, w, b)
            return vjp(ctv)
        return _col_kernel(fn, [(xbc_raw, first), (ct, 0)], [(conv_w, first), (conv_b, first)], [BF16],
                           [SSM_CONV, 1], name=f"mb_conv_bwd_{first}", n_tiles=n_tiles)

    nx, nb = D_INNER // COL_TILE, GN // COL_TILE
    parts = [conv_bwd(nx, dx, 0), conv_bwd(nb, db, nx), conv_bwd(nb, dc, nx + nb)]
    dxbc = jnp.concatenate([p[0] for p in parts], axis=1)
    dcw = jnp.concatenate([p[1] for p in parts], axis=1)
    dcb = jnp.concatenate([p[2] for p in parts], axis=1)
    ddt = jnp.stack([ddt_fw, ddt_bw]).transpose(2, 0, 1, 3).reshape(L, 2 * SSM_GROUPS * SSM_HPG).astype(BF16)
    du = _matmul(dz, w_z, nt=True, name="mb_z_dx")
    du = _matmul(dxbc, w_xbc, nt=True, add=du, name="mb_xbc_dx")
    du = _matmul(ddt, w_dt, nt=True, add=du, out_dtype=BF16, name="mb_dt_dx")
    ut = u.T
    dw_in = jnp.concatenate([_matmul(ut, dz, name="mb_z_dw"), _matmul(ut, dxbc, name="mb_xbc_dw"),
                             _matmul(ut, ddt, name="mb_dt_dw")], axis=1)
    dh, dh16, dn1 = _norm_bwd(h, norm_w, du, dh, "mb_norm_bwd")
    grads = dict(w_in=dw_in, w_out=dw_out, conv_w=dcw, conv_b=dcb, b_norm=dbn, norm=dn1,
                 dt_bias=jnp.stack([dbias_fw, dbias_bw]).reshape(1, 2, -1),
                 a_log=jnp.stack([dalog_fw, dalog_bw]).reshape(1, 2, -1), d_skip=ddsk.reshape(1, -1))
    return dh, dh16, grads


def _loss_head(h, target, w):
    d = h.shape[1]

    def fn(hv, tv, wv):
        def loss(hv, wv):
            err = _rms(hv, wv) - tv
            return 0.5 * jnp.sum(jnp.mean(err * err, axis=-1, keepdims=True), axis=0, keepdims=True)
        val, vjp = jax.vjp(loss, hv, wv)
        dh, dw = vjp(jnp.ones((1, 1), F32))
        return dh, dh, val, dw

    return _row_kernel(fn, [(h, d, 0), (target, d, 0)], [w], [(d, F32), (d, BF16)], [(1, 1), (1, d)], name="loss_head")


def _local_step(x, target, w):
    g_w, v_w = w["ffn_w_in"][:, :, :D_FF], w["ffn_w_in"][:, :, D_FF:]
    wz, wxbc, wdt = (w["b_w_in"][0, :, :D_INNER], w["b_w_in"][0, :, D_INNER:D_INNER + CONV_DIM],
                     w["b_w_in"][0, :, D_INNER + CONV_DIM:])
    ffn = [(w["norm2_w"][i:i + 1], g_w[i], v_w[i], w["ffn_w_out"][i], w["ffn_conv_w"][i], w["ffn_conv_b"][i:i + 1])
           for i in range(2)]
    hg = (w["norm1_w"][0:1], w["a_w_in"][0], w["a_lb_logits"], w["a_norm_w"], w["a_w_out"][0])
    mb = (w["norm1_w"][1:2], wz, wxbc, wdt, w["b_conv_w"][0], w["b_conv_b"], w["b_dt_bias"], w["b_a_log"],
          w["b_d_skip"], w["b_norm_w"], w["b_w_out"][0])

    h0 = x
    h1, s_hg = _hgrn2_fwd(h0, *hg)
    h2, s_f0 = _ffn_fwd(h1, *ffn[0], "ffn0")
    h3, s_mb = _mamba_fwd(h2, *mb)
    h4, s_f1 = _ffn_fwd(h3, *ffn[1], "ffn1")
    dh, dh16, loss, d_final = _loss_head(h4, target, w["final_norm_w"].reshape(1, -1))
    dh, dh16, g_f1 = _ffn_bwd(h3, dh, dh16, s_f1, *ffn[1], "ffn1")
    dh, dh16, g_mb = _mamba_bwd(h2, dh, dh16, s_mb, *mb)
    dh, dh16, g_f0 = _ffn_bwd(h1, dh, dh16, s_f0, *ffn[0], "ffn0")
    dh, dh16, g_hg = _hgrn2_bwd(h0, dh, dh16, s_hg, *hg)
    grads = dict(
        norm1_w=jnp.concatenate([g_hg["norm"], g_mb["norm"]], axis=0),
        norm2_w=jnp.concatenate([g_f0["norm"], g_f1["norm"]], axis=0),
        a_w_in=g_hg["w_in"][None], a_lb_logits=g_hg["lb"], a_norm_w=g_hg["a_norm"], a_w_out=g_hg["w_out"][None],
        b_w_in=g_mb["w_in"][None], b_conv_w=g_mb["conv_w"][None], b_conv_b=g_mb["conv_b"],
        b_dt_bias=g_mb["dt_bias"], b_a_log=g_mb["a_log"], b_d_skip=g_mb["d_skip"], b_norm_w=g_mb["b_norm"],
        b_w_out=g_mb["w_out"][None],
        ffn_w_in=jnp.stack([g_f0["w_in"], g_f1["w_in"]]), ffn_conv_w=jnp.stack([g_f0["conv_w"], g_f1["conv_w"]]),
        ffn_conv_b=jnp.concatenate([g_f0["conv_b"], g_f1["conv_b"]], axis=0),
        ffn_w_out=jnp.stack([g_f0["w_out"], g_f1["w_out"]]), final_norm_w=d_final.reshape(-1),
    )
    return loss, dh, grads


def _mesh_pos():
    return lax.axis_index("x"), lax.axis_index("y"), lax.axis_index("c")


N_COPIES = N_DEV - 1


def _comm_call(body, ins, out_shape, name):
    n = len(ins)
    hbm = pl.BlockSpec(memory_space=pl.ANY)
    return pl.pallas_call(
        body, name=name, out_shape=out_shape, in_specs=[hbm] * n, out_specs=[hbm] * n,
        scratch_shapes=[pltpu.SemaphoreType.DMA((n * N_COPIES,)), pltpu.SemaphoreType.DMA((n * N_COPIES,)),
                        pltpu.SemaphoreType.DMA((n,))],
    )(*ins)


def _all_gather(shards, name):
    n = len(shards)

    def body(*refs):
        x_refs, out_refs = refs[:n], refs[n:2 * n]
        send_sems, recv_sems, local_sems = refs[2 * n:]
        x, y, c = _mesh_pos()
        me, sibling = (x, y, c), (x, y, 1 - c)
        chips = [(1 - x, y), (x, 1 - y), (1 - x, 1 - y)]

        def copy(w, k, block, to, own=False):
            px, py, pc = block
            dst = out_refs[w].at[4 * px + 2 * py + pc]
            return pltpu.make_async_remote_copy(
                src_ref=x_refs[w] if own else dst, dst_ref=dst, send_sem=send_sems.at[w * N_COPIES + k],
                recv_sem=recv_sems.at[w * N_COPIES + k], device_id=to, device_id_type=MESH)

        mine = [pltpu.make_async_copy(x_refs[w], out_refs[w].at[4 * x + 2 * y + c], local_sems.at[w]) for w in range(n)]
        for cp in mine:
            cp.start()
        first = [copy(w, 1 + j, me, (*chip, c), own=True) for j, chip in enumerate(chips) for w in range(n)]
        first += [copy(w, 0, me, sibling, own=True) for w in range(n)]
        for cp in first:
            cp.start()
        passed = []
        for j, chip in enumerate(chips):
            for w in range(n):
                copy(w, 1 + j, (*chip, c), me).wait_recv()
                passed.append(copy(w, 4 + j, (*chip, c), sibling))
                passed[-1].start()
        for w in range(n):
            copy(w, 0, sibling, me).wait_recv()
        for j, chip in enumerate(chips):
            for w in range(n):
                copy(w, 4 + j, (*chip, 1 - c), me).wait_recv()
        for cp in first + passed:
            cp.wait_send()
        for cp in mine:
            cp.wait()

    out_shape = [jax.ShapeDtypeStruct((N_DEV, *s.shape), s.dtype) for s in shards]
    return _comm_call(body, shards, out_shape, name)


def _exchange_parts(parts, name):
    n = len(parts)

    def body(*refs):
        p_refs, got_refs = refs[:n], refs[n:2 * n]
        send_sems, recv_sems, local_sems = refs[2 * n:]
        x, y, c = _mesh_pos()
        me = 4 * x + 2 * y + c
        mine = [pltpu.make_async_copy(p_refs[w].at[me], got_refs[w].at[me], local_sems.at[w]) for w in range(n)]
        for cp in mine:
            cp.start()
        copies = []
        for k in (4, 2, 6, 5, 3, 7, 1):
            px = 1 - x if k & 4 else x
            py = 1 - y if k & 2 else y
            pc = 1 - c if k & 1 else c
            for w in range(n):
                cp = pltpu.make_async_remote_copy(
                    src_ref=p_refs[w].at[4 * px + 2 * py + pc], dst_ref=got_refs[w].at[me],
                    send_sem=send_sems.at[w * N_COPIES + k - 1], recv_sem=recv_sems.at[w * N_COPIES + k - 1],
                    device_id=(px, py, pc), device_id_type=MESH)
                cp.start()
                copies.append(cp)
        for cp in copies:
            cp.wait()
        for cp in mine:
            cp.wait()

    out_shape = [jax.ShapeDtypeStruct(p.shape, p.dtype) for p in parts]
    return _comm_call(body, parts, out_shape, name)


def _adamw(got, w, m, v, *, name, tile):
    rows, width = w.shape
    c1 = 1.0 / (1.0 - ADAM_B1 ** ADAM_STEP)
    c2 = 1.0 / (1.0 - ADAM_B2 ** ADAM_STEP)

    def body(got_ref, w_ref, m_ref, v_ref, g_ref, d_ref, nm_ref, nv_ref):
        g = got_ref[0]
        for s in range(1, N_DEV):
            g = g + got_ref[s]
        m_new = ADAM_B1 * m_ref[...] + (1.0 - ADAM_B1) * g
        v_new = ADAM_B2 * v_ref[...] + (1.0 - ADAM_B2) * (g * g)
        g_ref[...] = g
        nm_ref[...] = m_new
        nv_ref[...] = v_new
        d_ref[...] = -ADAM_LR * ((m_new * c1) / (jnp.sqrt(v_new * c2) + ADAM_EPS) + ADAM_WD * w_ref[...])

    spec = pl.BlockSpec((tile, width), lambda i: (i, 0))
    out = jax.ShapeDtypeStruct((rows, width), F32)
    return pl.pallas_call(
        body, name=name, grid=(rows // tile,),
        in_specs=[pl.BlockSpec((N_DEV, tile, width), lambda i: (0, i, 0)), spec, spec, spec],
        out_specs=[spec] * 4, out_shape=[out] * 4, compiler_params=_cparams("parallel"),
    )(got, w, m, v)


SHARDED = dict(a_w_in=(2, True), a_w_out=(1, True), b_w_in=(2, True), b_w_out=(1, True), ffn_w_in=(2, True),
               ffn_w_out=(1, True), b_conv_w=(2, False), b_conv_b=(1, False), b_norm_w=(1, False), ffn_conv_w=(2, False))
REPLICATED = ("norm1_w", "norm2_w", "a_lb_logits", "a_norm_w", "b_dt_bias", "b_a_log", "b_d_skip", "ffn_conv_b",
              "final_norm_w")
WEIGHTS = ("norm1_w", "norm2_w", "a_w_in", "a_lb_logits", "a_norm_w", "a_w_out", "b_w_in", "b_conv_w", "b_conv_b",
           "b_dt_bias", "b_a_log", "b_d_skip", "b_norm_w", "b_w_out", "ffn_w_in", "ffn_conv_w", "ffn_conv_b",
           "ffn_w_out", "final_norm_w")


def _pad_rows(flat, multiple):
    n = flat.shape[-1]
    per = PACK_W * multiple
    total = -(-n // per) * per
    flat = jnp.pad(flat, [(0, 0)] * (flat.ndim - 1) + [(0, total - n)])
    return flat.reshape(*flat.shape[:-1], total // PACK_W, PACK_W)


def _to_parts(full, axis):
    shp = full.shape
    t = full.reshape(*shp[:axis], N_DEV, shp[axis] // N_DEV, *shp[axis + 1:])
    return jnp.moveaxis(t, axis, 0)


def _from_parts(parts, axis):
    t = jnp.moveaxis(parts, 0, axis)
    shp = t.shape
    return t.reshape(*shp[:axis], shp[axis] * shp[axis + 1], *shp[axis + 2:])


BIG = tuple(n for n, (_, mm) in SHARDED.items() if mm)
SMALL = tuple(n for n, (_, mm) in SHARDED.items() if not mm)
SMALL_W = 512


def _small_rows(tree, lead):
    rows = []
    for n in SMALL:
        t = tree[n]
        t = t.reshape(*lead, -1, t.shape[-1])
        rows.append(jnp.pad(t, [(0, 0)] * (t.ndim - 1) + [(0, SMALL_W - t.shape[-1])]))
    buf = jnp.concatenate(rows, axis=-2)
    return jnp.pad(buf, [(0, 0)] * (buf.ndim - 2) + [(0, 16 - buf.shape[-2]), (0, 0)])


def _small_unrows(buf, like, lead):
    out, r = {}, 0
    for n in SMALL:
        shp = like[n].shape
        k = like[n].size // shp[-1]
        out[n] = buf[..., r:r + k, :shp[-1]].reshape(*lead, *shp)
        r += k
    return out


def _gather_weights(local):
    shards = [local[n].astype(BF16) for n in BIG] + [_small_rows(local, ())]
    got = _all_gather(shards, "gather_weights")
    whole = {n: _from_parts(g, SHARDED[n][0]) for n, g in zip(BIG, got)}
    small = _small_unrows(got[-1], local, (N_DEV,))
    whole.update({n: _from_parts(small[n], SHARDED[n][0]) for n in SMALL})
    return whole


def _pack_small(tree, extra):
    flat = jnp.concatenate([tree[n].reshape(-1) for n in REPLICATED] + [extra.reshape(-1)])
    return _pad_rows(flat, 8)


def _unpack_small(pack, like):
    flat, out, off = pack.reshape(-1), {}, 0
    for n in REPLICATED:
        out[n] = flat[off:off + like[n].size].reshape(like[n].shape)
        off += like[n].size
    return out, flat[off]


def kernel(x, norm1_w, norm2_w, a_w_in, a_lb_logits, a_norm_w, a_w_out, b_w_in, b_conv_w, b_conv_b, b_dt_bias, b_a_log, b_d_skip, b_norm_w, b_w_out, ffn_w_in, ffn_conv_w, ffn_conv_b, ffn_w_out, final_norm_w, loss_target, m_norm1_w, m_norm2_w, m_a_w_in, m_a_lb_logits, m_a_norm_w, m_a_w_out, m_b_w_in, m_b_conv_w, m_b_conv_b, m_b_dt_bias, m_b_a_log, m_b_d_skip, m_b_norm_w, m_b_w_out, m_ffn_w_in, m_ffn_conv_w, m_ffn_conv_b, m_ffn_w_out, m_final_norm_w, v_norm1_w, v_norm2_w, v_a_w_in, v_a_lb_logits, v_a_norm_w, v_a_w_out, v_b_w_in, v_b_conv_w, v_b_conv_b, v_b_dt_bias, v_b_a_log, v_b_d_skip, v_b_norm_w, v_b_w_out, v_ffn_w_in, v_ffn_conv_w, v_ffn_conv_b, v_ffn_w_out, v_final_norm_w):
    given = dict(locals())
    local = {n: given[n] for n in WEIGHTS}
    mom = {n: given["m_" + n] for n in WEIGHTS}
    var = {n: given["v_" + n] for n in WEIGHTS}

    whole = dict(local)
    whole.update(_gather_weights(local))
    loss, grad_x, grads = _local_step(x[0], loss_target[0], whole)

    parts = {n: _to_parts(grads[n], SHARDED[n][0]) for n in SHARDED}
    sent = [parts[n].reshape(N_DEV, -1, parts[n].shape[-1]) for n in BIG] + [_small_rows(parts, (N_DEV,))]
    got = _exchange_parts(sent, "exchange_grads")
    out_g, out_d, out_m, out_v = {}, {}, {}, {}

    def update(got_n, trees, shapes, name):
        w2, m2, v2 = (t.reshape(got_n.shape[1:]) for t in trees)
        res = _adamw(got_n, w2, m2, v2, name=name, tile=_pick(got_n.shape[1], (256, 176, 128, 16)))
        return [r.reshape(shapes) for r in res]

    for n, got_n in zip(BIG, got):
        out_g[n], out_d[n], out_m[n], out_v[n] = update(got_n, (local[n], mom[n], var[n]), local[n].shape, "adamw_" + n)
    res = update(got[-1], [_small_rows(t, ()) for t in (local, mom, var)], (16, SMALL_W), "adamw_small")
    for out, r in zip((out_g, out_d, out_m, out_v), res):
        out.update(_small_unrows(r, local, ()))

    small = _pack_small(grads, loss)
    rows = small.shape[0]
    got_s = _all_gather([small], "gather_small")[0]
    zero = jnp.zeros((1,), F32)
    g, dlt, nm, nv = _adamw(got_s, _pack_small(local, zero), _pack_small(mom, zero), _pack_small(var, zero),
                            name="adamw_replicated", tile=rows)
    (rep_g, total), (rep_d, _), (rep_m, _), (rep_v, _) = (_unpack_small(t, local) for t in (g, dlt, nm, nv))
    out_g.update(rep_g)
    out_d.update(rep_d)
    out_m.update(rep_m)
    out_v.update(rep_v)

    return (total, grad_x[None], *[out_g[n] for n in WEIGHTS], *[out_d[n] for n in WEIGHTS],
            *[out_m[n] for n in WEIGHTS], *[out_v[n] for n in WEIGHTS])
```

```python
import functools

import jax
import jax.numpy as jnp
from jax import lax
from jax.experimental import pallas as pl
from jax.experimental.pallas import tpu as pltpu

F32 = jnp.float32
BF16 = jnp.bfloat16
HIGHEST = lax.Precision.HIGHEST
MESH = pl.DeviceIdType.MESH

N_DEV = 8
EPS = 1e-6
D_MODEL = 1024
HG_HEADS = 8
HG_HEAD_DIM = 128
D_INNER = 2048
SSM_HEAD_DIM = 64
SSM_GROUPS = 8
SSM_HPG = 4
SSM_STATE = 128
GN = SSM_GROUPS * SSM_STATE
CONV_DIM = D_INNER + 2 * GN
D_FF = 2816
SSM_CONV = 5
FFN_CONV = 3

ADAM_LR = 0.001
ADAM_B1 = 0.9
ADAM_B2 = 0.999
ADAM_EPS = 1e-08
ADAM_WD = 0.01
ADAM_STEP = 10

LANES = 128
ROW_TILE = 256
COL_TILE = 128
GLA_CHUNK = 16
GLA_BLOCK = 512
GLA_HEADS_PER_STEP = 4
SSD_CHUNK = 128
SSD_BLOCK = 512
PACK_W = 1024
VMEM_LIMIT = 56 * 1024 * 1024

NT_DIMS = (((1,), (1,)), ((), ()))
TN_DIMS = (((0,), (0,)), ((), ()))


def _cparams(*sem):
    return pltpu.CompilerParams(dimension_semantics=sem, vmem_limit_bytes=VMEM_LIMIT)


def _rms(x, w):
    return x * lax.rsqrt(jnp.mean(x * x, axis=-1, keepdims=True) + EPS) * w


def _row_kernel(body_fn, rows, params, row_outs, acc_outs, *, name, tile=ROW_TILE):
    L = rows[0][0].shape[0]
    tile = min(tile, L)
    n_in = len(rows) + len(params)
    n_ro = len(row_outs)

    def body(*refs):
        outs = body_fn(*[r[...] for r in refs[:n_in]])
        for ref, o in zip(refs[n_in:n_in + n_ro], outs[:n_ro]):
            ref[...] = o.astype(ref.dtype)
        first = pl.program_id(0) == 0
        for ref, o in zip(refs[n_in + n_ro:], outs[n_ro:]):
            @pl.when(first)
            def _(ref=ref):
                ref[...] = jnp.zeros(ref.shape, ref.dtype)
            ref[...] += o

    in_specs = [pl.BlockSpec((tile, w), lambda i, cb=cb: (i, cb)) for _, w, cb in rows]
    in_specs += [pl.BlockSpec(p.shape, lambda i: (0, 0)) for p in params]
    out_specs = [pl.BlockSpec((tile, w), lambda i: (i, 0)) for w, _ in row_outs]
    out_specs += [pl.BlockSpec(s, lambda i: (0, 0)) for s in acc_outs]
    out_shape = [jax.ShapeDtypeStruct((L, w), dt) for w, dt in row_outs]
    out_shape += [jax.ShapeDtypeStruct(s, F32) for s in acc_outs]
    return pl.pallas_call(
        body, name=name, grid=(L // tile,), in_specs=in_specs, out_specs=out_specs, out_shape=out_shape,
        compiler_params=_cparams("arbitrary" if acc_outs else "parallel"),
    )(*[a for a, _, _ in rows], *params)


def _col_kernel(body_fn, cols, params, col_outs, par_outs, *, name, n_tiles):
    L = cols[0][0].shape[0]
    n_in = len(cols) + len(params)
    width = n_tiles * COL_TILE

    def body(*refs):
        outs = body_fn(*[r[...] for r in refs[:n_in]])
        for ref, o in zip(refs[n_in:], outs):
            ref[...] = o.astype(ref.dtype)

    in_specs = [pl.BlockSpec((L, COL_TILE), lambda j, cb=cb: (0, cb + j)) for _, cb in cols]
    in_specs += [pl.BlockSpec((p.shape[0], COL_TILE), lambda j, cb=cb: (0, cb + j)) for p, cb in params]
    out_specs = [pl.BlockSpec((L, COL_TILE), lambda j: (0, j)) for _ in col_outs]
    out_specs += [pl.BlockSpec((k, COL_TILE), lambda j: (0, j)) for k in par_outs]
    out_shape = [jax.ShapeDtypeStruct((L, width), dt) for dt in col_outs]
    out_shape += [jax.ShapeDtypeStruct((k, width), F32) for k in par_outs]
    return pl.pallas_call(
        body, name=name, grid=(n_tiles,), in_specs=in_specs, out_specs=out_specs, out_shape=out_shape,
        compiler_params=_cparams("parallel"),
    )(*[a for a, _ in cols], *[p for p, _ in params])


def _pick(n, options):
    for t in options:
        if n % t == 0:
            return t
    return n


MATMUL_VMEM = 40 * 1024 * 1024


def _matmul_tiles(M, N, K, out_bytes):
    best = None
    for tm in (1024, 512, 256, 128, M):
        for tn in (1408, 1024, 512, 256, 128, N):
            if M % tm or N % tn:
                continue
            if 2 * (2 * K * (tm + tn) + out_bytes * tm * tn) > MATMUL_VMEM:
                continue
            if best is None or tm * tn > best[0] * best[1]:
                best = (tm, tn)
    return best


def _matmul(a, b, *, name, nt=False, add=None, out_dtype=F32, after=None):
    M, K = a.shape
    N = b.shape[0] if nt else b.shape[1]
    tm, tn = _matmul_tiles(M, N, K, 4 * (1 + (add is not None)) if out_dtype == F32 else 2 + 4 * (add is not None))

    def body(*refs):
        a_ref, b_ref = refs[0], refs[1]
        o_ref = refs[-1]
        if nt:
            acc = lax.dot_general(a_ref[...], b_ref[...], NT_DIMS, preferred_element_type=F32)
        else:
            acc = jnp.dot(a_ref[...], b_ref[...], preferred_element_type=F32)
        if add is not None:
            acc = acc + refs[2][...]
        o_ref[...] = acc.astype(o_ref.dtype)

    in_specs = [pl.BlockSpec((tm, K), lambda i, j: (i, 0)),
                pl.BlockSpec((tn, K), lambda i, j: (j, 0)) if nt else pl.BlockSpec((K, tn), lambda i, j: (0, j))]
    args = [a, b]
    if add is not None:
        in_specs.append(pl.BlockSpec((tm, tn), lambda i, j: (i, j)))
        args.append(add)
    if after is not None:
        in_specs.append(pl.BlockSpec(memory_space=pl.ANY))
        args.append(after)
    return pl.pallas_call(
        body, name=name, grid=(M // tm, N // tn), in_specs=in_specs,
        out_specs=pl.BlockSpec((tm, tn), lambda i, j: (i, j)),
        out_shape=jax.ShapeDtypeStruct((M, N), out_dtype),
        compiler_params=_cparams("parallel", "parallel"),
    )(*args)


def _hgrn2_pre_fn(q_raw, ffw_raw, fbw_raw, lb_logits):
    lb = jax.nn.softmax(lb_logits, axis=0)[0:1]

    def gate(fr):
        f = lb + (1.0 - lb) * jax.nn.sigmoid(fr)
        return jnp.log(f), 1.0 - f

    lf_fw, k_fw = gate(ffw_raw)
    lf_bw, k_bw = gate(fbw_raw)
    return jax.nn.silu(q_raw), lf_fw, k_fw, lf_bw, k_bw


def _hgrn2_post_fn(o, g, norm_w):
    outs = []
    for h in range(HG_HEADS):
        sl = slice(h * HG_HEAD_DIM, (h + 1) * HG_HEAD_DIM)
        outs.append(_rms(o[:, sl], norm_w) * jax.nn.silu(g[:, sl]))
    return jnp.concatenate(outs, axis=1)


def _mamba_post_fn(y, z, norm_w):
    outs = []
    gw = D_INNER // SSM_GROUPS
    for gi in range(SSM_GROUPS):
        sl = slice(gi * gw, (gi + 1) * gw)
        outs.append(_rms(y[:, sl] * jax.nn.silu(z[:, sl]), norm_w[:, sl]))
    return jnp.concatenate(outs, axis=1)


def _shift_rows_impl(x, d):
    if d == 0:
        return x
    n = x.shape[0]
    t = lax.broadcasted_iota(jnp.int32, x.shape, 0)
    rolled = pltpu.roll(x, d % n, 0)
    return jnp.where((t >= d) if d > 0 else (t < n + d), rolled, 0.0)


@functools.partial(jax.custom_vjp, nondiff_argnums=(1,))
def _shift_rows(x, d):
    return _shift_rows_impl(x, d)


_shift_rows.defvjp(lambda x, d: (_shift_rows_impl(x, d), None), lambda d, _, g: (_shift_rows_impl(g, -d),))


def _dwconv(x, w, b):
    taps = w.shape[0]
    c = (taps - 1) // 2
    y = b + w[0:1, :] * _shift_rows(x, c)
    for k in range(1, taps):
        y = y + w[k:k + 1, :] * _shift_rows(x, c - k)
    return y


def _ffn_mid_fn(gate, val, w, b):
    return jax.nn.silu(_dwconv(gate, w, b)) * val


def _mamba_conv_fn(xbc, w, b):
    return jax.nn.silu(_dwconv(xbc, w, b))


def _gla_consts(rev):
    return lax.broadcasted_iota(jnp.int32, (GLA_CHUNK, HG_HEAD_DIM), 0)


def _segment_cumsum(x, seg, rev):
    n = x.shape[0]
    t = lax.broadcasted_iota(jnp.int32, x.shape, 0) & (seg - 1)
    s = 1
    while s < seg:
        if rev:
            x = x + jnp.where(t < seg - s, pltpu.roll(x, n - s, 0), 0.0)
        else:
            x = x + jnp.where(t >= s, pltpu.roll(x, s, 0), 0.0)
        s *= 2
    return x


@functools.partial(jax.custom_vjp, nondiff_argnums=(1, 2))
def _segment_cumsum_diff(x, seg, rev):
    return _segment_cumsum(x, seg, rev)


_segment_cumsum_diff.defvjp(lambda x, seg, rev: (_segment_cumsum(x, seg, rev), None),
                            lambda seg, rev, _, g: (_segment_cumsum(g, seg, not rev),))


def _gla_chunk(st, q, k, v, b, *, rev, consts):
    row = consts
    c = q.shape[0]
    o = lax.dot_general((q * jnp.exp(b)).astype(BF16), st.astype(BF16), NT_DIMS, preferred_element_type=F32)
    for s in range(c):
        e = jnp.exp(jnp.where((row <= s) if rev else (row >= s), b - b[s:s + 1], -jnp.inf))
        o = o + jnp.sum(q * (k[s:s + 1] * e), axis=-1, keepdims=True) * v[s:s + 1]
    b_end = b[0:1] if rev else b[c - 1:c]
    kd = (k * jnp.exp(b_end - b)).astype(BF16)
    st_new = st * jnp.exp(b_end) + lax.dot_general(v.astype(BF16), kd, TN_DIMS, preferred_element_type=F32)
    return st_new, o


def _gla_fwd(q, k, v, v_cb, g, *, rev, add, name):
    L = q.shape[0]
    blk = min(GLA_BLOCK, L)
    nblk, nsub = L // blk, blk // GLA_CHUNK
    hd, hps = HG_HEAD_DIM, GLA_HEADS_PER_STEP
    wide = hd * hps

    def body(*refs):
        q_ref, k_ref, v_ref, g_ref = refs[:4]
        add_ref = refs[4] if add is not None else None
        o_ref, st_out, st_scr, b_scr = refs[-4:]
        consts = _gla_consts(rev)

        @pl.when(pl.program_id(1) == 0)
        def _():
            st_scr[...] = jnp.zeros(st_scr.shape, F32)

        st_out[...] = st_scr[...]
        b_scr[...] = _segment_cumsum(g_ref[...], GLA_CHUNK, rev)

        def step(i, carry):
            sl = pl.ds(pl.multiple_of((nsub - 1 - i if rev else i) * GLA_CHUNK, GLA_CHUNK), GLA_CHUNK)
            lanes = [slice(hi * hd, (hi + 1) * hd) for hi in range(hps)]
            ins = [(st_scr[hi], q_ref[sl, ln], k_ref[sl, ln], v_ref[sl, ln], b_scr[sl, ln])
                   for hi, ln in enumerate(lanes)]
            adds = [add_ref[sl, ln] for ln in lanes] if add_ref is not None else None
            outs = [_gla_chunk(*args, rev=rev, consts=consts) for args in ins]
            for hi, ln in enumerate(lanes):
                st_scr[hi] = outs[hi][0]
                o_ref[sl, ln] = outs[hi][1] if adds is None else outs[hi][1] + adds[hi]
            return carry

        lax.fori_loop(0, nsub, step, 0)

    def pos(j):
        return nblk - 1 - j if rev else j

    spec = pl.BlockSpec((blk, wide), lambda h, j: (pos(j), h))
    in_specs = [spec, spec, pl.BlockSpec((blk, wide), lambda h, j: (pos(j), v_cb // hps + h)), spec]
    args = [q, k, v, g]
    if add is not None:
        in_specs.append(spec)
        args.append(add)
    return pl.pallas_call(
        body, name=name, grid=(HG_HEADS // hps, nblk), in_specs=in_specs,
        out_specs=[spec, pl.BlockSpec((hps, None, hd, hd), lambda h, j: (h, j, 0, 0))],
        out_shape=[jax.ShapeDtypeStruct((L, HG_HEADS * hd), F32),
                   jax.ShapeDtypeStruct((HG_HEADS, nblk, hd, hd), F32)],
        scratch_shapes=[pltpu.VMEM((hps, hd, hd), F32), pltpu.VMEM((blk, wide), F32)],
        compiler_params=_cparams("parallel", "arbitrary"),
    )(*args)


def _gla_bwd(q, k, v, v_cb, g, do, states, *, rev, adds, name):
    L = q.shape[0]
    blk = min(GLA_BLOCK, L)
    nblk, nsub = L // blk, blk // GLA_CHUNK
    hd, hps = HG_HEAD_DIM, GLA_HEADS_PER_STEP
    wide = hd * hps
    n_add = 0 if adds is None else 2

    def body(*refs):
        q_ref, k_ref, v_ref, g_ref, do_ref, st_in = refs[:6]
        add_refs = refs[6:6 + n_add]
        dq_ref, dk_ref, dv_ref, dg_ref, dst_scr, sub_scr, b_scr, db_scr = refs[6 + n_add:]
        consts = _gla_consts(rev)
        chunk = functools.partial(_gla_chunk, rev=rev, consts=consts)

        @pl.when(pl.program_id(1) == 0)
        def _():
            dst_scr[...] = jnp.zeros(dst_scr.shape, F32)

        b_scr[...] = _segment_cumsum(g_ref[...], GLA_CHUNK, rev)

        def rows(i):
            return pl.ds(pl.multiple_of((nsub - 1 - i if rev else i) * GLA_CHUNK, GLA_CHUNK), GLA_CHUNK)

        for hi in range(hps):
            sub_scr[hi] = st_in[hi]

        def replay(i, carry):
            sl = rows(i)
            lanes = [slice(hi * hd, (hi + 1) * hd) for hi in range(hps)]
            ins = [(sub_scr[i * hps + hi], q_ref[sl, ln], k_ref[sl, ln], v_ref[sl, ln], b_scr[sl, ln])
                   for hi, ln in enumerate(lanes)]
            outs = [chunk(*args)[0] for args in ins]
            for hi in range(hps):
                sub_scr[(i + 1) * hps + hi] = outs[hi]
            return carry

        lax.fori_loop(0, nsub - 1, replay, 0)

        def step(n, carry):
            i = nsub - 1 - n
            sl = rows(i)
            lanes = [slice(hi * hd, (hi + 1) * hd) for hi in range(hps)]
            ins = [(sub_scr[i * hps + hi], q_ref[sl, ln], k_ref[sl, ln], v_ref[sl, ln], b_scr[sl, ln])
                   for hi, ln in enumerate(lanes)]
            cts = [(dst_scr[hi], do_ref[sl, ln]) for hi, ln in enumerate(lanes)]
            adds = [(add_refs[0][sl, ln], add_refs[1][sl, ln]) for ln in lanes] if n_add else None
            outs = [jax.vjp(chunk, *args)[1](ct) for args, ct in zip(ins, cts)]
            for hi, ln in enumerate(lanes):
                dst, dq, dk, dv, db = outs[hi]
                dst_scr[hi] = dst
                dq_ref[sl, ln] = dq if adds is None else dq + adds[hi][0]
                dk_ref[sl, ln] = dk
                dv_ref[sl, ln] = dv if adds is None else dv + adds[hi][1]
                db_scr[sl, ln] = db
            return carry

        lax.fori_loop(0, nsub, step, 0)
        dg_ref[...] = _segment_cumsum(db_scr[...], GLA_CHUNK, not rev)

    def pos(j):
        p = nblk - 1 - j
        return nblk - 1 - p if rev else p

    spec = pl.BlockSpec((blk, wide), lambda h, j: (pos(j), h))
    in_specs = [spec, spec, pl.BlockSpec((blk, wide), lambda h, j: (pos(j), v_cb // hps + h)), spec, spec,
                pl.BlockSpec((hps, None, hd, hd), lambda h, j: (h, nblk - 1 - j, 0, 0))]
    args = [q, k, v, g, do, states]
    if adds is not None:
        in_specs += [spec, spec]
        args += list(adds)
    out = jax.ShapeDtypeStruct((L, HG_HEADS * hd), F32)
    return pl.pallas_call(
        body, name=name, grid=(HG_HEADS // hps, nblk), in_specs=in_specs,
        out_specs=[spec] * 4, out_shape=[out] * 4,
        scratch_shapes=[pltpu.VMEM((hps, hd, hd), F32), pltpu.VMEM((nsub * hps, hd, hd), F32),
                        pltpu.VMEM((blk, wide), F32), pltpu.VMEM((blk, wide), F32)],
        compiler_params=_cparams("parallel", "arbitrary"),
    )(*args)


def _ssd_consts(rev):
    c = SSD_CHUNK
    gw = SSM_HPG * SSM_HEAD_DIM
    r2 = lax.broadcasted_iota(jnp.int32, (c, c), 0)
    c2 = lax.broadcasted_iota(jnp.int32, (c, c), 1)
    low = (r2 <= c2) if rev else (r2 >= c2)
    lane = lax.broadcasted_iota(jnp.int32, (1, gw), 1)
    return low, r2 == c2, lane, lane >> 6


def _expand_heads(v4, head_of_lane):
    lane4 = lax.broadcasted_iota(jnp.int32, v4.shape, 1)
    out = None
    for j in range(SSM_HPG):
        col = jnp.sum(jnp.where(lane4 == j, v4, 0.0), axis=-1, keepdims=True)
        term = jnp.where(head_of_lane == j, col, 0.0)
        out = term if out is None else out + term
    return out


def _ssd_chunk(st, x, bm, cm, dtr, bias, alog, dsk, *, rev, consts, skip):
    low, eye, lane, head_of_lane = consts
    c = x.shape[0]
    dt_l = _expand_heads(jax.nn.softplus(dtr + bias), head_of_lane)
    a_l = _expand_heads(-jnp.exp(alog), head_of_lane)
    acum = _segment_cumsum_diff(dt_l * a_l, c, rev)
    xd = x * dt_l
    cb = lax.dot_general(cm.astype(BF16), bm.astype(BF16), NT_DIMS, preferred_element_type=F32)
    y = jnp.dot(cm.astype(BF16), st.astype(BF16), preferred_element_type=F32) * jnp.exp(acum)
    for j in range(SSM_HPG):
        acol = jnp.sum(jnp.where(lane == j * SSM_HEAD_DIM, acum, 0.0), axis=-1, keepdims=True)
        ab = jnp.broadcast_to(acol, (c, c))
        arow = jnp.sum(jnp.where(eye, ab, 0.0), axis=0, keepdims=True)
        lmat = jnp.exp(jnp.where(low, ab - arow, -jnp.inf))
        xj = jnp.where(head_of_lane == j, xd, 0.0)
        y = y + jnp.dot((cb * lmat).astype(BF16), xj.astype(BF16), preferred_element_type=F32)
    a_end = acum[0:1] if rev else acum[c - 1:c]
    xdec = (xd * jnp.exp(a_end - acum)).astype(BF16)
    st_new = st * jnp.exp(a_end) + lax.dot_general(bm.astype(BF16), xdec, TN_DIMS, preferred_element_type=F32)
    if skip:
        y = y + x * _expand_heads(dsk, head_of_lane)
    return st_new, y


def _ssd_specs(L, rev):
    blk = min(SSD_BLOCK, L)
    nblk = L // blk
    gw = SSM_HPG * SSM_HEAD_DIM

    def pos(j):
        return nblk - 1 - j if rev else j

    return blk, nblk, gw, pos


def _ssd_fwd(xbc, dtr, bias, alog, dsk, *, rev, skip, add, name):
    L = xbc.shape[0]
    blk, nblk, gw, pos = _ssd_specs(L, rev)
    nsub = blk // SSD_CHUNK
    n = SSM_STATE

    def body(*refs):
        x_ref, b_ref, c_ref, dt_ref, bias_ref, alog_ref, dsk_ref = refs[:7]
        add_ref = refs[7] if add is not None else None
        y_ref, st_out, st_scr = refs[-3:]
        consts = _ssd_consts(rev)

        @pl.when(pl.program_id(1) == 0)
        def _():
            st_scr[...] = jnp.zeros(st_scr.shape, F32)

        st_out[...] = st_scr[...]

        def step(i, carry):
            sl = pl.ds(pl.multiple_of((nsub - 1 - i if rev else i) * SSD_CHUNK, SSD_CHUNK), SSD_CHUNK)
            st_new, y = _ssd_chunk(st_scr[...], x_ref[sl, :], b_ref[sl, :], c_ref[sl, :], dt_ref[sl, :],
                                   bias_ref[...], alog_ref[...], dsk_ref[...], rev=rev, consts=consts, skip=skip)
            st_scr[...] = st_new
            if add_ref is not None:
                y = y + add_ref[sl, :]
            y_ref[sl, :] = y
            return carry

        lax.fori_loop(0, nsub, step, 0)

    b0 = D_INNER // n
    yspec = pl.BlockSpec((blk, gw), lambda g, j: (pos(j), g))
    pspec = pl.BlockSpec((None, 1, SSM_HPG), lambda g, j: (g, 0, 0))
    in_specs = [yspec,
                pl.BlockSpec((blk, n), lambda g, j: (pos(j), b0 + g)),
                pl.BlockSpec((blk, n), lambda g, j: (pos(j), b0 + SSM_GROUPS + g)),
                pl.BlockSpec((None, blk, SSM_HPG), lambda g, j: (g, pos(j), 0)),
                pspec, pspec, pspec]
    args = [xbc, xbc, xbc, dtr, bias, alog, dsk]
    if add is not None:
        in_specs.append(yspec)
        args.append(add)
    return pl.pallas_call(
        body, name=name, grid=(SSM_GROUPS, nblk), in_specs=in_specs,
        out_specs=[yspec, pl.BlockSpec((None, None, n, gw), lambda g, j: (g, j, 0, 0))],
        out_shape=[jax.ShapeDtypeStruct((L, D_INNER), F32),
                   jax.ShapeDtypeStruct((SSM_GROUPS, nblk, n, gw), F32)],
        scratch_shapes=[pltpu.VMEM((n, gw), F32)],
        compiler_params=_cparams("parallel", "arbitrary"),
    )(*args)


def _ssd_bwd(xbc, dtr, bias, alog, dsk, dy, states, *, rev, skip, adds, name):
    L = xbc.shape[0]
    blk, nblk, gw, _ = _ssd_specs(L, rev)
    nsub = blk // SSD_CHUNK
    n = SSM_STATE
    n_add = 0 if adds is None else 3

    def body(*refs):
        x_ref, b_ref, c_ref, dt_ref, bias_ref, alog_ref, dsk_ref, dy_ref, st_in = refs[:9]
        add_refs = refs[9:9 + n_add]
        dx_ref, db_ref, dc_ref, ddt_ref, dbias_ref, dalog_ref, ddsk_ref, dst_scr, sub_scr = refs[9 + n_add:]
        consts = _ssd_consts(rev)
        chunk = functools.partial(_ssd_chunk, rev=rev, consts=consts, skip=skip)

        @pl.when(pl.program_id(1) == 0)
        def _():
            dst_scr[...] = jnp.zeros(dst_scr.shape, F32)
            dbias_ref[...] = jnp.zeros(dbias_ref.shape, F32)
            dalog_ref[...] = jnp.zeros(dalog_ref.shape, F32)
            ddsk_ref[...] = jnp.zeros(ddsk_ref.shape, F32)

        def rows(i):
            return pl.ds(pl.multiple_of((nsub - 1 - i if rev else i) * SSD_CHUNK, SSD_CHUNK), SSD_CHUNK)

        def operands(i):
            sl = rows(i)
            return (x_ref[sl, :], b_ref[sl, :], c_ref[sl, :], dt_ref[sl, :], bias_ref[...], alog_ref[...], dsk_ref[...])

        def replay(i, st):
            sub_scr[i] = st
            return chunk(st, *operands(i))[0]

        lax.fori_loop(0, nsub, replay, st_in[...])

        def step(k, carry):
            i = nsub - 1 - k
            sl = rows(i)
            _, vjp = jax.vjp(chunk, sub_scr[i], *operands(i))
            dst, dx, db, dc, ddt, dbias, dalog, ddsk = vjp((dst_scr[...], dy_ref[sl, :]))
            dst_scr[...] = dst
            if n_add:
                dx = dx + add_refs[0][sl, :]
                db = db + add_refs[1][sl, :]
                dc = dc + add_refs[2][sl, :]
            dx_ref[sl, :] = dx
            db_ref[sl, :] = db
            dc_ref[sl, :] = dc
            ddt_ref[sl, :] = ddt
            dbias_ref[...] += dbias
            dalog_ref[...] += dalog
            ddsk_ref[...] += ddsk
            return carry

        lax.fori_loop(0, nsub, step, 0)

    def pos(j):
        p = nblk - 1 - j
        return nblk - 1 - p if rev else p

    b0 = D_INNER // n
    xspec = pl.BlockSpec((blk, gw), lambda g, j: (pos(j), g))
    nspec = pl.BlockSpec((blk, n), lambda g, j: (pos(j), g))
    dtspec = pl.BlockSpec((None, blk, SSM_HPG), lambda g, j: (g, pos(j), 0))
    pspec = pl.BlockSpec((None, 1, SSM_HPG), lambda g, j: (g, 0, 0))
    in_specs = [xspec,
                pl.BlockSpec((blk, n), lambda g, j: (pos(j), b0 + g)),
                pl.BlockSpec((blk, n), lambda g, j: (pos(j), b0 + SSM_GROUPS + g)),
                dtspec, pspec, pspec, pspec, xspec,
                pl.BlockSpec((None, None, n, gw), lambda g, j: (g, nblk - 1 - j, 0, 0))]
    args = [xbc, xbc, xbc, dtr, bias, alog, dsk, dy, states]
    if adds is not None:
        in_specs += [xspec, nspec, nspec]
        args += list(adds)
    par = jax.ShapeDtypeStruct((SSM_GROUPS, 1, SSM_HPG), F32)
    return pl.pallas_call(
        body, name=name, grid=(SSM_GROUPS, nblk), in_specs=in_specs,
        out_specs=[xspec, nspec, nspec, dtspec, pspec, pspec, pspec],
        out_shape=[jax.ShapeDtypeStruct((L, D_INNER), F32), jax.ShapeDtypeStruct((L, GN), F32),
                   jax.ShapeDtypeStruct((L, GN), F32), jax.ShapeDtypeStruct((SSM_GROUPS, L, SSM_HPG), F32),
                   par, par, par],
        scratch_shapes=[pltpu.VMEM((n, gw), F32), pltpu.VMEM((nsub, n, gw), F32)],
        compiler_params=_cparams("parallel", "arbitrary"),
    )(*args)


def _norm_fwd(h, w, name):
    d = h.shape[1]
    return _row_kernel(lambda hv, wv: (_rms(hv, wv),), [(h, d, 0)], [w], [(d, BF16)], [], name=name)[0]


def _norm_bwd(h, w, du, dh_in, name):
    d = h.shape[1]

    def fn(hv, duv, dhv, wv):
        _, vjp = jax.vjp(_rms, hv, wv)
        dh, dw = vjp(duv.astype(F32))
        dh = dh + dhv
        return dh, dh, dw

    return _row_kernel(fn, [(h, d, 0), (du, d, 0), (dh_in, d, 0)], [w], [(d, F32), (d, BF16)], [(1, d)], name=name)


def _ffn_fwd(h, norm_w, w_gate, w_val, w_out, conv_w, conv_b, tag):
    u = _norm_fwd(h, norm_w, f"{tag}_norm")
    pg = _matmul(u, w_gate, name=f"{tag}_in_gate")
    pv = _matmul(u, w_val, name=f"{tag}_in_val")
    y = _col_kernel(lambda g, v, w, b: (_ffn_mid_fn(g, v, w, b),), [(pg, 0), (pv, 0)], [(conv_w, 0), (conv_b, 0)],
                    [BF16], [], name=f"{tag}_mid", n_tiles=D_FF // COL_TILE)[0]
    h_out = _matmul(y, w_out, add=h, name=f"{tag}_out")
    return h_out, (u, pg, pv, y)


def _ffn_bwd(h, dh, dh16, saved, norm_w, w_gate, w_val, w_out, conv_w, conv_b, tag, after=None):
    u, pg, pv, y = saved
    dy = _matmul(dh16, w_out, nt=True, after=after, name=f"{tag}_out_dx")
    dw_out = _matmul(y.T, dh16, name=f"{tag}_out_dw")

    def fn(g, v, ct, w, b):
        _, vjp = jax.vjp(_ffn_mid_fn, g, v, w, b)
        return vjp(ct)

    dpg, dpv, dcw, dcb = _col_kernel(fn, [(pg, 0), (pv, 0), (dy, 0)], [(conv_w, 0), (conv_b, 0)], [BF16, BF16],
                                     [FFN_CONV, 1], name=f"{tag}_mid_bwd", n_tiles=D_FF // COL_TILE)
    du = _matmul(dpg, w_gate, nt=True, name=f"{tag}_gate_dx")
    du = _matmul(dpv, w_val, nt=True, add=du, out_dtype=BF16, name=f"{tag}_val_dx")
    ut = u.T
    dw_gate = _matmul(ut, dpg, name=f"{tag}_gate_dw")
    dw_val = _matmul(ut, dpv, name=f"{tag}_val_dw")
    dh, dh16, dnw = _norm_bwd(h, norm_w, du, dh, f"{tag}_norm_bwd")
    return dh, dh16, dict(w_in=jnp.concatenate([dw_gate, dw_val], axis=1), w_out=dw_out, conv_w=dcw, conv_b=dcb, norm=dnw)


def _hgrn2_fwd(h, norm_w, w_in, lb_logits, a_norm_w, w_out, after=None):
    d = D_MODEL
    u = _norm_fwd(h, norm_w, "hg_norm")
    pa = _matmul(u, w_in, after=after, name="hg_in")
    qs, lf_fw, k_fw, lf_bw, k_bw = _row_kernel(
        _hgrn2_pre_fn, [(pa, d, 0), (pa, d, 1), (pa, d, 2)], [lb_logits], [(d, F32)] * 5, [], name="hg_pre")
    o_fw, st_fw = _gla_fwd(qs, k_fw, pa, 3 * HG_HEADS, lf_fw, rev=False, add=None, name="hg_gla_fw")
    o, st_bw = _gla_fwd(qs, k_bw, pa, 3 * HG_HEADS, lf_bw, rev=True, add=o_fw, name="hg_gla_bw")
    y = _row_kernel(lambda ov, gv, wv: (_hgrn2_post_fn(ov, gv, wv),), [(o, d, 0), (pa, d, 4)], [a_norm_w],
                    [(d, BF16)], [], name="hg_post")[0]
    h_out = _matmul(y, w_out, add=h, name="hg_out")
    return h_out, (u, pa, qs, lf_fw, k_fw, lf_bw, k_bw, st_fw, st_bw, o, y)


def _hgrn2_bwd(h, dh, dh16, saved, norm_w, w_in, lb_logits, a_norm_w, w_out, send, after=None):
    d = D_MODEL
    u, pa, qs, lf_fw, k_fw, lf_bw, k_bw, st_fw, st_bw, o, y = saved
    dy = _matmul(dh16, w_out, nt=True, after=after, name="hg_out_dx")
    dw_out = _matmul(y.T, dh16, name="hg_out_dw")

    def post_bwd(ov, gv, ct, wv):
        _, vjp = jax.vjp(_hgrn2_post_fn, ov, gv, wv)
        return vjp(ct)

    do, dg, dnw = _row_kernel(post_bwd, [(o, d, 0), (pa, d, 4), (dy, d, 0)], [a_norm_w], [(d, F32), (d, F32)],
                              [(1, HG_HEAD_DIM)], name="hg_post_bwd")
    dq1, dk_fw, dv1, dlf_fw = _gla_bwd(qs, k_fw, pa, 3 * HG_HEADS, lf_fw, do, st_fw, rev=False, adds=None,
                                       name="hg_gla_fw_bwd")
    dqs, dk_bw, dv, dlf_bw = _gla_bwd(qs, k_bw, pa, 3 * HG_HEADS, lf_bw, do, st_bw, rev=True, adds=(dq1, dv1),
                                      name="hg_gla_bw_bwd")

    def pre_bwd(qr, fr, br, c0, c1, c2, c3, c4, dvv, dgv, lbl):
        _, vjp = jax.vjp(_hgrn2_pre_fn, qr, fr, br, lbl)
        dq, df, db, dlbl = vjp((c0, c1, c2, c3, c4))
        return jnp.concatenate([dq, df, db, dvv, dgv], axis=1), dlbl

    rows = [(pa, d, 0), (pa, d, 1), (pa, d, 2), (dqs, d, 0), (dlf_fw, d, 0), (dk_fw, d, 0), (dlf_bw, d, 0),
            (dk_bw, d, 0), (dv, d, 0), (dg, d, 0)]
    dpa, dlbl = _row_kernel(pre_bwd, rows, [lb_logits], [(5 * d, BF16)], [lb_logits.shape], name="hg_pre_bwd")
    dw_in = _matmul(u.T, dpa, name="hg_in_dw")
    token = send(dw_in, dw_out)
    du = _matmul(dpa, w_in, nt=True, out_dtype=BF16, after=token, name="hg_in_dx")
    dh, dh16, dn1 = _norm_bwd(h, norm_w, du, dh, "hg_norm_bwd")
    return dh, dh16, dict(lb=dlbl, a_norm=dnw, norm=dn1)


def _group_params(p):
    return p.reshape(SSM_GROUPS, 1, SSM_HPG)


def _mamba_fwd(h, norm_w, w_z, w_xbc, w_dt, conv_w, conv_b, dt_bias, a_log, d_skip, b_norm_w, w_out):
    L = h.shape[0]
    u = _norm_fwd(h, norm_w, "mb_norm")
    z = _matmul(u, w_z, name="mb_in_z")
    xbc_raw = _matmul(u, w_xbc, name="mb_in_xbc")
    dt_raw = _matmul(u, w_dt, name="mb_in_dt")
    xbc = _col_kernel(lambda xv, w, b: (_mamba_conv_fn(xv, w, b),), [(xbc_raw, 0)], [(conv_w, 0), (conv_b, 0)],
                      [F32], [], name="mb_conv", n_tiles=CONV_DIM // COL_TILE)[0]
    dtr = dt_raw.reshape(L, 2, SSM_GROUPS, SSM_HPG).transpose(1, 2, 0, 3)
    bias, alog = dt_bias.reshape(2, -1), a_log.reshape(2, -1)
    dsk = _group_params(d_skip.reshape(-1))
    y_fw, st_fw = _ssd_fwd(xbc, dtr[0], _group_params(bias[0]), _group_params(alog[0]), dsk, rev=False, skip=True,
                           add=None, name="mb_ssd_fw")
    ysum, st_bw = _ssd_fwd(xbc, dtr[1], _group_params(bias[1]), _group_params(alog[1]), dsk, rev=True, skip=False,
                           add=y_fw, name="mb_ssd_bw")
    y = _row_kernel(lambda yv, zv, wv: (_mamba_post_fn(yv, zv, wv),), [(ysum, D_INNER, 0), (z, D_INNER, 0)],
                    [b_norm_w], [(D_INNER, BF16)], [], name="mb_post")[0]
    h_out = _matmul(y, w_out, add=h, name="mb_out")
    return h_out, (u, z, xbc_raw, xbc, dtr, st_fw, st_bw, ysum, y)


def _mamba_bwd(h, dh, dh16, saved, norm_w, w_z, w_xbc, w_dt, conv_w, conv_b, dt_bias, a_log, d_skip, b_norm_w, w_out,
               after=None):
    L = h.shape[0]
    u, z, xbc_raw, xbc, dtr, st_fw, st_bw, ysum, y = saved
    dy = _matmul(dh16, w_out, nt=True, after=after, name="mb_out_dx")
    dw_out = _matmul(y.T, dh16, name="mb_out_dw")

    def post_bwd(yv, zv, ct, wv):
        _, vjp = jax.vjp(_mamba_post_fn, yv, zv, wv)
        return vjp(ct)

    dys, dz, dbn = _row_kernel(post_bwd, [(ysum, D_INNER, 0), (z, D_INNER, 0), (dy, D_INNER, 0)], [b_norm_w],
                               [(D_INNER, F32), (D_INNER, BF16)], [(1, D_INNER)], name="mb_post_bwd")
    bias, alog = dt_bias.reshape(2, -1), a_log.reshape(2, -1)
    dsk = _group_params(d_skip.reshape(-1))
    dx1, db1, dc1, ddt_fw, dbias_fw, dalog_fw, ddsk = _ssd_bwd(
        xbc, dtr[0], _group_params(bias[0]), _group_params(alog[0]), dsk, dys, st_fw, rev=False, skip=True,
        adds=None, name="mb_ssd_fw_bwd")
    dx, db, dc, ddt_bw, dbias_bw, dalog_bw, _ = _ssd_bwd(
        xbc, dtr[1], _group_params(bias[1]), _group_params(alog[1]), dsk, dys, st_bw, rev=True, skip=False,
        adds=(dx1, db1, dc1), name="mb_ssd_bw_bwd")

    def conv_bwd(n_tiles, ct, first):
        def fn(xv, ctv, w, b):
            _, vjp = jax.vjp(_mamba_conv_fn, xv, w, b)
            return vjp(ctv)
        return _col_kernel(fn, [(xbc_raw, first), (ct, 0)], [(conv_w, first), (conv_b, first)], [BF16],
                           [SSM_CONV, 1], name=f"mb_conv_bwd_{first}", n_tiles=n_tiles)

    nx, nb = D_INNER // COL_TILE, GN // COL_TILE
    parts = [conv_bwd(nx, dx, 0), conv_bwd(nb, db, nx), conv_bwd(nb, dc, nx + nb)]
    dxbc = jnp.concatenate([p[0] for p in parts], axis=1)
    dcw = jnp.concatenate([p[1] for p in parts], axis=1)
    dcb = jnp.concatenate([p[2] for p in parts], axis=1)
    ddt = jnp.stack([ddt_fw, ddt_bw]).transpose(2, 0, 1, 3).reshape(L, 2 * SSM_GROUPS * SSM_HPG).astype(BF16)
    du = _matmul(dz, w_z, nt=True, name="mb_z_dx")
    du = _matmul(dxbc, w_xbc, nt=True, add=du, name="mb_xbc_dx")
    du = _matmul(ddt, w_dt, nt=True, add=du, out_dtype=BF16, name="mb_dt_dx")
    ut = u.T
    dw_in = jnp.concatenate([_matmul(ut, dz, name="mb_z_dw"), _matmul(ut, dxbc, name="mb_xbc_dw"),
                             _matmul(ut, ddt, name="mb_dt_dw")], axis=1)
    dh, dh16, dn1 = _norm_bwd(h, norm_w, du, dh, "mb_norm_bwd")
    grads = dict(w_in=dw_in, w_out=dw_out, conv_w=dcw, conv_b=dcb, b_norm=dbn, norm=dn1,
                 dt_bias=jnp.stack([dbias_fw, dbias_bw]).reshape(1, 2, -1),
                 a_log=jnp.stack([dalog_fw, dalog_bw]).reshape(1, 2, -1), d_skip=ddsk.reshape(1, -1))
    return dh, dh16, grads


def _loss_head(h, target, w):
    d = h.shape[1]

    def fn(hv, tv, wv):
        def loss(hv, wv):
            err = _rms(hv, wv) - tv
            return 0.5 * jnp.sum(jnp.mean(err * err, axis=-1, keepdims=True), axis=0, keepdims=True)
        val, vjp = jax.vjp(loss, hv, wv)
        dh, dw = vjp(jnp.ones((1, 1), F32))
        return dh, dh, val, dw

    return _row_kernel(fn, [(h, d, 0), (target, d, 0)], [w], [(d, F32), (d, BF16)], [(1, 1), (1, d)], name="loss_head")


def _local_step(x, target, rep, comm):
    def ffn_args(i, w, small):
        return (rep["norm2_w"][i:i + 1], w["w_in"][:, :D_FF], w["w_in"][:, D_FF:], w["w_out"],
                small["ffn_conv_w"][i], rep["ffn_conv_b"][i:i + 1])

    def mamba_args(w, small):
        w_in = w["w_in"]
        return (rep["norm1_w"][1:2], w_in[:, :D_INNER], w_in[:, D_INNER:D_INNER + CONV_DIM],
                w_in[:, D_INNER + CONV_DIM:], small["b_conv_w"][0], small["b_conv_b"], rep["b_dt_bias"],
                rep["b_a_log"], rep["b_d_skip"], small["b_norm_w"], w["w_out"])

    w_hg = comm.weights("hg", None)
    hg = (rep["norm1_w"][0:1], w_hg["w_in"], rep["a_lb_logits"], rep["a_norm_w"], w_hg["w_out"])
    h0 = x
    h1, s_hg = _hgrn2_fwd(h0, *hg, after=w_hg.get("token"))
    w_f0 = comm.weights("f0", h1)
    small = w_f0["small"]
    f0 = ffn_args(0, w_f0, small)
    h2, s_f0 = _ffn_fwd(h1, *f0, "ffn0")
    mb = mamba_args(comm.weights("mb", h2), small)
    h3, s_mb = _mamba_fwd(h2, *mb)
    f1 = ffn_args(1, comm.weights("f1", h3), small)
    h4, s_f1 = _ffn_fwd(h3, *f1, "ffn1")
    dh, dh16, loss, d_final = _loss_head(h4, target, rep["final_norm_w"].reshape(1, -1))

    dh, dh16, g_f1 = _ffn_bwd(h3, dh, dh16, s_f1, *f1, "ffn1")
    token = comm.send("f1", dict(w_in=g_f1["w_in"], w_out=g_f1["w_out"]))
    dh, dh16, g_mb = _mamba_bwd(h2, dh, dh16, s_mb, *mb, after=token)
    token = comm.send("mb", dict(w_in=g_mb["w_in"], w_out=g_mb["w_out"]))
    dh, dh16, g_f0 = _ffn_bwd(h1, dh, dh16, s_f0, *f0, "ffn0", after=token)
    token = comm.send("f0", dict(w_in=g_f0["w_in"], w_out=g_f0["w_out"]))
    small_grads = dict(b_conv_w=g_mb["conv_w"][None], b_conv_b=g_mb["conv_b"], b_norm_w=g_mb["b_norm"],
                       ffn_conv_w=jnp.stack([g_f0["conv_w"], g_f1["conv_w"]]))
    dh, dh16, g_hg = _hgrn2_bwd(
        h0, dh, dh16, s_hg, *hg, after=token,
        send=lambda dw_in, dw_out: comm.send("hg", dict(w_in=dw_in, w_out=dw_out, small=small_grads)))
    grads = dict(
        norm1_w=jnp.concatenate([g_hg["norm"], g_mb["norm"]], axis=0),
        norm2_w=jnp.concatenate([g_f0["norm"], g_f1["norm"]], axis=0),
        a_lb_logits=g_hg["lb"], a_norm_w=g_hg["a_norm"], b_dt_bias=g_mb["dt_bias"], b_a_log=g_mb["a_log"],
        b_d_skip=g_mb["d_skip"], ffn_conv_b=jnp.concatenate([g_f0["conv_b"], g_f1["conv_b"]], axis=0),
        final_norm_w=d_final.reshape(-1),
    )
    return loss, dh, grads


def _mesh_pos():
    return lax.axis_index("x"), lax.axis_index("y"), lax.axis_index("c")


N_COPIES = N_DEV - 1


def _comm_call(body, ins, out_shape, name):
    n = len(ins)
    hbm = pl.BlockSpec(memory_space=pl.ANY)
    return pl.pallas_call(
        body, name=name, out_shape=out_shape, in_specs=[hbm] * n, out_specs=[hbm] * n,
        scratch_shapes=[pltpu.SemaphoreType.DMA((n * N_COPIES,)), pltpu.SemaphoreType.DMA((n * N_COPIES,)),
                        pltpu.SemaphoreType.DMA((n,))],
    )(*ins)


def _all_gather(shards, name):
    n = len(shards)

    def body(*refs):
        x_refs, out_refs = refs[:n], refs[n:2 * n]
        send_sems, recv_sems, local_sems = refs[2 * n:]
        x, y, c = _mesh_pos()
        me, sibling = (x, y, c), (x, y, 1 - c)
        chips = [(1 - x, y), (x, 1 - y), (1 - x, 1 - y)]

        def copy(w, k, block, to, own=False):
            px, py, pc = block
            dst = out_refs[w].at[4 * px + 2 * py + pc]
            return pltpu.make_async_remote_copy(
                src_ref=x_refs[w] if own else dst, dst_ref=dst, send_sem=send_sems.at[w * N_COPIES + k],
                recv_sem=recv_sems.at[w * N_COPIES + k], device_id=to, device_id_type=MESH)

        mine = [pltpu.make_async_copy(x_refs[w], out_refs[w].at[4 * x + 2 * y + c], local_sems.at[w]) for w in range(n)]
        for cp in mine:
            cp.start()
        first = [copy(w, 1 + j, me, (*chip, c), own=True) for j, chip in enumerate(chips) for w in range(n)]
        first += [copy(w, 0, me, sibling, own=True) for w in range(n)]
        for cp in first:
            cp.start()
        passed = []
        for j, chip in enumerate(chips):
            for w in range(n):
                copy(w, 1 + j, (*chip, c), me).wait_recv()
                passed.append(copy(w, 4 + j, (*chip, c), sibling))
                passed[-1].start()
        for w in range(n):
            copy(w, 0, sibling, me).wait_recv()
        for j, chip in enumerate(chips):
            for w in range(n):
                copy(w, 4 + j, (*chip, 1 - c), me).wait_recv()
        for cp in first + passed:
            cp.wait_send()
        for cp in mine:
            cp.wait()

    out_shape = [jax.ShapeDtypeStruct((N_DEV, *s.shape), s.dtype) for s in shards]
    return _comm_call(body, shards, out_shape, name)


HBM_SPEC = pl.BlockSpec(memory_space=pltpu.HBM)
SEM_SPEC = pl.BlockSpec(memory_space=pltpu.SEMAPHORE)
EFFECT = pltpu.SideEffectType.DATAFLOW_SIDE_EFFECTING
PEER_ORDER = (4, 2, 6, 5, 3, 7, 1)


def _peer_copies(src_refs, land_refs, send_sems, recv_sems, gather):
    x, y, c = _mesh_pos()
    me = 4 * x + 2 * y + c
    copies = []
    for k in PEER_ORDER:
        px = 1 - x if k & 4 else x
        py = 1 - y if k & 2 else y
        pc = 1 - c if k & 1 else c
        for w, (src, land) in enumerate(zip(src_refs, land_refs)):
            copies.append(pltpu.make_async_remote_copy(
                src_ref=src if gather else src.at[4 * px + 2 * py + pc],
                dst_ref=land.at[me] if gather else land.at[k - 1],
                send_sem=send_sems.at[w * N_COPIES + k - 1], recv_sem=recv_sems.at[w * N_COPIES + k - 1],
                device_id=(px, py, pc), device_id_type=MESH))
    return copies


def _copies_start(srcs, *, gather, after, name):
    n = len(srcs)
    lands = [lax.empty(((N_DEV,) + s.shape) if gather else ((N_COPIES,) + s.shape[1:]), s.dtype) for s in srcs]

    def body(*refs):
        src_refs, land_refs = refs[:n], refs[n:2 * n]
        send_sems, recv_sems = refs[-2 * n - 3], refs[-2 * n - 2]
        for cp in _peer_copies(src_refs, land_refs, send_sems, recv_sems, gather):
            cp.start()
        refs[-1][...] = jnp.zeros(refs[-1].shape, F32)

    ins = [pltpu.with_memory_space_constraint(a, pltpu.HBM) for a in srcs + lands]
    in_specs = [HBM_SPEC] * (2 * n)
    if after is not None:
        ins.append(after)
        in_specs.append(pl.BlockSpec(memory_space=pl.ANY))
    sems = pltpu.SemaphoreType.DMA((n * N_COPIES,))
    out = pl.pallas_call(
        body, name=name,
        out_shape=(sems, sems, *[pltpu.HBM(a.shape, a.dtype) for a in srcs + lands], jax.ShapeDtypeStruct((8, LANES), F32)),
        in_specs=in_specs,
        out_specs=(SEM_SPEC, SEM_SPEC, *[HBM_SPEC] * (2 * n), pl.BlockSpec(memory_space=pltpu.VMEM)),
        input_output_aliases={i: 2 + i for i in range(2 * n)},
        compiler_params=pltpu.CompilerParams(has_side_effects=EFFECT),
    )(*ins)
    return out[0], out[1], list(out[2:2 + n]), list(out[2 + n:2 + 2 * n]), out[-1]


def _copies_wait(started, *, gather, after, name):
    send_sems, recv_sems, srcs, lands, _ = started
    n = len(srcs)

    def body(*refs):
        src_refs, land_refs = refs[:n], refs[n:2 * n]
        for cp in _peer_copies(src_refs, land_refs, refs[2 * n], refs[2 * n + 1], gather):
            cp.wait_send()
            cp.wait_recv()

    out = pl.pallas_call(
        body, name=name, out_shape=tuple(pltpu.HBM(a.shape, a.dtype) for a in srcs + lands),
        in_specs=[HBM_SPEC] * (2 * n) + [SEM_SPEC, SEM_SPEC, pl.BlockSpec(memory_space=pl.ANY)],
        out_specs=tuple([HBM_SPEC] * (2 * n)), input_output_aliases={i: i for i in range(2 * n)},
        compiler_params=pltpu.CompilerParams(has_side_effects=EFFECT),
    )(*srcs, *lands, send_sems, recv_sems, after)
    return list(out[n:])


def _adamw(got, w, m, v, *, name, tile, own=None):
    rows, width = w.shape
    n_got = got.shape[0]
    c1 = 1.0 / (1.0 - ADAM_B1 ** ADAM_STEP)
    c2 = 1.0 / (1.0 - ADAM_B2 ** ADAM_STEP)

    def body(*refs):
        got_ref, w_ref, m_ref, v_ref = refs[:4]
        g_ref, d_ref, nm_ref, nv_ref = refs[-4:]
        g = got_ref[0] if own is None else refs[4][...] + got_ref[0]
        for s in range(1, n_got):
            g = g + got_ref[s]
        m_new = ADAM_B1 * m_ref[...] + (1.0 - ADAM_B1) * g
        v_new = ADAM_B2 * v_ref[...] + (1.0 - ADAM_B2) * (g * g)
        g_ref[...] = g
        nm_ref[...] = m_new
        nv_ref[...] = v_new
        d_ref[...] = -ADAM_LR * ((m_new * c1) / (jnp.sqrt(v_new * c2) + ADAM_EPS) + ADAM_WD * w_ref[...])

    spec = pl.BlockSpec((tile, width), lambda i: (i, 0))
    out = jax.ShapeDtypeStruct((rows, width), F32)
    return pl.pallas_call(
        body, name=name, grid=(rows // tile,),
        in_specs=[pl.BlockSpec((n_got, tile, width), lambda i: (0, i, 0))] + [spec] * (3 if own is None else 4),
        out_specs=[spec] * 4, out_shape=[out] * 4, compiler_params=_cparams("parallel"),
    )(*([got, w, m, v] + ([] if own is None else [own])))


SHARDED = dict(a_w_in=(2, True), a_w_out=(1, True), b_w_in=(2, True), b_w_out=(1, True), ffn_w_in=(2, True),
               ffn_w_out=(1, True), b_conv_w=(2, False), b_conv_b=(1, False), b_norm_w=(1, False), ffn_conv_w=(2, False))
REPLICATED = ("norm1_w", "norm2_w", "a_lb_logits", "a_norm_w", "b_dt_bias", "b_a_log", "b_d_skip", "ffn_conv_b",
              "final_norm_w")
WEIGHTS = ("norm1_w", "norm2_w", "a_w_in", "a_lb_logits", "a_norm_w", "a_w_out", "b_w_in", "b_conv_w", "b_conv_b",
           "b_dt_bias", "b_a_log", "b_d_skip", "b_norm_w", "b_w_out", "ffn_w_in", "ffn_conv_w", "ffn_conv_b",
           "ffn_w_out", "final_norm_w")


def _pad_rows(flat, multiple):
    n = flat.shape[-1]
    per = PACK_W * multiple
    total = -(-n // per) * per
    flat = jnp.pad(flat, [(0, 0)] * (flat.ndim - 1) + [(0, total - n)])
    return flat.reshape(*flat.shape[:-1], total // PACK_W, PACK_W)


def _to_parts(full, axis):
    shp = full.shape
    t = full.reshape(*shp[:axis], N_DEV, shp[axis] // N_DEV, *shp[axis + 1:])
    return jnp.moveaxis(t, axis, 0)


def _from_parts(parts, axis):
    t = jnp.moveaxis(parts, 0, axis)
    shp = t.shape
    return t.reshape(*shp[:axis], shp[axis] * shp[axis + 1], *shp[axis + 2:])


BIG = tuple(n for n, (_, mm) in SHARDED.items() if mm)
SMALL = tuple(n for n, (_, mm) in SHARDED.items() if not mm)
SMALL_W = 512


def _small_rows(tree, lead):
    rows = []
    for n in SMALL:
        t = tree[n]
        t = t.reshape(*lead, -1, t.shape[-1])
        rows.append(jnp.pad(t, [(0, 0)] * (t.ndim - 1) + [(0, SMALL_W - t.shape[-1])]))
    buf = jnp.concatenate(rows, axis=-2)
    return jnp.pad(buf, [(0, 0)] * (buf.ndim - 2) + [(0, 16 - buf.shape[-2]), (0, 0)])


def _small_unrows(buf, like, lead):
    out, r = {}, 0
    for n in SMALL:
        shp = like[n].shape
        k = like[n].size // shp[-1]
        out[n] = buf[..., r:r + k, :shp[-1]].reshape(*lead, *shp)
        r += k
    return out


GROUPS = dict(hg=(("a_w_in", 0), ("a_w_out", 0)), f0=(("ffn_w_in", 0), ("ffn_w_out", 0)),
              mb=(("b_w_in", 0), ("b_w_out", 0)), f1=(("ffn_w_in", 1), ("ffn_w_out", 1)))


class _Comm:
    def __init__(self, local):
        x, y, c = _mesh_pos()
        self.me = 4 * x + 2 * y + c
        self.local = local
        self.shards = {g: [local[n][i].astype(BF16) for n, i in names] for g, names in GROUPS.items()}
        self.shards["f0"].append(_small_rows(local, ()))
        self.first = _all_gather(self.shards["hg"], "gather_hg")
        self.gathers, self.sent, token = {}, {}, None
        for g in ("f0", "mb", "f1"):
            self.gathers[g] = _copies_start(self.shards[g], gather=True, after=token, name=f"gather_{g}_start")
            token = self.gathers[g][-1]
        self.token = token

    def weights(self, group, after):
        if group == "hg":
            got = self.first
        else:
            lands = _copies_wait(self.gathers[group], gather=True, after=after, name=f"gather_{group}_wait")
            got = [lax.dynamic_update_index_in_dim(land, shard, self.me, 0)
                   for land, shard in zip(lands, self.shards[group])]
        out = dict(w_in=_from_parts(got[0], 1), w_out=_from_parts(got[1], 0))
        if group == "hg":
            out["token"] = self.token
        if group == "f0":
            small = _small_unrows(got[2], self.local, (N_DEV,))
            out["small"] = {n: _from_parts(small[n], SHARDED[n][0]) for n in SMALL}
        return out

    def send(self, group, grads):
        parts = [_to_parts(grads["w_in"], 1), _to_parts(grads["w_out"], 0)]
        if "small" in grads:
            parts.append(_small_rows({n: _to_parts(grads["small"][n], SHARDED[n][0]) for n in SMALL}, (N_DEV,)))
        self.sent[group] = (parts, _copies_start(parts, gather=False, after=None, name=f"exchange_{group}_start"))
        return self.sent[group][1][-1]

    def finish(self, after, mom, var):
        res = {n: {} for n in BIG}
        for group in ("f1", "mb", "f0", "hg"):
            parts, started = self.sent[group]
            lands = _copies_wait(started, gather=False, after=after, name=f"exchange_{group}_wait")
            own = [lax.dynamic_index_in_dim(p, self.me, 0, keepdims=False) for p in parts]
            for (n, i), got, mine in zip(GROUPS[group], lands, own):
                out = _adamw(got, self.local[n][i], mom[n][i], var[n][i], own=mine, name=f"adamw_{n}_{i}",
                             tile=_pick(mine.shape[0], (256, 176, 128)))
                res[n][i] = out
                after = out[0]
        small = _adamw(lands[2], *[_small_rows(t, ()) for t in (self.local, mom, var)], own=own[2],
                       name="adamw_small", tile=16)
        return res, small


def _pack_small(tree, extra):
    flat = jnp.concatenate([tree[n].reshape(-1) for n in REPLICATED] + [extra.reshape(-1)])
    return _pad_rows(flat, 8)


def _unpack_small(pack, like):
    flat, out, off = pack.reshape(-1), {}, 0
    for n in REPLICATED:
        out[n] = flat[off:off + like[n].size].reshape(like[n].shape)
        off += like[n].size
    return out, flat[off]


def kernel(x, norm1_w, norm2_w, a_w_in, a_lb_logits, a_norm_w, a_w_out, b_w_in, b_conv_w, b_conv_b, b_dt_bias, b_a_log, b_d_skip, b_norm_w, b_w_out, ffn_w_in, ffn_conv_w, ffn_conv_b, ffn_w_out, final_norm_w, loss_target, m_norm1_w, m_norm2_w, m_a_w_in, m_a_lb_logits, m_a_norm_w, m_a_w_out, m_b_w_in, m_b_conv_w, m_b_conv_b, m_b_dt_bias, m_b_a_log, m_b_d_skip, m_b_norm_w, m_b_w_out, m_ffn_w_in, m_ffn_conv_w, m_ffn_conv_b, m_ffn_w_out, m_final_norm_w, v_norm1_w, v_norm2_w, v_a_w_in, v_a_lb_logits, v_a_norm_w, v_a_w_out, v_b_w_in, v_b_conv_w, v_b_conv_b, v_b_dt_bias, v_b_a_log, v_b_d_skip, v_b_norm_w, v_b_w_out, v_ffn_w_in, v_ffn_conv_w, v_ffn_conv_b, v_ffn_w_out, v_final_norm_w):
    given = dict(locals())
    local = {n: given[n] for n in WEIGHTS}
    mom = {n: given["m_" + n] for n in WEIGHTS}
    var = {n: given["v_" + n] for n in WEIGHTS}

    comm = _Comm(local)
    loss, grad_x, grads = _local_step(x[0], loss_target[0], local, comm)

    res, small_res = comm.finish(grad_x, mom, var)
    outs = ({}, {}, {}, {})
    for n in BIG:
        for out, layers in zip(outs, zip(*[res[n][i] for i in sorted(res[n])])):
            out[n] = jnp.stack(layers)
    for out, r in zip(outs, small_res):
        out.update(_small_unrows(r, local, ()))
    out_g, out_d, out_m, out_v = outs

    small = _pack_small(grads, loss)
    rows = small.shape[0]
    got_s = _all_gather([small], "gather_small")[0]
    zero = jnp.zeros((1,), F32)
    g, dlt, nm, nv = _adamw(got_s, _pack_small(local, zero), _pack_small(mom, zero), _pack_small(var, zero),
                            name="adamw_replicated", tile=rows)
    (rep_g, total), (rep_d, _), (rep_m, _), (rep_v, _) = (_unpack_small(t, local) for t in (g, dlt, nm, nv))
    out_g.update(rep_g)
    out_d.update(rep_d)
    out_m.update(rep_m)
    out_v.update(rep_v)

    return (total, grad_x[None], *[out_g[n] for n in WEIGHTS], *[out_d[n] for n in WEIGHTS],
            *[out_m[n] for n in WEIGHTS], *[out_v[n] for n in WEIGHTS])
```

```python
import functools

import jax
import jax.numpy as jnp
from jax import lax
from jax.experimental import pallas as pl
from jax.experimental.pallas import tpu as pltpu

F32 = jnp.float32
BF16 = jnp.bfloat16
HIGHEST = lax.Precision.HIGHEST
MESH = pl.DeviceIdType.MESH

N_DEV = 8
EPS = 1e-6
D_MODEL = 1024
HG_HEADS = 8
HG_HEAD_DIM = 128
D_INNER = 2048
SSM_HEAD_DIM = 64
SSM_GROUPS = 8
SSM_HPG = 4
SSM_STATE = 128
GN = SSM_GROUPS * SSM_STATE
CONV_DIM = D_INNER + 2 * GN
D_FF = 2816
SSM_CONV = 5
FFN_CONV = 3

ADAM_LR = 0.001
ADAM_B1 = 0.9
ADAM_B2 = 0.999
ADAM_EPS = 1e-08
ADAM_WD = 0.01
ADAM_STEP = 10

LANES = 128
ROW_TILE = 256
COL_TILE = 128
GLA_CHUNK = 16
GLA_BLOCK = 512
GLA_HEADS_PER_STEP = 4
SSD_CHUNK = 128
SSD_BLOCK = 512
PACK_W = 1024
VMEM_LIMIT = 56 * 1024 * 1024

NT_DIMS = (((1,), (1,)), ((), ()))
TN_DIMS = (((0,), (0,)), ((), ()))


def _cparams(*sem):
    return pltpu.CompilerParams(dimension_semantics=sem, vmem_limit_bytes=VMEM_LIMIT)


def _rms(x, w):
    return x * lax.rsqrt(jnp.mean(x * x, axis=-1, keepdims=True) + EPS) * w


def _row_kernel(body_fn, rows, params, row_outs, acc_outs, *, name, tile=ROW_TILE):
    L = rows[0][0].shape[0]
    tile = min(tile, L)
    n_in = len(rows) + len(params)
    n_ro = len(row_outs)

    def body(*refs):
        outs = body_fn(*[r[...] for r in refs[:n_in]])
        for ref, o in zip(refs[n_in:n_in + n_ro], outs[:n_ro]):
            ref[...] = o.astype(ref.dtype)
        first = pl.program_id(0) == 0
        for ref, o in zip(refs[n_in + n_ro:], outs[n_ro:]):
            @pl.when(first)
            def _(ref=ref):
                ref[...] = jnp.zeros(ref.shape, ref.dtype)
            ref[...] += o

    in_specs = [pl.BlockSpec((tile, w), lambda i, cb=cb: (i, cb)) for _, w, cb in rows]
    in_specs += [pl.BlockSpec(p.shape, lambda i: (0, 0)) for p in params]
    out_specs = [pl.BlockSpec((tile, w), lambda i: (i, 0)) for w, _ in row_outs]
    out_specs += [pl.BlockSpec(s, lambda i: (0, 0)) for s in acc_outs]
    out_shape = [jax.ShapeDtypeStruct((L, w), dt) for w, dt in row_outs]
    out_shape += [jax.ShapeDtypeStruct(s, F32) for s in acc_outs]
    return pl.pallas_call(
        body, name=name, grid=(L // tile,), in_specs=in_specs, out_specs=out_specs, out_shape=out_shape,
        compiler_params=_cparams("arbitrary" if acc_outs else "parallel"),
    )(*[a for a, _, _ in rows], *params)


def _col_kernel(body_fn, cols, params, col_outs, par_outs, *, name, n_tiles):
    L = cols[0][0].shape[0]
    n_in = len(cols) + len(params)
    width = n_tiles * COL_TILE

    def body(*refs):
        outs = body_fn(*[r[...] for r in refs[:n_in]])
        for ref, o in zip(refs[n_in:], outs):
            ref[...] = o.astype(ref.dtype)

    in_specs = [pl.BlockSpec((L, COL_TILE), lambda j, cb=cb: (0, cb + j)) for _, cb in cols]
    in_specs += [pl.BlockSpec((p.shape[0], COL_TILE), lambda j, cb=cb: (0, cb + j)) for p, cb in params]
    out_specs = [pl.BlockSpec((L, COL_TILE), lambda j: (0, j)) for _ in col_outs]
    out_specs += [pl.BlockSpec((k, COL_TILE), lambda j: (0, j)) for k in par_outs]
    out_shape = [jax.ShapeDtypeStruct((L, width), dt) for dt in col_outs]
    out_shape += [jax.ShapeDtypeStruct((k, width), F32) for k in par_outs]
    return pl.pallas_call(
        body, name=name, grid=(n_tiles,), in_specs=in_specs, out_specs=out_specs, out_shape=out_shape,
        compiler_params=_cparams("parallel"),
    )(*[a for a, _ in cols], *[p for p, _ in params])


def _pick(n, options):
    for t in options:
        if n % t == 0:
            return t
    return n


MATMUL_VMEM = 40 * 1024 * 1024


def _matmul_tiles(M, N, K, out_bytes):
    best = None
    for tm in (1024, 512, 256, 128, M):
        for tn in (1408, 1024, 512, 256, 128, N):
            if M % tm or N % tn:
                continue
            if 2 * (2 * K * (tm + tn) + out_bytes * tm * tn) > MATMUL_VMEM:
                continue
            if best is None or tm * tn > best[0] * best[1]:
                best = (tm, tn)
    return best


def _matmul(a, b, *, name, nt=False, ta=False, add=None, out_dtype=F32, after=None):
    K, M = a.shape[::-1] if not ta else a.shape
    N = b.shape[0] if nt else b.shape[1]
    tm, tn = _matmul_tiles(M, N, K, 4 * (1 + (add is not None)) if out_dtype == F32 else 2 + 4 * (add is not None))

    def body(*refs):
        a_ref, b_ref = refs[0], refs[1]
        o_ref = refs[-1]
        dims = TN_DIMS if ta else NT_DIMS if nt else (((1,), (0,)), ((), ()))
        acc = lax.dot_general(a_ref[...], b_ref[...], dims, preferred_element_type=F32)
        if add is not None:
            acc = acc + refs[2][...]
        o_ref[...] = acc.astype(o_ref.dtype)

    in_specs = [pl.BlockSpec((K, tm), lambda i, j: (0, i)) if ta else pl.BlockSpec((tm, K), lambda i, j: (i, 0)),
                pl.BlockSpec((tn, K), lambda i, j: (j, 0)) if nt else pl.BlockSpec((K, tn), lambda i, j: (0, j))]
    args = [a, b]
    if add is not None:
        in_specs.append(pl.BlockSpec((tm, tn), lambda i, j: (i, j)))
        args.append(add)
    if after is not None:
        in_specs.append(pl.BlockSpec(memory_space=pl.ANY))
        args.append(after)
    return pl.pallas_call(
        body, name=name, grid=(M // tm, N // tn), in_specs=in_specs,
        out_specs=pl.BlockSpec((tm, tn), lambda i, j: (i, j)),
        out_shape=jax.ShapeDtypeStruct((M, N), out_dtype),
        compiler_params=_cparams("parallel", "parallel"),
    )(*args)


def _hgrn2_pre_fn(q_raw, ffw_raw, fbw_raw, lb_logits):
    lb = jax.nn.softmax(lb_logits, axis=0)[0:1]

    def gate(fr):
        f = lb + (1.0 - lb) * jax.nn.sigmoid(fr)
        return jnp.log(f), 1.0 - f

    lf_fw, k_fw = gate(ffw_raw)
    lf_bw, k_bw = gate(fbw_raw)
    return jax.nn.silu(q_raw), lf_fw, k_fw, lf_bw, k_bw


def _hgrn2_post_fn(o, g, norm_w):
    outs = []
    for h in range(HG_HEADS):
        sl = slice(h * HG_HEAD_DIM, (h + 1) * HG_HEAD_DIM)
        outs.append(_rms(o[:, sl], norm_w) * jax.nn.silu(g[:, sl]))
    return jnp.concatenate(outs, axis=1)


def _mamba_post_fn(y, z, norm_w):
    outs = []
    gw = D_INNER // SSM_GROUPS
    for gi in range(SSM_GROUPS):
        sl = slice(gi * gw, (gi + 1) * gw)
        outs.append(_rms(y[:, sl] * jax.nn.silu(z[:, sl]), norm_w[:, sl]))
    return jnp.concatenate(outs, axis=1)


def _shift_rows_impl(x, d):
    if d == 0:
        return x
    n = x.shape[0]
    t = lax.broadcasted_iota(jnp.int32, x.shape, 0)
    rolled = pltpu.roll(x, d % n, 0)
    return jnp.where((t >= d) if d > 0 else (t < n + d), rolled, 0.0)


@functools.partial(jax.custom_vjp, nondiff_argnums=(1,))
def _shift_rows(x, d):
    return _shift_rows_impl(x, d)


_shift_rows.defvjp(lambda x, d: (_shift_rows_impl(x, d), None), lambda d, _, g: (_shift_rows_impl(g, -d),))


def _dwconv(x, w, b):
    taps = w.shape[0]
    c = (taps - 1) // 2
    y = b + w[0:1, :] * _shift_rows(x, c)
    for k in range(1, taps):
        y = y + w[k:k + 1, :] * _shift_rows(x, c - k)
    return y


def _ffn_mid_fn(gate, val, w, b):
    return jax.nn.silu(_dwconv(gate, w, b)) * val


def _mamba_conv_fn(xbc, w, b):
    return jax.nn.silu(_dwconv(xbc, w, b))


def _gla_consts(rev):
    return lax.broadcasted_iota(jnp.int32, (GLA_CHUNK, HG_HEAD_DIM), 0)


def _segment_cumsum(x, seg, rev):
    n = x.shape[0]
    t = lax.broadcasted_iota(jnp.int32, x.shape, 0) & (seg - 1)
    s = 1
    while s < seg:
        if rev:
            x = x + jnp.where(t < seg - s, pltpu.roll(x, n - s, 0), 0.0)
        else:
            x = x + jnp.where(t >= s, pltpu.roll(x, s, 0), 0.0)
        s *= 2
    return x


def _segment_cumsum_mxu(x, seg, rev):
    r = lax.broadcasted_iota(jnp.int32, (seg, seg), 0)
    c = lax.broadcasted_iota(jnp.int32, (seg, seg), 1)
    tri = ((r <= c) if rev else (r >= c)).astype(BF16)
    tri3 = jnp.concatenate([tri, tri, tri], axis=1)
    hi = x.astype(BF16)
    rest = x - hi.astype(F32)
    mid = rest.astype(BF16)
    lo = (rest - mid.astype(F32)).astype(BF16)
    outs = []
    for g in range(x.shape[0] // seg):
        rows = slice(g * seg, (g + 1) * seg)
        terms = jnp.concatenate([hi[rows], mid[rows], lo[rows]], axis=0)
        outs.append(jnp.dot(tri3, terms, preferred_element_type=F32))
    return jnp.concatenate(outs, axis=0)


@functools.partial(jax.custom_vjp, nondiff_argnums=(1, 2))
def _segment_cumsum_diff(x, seg, rev):
    return _segment_cumsum_mxu(x, seg, rev)


_segment_cumsum_diff.defvjp(lambda x, seg, rev: (_segment_cumsum_mxu(x, seg, rev), None),
                            lambda seg, rev, _, g: (_segment_cumsum_mxu(g, seg, not rev),))


def _gla_chunk(st, q, k, v, b, *, rev, consts):
    row = consts
    c = q.shape[0]
    o = lax.dot_general((q * jnp.exp(b)).astype(BF16), st.astype(BF16), NT_DIMS, preferred_element_type=F32)
    for s in range(c):
        e = jnp.exp(jnp.where((row <= s) if rev else (row >= s), b - b[s:s + 1], -jnp.inf))
        o = o + jnp.sum(q * (k[s:s + 1] * e), axis=-1, keepdims=True) * v[s:s + 1]
    b_end = b[0:1] if rev else b[c - 1:c]
    kd = (k * jnp.exp(b_end - b)).astype(BF16)
    st_new = st * jnp.exp(b_end) + lax.dot_general(v.astype(BF16), kd, TN_DIMS, preferred_element_type=F32)
    return st_new, o


def _gla_fwd(q, k, v, v_cb, g, *, rev, add, name):
    L = q.shape[0]
    blk = min(GLA_BLOCK, L)
    nblk, nsub = L // blk, blk // GLA_CHUNK
    hd, hps = HG_HEAD_DIM, GLA_HEADS_PER_STEP
    wide = hd * hps

    def body(*refs):
        q_ref, k_ref, v_ref, g_ref = refs[:4]
        add_ref = refs[4] if add is not None else None
        o_ref, st_out, st_scr, b_scr = refs[-4:]
        consts = _gla_consts(rev)

        @pl.when(pl.program_id(1) == 0)
        def _():
            st_scr[...] = jnp.zeros(st_scr.shape, F32)

        st_out[...] = st_scr[...]
        b_scr[...] = _segment_cumsum(g_ref[...], GLA_CHUNK, rev)

        def step(i, carry):
            sl = pl.ds(pl.multiple_of((nsub - 1 - i if rev else i) * GLA_CHUNK, GLA_CHUNK), GLA_CHUNK)
            lanes = [slice(hi * hd, (hi + 1) * hd) for hi in range(hps)]
            ins = [(st_scr[hi], q_ref[sl, ln], k_ref[sl, ln], v_ref[sl, ln], b_scr[sl, ln])
                   for hi, ln in enumerate(lanes)]
            adds = [add_ref[sl, ln] for ln in lanes] if add_ref is not None else None
            outs = [_gla_chunk(*args, rev=rev, consts=consts) for args in ins]
            for hi, ln in enumerate(lanes):
                st_scr[hi] = outs[hi][0]
                o_ref[sl, ln] = outs[hi][1] if adds is None else outs[hi][1] + adds[hi]
            return carry

        lax.fori_loop(0, nsub, step, 0)

    def pos(j):
        return nblk - 1 - j if rev else j

    spec = pl.BlockSpec((blk, wide), lambda h, j: (pos(j), h))
    in_specs = [spec, spec, pl.BlockSpec((blk, wide), lambda h, j: (pos(j), v_cb // hps + h)), spec]
    args = [q, k, v, g]
    if add is not None:
        in_specs.append(spec)
        args.append(add)
    return pl.pallas_call(
        body, name=name, grid=(HG_HEADS // hps, nblk), in_specs=in_specs,
        out_specs=[spec, pl.BlockSpec((hps, None, hd, hd), lambda h, j: (h, j, 0, 0))],
        out_shape=[jax.ShapeDtypeStruct((L, HG_HEADS * hd), F32),
                   jax.ShapeDtypeStruct((HG_HEADS, nblk, hd, hd), F32)],
        scratch_shapes=[pltpu.VMEM((hps, hd, hd), F32), pltpu.VMEM((blk, wide), F32)],
        compiler_params=_cparams("parallel", "arbitrary"),
    )(*args)


def _gla_bwd(q, k, v, v_cb, g, do, states, *, rev, adds, name):
    L = q.shape[0]
    blk = min(GLA_BLOCK, L)
    nblk, nsub = L // blk, blk // GLA_CHUNK
    hd, hps = HG_HEAD_DIM, GLA_HEADS_PER_STEP
    wide = hd * hps
    n_add = 0 if adds is None else 2

    def body(*refs):
        q_ref, k_ref, v_ref, g_ref, do_ref, st_in = refs[:6]
        add_refs = refs[6:6 + n_add]
        dq_ref, dk_ref, dv_ref, dg_ref, dst_scr, sub_scr, b_scr, db_scr = refs[6 + n_add:]
        consts = _gla_consts(rev)
        chunk = functools.partial(_gla_chunk, rev=rev, consts=consts)

        @pl.when(pl.program_id(1) == 0)
        def _():
            dst_scr[...] = jnp.zeros(dst_scr.shape, F32)

        b_scr[...] = _segment_cumsum(g_ref[...], GLA_CHUNK, rev)

        def rows(i):
            return pl.ds(pl.multiple_of((nsub - 1 - i if rev else i) * GLA_CHUNK, GLA_CHUNK), GLA_CHUNK)

        for hi in range(hps):
            sub_scr[hi] = st_in[hi]

        def replay(i, carry):
            sl = rows(i)
            lanes = [slice(hi * hd, (hi + 1) * hd) for hi in range(hps)]
            ins = [(sub_scr[i * hps + hi], q_ref[sl, ln], k_ref[sl, ln], v_ref[sl, ln], b_scr[sl, ln])
                   for hi, ln in enumerate(lanes)]
            outs = [chunk(*args)[0] for args in ins]
            for hi in range(hps):
                sub_scr[(i + 1) * hps + hi] = outs[hi]
            return carry

        lax.fori_loop(0, nsub - 1, replay, 0)

        def step(n, carry):
            i = nsub - 1 - n
            sl = rows(i)
            lanes = [slice(hi * hd, (hi + 1) * hd) for hi in range(hps)]
            ins = [(sub_scr[i * hps + hi], q_ref[sl, ln], k_ref[sl, ln], v_ref[sl, ln], b_scr[sl, ln])
                   for hi, ln in enumerate(lanes)]
            cts = [(dst_scr[hi], do_ref[sl, ln]) for hi, ln in enumerate(lanes)]
            adds = [(add_refs[0][sl, ln], add_refs[1][sl, ln]) for ln in lanes] if n_add else None
            outs = [jax.vjp(chunk, *args)[1](ct) for args, ct in zip(ins, cts)]
            for hi, ln in enumerate(lanes):
                dst, dq, dk, dv, db = outs[hi]
                dst_scr[hi] = dst
                dq_ref[sl, ln] = dq if adds is None else dq + adds[hi][0]
                dk_ref[sl, ln] = dk
                dv_ref[sl, ln] = dv if adds is None else dv + adds[hi][1]
                db_scr[sl, ln] = db
            return carry

        lax.fori_loop(0, nsub, step, 0)
        dg_ref[...] = _segment_cumsum(db_scr[...], GLA_CHUNK, not rev)

    def pos(j):
        p = nblk - 1 - j
        return nblk - 1 - p if rev else p

    spec = pl.BlockSpec((blk, wide), lambda h, j: (pos(j), h))
    in_specs = [spec, spec, pl.BlockSpec((blk, wide), lambda h, j: (pos(j), v_cb // hps + h)), spec, spec,
                pl.BlockSpec((hps, None, hd, hd), lambda h, j: (h, nblk - 1 - j, 0, 0))]
    args = [q, k, v, g, do, states]
    if adds is not None:
        in_specs += [spec, spec]
        args += list(adds)
    out = jax.ShapeDtypeStruct((L, HG_HEADS * hd), F32)
    return pl.pallas_call(
        body, name=name, grid=(HG_HEADS // hps, nblk), in_specs=in_specs,
        out_specs=[spec] * 4, out_shape=[out] * 4,
        scratch_shapes=[pltpu.VMEM((hps, hd, hd), F32), pltpu.VMEM((nsub * hps, hd, hd), F32),
                        pltpu.VMEM((blk, wide), F32), pltpu.VMEM((blk, wide), F32)],
        compiler_params=_cparams("parallel", "arbitrary"),
    )(*args)


def _ssd_consts(rev):
    c = SSD_CHUNK
    gw = SSM_HPG * SSM_HEAD_DIM
    r2 = lax.broadcasted_iota(jnp.int32, (c, c), 0)
    c2 = lax.broadcasted_iota(jnp.int32, (c, c), 1)
    low = (r2 <= c2) if rev else (r2 >= c2)
    lane = lax.broadcasted_iota(jnp.int32, (1, gw), 1)
    return low, r2 == c2, lane, lane >> 6


def _expand_heads(v4, head_of_lane):
    first = head_of_lane[:, :LANES] == 0
    cols = [v4[:, j:j + 1] for j in range(SSM_HPG)]
    return jnp.concatenate([jnp.where(first, cols[0], cols[1]), jnp.where(first, cols[2], cols[3])], axis=1)


def _ssd_prep(dtr, bias, alog, dsk, *, rev, consts):
    head_of_lane = consts[-1]
    dt_l = _expand_heads(jax.nn.softplus(dtr + bias), head_of_lane)
    a_l = _expand_heads(-jnp.exp(alog), head_of_lane)
    return dt_l, _segment_cumsum_diff(dt_l * a_l, SSD_CHUNK, rev), _expand_heads(dsk, head_of_lane)


def _ssd_chunk(st, x, bm, cm, dt_l, acum, dsk_l, *, rev, consts, skip):
    low, eye, lane, head_of_lane = consts
    c = x.shape[0]
    xd = x * dt_l
    cb = lax.dot_general(cm.astype(BF16), bm.astype(BF16), NT_DIMS, preferred_element_type=F32)
    y = jnp.dot(cm.astype(BF16), st.astype(BF16), preferred_element_type=F32) * jnp.exp(acum)
    for j in range(SSM_HPG):
        acol = jnp.sum(jnp.where(lane == j * SSM_HEAD_DIM, acum, 0.0), axis=-1, keepdims=True)
        ab = jnp.broadcast_to(acol, (c, c))
        arow = jnp.sum(jnp.where(eye, ab, 0.0), axis=0, keepdims=True)
        lmat = jnp.exp(jnp.where(low, ab - arow, -jnp.inf))
        xj = jnp.where(head_of_lane == j, xd, 0.0)
        y = y + jnp.dot((cb * lmat).astype(BF16), xj.astype(BF16), preferred_element_type=F32)
    a_end = acum[0:1] if rev else acum[c - 1:c]
    xdec = (xd * jnp.exp(a_end - acum)).astype(BF16)
    st_new = st * jnp.exp(a_end) + lax.dot_general(bm.astype(BF16), xdec, TN_DIMS, preferred_element_type=F32)
    if skip:
        y = y + x * dsk_l
    return st_new, y


def _ssd_specs(L, rev):
    blk = min(SSD_BLOCK, L)
    nblk = L // blk
    gw = SSM_HPG * SSM_HEAD_DIM

    def pos(j):
        return nblk - 1 - j if rev else j

    return blk, nblk, gw, pos


def _ssd_fwd(xbc, dtr, bias, alog, dsk, *, rev, skip, add, name):
    L = xbc.shape[0]
    blk, nblk, gw, pos = _ssd_specs(L, rev)
    nsub = blk // SSD_CHUNK
    n = SSM_STATE

    def body(*refs):
        x_ref, b_ref, c_ref, dt_ref, bias_ref, alog_ref, dsk_ref = refs[:7]
        add_ref = refs[7] if add is not None else None
        y_ref, st_out, st_scr, dt_scr, ac_scr = refs[-5:]
        consts = _ssd_consts(rev)

        @pl.when(pl.program_id(1) == 0)
        def _():
            st_scr[...] = jnp.zeros(st_scr.shape, F32)

        st_out[...] = st_scr[...]
        dt_scr[...], ac_scr[...], dsk_l = _ssd_prep(dt_ref[...], bias_ref[...], alog_ref[...], dsk_ref[...],
                                                   rev=rev, consts=consts)

        def step(i, carry):
            sl = pl.ds(pl.multiple_of((nsub - 1 - i if rev else i) * SSD_CHUNK, SSD_CHUNK), SSD_CHUNK)
            st_new, y = _ssd_chunk(st_scr[...], x_ref[sl, :], b_ref[sl, :], c_ref[sl, :], dt_scr[sl, :],
                                   ac_scr[sl, :], dsk_l, rev=rev, consts=consts, skip=skip)
            st_scr[...] = st_new
            if add_ref is not None:
                y = y + add_ref[sl, :]
            y_ref[sl, :] = y
            return carry

        lax.fori_loop(0, nsub, step, 0)

    b0 = D_INNER // n
    yspec = pl.BlockSpec((blk, gw), lambda g, j: (pos(j), g))
    pspec = pl.BlockSpec((None, 1, SSM_HPG), lambda g, j: (g, 0, 0))
    in_specs = [yspec,
                pl.BlockSpec((blk, n), lambda g, j: (pos(j), b0 + g)),
                pl.BlockSpec((blk, n), lambda g, j: (pos(j), b0 + SSM_GROUPS + g)),
                pl.BlockSpec((None, blk, SSM_HPG), lambda g, j: (g, pos(j), 0)),
                pspec, pspec, pspec]
    args = [xbc, xbc, xbc, dtr, bias, alog, dsk]
    if add is not None:
        in_specs.append(yspec)
        args.append(add)
    return pl.pallas_call(
        body, name=name, grid=(SSM_GROUPS, nblk), in_specs=in_specs,
        out_specs=[yspec, pl.BlockSpec((None, None, n, gw), lambda g, j: (g, j, 0, 0))],
        out_shape=[jax.ShapeDtypeStruct((L, D_INNER), F32),
                   jax.ShapeDtypeStruct((SSM_GROUPS, nblk, n, gw), F32)],
        scratch_shapes=[pltpu.VMEM((n, gw), F32), pltpu.VMEM((blk, gw), F32), pltpu.VMEM((blk, gw), F32)],
        compiler_params=_cparams("parallel", "arbitrary"),
    )(*args)


def _ssd_bwd(xbc, dtr, bias, alog, dsk, dy, states, *, rev, skip, adds, name):
    L = xbc.shape[0]
    blk, nblk, gw, _ = _ssd_specs(L, rev)
    nsub = blk // SSD_CHUNK
    n = SSM_STATE
    n_add = 0 if adds is None else 3

    def body(*refs):
        x_ref, b_ref, c_ref, dt_ref, bias_ref, alog_ref, dsk_ref, dy_ref, st_in = refs[:9]
        add_refs = refs[9:9 + n_add]
        outs = refs[9 + n_add:]
        dx_ref, db_ref, dc_ref, ddt_ref, dbias_ref, dalog_ref, ddsk_ref = outs[:7]
        dst_scr, sub_scr, dt_scr, ac_scr, ddt_scr, dac_scr = outs[7:]
        consts = _ssd_consts(rev)
        chunk = functools.partial(_ssd_chunk, rev=rev, consts=consts, skip=skip)
        prep = functools.partial(_ssd_prep, rev=rev, consts=consts)

        @pl.when(pl.program_id(1) == 0)
        def _():
            dst_scr[...] = jnp.zeros(dst_scr.shape, F32)
            dbias_ref[...] = jnp.zeros(dbias_ref.shape, F32)
            dalog_ref[...] = jnp.zeros(dalog_ref.shape, F32)
            ddsk_ref[...] = jnp.zeros(ddsk_ref.shape, F32)

        narrow = (dt_ref[...], bias_ref[...], alog_ref[...], dsk_ref[...])
        (dt_scr[...], ac_scr[...], dsk_l), prep_vjp = jax.vjp(prep, *narrow)

        def rows(i):
            return pl.ds(pl.multiple_of((nsub - 1 - i if rev else i) * SSD_CHUNK, SSD_CHUNK), SSD_CHUNK)

        def operands(i):
            sl = rows(i)
            return (x_ref[sl, :], b_ref[sl, :], c_ref[sl, :], dt_scr[sl, :], ac_scr[sl, :], dsk_l)

        def replay(i, st):
            sub_scr[i] = st
            return chunk(st, *operands(i))[0]

        lax.fori_loop(0, nsub, replay, st_in[...])

        def step(k, ddsk_l):
            i = nsub - 1 - k
            sl = rows(i)
            _, vjp = jax.vjp(chunk, sub_scr[i], *operands(i))
            dst, dx, db, dc, ddt_l, dac, ddsk_k = vjp((dst_scr[...], dy_ref[sl, :]))
            dst_scr[...] = dst
            if n_add:
                dx = dx + add_refs[0][sl, :]
                db = db + add_refs[1][sl, :]
                dc = dc + add_refs[2][sl, :]
            dx_ref[sl, :] = dx
            db_ref[sl, :] = db
            dc_ref[sl, :] = dc
            ddt_scr[sl, :] = ddt_l
            dac_scr[sl, :] = dac
            return ddsk_l + ddsk_k

        ddsk_l = lax.fori_loop(0, nsub, step, jnp.zeros((1, gw), F32))
        ddt, dbias, dalog, ddsk = prep_vjp((ddt_scr[...], dac_scr[...], ddsk_l))
        ddt_ref[...] = ddt
        dbias_ref[...] += dbias
        dalog_ref[...] += dalog
        ddsk_ref[...] += ddsk

    def pos(j):
        p = nblk - 1 - j
        return nblk - 1 - p if rev else p

    b0 = D_INNER // n
    xspec = pl.BlockSpec((blk, gw), lambda g, j: (pos(j), g))
    nspec = pl.BlockSpec((blk, n), lambda g, j: (pos(j), g))
    dtspec = pl.BlockSpec((None, blk, SSM_HPG), lambda g, j: (g, pos(j), 0))
    pspec = pl.BlockSpec((None, 1, SSM_HPG), lambda g, j: (g, 0, 0))
    in_specs = [xspec,
                pl.BlockSpec((blk, n), lambda g, j: (pos(j), b0 + g)),
                pl.BlockSpec((blk, n), lambda g, j: (pos(j), b0 + SSM_GROUPS + g)),
                dtspec, pspec, pspec, pspec, xspec,
                pl.BlockSpec((None, None, n, gw), lambda g, j: (g, nblk - 1 - j, 0, 0))]
    args = [xbc, xbc, xbc, dtr, bias, alog, dsk, dy, states]
    if adds is not None:
        in_specs += [xspec, nspec, nspec]
        args += list(adds)
    par = jax.ShapeDtypeStruct((SSM_GROUPS, 1, SSM_HPG), F32)
    return pl.pallas_call(
        body, name=name, grid=(SSM_GROUPS, nblk), in_specs=in_specs,
        out_specs=[xspec, nspec, nspec, dtspec, pspec, pspec, pspec],
        out_shape=[jax.ShapeDtypeStruct((L, D_INNER), F32), jax.ShapeDtypeStruct((L, GN), F32),
                   jax.ShapeDtypeStruct((L, GN), F32), jax.ShapeDtypeStruct((SSM_GROUPS, L, SSM_HPG), F32),
                   par, par, par],
        scratch_shapes=[pltpu.VMEM((n, gw), F32), pltpu.VMEM((nsub, n, gw), F32)] + [pltpu.VMEM((blk, gw), F32)] * 4,
        compiler_params=_cparams("parallel", "arbitrary"),
    )(*args)


def _norm_fwd(h, w, name):
    d = h.shape[1]
    return _row_kernel(lambda hv, wv: (_rms(hv, wv),), [(h, d, 0)], [w], [(d, BF16)], [], name=name)[0]


def _norm_bwd(h, w, du, dh_in, name):
    d = h.shape[1]

    def fn(hv, duv, dhv, wv):
        _, vjp = jax.vjp(_rms, hv, wv)
        dh, dw = vjp(duv.astype(F32))
        dh = dh + dhv
        return dh, dh, dw

    return _row_kernel(fn, [(h, d, 0), (du, d, 0), (dh_in, d, 0)], [w], [(d, F32), (d, BF16)], [(1, d)], name=name)


def _ffn_fwd(h, norm_w, w_gate, w_val, w_out, conv_w, conv_b, tag):
    u = _norm_fwd(h, norm_w, f"{tag}_norm")
    pg = _matmul(u, w_gate, name=f"{tag}_in_gate")
    pv = _matmul(u, w_val, name=f"{tag}_in_val")
    y = _col_kernel(lambda g, v, w, b: (_ffn_mid_fn(g, v, w, b),), [(pg, 0), (pv, 0)], [(conv_w, 0), (conv_b, 0)],
                    [BF16], [], name=f"{tag}_mid", n_tiles=D_FF // COL_TILE)[0]
    h_out = _matmul(y, w_out, add=h, name=f"{tag}_out")
    return h_out, (u, pg, pv, y)


def _ffn_bwd(h, dh, dh16, saved, norm_w, w_gate, w_val, w_out, conv_w, conv_b, tag, after=None):
    u, pg, pv, y = saved
    dy = _matmul(dh16, w_out, nt=True, after=after, name=f"{tag}_out_dx")
    dw_out = _matmul(y, dh16, ta=True, name=f"{tag}_out_dw")

    def fn(g, v, ct, w, b):
        _, vjp = jax.vjp(_ffn_mid_fn, g, v, w, b)
        return vjp(ct)

    dpg, dpv, dcw, dcb = _col_kernel(fn, [(pg, 0), (pv, 0), (dy, 0)], [(conv_w, 0), (conv_b, 0)], [BF16, BF16],
                                     [FFN_CONV, 1], name=f"{tag}_mid_bwd", n_tiles=D_FF // COL_TILE)
    du = _matmul(dpg, w_gate, nt=True, name=f"{tag}_gate_dx")
    du = _matmul(dpv, w_val, nt=True, add=du, out_dtype=BF16, name=f"{tag}_val_dx")
    dw_gate = _matmul(u, dpg, ta=True, name=f"{tag}_gate_dw")
    dw_val = _matmul(u, dpv, ta=True, name=f"{tag}_val_dw")
    dh, dh16, dnw = _norm_bwd(h, norm_w, du, dh, f"{tag}_norm_bwd")
    return dh, dh16, dict(w_in=(dw_gate, dw_val), w_out=dw_out, conv_w=dcw, conv_b=dcb, norm=dnw)


def _hgrn2_fwd(h, norm_w, w_in, lb_logits, a_norm_w, w_out, after=None):
    d = D_MODEL
    u = _norm_fwd(h, norm_w, "hg_norm")
    pa = _matmul(u, w_in, after=after, name="hg_in")
    qs, lf_fw, k_fw, lf_bw, k_bw = _row_kernel(
        _hgrn2_pre_fn, [(pa, d, 0), (pa, d, 1), (pa, d, 2)], [lb_logits], [(d, F32)] * 5, [], name="hg_pre")
    o_fw, st_fw = _gla_fwd(qs, k_fw, pa, 3 * HG_HEADS, lf_fw, rev=False, add=None, name="hg_gla_fw")
    o, st_bw = _gla_fwd(qs, k_bw, pa, 3 * HG_HEADS, lf_bw, rev=True, add=o_fw, name="hg_gla_bw")
    y = _row_kernel(lambda ov, gv, wv: (_hgrn2_post_fn(ov, gv, wv),), [(o, d, 0), (pa, d, 4)], [a_norm_w],
                    [(d, BF16)], [], name="hg_post")[0]
    h_out = _matmul(y, w_out, add=h, name="hg_out")
    return h_out, (u, pa, qs, lf_fw, k_fw, lf_bw, k_bw, st_fw, st_bw, o, y)


def _hgrn2_bwd(h, dh, dh16, saved, norm_w, w_in, lb_logits, a_norm_w, w_out, send, after=None):
    d = D_MODEL
    u, pa, qs, lf_fw, k_fw, lf_bw, k_bw, st_fw, st_bw, o, y = saved
    dy = _matmul(dh16, w_out, nt=True, after=after, name="hg_out_dx")
    dw_out = _matmul(y, dh16, ta=True, name="hg_out_dw")

    def post_bwd(ov, gv, ct, wv):
        _, vjp = jax.vjp(_hgrn2_post_fn, ov, gv, wv)
        return vjp(ct)

    do, dg, dnw = _row_kernel(post_bwd, [(o, d, 0), (pa, d, 4), (dy, d, 0)], [a_norm_w], [(d, F32), (d, F32)],
                              [(1, HG_HEAD_DIM)], name="hg_post_bwd")
    dq1, dk_fw, dv1, dlf_fw = _gla_bwd(qs, k_fw, pa, 3 * HG_HEADS, lf_fw, do, st_fw, rev=False, adds=None,
                                       name="hg_gla_fw_bwd")
    dqs, dk_bw, dv, dlf_bw = _gla_bwd(qs, k_bw, pa, 3 * HG_HEADS, lf_bw, do, st_bw, rev=True, adds=(dq1, dv1),
                                      name="hg_gla_bw_bwd")

    def pre_bwd(qr, fr, br, c0, c1, c2, c3, c4, dvv, dgv, lbl):
        _, vjp = jax.vjp(_hgrn2_pre_fn, qr, fr, br, lbl)
        dq, df, db, dlbl = vjp((c0, c1, c2, c3, c4))
        return jnp.concatenate([dq, df, db, dvv, dgv], axis=1), dlbl

    rows = [(pa, d, 0), (pa, d, 1), (pa, d, 2), (dqs, d, 0), (dlf_fw, d, 0), (dk_fw, d, 0), (dlf_bw, d, 0),
            (dk_bw, d, 0), (dv, d, 0), (dg, d, 0)]
    dpa, dlbl = _row_kernel(pre_bwd, rows, [lb_logits], [(5 * d, BF16)], [lb_logits.shape], name="hg_pre_bwd")
    dw_in = _matmul(u, dpa, ta=True, name="hg_in_dw")
    token = send(dw_in, dw_out)
    du = _matmul(dpa, w_in, nt=True, out_dtype=BF16, after=token, name="hg_in_dx")
    dh, dh16, dn1 = _norm_bwd(h, norm_w, du, dh, "hg_norm_bwd")
    return dh, dh16, dict(lb=dlbl, a_norm=dnw, norm=dn1)


def _group_params(p):
    return p.reshape(SSM_GROUPS, 1, SSM_HPG)


def _mamba_fwd(h, norm_w, w_z, w_xbc, w_dt, conv_w, conv_b, dt_bias, a_log, d_skip, b_norm_w, w_out):
    L = h.shape[0]
    u = _norm_fwd(h, norm_w, "mb_norm")
    z = _matmul(u, w_z, name="mb_in_z")
    xbc_raw = _matmul(u, w_xbc, name="mb_in_xbc")
    dt_raw = _matmul(u, w_dt, name="mb_in_dt")
    xbc = _col_kernel(lambda xv, w, b: (_mamba_conv_fn(xv, w, b),), [(xbc_raw, 0)], [(conv_w, 0), (conv_b, 0)],
                      [F32], [], name="mb_conv", n_tiles=CONV_DIM // COL_TILE)[0]
    dtr = dt_raw.reshape(L, 2, SSM_GROUPS, SSM_HPG).transpose(1, 2, 0, 3)
    bias, alog = dt_bias.reshape(2, -1), a_log.reshape(2, -1)
    dsk = _group_params(d_skip.reshape(-1))
    y_fw, st_fw = _ssd_fwd(xbc, dtr[0], _group_params(bias[0]), _group_params(alog[0]), dsk, rev=False, skip=True,
                           add=None, name="mb_ssd_fw")
    ysum, st_bw = _ssd_fwd(xbc, dtr[1], _group_params(bias[1]), _group_params(alog[1]), dsk, rev=True, skip=False,
                           add=y_fw, name="mb_ssd_bw")
    y = _row_kernel(lambda yv, zv, wv: (_mamba_post_fn(yv, zv, wv),), [(ysum, D_INNER, 0), (z, D_INNER, 0)],
                    [b_norm_w], [(D_INNER, BF16)], [], name="mb_post")[0]
    h_out = _matmul(y, w_out, add=h, name="mb_out")
    return h_out, (u, z, xbc_raw, xbc, dtr, st_fw, st_bw, ysum, y)


def _mamba_bwd(h, dh, dh16, saved, norm_w, w_z, w_xbc, w_dt, conv_w, conv_b, dt_bias, a_log, d_skip, b_norm_w, w_out,
               after=None):
    L = h.shape[0]
    u, z, xbc_raw, xbc, dtr, st_fw, st_bw, ysum, y = saved
    dy = _matmul(dh16, w_out, nt=True, after=after, name="mb_out_dx")
    dw_out = _matmul(y, dh16, ta=True, name="mb_out_dw")

    def post_bwd(yv, zv, ct, wv):
        _, vjp = jax.vjp(_mamba_post_fn, yv, zv, wv)
        return vjp(ct)

    dys, dz, dbn = _row_kernel(post_bwd, [(ysum, D_INNER, 0), (z, D_INNER, 0), (dy, D_INNER, 0)], [b_norm_w],
                               [(D_INNER, F32), (D_INNER, BF16)], [(1, D_INNER)], name="mb_post_bwd")
    bias, alog = dt_bias.reshape(2, -1), a_log.reshape(2, -1)
    dsk = _group_params(d_skip.reshape(-1))
    dx1, db1, dc1, ddt_fw, dbias_fw, dalog_fw, ddsk = _ssd_bwd(
        xbc, dtr[0], _group_params(bias[0]), _group_params(alog[0]), dsk, dys, st_fw, rev=False, skip=True,
        adds=None, name="mb_ssd_fw_bwd")
    dx, db, dc, ddt_bw, dbias_bw, dalog_bw, _ = _ssd_bwd(
        xbc, dtr[1], _group_params(bias[1]), _group_params(alog[1]), dsk, dys, st_bw, rev=True, skip=False,
        adds=(dx1, db1, dc1), name="mb_ssd_bw_bwd")

    def conv_bwd(n_tiles, ct, first):
        def fn(xv, ctv, w, b):
            _, vjp = jax.vjp(_mamba_conv_fn, xv, w, b)
            return vjp(ctv)
        return _col_kernel(fn, [(xbc_raw, first), (ct, 0)], [(conv_w, first), (conv_b, first)], [BF16],
                           [SSM_CONV, 1], name=f"mb_conv_bwd_{first}", n_tiles=n_tiles)

    nx, nb = D_INNER // COL_TILE, GN // COL_TILE
    parts = [conv_bwd(nx, dx, 0), conv_bwd(nb, db, nx), conv_bwd(nb, dc, nx + nb)]
    dxbc = jnp.concatenate([p[0] for p in parts], axis=1)
    dcw = jnp.concatenate([p[1] for p in parts], axis=1)
    dcb = jnp.concatenate([p[2] for p in parts], axis=1)
    ddt = jnp.stack([ddt_fw, ddt_bw]).transpose(2, 0, 1, 3).reshape(L, 2 * SSM_GROUPS * SSM_HPG).astype(BF16)
    du = _matmul(dz, w_z, nt=True, name="mb_z_dx")
    du = _matmul(dxbc, w_xbc, nt=True, add=du, name="mb_xbc_dx")
    du = _matmul(ddt, w_dt, nt=True, add=du, out_dtype=BF16, name="mb_dt_dx")
    dw_in = jnp.concatenate([_matmul(u, dz, ta=True, name="mb_z_dw"), _matmul(u, dxbc, ta=True, name="mb_xbc_dw"),
                             _matmul(u, ddt, ta=True, name="mb_dt_dw")], axis=1)
    dh, dh16, dn1 = _norm_bwd(h, norm_w, du, dh, "mb_norm_bwd")
    grads = dict(w_in=dw_in, w_out=dw_out, conv_w=dcw, conv_b=dcb, b_norm=dbn, norm=dn1,
                 dt_bias=jnp.stack([dbias_fw, dbias_bw]).reshape(1, 2, -1),
                 a_log=jnp.stack([dalog_fw, dalog_bw]).reshape(1, 2, -1), d_skip=ddsk.reshape(1, -1))
    return dh, dh16, grads


def _loss_head(h, target, w):
    d = h.shape[1]

    def fn(hv, tv, wv):
        def loss(hv, wv):
            err = _rms(hv, wv) - tv
            return 0.5 * jnp.sum(jnp.mean(err * err, axis=-1, keepdims=True), axis=0, keepdims=True)
        val, vjp = jax.vjp(loss, hv, wv)
        dh, dw = vjp(jnp.ones((1, 1), F32))
        return dh, dh, val, dw

    return _row_kernel(fn, [(h, d, 0), (target, d, 0)], [w], [(d, F32), (d, BF16)], [(1, 1), (1, d)], name="loss_head")


def _local_step(x, target, rep, comm):
    def ffn_args(i, w, small):
        return (rep["norm2_w"][i:i + 1], w["w_in"][:, :D_FF], w["w_in"][:, D_FF:], w["w_out"],
                small["ffn_conv_w"][i], rep["ffn_conv_b"][i:i + 1])

    def mamba_args(w, small):
        w_in = w["w_in"]
        return (rep["norm1_w"][1:2], w_in[:, :D_INNER], w_in[:, D_INNER:D_INNER + CONV_DIM],
                w_in[:, D_INNER + CONV_DIM:], small["b_conv_w"][0], small["b_conv_b"], rep["b_dt_bias"],
                rep["b_a_log"], rep["b_d_skip"], small["b_norm_w"], w["w_out"])

    w_hg = comm.weights("hg", None)
    hg = (rep["norm1_w"][0:1], w_hg["w_in"], rep["a_lb_logits"], rep["a_norm_w"], w_hg["w_out"])
    h0 = x
    h1, s_hg = _hgrn2_fwd(h0, *hg, after=w_hg.get("token"))
    w_f0 = comm.weights("f0", h1)
    small = w_f0["small"]
    f0 = ffn_args(0, w_f0, small)
    h2, s_f0 = _ffn_fwd(h1, *f0, "ffn0")
    mb = mamba_args(comm.weights("mb", h2), small)
    h3, s_mb = _mamba_fwd(h2, *mb)
    f1 = ffn_args(1, comm.weights("f1", h3), small)
    h4, s_f1 = _ffn_fwd(h3, *f1, "ffn1")
    dh, dh16, loss, d_final = _loss_head(h4, target, rep["final_norm_w"].reshape(1, -1))

    dh, dh16, g_f1 = _ffn_bwd(h3, dh, dh16, s_f1, *f1, "ffn1")
    token = comm.send("f1", dict(w_in=g_f1["w_in"], w_out=g_f1["w_out"]))
    dh, dh16, g_mb = _mamba_bwd(h2, dh, dh16, s_mb, *mb, after=token)
    token = comm.send("mb", dict(w_in=g_mb["w_in"], w_out=g_mb["w_out"]))
    dh, dh16, g_f0 = _ffn_bwd(h1, dh, dh16, s_f0, *f0, "ffn0", after=token)
    token = comm.send("f0", dict(w_in=g_f0["w_in"], w_out=g_f0["w_out"]))
    small_grads = dict(b_conv_w=g_mb["conv_w"][None], b_conv_b=g_mb["conv_b"], b_norm_w=g_mb["b_norm"],
                       ffn_conv_w=jnp.stack([g_f0["conv_w"], g_f1["conv_w"]]))
    dh, dh16, g_hg = _hgrn2_bwd(
        h0, dh, dh16, s_hg, *hg, after=token,
        send=lambda dw_in, dw_out: comm.send("hg", dict(w_in=dw_in, w_out=dw_out, small=small_grads)))
    grads = dict(
        norm1_w=jnp.concatenate([g_hg["norm"], g_mb["norm"]], axis=0),
        norm2_w=jnp.concatenate([g_f0["norm"], g_f1["norm"]], axis=0),
        a_lb_logits=g_hg["lb"], a_norm_w=g_hg["a_norm"], b_dt_bias=g_mb["dt_bias"], b_a_log=g_mb["a_log"],
        b_d_skip=g_mb["d_skip"], ffn_conv_b=jnp.concatenate([g_f0["conv_b"], g_f1["conv_b"]], axis=0),
        final_norm_w=d_final.reshape(-1),
    )
    return loss, dh, grads


def _mesh_pos():
    return lax.axis_index("x"), lax.axis_index("y"), lax.axis_index("c")


N_COPIES = N_DEV - 1


def _comm_call(body, ins, out_shape, name):
    n = len(ins)
    hbm = pl.BlockSpec(memory_space=pl.ANY)
    return pl.pallas_call(
        body, name=name, out_shape=out_shape, in_specs=[hbm] * n, out_specs=[hbm] * n,
        scratch_shapes=[pltpu.SemaphoreType.DMA((n * N_COPIES,)), pltpu.SemaphoreType.DMA((n * N_COPIES,)),
                        pltpu.SemaphoreType.DMA((n,))],
    )(*ins)


def _all_gather(shards, name):
    n = len(shards)

    def body(*refs):
        x_refs, out_refs = refs[:n], refs[n:2 * n]
        send_sems, recv_sems, local_sems = refs[2 * n:]
        x, y, c = _mesh_pos()
        me, sibling = (x, y, c), (x, y, 1 - c)
        chips = [(1 - x, y), (x, 1 - y), (1 - x, 1 - y)]

        def copy(w, k, block, to, own=False):
            px, py, pc = block
            dst = out_refs[w].at[4 * px + 2 * py + pc]
            return pltpu.make_async_remote_copy(
                src_ref=x_refs[w] if own else dst, dst_ref=dst, send_sem=send_sems.at[w * N_COPIES + k],
                recv_sem=recv_sems.at[w * N_COPIES + k], device_id=to, device_id_type=MESH)

        mine = [pltpu.make_async_copy(x_refs[w], out_refs[w].at[4 * x + 2 * y + c], local_sems.at[w]) for w in range(n)]
        for cp in mine:
            cp.start()
        first = [copy(w, 1 + j, me, (*chip, c), own=True) for j, chip in enumerate(chips) for w in range(n)]
        first += [copy(w, 0, me, sibling, own=True) for w in range(n)]
        for cp in first:
            cp.start()
        passed = []
        for j, chip in enumerate(chips):
            for w in range(n):
                copy(w, 1 + j, (*chip, c), me).wait_recv()
                passed.append(copy(w, 4 + j, (*chip, c), sibling))
                passed[-1].start()
        for w in range(n):
            copy(w, 0, sibling, me).wait_recv()
        for j, chip in enumerate(chips):
            for w in range(n):
                copy(w, 4 + j, (*chip, 1 - c), me).wait_recv()
        for cp in first + passed:
            cp.wait_send()
        for cp in mine:
            cp.wait()

    out_shape = [jax.ShapeDtypeStruct((N_DEV, *s.shape), s.dtype) for s in shards]
    return _comm_call(body, shards, out_shape, name)


HBM_SPEC = pl.BlockSpec(memory_space=pltpu.HBM)
SEM_SPEC = pl.BlockSpec(memory_space=pltpu.SEMAPHORE)
EFFECT = pltpu.SideEffectType.DATAFLOW_SIDE_EFFECTING
PEER_ORDER = (4, 2, 6, 5, 3, 7, 1)


def _peer_copies(src_refs, land_refs, send_sems, recv_sems, gather):
    x, y, c = _mesh_pos()
    me = 4 * x + 2 * y + c
    copies = []
    for k in PEER_ORDER:
        px = 1 - x if k & 4 else x
        py = 1 - y if k & 2 else y
        pc = 1 - c if k & 1 else c
        for w, (src, land) in enumerate(zip(src_refs, land_refs)):
            copies.append(pltpu.make_async_remote_copy(
                src_ref=src if gather else src.at[4 * px + 2 * py + pc],
                dst_ref=land.at[me] if gather else land.at[k - 1],
                send_sem=send_sems.at[w * N_COPIES + k - 1], recv_sem=recv_sems.at[w * N_COPIES + k - 1],
                device_id=(px, py, pc), device_id_type=MESH))
    return copies


def _copies_start(srcs, *, gather, after, name):
    n = len(srcs)
    lands = [lax.empty(((N_DEV,) + s.shape) if gather else ((N_COPIES,) + s.shape[1:]), s.dtype) for s in srcs]

    def body(*refs):
        src_refs, land_refs = refs[:n], refs[n:2 * n]
        send_sems, recv_sems = refs[-2 * n - 3], refs[-2 * n - 2]
        for cp in _peer_copies(src_refs, land_refs, send_sems, recv_sems, gather):
            cp.start()
        refs[-1][...] = jnp.zeros(refs[-1].shape, F32)

    ins = [pltpu.with_memory_space_constraint(a, pltpu.HBM) for a in srcs + lands]
    in_specs = [HBM_SPEC] * (2 * n)
    if after is not None:
        ins.append(after)
        in_specs.append(pl.BlockSpec(memory_space=pl.ANY))
    sems = pltpu.SemaphoreType.DMA((n * N_COPIES,))
    out = pl.pallas_call(
        body, name=name,
        out_shape=(sems, sems, *[pltpu.HBM(a.shape, a.dtype) for a in srcs + lands], jax.ShapeDtypeStruct((8, LANES), F32)),
        in_specs=in_specs,
        out_specs=(SEM_SPEC, SEM_SPEC, *[HBM_SPEC] * (2 * n), pl.BlockSpec(memory_space=pltpu.VMEM)),
        input_output_aliases={i: 2 + i for i in range(2 * n)},
        compiler_params=pltpu.CompilerParams(has_side_effects=EFFECT),
    )(*ins)
    return out[0], out[1], list(out[2:2 + n]), list(out[2 + n:2 + 2 * n]), out[-1]


def _copies_wait(started, *, gather, after, name):
    send_sems, recv_sems, srcs, lands, _ = started
    n = len(srcs)

    def body(*refs):
        src_refs, land_refs = refs[:n], refs[n:2 * n]
        for cp in _peer_copies(src_refs, land_refs, refs[2 * n], refs[2 * n + 1], gather):
            cp.wait_send()
            cp.wait_recv()

    out = pl.pallas_call(
        body, name=name, out_shape=tuple(pltpu.HBM(a.shape, a.dtype) for a in srcs + lands),
        in_specs=[HBM_SPEC] * (2 * n) + [SEM_SPEC, SEM_SPEC, pl.BlockSpec(memory_space=pl.ANY)],
        out_specs=tuple([HBM_SPEC] * (2 * n)), input_output_aliases={i: i for i in range(2 * n)},
        compiler_params=pltpu.CompilerParams(has_side_effects=EFFECT),
    )(*srcs, *lands, send_sems, recv_sems, after)
    return list(out[n:])


def _adamw(got, w, m, v, *, name, tile, own=None):
    rows, width = w.shape
    n_got = got.shape[0]
    c1 = 1.0 / (1.0 - ADAM_B1 ** ADAM_STEP)
    c2 = 1.0 / (1.0 - ADAM_B2 ** ADAM_STEP)

    def body(*refs):
        got_ref, w_ref, m_ref, v_ref = refs[:4]
        g_ref, d_ref, nm_ref, nv_ref = refs[-4:]
        g = got_ref[0] if own is None else refs[4][...] + got_ref[0]
        for s in range(1, n_got):
            g = g + got_ref[s]
        m_new = ADAM_B1 * m_ref[...] + (1.0 - ADAM_B1) * g
        v_new = ADAM_B2 * v_ref[...] + (1.0 - ADAM_B2) * (g * g)
        g_ref[...] = g
        nm_ref[...] = m_new
        nv_ref[...] = v_new
        d_ref[...] = -ADAM_LR * ((m_new * c1) / (jnp.sqrt(v_new * c2) + ADAM_EPS) + ADAM_WD * w_ref[...])

    spec = pl.BlockSpec((tile, width), lambda i: (i, 0))
    out = jax.ShapeDtypeStruct((rows, width), F32)
    return pl.pallas_call(
        body, name=name, grid=(rows // tile,),
        in_specs=[pl.BlockSpec((n_got, tile, width), lambda i: (0, i, 0))] + [spec] * (3 if own is None else 4),
        out_specs=[spec] * 4, out_shape=[out] * 4, compiler_params=_cparams("parallel"),
    )(*([got, w, m, v] + ([] if own is None else [own])))


SHARDED = dict(a_w_in=(2, True), a_w_out=(1, True), b_w_in=(2, True), b_w_out=(1, True), ffn_w_in=(2, True),
               ffn_w_out=(1, True), b_conv_w=(2, False), b_conv_b=(1, False), b_norm_w=(1, False), ffn_conv_w=(2, False))
REPLICATED = ("norm1_w", "norm2_w", "a_lb_logits", "a_norm_w", "b_dt_bias", "b_a_log", "b_d_skip", "ffn_conv_b",
              "final_norm_w")
WEIGHTS = ("norm1_w", "norm2_w", "a_w_in", "a_lb_logits", "a_norm_w", "a_w_out", "b_w_in", "b_conv_w", "b_conv_b",
           "b_dt_bias", "b_a_log", "b_d_skip", "b_norm_w", "b_w_out", "ffn_w_in", "ffn_conv_w", "ffn_conv_b",
           "ffn_w_out", "final_norm_w")


def _pad_rows(flat, multiple):
    n = flat.shape[-1]
    per = PACK_W * multiple
    total = -(-n // per) * per
    flat = jnp.pad(flat, [(0, 0)] * (flat.ndim - 1) + [(0, total - n)])
    return flat.reshape(*flat.shape[:-1], total // PACK_W, PACK_W)


def _to_parts(full, axis, n=N_DEV):
    shp = full.shape
    t = full.reshape(*shp[:axis], n, shp[axis] // n, *shp[axis + 1:])
    return jnp.moveaxis(t, axis, 0)


def _from_parts(parts, axis):
    t = jnp.moveaxis(parts, 0, axis)
    shp = t.shape
    return t.reshape(*shp[:axis], shp[axis] * shp[axis + 1], *shp[axis + 2:])


BIG = tuple(n for n, (_, mm) in SHARDED.items() if mm)
SMALL = tuple(n for n, (_, mm) in SHARDED.items() if not mm)
SMALL_W = 512


def _small_rows(tree, lead):
    rows = []
    for n in SMALL:
        t = tree[n]
        t = t.reshape(*lead, -1, t.shape[-1])
        rows.append(jnp.pad(t, [(0, 0)] * (t.ndim - 1) + [(0, SMALL_W - t.shape[-1])]))
    buf = jnp.concatenate(rows, axis=-2)
    return jnp.pad(buf, [(0, 0)] * (buf.ndim - 2) + [(0, 16 - buf.shape[-2]), (0, 0)])


def _small_unrows(buf, like, lead):
    out, r = {}, 0
    for n in SMALL:
        shp = like[n].shape
        k = like[n].size // shp[-1]
        out[n] = buf[..., r:r + k, :shp[-1]].reshape(*lead, *shp)
        r += k
    return out


GROUPS = dict(hg=(("a_w_in", 0), ("a_w_out", 0)), f0=(("ffn_w_in", 0), ("ffn_w_out", 0)),
              mb=(("b_w_in", 0), ("b_w_out", 0)), f1=(("ffn_w_in", 1), ("ffn_w_out", 1)))


class _Comm:
    def __init__(self, local):
        x, y, c = _mesh_pos()
        self.me = 4 * x + 2 * y + c
        self.local = local
        self.shards = {g: [local[n][i].astype(BF16) for n, i in names] for g, names in GROUPS.items()}
        self.shards["f0"].append(_small_rows(local, ()))
        self.first = _all_gather(self.shards["hg"], "gather_hg")
        self.gathers, self.sent, token = {}, {}, None
        for g in ("f0", "mb", "f1"):
            self.gathers[g] = _copies_start(self.shards[g], gather=True, after=token, name=f"gather_{g}_start")
            token = self.gathers[g][-1]
        self.token = token

    def weights(self, group, after):
        if group == "hg":
            got = self.first
        else:
            lands = _copies_wait(self.gathers[group], gather=True, after=after, name=f"gather_{group}_wait")
            got = [lax.dynamic_update_index_in_dim(land, shard, self.me, 0)
                   for land, shard in zip(lands, self.shards[group])]
        out = dict(w_in=_from_parts(got[0], 1), w_out=_from_parts(got[1], 0))
        if group == "hg":
            out["token"] = self.token
        if group == "f0":
            small = _small_unrows(got[2], self.local, (N_DEV,))
            out["small"] = {n: _from_parts(small[n], SHARDED[n][0]) for n in SMALL}
        return out

    def send(self, group, grads):
        w_in = grads["w_in"]
        if isinstance(w_in, tuple):
            half = N_DEV // len(w_in)
            parts_in = jnp.concatenate([_to_parts(t, 1, half) for t in w_in], axis=0)
        else:
            parts_in = _to_parts(w_in, 1)
        parts = [parts_in, _to_parts(grads["w_out"], 0)]
        if "small" in grads:
            parts.append(_small_rows({n: _to_parts(grads["small"][n], SHARDED[n][0]) for n in SMALL}, (N_DEV,)))
        sent = [p.astype(BF16) for p in parts[:2]] + parts[2:] if group == "hg" else parts
        self.sent[group] = (parts, _copies_start(sent, gather=False, after=None, name=f"exchange_{group}_start"))
        return self.sent[group][1][-1]

    def finish(self, after, mom, var):
        res = {n: {} for n in BIG}
        for group in ("f1", "mb", "f0", "hg"):
            parts, started = self.sent[group]
            lands = _copies_wait(started, gather=False, after=after, name=f"exchange_{group}_wait")
            own = [lax.dynamic_index_in_dim(p, self.me, 0, keepdims=False) for p in parts]
            for (n, i), got, mine in zip(GROUPS[group], lands, own):
                out = _adamw(got, self.local[n][i], mom[n][i], var[n][i], own=mine, name=f"adamw_{n}_{i}",
                             tile=_pick(mine.shape[0], (256, 176, 128)))
                res[n][i] = out
                after = out[0]
        small = _adamw(lands[2], *[_small_rows(t, ()) for t in (self.local, mom, var)], own=own[2],
                       name="adamw_small", tile=16)
        return res, small


def _pack_small(tree, extra):
    flat = jnp.concatenate([tree[n].reshape(-1) for n in REPLICATED] + [extra.reshape(-1)])
    return _pad_rows(flat, 8)


def _unpack_small(pack, like):
    flat, out, off = pack.reshape(-1), {}, 0
    for n in REPLICATED:
        out[n] = flat[off:off + like[n].size].reshape(like[n].shape)
        off += like[n].size
    return out, flat[off]


def kernel(x, norm1_w, norm2_w, a_w_in, a_lb_logits, a_norm_w, a_w_out, b_w_in, b_conv_w, b_conv_b, b_dt_bias, b_a_log, b_d_skip, b_norm_w, b_w_out, ffn_w_in, ffn_conv_w, ffn_conv_b, ffn_w_out, final_norm_w, loss_target, m_norm1_w, m_norm2_w, m_a_w_in, m_a_lb_logits, m_a_norm_w, m_a_w_out, m_b_w_in, m_b_conv_w, m_b_conv_b, m_b_dt_bias, m_b_a_log, m_b_d_skip, m_b_norm_w, m_b_w_out, m_ffn_w_in, m_ffn_conv_w, m_ffn_conv_b, m_ffn_w_out, m_final_norm_w, v_norm1_w, v_norm2_w, v_a_w_in, v_a_lb_logits, v_a_norm_w, v_a_w_out, v_b_w_in, v_b_conv_w, v_b_conv_b, v_b_dt_bias, v_b_a_log, v_b_d_skip, v_b_norm_w, v_b_w_out, v_ffn_w_in, v_ffn_conv_w, v_ffn_conv_b, v_ffn_w_out, v_final_norm_w):
    given = dict(locals())
    local = {n: given[n] for n in WEIGHTS}
    mom = {n: given["m_" + n] for n in WEIGHTS}
    var = {n: given["v_" + n] for n in WEIGHTS}

    comm = _Comm(local)
    loss, grad_x, grads = _local_step(x[0], loss_target[0], local, comm)

    res, small_res = comm.finish(grad_x, mom, var)
    outs = ({}, {}, {}, {})
    for n in BIG:
        for out, layers in zip(outs, zip(*[res[n][i] for i in sorted(res[n])])):
            out[n] = jnp.stack(layers)
    for out, r in zip(outs, small_res):
        out.update(_small_unrows(r, local, ()))
    out_g, out_d, out_m, out_v = outs

    small = _pack_small(grads, loss)
    rows = small.shape[0]
    got_s = _all_gather([small], "gather_small")[0]
    zero = jnp.zeros((1,), F32)
    g, dlt, nm, nv = _adamw(got_s, _pack_small(local, zero), _pack_small(mom, zero), _pack_small(var, zero),
                            name="adamw_replicated", tile=rows)
    (rep_g, total), (rep_d, _), (rep_m, _), (rep_v, _) = (_unpack_small(t, local) for t in (g, dlt, nm, nv))
    out_g.update(rep_g)
    out_d.update(rep_d)
    out_m.update(rep_m)
    out_v.update(rep_v)

    return (total, grad_x[None], *[out_g[n] for n in WEIGHTS], *[out_d[n] for n in WEIGHTS],
            *[out_m[n] for n in WEIGHTS], *[out_v[n] for n in WEIGHTS])
```

```python
import functools

import jax
import jax.numpy as jnp
from jax import lax
from jax.experimental import pallas as pl
from jax.experimental.pallas import tpu as pltpu

F32 = jnp.float32
BF16 = jnp.bfloat16
HIGHEST = lax.Precision.HIGHEST
MESH = pl.DeviceIdType.MESH

N_DEV = 8
EPS = 1e-6
D_MODEL = 1024
HG_HEADS = 8
HG_HEAD_DIM = 128
D_INNER = 2048
SSM_HEAD_DIM = 64
SSM_GROUPS = 8
SSM_HPG = 4
SSM_STATE = 128
GN = SSM_GROUPS * SSM_STATE
CONV_DIM = D_INNER + 2 * GN
D_FF = 2816
SSM_CONV = 5
FFN_CONV = 3

ADAM_LR = 0.001
ADAM_B1 = 0.9
ADAM_B2 = 0.999
ADAM_EPS = 1e-08
ADAM_WD = 0.01
ADAM_STEP = 10

LANES = 128
ROW_TILE = 256
COL_TILE = 128
GLA_CHUNK = 16
GLA_BLOCK = 256
GLA_HEADS_PER_STEP = 8
SSD_CHUNK = 128
SSD_BLOCK = 512
PACK_W = 1024
VMEM_LIMIT = 56 * 1024 * 1024

NT_DIMS = (((1,), (1,)), ((), ()))
TN_DIMS = (((0,), (0,)), ((), ()))


def _cparams(*sem):
    return pltpu.CompilerParams(dimension_semantics=sem, vmem_limit_bytes=VMEM_LIMIT)


def _rms(x, w):
    return x * lax.rsqrt(jnp.mean(x * x, axis=-1, keepdims=True) + EPS) * w


def _row_kernel(body_fn, rows, params, row_outs, acc_outs, *, name, tile=ROW_TILE):
    L = rows[0][0].shape[0]
    tile = min(tile, L)
    n_in = len(rows) + len(params)
    n_ro = len(row_outs)

    def body(*refs):
        outs = body_fn(*[r[...] for r in refs[:n_in]])
        for ref, o in zip(refs[n_in:n_in + n_ro], outs[:n_ro]):
            ref[...] = o.astype(ref.dtype)
        first = pl.program_id(0) == 0
        for ref, o in zip(refs[n_in + n_ro:], outs[n_ro:]):
            @pl.when(first)
            def _(ref=ref):
                ref[...] = jnp.zeros(ref.shape, ref.dtype)
            ref[...] += o

    in_specs = [pl.BlockSpec((tile, w), lambda i, cb=cb: (i, cb)) for _, w, cb in rows]
    in_specs += [pl.BlockSpec(p.shape, lambda i: (0, 0)) for p in params]
    out_specs = [pl.BlockSpec((tile, w), lambda i: (i, 0)) for w, _ in row_outs]
    out_specs += [pl.BlockSpec(s, lambda i: (0, 0)) for s in acc_outs]
    out_shape = [jax.ShapeDtypeStruct((L, w), dt) for w, dt in row_outs]
    out_shape += [jax.ShapeDtypeStruct(s, F32) for s in acc_outs]
    return pl.pallas_call(
        body, name=name, grid=(L // tile,), in_specs=in_specs, out_specs=out_specs, out_shape=out_shape,
        compiler_params=_cparams("arbitrary" if acc_outs else "parallel"),
    )(*[a for a, _, _ in rows], *params)


def _col_kernel(body_fn, cols, params, col_outs, par_outs, *, name, n_tiles):
    L = cols[0][0].shape[0]
    n_in = len(cols) + len(params)
    width = n_tiles * COL_TILE

    def body(*refs):
        outs = body_fn(*[r[...] for r in refs[:n_in]])
        for ref, o in zip(refs[n_in:], outs):
            ref[...] = o.astype(ref.dtype)

    in_specs = [pl.BlockSpec((L, COL_TILE), lambda j, cb=cb: (0, cb + j)) for _, cb in cols]
    in_specs += [pl.BlockSpec((p.shape[0], COL_TILE), lambda j, cb=cb: (0, cb + j)) for p, cb in params]
    out_specs = [pl.BlockSpec((L, COL_TILE), lambda j: (0, j)) for _ in col_outs]
    out_specs += [pl.BlockSpec((k, COL_TILE), lambda j: (0, j)) for k in par_outs]
    out_shape = [jax.ShapeDtypeStruct((L, width), dt) for dt in col_outs]
    out_shape += [jax.ShapeDtypeStruct((k, width), F32) for k in par_outs]
    return pl.pallas_call(
        body, name=name, grid=(n_tiles,), in_specs=in_specs, out_specs=out_specs, out_shape=out_shape,
        compiler_params=_cparams("parallel"),
    )(*[a for a, _ in cols], *[p for p, _ in params])


def _pick(n, options):
    for t in options:
        if n % t == 0:
            return t
    return n


MATMUL_VMEM = 40 * 1024 * 1024


def _matmul_tiles(M, N, K, out_bytes):
    best = None
    for tm in (1024, 512, 256, 128, M):
        for tn in (1408, 1024, 512, 256, 128, N):
            if M % tm or N % tn:
                continue
            if 2 * (2 * K * (tm + tn) + out_bytes * tm * tn) > MATMUL_VMEM:
                continue
            if best is None or tm * tn > best[0] * best[1]:
                best = (tm, tn)
    return best


def _matmul(a, b, *, name, nt=False, ta=False, add=None, out_dtype=F32, after=None):
    K, M = a.shape[::-1] if not ta else a.shape
    N = b.shape[0] if nt else b.shape[1]
    tm, tn = _matmul_tiles(M, N, K, 4 * (1 + (add is not None)) if out_dtype == F32 else 2 + 4 * (add is not None))

    def body(*refs):
        a_ref, b_ref = refs[0], refs[1]
        o_ref = refs[-1]
        dims = TN_DIMS if ta else NT_DIMS if nt else (((1,), (0,)), ((), ()))
        acc = lax.dot_general(a_ref[...], b_ref[...], dims, preferred_element_type=F32)
        if add is not None:
            acc = acc + refs[2][...]
        o_ref[...] = acc.astype(o_ref.dtype)

    in_specs = [pl.BlockSpec((K, tm), lambda i, j: (0, i)) if ta else pl.BlockSpec((tm, K), lambda i, j: (i, 0)),
                pl.BlockSpec((tn, K), lambda i, j: (j, 0)) if nt else pl.BlockSpec((K, tn), lambda i, j: (0, j))]
    args = [a, b]
    if add is not None:
        in_specs.append(pl.BlockSpec((tm, tn), lambda i, j: (i, j)))
        args.append(add)
    if after is not None:
        in_specs.append(pl.BlockSpec(memory_space=pl.ANY))
        args.append(after)
    return pl.pallas_call(
        body, name=name, grid=(M // tm, N // tn), in_specs=in_specs,
        out_specs=pl.BlockSpec((tm, tn), lambda i, j: (i, j)),
        out_shape=jax.ShapeDtypeStruct((M, N), out_dtype),
        compiler_params=_cparams("parallel", "parallel"),
    )(*args)


def _hgrn2_pre_fn(q_raw, ffw_raw, fbw_raw, lb_logits):
    lb = jax.nn.softmax(lb_logits, axis=0)[0:1]

    def gate(fr):
        f = lb + (1.0 - lb) * jax.nn.sigmoid(fr)
        return jnp.log(f), 1.0 - f

    lf_fw, k_fw = gate(ffw_raw)
    lf_bw, k_bw = gate(fbw_raw)
    return jax.nn.silu(q_raw), lf_fw, k_fw, lf_bw, k_bw


def _hgrn2_post_fn(o, g, norm_w):
    outs = []
    for h in range(HG_HEADS):
        sl = slice(h * HG_HEAD_DIM, (h + 1) * HG_HEAD_DIM)
        outs.append(_rms(o[:, sl], norm_w) * jax.nn.silu(g[:, sl]))
    return jnp.concatenate(outs, axis=1)


def _mamba_post_fn(y, z, norm_w):
    outs = []
    gw = D_INNER // SSM_GROUPS
    for gi in range(SSM_GROUPS):
        sl = slice(gi * gw, (gi + 1) * gw)
        outs.append(_rms(y[:, sl] * jax.nn.silu(z[:, sl]), norm_w[:, sl]))
    return jnp.concatenate(outs, axis=1)


def _shift_rows_impl(x, d):
    if d == 0:
        return x
    n, edge = x.shape[0], 8
    t = lax.broadcasted_iota(jnp.int32, (edge, x.shape[1]), 0)
    rolled = pltpu.roll(x, d % n, 0)
    if d > 0:
        return jnp.concatenate([jnp.where(t >= d, rolled[:edge], 0.0), rolled[edge:]], axis=0)
    return jnp.concatenate([rolled[:n - edge], jnp.where(t < edge + d, rolled[n - edge:], 0.0)], axis=0)


@functools.partial(jax.custom_vjp, nondiff_argnums=(1,))
def _shift_rows(x, d):
    return _shift_rows_impl(x, d)


_shift_rows.defvjp(lambda x, d: (_shift_rows_impl(x, d), None), lambda d, _, g: (_shift_rows_impl(g, -d),))


def _dwconv(x, w, b):
    taps = w.shape[0]
    c = (taps - 1) // 2
    y = b + w[0:1, :] * _shift_rows(x, c)
    for k in range(1, taps):
        y = y + w[k:k + 1, :] * _shift_rows(x, c - k)
    return y


def _ffn_mid_fn(gate, val, w, b):
    return jax.nn.silu(_dwconv(gate, w, b)) * val


def _mamba_conv_fn(xbc, w, b):
    return jax.nn.silu(_dwconv(xbc, w, b))


def _gla_consts(rev):
    return lax.broadcasted_iota(jnp.int32, (GLA_CHUNK, HG_HEAD_DIM), 0)


def _segment_cumsum(x, seg, rev):
    n = x.shape[0]
    t = lax.broadcasted_iota(jnp.int32, x.shape, 0) & (seg - 1)
    s = 1
    while s < seg:
        if rev:
            x = x + jnp.where(t < seg - s, pltpu.roll(x, n - s, 0), 0.0)
        else:
            x = x + jnp.where(t >= s, pltpu.roll(x, s, 0), 0.0)
        s *= 2
    return x


def _segment_cumsum_mxu(x, seg, rev):
    r = lax.broadcasted_iota(jnp.int32, (seg, seg), 0)
    c = lax.broadcasted_iota(jnp.int32, (seg, seg), 1)
    tri = ((r <= c) if rev else (r >= c)).astype(BF16)
    tri3 = jnp.concatenate([tri, tri, tri], axis=1)
    hi = x.astype(BF16)
    rest = x - hi.astype(F32)
    mid = rest.astype(BF16)
    lo = (rest - mid.astype(F32)).astype(BF16)
    outs = []
    for g in range(x.shape[0] // seg):
        rows = slice(g * seg, (g + 1) * seg)
        terms = jnp.concatenate([hi[rows], mid[rows], lo[rows]], axis=0)
        outs.append(jnp.dot(tri3, terms, preferred_element_type=F32))
    return jnp.concatenate(outs, axis=0)


@functools.partial(jax.custom_vjp, nondiff_argnums=(1, 2))
def _segment_cumsum_diff(x, seg, rev):
    return _segment_cumsum_mxu(x, seg, rev)


_segment_cumsum_diff.defvjp(lambda x, seg, rev: (_segment_cumsum_mxu(x, seg, rev), None),
                            lambda seg, rev, _, g: (_segment_cumsum_mxu(g, seg, not rev),))


def _gla_chunk(st, q, k, v, b, *, rev, consts):
    row = consts
    c = q.shape[0]
    o = lax.dot_general((q * jnp.exp(b)).astype(BF16), st.astype(BF16), NT_DIMS, preferred_element_type=F32)
    for s in range(c):
        e = jnp.exp(jnp.where((row <= s) if rev else (row >= s), b - b[s:s + 1], -jnp.inf))
        o = o + jnp.sum(q * (k[s:s + 1] * e), axis=-1, keepdims=True) * v[s:s + 1]
    b_end = b[0:1] if rev else b[c - 1:c]
    kd = (k * jnp.exp(b_end - b)).astype(BF16)
    st_new = st * jnp.exp(b_end) + lax.dot_general(v.astype(BF16), kd, TN_DIMS, preferred_element_type=F32)
    return st_new, o


def _gla_fwd(q, k, v, v_cb, g, *, rev, add, name):
    L = q.shape[0]
    blk = min(GLA_BLOCK, L)
    nblk, nsub = L // blk, blk // GLA_CHUNK
    hd, hps = HG_HEAD_DIM, GLA_HEADS_PER_STEP
    wide = hd * hps

    def body(*refs):
        q_ref, k_ref, v_ref, g_ref = refs[:4]
        add_ref = refs[4] if add is not None else None
        o_ref, st_out, st_scr, b_scr = refs[-4:]
        consts = _gla_consts(rev)

        @pl.when(pl.program_id(1) == 0)
        def _():
            st_scr[...] = jnp.zeros(st_scr.shape, F32)

        st_out[...] = st_scr[...]
        b_scr[...] = _segment_cumsum(g_ref[...], GLA_CHUNK, rev)

        def step(i, carry):
            sl = pl.ds(pl.multiple_of((nsub - 1 - i if rev else i) * GLA_CHUNK, GLA_CHUNK), GLA_CHUNK)
            lanes = [slice(hi * hd, (hi + 1) * hd) for hi in range(hps)]
            ins = [(st_scr[hi], q_ref[sl, ln], k_ref[sl, ln], v_ref[sl, ln], b_scr[sl, ln])
                   for hi, ln in enumerate(lanes)]
            adds = [add_ref[sl, ln] for ln in lanes] if add_ref is not None else None
            outs = [_gla_chunk(*args, rev=rev, consts=consts) for args in ins]
            for hi, ln in enumerate(lanes):
                st_scr[hi] = outs[hi][0]
                o_ref[sl, ln] = outs[hi][1] if adds is None else outs[hi][1] + adds[hi]
            return carry

        lax.fori_loop(0, nsub, step, 0)

    def pos(j):
        return nblk - 1 - j if rev else j

    spec = pl.BlockSpec((blk, wide), lambda h, j: (pos(j), h))
    in_specs = [spec, spec, pl.BlockSpec((blk, wide), lambda h, j: (pos(j), v_cb // hps + h)), spec]
    args = [q, k, v, g]
    if add is not None:
        in_specs.append(spec)
        args.append(add)
    return pl.pallas_call(
        body, name=name, grid=(HG_HEADS // hps, nblk), in_specs=in_specs,
        out_specs=[spec, pl.BlockSpec((hps, None, hd, hd), lambda h, j: (h, j, 0, 0))],
        out_shape=[jax.ShapeDtypeStruct((L, HG_HEADS * hd), F32),
                   jax.ShapeDtypeStruct((HG_HEADS, nblk, hd, hd), F32)],
        scratch_shapes=[pltpu.VMEM((hps, hd, hd), F32), pltpu.VMEM((blk, wide), F32)],
        compiler_params=_cparams("parallel", "arbitrary"),
    )(*args)


def _gla_bwd(q, k, v, v_cb, g, do, states, *, rev, adds, name):
    L = q.shape[0]
    blk = min(GLA_BLOCK, L)
    nblk, nsub = L // blk, blk // GLA_CHUNK
    hd, hps = HG_HEAD_DIM, GLA_HEADS_PER_STEP
    wide = hd * hps
    n_add = 0 if adds is None else 2

    def body(*refs):
        q_ref, k_ref, v_ref, g_ref, do_ref, st_in = refs[:6]
        add_refs = refs[6:6 + n_add]
        dq_ref, dk_ref, dv_ref, dg_ref, dst_scr, sub_scr, b_scr, db_scr = refs[6 + n_add:]
        consts = _gla_consts(rev)
        chunk = functools.partial(_gla_chunk, rev=rev, consts=consts)

        @pl.when(pl.program_id(1) == 0)
        def _():
            dst_scr[...] = jnp.zeros(dst_scr.shape, F32)

        b_scr[...] = _segment_cumsum(g_ref[...], GLA_CHUNK, rev)

        def rows(i):
            return pl.ds(pl.multiple_of((nsub - 1 - i if rev else i) * GLA_CHUNK, GLA_CHUNK), GLA_CHUNK)

        for hi in range(hps):
            sub_scr[hi] = st_in[hi]

        def replay(i, carry):
            sl = rows(i)
            lanes = [slice(hi * hd, (hi + 1) * hd) for hi in range(hps)]
            ins = [(sub_scr[i * hps + hi], q_ref[sl, ln], k_ref[sl, ln], v_ref[sl, ln], b_scr[sl, ln])
                   for hi, ln in enumerate(lanes)]
            outs = [chunk(*args)[0] for args in ins]
            for hi in range(hps):
                sub_scr[(i + 1) * hps + hi] = outs[hi]
            return carry

        lax.fori_loop(0, nsub - 1, replay, 0)

        def step(n, carry):
            i = nsub - 1 - n
            sl = rows(i)
            lanes = [slice(hi * hd, (hi + 1) * hd) for hi in range(hps)]
            ins = [(sub_scr[i * hps + hi], q_ref[sl, ln], k_ref[sl, ln], v_ref[sl, ln], b_scr[sl, ln])
                   for hi, ln in enumerate(lanes)]
            cts = [(dst_scr[hi], do_ref[sl, ln]) for hi, ln in enumerate(lanes)]
            adds = [(add_refs[0][sl, ln], add_refs[1][sl, ln]) for ln in lanes] if n_add else None
            outs = [jax.vjp(chunk, *args)[1](ct) for args, ct in zip(ins, cts)]
            for hi, ln in enumerate(lanes):
                dst, dq, dk, dv, db = outs[hi]
                dst_scr[hi] = dst
                dq_ref[sl, ln] = dq if adds is None else dq + adds[hi][0]
                dk_ref[sl, ln] = dk
                dv_ref[sl, ln] = dv if adds is None else dv + adds[hi][1]
                db_scr[sl, ln] = db
            return carry

        lax.fori_loop(0, nsub, step, 0)
        dg_ref[...] = _segment_cumsum(db_scr[...], GLA_CHUNK, not rev)

    def pos(j):
        p = nblk - 1 - j
        return nblk - 1 - p if rev else p

    spec = pl.BlockSpec((blk, wide), lambda h, j: (pos(j), h))
    in_specs = [spec, spec, pl.BlockSpec((blk, wide), lambda h, j: (pos(j), v_cb // hps + h)), spec, spec,
                pl.BlockSpec((hps, None, hd, hd), lambda h, j: (h, nblk - 1 - j, 0, 0))]
    args = [q, k, v, g, do, states]
    if adds is not None:
        in_specs += [spec, spec]
        args += list(adds)
    out = jax.ShapeDtypeStruct((L, HG_HEADS * hd), F32)
    return pl.pallas_call(
        body, name=name, grid=(HG_HEADS // hps, nblk), in_specs=in_specs,
        out_specs=[spec] * 4, out_shape=[out] * 4,
        scratch_shapes=[pltpu.VMEM((hps, hd, hd), F32), pltpu.VMEM((nsub * hps, hd, hd), F32),
                        pltpu.VMEM((blk, wide), F32), pltpu.VMEM((blk, wide), F32)],
        compiler_params=_cparams("parallel", "arbitrary"),
    )(*args)


def _ssd_consts(rev):
    c = SSD_CHUNK
    gw = SSM_HPG * SSM_HEAD_DIM
    r2 = lax.broadcasted_iota(jnp.int32, (c, c), 0)
    c2 = lax.broadcasted_iota(jnp.int32, (c, c), 1)
    low = (r2 <= c2) if rev else (r2 >= c2)
    lane = lax.broadcasted_iota(jnp.int32, (1, gw), 1)
    return low, r2 == c2, lane, lane >> 6


def _expand_heads(v4, head_of_lane):
    first = head_of_lane[:, :LANES] == 0
    cols = [v4[:, j:j + 1] for j in range(SSM_HPG)]
    return jnp.concatenate([jnp.where(first, cols[0], cols[1]), jnp.where(first, cols[2], cols[3])], axis=1)


def _ssd_prep(dtr, bias, alog, dsk, *, rev, consts):
    head_of_lane = consts[-1]
    dt_l = _expand_heads(jax.nn.softplus(dtr + bias), head_of_lane)
    a_l = _expand_heads(-jnp.exp(alog), head_of_lane)
    return dt_l, _segment_cumsum_diff(dt_l * a_l, SSD_CHUNK, rev), _expand_heads(dsk, head_of_lane)


def _ssd_chunk(st, x, bm, cm, dt_l, acum, dsk_l, *, rev, consts, skip):
    low, eye, lane, head_of_lane = consts
    c = x.shape[0]
    xd = x * dt_l
    cb = lax.dot_general(cm.astype(BF16), bm.astype(BF16), NT_DIMS, preferred_element_type=F32)
    y = jnp.dot(cm.astype(BF16), st.astype(BF16), preferred_element_type=F32) * jnp.exp(acum)
    for j in range(SSM_HPG):
        acol = jnp.sum(jnp.where(lane == j * SSM_HEAD_DIM, acum, 0.0), axis=-1, keepdims=True)
        ab = jnp.broadcast_to(acol, (c, c))
        arow = jnp.sum(jnp.where(eye, ab, 0.0), axis=0, keepdims=True)
        lmat = jnp.exp(jnp.where(low, ab - arow, -jnp.inf))
        xj = jnp.where(head_of_lane == j, xd, 0.0)
        y = y + jnp.dot((cb * lmat).astype(BF16), xj.astype(BF16), preferred_element_type=F32)
    a_end = acum[0:1] if rev else acum[c - 1:c]
    xdec = (xd * jnp.exp(a_end - acum)).astype(BF16)
    st_new = st * jnp.exp(a_end) + lax.dot_general(bm.astype(BF16), xdec, TN_DIMS, preferred_element_type=F32)
    if skip:
        y = y + x * dsk_l
    return st_new, y


def _ssd_specs(L, rev):
    blk = min(SSD_BLOCK, L)
    nblk = L // blk
    gw = SSM_HPG * SSM_HEAD_DIM

    def pos(j):
        return nblk - 1 - j if rev else j

    return blk, nblk, gw, pos


def _ssd_fwd(xbc, dtr, bias, alog, dsk, *, rev, skip, add, name):
    L = xbc.shape[0]
    blk, nblk, gw, pos = _ssd_specs(L, rev)
    nsub = blk // SSD_CHUNK
    n = SSM_STATE

    def body(*refs):
        x_ref, b_ref, c_ref, dt_ref, bias_ref, alog_ref, dsk_ref = refs[:7]
        add_ref = refs[7] if add is not None else None
        y_ref, st_out, st_scr, dt_scr, ac_scr = refs[-5:]
        consts = _ssd_consts(rev)

        @pl.when(pl.program_id(1) == 0)
        def _():
            st_scr[...] = jnp.zeros(st_scr.shape, F32)

        st_out[...] = st_scr[...]
        dt_scr[...], ac_scr[...], dsk_l = _ssd_prep(dt_ref[...], bias_ref[...], alog_ref[...], dsk_ref[...],
                                                   rev=rev, consts=consts)

        def step(i, carry):
            sl = pl.ds(pl.multiple_of((nsub - 1 - i if rev else i) * SSD_CHUNK, SSD_CHUNK), SSD_CHUNK)
            st_new, y = _ssd_chunk(st_scr[...], x_ref[sl, :], b_ref[sl, :], c_ref[sl, :], dt_scr[sl, :],
                                   ac_scr[sl, :], dsk_l, rev=rev, consts=consts, skip=skip)
            st_scr[...] = st_new
            if add_ref is not None:
                y = y + add_ref[sl, :]
            y_ref[sl, :] = y
            return carry

        lax.fori_loop(0, nsub, step, 0)

    b0 = D_INNER // n
    yspec = pl.BlockSpec((blk, gw), lambda g, j: (pos(j), g))
    pspec = pl.BlockSpec((None, 1, SSM_HPG), lambda g, j: (g, 0, 0))
    in_specs = [yspec,
                pl.BlockSpec((blk, n), lambda g, j: (pos(j), b0 + g)),
                pl.BlockSpec((blk, n), lambda g, j: (pos(j), b0 + SSM_GROUPS + g)),
                pl.BlockSpec((None, blk, SSM_HPG), lambda g, j: (g, pos(j), 0)),
                pspec, pspec, pspec]
    args = [xbc, xbc, xbc, dtr, bias, alog, dsk]
    if add is not None:
        in_specs.append(yspec)
        args.append(add)
    return pl.pallas_call(
        body, name=name, grid=(SSM_GROUPS, nblk), in_specs=in_specs,
        out_specs=[yspec, pl.BlockSpec((None, None, n, gw), lambda g, j: (g, j, 0, 0))],
        out_shape=[jax.ShapeDtypeStruct((L, D_INNER), F32),
                   jax.ShapeDtypeStruct((SSM_GROUPS, nblk, n, gw), F32)],
        scratch_shapes=[pltpu.VMEM((n, gw), F32), pltpu.VMEM((blk, gw), F32), pltpu.VMEM((blk, gw), F32)],
        compiler_params=_cparams("parallel", "arbitrary"),
    )(*args)


def _ssd_bwd(xbc, dtr, bias, alog, dsk, dy, states, *, rev, skip, adds, name):
    L = xbc.shape[0]
    blk, nblk, gw, _ = _ssd_specs(L, rev)
    nsub = blk // SSD_CHUNK
    n = SSM_STATE
    n_add = 0 if adds is None else 3

    def body(*refs):
        x_ref, b_ref, c_ref, dt_ref, bias_ref, alog_ref, dsk_ref, dy_ref, st_in = refs[:9]
        add_refs = refs[9:9 + n_add]
        outs = refs[9 + n_add:]
        dx_ref, db_ref, dc_ref, ddt_ref, dbias_ref, dalog_ref, ddsk_ref = outs[:7]
        dst_scr, sub_scr, dt_scr, ac_scr, ddt_scr, dac_scr = outs[7:]
        consts = _ssd_consts(rev)
        chunk = functools.partial(_ssd_chunk, rev=rev, consts=consts, skip=skip)
        prep = functools.partial(_ssd_prep, rev=rev, consts=consts)

        @pl.when(pl.program_id(1) == 0)
        def _():
            dst_scr[...] = jnp.zeros(dst_scr.shape, F32)
            dbias_ref[...] = jnp.zeros(dbias_ref.shape, F32)
            dalog_ref[...] = jnp.zeros(dalog_ref.shape, F32)
            ddsk_ref[...] = jnp.zeros(ddsk_ref.shape, F32)

        narrow = (dt_ref[...], bias_ref[...], alog_ref[...], dsk_ref[...])
        (dt_scr[...], ac_scr[...], dsk_l), prep_vjp = jax.vjp(prep, *narrow)

        def rows(i):
            return pl.ds(pl.multiple_of((nsub - 1 - i if rev else i) * SSD_CHUNK, SSD_CHUNK), SSD_CHUNK)

        def operands(i):
            sl = rows(i)
            return (x_ref[sl, :], b_ref[sl, :], c_ref[sl, :], dt_scr[sl, :], ac_scr[sl, :], dsk_l)

        def replay(i, st):
            sub_scr[i] = st
            return chunk(st, *operands(i))[0]

        lax.fori_loop(0, nsub, replay, st_in[...])

        def step(k, ddsk_l):
            i = nsub - 1 - k
            sl = rows(i)
            _, vjp = jax.vjp(chunk, sub_scr[i], *operands(i))
            dst, dx, db, dc, ddt_l, dac, ddsk_k = vjp((dst_scr[...], dy_ref[sl, :]))
            dst_scr[...] = dst
            if n_add:
                dx = dx + add_refs[0][sl, :]
                db = db + add_refs[1][sl, :]
                dc = dc + add_refs[2][sl, :]
            dx_ref[sl, :] = dx
            db_ref[sl, :] = db
            dc_ref[sl, :] = dc
            ddt_scr[sl, :] = ddt_l
            dac_scr[sl, :] = dac
            return ddsk_l + ddsk_k

        ddsk_l = lax.fori_loop(0, nsub, step, jnp.zeros((1, gw), F32))
        ddt, dbias, dalog, ddsk = prep_vjp((ddt_scr[...], dac_scr[...], ddsk_l))
        ddt_ref[...] = ddt
        dbias_ref[...] += dbias
        dalog_ref[...] += dalog
        ddsk_ref[...] += ddsk

    def pos(j):
        p = nblk - 1 - j
        return nblk - 1 - p if rev else p

    b0 = D_INNER // n
    xspec = pl.BlockSpec((blk, gw), lambda g, j: (pos(j), g))
    nspec = pl.BlockSpec((blk, n), lambda g, j: (pos(j), g))
    dtspec = pl.BlockSpec((None, blk, SSM_HPG), lambda g, j: (g, pos(j), 0))
    pspec = pl.BlockSpec((None, 1, SSM_HPG), lambda g, j: (g, 0, 0))
    in_specs = [xspec,
                pl.BlockSpec((blk, n), lambda g, j: (pos(j), b0 + g)),
                pl.BlockSpec((blk, n), lambda g, j: (pos(j), b0 + SSM_GROUPS + g)),
                dtspec, pspec, pspec, pspec, xspec,
                pl.BlockSpec((None, None, n, gw), lambda g, j: (g, nblk - 1 - j, 0, 0))]
    args = [xbc, xbc, xbc, dtr, bias, alog, dsk, dy, states]
    if adds is not None:
        in_specs += [xspec, nspec, nspec]
        args += list(adds)
    par = jax.ShapeDtypeStruct((SSM_GROUPS, 1, SSM_HPG), F32)
    return pl.pallas_call(
        body, name=name, grid=(SSM_GROUPS, nblk), in_specs=in_specs,
        out_specs=[xspec, nspec, nspec, dtspec, pspec, pspec, pspec],
        out_shape=[jax.ShapeDtypeStruct((L, D_INNER), F32), jax.ShapeDtypeStruct((L, GN), F32),
                   jax.ShapeDtypeStruct((L, GN), F32), jax.ShapeDtypeStruct((SSM_GROUPS, L, SSM_HPG), F32),
                   par, par, par],
        scratch_shapes=[pltpu.VMEM((n, gw), F32), pltpu.VMEM((nsub, n, gw), F32)] + [pltpu.VMEM((blk, gw), F32)] * 4,
        compiler_params=_cparams("parallel", "arbitrary"),
    )(*args)


def _norm_fwd(h, w, name):
    d = h.shape[1]
    return _row_kernel(lambda hv, wv: (_rms(hv, wv),), [(h, d, 0)], [w], [(d, BF16)], [], name=name)[0]


def _norm_bwd(h, w, du, dh_in, name):
    d = h.shape[1]

    def fn(hv, duv, dhv, wv):
        _, vjp = jax.vjp(_rms, hv, wv)
        dh, dw = vjp(duv.astype(F32))
        dh = dh + dhv
        return dh, dh, dw

    return _row_kernel(fn, [(h, d, 0), (du, d, 0), (dh_in, d, 0)], [w], [(d, F32), (d, BF16)], [(1, d)], name=name)


def _ffn_fwd(h, norm_w, w_gate, w_val, w_out, conv_w, conv_b, tag):
    u = _norm_fwd(h, norm_w, f"{tag}_norm")
    pg = _matmul(u, w_gate, name=f"{tag}_in_gate")
    pv = _matmul(u, w_val, name=f"{tag}_in_val")
    y = _col_kernel(lambda g, v, w, b: (_ffn_mid_fn(g, v, w, b),), [(pg, 0), (pv, 0)], [(conv_w, 0), (conv_b, 0)],
                    [BF16], [], name=f"{tag}_mid", n_tiles=D_FF // COL_TILE)[0]
    h_out = _matmul(y, w_out, add=h, name=f"{tag}_out")
    return h_out, (u, pg, pv, y)


def _ffn_bwd(h, dh, dh16, saved, norm_w, w_gate, w_val, w_out, conv_w, conv_b, tag, after=None):
    u, pg, pv, y = saved
    dy = _matmul(dh16, w_out, nt=True, after=after, name=f"{tag}_out_dx")
    dw_out = _matmul(y, dh16, ta=True, name=f"{tag}_out_dw")

    def fn(g, v, ct, w, b):
        _, vjp = jax.vjp(_ffn_mid_fn, g, v, w, b)
        return vjp(ct)

    dpg, dpv, dcw, dcb = _col_kernel(fn, [(pg, 0), (pv, 0), (dy, 0)], [(conv_w, 0), (conv_b, 0)], [BF16, BF16],
                                     [FFN_CONV, 1], name=f"{tag}_mid_bwd", n_tiles=D_FF // COL_TILE)
    du = _matmul(dpg, w_gate, nt=True, name=f"{tag}_gate_dx")
    du = _matmul(dpv, w_val, nt=True, add=du, out_dtype=BF16, name=f"{tag}_val_dx")
    dw_gate = _matmul(u, dpg, ta=True, name=f"{tag}_gate_dw")
    dw_val = _matmul(u, dpv, ta=True, name=f"{tag}_val_dw")
    dh, dh16, dnw = _norm_bwd(h, norm_w, du, dh, f"{tag}_norm_bwd")
    return dh, dh16, dict(w_in=(dw_gate, dw_val), w_out=dw_out, conv_w=dcw, conv_b=dcb, norm=dnw)


def _hgrn2_fwd(h, norm_w, w_in, lb_logits, a_norm_w, w_out, after=None):
    d = D_MODEL
    u = _norm_fwd(h, norm_w, "hg_norm")
    pa = _matmul(u, w_in, after=after, name="hg_in")
    qs, lf_fw, k_fw, lf_bw, k_bw = _row_kernel(
        _hgrn2_pre_fn, [(pa, d, 0), (pa, d, 1), (pa, d, 2)], [lb_logits], [(d, F32)] * 5, [], name="hg_pre")
    o_fw, st_fw = _gla_fwd(qs, k_fw, pa, 3 * HG_HEADS, lf_fw, rev=False, add=None, name="hg_gla_fw")
    o, st_bw = _gla_fwd(qs, k_bw, pa, 3 * HG_HEADS, lf_bw, rev=True, add=o_fw, name="hg_gla_bw")
    y = _row_kernel(lambda ov, gv, wv: (_hgrn2_post_fn(ov, gv, wv),), [(o, d, 0), (pa, d, 4)], [a_norm_w],
                    [(d, BF16)], [], name="hg_post")[0]
    h_out = _matmul(y, w_out, add=h, name="hg_out")
    return h_out, (u, pa, qs, lf_fw, k_fw, lf_bw, k_bw, st_fw, st_bw, o, y)


def _hgrn2_bwd(h, dh, dh16, saved, norm_w, w_in, lb_logits, a_norm_w, w_out, send, after=None):
    d = D_MODEL
    u, pa, qs, lf_fw, k_fw, lf_bw, k_bw, st_fw, st_bw, o, y = saved
    dy = _matmul(dh16, w_out, nt=True, after=after, name="hg_out_dx")
    dw_out = _matmul(y, dh16, ta=True, name="hg_out_dw")

    def post_bwd(ov, gv, ct, wv):
        _, vjp = jax.vjp(_hgrn2_post_fn, ov, gv, wv)
        return vjp(ct)

    do, dg, dnw = _row_kernel(post_bwd, [(o, d, 0), (pa, d, 4), (dy, d, 0)], [a_norm_w], [(d, F32), (d, F32)],
                              [(1, HG_HEAD_DIM)], name="hg_post_bwd")
    dq1, dk_fw, dv1, dlf_fw = _gla_bwd(qs, k_fw, pa, 3 * HG_HEADS, lf_fw, do, st_fw, rev=False, adds=None,
                                       name="hg_gla_fw_bwd")
    dqs, dk_bw, dv, dlf_bw = _gla_bwd(qs, k_bw, pa, 3 * HG_HEADS, lf_bw, do, st_bw, rev=True, adds=(dq1, dv1),
                                      name="hg_gla_bw_bwd")

    def pre_bwd(qr, fr, br, c0, c1, c2, c3, c4, dvv, dgv, lbl):
        _, vjp = jax.vjp(_hgrn2_pre_fn, qr, fr, br, lbl)
        dq, df, db, dlbl = vjp((c0, c1, c2, c3, c4))
        return jnp.concatenate([dq, df, db, dvv, dgv], axis=1), dlbl

    rows = [(pa, d, 0), (pa, d, 1), (pa, d, 2), (dqs, d, 0), (dlf_fw, d, 0), (dk_fw, d, 0), (dlf_bw, d, 0),
            (dk_bw, d, 0), (dv, d, 0), (dg, d, 0)]
    dpa, dlbl = _row_kernel(pre_bwd, rows, [lb_logits], [(5 * d, BF16)], [lb_logits.shape], name="hg_pre_bwd")
    dw_in = _matmul(u, dpa, ta=True, name="hg_in_dw")
    token = send(dw_in, dw_out)
    du = _matmul(dpa, w_in, nt=True, out_dtype=BF16, after=token, name="hg_in_dx")
    dh, dh16, dn1 = _norm_bwd(h, norm_w, du, dh, "hg_norm_bwd")
    return dh, dh16, dict(lb=dlbl, a_norm=dnw, norm=dn1)


def _group_params(p):
    return p.reshape(SSM_GROUPS, 1, SSM_HPG)


def _mamba_fwd(h, norm_w, w_z, w_xbc, w_dt, conv_w, conv_b, dt_bias, a_log, d_skip, b_norm_w, w_out):
    L = h.shape[0]
    u = _norm_fwd(h, norm_w, "mb_norm")
    z = _matmul(u, w_z, name="mb_in_z")
    xbc_raw = _matmul(u, w_xbc, name="mb_in_xbc")
    dt_raw = _matmul(u, w_dt, name="mb_in_dt")
    xbc = _col_kernel(lambda xv, w, b: (_mamba_conv_fn(xv, w, b),), [(xbc_raw, 0)], [(conv_w, 0), (conv_b, 0)],
                      [F32], [], name="mb_conv", n_tiles=CONV_DIM // COL_TILE)[0]
    dtr = dt_raw.reshape(L, 2, SSM_GROUPS, SSM_HPG).transpose(1, 2, 0, 3)
    bias, alog = dt_bias.reshape(2, -1), a_log.reshape(2, -1)
    dsk = _group_params(d_skip.reshape(-1))
    y_fw, st_fw = _ssd_fwd(xbc, dtr[0], _group_params(bias[0]), _group_params(alog[0]), dsk, rev=False, skip=True,
                           add=None, name="mb_ssd_fw")
    ysum, st_bw = _ssd_fwd(xbc, dtr[1], _group_params(bias[1]), _group_params(alog[1]), dsk, rev=True, skip=False,
                           add=y_fw, name="mb_ssd_bw")
    y = _row_kernel(lambda yv, zv, wv: (_mamba_post_fn(yv, zv, wv),), [(ysum, D_INNER, 0), (z, D_INNER, 0)],
                    [b_norm_w], [(D_INNER, BF16)], [], name="mb_post")[0]
    h_out = _matmul(y, w_out, add=h, name="mb_out")
    return h_out, (u, z, xbc_raw, xbc, dtr, st_fw, st_bw, ysum, y)


def _mamba_bwd(h, dh, dh16, saved, norm_w, w_z, w_xbc, w_dt, conv_w, conv_b, dt_bias, a_log, d_skip, b_norm_w, w_out,
               after=None):
    L = h.shape[0]
    u, z, xbc_raw, xbc, dtr, st_fw, st_bw, ysum, y = saved
    dy = _matmul(dh16, w_out, nt=True, after=after, name="mb_out_dx")
    dw_out = _matmul(y, dh16, ta=True, name="mb_out_dw")

    def post_bwd(yv, zv, ct, wv):
        _, vjp = jax.vjp(_mamba_post_fn, yv, zv, wv)
        return vjp(ct)

    dys, dz, dbn = _row_kernel(post_bwd, [(ysum, D_INNER, 0), (z, D_INNER, 0), (dy, D_INNER, 0)], [b_norm_w],
                               [(D_INNER, F32), (D_INNER, BF16)], [(1, D_INNER)], name="mb_post_bwd")
    bias, alog = dt_bias.reshape(2, -1), a_log.reshape(2, -1)
    dsk = _group_params(d_skip.reshape(-1))
    dx1, db1, dc1, ddt_fw, dbias_fw, dalog_fw, ddsk = _ssd_bwd(
        xbc, dtr[0], _group_params(bias[0]), _group_params(alog[0]), dsk, dys, st_fw, rev=False, skip=True,
        adds=None, name="mb_ssd_fw_bwd")
    dx, db, dc, ddt_bw, dbias_bw, dalog_bw, _ = _ssd_bwd(
        xbc, dtr[1], _group_params(bias[1]), _group_params(alog[1]), dsk, dys, st_bw, rev=True, skip=False,
        adds=(dx1, db1, dc1), name="mb_ssd_bw_bwd")

    def conv_bwd(n_tiles, ct, first):
        def fn(xv, ctv, w, b):
            _, vjp = jax.vjp(_mamba_conv_fn, xv, w, b)
            return vjp(ctv)
        return _col_kernel(fn, [(xbc_raw, first), (ct, 0)], [(conv_w, first), (conv_b, first)], [BF16],
                           [SSM_CONV, 1], name=f"mb_conv_bwd_{first}", n_tiles=n_tiles)

    nx, nb = D_INNER // COL_TILE, GN // COL_TILE
    parts = [conv_bwd(nx, dx, 0), conv_bwd(nb, db, nx), conv_bwd(nb, dc, nx + nb)]
    dxbc = jnp.concatenate([p[0] for p in parts], axis=1)
    dcw = jnp.concatenate([p[1] for p in parts], axis=1)
    dcb = jnp.concatenate([p[2] for p in parts], axis=1)
    ddt = jnp.stack([ddt_fw, ddt_bw]).transpose(2, 0, 1, 3).reshape(L, 2 * SSM_GROUPS * SSM_HPG).astype(BF16)
    du = _matmul(dz, w_z, nt=True, name="mb_z_dx")
    du = _matmul(dxbc, w_xbc, nt=True, add=du, name="mb_xbc_dx")
    du = _matmul(ddt, w_dt, nt=True, add=du, out_dtype=BF16, name="mb_dt_dx")
    dw_in = jnp.concatenate([_matmul(u, dz, ta=True, name="mb_z_dw"), _matmul(u, dxbc, ta=True, name="mb_xbc_dw"),
                             _matmul(u, ddt, ta=True, name="mb_dt_dw")], axis=1)
    dh, dh16, dn1 = _norm_bwd(h, norm_w, du, dh, "mb_norm_bwd")
    grads = dict(w_in=dw_in, w_out=dw_out, conv_w=dcw, conv_b=dcb, b_norm=dbn, norm=dn1,
                 dt_bias=jnp.stack([dbias_fw, dbias_bw]).reshape(1, 2, -1),
                 a_log=jnp.stack([dalog_fw, dalog_bw]).reshape(1, 2, -1), d_skip=ddsk.reshape(1, -1))
    return dh, dh16, grads


def _loss_head(h, target, w):
    d = h.shape[1]

    def fn(hv, tv, wv):
        def loss(hv, wv):
            err = _rms(hv, wv) - tv
            return 0.5 * jnp.sum(jnp.mean(err * err, axis=-1, keepdims=True), axis=0, keepdims=True)
        val, vjp = jax.vjp(loss, hv, wv)
        dh, dw = vjp(jnp.ones((1, 1), F32))
        return dh, dh, val, dw

    return _row_kernel(fn, [(h, d, 0), (target, d, 0)], [w], [(d, F32), (d, BF16)], [(1, 1), (1, d)], name="loss_head")


def _local_step(x, target, rep, comm):
    def ffn_args(i, w, small):
        return (rep["norm2_w"][i:i + 1], w["w_in"][:, :D_FF], w["w_in"][:, D_FF:], w["w_out"],
                small["ffn_conv_w"][i], rep["ffn_conv_b"][i:i + 1])

    def mamba_args(w, small):
        w_in = w["w_in"]
        return (rep["norm1_w"][1:2], w_in[:, :D_INNER], w_in[:, D_INNER:D_INNER + CONV_DIM],
                w_in[:, D_INNER + CONV_DIM:], small["b_conv_w"][0], small["b_conv_b"], rep["b_dt_bias"],
                rep["b_a_log"], rep["b_d_skip"], small["b_norm_w"], w["w_out"])

    w_hg = comm.weights("hg", None)
    hg = (rep["norm1_w"][0:1], w_hg["w_in"], rep["a_lb_logits"], rep["a_norm_w"], w_hg["w_out"])
    h0 = x
    h1, s_hg = _hgrn2_fwd(h0, *hg, after=w_hg.get("token"))
    w_f0 = comm.weights("f0", h1)
    small = w_f0["small"]
    f0 = ffn_args(0, w_f0, small)
    h2, s_f0 = _ffn_fwd(h1, *f0, "ffn0")
    mb = mamba_args(comm.weights("mb", h2), small)
    h3, s_mb = _mamba_fwd(h2, *mb)
    f1 = ffn_args(1, comm.weights("f1", h3), small)
    h4, s_f1 = _ffn_fwd(h3, *f1, "ffn1")
    dh, dh16, loss, d_final = _loss_head(h4, target, rep["final_norm_w"].reshape(1, -1))

    dh, dh16, g_f1 = _ffn_bwd(h3, dh, dh16, s_f1, *f1, "ffn1")
    token = comm.send("f1", dict(w_in=g_f1["w_in"], w_out=g_f1["w_out"]))
    dh, dh16, g_mb = _mamba_bwd(h2, dh, dh16, s_mb, *mb, after=token)
    token = comm.send("mb", dict(w_in=g_mb["w_in"], w_out=g_mb["w_out"]))
    dh, dh16, g_f0 = _ffn_bwd(h1, dh, dh16, s_f0, *f0, "ffn0", after=token)
    token = comm.send("f0", dict(w_in=g_f0["w_in"], w_out=g_f0["w_out"]))
    small_grads = dict(b_conv_w=g_mb["conv_w"][None], b_conv_b=g_mb["conv_b"], b_norm_w=g_mb["b_norm"],
                       ffn_conv_w=jnp.stack([g_f0["conv_w"], g_f1["conv_w"]]))
    dh, dh16, g_hg = _hgrn2_bwd(
        h0, dh, dh16, s_hg, *hg, after=token,
        send=lambda dw_in, dw_out: comm.send("hg", dict(w_in=dw_in, w_out=dw_out, small=small_grads)))
    grads = dict(
        norm1_w=jnp.concatenate([g_hg["norm"], g_mb["norm"]], axis=0),
        norm2_w=jnp.concatenate([g_f0["norm"], g_f1["norm"]], axis=0),
        a_lb_logits=g_hg["lb"], a_norm_w=g_hg["a_norm"], b_dt_bias=g_mb["dt_bias"], b_a_log=g_mb["a_log"],
        b_d_skip=g_mb["d_skip"], ffn_conv_b=jnp.concatenate([g_f0["conv_b"], g_f1["conv_b"]], axis=0),
        final_norm_w=d_final.reshape(-1),
    )
    return loss, dh, grads


def _mesh_pos():
    return lax.axis_index("x"), lax.axis_index("y"), lax.axis_index("c")


N_COPIES = N_DEV - 1


def _comm_call(body, ins, out_shape, name):
    n = len(ins)
    hbm = pl.BlockSpec(memory_space=pl.ANY)
    return pl.pallas_call(
        body, name=name, out_shape=out_shape, in_specs=[hbm] * n, out_specs=[hbm] * n,
        scratch_shapes=[pltpu.SemaphoreType.DMA((n * N_COPIES,)), pltpu.SemaphoreType.DMA((n * N_COPIES,)),
                        pltpu.SemaphoreType.DMA((n,))],
    )(*ins)


def _all_gather(shards, name):
    n = len(shards)

    def body(*refs):
        x_refs, out_refs = refs[:n], refs[n:2 * n]
        send_sems, recv_sems, local_sems = refs[2 * n:]
        x, y, c = _mesh_pos()
        me, sibling = (x, y, c), (x, y, 1 - c)
        chips = [(1 - x, y), (x, 1 - y), (1 - x, 1 - y)]

        def copy(w, k, block, to, own=False):
            px, py, pc = block
            dst = out_refs[w].at[4 * px + 2 * py + pc]
            return pltpu.make_async_remote_copy(
                src_ref=x_refs[w] if own else dst, dst_ref=dst, send_sem=send_sems.at[w * N_COPIES + k],
                recv_sem=recv_sems.at[w * N_COPIES + k], device_id=to, device_id_type=MESH)

        mine = [pltpu.make_async_copy(x_refs[w], out_refs[w].at[4 * x + 2 * y + c], local_sems.at[w]) for w in range(n)]
        for cp in mine:
            cp.start()
        first = [copy(w, 1 + j, me, (*chip, c), own=True) for j, chip in enumerate(chips) for w in range(n)]
        first += [copy(w, 0, me, sibling, own=True) for w in range(n)]
        for cp in first:
            cp.start()
        passed = []
        for j, chip in enumerate(chips):
            for w in range(n):
                copy(w, 1 + j, (*chip, c), me).wait_recv()
                passed.append(copy(w, 4 + j, (*chip, c), sibling))
                passed[-1].start()
        for w in range(n):
            copy(w, 0, sibling, me).wait_recv()
        for j, chip in enumerate(chips):
            for w in range(n):
                copy(w, 4 + j, (*chip, 1 - c), me).wait_recv()
        for cp in first + passed:
            cp.wait_send()
        for cp in mine:
            cp.wait()

    out_shape = [jax.ShapeDtypeStruct((N_DEV, *s.shape), s.dtype) for s in shards]
    return _comm_call(body, shards, out_shape, name)


HBM_SPEC = pl.BlockSpec(memory_space=pltpu.HBM)
SEM_SPEC = pl.BlockSpec(memory_space=pltpu.SEMAPHORE)
EFFECT = pltpu.SideEffectType.DATAFLOW_SIDE_EFFECTING
PEER_ORDER = (4, 2, 6, 5, 3, 7, 1)


def _peer_copies(src_refs, land_refs, send_sems, recv_sems, gather):
    x, y, c = _mesh_pos()
    me = 4 * x + 2 * y + c
    copies = []
    for k in PEER_ORDER:
        px = 1 - x if k & 4 else x
        py = 1 - y if k & 2 else y
        pc = 1 - c if k & 1 else c
        for w, (src, land) in enumerate(zip(src_refs, land_refs)):
            copies.append(pltpu.make_async_remote_copy(
                src_ref=src if gather else src.at[4 * px + 2 * py + pc],
                dst_ref=land.at[me] if gather else land.at[k - 1],
                send_sem=send_sems.at[w * N_COPIES + k - 1], recv_sem=recv_sems.at[w * N_COPIES + k - 1],
                device_id=(px, py, pc), device_id_type=MESH))
    return copies


def _copies_start(srcs, *, gather, after, name):
    n = len(srcs)
    lands = [lax.empty(((N_DEV,) + s.shape) if gather else ((N_COPIES,) + s.shape[1:]), s.dtype) for s in srcs]

    def body(*refs):
        src_refs, land_refs = refs[:n], refs[n:2 * n]
        send_sems, recv_sems = refs[-2 * n - 3], refs[-2 * n - 2]
        for cp in _peer_copies(src_refs, land_refs, send_sems, recv_sems, gather):
            cp.start()
        refs[-1][...] = jnp.zeros(refs[-1].shape, F32)

    ins = [pltpu.with_memory_space_constraint(a, pltpu.HBM) for a in srcs + lands]
    in_specs = [HBM_SPEC] * (2 * n)
    if after is not None:
        ins.append(after)
        in_specs.append(pl.BlockSpec(memory_space=pl.ANY))
    sems = pltpu.SemaphoreType.DMA((n * N_COPIES,))
    out = pl.pallas_call(
        body, name=name,
        out_shape=(sems, sems, *[pltpu.HBM(a.shape, a.dtype) for a in srcs + lands], jax.ShapeDtypeStruct((8, LANES), F32)),
        in_specs=in_specs,
        out_specs=(SEM_SPEC, SEM_SPEC, *[HBM_SPEC] * (2 * n), pl.BlockSpec(memory_space=pltpu.VMEM)),
        input_output_aliases={i: 2 + i for i in range(2 * n)},
        compiler_params=pltpu.CompilerParams(has_side_effects=EFFECT),
    )(*ins)
    return out[0], out[1], list(out[2:2 + n]), list(out[2 + n:2 + 2 * n]), out[-1]


def _copies_wait(started, *, gather, after, name):
    send_sems, recv_sems, srcs, lands, _ = started
    n = len(srcs)

    def body(*refs):
        src_refs, land_refs = refs[:n], refs[n:2 * n]
        for cp in _peer_copies(src_refs, land_refs, refs[2 * n], refs[2 * n + 1], gather):
            cp.wait_send()
            cp.wait_recv()

    out = pl.pallas_call(
        body, name=name, out_shape=tuple(pltpu.HBM(a.shape, a.dtype) for a in srcs + lands),
        in_specs=[HBM_SPEC] * (2 * n) + [SEM_SPEC, SEM_SPEC, pl.BlockSpec(memory_space=pl.ANY)],
        out_specs=tuple([HBM_SPEC] * (2 * n)), input_output_aliases={i: i for i in range(2 * n)},
        compiler_params=pltpu.CompilerParams(has_side_effects=EFFECT),
    )(*srcs, *lands, send_sems, recv_sems, after)
    return list(out[n:])


def _adamw(got, w, m, v, *, name, tile, own=None):
    rows, width = w.shape
    n_got = got.shape[0]
    c1 = 1.0 / (1.0 - ADAM_B1 ** ADAM_STEP)
    c2 = 1.0 / (1.0 - ADAM_B2 ** ADAM_STEP)

    def body(*refs):
        got_ref, w_ref, m_ref, v_ref = refs[:4]
        g_ref, d_ref, nm_ref, nv_ref = refs[-4:]
        g = got_ref[0] if own is None else refs[4][...] + got_ref[0]
        for s in range(1, n_got):
            g = g + got_ref[s]
        m_new = ADAM_B1 * m_ref[...] + (1.0 - ADAM_B1) * g
        v_new = ADAM_B2 * v_ref[...] + (1.0 - ADAM_B2) * (g * g)
        g_ref[...] = g
        nm_ref[...] = m_new
        nv_ref[...] = v_new
        d_ref[...] = -ADAM_LR * ((m_new * c1) / (jnp.sqrt(v_new * c2) + ADAM_EPS) + ADAM_WD * w_ref[...])

    spec = pl.BlockSpec((tile, width), lambda i: (i, 0))
    out = jax.ShapeDtypeStruct((rows, width), F32)
    return pl.pallas_call(
        body, name=name, grid=(rows // tile,),
        in_specs=[pl.BlockSpec((n_got, tile, width), lambda i: (0, i, 0))] + [spec] * (3 if own is None else 4),
        out_specs=[spec] * 4, out_shape=[out] * 4, compiler_params=_cparams("parallel"),
    )(*([got, w, m, v] + ([] if own is None else [own])))


SHARDED = dict(a_w_in=(2, True), a_w_out=(1, True), b_w_in=(2, True), b_w_out=(1, True), ffn_w_in=(2, True),
               ffn_w_out=(1, True), b_conv_w=(2, False), b_conv_b=(1, False), b_norm_w=(1, False), ffn_conv_w=(2, False))
REPLICATED = ("norm1_w", "norm2_w", "a_lb_logits", "a_norm_w", "b_dt_bias", "b_a_log", "b_d_skip", "ffn_conv_b",
              "final_norm_w")
WEIGHTS = ("norm1_w", "norm2_w", "a_w_in", "a_lb_logits", "a_norm_w", "a_w_out", "b_w_in", "b_conv_w", "b_conv_b",
           "b_dt_bias", "b_a_log", "b_d_skip", "b_norm_w", "b_w_out", "ffn_w_in", "ffn_conv_w", "ffn_conv_b",
           "ffn_w_out", "final_norm_w")


def _pad_rows(flat, multiple):
    n = flat.shape[-1]
    per = PACK_W * multiple
    total = -(-n // per) * per
    flat = jnp.pad(flat, [(0, 0)] * (flat.ndim - 1) + [(0, total - n)])
    return flat.reshape(*flat.shape[:-1], total // PACK_W, PACK_W)


def _to_parts(full, axis, n=N_DEV):
    shp = full.shape
    t = full.reshape(*shp[:axis], n, shp[axis] // n, *shp[axis + 1:])
    return jnp.moveaxis(t, axis, 0)


def _from_parts(parts, axis):
    t = jnp.moveaxis(parts, 0, axis)
    shp = t.shape
    return t.reshape(*shp[:axis], shp[axis] * shp[axis + 1], *shp[axis + 2:])


BIG = tuple(n for n, (_, mm) in SHARDED.items() if mm)
SMALL = tuple(n for n, (_, mm) in SHARDED.items() if not mm)
SMALL_W = 512


def _small_rows(tree, lead):
    rows = []
    for n in SMALL:
        t = tree[n]
        t = t.reshape(*lead, -1, t.shape[-1])
        rows.append(jnp.pad(t, [(0, 0)] * (t.ndim - 1) + [(0, SMALL_W - t.shape[-1])]))
    buf = jnp.concatenate(rows, axis=-2)
    return jnp.pad(buf, [(0, 0)] * (buf.ndim - 2) + [(0, 16 - buf.shape[-2]), (0, 0)])


def _small_unrows(buf, like, lead):
    out, r = {}, 0
    for n in SMALL:
        shp = like[n].shape
        k = like[n].size // shp[-1]
        out[n] = buf[..., r:r + k, :shp[-1]].reshape(*lead, *shp)
        r += k
    return out


GROUPS = dict(hg=(("a_w_in", 0), ("a_w_out", 0)), f0=(("ffn_w_in", 0), ("ffn_w_out", 0)),
              mb=(("b_w_in", 0), ("b_w_out", 0)), f1=(("ffn_w_in", 1), ("ffn_w_out", 1)))


class _Comm:
    def __init__(self, local):
        x, y, c = _mesh_pos()
        self.me = 4 * x + 2 * y + c
        self.local = local
        self.shards = {g: [local[n][i].astype(BF16) for n, i in names] for g, names in GROUPS.items()}
        self.shards["f0"].append(_small_rows(local, ()))
        self.first = _all_gather(self.shards["hg"], "gather_hg")
        self.gathers, self.sent, token = {}, {}, None
        for g in ("f0", "mb", "f1"):
            self.gathers[g] = _copies_start(self.shards[g], gather=True, after=token, name=f"gather_{g}_start")
            token = self.gathers[g][-1]
        self.token = token

    def weights(self, group, after):
        if group == "hg":
            got = self.first
        else:
            lands = _copies_wait(self.gathers[group], gather=True, after=after, name=f"gather_{group}_wait")
            got = [lax.dynamic_update_index_in_dim(land, shard, self.me, 0)
                   for land, shard in zip(lands, self.shards[group])]
        out = dict(w_in=_from_parts(got[0], 1), w_out=_from_parts(got[1], 0))
        if group == "hg":
            out["token"] = self.token
        if group == "f0":
            small = _small_unrows(got[2], self.local, (N_DEV,))
            out["small"] = {n: _from_parts(small[n], SHARDED[n][0]) for n in SMALL}
        return out

    def send(self, group, grads):
        w_in = grads["w_in"]
        if isinstance(w_in, tuple):
            half = N_DEV // len(w_in)
            parts_in = jnp.concatenate([_to_parts(t, 1, half) for t in w_in], axis=0)
        else:
            parts_in = _to_parts(w_in, 1)
        parts = [parts_in, _to_parts(grads["w_out"], 0)]
        if "small" in grads:
            parts.append(_small_rows({n: _to_parts(grads["small"][n], SHARDED[n][0]) for n in SMALL}, (N_DEV,)))
        sent = [p.astype(BF16) for p in parts[:2]] + parts[2:] if group == "hg" else parts
        self.sent[group] = (parts, _copies_start(sent, gather=False, after=None, name=f"exchange_{group}_start"))
        return self.sent[group][1][-1]

    def finish(self, after, mom, var):
        res = {n: {} for n in BIG}
        for group in ("f1", "mb", "f0", "hg"):
            parts, started = self.sent[group]
            lands = _copies_wait(started, gather=False, after=after, name=f"exchange_{group}_wait")
            own = [lax.dynamic_index_in_dim(p, self.me, 0, keepdims=False) for p in parts]
            for (n, i), got, mine in zip(GROUPS[group], lands, own):
                out = _adamw(got, self.local[n][i], mom[n][i], var[n][i], own=mine, name=f"adamw_{n}_{i}",
                             tile=_pick(mine.shape[0], (256, 176, 128)))
                res[n][i] = out
                after = out[0]
        small = _adamw(lands[2], *[_small_rows(t, ()) for t in (self.local, mom, var)], own=own[2],
                       name="adamw_small", tile=16)
        return res, small


def _pack_small(tree, extra):
    flat = jnp.concatenate([tree[n].reshape(-1) for n in REPLICATED] + [extra.reshape(-1)])
    return _pad_rows(flat, 8)


def _unpack_small(pack, like):
    flat, out, off = pack.reshape(-1), {}, 0
    for n in REPLICATED:
        out[n] = flat[off:off + like[n].size].reshape(like[n].shape)
        off += like[n].size
    return out, flat[off]


def kernel(x, norm1_w, norm2_w, a_w_in, a_lb_logits, a_norm_w, a_w_out, b_w_in, b_conv_w, b_conv_b, b_dt_bias, b_a_log, b_d_skip, b_norm_w, b_w_out, ffn_w_in, ffn_conv_w, ffn_conv_b, ffn_w_out, final_norm_w, loss_target, m_norm1_w, m_norm2_w, m_a_w_in, m_a_lb_logits, m_a_norm_w, m_a_w_out, m_b_w_in, m_b_conv_w, m_b_conv_b, m_b_dt_bias, m_b_a_log, m_b_d_skip, m_b_norm_w, m_b_w_out, m_ffn_w_in, m_ffn_conv_w, m_ffn_conv_b, m_ffn_w_out, m_final_norm_w, v_norm1_w, v_norm2_w, v_a_w_in, v_a_lb_logits, v_a_norm_w, v_a_w_out, v_b_w_in, v_b_conv_w, v_b_conv_b, v_b_dt_bias, v_b_a_log, v_b_d_skip, v_b_norm_w, v_b_w_out, v_ffn_w_in, v_ffn_conv_w, v_ffn_conv_b, v_ffn_w_out, v_final_norm_w):
    given = dict(locals())
    local = {n: given[n] for n in WEIGHTS}
    mom = {n: given["m_" + n] for n in WEIGHTS}
    var = {n: given["v_" + n] for n in WEIGHTS}

    comm = _Comm(local)
    loss, grad_x, grads = _local_step(x[0], loss_target[0], local, comm)

    res, small_res = comm.finish(grad_x, mom, var)
    outs = ({}, {}, {}, {})
    for n in BIG:
        for out, layers in zip(outs, zip(*[res[n][i] for i in sorted(res[n])])):
            out[n] = jnp.stack(layers)
    for out, r in zip(outs, small_res):
        out.update(_small_unrows(r, local, ()))
    out_g, out_d, out_m, out_v = outs

    small = _pack_small(grads, loss)
    rows = small.shape[0]
    got_s = _all_gather([small], "gather_small")[0]
    zero = jnp.zeros((1,), F32)
    g, dlt, nm, nv = _adamw(got_s, _pack_small(local, zero), _pack_small(mom, zero), _pack_small(var, zero),
                            name="adamw_replicated", tile=rows)
    (rep_g, total), (rep_d, _), (rep_m, _), (rep_v, _) = (_unpack_small(t, local) for t in (g, dlt, nm, nv))
    out_g.update(rep_g)
    out_d.update(rep_d)
    out_m.update(rep_m)
    out_v.update(rep_v)

    return (total, grad_x[None], *[out_g[n] for n in WEIGHTS], *[out_d[n] for n in WEIGHTS],
            *[out_m[n] for n in WEIGHTS], *[out_v[n] for n in WEIGHTS])
```

```python
import functools

import jax
import jax.numpy as jnp
from jax import lax
from jax.experimental import pallas as pl
from jax.experimental.pallas import tpu as pltpu

F32 = jnp.float32
BF16 = jnp.bfloat16
HIGHEST = lax.Precision.HIGHEST
MESH = pl.DeviceIdType.MESH

N_DEV = 8
EPS = 1e-6
D_MODEL = 1024
HG_HEADS = 8
HG_HEAD_DIM = 128
D_INNER = 2048
SSM_HEAD_DIM = 64
SSM_GROUPS = 8
SSM_HPG = 4
SSM_STATE = 128
GN = SSM_GROUPS * SSM_STATE
CONV_DIM = D_INNER + 2 * GN
D_FF = 2816
SSM_CONV = 5
FFN_CONV = 3

ADAM_LR = 0.001
ADAM_B1 = 0.9
ADAM_B2 = 0.999
ADAM_EPS = 1e-08
ADAM_WD = 0.01
ADAM_STEP = 10

LANES = 128
ROW_TILE = 256
COL_TILE = 128
GLA_CHUNK = 16
GLA_BLOCK = 256
GLA_HEADS_PER_STEP = 8
SSD_CHUNK = 128
SSD_BLOCK = 512
PACK_W = 1024
VMEM_LIMIT = 56 * 1024 * 1024

NT_DIMS = (((1,), (1,)), ((), ()))
TN_DIMS = (((0,), (0,)), ((), ()))


def _cparams(*sem):
    return pltpu.CompilerParams(dimension_semantics=sem, vmem_limit_bytes=VMEM_LIMIT)


def _rms(x, w):
    return x * lax.rsqrt(jnp.mean(x * x, axis=-1, keepdims=True) + EPS) * w


def _row_kernel(body_fn, rows, params, row_outs, acc_outs, *, name, tile=ROW_TILE):
    L = rows[0][0].shape[0]
    tile = min(tile, L)
    n_in = len(rows) + len(params)
    n_ro = len(row_outs)

    def body(*refs):
        outs = body_fn(*[r[...] for r in refs[:n_in]])
        for ref, o in zip(refs[n_in:n_in + n_ro], outs[:n_ro]):
            ref[...] = o.astype(ref.dtype)
        first = pl.program_id(0) == 0
        for ref, o in zip(refs[n_in + n_ro:], outs[n_ro:]):
            @pl.when(first)
            def _(ref=ref):
                ref[...] = jnp.zeros(ref.shape, ref.dtype)
            ref[...] += o

    in_specs = [pl.BlockSpec((tile, w), lambda i, cb=cb: (i, cb)) for _, w, cb in rows]
    in_specs += [pl.BlockSpec(p.shape, lambda i: (0, 0)) for p in params]
    out_specs = [pl.BlockSpec((tile, w), lambda i: (i, 0)) for w, _ in row_outs]
    out_specs += [pl.BlockSpec(s, lambda i: (0, 0)) for s in acc_outs]
    out_shape = [jax.ShapeDtypeStruct((L, w), dt) for w, dt in row_outs]
    out_shape += [jax.ShapeDtypeStruct(s, F32) for s in acc_outs]
    return pl.pallas_call(
        body, name=name, grid=(L // tile,), in_specs=in_specs, out_specs=out_specs, out_shape=out_shape,
        compiler_params=_cparams("arbitrary" if acc_outs else "parallel"),
    )(*[a for a, _, _ in rows], *params)


def _col_kernel(body_fn, cols, params, col_outs, par_outs, *, name, n_tiles):
    L = cols[0][0].shape[0]
    n_in = len(cols) + len(params)
    width = n_tiles * COL_TILE

    def body(*refs):
        outs = body_fn(*[r[...] for r in refs[:n_in]])
        for ref, o in zip(refs[n_in:], outs):
            ref[...] = o.astype(ref.dtype)

    in_specs = [pl.BlockSpec((L, COL_TILE), lambda j, cb=cb: (0, cb + j)) for _, cb in cols]
    in_specs += [pl.BlockSpec((p.shape[0], COL_TILE), lambda j, cb=cb: (0, cb + j)) for p, cb in params]
    out_specs = [pl.BlockSpec((L, COL_TILE), lambda j: (0, j)) for _ in col_outs]
    out_specs += [pl.BlockSpec((k, COL_TILE), lambda j: (0, j)) for k in par_outs]
    out_shape = [jax.ShapeDtypeStruct((L, width), dt) for dt in col_outs]
    out_shape += [jax.ShapeDtypeStruct((k, width), F32) for k in par_outs]
    return pl.pallas_call(
        body, name=name, grid=(n_tiles,), in_specs=in_specs, out_specs=out_specs, out_shape=out_shape,
        compiler_params=_cparams("parallel"),
    )(*[a for a, _ in cols], *[p for p, _ in params])


def _pick(n, options):
    for t in options:
        if n % t == 0:
            return t
    return n


MATMUL_VMEM = 40 * 1024 * 1024


def _matmul_tiles(M, N, K, out_bytes):
    best = None
    for tm in (1024, 512, 256, 128, M):
        for tn in (1408, 1024, 512, 256, 128, N):
            if M % tm or N % tn:
                continue
            if 2 * (2 * K * (tm + tn) + out_bytes * tm * tn) > MATMUL_VMEM:
                continue
            if best is None or tm * tn > best[0] * best[1]:
                best = (tm, tn)
    return best


def _matmul(a, b, *, name, nt=False, ta=False, add=None, out_dtype=F32, after=None, b_cols=None):
    K, M = a.shape[::-1] if not ta else a.shape
    cb, width = b_cols if b_cols is not None else (0, b.shape[1])
    N = b.shape[0] if nt else width
    assert width == K or not nt
    tm, tn = _matmul_tiles(M, N, K, 4 * (1 + (add is not None)) if out_dtype == F32 else 2 + 4 * (add is not None))
    col0 = cb if nt else cb * (width // tn)

    def body(*refs):
        a_ref, b_ref = refs[0], refs[1]
        o_ref = refs[-1]
        dims = TN_DIMS if ta else NT_DIMS if nt else (((1,), (0,)), ((), ()))
        acc = lax.dot_general(a_ref[...], b_ref[...], dims, preferred_element_type=F32)
        if add is not None:
            acc = acc + refs[2][...]
        o_ref[...] = acc.astype(o_ref.dtype)

    in_specs = [pl.BlockSpec((K, tm), lambda i, j: (0, i)) if ta else pl.BlockSpec((tm, K), lambda i, j: (i, 0)),
                pl.BlockSpec((tn, K), lambda i, j: (j, col0)) if nt
                else pl.BlockSpec((K, tn), lambda i, j: (0, col0 + j))]
    args = [a, b]
    if add is not None:
        in_specs.append(pl.BlockSpec((tm, tn), lambda i, j: (i, j)))
        args.append(add)
    if after is not None:
        in_specs.append(pl.BlockSpec(memory_space=pl.ANY))
        args.append(after)
    return pl.pallas_call(
        body, name=name, grid=(M // tm, N // tn), in_specs=in_specs,
        out_specs=pl.BlockSpec((tm, tn), lambda i, j: (i, j)),
        out_shape=jax.ShapeDtypeStruct((M, N), out_dtype),
        compiler_params=_cparams("parallel", "parallel"),
    )(*args)


def _hgrn2_pre_fn(q_raw, ffw_raw, fbw_raw, lb_logits):
    lb = jax.nn.softmax(lb_logits, axis=0)[0:1]

    def gate(fr):
        f = lb + (1.0 - lb) * jax.nn.sigmoid(fr)
        return jnp.log(f), 1.0 - f

    lf_fw, k_fw = gate(ffw_raw)
    lf_bw, k_bw = gate(fbw_raw)
    return jax.nn.silu(q_raw), lf_fw, k_fw, lf_bw, k_bw


def _hgrn2_post_fn(o, g, norm_w):
    outs = []
    for h in range(HG_HEADS):
        sl = slice(h * HG_HEAD_DIM, (h + 1) * HG_HEAD_DIM)
        outs.append(_rms(o[:, sl], norm_w) * jax.nn.silu(g[:, sl]))
    return jnp.concatenate(outs, axis=1)


def _mamba_post_fn(y, z, norm_w):
    outs = []
    gw = D_INNER // SSM_GROUPS
    for gi in range(SSM_GROUPS):
        sl = slice(gi * gw, (gi + 1) * gw)
        outs.append(_rms(y[:, sl] * jax.nn.silu(z[:, sl]), norm_w[:, sl]))
    return jnp.concatenate(outs, axis=1)


def _shift_rows_impl(x, d):
    if d == 0:
        return x
    n, edge = x.shape[0], 8
    t = lax.broadcasted_iota(jnp.int32, (edge, x.shape[1]), 0)
    rolled = pltpu.roll(x, d % n, 0)
    if d > 0:
        return jnp.concatenate([jnp.where(t >= d, rolled[:edge], 0.0), rolled[edge:]], axis=0)
    return jnp.concatenate([rolled[:n - edge], jnp.where(t < edge + d, rolled[n - edge:], 0.0)], axis=0)


@functools.partial(jax.custom_vjp, nondiff_argnums=(1,))
def _shift_rows(x, d):
    return _shift_rows_impl(x, d)


_shift_rows.defvjp(lambda x, d: (_shift_rows_impl(x, d), None), lambda d, _, g: (_shift_rows_impl(g, -d),))


def _dwconv(x, w, b):
    taps = w.shape[0]
    c = (taps - 1) // 2
    y = b + w[0:1, :] * _shift_rows(x, c)
    for k in range(1, taps):
        y = y + w[k:k + 1, :] * _shift_rows(x, c - k)
    return y


def _ffn_mid_fn(gate, val, w, b):
    return jax.nn.silu(_dwconv(gate, w, b)) * val


def _mamba_conv_fn(xbc, w, b):
    return jax.nn.silu(_dwconv(xbc, w, b))


def _gla_consts(rev):
    return lax.broadcasted_iota(jnp.int32, (GLA_CHUNK, HG_HEAD_DIM), 0)


def _segment_cumsum(x, seg, rev):
    n = x.shape[0]
    t = lax.broadcasted_iota(jnp.int32, x.shape, 0) & (seg - 1)
    s = 1
    while s < seg:
        if rev:
            x = x + jnp.where(t < seg - s, pltpu.roll(x, n - s, 0), 0.0)
        else:
            x = x + jnp.where(t >= s, pltpu.roll(x, s, 0), 0.0)
        s *= 2
    return x


def _segment_cumsum_mxu(x, seg, rev):
    r = lax.broadcasted_iota(jnp.int32, (seg, seg), 0)
    c = lax.broadcasted_iota(jnp.int32, (seg, seg), 1)
    tri = ((r <= c) if rev else (r >= c)).astype(BF16)
    tri3 = jnp.concatenate([tri, tri, tri], axis=1)
    hi = x.astype(BF16)
    rest = x - hi.astype(F32)
    mid = rest.astype(BF16)
    lo = (rest - mid.astype(F32)).astype(BF16)
    outs = []
    for g in range(x.shape[0] // seg):
        rows = slice(g * seg, (g + 1) * seg)
        terms = jnp.concatenate([hi[rows], mid[rows], lo[rows]], axis=0)
        outs.append(jnp.dot(tri3, terms, preferred_element_type=F32))
    return jnp.concatenate(outs, axis=0)


@functools.partial(jax.custom_vjp, nondiff_argnums=(1, 2))
def _segment_cumsum_diff(x, seg, rev):
    return _segment_cumsum_mxu(x, seg, rev)


_segment_cumsum_diff.defvjp(lambda x, seg, rev: (_segment_cumsum_mxu(x, seg, rev), None),
                            lambda seg, rev, _, g: (_segment_cumsum_mxu(g, seg, not rev),))


def _gla_chunk(st, q, k, v, b, *, rev, consts):
    row = consts
    c = q.shape[0]
    o = lax.dot_general((q * jnp.exp(b)).astype(BF16), st.astype(BF16), NT_DIMS, preferred_element_type=F32)
    for s in range(c):
        e = jnp.exp(jnp.where((row <= s) if rev else (row >= s), b - b[s:s + 1], -jnp.inf))
        o = o + jnp.sum(q * (k[s:s + 1] * e), axis=-1, keepdims=True) * v[s:s + 1]
    b_end = b[0:1] if rev else b[c - 1:c]
    kd = (k * jnp.exp(b_end - b)).astype(BF16)
    st_new = st * jnp.exp(b_end) + lax.dot_general(v.astype(BF16), kd, TN_DIMS, preferred_element_type=F32)
    return st_new, o


def _gla_fwd(q, k, v, v_cb, g, *, rev, add, name):
    L = q.shape[0]
    blk = min(GLA_BLOCK, L)
    nblk, nsub = L // blk, blk // GLA_CHUNK
    hd, hps = HG_HEAD_DIM, GLA_HEADS_PER_STEP
    wide = hd * hps

    def body(*refs):
        q_ref, k_ref, v_ref, g_ref = refs[:4]
        add_ref = refs[4] if add is not None else None
        o_ref, st_out, st_scr, b_scr = refs[-4:]
        consts = _gla_consts(rev)

        @pl.when(pl.program_id(1) == 0)
        def _():
            st_scr[...] = jnp.zeros(st_scr.shape, F32)

        st_out[...] = st_scr[...]
        b_scr[...] = _segment_cumsum(g_ref[...], GLA_CHUNK, rev)

        def step(i, carry):
            sl = pl.ds(pl.multiple_of((nsub - 1 - i if rev else i) * GLA_CHUNK, GLA_CHUNK), GLA_CHUNK)
            lanes = [slice(hi * hd, (hi + 1) * hd) for hi in range(hps)]
            ins = [(st_scr[hi], q_ref[sl, ln], k_ref[sl, ln], v_ref[sl, ln], b_scr[sl, ln])
                   for hi, ln in enumerate(lanes)]
            adds = [add_ref[sl, ln] for ln in lanes] if add_ref is not None else None
            outs = [_gla_chunk(*args, rev=rev, consts=consts) for args in ins]
            for hi, ln in enumerate(lanes):
                st_scr[hi] = outs[hi][0]
                o_ref[sl, ln] = outs[hi][1] if adds is None else outs[hi][1] + adds[hi]
            return carry

        lax.fori_loop(0, nsub, step, 0)

    def pos(j):
        return nblk - 1 - j if rev else j

    spec = pl.BlockSpec((blk, wide), lambda h, j: (pos(j), h))
    in_specs = [spec, spec, pl.BlockSpec((blk, wide), lambda h, j: (pos(j), v_cb // hps + h)), spec]
    args = [q, k, v, g]
    if add is not None:
        in_specs.append(spec)
        args.append(add)
    return pl.pallas_call(
        body, name=name, grid=(HG_HEADS // hps, nblk), in_specs=in_specs,
        out_specs=[spec, pl.BlockSpec((hps, None, hd, hd), lambda h, j: (h, j, 0, 0))],
        out_shape=[jax.ShapeDtypeStruct((L, HG_HEADS * hd), F32),
                   jax.ShapeDtypeStruct((HG_HEADS, nblk, hd, hd), F32)],
        scratch_shapes=[pltpu.VMEM((hps, hd, hd), F32), pltpu.VMEM((blk, wide), F32)],
        compiler_params=_cparams("parallel", "arbitrary"),
    )(*args)


def _gla_bwd(q, k, v, v_cb, g, do, states, *, rev, adds, name):
    L = q.shape[0]
    blk = min(GLA_BLOCK, L)
    nblk, nsub = L // blk, blk // GLA_CHUNK
    hd, hps = HG_HEAD_DIM, GLA_HEADS_PER_STEP
    wide = hd * hps
    n_add = 0 if adds is None else 2

    def body(*refs):
        q_ref, k_ref, v_ref, g_ref, do_ref, st_in = refs[:6]
        add_refs = refs[6:6 + n_add]
        dq_ref, dk_ref, dv_ref, dg_ref, dst_scr, sub_scr, b_scr, db_scr = refs[6 + n_add:]
        consts = _gla_consts(rev)
        chunk = functools.partial(_gla_chunk, rev=rev, consts=consts)

        @pl.when(pl.program_id(1) == 0)
        def _():
            dst_scr[...] = jnp.zeros(dst_scr.shape, F32)

        b_scr[...] = _segment_cumsum(g_ref[...], GLA_CHUNK, rev)

        def rows(i):
            return pl.ds(pl.multiple_of((nsub - 1 - i if rev else i) * GLA_CHUNK, GLA_CHUNK), GLA_CHUNK)

        for hi in range(hps):
            sub_scr[hi] = st_in[hi]

        def replay(i, carry):
            sl = rows(i)
            lanes = [slice(hi * hd, (hi + 1) * hd) for hi in range(hps)]
            ins = [(sub_scr[i * hps + hi], q_ref[sl, ln], k_ref[sl, ln], v_ref[sl, ln], b_scr[sl, ln])
                   for hi, ln in enumerate(lanes)]
            outs = [chunk(*args)[0] for args in ins]
            for hi in range(hps):
                sub_scr[(i + 1) * hps + hi] = outs[hi]
            return carry

        lax.fori_loop(0, nsub - 1, replay, 0)

        def step(n, carry):
            i = nsub - 1 - n
            sl = rows(i)
            lanes = [slice(hi * hd, (hi + 1) * hd) for hi in range(hps)]
            ins = [(sub_scr[i * hps + hi], q_ref[sl, ln], k_ref[sl, ln], v_ref[sl, ln], b_scr[sl, ln])
                   for hi, ln in enumerate(lanes)]
            cts = [(dst_scr[hi], do_ref[sl, ln]) for hi, ln in enumerate(lanes)]
            adds = [(add_refs[0][sl, ln], add_refs[1][sl, ln]) for ln in lanes] if n_add else None
            outs = [jax.vjp(chunk, *args)[1](ct) for args, ct in zip(ins, cts)]
            for hi, ln in enumerate(lanes):
                dst, dq, dk, dv, db = outs[hi]
                dst_scr[hi] = dst
                dq_ref[sl, ln] = dq if adds is None else dq + adds[hi][0]
                dk_ref[sl, ln] = dk
                dv_ref[sl, ln] = dv if adds is None else dv + adds[hi][1]
                db_scr[sl, ln] = db
            return carry

        lax.fori_loop(0, nsub, step, 0)
        dg_ref[...] = _segment_cumsum(db_scr[...], GLA_CHUNK, not rev)

    def pos(j):
        p = nblk - 1 - j
        return nblk - 1 - p if rev else p

    spec = pl.BlockSpec((blk, wide), lambda h, j: (pos(j), h))
    in_specs = [spec, spec, pl.BlockSpec((blk, wide), lambda h, j: (pos(j), v_cb // hps + h)), spec, spec,
                pl.BlockSpec((hps, None, hd, hd), lambda h, j: (h, nblk - 1 - j, 0, 0))]
    args = [q, k, v, g, do, states]
    if adds is not None:
        in_specs += [spec, spec]
        args += list(adds)
    out = jax.ShapeDtypeStruct((L, HG_HEADS * hd), F32)
    return pl.pallas_call(
        body, name=name, grid=(HG_HEADS // hps, nblk), in_specs=in_specs,
        out_specs=[spec] * 4, out_shape=[out] * 4,
        scratch_shapes=[pltpu.VMEM((hps, hd, hd), F32), pltpu.VMEM((nsub * hps, hd, hd), F32),
                        pltpu.VMEM((blk, wide), F32), pltpu.VMEM((blk, wide), F32)],
        compiler_params=_cparams("parallel", "arbitrary"),
    )(*args)


def _ssd_consts(rev):
    c = SSD_CHUNK
    gw = SSM_HPG * SSM_HEAD_DIM
    r2 = lax.broadcasted_iota(jnp.int32, (c, c), 0)
    c2 = lax.broadcasted_iota(jnp.int32, (c, c), 1)
    low = (r2 <= c2) if rev else (r2 >= c2)
    lane = lax.broadcasted_iota(jnp.int32, (1, gw), 1)
    return low, r2 == c2, lane, lane >> 6


def _expand_heads(v4, head_of_lane):
    first = head_of_lane[:, :LANES] == 0
    cols = [v4[:, j:j + 1] for j in range(SSM_HPG)]
    return jnp.concatenate([jnp.where(first, cols[0], cols[1]), jnp.where(first, cols[2], cols[3])], axis=1)


def _ssd_prep(dtr, bias, alog, dsk, *, rev, consts):
    head_of_lane = consts[-1]
    dt_l = _expand_heads(jax.nn.softplus(dtr + bias), head_of_lane)
    a_l = _expand_heads(-jnp.exp(alog), head_of_lane)
    return dt_l, _segment_cumsum_diff(dt_l * a_l, SSD_CHUNK, rev), _expand_heads(dsk, head_of_lane)


def _ssd_chunk(st, x, bm, cm, dt_l, acum, dsk_l, *, rev, consts, skip):
    low, eye, lane, head_of_lane = consts
    c = x.shape[0]
    xd = x * dt_l
    cb = lax.dot_general(cm.astype(BF16), bm.astype(BF16), NT_DIMS, preferred_element_type=F32)
    y = jnp.dot(cm.astype(BF16), st.astype(BF16), preferred_element_type=F32) * jnp.exp(acum)
    for j in range(SSM_HPG):
        acol = jnp.sum(jnp.where(lane == j * SSM_HEAD_DIM, acum, 0.0), axis=-1, keepdims=True)
        ab = jnp.broadcast_to(acol, (c, c))
        arow = jnp.sum(jnp.where(eye, ab, 0.0), axis=0, keepdims=True)
        lmat = jnp.exp(jnp.where(low, ab - arow, -jnp.inf))
        xj = jnp.where(head_of_lane == j, xd, 0.0)
        y = y + jnp.dot((cb * lmat).astype(BF16), xj.astype(BF16), preferred_element_type=F32)
    a_end = acum[0:1] if rev else acum[c - 1:c]
    xdec = (xd * jnp.exp(a_end - acum)).astype(BF16)
    st_new = st * jnp.exp(a_end) + lax.dot_general(bm.astype(BF16), xdec, TN_DIMS, preferred_element_type=F32)
    if skip:
        y = y + x * dsk_l
    return st_new, y


def _ssd_specs(L, rev):
    blk = min(SSD_BLOCK, L)
    nblk = L // blk
    gw = SSM_HPG * SSM_HEAD_DIM

    def pos(j):
        return nblk - 1 - j if rev else j

    return blk, nblk, gw, pos


def _ssd_fwd(xbc, dtr, bias, alog, dsk, *, rev, skip, add, name):
    L = xbc.shape[0]
    blk, nblk, gw, pos = _ssd_specs(L, rev)
    nsub = blk // SSD_CHUNK
    n = SSM_STATE

    def body(*refs):
        x_ref, b_ref, c_ref, dt_ref, bias_ref, alog_ref, dsk_ref = refs[:7]
        add_ref = refs[7] if add is not None else None
        y_ref, st_out, st_scr, dt_scr, ac_scr = refs[-5:]
        consts = _ssd_consts(rev)

        @pl.when(pl.program_id(1) == 0)
        def _():
            st_scr[...] = jnp.zeros(st_scr.shape, F32)

        st_out[...] = st_scr[...]
        dt_scr[...], ac_scr[...], dsk_l = _ssd_prep(dt_ref[...], bias_ref[...], alog_ref[...], dsk_ref[...],
                                                   rev=rev, consts=consts)

        def step(i, carry):
            sl = pl.ds(pl.multiple_of((nsub - 1 - i if rev else i) * SSD_CHUNK, SSD_CHUNK), SSD_CHUNK)
            st_new, y = _ssd_chunk(st_scr[...], x_ref[sl, :], b_ref[sl, :], c_ref[sl, :], dt_scr[sl, :],
                                   ac_scr[sl, :], dsk_l, rev=rev, consts=consts, skip=skip)
            st_scr[...] = st_new
            if add_ref is not None:
                y = y + add_ref[sl, :]
            y_ref[sl, :] = y
            return carry

        lax.fori_loop(0, nsub, step, 0)

    b0 = D_INNER // n
    yspec = pl.BlockSpec((blk, gw), lambda g, j: (pos(j), g))
    pspec = pl.BlockSpec((None, 1, SSM_HPG), lambda g, j: (g, 0, 0))
    in_specs = [yspec,
                pl.BlockSpec((blk, n), lambda g, j: (pos(j), b0 + g)),
                pl.BlockSpec((blk, n), lambda g, j: (pos(j), b0 + SSM_GROUPS + g)),
                pl.BlockSpec((None, blk, SSM_HPG), lambda g, j: (g, pos(j), 0)),
                pspec, pspec, pspec]
    args = [xbc, xbc, xbc, dtr, bias, alog, dsk]
    if add is not None:
        in_specs.append(yspec)
        args.append(add)
    return pl.pallas_call(
        body, name=name, grid=(SSM_GROUPS, nblk), in_specs=in_specs,
        out_specs=[yspec, pl.BlockSpec((None, None, n, gw), lambda g, j: (g, j, 0, 0))],
        out_shape=[jax.ShapeDtypeStruct((L, D_INNER), F32),
                   jax.ShapeDtypeStruct((SSM_GROUPS, nblk, n, gw), F32)],
        scratch_shapes=[pltpu.VMEM((n, gw), F32), pltpu.VMEM((blk, gw), F32), pltpu.VMEM((blk, gw), F32)],
        compiler_params=_cparams("parallel", "arbitrary"),
    )(*args)


def _ssd_bwd(xbc, dtr, bias, alog, dsk, dy, states, *, rev, skip, adds, name):
    L = xbc.shape[0]
    blk, nblk, gw, _ = _ssd_specs(L, rev)
    nsub = blk // SSD_CHUNK
    n = SSM_STATE
    n_add = 0 if adds is None else 3

    def body(*refs):
        x_ref, b_ref, c_ref, dt_ref, bias_ref, alog_ref, dsk_ref, dy_ref, st_in = refs[:9]
        add_refs = refs[9:9 + n_add]
        outs = refs[9 + n_add:]
        dx_ref, db_ref, dc_ref, ddt_ref, dbias_ref, dalog_ref, ddsk_ref = outs[:7]
        dst_scr, sub_scr, dt_scr, ac_scr, ddt_scr, dac_scr = outs[7:]
        consts = _ssd_consts(rev)
        chunk = functools.partial(_ssd_chunk, rev=rev, consts=consts, skip=skip)
        prep = functools.partial(_ssd_prep, rev=rev, consts=consts)

        @pl.when(pl.program_id(1) == 0)
        def _():
            dst_scr[...] = jnp.zeros(dst_scr.shape, F32)
            dbias_ref[...] = jnp.zeros(dbias_ref.shape, F32)
            dalog_ref[...] = jnp.zeros(dalog_ref.shape, F32)
            ddsk_ref[...] = jnp.zeros(ddsk_ref.shape, F32)

        narrow = (dt_ref[...], bias_ref[...], alog_ref[...], dsk_ref[...])
        (dt_scr[...], ac_scr[...], dsk_l), prep_vjp = jax.vjp(prep, *narrow)

        def rows(i):
            return pl.ds(pl.multiple_of((nsub - 1 - i if rev else i) * SSD_CHUNK, SSD_CHUNK), SSD_CHUNK)

        def operands(i):
            sl = rows(i)
            return (x_ref[sl, :], b_ref[sl, :], c_ref[sl, :], dt_scr[sl, :], ac_scr[sl, :], dsk_l)

        def replay(i, st):
            sub_scr[i] = st
            return chunk(st, *operands(i))[0]

        lax.fori_loop(0, nsub, replay, st_in[...])

        def step(k, ddsk_l):
            i = nsub - 1 - k
            sl = rows(i)
            _, vjp = jax.vjp(chunk, sub_scr[i], *operands(i))
            dst, dx, db, dc, ddt_l, dac, ddsk_k = vjp((dst_scr[...], dy_ref[sl, :]))
            dst_scr[...] = dst
            if n_add:
                dx = dx + add_refs[0][sl, :]
                db = db + add_refs[1][sl, :]
                dc = dc + add_refs[2][sl, :]
            dx_ref[sl, :] = dx
            db_ref[sl, :] = db
            dc_ref[sl, :] = dc
            ddt_scr[sl, :] = ddt_l
            dac_scr[sl, :] = dac
            return ddsk_l + ddsk_k

        ddsk_l = lax.fori_loop(0, nsub, step, jnp.zeros((1, gw), F32))
        ddt, dbias, dalog, ddsk = prep_vjp((ddt_scr[...], dac_scr[...], ddsk_l))
        ddt_ref[...] = ddt
        dbias_ref[...] += dbias
        dalog_ref[...] += dalog
        ddsk_ref[...] += ddsk

    def pos(j):
        p = nblk - 1 - j
        return nblk - 1 - p if rev else p

    b0 = D_INNER // n
    xspec = pl.BlockSpec((blk, gw), lambda g, j: (pos(j), g))
    nspec = pl.BlockSpec((blk, n), lambda g, j: (pos(j), g))
    dtspec = pl.BlockSpec((None, blk, SSM_HPG), lambda g, j: (g, pos(j), 0))
    pspec = pl.BlockSpec((None, 1, SSM_HPG), lambda g, j: (g, 0, 0))
    in_specs = [xspec,
                pl.BlockSpec((blk, n), lambda g, j: (pos(j), b0 + g)),
                pl.BlockSpec((blk, n), lambda g, j: (pos(j), b0 + SSM_GROUPS + g)),
                dtspec, pspec, pspec, pspec, xspec,
                pl.BlockSpec((None, None, n, gw), lambda g, j: (g, nblk - 1 - j, 0, 0))]
    args = [xbc, xbc, xbc, dtr, bias, alog, dsk, dy, states]
    if adds is not None:
        in_specs += [xspec, nspec, nspec]
        args += list(adds)
    par = jax.ShapeDtypeStruct((SSM_GROUPS, 1, SSM_HPG), F32)
    return pl.pallas_call(
        body, name=name, grid=(SSM_GROUPS, nblk), in_specs=in_specs,
        out_specs=[xspec, nspec, nspec, dtspec, pspec, pspec, pspec],
        out_shape=[jax.ShapeDtypeStruct((L, D_INNER), F32), jax.ShapeDtypeStruct((L, GN), F32),
                   jax.ShapeDtypeStruct((L, GN), F32), jax.ShapeDtypeStruct((SSM_GROUPS, L, SSM_HPG), F32),
                   par, par, par],
        scratch_shapes=[pltpu.VMEM((n, gw), F32), pltpu.VMEM((nsub, n, gw), F32)] + [pltpu.VMEM((blk, gw), F32)] * 4,
        compiler_params=_cparams("parallel", "arbitrary"),
    )(*args)


def _norm_fwd(h, w, name):
    d = h.shape[1]
    return _row_kernel(lambda hv, wv: (_rms(hv, wv),), [(h, d, 0)], [w], [(d, BF16)], [], name=name)[0]


def _norm_bwd(h, w, du, dh_in, name):
    d = h.shape[1]

    def fn(hv, duv, dhv, wv):
        _, vjp = jax.vjp(_rms, hv, wv)
        dh, dw = vjp(duv.astype(F32))
        dh = dh + dhv
        return dh, dh, dw

    return _row_kernel(fn, [(h, d, 0), (du, d, 0), (dh_in, d, 0)], [w], [(d, F32), (d, BF16)], [(1, d)], name=name)


def _ffn_fwd(h, norm_w, w_in, w_out, conv_w, conv_b, tag):
    u = _norm_fwd(h, norm_w, f"{tag}_norm")
    pg = _matmul(u, w_in, b_cols=(0, D_FF), name=f"{tag}_in_gate")
    pv = _matmul(u, w_in, b_cols=(1, D_FF), name=f"{tag}_in_val")
    y = _col_kernel(lambda g, v, w, b: (_ffn_mid_fn(g, v, w, b),), [(pg, 0), (pv, 0)], [(conv_w, 0), (conv_b, 0)],
                    [BF16], [], name=f"{tag}_mid", n_tiles=D_FF // COL_TILE)[0]
    h_out = _matmul(y, w_out, add=h, name=f"{tag}_out")
    return h_out, (u, pg, pv, y)


def _ffn_bwd(h, dh, dh16, saved, norm_w, w_in, w_out, conv_w, conv_b, tag, after=None):
    u, pg, pv, y = saved
    dy = _matmul(dh16, w_out, nt=True, after=after, name=f"{tag}_out_dx")
    dw_out = _matmul(y, dh16, ta=True, name=f"{tag}_out_dw")

    def fn(g, v, ct, w, b):
        _, vjp = jax.vjp(_ffn_mid_fn, g, v, w, b)
        return vjp(ct)

    dpg, dpv, dcw, dcb = _col_kernel(fn, [(pg, 0), (pv, 0), (dy, 0)], [(conv_w, 0), (conv_b, 0)], [BF16, BF16],
                                     [FFN_CONV, 1], name=f"{tag}_mid_bwd", n_tiles=D_FF // COL_TILE)
    du = _matmul(dpg, w_in, nt=True, b_cols=(0, D_FF), name=f"{tag}_gate_dx")
    du = _matmul(dpv, w_in, nt=True, b_cols=(1, D_FF), add=du, out_dtype=BF16, name=f"{tag}_val_dx")
    dw_gate = _matmul(u, dpg, ta=True, name=f"{tag}_gate_dw")
    dw_val = _matmul(u, dpv, ta=True, name=f"{tag}_val_dw")
    dh, dh16, dnw = _norm_bwd(h, norm_w, du, dh, f"{tag}_norm_bwd")
    return dh, dh16, dict(w_in=(dw_gate, dw_val), w_out=dw_out, conv_w=dcw, conv_b=dcb, norm=dnw)


def _hgrn2_fwd(h, norm_w, w_in, lb_logits, a_norm_w, w_out, after=None):
    d = D_MODEL
    u = _norm_fwd(h, norm_w, "hg_norm")
    pa = _matmul(u, w_in, after=after, name="hg_in")
    qs, lf_fw, k_fw, lf_bw, k_bw = _row_kernel(
        _hgrn2_pre_fn, [(pa, d, 0), (pa, d, 1), (pa, d, 2)], [lb_logits], [(d, F32)] * 5, [], name="hg_pre")
    o_fw, st_fw = _gla_fwd(qs, k_fw, pa, 3 * HG_HEADS, lf_fw, rev=False, add=None, name="hg_gla_fw")
    o, st_bw = _gla_fwd(qs, k_bw, pa, 3 * HG_HEADS, lf_bw, rev=True, add=o_fw, name="hg_gla_bw")
    y = _row_kernel(lambda ov, gv, wv: (_hgrn2_post_fn(ov, gv, wv),), [(o, d, 0), (pa, d, 4)], [a_norm_w],
                    [(d, BF16)], [], name="hg_post")[0]
    h_out = _matmul(y, w_out, add=h, name="hg_out")
    return h_out, (u, pa, qs, lf_fw, k_fw, lf_bw, k_bw, st_fw, st_bw, o, y)


def _hgrn2_bwd(h, dh, dh16, saved, norm_w, w_in, lb_logits, a_norm_w, w_out, send, after=None):
    d = D_MODEL
    u, pa, qs, lf_fw, k_fw, lf_bw, k_bw, st_fw, st_bw, o, y = saved
    dy = _matmul(dh16, w_out, nt=True, after=after, name="hg_out_dx")
    dw_out = _matmul(y, dh16, ta=True, name="hg_out_dw")

    def post_bwd(ov, gv, ct, wv):
        _, vjp = jax.vjp(_hgrn2_post_fn, ov, gv, wv)
        return vjp(ct)

    do, dg, dnw = _row_kernel(post_bwd, [(o, d, 0), (pa, d, 4), (dy, d, 0)], [a_norm_w], [(d, F32), (d, F32)],
                              [(1, HG_HEAD_DIM)], name="hg_post_bwd")
    dq1, dk_fw, dv1, dlf_fw = _gla_bwd(qs, k_fw, pa, 3 * HG_HEADS, lf_fw, do, st_fw, rev=False, adds=None,
                                       name="hg_gla_fw_bwd")
    dqs, dk_bw, dv, dlf_bw = _gla_bwd(qs, k_bw, pa, 3 * HG_HEADS, lf_bw, do, st_bw, rev=True, adds=(dq1, dv1),
                                      name="hg_gla_bw_bwd")

    def pre_bwd(qr, fr, br, c0, c1, c2, c3, c4, dvv, dgv, lbl):
        _, vjp = jax.vjp(_hgrn2_pre_fn, qr, fr, br, lbl)
        dq, df, db, dlbl = vjp((c0, c1, c2, c3, c4))
        return jnp.concatenate([dq, df, db, dvv, dgv], axis=1), dlbl

    rows = [(pa, d, 0), (pa, d, 1), (pa, d, 2), (dqs, d, 0), (dlf_fw, d, 0), (dk_fw, d, 0), (dlf_bw, d, 0),
            (dk_bw, d, 0), (dv, d, 0), (dg, d, 0)]
    dpa, dlbl = _row_kernel(pre_bwd, rows, [lb_logits], [(5 * d, BF16)], [lb_logits.shape], name="hg_pre_bwd")
    dw_in = _matmul(u, dpa, ta=True, name="hg_in_dw")
    token = send(dw_in, dw_out)
    du = _matmul(dpa, w_in, nt=True, out_dtype=BF16, after=token, name="hg_in_dx")
    dh, dh16, dn1 = _norm_bwd(h, norm_w, du, dh, "hg_norm_bwd")
    return dh, dh16, dict(lb=dlbl, a_norm=dnw, norm=dn1)


def _group_params(p):
    return p.reshape(SSM_GROUPS, 1, SSM_HPG)


def _mamba_fwd(h, norm_w, w_z, w_xbc, w_dt, conv_w, conv_b, dt_bias, a_log, d_skip, b_norm_w, w_out):
    L = h.shape[0]
    u = _norm_fwd(h, norm_w, "mb_norm")
    z = _matmul(u, w_z, name="mb_in_z")
    xbc_raw = _matmul(u, w_xbc, name="mb_in_xbc")
    dt_raw = _matmul(u, w_dt, name="mb_in_dt")
    xbc = _col_kernel(lambda xv, w, b: (_mamba_conv_fn(xv, w, b),), [(xbc_raw, 0)], [(conv_w, 0), (conv_b, 0)],
                      [F32], [], name="mb_conv", n_tiles=CONV_DIM // COL_TILE)[0]
    dtr = dt_raw.reshape(L, 2, SSM_GROUPS, SSM_HPG).transpose(1, 2, 0, 3)
    bias, alog = dt_bias.reshape(2, -1), a_log.reshape(2, -1)
    dsk = _group_params(d_skip.reshape(-1))
    y_fw, st_fw = _ssd_fwd(xbc, dtr[0], _group_params(bias[0]), _group_params(alog[0]), dsk, rev=False, skip=True,
                           add=None, name="mb_ssd_fw")
    ysum, st_bw = _ssd_fwd(xbc, dtr[1], _group_params(bias[1]), _group_params(alog[1]), dsk, rev=True, skip=False,
                           add=y_fw, name="mb_ssd_bw")
    y = _row_kernel(lambda yv, zv, wv: (_mamba_post_fn(yv, zv, wv),), [(ysum, D_INNER, 0), (z, D_INNER, 0)],
                    [b_norm_w], [(D_INNER, BF16)], [], name="mb_post")[0]
    h_out = _matmul(y, w_out, add=h, name="mb_out")
    return h_out, (u, z, xbc_raw, xbc, dtr, st_fw, st_bw, ysum, y)


def _mamba_bwd(h, dh, dh16, saved, norm_w, w_z, w_xbc, w_dt, conv_w, conv_b, dt_bias, a_log, d_skip, b_norm_w, w_out,
               after=None):
    L = h.shape[0]
    u, z, xbc_raw, xbc, dtr, st_fw, st_bw, ysum, y = saved
    dy = _matmul(dh16, w_out, nt=True, after=after, name="mb_out_dx")
    dw_out = _matmul(y, dh16, ta=True, name="mb_out_dw")

    def post_bwd(yv, zv, ct, wv):
        _, vjp = jax.vjp(_mamba_post_fn, yv, zv, wv)
        return vjp(ct)

    dys, dz, dbn = _row_kernel(post_bwd, [(ysum, D_INNER, 0), (z, D_INNER, 0), (dy, D_INNER, 0)], [b_norm_w],
                               [(D_INNER, F32), (D_INNER, BF16)], [(1, D_INNER)], name="mb_post_bwd")
    bias, alog = dt_bias.reshape(2, -1), a_log.reshape(2, -1)
    dsk = _group_params(d_skip.reshape(-1))
    dx1, db1, dc1, ddt_fw, dbias_fw, dalog_fw, ddsk = _ssd_bwd(
        xbc, dtr[0], _group_params(bias[0]), _group_params(alog[0]), dsk, dys, st_fw, rev=False, skip=True,
        adds=None, name="mb_ssd_fw_bwd")
    dx, db, dc, ddt_bw, dbias_bw, dalog_bw, _ = _ssd_bwd(
        xbc, dtr[1], _group_params(bias[1]), _group_params(alog[1]), dsk, dys, st_bw, rev=True, skip=False,
        adds=(dx1, db1, dc1), name="mb_ssd_bw_bwd")

    def conv_bwd(n_tiles, ct, first):
        def fn(xv, ctv, w, b):
            _, vjp = jax.vjp(_mamba_conv_fn, xv, w, b)
            return vjp(ctv)
        return _col_kernel(fn, [(xbc_raw, first), (ct, 0)], [(conv_w, first), (conv_b, first)], [BF16],
                           [SSM_CONV, 1], name=f"mb_conv_bwd_{first}", n_tiles=n_tiles)

    nx, nb = D_INNER // COL_TILE, GN // COL_TILE
    parts = [conv_bwd(nx, dx, 0), conv_bwd(nb, db, nx), conv_bwd(nb, dc, nx + nb)]
    dxbc = jnp.concatenate([p[0] for p in parts], axis=1)
    dcw = jnp.concatenate([p[1] for p in parts], axis=1)
    dcb = jnp.concatenate([p[2] for p in parts], axis=1)
    ddt = jnp.stack([ddt_fw, ddt_bw]).transpose(2, 0, 1, 3).reshape(L, 2 * SSM_GROUPS * SSM_HPG).astype(BF16)
    du = _matmul(dz, w_z, nt=True, name="mb_z_dx")
    du = _matmul(dxbc, w_xbc, nt=True, add=du, name="mb_xbc_dx")
    du = _matmul(ddt, w_dt, nt=True, add=du, out_dtype=BF16, name="mb_dt_dx")
    dw_in = jnp.concatenate([_matmul(u, dz, ta=True, name="mb_z_dw"), _matmul(u, dxbc, ta=True, name="mb_xbc_dw"),
                             _matmul(u, ddt, ta=True, name="mb_dt_dw")], axis=1)
    dh, dh16, dn1 = _norm_bwd(h, norm_w, du, dh, "mb_norm_bwd")
    grads = dict(w_in=dw_in, w_out=dw_out, conv_w=dcw, conv_b=dcb, b_norm=dbn, norm=dn1,
                 dt_bias=jnp.stack([dbias_fw, dbias_bw]).reshape(1, 2, -1),
                 a_log=jnp.stack([dalog_fw, dalog_bw]).reshape(1, 2, -1), d_skip=ddsk.reshape(1, -1))
    return dh, dh16, grads


def _loss_head(h, target, w):
    d = h.shape[1]

    def fn(hv, tv, wv):
        def loss(hv, wv):
            err = _rms(hv, wv) - tv
            return 0.5 * jnp.sum(jnp.mean(err * err, axis=-1, keepdims=True), axis=0, keepdims=True)
        val, vjp = jax.vjp(loss, hv, wv)
        dh, dw = vjp(jnp.ones((1, 1), F32))
        return dh, dh, val, dw

    return _row_kernel(fn, [(h, d, 0), (target, d, 0)], [w], [(d, F32), (d, BF16)], [(1, 1), (1, d)], name="loss_head")


def _local_step(x, target, rep, comm):
    def ffn_args(i, w, small):
        return (rep["norm2_w"][i:i + 1], w["w_in"], w["w_out"], small["ffn_conv_w"][i], rep["ffn_conv_b"][i:i + 1])

    def mamba_args(w, small):
        w_in = w["w_in"]
        return (rep["norm1_w"][1:2], w_in[:, :D_INNER], w_in[:, D_INNER:D_INNER + CONV_DIM],
                w_in[:, D_INNER + CONV_DIM:], small["b_conv_w"][0], small["b_conv_b"], rep["b_dt_bias"],
                rep["b_a_log"], rep["b_d_skip"], small["b_norm_w"], w["w_out"])

    w_hg = comm.weights("hg", None)
    hg = (rep["norm1_w"][0:1], w_hg["w_in"], rep["a_lb_logits"], rep["a_norm_w"], w_hg["w_out"])
    h0 = x
    h1, s_hg = _hgrn2_fwd(h0, *hg, after=w_hg.get("token"))
    w_f0 = comm.weights("f0", h1)
    small = w_f0["small"]
    f0 = ffn_args(0, w_f0, small)
    h2, s_f0 = _ffn_fwd(h1, *f0, "ffn0")
    mb = mamba_args(comm.weights("mb", h2), small)
    h3, s_mb = _mamba_fwd(h2, *mb)
    f1 = ffn_args(1, comm.weights("f1", h3), small)
    h4, s_f1 = _ffn_fwd(h3, *f1, "ffn1")
    dh, dh16, loss, d_final = _loss_head(h4, target, rep["final_norm_w"].reshape(1, -1))

    dh, dh16, g_f1 = _ffn_bwd(h3, dh, dh16, s_f1, *f1, "ffn1")
    token = comm.send("f1", dict(w_in=g_f1["w_in"], w_out=g_f1["w_out"]))
    dh, dh16, g_mb = _mamba_bwd(h2, dh, dh16, s_mb, *mb, after=token)
    token = comm.send("mb", dict(w_in=g_mb["w_in"], w_out=g_mb["w_out"]))
    dh, dh16, g_f0 = _ffn_bwd(h1, dh, dh16, s_f0, *f0, "ffn0", after=token)
    token = comm.send("f0", dict(w_in=g_f0["w_in"], w_out=g_f0["w_out"]))
    small_grads = dict(b_conv_w=g_mb["conv_w"][None], b_conv_b=g_mb["conv_b"], b_norm_w=g_mb["b_norm"],
                       ffn_conv_w=jnp.stack([g_f0["conv_w"], g_f1["conv_w"]]))
    dh, dh16, g_hg = _hgrn2_bwd(
        h0, dh, dh16, s_hg, *hg, after=token,
        send=lambda dw_in, dw_out: comm.send("hg", dict(w_in=dw_in, w_out=dw_out, small=small_grads)))
    grads = dict(
        norm1_w=jnp.concatenate([g_hg["norm"], g_mb["norm"]], axis=0),
        norm2_w=jnp.concatenate([g_f0["norm"], g_f1["norm"]], axis=0),
        a_lb_logits=g_hg["lb"], a_norm_w=g_hg["a_norm"], b_dt_bias=g_mb["dt_bias"], b_a_log=g_mb["a_log"],
        b_d_skip=g_mb["d_skip"], ffn_conv_b=jnp.concatenate([g_f0["conv_b"], g_f1["conv_b"]], axis=0),
        final_norm_w=d_final.reshape(-1),
    )
    return loss, dh, grads


def _mesh_pos():
    return lax.axis_index("x"), lax.axis_index("y"), lax.axis_index("c")


N_COPIES = N_DEV - 1


def _comm_call(body, ins, out_shape, name):
    n = len(ins)
    hbm = pl.BlockSpec(memory_space=pl.ANY)
    return pl.pallas_call(
        body, name=name, out_shape=out_shape, in_specs=[hbm] * n, out_specs=[hbm] * n,
        scratch_shapes=[pltpu.SemaphoreType.DMA((n * N_COPIES,)), pltpu.SemaphoreType.DMA((n * N_COPIES,)),
                        pltpu.SemaphoreType.DMA((n,))],
    )(*ins)


def _all_gather(shards, name):
    n = len(shards)

    def body(*refs):
        x_refs, out_refs = refs[:n], refs[n:2 * n]
        send_sems, recv_sems, local_sems = refs[2 * n:]
        x, y, c = _mesh_pos()
        me, sibling = (x, y, c), (x, y, 1 - c)
        chips = [(1 - x, y), (x, 1 - y), (1 - x, 1 - y)]

        def copy(w, k, block, to, own=False):
            px, py, pc = block
            dst = out_refs[w].at[4 * px + 2 * py + pc]
            return pltpu.make_async_remote_copy(
                src_ref=x_refs[w] if own else dst, dst_ref=dst, send_sem=send_sems.at[w * N_COPIES + k],
                recv_sem=recv_sems.at[w * N_COPIES + k], device_id=to, device_id_type=MESH)

        mine = [pltpu.make_async_copy(x_refs[w], out_refs[w].at[4 * x + 2 * y + c], local_sems.at[w]) for w in range(n)]
        for cp in mine:
            cp.start()
        first = [copy(w, 1 + j, me, (*chip, c), own=True) for j, chip in enumerate(chips) for w in range(n)]
        first += [copy(w, 0, me, sibling, own=True) for w in range(n)]
        for cp in first:
            cp.start()
        passed = []
        for j, chip in enumerate(chips):
            for w in range(n):
                copy(w, 1 + j, (*chip, c), me).wait_recv()
                passed.append(copy(w, 4 + j, (*chip, c), sibling))
                passed[-1].start()
        for w in range(n):
            copy(w, 0, sibling, me).wait_recv()
        for j, chip in enumerate(chips):
            for w in range(n):
                copy(w, 4 + j, (*chip, 1 - c), me).wait_recv()
        for cp in first + passed:
            cp.wait_send()
        for cp in mine:
            cp.wait()

    out_shape = [jax.ShapeDtypeStruct((N_DEV, *s.shape), s.dtype) for s in shards]
    return _comm_call(body, shards, out_shape, name)


HBM_SPEC = pl.BlockSpec(memory_space=pltpu.HBM)
SEM_SPEC = pl.BlockSpec(memory_space=pltpu.SEMAPHORE)
EFFECT = pltpu.SideEffectType.DATAFLOW_SIDE_EFFECTING
PEER_ORDER = (4, 2, 6, 5, 3, 7, 1)


def _peer_copies(src_refs, land_refs, send_sems, recv_sems, gather):
    x, y, c = _mesh_pos()
    me = 4 * x + 2 * y + c
    copies = []
    for k in PEER_ORDER:
        px = 1 - x if k & 4 else x
        py = 1 - y if k & 2 else y
        pc = 1 - c if k & 1 else c
        for w, (src, land) in enumerate(zip(src_refs, land_refs)):
            copies.append(pltpu.make_async_remote_copy(
                src_ref=src if gather else src.at[4 * px + 2 * py + pc],
                dst_ref=land.at[me] if gather else land.at[k - 1],
                send_sem=send_sems.at[w * N_COPIES + k - 1], recv_sem=recv_sems.at[w * N_COPIES + k - 1],
                device_id=(px, py, pc), device_id_type=MESH))
    return copies


def _all_gather_small(shard, name):
    def body(x_ref, out_ref, send_sems, recv_sems):
        x, y, c = _mesh_pos()
        out_ref[4 * x + 2 * y + c] = x_ref[...]
        copies = _peer_copies([x_ref], [out_ref], send_sems, recv_sems, True)
        for cp in copies:
            cp.start()
        for cp in copies:
            cp.wait()

    vmem = pl.BlockSpec(memory_space=pltpu.VMEM)
    return pl.pallas_call(
        body, name=name, out_shape=jax.ShapeDtypeStruct((N_DEV, *shard.shape), shard.dtype), in_specs=[vmem],
        out_specs=vmem, scratch_shapes=[pltpu.SemaphoreType.DMA((N_COPIES,)), pltpu.SemaphoreType.DMA((N_COPIES,))],
    )(shard)


def _copies_start(srcs, *, gather, after, name):
    n = len(srcs)
    lands = [lax.empty(((N_DEV,) + s.shape) if gather else ((N_COPIES,) + s.shape[1:]), s.dtype) for s in srcs]

    def body(*refs):
        src_refs, land_refs = refs[:n], refs[n:2 * n]
        send_sems, recv_sems = refs[-2 * n - 3], refs[-2 * n - 2]
        for cp in _peer_copies(src_refs, land_refs, send_sems, recv_sems, gather):
            cp.start()
        refs[-1][...] = jnp.zeros(refs[-1].shape, F32)

    ins = [pltpu.with_memory_space_constraint(a, pltpu.HBM) for a in srcs + lands]
    in_specs = [HBM_SPEC] * (2 * n)
    if after is not None:
        ins.append(after)
        in_specs.append(pl.BlockSpec(memory_space=pl.ANY))
    sems = pltpu.SemaphoreType.DMA((n * N_COPIES,))
    out = pl.pallas_call(
        body, name=name,
        out_shape=(sems, sems, *[pltpu.HBM(a.shape, a.dtype) for a in srcs + lands], jax.ShapeDtypeStruct((8, LANES), F32)),
        in_specs=in_specs,
        out_specs=(SEM_SPEC, SEM_SPEC, *[HBM_SPEC] * (2 * n), pl.BlockSpec(memory_space=pltpu.VMEM)),
        input_output_aliases={i: 2 + i for i in range(2 * n)},
        compiler_params=pltpu.CompilerParams(has_side_effects=EFFECT),
    )(*ins)
    return out[0], out[1], list(out[2:2 + n]), list(out[2 + n:2 + 2 * n]), out[-1]


def _copies_wait(started, *, gather, after, name):
    send_sems, recv_sems, srcs, lands, _ = started
    n = len(srcs)

    def body(*refs):
        src_refs, land_refs = refs[:n], refs[n:2 * n]
        for cp in _peer_copies(src_refs, land_refs, refs[2 * n], refs[2 * n + 1], gather):
            cp.wait_send()
            cp.wait_recv()

    out = pl.pallas_call(
        body, name=name, out_shape=tuple(pltpu.HBM(a.shape, a.dtype) for a in srcs + lands),
        in_specs=[HBM_SPEC] * (2 * n) + [SEM_SPEC, SEM_SPEC, pl.BlockSpec(memory_space=pl.ANY)],
        out_specs=tuple([HBM_SPEC] * (2 * n)), input_output_aliases={i: i for i in range(2 * n)},
        compiler_params=pltpu.CompilerParams(has_side_effects=EFFECT),
    )(*srcs, *lands, send_sems, recv_sems, after)
    return list(out[n:])


def _adamw(got, w, m, v, *, name, tile, own=None, layer=None, into=None):
    rows, width = w.shape[-2:]
    n_got = got.shape[0]
    c1 = 1.0 / (1.0 - ADAM_B1 ** ADAM_STEP)
    c2 = 1.0 / (1.0 - ADAM_B2 ** ADAM_STEP)

    def body(*refs):
        got_ref, w_ref, m_ref, v_ref = refs[:4]
        g_ref, d_ref, nm_ref, nv_ref = refs[-4:]
        g = got_ref[0] if own is None else refs[4][...] + got_ref[0]
        for s in range(1, n_got):
            g = g + got_ref[s]
        m_new = ADAM_B1 * m_ref[...] + (1.0 - ADAM_B1) * g
        v_new = ADAM_B2 * v_ref[...] + (1.0 - ADAM_B2) * (g * g)
        g_ref[...] = g
        nm_ref[...] = m_new
        nv_ref[...] = v_new
        d_ref[...] = -ADAM_LR * ((m_new * c1) / (jnp.sqrt(v_new * c2) + ADAM_EPS) + ADAM_WD * w_ref[...])

    spec = pl.BlockSpec((tile, width), lambda i: (i, 0))
    wspec = spec if layer is None else pl.BlockSpec((None, tile, width), lambda i: (layer, i, 0))
    args = [got, w, m, v] + ([] if own is None else [own])
    in_specs = [pl.BlockSpec((n_got, tile, width), lambda i: (0, i, 0))] + [wspec] * 3 + [spec] * (len(args) - 4)
    aliases = {}
    if into is not None:
        aliases = {len(args) + i: i for i in range(4)}
        args += list(into)
        in_specs += [pl.BlockSpec(memory_space=pl.ANY)] * 4
    return pl.pallas_call(
        body, name=name, grid=(rows // tile,), in_specs=in_specs, out_specs=[wspec] * 4,
        out_shape=[jax.ShapeDtypeStruct(w.shape, F32)] * 4, input_output_aliases=aliases,
        compiler_params=_cparams("parallel"),
    )(*args)


SHARDED = dict(a_w_in=(2, True), a_w_out=(1, True), b_w_in=(2, True), b_w_out=(1, True), ffn_w_in=(2, True),
               ffn_w_out=(1, True), b_conv_w=(2, False), b_conv_b=(1, False), b_norm_w=(1, False), ffn_conv_w=(2, False))
REPLICATED = ("norm1_w", "norm2_w", "a_lb_logits", "a_norm_w", "b_dt_bias", "b_a_log", "b_d_skip", "ffn_conv_b",
              "final_norm_w")
WEIGHTS = ("norm1_w", "norm2_w", "a_w_in", "a_lb_logits", "a_norm_w", "a_w_out", "b_w_in", "b_conv_w", "b_conv_b",
           "b_dt_bias", "b_a_log", "b_d_skip", "b_norm_w", "b_w_out", "ffn_w_in", "ffn_conv_w", "ffn_conv_b",
           "ffn_w_out", "final_norm_w")


def _pad_rows(flat, multiple):
    n = flat.shape[-1]
    per = PACK_W * multiple
    total = -(-n // per) * per
    flat = jnp.pad(flat, [(0, 0)] * (flat.ndim - 1) + [(0, total - n)])
    return flat.reshape(*flat.shape[:-1], total // PACK_W, PACK_W)


def _to_parts(full, axis, n=N_DEV):
    shp = full.shape
    t = full.reshape(*shp[:axis], n, shp[axis] // n, *shp[axis + 1:])
    return jnp.moveaxis(t, axis, 0)


def _from_parts(parts, axis):
    t = jnp.moveaxis(parts, 0, axis)
    shp = t.shape
    return t.reshape(*shp[:axis], shp[axis] * shp[axis + 1], *shp[axis + 2:])


BIG = tuple(n for n, (_, mm) in SHARDED.items() if mm)
SMALL = tuple(n for n, (_, mm) in SHARDED.items() if not mm)
SMALL_W = 512


def _small_rows(tree, lead):
    rows = []
    for n in SMALL:
        t = tree[n]
        t = t.reshape(*lead, -1, t.shape[-1])
        rows.append(jnp.pad(t, [(0, 0)] * (t.ndim - 1) + [(0, SMALL_W - t.shape[-1])]))
    buf = jnp.concatenate(rows, axis=-2)
    return jnp.pad(buf, [(0, 0)] * (buf.ndim - 2) + [(0, 16 - buf.shape[-2]), (0, 0)])


def _small_unrows(buf, like, lead):
    out, r = {}, 0
    for n in SMALL:
        shp = like[n].shape
        k = like[n].size // shp[-1]
        out[n] = buf[..., r:r + k, :shp[-1]].reshape(*lead, *shp)
        r += k
    return out


GROUPS = dict(hg=(("a_w_in", 0), ("a_w_out", 0)), f0=(("ffn_w_in", 0), ("ffn_w_out", 0)),
              mb=(("b_w_in", 0), ("b_w_out", 0)), f1=(("ffn_w_in", 1), ("ffn_w_out", 1)))


class _Comm:
    def __init__(self, local):
        x, y, c = _mesh_pos()
        self.me = 4 * x + 2 * y + c
        self.local = local
        self.shards = {g: [local[n][i].astype(BF16) for n, i in names] for g, names in GROUPS.items()}
        self.shards["f0"].append(_small_rows(local, ()))
        self.first = _all_gather(self.shards["hg"], "gather_hg")
        self.gathers, self.sent, token = {}, {}, None
        for g in ("f0", "mb", "f1"):
            self.gathers[g] = _copies_start(self.shards[g], gather=True, after=token, name=f"gather_{g}_start")
            token = self.gathers[g][-1]
        self.token = token

    def weights(self, group, after):
        if group == "hg":
            got = self.first
        else:
            lands = _copies_wait(self.gathers[group], gather=True, after=after, name=f"gather_{group}_wait")
            got = [lax.dynamic_update_index_in_dim(land, shard, self.me, 0)
                   for land, shard in zip(lands, self.shards[group])]
        out = dict(w_in=_from_parts(got[0], 1), w_out=_from_parts(got[1], 0))
        if group == "hg":
            out["token"] = self.token
        if group == "f0":
            small = _small_unrows(got[2], self.local, (N_DEV,))
            out["small"] = {n: _from_parts(small[n], SHARDED[n][0]) for n in SMALL}
        return out

    def send(self, group, grads):
        w_in = grads["w_in"]
        if isinstance(w_in, tuple):
            half = N_DEV // len(w_in)
            parts_in = jnp.concatenate([_to_parts(t, 1, half) for t in w_in], axis=0)
        else:
            parts_in = _to_parts(w_in, 1)
        parts = [parts_in, _to_parts(grads["w_out"], 0)]
        if "small" in grads:
            parts.append(_small_rows({n: _to_parts(grads["small"][n], SHARDED[n][0]) for n in SMALL}, (N_DEV,)))
        sent = [p.astype(BF16) for p in parts[:2]] + parts[2:] if group == "hg" else parts
        self.sent[group] = (parts, _copies_start(sent, gather=False, after=None, name=f"exchange_{group}_start"))
        return self.sent[group][1][-1]

    def finish(self, after, mom, var):
        res = {}
        for group in ("f1", "mb", "f0", "hg"):
            parts, started = self.sent[group]
            lands = _copies_wait(started, gather=False, after=after, name=f"exchange_{group}_wait")
            own = [lax.dynamic_index_in_dim(p, self.me, 0, keepdims=False) for p in parts]
            for (n, i), got, mine in zip(GROUPS[group], lands, own):
                res[n] = _adamw(got, self.local[n], mom[n], var[n], own=mine, layer=i, into=res.get(n),
                                name=f"adamw_{n}_{i}", tile=_pick(mine.shape[0], (256, 176, 128)))
                after = res[n][0]
        small = _adamw(lands[2], *[_small_rows(t, ()) for t in (self.local, mom, var)], own=own[2],
                       name="adamw_small", tile=16)
        return res, small


def _pack_small(tree, extra):
    flat = jnp.concatenate([tree[n].reshape(-1) for n in REPLICATED] + [extra.reshape(-1)])
    return _pad_rows(flat, 8)


def _unpack_small(pack, like):
    flat, out, off = pack.reshape(-1), {}, 0
    for n in REPLICATED:
        out[n] = flat[off:off + like[n].size].reshape(like[n].shape)
        off += like[n].size
    return out, flat[off]


def kernel(x, norm1_w, norm2_w, a_w_in, a_lb_logits, a_norm_w, a_w_out, b_w_in, b_conv_w, b_conv_b, b_dt_bias, b_a_log, b_d_skip, b_norm_w, b_w_out, ffn_w_in, ffn_conv_w, ffn_conv_b, ffn_w_out, final_norm_w, loss_target, m_norm1_w, m_norm2_w, m_a_w_in, m_a_lb_logits, m_a_norm_w, m_a_w_out, m_b_w_in, m_b_conv_w, m_b_conv_b, m_b_dt_bias, m_b_a_log, m_b_d_skip, m_b_norm_w, m_b_w_out, m_ffn_w_in, m_ffn_conv_w, m_ffn_conv_b, m_ffn_w_out, m_final_norm_w, v_norm1_w, v_norm2_w, v_a_w_in, v_a_lb_logits, v_a_norm_w, v_a_w_out, v_b_w_in, v_b_conv_w, v_b_conv_b, v_b_dt_bias, v_b_a_log, v_b_d_skip, v_b_norm_w, v_b_w_out, v_ffn_w_in, v_ffn_conv_w, v_ffn_conv_b, v_ffn_w_out, v_final_norm_w):
    given = dict(locals())
    local = {n: given[n] for n in WEIGHTS}
    mom = {n: given["m_" + n] for n in WEIGHTS}
    var = {n: given["v_" + n] for n in WEIGHTS}

    comm = _Comm(local)
    loss, grad_x, grads = _local_step(x[0], loss_target[0], local, comm)

    res, small_res = comm.finish(grad_x, mom, var)
    outs = ({}, {}, {}, {})
    for n in BIG:
        for out, r in zip(outs, res[n]):
            out[n] = r
    for out, r in zip(outs, small_res):
        out.update(_small_unrows(r, local, ()))
    out_g, out_d, out_m, out_v = outs

    small = _pack_small(grads, loss)
    rows = small.shape[0]
    got_s = _all_gather_small(small, "gather_small")
    zero = jnp.zeros((1,), F32)
    g, dlt, nm, nv = _adamw(got_s, _pack_small(local, zero), _pack_small(mom, zero), _pack_small(var, zero),
                            name="adamw_replicated", tile=rows)
    (rep_g, total), (rep_d, _), (rep_m, _), (rep_v, _) = (_unpack_small(t, local) for t in (g, dlt, nm, nv))
    out_g.update(rep_g)
    out_d.update(rep_d)
    out_m.update(rep_m)
    out_v.update(rep_v)

    return (total, grad_x[None], *[out_g[n] for n in WEIGHTS], *[out_d[n] for n in WEIGHTS],
            *[out_m[n] for n in WEIGHTS], *[out_v[n] for n in WEIGHTS])
```

```python
import functools

import jax
import jax.numpy as jnp
from jax import lax
from jax.experimental import pallas as pl
from jax.experimental.pallas import tpu as pltpu

F32 = jnp.float32
BF16 = jnp.bfloat16
HIGHEST = lax.Precision.HIGHEST
MESH = pl.DeviceIdType.MESH

N_DEV = 8
EPS = 1e-6
D_MODEL = 1024
HG_HEADS = 8
HG_HEAD_DIM = 128
D_INNER = 2048
SSM_HEAD_DIM = 64
SSM_GROUPS = 8
SSM_HPG = 4
SSM_STATE = 128
GN = SSM_GROUPS * SSM_STATE
CONV_DIM = D_INNER + 2 * GN
D_FF = 2816
SSM_CONV = 5
FFN_CONV = 3

ADAM_LR = 0.001
ADAM_B1 = 0.9
ADAM_B2 = 0.999
ADAM_EPS = 1e-08
ADAM_WD = 0.01
ADAM_STEP = 10

LANES = 128
ROW_TILE = 256
COL_TILE = 128
GLA_CHUNK = 16
GLA_BLOCK = 256
GLA_HEADS_PER_STEP = 8
SSD_CHUNK = 128
SSD_BLOCK = 512
PACK_W = 1024
VMEM_LIMIT = 56 * 1024 * 1024

NT_DIMS = (((1,), (1,)), ((), ()))
TN_DIMS = (((0,), (0,)), ((), ()))


def _cparams(*sem):
    return pltpu.CompilerParams(dimension_semantics=sem, vmem_limit_bytes=VMEM_LIMIT)


def _rms(x, w):
    return x * lax.rsqrt(jnp.mean(x * x, axis=-1, keepdims=True) + EPS) * w


def _row_kernel(body_fn, rows, params, row_outs, acc_outs, *, name, tile=ROW_TILE):
    L = rows[0][0].shape[0]
    tile = min(tile, L)
    n_in = len(rows) + len(params)
    n_ro = len(row_outs)

    def body(*refs):
        outs = body_fn(*[r[...] for r in refs[:n_in]])
        for ref, o in zip(refs[n_in:n_in + n_ro], outs[:n_ro]):
            ref[...] = o.astype(ref.dtype)
        first = pl.program_id(0) == 0
        for ref, o in zip(refs[n_in + n_ro:], outs[n_ro:]):
            @pl.when(first)
            def _(ref=ref):
                ref[...] = jnp.zeros(ref.shape, ref.dtype)
            ref[...] += o

    in_specs = [pl.BlockSpec((tile, w), lambda i, cb=cb: (i, cb)) for _, w, cb in rows]
    in_specs += [pl.BlockSpec(p.shape, lambda i: (0, 0)) for p in params]
    out_specs = [pl.BlockSpec((tile, w), lambda i: (i, 0)) for w, _ in row_outs]
    out_specs += [pl.BlockSpec(s, lambda i: (0, 0)) for s in acc_outs]
    out_shape = [jax.ShapeDtypeStruct((L, w), dt) for w, dt in row_outs]
    out_shape += [jax.ShapeDtypeStruct(s, F32) for s in acc_outs]
    return pl.pallas_call(
        body, name=name, grid=(L // tile,), in_specs=in_specs, out_specs=out_specs, out_shape=out_shape,
        compiler_params=_cparams("arbitrary" if acc_outs else "parallel"),
    )(*[a for a, _, _ in rows], *params)


def _col_kernel(body_fn, cols, params, col_outs, par_outs, *, name, n_tiles):
    L = cols[0][0].shape[0]
    n_in = len(cols) + len(params)
    width = n_tiles * COL_TILE

    def body(*refs):
        outs = body_fn(*[r[...] for r in refs[:n_in]])
        for ref, o in zip(refs[n_in:], outs):
            ref[...] = o.astype(ref.dtype)

    in_specs = [pl.BlockSpec((L, COL_TILE), lambda j, cb=cb: (0, cb + j)) for _, cb in cols]
    in_specs += [pl.BlockSpec((p.shape[0], COL_TILE), lambda j, cb=cb: (0, cb + j)) for p, cb in params]
    out_specs = [pl.BlockSpec((L, COL_TILE), lambda j: (0, j)) for _ in col_outs]
    out_specs += [pl.BlockSpec((k, COL_TILE), lambda j: (0, j)) for k in par_outs]
    out_shape = [jax.ShapeDtypeStruct((L, width), dt) for dt in col_outs]
    out_shape += [jax.ShapeDtypeStruct((k, width), F32) for k in par_outs]
    return pl.pallas_call(
        body, name=name, grid=(n_tiles,), in_specs=in_specs, out_specs=out_specs, out_shape=out_shape,
        compiler_params=_cparams("parallel"),
    )(*[a for a, _ in cols], *[p for p, _ in params])


def _pick(n, options):
    for t in options:
        if n % t == 0:
            return t
    return n


MATMUL_VMEM = 40 * 1024 * 1024


def _matmul_tiles(M, N, K, out_bytes):
    best = None
    for tm in (1024, 512, 256, 128, M):
        for tn in (1408, 1024, 512, 256, 128, N):
            if M % tm or N % tn:
                continue
            if 2 * (2 * K * (tm + tn) + out_bytes * tm * tn) > MATMUL_VMEM:
                continue
            if best is None or tm * tn > best[0] * best[1]:
                best = (tm, tn)
    return best


def _matmul(a, b, *, name, nt=False, ta=False, add=None, out_dtype=F32, after=None, b_cols=None):
    K, M = a.shape[::-1] if not ta else a.shape
    cb, width = b_cols if b_cols is not None else (0, b.shape[1])
    N = b.shape[0] if nt else width
    assert width == K or not nt
    tm, tn = _matmul_tiles(M, N, K, 4 * (1 + (add is not None)) if out_dtype == F32 else 2 + 4 * (add is not None))
    col0 = cb if nt else cb * (width // tn)

    def body(*refs):
        a_ref, b_ref = refs[0], refs[1]
        o_ref = refs[-1]
        dims = TN_DIMS if ta else NT_DIMS if nt else (((1,), (0,)), ((), ()))
        acc = lax.dot_general(a_ref[...], b_ref[...], dims, preferred_element_type=F32)
        if add is not None:
            acc = acc + refs[2][...]
        o_ref[...] = acc.astype(o_ref.dtype)

    in_specs = [pl.BlockSpec((K, tm), lambda i, j: (0, i)) if ta else pl.BlockSpec((tm, K), lambda i, j: (i, 0)),
                pl.BlockSpec((tn, K), lambda i, j: (j, col0)) if nt
                else pl.BlockSpec((K, tn), lambda i, j: (0, col0 + j))]
    args = [a, b]
    if add is not None:
        in_specs.append(pl.BlockSpec((tm, tn), lambda i, j: (i, j)))
        args.append(add)
    if after is not None:
        in_specs.append(pl.BlockSpec(memory_space=pl.ANY))
        args.append(after)
    return pl.pallas_call(
        body, name=name, grid=(M // tm, N // tn), in_specs=in_specs,
        out_specs=pl.BlockSpec((tm, tn), lambda i, j: (i, j)),
        out_shape=jax.ShapeDtypeStruct((M, N), out_dtype),
        compiler_params=_cparams("parallel", "parallel"),
    )(*args)


def _hgrn2_pre_fn(q_raw, ffw_raw, fbw_raw, lb_logits):
    lb = jax.nn.softmax(lb_logits, axis=0)[0:1]

    def gate(fr):
        f = lb + (1.0 - lb) * jax.nn.sigmoid(fr)
        return jnp.log(f), 1.0 - f

    lf_fw, k_fw = gate(ffw_raw)
    lf_bw, k_bw = gate(fbw_raw)
    return jax.nn.silu(q_raw), lf_fw, k_fw, lf_bw, k_bw


def _hgrn2_post_fn(o, g, norm_w):
    outs = []
    for h in range(HG_HEADS):
        sl = slice(h * HG_HEAD_DIM, (h + 1) * HG_HEAD_DIM)
        outs.append(_rms(o[:, sl], norm_w) * jax.nn.silu(g[:, sl]))
    return jnp.concatenate(outs, axis=1)


def _mamba_post_fn(y, z, norm_w):
    outs = []
    gw = D_INNER // SSM_GROUPS
    for gi in range(SSM_GROUPS):
        sl = slice(gi * gw, (gi + 1) * gw)
        outs.append(_rms(y[:, sl] * jax.nn.silu(z[:, sl]), norm_w[:, sl]))
    return jnp.concatenate(outs, axis=1)


def _shift_rows_impl(x, d):
    if d == 0:
        return x
    n, edge = x.shape[0], 8
    t = lax.broadcasted_iota(jnp.int32, (edge, x.shape[1]), 0)
    rolled = pltpu.roll(x, d % n, 0)
    if d > 0:
        return jnp.concatenate([jnp.where(t >= d, rolled[:edge], 0.0), rolled[edge:]], axis=0)
    return jnp.concatenate([rolled[:n - edge], jnp.where(t < edge + d, rolled[n - edge:], 0.0)], axis=0)


@functools.partial(jax.custom_vjp, nondiff_argnums=(1,))
def _shift_rows(x, d):
    return _shift_rows_impl(x, d)


_shift_rows.defvjp(lambda x, d: (_shift_rows_impl(x, d), None), lambda d, _, g: (_shift_rows_impl(g, -d),))


def _dwconv(x, w, b):
    taps = w.shape[0]
    c = (taps - 1) // 2
    y = b + w[0:1, :] * _shift_rows(x, c)
    for k in range(1, taps):
        y = y + w[k:k + 1, :] * _shift_rows(x, c - k)
    return y


def _ffn_mid_fn(gate, val, w, b):
    return jax.nn.silu(_dwconv(gate, w, b)) * val


def _mamba_conv_fn(xbc, w, b):
    return jax.nn.silu(_dwconv(xbc, w, b))


def _gla_consts(rev):
    return lax.broadcasted_iota(jnp.int32, (GLA_CHUNK, HG_HEAD_DIM), 0)


def _segment_cumsum(x, seg, rev):
    n = x.shape[0]
    t = lax.broadcasted_iota(jnp.int32, x.shape, 0) & (seg - 1)
    s = 1
    while s < seg:
        if rev:
            x = x + jnp.where(t < seg - s, pltpu.roll(x, n - s, 0), 0.0)
        else:
            x = x + jnp.where(t >= s, pltpu.roll(x, s, 0), 0.0)
        s *= 2
    return x


def _segment_cumsum_mxu(x, seg, rev):
    r = lax.broadcasted_iota(jnp.int32, (seg, seg), 0)
    c = lax.broadcasted_iota(jnp.int32, (seg, seg), 1)
    tri = ((r <= c) if rev else (r >= c)).astype(BF16)
    tri3 = jnp.concatenate([tri, tri, tri], axis=1)
    hi = x.astype(BF16)
    rest = x - hi.astype(F32)
    mid = rest.astype(BF16)
    lo = (rest - mid.astype(F32)).astype(BF16)
    outs = []
    for g in range(x.shape[0] // seg):
        rows = slice(g * seg, (g + 1) * seg)
        terms = jnp.concatenate([hi[rows], mid[rows], lo[rows]], axis=0)
        outs.append(jnp.dot(tri3, terms, preferred_element_type=F32))
    return jnp.concatenate(outs, axis=0)


@functools.partial(jax.custom_vjp, nondiff_argnums=(1, 2))
def _segment_cumsum_diff(x, seg, rev):
    return _segment_cumsum_mxu(x, seg, rev)


_segment_cumsum_diff.defvjp(lambda x, seg, rev: (_segment_cumsum_mxu(x, seg, rev), None),
                            lambda seg, rev, _, g: (_segment_cumsum_mxu(g, seg, not rev),))


def _gla_chunk(st, q, k, v, b, *, rev, consts):
    row = consts
    c = q.shape[0]
    o = lax.dot_general((q * jnp.exp(b)).astype(BF16), st.astype(BF16), NT_DIMS, preferred_element_type=F32)
    for s in range(c):
        e = jnp.exp(jnp.where((row <= s) if rev else (row >= s), b - b[s:s + 1], -jnp.inf))
        o = o + jnp.sum(q * (k[s:s + 1] * e), axis=-1, keepdims=True) * v[s:s + 1]
    b_end = b[0:1] if rev else b[c - 1:c]
    kd = (k * jnp.exp(b_end - b)).astype(BF16)
    st_new = st * jnp.exp(b_end) + lax.dot_general(v.astype(BF16), kd, TN_DIMS, preferred_element_type=F32)
    return st_new, o


def _gla_fwd(q, k, v, v_cb, g, *, rev, add, name):
    L = q.shape[0]
    blk = min(GLA_BLOCK, L)
    nblk, nsub = L // blk, blk // GLA_CHUNK
    hd, hps = HG_HEAD_DIM, GLA_HEADS_PER_STEP
    wide = hd * hps

    def body(*refs):
        q_ref, k_ref, v_ref, g_ref = refs[:4]
        add_ref = refs[4] if add is not None else None
        o_ref, st_out, st_scr, b_scr = refs[-4:]
        consts = _gla_consts(rev)

        @pl.when(pl.program_id(1) == 0)
        def _():
            st_scr[...] = jnp.zeros(st_scr.shape, F32)

        st_out[...] = st_scr[...]
        b_scr[...] = _segment_cumsum(g_ref[...], GLA_CHUNK, rev)

        def step(i, carry):
            sl = pl.ds(pl.multiple_of((nsub - 1 - i if rev else i) * GLA_CHUNK, GLA_CHUNK), GLA_CHUNK)
            lanes = [slice(hi * hd, (hi + 1) * hd) for hi in range(hps)]
            ins = [(st_scr[hi], q_ref[sl, ln], k_ref[sl, ln], v_ref[sl, ln], b_scr[sl, ln])
                   for hi, ln in enumerate(lanes)]
            adds = [add_ref[sl, ln] for ln in lanes] if add_ref is not None else None
            outs = [_gla_chunk(*args, rev=rev, consts=consts) for args in ins]
            for hi, ln in enumerate(lanes):
                st_scr[hi] = outs[hi][0]
                o_ref[sl, ln] = outs[hi][1] if adds is None else outs[hi][1] + adds[hi]
            return carry

        lax.fori_loop(0, nsub, step, 0)

    def pos(j):
        return nblk - 1 - j if rev else j

    spec = pl.BlockSpec((blk, wide), lambda h, j: (pos(j), h))
    in_specs = [spec, spec, pl.BlockSpec((blk, wide), lambda h, j: (pos(j), v_cb // hps + h)), spec]
    args = [q, k, v, g]
    if add is not None:
        in_specs.append(spec)
        args.append(add)
    return pl.pallas_call(
        body, name=name, grid=(HG_HEADS // hps, nblk), in_specs=in_specs,
        out_specs=[spec, pl.BlockSpec((hps, None, hd, hd), lambda h, j: (h, j, 0, 0))],
        out_shape=[jax.ShapeDtypeStruct((L, HG_HEADS * hd), F32),
                   jax.ShapeDtypeStruct((HG_HEADS, nblk, hd, hd), F32)],
        scratch_shapes=[pltpu.VMEM((hps, hd, hd), F32), pltpu.VMEM((blk, wide), F32)],
        compiler_params=_cparams("parallel", "arbitrary"),
    )(*args)


def _gla_bwd(q, k, v, v_cb, g, do, states, *, rev, adds, name):
    L = q.shape[0]
    blk = min(GLA_BLOCK, L)
    nblk, nsub = L // blk, blk // GLA_CHUNK
    hd, hps = HG_HEAD_DIM, GLA_HEADS_PER_STEP
    wide = hd * hps
    n_add = 0 if adds is None else 2

    def body(*refs):
        q_ref, k_ref, v_ref, g_ref, do_ref, st_in = refs[:6]
        add_refs = refs[6:6 + n_add]
        dq_ref, dk_ref, dv_ref, dg_ref, dst_scr, sub_scr, b_scr, db_scr = refs[6 + n_add:]
        consts = _gla_consts(rev)
        chunk = functools.partial(_gla_chunk, rev=rev, consts=consts)

        @pl.when(pl.program_id(1) == 0)
        def _():
            dst_scr[...] = jnp.zeros(dst_scr.shape, F32)

        b_scr[...] = _segment_cumsum(g_ref[...], GLA_CHUNK, rev)

        def rows(i):
            return pl.ds(pl.multiple_of((nsub - 1 - i if rev else i) * GLA_CHUNK, GLA_CHUNK), GLA_CHUNK)

        for hi in range(hps):
            sub_scr[hi] = st_in[hi]

        def replay(i, carry):
            sl = rows(i)
            lanes = [slice(hi * hd, (hi + 1) * hd) for hi in range(hps)]
            ins = [(sub_scr[i * hps + hi], q_ref[sl, ln], k_ref[sl, ln], v_ref[sl, ln], b_scr[sl, ln])
                   for hi, ln in enumerate(lanes)]
            outs = [chunk(*args)[0] for args in ins]
            for hi in range(hps):
                sub_scr[(i + 1) * hps + hi] = outs[hi]
            return carry

        lax.fori_loop(0, nsub - 1, replay, 0)

        def step(n, carry):
            i = nsub - 1 - n
            sl = rows(i)
            lanes = [slice(hi * hd, (hi + 1) * hd) for hi in range(hps)]
            ins = [(sub_scr[i * hps + hi], q_ref[sl, ln], k_ref[sl, ln], v_ref[sl, ln], b_scr[sl, ln])
                   for hi, ln in enumerate(lanes)]
            cts = [(dst_scr[hi], do_ref[sl, ln]) for hi, ln in enumerate(lanes)]
            adds = [(add_refs[0][sl, ln], add_refs[1][sl, ln]) for ln in lanes] if n_add else None
            outs = [jax.vjp(chunk, *args)[1](ct) for args, ct in zip(ins, cts)]
            for hi, ln in enumerate(lanes):
                dst, dq, dk, dv, db = outs[hi]
                dst_scr[hi] = dst
                dq_ref[sl, ln] = dq if adds is None else dq + adds[hi][0]
                dk_ref[sl, ln] = dk
                dv_ref[sl, ln] = dv if adds is None else dv + adds[hi][1]
                db_scr[sl, ln] = db
            return carry

        lax.fori_loop(0, nsub, step, 0)
        dg_ref[...] = _segment_cumsum(db_scr[...], GLA_CHUNK, not rev)

    def pos(j):
        p = nblk - 1 - j
        return nblk - 1 - p if rev else p

    spec = pl.BlockSpec((blk, wide), lambda h, j: (pos(j), h))
    in_specs = [spec, spec, pl.BlockSpec((blk, wide), lambda h, j: (pos(j), v_cb // hps + h)), spec, spec,
                pl.BlockSpec((hps, None, hd, hd), lambda h, j: (h, nblk - 1 - j, 0, 0))]
    args = [q, k, v, g, do, states]
    if adds is not None:
        in_specs += [spec, spec]
        args += list(adds)
    out = jax.ShapeDtypeStruct((L, HG_HEADS * hd), F32)
    return pl.pallas_call(
        body, name=name, grid=(HG_HEADS // hps, nblk), in_specs=in_specs,
        out_specs=[spec] * 4, out_shape=[out] * 4,
        scratch_shapes=[pltpu.VMEM((hps, hd, hd), F32), pltpu.VMEM((nsub * hps, hd, hd), F32),
                        pltpu.VMEM((blk, wide), F32), pltpu.VMEM((blk, wide), F32)],
        compiler_params=_cparams("parallel", "arbitrary"),
    )(*args)


def _ssd_consts(rev):
    c = SSD_CHUNK
    gw = SSM_HPG * SSM_HEAD_DIM
    r2 = lax.broadcasted_iota(jnp.int32, (c, c), 0)
    c2 = lax.broadcasted_iota(jnp.int32, (c, c), 1)
    low = (r2 <= c2) if rev else (r2 >= c2)
    lane = lax.broadcasted_iota(jnp.int32, (1, gw), 1)
    return low, r2 == c2, lane, lane >> 6


def _expand_heads(v4, head_of_lane):
    first = head_of_lane[:, :LANES] == 0
    cols = [v4[:, j:j + 1] for j in range(SSM_HPG)]
    return jnp.concatenate([jnp.where(first, cols[0], cols[1]), jnp.where(first, cols[2], cols[3])], axis=1)


def _ssd_prep(dtr, bias, alog, dsk, *, rev, consts):
    head_of_lane = consts[-1]
    dt_l = _expand_heads(jax.nn.softplus(dtr + bias), head_of_lane)
    a_l = _expand_heads(-jnp.exp(alog), head_of_lane)
    return dt_l, _segment_cumsum_diff(dt_l * a_l, SSD_CHUNK, rev), _expand_heads(dsk, head_of_lane)


def _ssd_chunk(st, x, bm, cm, dt_l, acum, dsk_l, *, rev, consts, skip):
    low, eye, lane, head_of_lane = consts
    c = x.shape[0]
    xd = x * dt_l
    cb = lax.dot_general(cm.astype(BF16), bm.astype(BF16), NT_DIMS, preferred_element_type=F32)
    y = jnp.dot(cm.astype(BF16), st.astype(BF16), preferred_element_type=F32) * jnp.exp(acum)
    for j in range(SSM_HPG):
        acol = jnp.sum(jnp.where(lane == j * SSM_HEAD_DIM, acum, 0.0), axis=-1, keepdims=True)
        ab = jnp.broadcast_to(acol, (c, c))
        arow = jnp.sum(jnp.where(eye, ab, 0.0), axis=0, keepdims=True)
        lmat = jnp.exp(jnp.where(low, ab - arow, -jnp.inf))
        xj = jnp.where(head_of_lane == j, xd, 0.0)
        y = y + jnp.dot((cb * lmat).astype(BF16), xj.astype(BF16), preferred_element_type=F32)
    a_end = acum[0:1] if rev else acum[c - 1:c]
    xdec = (xd * jnp.exp(a_end - acum)).astype(BF16)
    st_new = st * jnp.exp(a_end) + lax.dot_general(bm.astype(BF16), xdec, TN_DIMS, preferred_element_type=F32)
    if skip:
        y = y + x * dsk_l
    return st_new, y


def _ssd_specs(L, rev):
    blk = min(SSD_BLOCK, L)
    nblk = L // blk
    gw = SSM_HPG * SSM_HEAD_DIM

    def pos(j):
        return nblk - 1 - j if rev else j

    return blk, nblk, gw, pos


def _ssd_fwd(xbc, dtr, bias, alog, dsk, *, rev, skip, add, name):
    L = xbc.shape[0]
    blk, nblk, gw, pos = _ssd_specs(L, rev)
    nsub = blk // SSD_CHUNK
    n = SSM_STATE

    def body(*refs):
        x_ref, b_ref, c_ref, dt_ref, bias_ref, alog_ref, dsk_ref = refs[:7]
        add_ref = refs[7] if add is not None else None
        y_ref, st_out, st_scr, dt_scr, ac_scr = refs[-5:]
        consts = _ssd_consts(rev)

        @pl.when(pl.program_id(1) == 0)
        def _():
            st_scr[...] = jnp.zeros(st_scr.shape, F32)

        dt_scr[...], ac_scr[...], dsk_l = _ssd_prep(dt_ref[...], bias_ref[...], alog_ref[...], dsk_ref[...],
                                                   rev=rev, consts=consts)

        def step(i, carry):
            sl = pl.ds(pl.multiple_of((nsub - 1 - i if rev else i) * SSD_CHUNK, SSD_CHUNK), SSD_CHUNK)
            st = st_scr[...]
            st_out[i] = st
            st_new, y = _ssd_chunk(st, x_ref[sl, :], b_ref[sl, :], c_ref[sl, :], dt_scr[sl, :],
                                   ac_scr[sl, :], dsk_l, rev=rev, consts=consts, skip=skip)
            st_scr[...] = st_new
            if add_ref is not None:
                y = y + add_ref[sl, :]
            y_ref[sl, :] = y
            return carry

        lax.fori_loop(0, nsub, step, 0)

    b0 = D_INNER // n
    yspec = pl.BlockSpec((blk, gw), lambda g, j: (pos(j), g))
    pspec = pl.BlockSpec((None, 1, SSM_HPG), lambda g, j: (g, 0, 0))
    in_specs = [yspec,
                pl.BlockSpec((blk, n), lambda g, j: (pos(j), b0 + g)),
                pl.BlockSpec((blk, n), lambda g, j: (pos(j), b0 + SSM_GROUPS + g)),
                pl.BlockSpec((None, blk, SSM_HPG), lambda g, j: (g, pos(j), 0)),
                pspec, pspec, pspec]
    args = [xbc, xbc, xbc, dtr, bias, alog, dsk]
    if add is not None:
        in_specs.append(yspec)
        args.append(add)
    return pl.pallas_call(
        body, name=name, grid=(SSM_GROUPS, nblk), in_specs=in_specs,
        out_specs=[yspec, pl.BlockSpec((None, None, nsub, n, gw), lambda g, j: (g, j, 0, 0, 0))],
        out_shape=[jax.ShapeDtypeStruct((L, D_INNER), F32),
                   jax.ShapeDtypeStruct((SSM_GROUPS, nblk, nsub, n, gw), F32)],
        scratch_shapes=[pltpu.VMEM((n, gw), F32), pltpu.VMEM((blk, gw), F32), pltpu.VMEM((blk, gw), F32)],
        compiler_params=_cparams("parallel", "arbitrary"),
    )(*args)


def _ssd_bwd(xbc, dtr, bias, alog, dsk, dy, states, *, rev, skip, adds, name):
    L = xbc.shape[0]
    blk, nblk, gw, _ = _ssd_specs(L, rev)
    nsub = blk // SSD_CHUNK
    n = SSM_STATE
    n_add = 0 if adds is None else 3

    def body(*refs):
        x_ref, b_ref, c_ref, dt_ref, bias_ref, alog_ref, dsk_ref, dy_ref, st_in = refs[:9]
        add_refs = refs[9:9 + n_add]
        outs = refs[9 + n_add:]
        dx_ref, db_ref, dc_ref, ddt_ref, dbias_ref, dalog_ref, ddsk_ref = outs[:7]
        dst_scr, dt_scr, ac_scr, ddt_scr, dac_scr = outs[7:]
        consts = _ssd_consts(rev)
        chunk = functools.partial(_ssd_chunk, rev=rev, consts=consts, skip=skip)
        prep = functools.partial(_ssd_prep, rev=rev, consts=consts)

        @pl.when(pl.program_id(1) == 0)
        def _():
            dst_scr[...] = jnp.zeros(dst_scr.shape, F32)
            dbias_ref[...] = jnp.zeros(dbias_ref.shape, F32)
            dalog_ref[...] = jnp.zeros(dalog_ref.shape, F32)
            ddsk_ref[...] = jnp.zeros(ddsk_ref.shape, F32)

        narrow = (dt_ref[...], bias_ref[...], alog_ref[...], dsk_ref[...])
        (dt_scr[...], ac_scr[...], dsk_l), prep_vjp = jax.vjp(prep, *narrow)

        def rows(i):
            return pl.ds(pl.multiple_of((nsub - 1 - i if rev else i) * SSD_CHUNK, SSD_CHUNK), SSD_CHUNK)

        def operands(i):
            sl = rows(i)
            return (x_ref[sl, :], b_ref[sl, :], c_ref[sl, :], dt_scr[sl, :], ac_scr[sl, :], dsk_l)

        def step(k, ddsk_l):
            i = nsub - 1 - k
            sl = rows(i)
            _, vjp = jax.vjp(chunk, st_in[i], *operands(i))
            dst, dx, db, dc, ddt_l, dac, ddsk_k = vjp((dst_scr[...], dy_ref[sl, :]))
            dst_scr[...] = dst
            if n_add:
                dx = dx + add_refs[0][sl, :]
                db = db + add_refs[1][sl, :]
                dc = dc + add_refs[2][sl, :]
            dx_ref[sl, :] = dx
            db_ref[sl, :] = db
            dc_ref[sl, :] = dc
            ddt_scr[sl, :] = ddt_l
            dac_scr[sl, :] = dac
            return ddsk_l + ddsk_k

        ddsk_l = lax.fori_loop(0, nsub, step, jnp.zeros((1, gw), F32))
        ddt, dbias, dalog, ddsk = prep_vjp((ddt_scr[...], dac_scr[...], ddsk_l))
        ddt_ref[...] = ddt
        dbias_ref[...] += dbias
        dalog_ref[...] += dalog
        ddsk_ref[...] += ddsk

    def pos(j):
        p = nblk - 1 - j
        return nblk - 1 - p if rev else p

    b0 = D_INNER // n
    xspec = pl.BlockSpec((blk, gw), lambda g, j: (pos(j), g))
    nspec = pl.BlockSpec((blk, n), lambda g, j: (pos(j), g))
    dtspec = pl.BlockSpec((None, blk, SSM_HPG), lambda g, j: (g, pos(j), 0))
    pspec = pl.BlockSpec((None, 1, SSM_HPG), lambda g, j: (g, 0, 0))
    in_specs = [xspec,
                pl.BlockSpec((blk, n), lambda g, j: (pos(j), b0 + g)),
                pl.BlockSpec((blk, n), lambda g, j: (pos(j), b0 + SSM_GROUPS + g)),
                dtspec, pspec, pspec, pspec, xspec,
                pl.BlockSpec((None, None, nsub, n, gw), lambda g, j: (g, nblk - 1 - j, 0, 0, 0))]
    args = [xbc, xbc, xbc, dtr, bias, alog, dsk, dy, states]
    if adds is not None:
        in_specs += [xspec, nspec, nspec]
        args += list(adds)
    par = jax.ShapeDtypeStruct((SSM_GROUPS, 1, SSM_HPG), F32)
    return pl.pallas_call(
        body, name=name, grid=(SSM_GROUPS, nblk), in_specs=in_specs,
        out_specs=[xspec, nspec, nspec, dtspec, pspec, pspec, pspec],
        out_shape=[jax.ShapeDtypeStruct((L, D_INNER), F32), jax.ShapeDtypeStruct((L, GN), F32),
                   jax.ShapeDtypeStruct((L, GN), F32), jax.ShapeDtypeStruct((SSM_GROUPS, L, SSM_HPG), F32),
                   par, par, par],
        scratch_shapes=[pltpu.VMEM((n, gw), F32)] + [pltpu.VMEM((blk, gw), F32)] * 4,
        compiler_params=_cparams("parallel", "arbitrary"),
    )(*args)


def _norm_fwd(h, w, name):
    d = h.shape[1]
    return _row_kernel(lambda hv, wv: (_rms(hv, wv),), [(h, d, 0)], [w], [(d, BF16)], [], name=name)[0]


def _matmul_norm_bwd(a, b, h, w, dh_in, *, name, add=None, b_cols=None, after=None):
    M, K = a.shape
    N = b.shape[0]
    cb = 0 if b_cols is None else b_cols[0]
    tm = 512 if K <= 4096 else 256

    def body(*refs):
        a_ref, b_ref, h_ref, w_ref, dh_ref = refs[:5]
        o32_ref, o16_ref, dw_ref = refs[-3:]
        du = lax.dot_general(a_ref[...], b_ref[...], NT_DIMS, preferred_element_type=F32)
        if add is not None:
            du = du + refs[5][...]
        _, vjp = jax.vjp(_rms, h_ref[...], w_ref[...])
        dh, dw = vjp(du)
        dh = dh + dh_ref[...]
        o32_ref[...] = dh
        o16_ref[...] = dh.astype(BF16)

        @pl.when(pl.program_id(0) == 0)
        def _():
            dw_ref[...] = jnp.zeros(dw_ref.shape, F32)

        dw_ref[...] += dw

    rows = pl.BlockSpec((tm, N), lambda i: (i, 0))
    in_specs = [pl.BlockSpec((tm, K), lambda i: (i, 0)), pl.BlockSpec((N, K), lambda i: (0, cb)), rows,
                pl.BlockSpec((1, N), lambda i: (0, 0)), rows]
    args = [a, b, h, w, dh_in]
    if add is not None:
        in_specs.append(rows)
        args.append(add)
    if after is not None:
        in_specs.append(pl.BlockSpec(memory_space=pl.ANY))
        args.append(after)
    return pl.pallas_call(
        body, name=name, grid=(M // tm,), in_specs=in_specs,
        out_specs=[rows, rows, pl.BlockSpec((1, N), lambda i: (0, 0))],
        out_shape=[jax.ShapeDtypeStruct((M, N), F32), jax.ShapeDtypeStruct((M, N), BF16),
                   jax.ShapeDtypeStruct((1, N), F32)],
        compiler_params=_cparams("arbitrary"),
    )(*args)


def _ffn_fwd(h, norm_w, w_in, w_out, conv_w, conv_b, tag):
    u = _norm_fwd(h, norm_w, f"{tag}_norm")
    pg = _matmul(u, w_in, b_cols=(0, D_FF), name=f"{tag}_in_gate")
    pv = _matmul(u, w_in, b_cols=(1, D_FF), name=f"{tag}_in_val")
    y = _col_kernel(lambda g, v, w, b: (_ffn_mid_fn(g, v, w, b),), [(pg, 0), (pv, 0)], [(conv_w, 0), (conv_b, 0)],
                    [BF16], [], name=f"{tag}_mid", n_tiles=D_FF // COL_TILE)[0]
    h_out = _matmul(y, w_out, add=h, name=f"{tag}_out")
    return h_out, (u, pg, pv, y)


def _ffn_bwd(h, dh, dh16, saved, norm_w, w_in, w_out, conv_w, conv_b, tag, after=None):
    u, pg, pv, y = saved
    dy = _matmul(dh16, w_out, nt=True, after=after, name=f"{tag}_out_dx")
    dw_out = _matmul(y, dh16, ta=True, name=f"{tag}_out_dw")

    def fn(g, v, ct, w, b):
        _, vjp = jax.vjp(_ffn_mid_fn, g, v, w, b)
        return vjp(ct)

    dpg, dpv, dcw, dcb = _col_kernel(fn, [(pg, 0), (pv, 0), (dy, 0)], [(conv_w, 0), (conv_b, 0)], [BF16, BF16],
                                     [FFN_CONV, 1], name=f"{tag}_mid_bwd", n_tiles=D_FF // COL_TILE)
    du = _matmul(dpg, w_in, nt=True, b_cols=(0, D_FF), name=f"{tag}_gate_dx")
    dw_gate = _matmul(u, dpg, ta=True, name=f"{tag}_gate_dw")
    dw_val = _matmul(u, dpv, ta=True, name=f"{tag}_val_dw")
    dh, dh16, dnw = _matmul_norm_bwd(dpv, w_in, h, norm_w, dh, add=du, b_cols=(1, D_FF), name=f"{tag}_val_dx_norm")
    return dh, dh16, dict(w_in=(dw_gate, dw_val), w_out=dw_out, conv_w=dcw, conv_b=dcb, norm=dnw)


def _hgrn2_fwd(h, norm_w, w_in, lb_logits, a_norm_w, w_out, after=None):
    d = D_MODEL
    u = _norm_fwd(h, norm_w, "hg_norm")
    pa = _matmul(u, w_in, after=after, name="hg_in")
    qs, lf_fw, k_fw, lf_bw, k_bw = _row_kernel(
        _hgrn2_pre_fn, [(pa, d, 0), (pa, d, 1), (pa, d, 2)], [lb_logits], [(d, F32)] * 5, [], name="hg_pre")
    o_fw, st_fw = _gla_fwd(qs, k_fw, pa, 3 * HG_HEADS, lf_fw, rev=False, add=None, name="hg_gla_fw")
    o, st_bw = _gla_fwd(qs, k_bw, pa, 3 * HG_HEADS, lf_bw, rev=True, add=o_fw, name="hg_gla_bw")
    y = _row_kernel(lambda ov, gv, wv: (_hgrn2_post_fn(ov, gv, wv),), [(o, d, 0), (pa, d, 4)], [a_norm_w],
                    [(d, BF16)], [], name="hg_post")[0]
    h_out = _matmul(y, w_out, add=h, name="hg_out")
    return h_out, (u, pa, qs, lf_fw, k_fw, lf_bw, k_bw, st_fw, st_bw, o, y)


def _hgrn2_bwd(h, dh, dh16, saved, norm_w, w_in, lb_logits, a_norm_w, w_out, send, after=None):
    d = D_MODEL
    u, pa, qs, lf_fw, k_fw, lf_bw, k_bw, st_fw, st_bw, o, y = saved
    dy = _matmul(dh16, w_out, nt=True, after=after, name="hg_out_dx")
    dw_out = _matmul(y, dh16, ta=True, name="hg_out_dw")

    def post_bwd(ov, gv, ct, wv):
        _, vjp = jax.vjp(_hgrn2_post_fn, ov, gv, wv)
        return vjp(ct)

    do, dg, dnw = _row_kernel(post_bwd, [(o, d, 0), (pa, d, 4), (dy, d, 0)], [a_norm_w], [(d, F32), (d, F32)],
                              [(1, HG_HEAD_DIM)], name="hg_post_bwd")
    dq1, dk_fw, dv1, dlf_fw = _gla_bwd(qs, k_fw, pa, 3 * HG_HEADS, lf_fw, do, st_fw, rev=False, adds=None,
                                       name="hg_gla_fw_bwd")
    dqs, dk_bw, dv, dlf_bw = _gla_bwd(qs, k_bw, pa, 3 * HG_HEADS, lf_bw, do, st_bw, rev=True, adds=(dq1, dv1),
                                      name="hg_gla_bw_bwd")

    def pre_bwd(qr, fr, br, c0, c1, c2, c3, c4, dvv, dgv, lbl):
        _, vjp = jax.vjp(_hgrn2_pre_fn, qr, fr, br, lbl)
        dq, df, db, dlbl = vjp((c0, c1, c2, c3, c4))
        return jnp.concatenate([dq, df, db, dvv, dgv], axis=1), dlbl

    rows = [(pa, d, 0), (pa, d, 1), (pa, d, 2), (dqs, d, 0), (dlf_fw, d, 0), (dk_fw, d, 0), (dlf_bw, d, 0),
            (dk_bw, d, 0), (dv, d, 0), (dg, d, 0)]
    dpa, dlbl = _row_kernel(pre_bwd, rows, [lb_logits], [(5 * d, BF16)], [lb_logits.shape], name="hg_pre_bwd")
    dw_in = _matmul(u, dpa, ta=True, name="hg_in_dw")
    token = send(dw_in, dw_out)
    dh, dh16, dn1 = _matmul_norm_bwd(dpa, w_in, h, norm_w, dh, after=token, name="hg_in_dx_norm")
    return dh, dh16, dict(lb=dlbl, a_norm=dnw, norm=dn1)


def _group_params(p):
    return p.reshape(SSM_GROUPS, 1, SSM_HPG)


def _mamba_fwd(h, norm_w, w_z, w_xbc, w_dt, conv_w, conv_b, dt_bias, a_log, d_skip, b_norm_w, w_out):
    L = h.shape[0]
    u = _norm_fwd(h, norm_w, "mb_norm")
    z = _matmul(u, w_z, name="mb_in_z")
    xbc_raw = _matmul(u, w_xbc, name="mb_in_xbc")
    dt_raw = _matmul(u, w_dt, name="mb_in_dt")
    xbc = _col_kernel(lambda xv, w, b: (_mamba_conv_fn(xv, w, b),), [(xbc_raw, 0)], [(conv_w, 0), (conv_b, 0)],
                      [F32], [], name="mb_conv", n_tiles=CONV_DIM // COL_TILE)[0]
    dtr = dt_raw.reshape(L, 2, SSM_GROUPS, SSM_HPG).transpose(1, 2, 0, 3)
    bias, alog = dt_bias.reshape(2, -1), a_log.reshape(2, -1)
    dsk = _group_params(d_skip.reshape(-1))
    y_fw, st_fw = _ssd_fwd(xbc, dtr[0], _group_params(bias[0]), _group_params(alog[0]), dsk, rev=False, skip=True,
                           add=None, name="mb_ssd_fw")
    ysum, st_bw = _ssd_fwd(xbc, dtr[1], _group_params(bias[1]), _group_params(alog[1]), dsk, rev=True, skip=False,
                           add=y_fw, name="mb_ssd_bw")
    y = _row_kernel(lambda yv, zv, wv: (_mamba_post_fn(yv, zv, wv),), [(ysum, D_INNER, 0), (z, D_INNER, 0)],
                    [b_norm_w], [(D_INNER, BF16)], [], name="mb_post")[0]
    h_out = _matmul(y, w_out, add=h, name="mb_out")
    return h_out, (u, z, xbc_raw, xbc, dtr, st_fw, st_bw, ysum, y)


def _mamba_bwd(h, dh, dh16, saved, norm_w, w_z, w_xbc, w_dt, conv_w, conv_b, dt_bias, a_log, d_skip, b_norm_w, w_out,
               after=None):
    L = h.shape[0]
    u, z, xbc_raw, xbc, dtr, st_fw, st_bw, ysum, y = saved
    dy = _matmul(dh16, w_out, nt=True, after=after, name="mb_out_dx")
    dw_out = _matmul(y, dh16, ta=True, name="mb_out_dw")

    def post_bwd(yv, zv, ct, wv):
        _, vjp = jax.vjp(_mamba_post_fn, yv, zv, wv)
        return vjp(ct)

    dys, dz, dbn = _row_kernel(post_bwd, [(ysum, D_INNER, 0), (z, D_INNER, 0), (dy, D_INNER, 0)], [b_norm_w],
                               [(D_INNER, F32), (D_INNER, BF16)], [(1, D_INNER)], name="mb_post_bwd")
    bias, alog = dt_bias.reshape(2, -1), a_log.reshape(2, -1)
    dsk = _group_params(d_skip.reshape(-1))
    dx1, db1, dc1, ddt_fw, dbias_fw, dalog_fw, ddsk = _ssd_bwd(
        xbc, dtr[0], _group_params(bias[0]), _group_params(alog[0]), dsk, dys, st_fw, rev=False, skip=True,
        adds=None, name="mb_ssd_fw_bwd")
    dx, db, dc, ddt_bw, dbias_bw, dalog_bw, _ = _ssd_bwd(
        xbc, dtr[1], _group_params(bias[1]), _group_params(alog[1]), dsk, dys, st_bw, rev=True, skip=False,
        adds=(dx1, db1, dc1), name="mb_ssd_bw_bwd")

    def conv_bwd(n_tiles, ct, first):
        def fn(xv, ctv, w, b):
            _, vjp = jax.vjp(_mamba_conv_fn, xv, w, b)
            return vjp(ctv)
        return _col_kernel(fn, [(xbc_raw, first), (ct, 0)], [(conv_w, first), (conv_b, first)], [BF16],
                           [SSM_CONV, 1], name=f"mb_conv_bwd_{first}", n_tiles=n_tiles)

    nx, nb = D_INNER // COL_TILE, GN // COL_TILE
    parts = [conv_bwd(nx, dx, 0), conv_bwd(nb, db, nx), conv_bwd(nb, dc, nx + nb)]
    dxbc = jnp.concatenate([p[0] for p in parts], axis=1)
    dcw = jnp.concatenate([p[1] for p in parts], axis=1)
    dcb = jnp.concatenate([p[2] for p in parts], axis=1)
    ddt = jnp.stack([ddt_fw, ddt_bw]).transpose(2, 0, 1, 3).reshape(L, 2 * SSM_GROUPS * SSM_HPG).astype(BF16)
    du = _matmul(dz, w_z, nt=True, name="mb_z_dx")
    du = _matmul(dxbc, w_xbc, nt=True, add=du, name="mb_xbc_dx")
    dw_in = jnp.concatenate([_matmul(u, dz, ta=True, name="mb_z_dw"), _matmul(u, dxbc, ta=True, name="mb_xbc_dw"),
                             _matmul(u, ddt, ta=True, name="mb_dt_dw")], axis=1)
    dh, dh16, dn1 = _matmul_norm_bwd(ddt, w_dt, h, norm_w, dh, add=du, name="mb_dt_dx_norm")
    grads = dict(w_in=dw_in, w_out=dw_out, conv_w=dcw, conv_b=dcb, b_norm=dbn, norm=dn1,
                 dt_bias=jnp.stack([dbias_fw, dbias_bw]).reshape(1, 2, -1),
                 a_log=jnp.stack([dalog_fw, dalog_bw]).reshape(1, 2, -1), d_skip=ddsk.reshape(1, -1))
    return dh, dh16, grads


def _loss_head(h, target, w):
    d = h.shape[1]

    def fn(hv, tv, wv):
        def loss(hv, wv):
            err = _rms(hv, wv) - tv
            return 0.5 * jnp.sum(jnp.mean(err * err, axis=-1, keepdims=True), axis=0, keepdims=True)
        val, vjp = jax.vjp(loss, hv, wv)
        dh, dw = vjp(jnp.ones((1, 1), F32))
        return dh, dh, val, dw

    return _row_kernel(fn, [(h, d, 0), (target, d, 0)], [w], [(d, F32), (d, BF16)], [(1, 1), (1, d)], name="loss_head")


def _local_step(x, target, rep, comm):
    def ffn_args(i, w, small):
        return (rep["norm2_w"][i:i + 1], w["w_in"], w["w_out"], small["ffn_conv_w"][i], rep["ffn_conv_b"][i:i + 1])

    def mamba_args(w, small):
        w_in = w["w_in"]
        return (rep["norm1_w"][1:2], w_in[:, :D_INNER], w_in[:, D_INNER:D_INNER + CONV_DIM],
                w_in[:, D_INNER + CONV_DIM:], small["b_conv_w"][0], small["b_conv_b"], rep["b_dt_bias"],
                rep["b_a_log"], rep["b_d_skip"], small["b_norm_w"], w["w_out"])

    w_hg = comm.weights("hg", None)
    hg = (rep["norm1_w"][0:1], w_hg["w_in"], rep["a_lb_logits"], rep["a_norm_w"], w_hg["w_out"])
    h0 = x
    h1, s_hg = _hgrn2_fwd(h0, *hg, after=w_hg.get("token"))
    w_f0 = comm.weights("f0", h1)
    small = w_f0["small"]
    f0 = ffn_args(0, w_f0, small)
    h2, s_f0 = _ffn_fwd(h1, *f0, "ffn0")
    mb = mamba_args(comm.weights("mb", h2), small)
    h3, s_mb = _mamba_fwd(h2, *mb)
    f1 = ffn_args(1, comm.weights("f1", h3), small)
    h4, s_f1 = _ffn_fwd(h3, *f1, "ffn1")
    dh, dh16, loss, d_final = _loss_head(h4, target, rep["final_norm_w"].reshape(1, -1))

    dh, dh16, g_f1 = _ffn_bwd(h3, dh, dh16, s_f1, *f1, "ffn1")
    token = comm.send("f1", dict(w_in=g_f1["w_in"], w_out=g_f1["w_out"]))
    dh, dh16, g_mb = _mamba_bwd(h2, dh, dh16, s_mb, *mb, after=token)
    token = comm.send("mb", dict(w_in=g_mb["w_in"], w_out=g_mb["w_out"]))
    dh, dh16, g_f0 = _ffn_bwd(h1, dh, dh16, s_f0, *f0, "ffn0", after=token)
    token = comm.send("f0", dict(w_in=g_f0["w_in"], w_out=g_f0["w_out"]))
    small_grads = dict(b_conv_w=g_mb["conv_w"][None], b_conv_b=g_mb["conv_b"], b_norm_w=g_mb["b_norm"],
                       ffn_conv_w=jnp.stack([g_f0["conv_w"], g_f1["conv_w"]]))
    dh, dh16, g_hg = _hgrn2_bwd(
        h0, dh, dh16, s_hg, *hg, after=token,
        send=lambda dw_in, dw_out: comm.send("hg", dict(w_in=dw_in, w_out=dw_out, small=small_grads)))
    grads = dict(
        norm1_w=jnp.concatenate([g_hg["norm"], g_mb["norm"]], axis=0),
        norm2_w=jnp.concatenate([g_f0["norm"], g_f1["norm"]], axis=0),
        a_lb_logits=g_hg["lb"], a_norm_w=g_hg["a_norm"], b_dt_bias=g_mb["dt_bias"], b_a_log=g_mb["a_log"],
        b_d_skip=g_mb["d_skip"], ffn_conv_b=jnp.concatenate([g_f0["conv_b"], g_f1["conv_b"]], axis=0),
        final_norm_w=d_final.reshape(-1),
    )
    return loss, dh, grads


def _mesh_pos():
    return lax.axis_index("x"), lax.axis_index("y"), lax.axis_index("c")


N_COPIES = N_DEV - 1


def _comm_call(body, ins, out_shape, name):
    n = len(ins)
    hbm = pl.BlockSpec(memory_space=pl.ANY)
    return pl.pallas_call(
        body, name=name, out_shape=out_shape, in_specs=[hbm] * n, out_specs=[hbm] * n,
        scratch_shapes=[pltpu.SemaphoreType.DMA((n * N_COPIES,)), pltpu.SemaphoreType.DMA((n * N_COPIES,)),
                        pltpu.SemaphoreType.DMA((n,))],
    )(*ins)


def _all_gather(shards, name):
    n = len(shards)

    def body(*refs):
        x_refs, out_refs = refs[:n], refs[n:2 * n]
        send_sems, recv_sems, local_sems = refs[2 * n:]
        x, y, c = _mesh_pos()
        me, sibling = (x, y, c), (x, y, 1 - c)
        chips = [(1 - x, y), (x, 1 - y), (1 - x, 1 - y)]

        def copy(w, k, block, to, own=False):
            px, py, pc = block
            dst = out_refs[w].at[4 * px + 2 * py + pc]
            return pltpu.make_async_remote_copy(
                src_ref=x_refs[w] if own else dst, dst_ref=dst, send_sem=send_sems.at[w * N_COPIES + k],
                recv_sem=recv_sems.at[w * N_COPIES + k], device_id=to, device_id_type=MESH)

        mine = [pltpu.make_async_copy(x_refs[w], out_refs[w].at[4 * x + 2 * y + c], local_sems.at[w]) for w in range(n)]
        for cp in mine:
            cp.start()
        first = [copy(w, 1 + j, me, (*chip, c), own=True) for j, chip in enumerate(chips) for w in range(n)]
        first += [copy(w, 0, me, sibling, own=True) for w in range(n)]
        for cp in first:
            cp.start()
        passed = []
        for j, chip in enumerate(chips):
            for w in range(n):
                copy(w, 1 + j, (*chip, c), me).wait_recv()
                passed.append(copy(w, 4 + j, (*chip, c), sibling))
                passed[-1].start()
        for w in range(n):
            copy(w, 0, sibling, me).wait_recv()
        for j, chip in enumerate(chips):
            for w in range(n):
                copy(w, 4 + j, (*chip, 1 - c), me).wait_recv()
        for cp in first + passed:
            cp.wait_send()
        for cp in mine:
            cp.wait()

    out_shape = [jax.ShapeDtypeStruct((N_DEV, *s.shape), s.dtype) for s in shards]
    return _comm_call(body, shards, out_shape, name)


HBM_SPEC = pl.BlockSpec(memory_space=pltpu.HBM)
SEM_SPEC = pl.BlockSpec(memory_space=pltpu.SEMAPHORE)
EFFECT = pltpu.SideEffectType.DATAFLOW_SIDE_EFFECTING
PEER_ORDER = (4, 2, 6, 5, 3, 7, 1)


def _peer_copies(src_refs, land_refs, send_sems, recv_sems, gather):
    x, y, c = _mesh_pos()
    me = 4 * x + 2 * y + c
    copies = []
    for k in PEER_ORDER:
        px = 1 - x if k & 4 else x
        py = 1 - y if k & 2 else y
        pc = 1 - c if k & 1 else c
        for w, (src, land) in enumerate(zip(src_refs, land_refs)):
            copies.append(pltpu.make_async_remote_copy(
                src_ref=src if gather else src.at[4 * px + 2 * py + pc],
                dst_ref=land.at[me] if gather else land.at[k - 1],
                send_sem=send_sems.at[w * N_COPIES + k - 1], recv_sem=recv_sems.at[w * N_COPIES + k - 1],
                device_id=(px, py, pc), device_id_type=MESH))
    return copies


def _all_gather_small(shard, name):
    def body(x_ref, out_ref, send_sems, recv_sems):
        x, y, c = _mesh_pos()
        out_ref[4 * x + 2 * y + c] = x_ref[...]
        copies = _peer_copies([x_ref], [out_ref], send_sems, recv_sems, True)
        for cp in copies:
            cp.start()
        for cp in copies:
            cp.wait()

    vmem = pl.BlockSpec(memory_space=pltpu.VMEM)
    return pl.pallas_call(
        body, name=name, out_shape=jax.ShapeDtypeStruct((N_DEV, *shard.shape), shard.dtype), in_specs=[vmem],
        out_specs=vmem, scratch_shapes=[pltpu.SemaphoreType.DMA((N_COPIES,)), pltpu.SemaphoreType.DMA((N_COPIES,))],
    )(shard)


def _copies_start(srcs, *, gather, after, name):
    n = len(srcs)
    lands = [lax.empty(((N_DEV,) + s.shape) if gather else ((N_COPIES,) + s.shape[1:]), s.dtype) for s in srcs]

    def body(*refs):
        src_refs, land_refs = refs[:n], refs[n:2 * n]
        send_sems, recv_sems = refs[-2 * n - 3], refs[-2 * n - 2]
        for cp in _peer_copies(src_refs, land_refs, send_sems, recv_sems, gather):
            cp.start()
        refs[-1][...] = jnp.zeros(refs[-1].shape, F32)

    ins = [pltpu.with_memory_space_constraint(a, pltpu.HBM) for a in srcs + lands]
    in_specs = [HBM_SPEC] * (2 * n)
    if after is not None:
        ins.append(after)
        in_specs.append(pl.BlockSpec(memory_space=pl.ANY))
    sems = pltpu.SemaphoreType.DMA((n * N_COPIES,))
    out = pl.pallas_call(
        body, name=name,
        out_shape=(sems, sems, *[pltpu.HBM(a.shape, a.dtype) for a in srcs + lands], jax.ShapeDtypeStruct((8, LANES), F32)),
        in_specs=in_specs,
        out_specs=(SEM_SPEC, SEM_SPEC, *[HBM_SPEC] * (2 * n), pl.BlockSpec(memory_space=pltpu.VMEM)),
        input_output_aliases={i: 2 + i for i in range(2 * n)},
        compiler_params=pltpu.CompilerParams(has_side_effects=EFFECT),
    )(*ins)
    return out[0], out[1], list(out[2:2 + n]), list(out[2 + n:2 + 2 * n]), out[-1]


def _copies_wait(started, *, gather, after, name):
    send_sems, recv_sems, srcs, lands, _ = started
    n = len(srcs)

    def body(*refs):
        src_refs, land_refs = refs[:n], refs[n:2 * n]
        for cp in _peer_copies(src_refs, land_refs, refs[2 * n], refs[2 * n + 1], gather):
            cp.wait_send()
            cp.wait_recv()

    out = pl.pallas_call(
        body, name=name, out_shape=tuple(pltpu.HBM(a.shape, a.dtype) for a in srcs + lands),
        in_specs=[HBM_SPEC] * (2 * n) + [SEM_SPEC, SEM_SPEC, pl.BlockSpec(memory_space=pl.ANY)],
        out_specs=tuple([HBM_SPEC] * (2 * n)), input_output_aliases={i: i for i in range(2 * n)},
        compiler_params=pltpu.CompilerParams(has_side_effects=EFFECT),
    )(*srcs, *lands, send_sems, recv_sems, after)
    return list(out[n:])


def _adamw(got, w, m, v, *, name, tile, own=None, layer=None, into=None):
    rows, width = w.shape[-2:]
    n_got = got.shape[0]
    c1 = 1.0 / (1.0 - ADAM_B1 ** ADAM_STEP)
    c2 = 1.0 / (1.0 - ADAM_B2 ** ADAM_STEP)

    def body(*refs):
        got_ref, w_ref, m_ref, v_ref = refs[:4]
        g_ref, d_ref, nm_ref, nv_ref = refs[-4:]
        g = got_ref[0] if own is None else refs[4][...] + got_ref[0]
        for s in range(1, n_got):
            g = g + got_ref[s]
        m_new = ADAM_B1 * m_ref[...] + (1.0 - ADAM_B1) * g
        v_new = ADAM_B2 * v_ref[...] + (1.0 - ADAM_B2) * (g * g)
        g_ref[...] = g
        nm_ref[...] = m_new
        nv_ref[...] = v_new
        d_ref[...] = -ADAM_LR * ((m_new * c1) / (jnp.sqrt(v_new * c2) + ADAM_EPS) + ADAM_WD * w_ref[...])

    spec = pl.BlockSpec((tile, width), lambda i: (i, 0))
    wspec = spec if layer is None else pl.BlockSpec((None, tile, width), lambda i: (layer, i, 0))
    args = [got, w, m, v] + ([] if own is None else [own])
    in_specs = [pl.BlockSpec((n_got, tile, width), lambda i: (0, i, 0))] + [wspec] * 3 + [spec] * (len(args) - 4)
    aliases = {}
    if into is not None:
        aliases = {len(args) + i: i for i in range(4)}
        args += list(into)
        in_specs += [pl.BlockSpec(memory_space=pl.ANY)] * 4
    return pl.pallas_call(
        body, name=name, grid=(rows // tile,), in_specs=in_specs, out_specs=[wspec] * 4,
        out_shape=[jax.ShapeDtypeStruct(w.shape, F32)] * 4, input_output_aliases=aliases,
        compiler_params=_cparams("parallel"),
    )(*args)


SHARDED = dict(a_w_in=(2, True), a_w_out=(1, True), b_w_in=(2, True), b_w_out=(1, True), ffn_w_in=(2, True),
               ffn_w_out=(1, True), b_conv_w=(2, False), b_conv_b=(1, False), b_norm_w=(1, False), ffn_conv_w=(2, False))
REPLICATED = ("norm1_w", "norm2_w", "a_lb_logits", "a_norm_w", "b_dt_bias", "b_a_log", "b_d_skip", "ffn_conv_b",
              "final_norm_w")
WEIGHTS = ("norm1_w", "norm2_w", "a_w_in", "a_lb_logits", "a_norm_w", "a_w_out", "b_w_in", "b_conv_w", "b_conv_b",
           "b_dt_bias", "b_a_log", "b_d_skip", "b_norm_w", "b_w_out", "ffn_w_in", "ffn_conv_w", "ffn_conv_b",
           "ffn_w_out", "final_norm_w")


def _pad_rows(flat, multiple):
    n = flat.shape[-1]
    per = PACK_W * multiple
    total = -(-n // per) * per
    flat = jnp.pad(flat, [(0, 0)] * (flat.ndim - 1) + [(0, total - n)])
    return flat.reshape(*flat.shape[:-1], total // PACK_W, PACK_W)


def _to_parts(full, axis, n=N_DEV):
    shp = full.shape
    t = full.reshape(*shp[:axis], n, shp[axis] // n, *shp[axis + 1:])
    return jnp.moveaxis(t, axis, 0)


def _from_parts(parts, axis):
    t = jnp.moveaxis(parts, 0, axis)
    shp = t.shape
    return t.reshape(*shp[:axis], shp[axis] * shp[axis + 1], *shp[axis + 2:])


BIG = tuple(n for n, (_, mm) in SHARDED.items() if mm)
SMALL = tuple(n for n, (_, mm) in SHARDED.items() if not mm)
SMALL_W = 512


def _small_rows(tree, lead):
    rows = []
    for n in SMALL:
        t = tree[n]
        t = t.reshape(*lead, -1, t.shape[-1])
        rows.append(jnp.pad(t, [(0, 0)] * (t.ndim - 1) + [(0, SMALL_W - t.shape[-1])]))
    buf = jnp.concatenate(rows, axis=-2)
    return jnp.pad(buf, [(0, 0)] * (buf.ndim - 2) + [(0, 16 - buf.shape[-2]), (0, 0)])


def _small_unrows(buf, like, lead):
    out, r = {}, 0
    for n in SMALL:
        shp = like[n].shape
        k = like[n].size // shp[-1]
        out[n] = buf[..., r:r + k, :shp[-1]].reshape(*lead, *shp)
        r += k
    return out


GROUPS = dict(hg=(("a_w_in", 0), ("a_w_out", 0)), f0=(("ffn_w_in", 0), ("ffn_w_out", 0)),
              mb=(("b_w_in", 0), ("b_w_out", 0)), f1=(("ffn_w_in", 1), ("ffn_w_out", 1)))


class _Comm:
    def __init__(self, local):
        x, y, c = _mesh_pos()
        self.me = 4 * x + 2 * y + c
        self.local = local
        self.shards = {g: [local[n][i].astype(BF16) for n, i in names] for g, names in GROUPS.items()}
        self.shards["f0"].append(_small_rows(local, ()))
        self.first = _all_gather(self.shards["hg"], "gather_hg")
        self.gathers, self.sent, token = {}, {}, None
        for g in ("f0", "mb", "f1"):
            self.gathers[g] = _copies_start(self.shards[g], gather=True, after=token, name=f"gather_{g}_start")
            token = self.gathers[g][-1]
        self.token = token

    def weights(self, group, after):
        if group == "hg":
            got = self.first
        else:
            lands = _copies_wait(self.gathers[group], gather=True, after=after, name=f"gather_{group}_wait")
            got = [lax.dynamic_update_index_in_dim(land, shard, self.me, 0)
                   for land, shard in zip(lands, self.shards[group])]
        out = dict(w_in=_from_parts(got[0], 1), w_out=_from_parts(got[1], 0))
        if group == "hg":
            out["token"] = self.token
        if group == "f0":
            small = _small_unrows(got[2], self.local, (N_DEV,))
            out["small"] = {n: _from_parts(small[n], SHARDED[n][0]) for n in SMALL}
        return out

    def send(self, group, grads):
        w_in = grads["w_in"]
        if isinstance(w_in, tuple):
            half = N_DEV // len(w_in)
            parts_in = jnp.concatenate([_to_parts(t, 1, half) for t in w_in], axis=0)
        else:
            parts_in = _to_parts(w_in, 1)
        parts = [parts_in, _to_parts(grads["w_out"], 0)]
        if "small" in grads:
            parts.append(_small_rows({n: _to_parts(grads["small"][n], SHARDED[n][0]) for n in SMALL}, (N_DEV,)))
        sent = [p.astype(BF16) for p in parts[:2]] + parts[2:] if group == "hg" else parts
        self.sent[group] = (parts, _copies_start(sent, gather=False, after=None, name=f"exchange_{group}_start"))
        return self.sent[group][1][-1]

    def finish(self, after, mom, var):
        res = {}
        for group in ("f1", "mb", "f0", "hg"):
            parts, started = self.sent[group]
            lands = _copies_wait(started, gather=False, after=after, name=f"exchange_{group}_wait")
            own = [lax.dynamic_index_in_dim(p, self.me, 0, keepdims=False) for p in parts]
            for (n, i), got, mine in zip(GROUPS[group], lands, own):
                res[n] = _adamw(got, self.local[n], mom[n], var[n], own=mine, layer=i, into=res.get(n),
                                name=f"adamw_{n}_{i}", tile=_pick(mine.shape[0], (256, 176, 128)))
                after = res[n][0]
        small = _adamw(lands[2], *[_small_rows(t, ()) for t in (self.local, mom, var)], own=own[2],
                       name="adamw_small", tile=16)
        return res, small


def _pack_small(tree, extra):
    flat = jnp.concatenate([tree[n].reshape(-1) for n in REPLICATED] + [extra.reshape(-1)])
    return _pad_rows(flat, 8)


def _unpack_small(pack, like):
    flat, out, off = pack.reshape(-1), {}, 0
    for n in REPLICATED:
        out[n] = flat[off:off + like[n].size].reshape(like[n].shape)
        off += like[n].size
    return out, flat[off]


def kernel(x, norm1_w, norm2_w, a_w_in, a_lb_logits, a_norm_w, a_w_out, b_w_in, b_conv_w, b_conv_b, b_dt_bias, b_a_log, b_d_skip, b_norm_w, b_w_out, ffn_w_in, ffn_conv_w, ffn_conv_b, ffn_w_out, final_norm_w, loss_target, m_norm1_w, m_norm2_w, m_a_w_in, m_a_lb_logits, m_a_norm_w, m_a_w_out, m_b_w_in, m_b_conv_w, m_b_conv_b, m_b_dt_bias, m_b_a_log, m_b_d_skip, m_b_norm_w, m_b_w_out, m_ffn_w_in, m_ffn_conv_w, m_ffn_conv_b, m_ffn_w_out, m_final_norm_w, v_norm1_w, v_norm2_w, v_a_w_in, v_a_lb_logits, v_a_norm_w, v_a_w_out, v_b_w_in, v_b_conv_w, v_b_conv_b, v_b_dt_bias, v_b_a_log, v_b_d_skip, v_b_norm_w, v_b_w_out, v_ffn_w_in, v_ffn_conv_w, v_ffn_conv_b, v_ffn_w_out, v_final_norm_w):
    given = dict(locals())
    local = {n: given[n] for n in WEIGHTS}
    mom = {n: given["m_" + n] for n in WEIGHTS}
    var = {n: given["v_" + n] for n in WEIGHTS}

    comm = _Comm(local)
    loss, grad_x, grads = _local_step(x[0], loss_target[0], local, comm)

    res, small_res = comm.finish(grad_x, mom, var)
    outs = ({}, {}, {}, {})
    for n in BIG:
        for out, r in zip(outs, res[n]):
            out[n] = r
    for out, r in zip(outs, small_res):
        out.update(_small_unrows(r, local, ()))
    out_g, out_d, out_m, out_v = outs

    small = _pack_small(grads, loss)
    rows = small.shape[0]
    got_s = _all_gather_small(small, "gather_small")
    zero = jnp.zeros((1,), F32)
    g, dlt, nm, nv = _adamw(got_s, _pack_small(local, zero), _pack_small(mom, zero), _pack_small(var, zero),
                            name="adamw_replicated", tile=rows)
    (rep_g, total), (rep_d, _), (rep_m, _), (rep_v, _) = (_unpack_small(t, local) for t in (g, dlt, nm, nv))
    out_g.update(rep_g)
    out_d.update(rep_d)
    out_m.update(rep_m)
    out_v.update(rep_v)

    return (total, grad_x[None], *[out_g[n] for n in WEIGHTS], *[out_d[n] for n in WEIGHTS],
            *[out_m[n] for n in WEIGHTS], *[out_v[n] for n in WEIGHTS])
```

```python
import functools

import jax
import jax.numpy as jnp
from jax import lax
from jax.experimental import pallas as pl
from jax.experimental.pallas import tpu as pltpu

F32 = jnp.float32
BF16 = jnp.bfloat16
HIGHEST = lax.Precision.HIGHEST
MESH = pl.DeviceIdType.MESH

N_DEV = 8
EPS = 1e-6
D_MODEL = 1024
HG_HEADS = 8
HG_HEAD_DIM = 128
D_INNER = 2048
SSM_HEAD_DIM = 64
SSM_GROUPS = 8
SSM_HPG = 4
SSM_STATE = 128
GN = SSM_GROUPS * SSM_STATE
CONV_DIM = D_INNER + 2 * GN
D_FF = 2816
SSM_CONV = 5
FFN_CONV = 3

ADAM_LR = 0.001
ADAM_B1 = 0.9
ADAM_B2 = 0.999
ADAM_EPS = 1e-08
ADAM_WD = 0.01
ADAM_STEP = 10

LANES = 128
ROW_TILE = 256
COL_TILE = 128
GLA_CHUNK = 16
GLA_BLOCK = 256
GLA_HEADS_PER_STEP = 8
SSD_CHUNK = 128
SSD_BLOCK = 512
PACK_W = 1024
VMEM_LIMIT = 56 * 1024 * 1024

NT_DIMS = (((1,), (1,)), ((), ()))
TN_DIMS = (((0,), (0,)), ((), ()))


def _cparams(*sem):
    return pltpu.CompilerParams(dimension_semantics=sem, vmem_limit_bytes=VMEM_LIMIT)


def _rms(x, w):
    return x * lax.rsqrt(jnp.mean(x * x, axis=-1, keepdims=True) + EPS) * w


def _row_kernel(body_fn, rows, params, row_outs, acc_outs, *, name, tile=ROW_TILE):
    L = rows[0][0].shape[0]
    tile = min(tile, L)
    n_in = len(rows) + len(params)
    n_ro = len(row_outs)

    def body(*refs):
        outs = body_fn(*[r[...] for r in refs[:n_in]])
        for ref, o in zip(refs[n_in:n_in + n_ro], outs[:n_ro]):
            ref[...] = o.astype(ref.dtype)
        first = pl.program_id(0) == 0
        for ref, o in zip(refs[n_in + n_ro:], outs[n_ro:]):
            @pl.when(first)
            def _(ref=ref):
                ref[...] = jnp.zeros(ref.shape, ref.dtype)
            ref[...] += o

    in_specs = [pl.BlockSpec((tile, w), lambda i, cb=cb: (i, cb)) for _, w, cb in rows]
    in_specs += [pl.BlockSpec(p.shape, lambda i: (0, 0)) for p in params]
    out_specs = [pl.BlockSpec((tile, w), lambda i: (i, 0)) for w, _ in row_outs]
    out_specs += [pl.BlockSpec(s, lambda i: (0, 0)) for s in acc_outs]
    out_shape = [jax.ShapeDtypeStruct((L, w), dt) for w, dt in row_outs]
    out_shape += [jax.ShapeDtypeStruct(s, F32) for s in acc_outs]
    return pl.pallas_call(
        body, name=name, grid=(L // tile,), in_specs=in_specs, out_specs=out_specs, out_shape=out_shape,
        compiler_params=_cparams("arbitrary" if acc_outs else "parallel"),
    )(*[a for a, _, _ in rows], *params)


def _col_kernel(body_fn, cols, params, col_outs, par_outs, *, name, n_tiles):
    L = cols[0][0].shape[0]
    n_in = len(cols) + len(params)
    width = n_tiles * COL_TILE

    def body(*refs):
        outs = body_fn(*[r[...] for r in refs[:n_in]])
        for ref, o in zip(refs[n_in:], outs):
            ref[...] = o.astype(ref.dtype)

    in_specs = [pl.BlockSpec((L, COL_TILE), lambda j, cb=cb: (0, cb + j)) for _, cb in cols]
    in_specs += [pl.BlockSpec((p.shape[0], COL_TILE), lambda j, cb=cb: (0, cb + j)) for p, cb in params]
    out_specs = [pl.BlockSpec((L, COL_TILE), lambda j: (0, j)) for _ in col_outs]
    out_specs += [pl.BlockSpec((k, COL_TILE), lambda j: (0, j)) for k in par_outs]
    out_shape = [jax.ShapeDtypeStruct((L, width), dt) for dt in col_outs]
    out_shape += [jax.ShapeDtypeStruct((k, width), F32) for k in par_outs]
    return pl.pallas_call(
        body, name=name, grid=(n_tiles,), in_specs=in_specs, out_specs=out_specs, out_shape=out_shape,
        compiler_params=_cparams("parallel"),
    )(*[a for a, _ in cols], *[p for p, _ in params])


def _pick(n, options):
    for t in options:
        if n % t == 0:
            return t
    return n


MATMUL_VMEM = 40 * 1024 * 1024


def _matmul_tiles(M, N, K, out_bytes):
    best = None
    for tm in (1024, 512, 256, 128, M):
        for tn in (1408, 1024, 512, 256, 128, N):
            if M % tm or N % tn:
                continue
            if 2 * (2 * K * (tm + tn) + out_bytes * tm * tn) > MATMUL_VMEM:
                continue
            if best is None or tm * tn > best[0] * best[1]:
                best = (tm, tn)
    return best


def _matmul(a, b, *, name, nt=False, ta=False, add=None, out_dtype=F32, after=None, b_cols=None):
    K, M = a.shape[::-1] if not ta else a.shape
    cb, width = b_cols if b_cols is not None else (0, b.shape[1])
    N = b.shape[0] if nt else width
    assert width == K or not nt
    tm, tn = _matmul_tiles(M, N, K, 4 * (1 + (add is not None)) if out_dtype == F32 else 2 + 4 * (add is not None))
    col0 = cb if nt else cb * (width // tn)

    def body(*refs):
        a_ref, b_ref = refs[0], refs[1]
        o_ref = refs[-1]
        dims = TN_DIMS if ta else NT_DIMS if nt else (((1,), (0,)), ((), ()))
        acc = lax.dot_general(a_ref[...], b_ref[...], dims, preferred_element_type=F32)
        if add is not None:
            acc = acc + refs[2][...]
        o_ref[...] = acc.astype(o_ref.dtype)

    in_specs = [pl.BlockSpec((K, tm), lambda i, j: (0, i)) if ta else pl.BlockSpec((tm, K), lambda i, j: (i, 0)),
                pl.BlockSpec((tn, K), lambda i, j: (j, col0)) if nt
                else pl.BlockSpec((K, tn), lambda i, j: (0, col0 + j))]
    args = [a, b]
    if add is not None:
        in_specs.append(pl.BlockSpec((tm, tn), lambda i, j: (i, j)))
        args.append(add)
    if after is not None:
        in_specs.append(pl.BlockSpec(memory_space=pl.ANY))
        args.append(after)
    return pl.pallas_call(
        body, name=name, grid=(M // tm, N // tn), in_specs=in_specs,
        out_specs=pl.BlockSpec((tm, tn), lambda i, j: (i, j)),
        out_shape=jax.ShapeDtypeStruct((M, N), out_dtype),
        compiler_params=_cparams("parallel", "parallel"),
    )(*args)


def _hgrn2_pre_fn(q_raw, ffw_raw, fbw_raw, lb_logits):
    lb = jax.nn.softmax(lb_logits, axis=0)[0:1]

    def gate(fr):
        f = lb + (1.0 - lb) * jax.nn.sigmoid(fr)
        return jnp.log(f), 1.0 - f

    lf_fw, k_fw = gate(ffw_raw)
    lf_bw, k_bw = gate(fbw_raw)
    return jax.nn.silu(q_raw), lf_fw, k_fw, lf_bw, k_bw


def _hgrn2_post_fn(o, g, norm_w):
    outs = []
    for h in range(HG_HEADS):
        sl = slice(h * HG_HEAD_DIM, (h + 1) * HG_HEAD_DIM)
        outs.append(_rms(o[:, sl], norm_w) * jax.nn.silu(g[:, sl]))
    return jnp.concatenate(outs, axis=1)


def _mamba_post_fn(y, z, norm_w):
    outs = []
    gw = D_INNER // SSM_GROUPS
    for gi in range(SSM_GROUPS):
        sl = slice(gi * gw, (gi + 1) * gw)
        outs.append(_rms(y[:, sl] * jax.nn.silu(z[:, sl]), norm_w[:, sl]))
    return jnp.concatenate(outs, axis=1)


def _shift_rows_impl(x, d):
    if d == 0:
        return x
    n, edge = x.shape[0], 8
    t = lax.broadcasted_iota(jnp.int32, (edge, x.shape[1]), 0)
    rolled = pltpu.roll(x, d % n, 0)
    if d > 0:
        return jnp.concatenate([jnp.where(t >= d, rolled[:edge], 0.0), rolled[edge:]], axis=0)
    return jnp.concatenate([rolled[:n - edge], jnp.where(t < edge + d, rolled[n - edge:], 0.0)], axis=0)


@functools.partial(jax.custom_vjp, nondiff_argnums=(1,))
def _shift_rows(x, d):
    return _shift_rows_impl(x, d)


_shift_rows.defvjp(lambda x, d: (_shift_rows_impl(x, d), None), lambda d, _, g: (_shift_rows_impl(g, -d),))


def _dwconv(x, w, b):
    taps = w.shape[0]
    c = (taps - 1) // 2
    y = b + w[0:1, :] * _shift_rows(x, c)
    for k in range(1, taps):
        y = y + w[k:k + 1, :] * _shift_rows(x, c - k)
    return y


def _ffn_mid_fn(gate, val, w, b):
    return jax.nn.silu(_dwconv(gate, w, b)) * val


def _mamba_conv_fn(xbc, w, b):
    return jax.nn.silu(_dwconv(xbc, w, b))


def _gla_consts(rev):
    return lax.broadcasted_iota(jnp.int32, (GLA_CHUNK, HG_HEAD_DIM), 0)


def _segment_cumsum(x, seg, rev):
    n = x.shape[0]
    t = lax.broadcasted_iota(jnp.int32, x.shape, 0) & (seg - 1)
    s = 1
    while s < seg:
        if rev:
            x = x + jnp.where(t < seg - s, pltpu.roll(x, n - s, 0), 0.0)
        else:
            x = x + jnp.where(t >= s, pltpu.roll(x, s, 0), 0.0)
        s *= 2
    return x


def _segment_cumsum_mxu(x, seg, rev):
    r = lax.broadcasted_iota(jnp.int32, (seg, seg), 0)
    c = lax.broadcasted_iota(jnp.int32, (seg, seg), 1)
    tri = ((r <= c) if rev else (r >= c)).astype(BF16)
    tri3 = jnp.concatenate([tri, tri, tri], axis=1)
    hi = x.astype(BF16)
    rest = x - hi.astype(F32)
    mid = rest.astype(BF16)
    lo = (rest - mid.astype(F32)).astype(BF16)
    outs = []
    for g in range(x.shape[0] // seg):
        rows = slice(g * seg, (g + 1) * seg)
        terms = jnp.concatenate([hi[rows], mid[rows], lo[rows]], axis=0)
        outs.append(jnp.dot(tri3, terms, preferred_element_type=F32))
    return jnp.concatenate(outs, axis=0)


@functools.partial(jax.custom_vjp, nondiff_argnums=(1, 2))
def _segment_cumsum_diff(x, seg, rev):
    return _segment_cumsum_mxu(x, seg, rev)


_segment_cumsum_diff.defvjp(lambda x, seg, rev: (_segment_cumsum_mxu(x, seg, rev), None),
                            lambda seg, rev, _, g: (_segment_cumsum_mxu(g, seg, not rev),))


def _gla_chunk(st, q, k, v, b, *, rev, consts):
    row = consts
    c = q.shape[0]
    o = lax.dot_general((q * jnp.exp(b)).astype(BF16), st.astype(BF16), NT_DIMS, preferred_element_type=F32)
    for s in range(c):
        e = jnp.exp(jnp.where((row <= s) if rev else (row >= s), b - b[s:s + 1], -jnp.inf))
        o = o + jnp.sum(q * (k[s:s + 1] * e), axis=-1, keepdims=True) * v[s:s + 1]
    b_end = b[0:1] if rev else b[c - 1:c]
    kd = (k * jnp.exp(b_end - b)).astype(BF16)
    st_new = st * jnp.exp(b_end) + lax.dot_general(v.astype(BF16), kd, TN_DIMS, preferred_element_type=F32)
    return st_new, o


def _gla_fwd(q, k, v, v_cb, g, *, rev, add, name):
    L = q.shape[0]
    blk = min(GLA_BLOCK, L)
    nblk, nsub = L // blk, blk // GLA_CHUNK
    hd, hps = HG_HEAD_DIM, GLA_HEADS_PER_STEP
    wide = hd * hps

    def body(*refs):
        q_ref, k_ref, v_ref, g_ref = refs[:4]
        add_ref = refs[4] if add is not None else None
        o_ref, st_out, st_scr, b_scr = refs[-4:]
        consts = _gla_consts(rev)

        @pl.when(pl.program_id(1) == 0)
        def _():
            st_scr[...] = jnp.zeros(st_scr.shape, F32)

        b_scr[...] = _segment_cumsum(g_ref[...], GLA_CHUNK, rev)

        def step(i, carry):
            sl = pl.ds(pl.multiple_of((nsub - 1 - i if rev else i) * GLA_CHUNK, GLA_CHUNK), GLA_CHUNK)
            lanes = [slice(hi * hd, (hi + 1) * hd) for hi in range(hps)]
            ins = [(st_scr[hi], q_ref[sl, ln], k_ref[sl, ln], v_ref[sl, ln], b_scr[sl, ln])
                   for hi, ln in enumerate(lanes)]
            adds = [add_ref[sl, ln] for ln in lanes] if add_ref is not None else None
            outs = [_gla_chunk(*args, rev=rev, consts=consts) for args in ins]
            for hi, ln in enumerate(lanes):
                st_out[hi, i] = ins[hi][0]
                st_scr[hi] = outs[hi][0]
                o_ref[sl, ln] = outs[hi][1] if adds is None else outs[hi][1] + adds[hi]
            return carry

        lax.fori_loop(0, nsub, step, 0)

    def pos(j):
        return nblk - 1 - j if rev else j

    spec = pl.BlockSpec((blk, wide), lambda h, j: (pos(j), h))
    in_specs = [spec, spec, pl.BlockSpec((blk, wide), lambda h, j: (pos(j), v_cb // hps + h)), spec]
    args = [q, k, v, g]
    if add is not None:
        in_specs.append(spec)
        args.append(add)
    return pl.pallas_call(
        body, name=name, grid=(HG_HEADS // hps, nblk), in_specs=in_specs,
        out_specs=[spec, pl.BlockSpec((hps, None, nsub, hd, hd), lambda h, j: (h, j, 0, 0, 0))],
        out_shape=[jax.ShapeDtypeStruct((L, HG_HEADS * hd), F32),
                   jax.ShapeDtypeStruct((HG_HEADS, nblk, nsub, hd, hd), F32)],
        scratch_shapes=[pltpu.VMEM((hps, hd, hd), F32), pltpu.VMEM((blk, wide), F32)],
        compiler_params=_cparams("parallel", "arbitrary"),
    )(*args)


def _gla_bwd(q, k, v, v_cb, g, do, states, *, rev, adds, name):
    L = q.shape[0]
    blk = min(GLA_BLOCK, L)
    nblk, nsub = L // blk, blk // GLA_CHUNK
    hd, hps = HG_HEAD_DIM, GLA_HEADS_PER_STEP
    wide = hd * hps
    n_add = 0 if adds is None else 2

    def body(*refs):
        q_ref, k_ref, v_ref, g_ref, do_ref, st_in = refs[:6]
        add_refs = refs[6:6 + n_add]
        dq_ref, dk_ref, dv_ref, dg_ref, dst_scr, b_scr, db_scr = refs[6 + n_add:]
        consts = _gla_consts(rev)
        chunk = functools.partial(_gla_chunk, rev=rev, consts=consts)

        @pl.when(pl.program_id(1) == 0)
        def _():
            dst_scr[...] = jnp.zeros(dst_scr.shape, F32)

        b_scr[...] = _segment_cumsum(g_ref[...], GLA_CHUNK, rev)

        def rows(i):
            return pl.ds(pl.multiple_of((nsub - 1 - i if rev else i) * GLA_CHUNK, GLA_CHUNK), GLA_CHUNK)

        def step(n, carry):
            i = nsub - 1 - n
            sl = rows(i)
            lanes = [slice(hi * hd, (hi + 1) * hd) for hi in range(hps)]
            ins = [(st_in[hi, i], q_ref[sl, ln], k_ref[sl, ln], v_ref[sl, ln], b_scr[sl, ln])
                   for hi, ln in enumerate(lanes)]
            cts = [(dst_scr[hi], do_ref[sl, ln]) for hi, ln in enumerate(lanes)]
            adds = [(add_refs[0][sl, ln], add_refs[1][sl, ln]) for ln in lanes] if n_add else None
            outs = [jax.vjp(chunk, *args)[1](ct) for args, ct in zip(ins, cts)]
            for hi, ln in enumerate(lanes):
                dst, dq, dk, dv, db = outs[hi]
                dst_scr[hi] = dst
                dq_ref[sl, ln] = dq if adds is None else dq + adds[hi][0]
                dk_ref[sl, ln] = dk
                dv_ref[sl, ln] = dv if adds is None else dv + adds[hi][1]
                db_scr[sl, ln] = db
            return carry

        lax.fori_loop(0, nsub, step, 0)
        dg_ref[...] = _segment_cumsum(db_scr[...], GLA_CHUNK, not rev)

    def pos(j):
        p = nblk - 1 - j
        return nblk - 1 - p if rev else p

    spec = pl.BlockSpec((blk, wide), lambda h, j: (pos(j), h))
    in_specs = [spec, spec, pl.BlockSpec((blk, wide), lambda h, j: (pos(j), v_cb // hps + h)), spec, spec,
                pl.BlockSpec((hps, None, nsub, hd, hd), lambda h, j: (h, nblk - 1 - j, 0, 0, 0))]
    args = [q, k, v, g, do, states]
    if adds is not None:
        in_specs += [spec, spec]
        args += list(adds)
    out = jax.ShapeDtypeStruct((L, HG_HEADS * hd), F32)
    return pl.pallas_call(
        body, name=name, grid=(HG_HEADS // hps, nblk), in_specs=in_specs,
        out_specs=[spec] * 4, out_shape=[out] * 4,
        scratch_shapes=[pltpu.VMEM((hps, hd, hd), F32), pltpu.VMEM((blk, wide), F32), pltpu.VMEM((blk, wide), F32)],
        compiler_params=_cparams("parallel", "arbitrary"),
    )(*args)


def _ssd_consts(rev):
    c = SSD_CHUNK
    gw = SSM_HPG * SSM_HEAD_DIM
    r2 = lax.broadcasted_iota(jnp.int32, (c, c), 0)
    c2 = lax.broadcasted_iota(jnp.int32, (c, c), 1)
    low = (r2 <= c2) if rev else (r2 >= c2)
    lane = lax.broadcasted_iota(jnp.int32, (1, gw), 1)
    return low, r2 == c2, lane, lane >> 6


def _expand_heads(v4, head_of_lane):
    first = head_of_lane[:, :LANES] == 0
    cols = [v4[:, j:j + 1] for j in range(SSM_HPG)]
    return jnp.concatenate([jnp.where(first, cols[0], cols[1]), jnp.where(first, cols[2], cols[3])], axis=1)


def _ssd_prep(dtr, bias, alog, dsk, *, rev, consts):
    head_of_lane = consts[-1]
    dt_l = _expand_heads(jax.nn.softplus(dtr + bias), head_of_lane)
    a_l = _expand_heads(-jnp.exp(alog), head_of_lane)
    return dt_l, _segment_cumsum_diff(dt_l * a_l, SSD_CHUNK, rev), _expand_heads(dsk, head_of_lane)


def _ssd_chunk(st, x, bm, cm, dt_l, acum, dsk_l, *, rev, consts, skip):
    low, eye, lane, head_of_lane = consts
    c = x.shape[0]
    xd = x * dt_l
    cb = lax.dot_general(cm.astype(BF16), bm.astype(BF16), NT_DIMS, preferred_element_type=F32)
    y = jnp.dot(cm.astype(BF16), st.astype(BF16), preferred_element_type=F32) * jnp.exp(acum)
    for j in range(SSM_HPG):
        acol = jnp.sum(jnp.where(lane == j * SSM_HEAD_DIM, acum, 0.0), axis=-1, keepdims=True)
        ab = jnp.broadcast_to(acol, (c, c))
        arow = jnp.sum(jnp.where(eye, ab, 0.0), axis=0, keepdims=True)
        lmat = jnp.exp(jnp.where(low, ab - arow, -jnp.inf))
        xj = jnp.where(head_of_lane == j, xd, 0.0)
        y = y + jnp.dot((cb * lmat).astype(BF16), xj.astype(BF16), preferred_element_type=F32)
    a_end = acum[0:1] if rev else acum[c - 1:c]
    xdec = (xd * jnp.exp(a_end - acum)).astype(BF16)
    st_new = st * jnp.exp(a_end) + lax.dot_general(bm.astype(BF16), xdec, TN_DIMS, preferred_element_type=F32)
    if skip:
        y = y + x * dsk_l
    return st_new, y


def _ssd_specs(L, rev):
    blk = min(SSD_BLOCK, L)
    nblk = L // blk
    gw = SSM_HPG * SSM_HEAD_DIM

    def pos(j):
        return nblk - 1 - j if rev else j

    return blk, nblk, gw, pos


def _ssd_fwd(xbc, dtr, bias, alog, dsk, *, rev, skip, add, name):
    L = xbc.shape[0]
    blk, nblk, gw, pos = _ssd_specs(L, rev)
    nsub = blk // SSD_CHUNK
    n = SSM_STATE

    def body(*refs):
        x_ref, b_ref, c_ref, dt_ref, bias_ref, alog_ref, dsk_ref = refs[:7]
        add_ref = refs[7] if add is not None else None
        y_ref, st_out, st_scr, dt_scr, ac_scr = refs[-5:]
        consts = _ssd_consts(rev)

        @pl.when(pl.program_id(1) == 0)
        def _():
            st_scr[...] = jnp.zeros(st_scr.shape, F32)

        dt_scr[...], ac_scr[...], dsk_l = _ssd_prep(dt_ref[...], bias_ref[...], alog_ref[...], dsk_ref[...],
                                                   rev=rev, consts=consts)

        def step(i, carry):
            sl = pl.ds(pl.multiple_of((nsub - 1 - i if rev else i) * SSD_CHUNK, SSD_CHUNK), SSD_CHUNK)
            st = st_scr[...]
            st_out[i] = st
            st_new, y = _ssd_chunk(st, x_ref[sl, :], b_ref[sl, :], c_ref[sl, :], dt_scr[sl, :],
                                   ac_scr[sl, :], dsk_l, rev=rev, consts=consts, skip=skip)
            st_scr[...] = st_new
            if add_ref is not None:
                y = y + add_ref[sl, :]
            y_ref[sl, :] = y
            return carry

        lax.fori_loop(0, nsub, step, 0)

    b0 = D_INNER // n
    yspec = pl.BlockSpec((blk, gw), lambda g, j: (pos(j), g))
    pspec = pl.BlockSpec((None, 1, SSM_HPG), lambda g, j: (g, 0, 0))
    in_specs = [yspec,
                pl.BlockSpec((blk, n), lambda g, j: (pos(j), b0 + g)),
                pl.BlockSpec((blk, n), lambda g, j: (pos(j), b0 + SSM_GROUPS + g)),
                pl.BlockSpec((None, blk, SSM_HPG), lambda g, j: (g, pos(j), 0)),
                pspec, pspec, pspec]
    args = [xbc, xbc, xbc, dtr, bias, alog, dsk]
    if add is not None:
        in_specs.append(yspec)
        args.append(add)
    return pl.pallas_call(
        body, name=name, grid=(SSM_GROUPS, nblk), in_specs=in_specs,
        out_specs=[yspec, pl.BlockSpec((None, None, nsub, n, gw), lambda g, j: (g, j, 0, 0, 0))],
        out_shape=[jax.ShapeDtypeStruct((L, D_INNER), F32),
                   jax.ShapeDtypeStruct((SSM_GROUPS, nblk, nsub, n, gw), F32)],
        scratch_shapes=[pltpu.VMEM((n, gw), F32), pltpu.VMEM((blk, gw), F32), pltpu.VMEM((blk, gw), F32)],
        compiler_params=_cparams("parallel", "arbitrary"),
    )(*args)


def _ssd_bwd(xbc, dtr, bias, alog, dsk, dy, states, *, rev, skip, adds, name):
    L = xbc.shape[0]
    blk, nblk, gw, _ = _ssd_specs(L, rev)
    nsub = blk // SSD_CHUNK
    n = SSM_STATE
    n_add = 0 if adds is None else 3

    def body(*refs):
        x_ref, b_ref, c_ref, dt_ref, bias_ref, alog_ref, dsk_ref, dy_ref, st_in = refs[:9]
        add_refs = refs[9:9 + n_add]
        outs = refs[9 + n_add:]
        dx_ref, db_ref, dc_ref, ddt_ref, dbias_ref, dalog_ref, ddsk_ref = outs[:7]
        dst_scr, dt_scr, ac_scr, ddt_scr, dac_scr = outs[7:]
        consts = _ssd_consts(rev)
        chunk = functools.partial(_ssd_chunk, rev=rev, consts=consts, skip=skip)
        prep = functools.partial(_ssd_prep, rev=rev, consts=consts)

        @pl.when(pl.program_id(1) == 0)
        def _():
            dst_scr[...] = jnp.zeros(dst_scr.shape, F32)
            dbias_ref[...] = jnp.zeros(dbias_ref.shape, F32)
            dalog_ref[...] = jnp.zeros(dalog_ref.shape, F32)
            ddsk_ref[...] = jnp.zeros(ddsk_ref.shape, F32)

        narrow = (dt_ref[...], bias_ref[...], alog_ref[...], dsk_ref[...])
        (dt_scr[...], ac_scr[...], dsk_l), prep_vjp = jax.vjp(prep, *narrow)

        def rows(i):
            return pl.ds(pl.multiple_of((nsub - 1 - i if rev else i) * SSD_CHUNK, SSD_CHUNK), SSD_CHUNK)

        def operands(i):
            sl = rows(i)
            return (x_ref[sl, :], b_ref[sl, :], c_ref[sl, :], dt_scr[sl, :], ac_scr[sl, :], dsk_l)

        def step(k, ddsk_l):
            i = nsub - 1 - k
            sl = rows(i)
            _, vjp = jax.vjp(chunk, st_in[i], *operands(i))
            dst, dx, db, dc, ddt_l, dac, ddsk_k = vjp((dst_scr[...], dy_ref[sl, :]))
            dst_scr[...] = dst
            if n_add:
                dx = dx + add_refs[0][sl, :]
                db = db + add_refs[1][sl, :]
                dc = dc + add_refs[2][sl, :]
            dx_ref[sl, :] = dx
            db_ref[sl, :] = db
            dc_ref[sl, :] = dc
            ddt_scr[sl, :] = ddt_l
            dac_scr[sl, :] = dac
            return ddsk_l + ddsk_k

        ddsk_l = lax.fori_loop(0, nsub, step, jnp.zeros((1, gw), F32))
        ddt, dbias, dalog, ddsk = prep_vjp((ddt_scr[...], dac_scr[...], ddsk_l))
        ddt_ref[...] = ddt
        dbias_ref[...] += dbias
        dalog_ref[...] += dalog
        ddsk_ref[...] += ddsk

    def pos(j):
        p = nblk - 1 - j
        return nblk - 1 - p if rev else p

    b0 = D_INNER // n
    xspec = pl.BlockSpec((blk, gw), lambda g, j: (pos(j), g))
    nspec = pl.BlockSpec((blk, n), lambda g, j: (pos(j), g))
    dtspec = pl.BlockSpec((None, blk, SSM_HPG), lambda g, j: (g, pos(j), 0))
    pspec = pl.BlockSpec((None, 1, SSM_HPG), lambda g, j: (g, 0, 0))
    in_specs = [xspec,
                pl.BlockSpec((blk, n), lambda g, j: (pos(j), b0 + g)),
                pl.BlockSpec((blk, n), lambda g, j: (pos(j), b0 + SSM_GROUPS + g)),
                dtspec, pspec, pspec, pspec, xspec,
                pl.BlockSpec((None, None, nsub, n, gw), lambda g, j: (g, nblk - 1 - j, 0, 0, 0))]
    args = [xbc, xbc, xbc, dtr, bias, alog, dsk, dy, states]
    if adds is not None:
        in_specs += [xspec, nspec, nspec]
        args += list(adds)
    par = jax.ShapeDtypeStruct((SSM_GROUPS, 1, SSM_HPG), F32)
    return pl.pallas_call(
        body, name=name, grid=(SSM_GROUPS, nblk), in_specs=in_specs,
        out_specs=[xspec, nspec, nspec, dtspec, pspec, pspec, pspec],
        out_shape=[jax.ShapeDtypeStruct((L, D_INNER), F32), jax.ShapeDtypeStruct((L, GN), F32),
                   jax.ShapeDtypeStruct((L, GN), F32), jax.ShapeDtypeStruct((SSM_GROUPS, L, SSM_HPG), F32),
                   par, par, par],
        scratch_shapes=[pltpu.VMEM((n, gw), F32)] + [pltpu.VMEM((blk, gw), F32)] * 4,
        compiler_params=_cparams("parallel", "arbitrary"),
    )(*args)


def _out_proj(y, w_out, h, next_norm, name):
    if next_norm is None:
        return _matmul(y, w_out, add=h, name=name), None
    M, K = y.shape
    N = w_out.shape[1]
    tm = 512

    def body(y_ref, w_ref, h_ref, n_ref, o_ref, u_ref):
        acc = jnp.dot(y_ref[...], w_ref[...], preferred_element_type=F32) + h_ref[...]
        o_ref[...] = acc
        u_ref[...] = _rms(acc, n_ref[...]).astype(BF16)

    rows = pl.BlockSpec((tm, N), lambda i: (i, 0))
    return pl.pallas_call(
        body, name=name, grid=(M // tm,),
        in_specs=[pl.BlockSpec((tm, K), lambda i: (i, 0)), pl.BlockSpec((K, N), lambda i: (0, 0)), rows,
                  pl.BlockSpec((1, N), lambda i: (0, 0))],
        out_specs=[rows, rows], out_shape=[jax.ShapeDtypeStruct((M, N), F32), jax.ShapeDtypeStruct((M, N), BF16)],
        compiler_params=_cparams("parallel"),
    )(y, w_out, h, next_norm)


def _norm_fwd(h, w, name):
    d = h.shape[1]
    return _row_kernel(lambda hv, wv: (_rms(hv, wv),), [(h, d, 0)], [w], [(d, BF16)], [], name=name)[0]


def _matmul_norm_bwd(a, b, h, w, dh_in, *, name, add=None, b_cols=None, after=None):
    M, K = a.shape
    N = b.shape[0]
    cb = 0 if b_cols is None else b_cols[0]
    tm = 512 if K <= 4096 else 256

    def body(*refs):
        a_ref, b_ref, h_ref, w_ref, dh_ref = refs[:5]
        o32_ref, o16_ref, dw_ref = refs[-3:]
        du = lax.dot_general(a_ref[...], b_ref[...], NT_DIMS, preferred_element_type=F32)
        if add is not None:
            du = du + refs[5][...]
        _, vjp = jax.vjp(_rms, h_ref[...], w_ref[...])
        dh, dw = vjp(du)
        dh = dh + dh_ref[...]
        o32_ref[...] = dh
        o16_ref[...] = dh.astype(BF16)

        @pl.when(pl.program_id(0) == 0)
        def _():
            dw_ref[...] = jnp.zeros(dw_ref.shape, F32)

        dw_ref[...] += dw

    rows = pl.BlockSpec((tm, N), lambda i: (i, 0))
    in_specs = [pl.BlockSpec((tm, K), lambda i: (i, 0)), pl.BlockSpec((N, K), lambda i: (0, cb)), rows,
                pl.BlockSpec((1, N), lambda i: (0, 0)), rows]
    args = [a, b, h, w, dh_in]
    if add is not None:
        in_specs.append(rows)
        args.append(add)
    if after is not None:
        in_specs.append(pl.BlockSpec(memory_space=pl.ANY))
        args.append(after)
    return pl.pallas_call(
        body, name=name, grid=(M // tm,), in_specs=in_specs,
        out_specs=[rows, rows, pl.BlockSpec((1, N), lambda i: (0, 0))],
        out_shape=[jax.ShapeDtypeStruct((M, N), F32), jax.ShapeDtypeStruct((M, N), BF16),
                   jax.ShapeDtypeStruct((1, N), F32)],
        compiler_params=_cparams("arbitrary"),
    )(*args)


def _ffn_fwd(h, norm_w, w_in, w_out, conv_w, conv_b, tag, u, next_norm=None):
    pg = _matmul(u, w_in, b_cols=(0, D_FF), name=f"{tag}_in_gate")
    pv = _matmul(u, w_in, b_cols=(1, D_FF), name=f"{tag}_in_val")
    y = _col_kernel(lambda g, v, w, b: (_ffn_mid_fn(g, v, w, b),), [(pg, 0), (pv, 0)], [(conv_w, 0), (conv_b, 0)],
                    [BF16], [], name=f"{tag}_mid", n_tiles=D_FF // COL_TILE)[0]
    h_out, u_next = _out_proj(y, w_out, h, next_norm, f"{tag}_out")
    return h_out, (u, pg, pv, y), u_next


def _ffn_bwd(h, dh, dh16, saved, norm_w, w_in, w_out, conv_w, conv_b, tag, after=None):
    u, pg, pv, y = saved
    dy = _matmul(dh16, w_out, nt=True, after=after, name=f"{tag}_out_dx")
    dw_out = _matmul(y, dh16, ta=True, name=f"{tag}_out_dw")

    def fn(g, v, ct, w, b):
        _, vjp = jax.vjp(_ffn_mid_fn, g, v, w, b)
        return vjp(ct)

    dpg, dpv, dcw, dcb = _col_kernel(fn, [(pg, 0), (pv, 0), (dy, 0)], [(conv_w, 0), (conv_b, 0)], [BF16, BF16],
                                     [FFN_CONV, 1], name=f"{tag}_mid_bwd", n_tiles=D_FF // COL_TILE)
    du = _matmul(dpg, w_in, nt=True, b_cols=(0, D_FF), name=f"{tag}_gate_dx")
    dw_gate = _matmul(u, dpg, ta=True, name=f"{tag}_gate_dw")
    dw_val = _matmul(u, dpv, ta=True, name=f"{tag}_val_dw")
    dh, dh16, dnw = _matmul_norm_bwd(dpv, w_in, h, norm_w, dh, add=du, b_cols=(1, D_FF), name=f"{tag}_val_dx_norm")
    return dh, dh16, dict(w_in=(dw_gate, dw_val), w_out=dw_out, conv_w=dcw, conv_b=dcb, norm=dnw)


def _hgrn2_fwd(h, norm_w, w_in, lb_logits, a_norm_w, w_out, after=None, next_norm=None):
    d = D_MODEL
    u = _norm_fwd(h, norm_w, "hg_norm")
    pa = _matmul(u, w_in, after=after, name="hg_in")
    qs, lf_fw, k_fw, lf_bw, k_bw = _row_kernel(
        _hgrn2_pre_fn, [(pa, d, 0), (pa, d, 1), (pa, d, 2)], [lb_logits], [(d, F32)] * 5, [], name="hg_pre")
    o_fw, st_fw = _gla_fwd(qs, k_fw, pa, 3 * HG_HEADS, lf_fw, rev=False, add=None, name="hg_gla_fw")
    o, st_bw = _gla_fwd(qs, k_bw, pa, 3 * HG_HEADS, lf_bw, rev=True, add=o_fw, name="hg_gla_bw")
    y = _row_kernel(lambda ov, gv, wv: (_hgrn2_post_fn(ov, gv, wv),), [(o, d, 0), (pa, d, 4)], [a_norm_w],
                    [(d, BF16)], [], name="hg_post")[0]
    h_out, u_next = _out_proj(y, w_out, h, next_norm, "hg_out")
    return h_out, (u, pa, qs, lf_fw, k_fw, lf_bw, k_bw, st_fw, st_bw, o, y), u_next


def _hgrn2_bwd(h, dh, dh16, saved, norm_w, w_in, lb_logits, a_norm_w, w_out, send, after=None):
    d = D_MODEL
    u, pa, qs, lf_fw, k_fw, lf_bw, k_bw, st_fw, st_bw, o, y = saved
    dy = _matmul(dh16, w_out, nt=True, after=after, name="hg_out_dx")
    dw_out = _matmul(y, dh16, ta=True, name="hg_out_dw")

    def post_bwd(ov, gv, ct, wv):
        _, vjp = jax.vjp(_hgrn2_post_fn, ov, gv, wv)
        return vjp(ct)

    do, dg, dnw = _row_kernel(post_bwd, [(o, d, 0), (pa, d, 4), (dy, d, 0)], [a_norm_w], [(d, F32), (d, F32)],
                              [(1, HG_HEAD_DIM)], name="hg_post_bwd")
    dq1, dk_fw, dv1, dlf_fw = _gla_bwd(qs, k_fw, pa, 3 * HG_HEADS, lf_fw, do, st_fw, rev=False, adds=None,
                                       name="hg_gla_fw_bwd")
    dqs, dk_bw, dv, dlf_bw = _gla_bwd(qs, k_bw, pa, 3 * HG_HEADS, lf_bw, do, st_bw, rev=True, adds=(dq1, dv1),
                                      name="hg_gla_bw_bwd")

    def pre_bwd(qr, fr, br, c0, c1, c2, c3, c4, dvv, dgv, lbl):
        _, vjp = jax.vjp(_hgrn2_pre_fn, qr, fr, br, lbl)
        dq, df, db, dlbl = vjp((c0, c1, c2, c3, c4))
        return jnp.concatenate([dq, df, db, dvv, dgv], axis=1), dlbl

    rows = [(pa, d, 0), (pa, d, 1), (pa, d, 2), (dqs, d, 0), (dlf_fw, d, 0), (dk_fw, d, 0), (dlf_bw, d, 0),
            (dk_bw, d, 0), (dv, d, 0), (dg, d, 0)]
    dpa, dlbl = _row_kernel(pre_bwd, rows, [lb_logits], [(5 * d, BF16)], [lb_logits.shape], name="hg_pre_bwd")
    dw_in = _matmul(u, dpa, ta=True, name="hg_in_dw")
    token = send(dw_in, dw_out)
    dh, dh16, dn1 = _matmul_norm_bwd(dpa, w_in, h, norm_w, dh, after=token, name="hg_in_dx_norm")
    return dh, dh16, dict(lb=dlbl, a_norm=dnw, norm=dn1)


def _group_params(p):
    return p.reshape(SSM_GROUPS, 1, SSM_HPG)


def _mamba_fwd(h, norm_w, w_z, w_xbc, w_dt, conv_w, conv_b, dt_bias, a_log, d_skip, b_norm_w, w_out, u, next_norm):
    L = h.shape[0]
    z = _matmul(u, w_z, name="mb_in_z")
    xbc_raw = _matmul(u, w_xbc, name="mb_in_xbc")
    dt_raw = _matmul(u, w_dt, name="mb_in_dt")
    xbc = _col_kernel(lambda xv, w, b: (_mamba_conv_fn(xv, w, b),), [(xbc_raw, 0)], [(conv_w, 0), (conv_b, 0)],
                      [F32], [], name="mb_conv", n_tiles=CONV_DIM // COL_TILE)[0]
    dtr = dt_raw.reshape(L, 2, SSM_GROUPS, SSM_HPG).transpose(1, 2, 0, 3)
    bias, alog = dt_bias.reshape(2, -1), a_log.reshape(2, -1)
    dsk = _group_params(d_skip.reshape(-1))
    y_fw, st_fw = _ssd_fwd(xbc, dtr[0], _group_params(bias[0]), _group_params(alog[0]), dsk, rev=False, skip=True,
                           add=None, name="mb_ssd_fw")
    ysum, st_bw = _ssd_fwd(xbc, dtr[1], _group_params(bias[1]), _group_params(alog[1]), dsk, rev=True, skip=False,
                           add=y_fw, name="mb_ssd_bw")
    y = _row_kernel(lambda yv, zv, wv: (_mamba_post_fn(yv, zv, wv),), [(ysum, D_INNER, 0), (z, D_INNER, 0)],
                    [b_norm_w], [(D_INNER, BF16)], [], name="mb_post")[0]
    h_out, u_next = _out_proj(y, w_out, h, next_norm, "mb_out")
    return h_out, (u, z, xbc_raw, xbc, dtr, st_fw, st_bw, ysum, y), u_next


def _mamba_bwd(h, dh, dh16, saved, norm_w, w_z, w_xbc, w_dt, conv_w, conv_b, dt_bias, a_log, d_skip, b_norm_w, w_out,
               after=None):
    L = h.shape[0]
    u, z, xbc_raw, xbc, dtr, st_fw, st_bw, ysum, y = saved
    dy = _matmul(dh16, w_out, nt=True, after=after, name="mb_out_dx")
    dw_out = _matmul(y, dh16, ta=True, name="mb_out_dw")

    def post_bwd(yv, zv, ct, wv):
        _, vjp = jax.vjp(_mamba_post_fn, yv, zv, wv)
        return vjp(ct)

    dys, dz, dbn = _row_kernel(post_bwd, [(ysum, D_INNER, 0), (z, D_INNER, 0), (dy, D_INNER, 0)], [b_norm_w],
                               [(D_INNER, F32), (D_INNER, BF16)], [(1, D_INNER)], name="mb_post_bwd")
    bias, alog = dt_bias.reshape(2, -1), a_log.reshape(2, -1)
    dsk = _group_params(d_skip.reshape(-1))
    dx1, db1, dc1, ddt_fw, dbias_fw, dalog_fw, ddsk = _ssd_bwd(
        xbc, dtr[0], _group_params(bias[0]), _group_params(alog[0]), dsk, dys, st_fw, rev=False, skip=True,
        adds=None, name="mb_ssd_fw_bwd")
    dx, db, dc, ddt_bw, dbias_bw, dalog_bw, _ = _ssd_bwd(
        xbc, dtr[1], _group_params(bias[1]), _group_params(alog[1]), dsk, dys, st_bw, rev=True, skip=False,
        adds=(dx1, db1, dc1), name="mb_ssd_bw_bwd")

    def conv_bwd(n_tiles, ct, first):
        def fn(xv, ctv, w, b):
            _, vjp = jax.vjp(_mamba_conv_fn, xv, w, b)
            return vjp(ctv)
        return _col_kernel(fn, [(xbc_raw, first), (ct, 0)], [(conv_w, first), (conv_b, first)], [BF16],
                           [SSM_CONV, 1], name=f"mb_conv_bwd_{first}", n_tiles=n_tiles)

    nx, nb = D_INNER // COL_TILE, GN // COL_TILE
    parts = [conv_bwd(nx, dx, 0), conv_bwd(nb, db, nx), conv_bwd(nb, dc, nx + nb)]
    dxbc = jnp.concatenate([p[0] for p in parts], axis=1)
    dcw = jnp.concatenate([p[1] for p in parts], axis=1)
    dcb = jnp.concatenate([p[2] for p in parts], axis=1)
    ddt = jnp.stack([ddt_fw, ddt_bw]).transpose(2, 0, 1, 3).reshape(L, 2 * SSM_GROUPS * SSM_HPG).astype(BF16)
    du = _matmul(dz, w_z, nt=True, name="mb_z_dx")
    du = _matmul(dxbc, w_xbc, nt=True, add=du, name="mb_xbc_dx")
    dw_in = jnp.concatenate([_matmul(u, dz, ta=True, name="mb_z_dw"), _matmul(u, dxbc, ta=True, name="mb_xbc_dw"),
                             _matmul(u, ddt, ta=True, name="mb_dt_dw")], axis=1)
    dh, dh16, dn1 = _matmul_norm_bwd(ddt, w_dt, h, norm_w, dh, add=du, name="mb_dt_dx_norm")
    grads = dict(w_in=dw_in, w_out=dw_out, conv_w=dcw, conv_b=dcb, b_norm=dbn, norm=dn1,
                 dt_bias=jnp.stack([dbias_fw, dbias_bw]).reshape(1, 2, -1),
                 a_log=jnp.stack([dalog_fw, dalog_bw]).reshape(1, 2, -1), d_skip=ddsk.reshape(1, -1))
    return dh, dh16, grads


def _loss_head(h, target, w):
    d = h.shape[1]

    def fn(hv, tv, wv):
        def loss(hv, wv):
            err = _rms(hv, wv) - tv
            return 0.5 * jnp.sum(jnp.mean(err * err, axis=-1, keepdims=True), axis=0, keepdims=True)
        val, vjp = jax.vjp(loss, hv, wv)
        dh, dw = vjp(jnp.ones((1, 1), F32))
        return dh, dh, val, dw

    return _row_kernel(fn, [(h, d, 0), (target, d, 0)], [w], [(d, F32), (d, BF16)], [(1, 1), (1, d)], name="loss_head")


def _local_step(x, target, rep, comm):
    def ffn_args(i, w, small):
        return (rep["norm2_w"][i:i + 1], w["w_in"], w["w_out"], small["ffn_conv_w"][i], rep["ffn_conv_b"][i:i + 1])

    def mamba_args(w, small):
        w_in = w["w_in"]
        return (rep["norm1_w"][1:2], w_in[:, :D_INNER], w_in[:, D_INNER:D_INNER + CONV_DIM],
                w_in[:, D_INNER + CONV_DIM:], small["b_conv_w"][0], small["b_conv_b"], rep["b_dt_bias"],
                rep["b_a_log"], rep["b_d_skip"], small["b_norm_w"], w["w_out"])

    w_hg = comm.weights("hg", None)
    hg = (rep["norm1_w"][0:1], w_hg["w_in"], rep["a_lb_logits"], rep["a_norm_w"], w_hg["w_out"])
    h0 = x
    h1, s_hg, u1 = _hgrn2_fwd(h0, *hg, after=w_hg.get("token"), next_norm=rep["norm2_w"][0:1])
    w_f0 = comm.weights("f0", h1)
    small = w_f0["small"]
    f0 = ffn_args(0, w_f0, small)
    h2, s_f0, u2 = _ffn_fwd(h1, *f0, "ffn0", u1, next_norm=rep["norm1_w"][1:2])
    mb = mamba_args(comm.weights("mb", h2), small)
    h3, s_mb, u3 = _mamba_fwd(h2, *mb, u2, rep["norm2_w"][1:2])
    f1 = ffn_args(1, comm.weights("f1", h3), small)
    h4, s_f1, _ = _ffn_fwd(h3, *f1, "ffn1", u3)
    dh, dh16, loss, d_final = _loss_head(h4, target, rep["final_norm_w"].reshape(1, -1))

    dh, dh16, g_f1 = _ffn_bwd(h3, dh, dh16, s_f1, *f1, "ffn1")
    token = comm.send("f1", dict(w_in=g_f1["w_in"], w_out=g_f1["w_out"]))
    dh, dh16, g_mb = _mamba_bwd(h2, dh, dh16, s_mb, *mb, after=token)
    token = comm.send("mb", dict(w_in=g_mb["w_in"], w_out=g_mb["w_out"]))
    dh, dh16, g_f0 = _ffn_bwd(h1, dh, dh16, s_f0, *f0, "ffn0", after=token)
    token = comm.send("f0", dict(w_in=g_f0["w_in"], w_out=g_f0["w_out"]))
    small_grads = dict(b_conv_w=g_mb["conv_w"][None], b_conv_b=g_mb["conv_b"], b_norm_w=g_mb["b_norm"],
                       ffn_conv_w=jnp.stack([g_f0["conv_w"], g_f1["conv_w"]]))
    dh, dh16, g_hg = _hgrn2_bwd(
        h0, dh, dh16, s_hg, *hg, after=token,
        send=lambda dw_in, dw_out: comm.send("hg", dict(w_in=dw_in, w_out=dw_out, small=small_grads)))
    grads = dict(
        norm1_w=jnp.concatenate([g_hg["norm"], g_mb["norm"]], axis=0),
        norm2_w=jnp.concatenate([g_f0["norm"], g_f1["norm"]], axis=0),
        a_lb_logits=g_hg["lb"], a_norm_w=g_hg["a_norm"], b_dt_bias=g_mb["dt_bias"], b_a_log=g_mb["a_log"],
        b_d_skip=g_mb["d_skip"], ffn_conv_b=jnp.concatenate([g_f0["conv_b"], g_f1["conv_b"]], axis=0),
        final_norm_w=d_final.reshape(-1),
    )
    return loss, dh, grads


def _mesh_pos():
    return lax.axis_index("x"), lax.axis_index("y"), lax.axis_index("c")


N_COPIES = N_DEV - 1


def _comm_call(body, ins, out_shape, name):
    n = len(ins)
    hbm = pl.BlockSpec(memory_space=pl.ANY)
    return pl.pallas_call(
        body, name=name, out_shape=out_shape, in_specs=[hbm] * n, out_specs=[hbm] * n,
        scratch_shapes=[pltpu.SemaphoreType.DMA((n * N_COPIES,)), pltpu.SemaphoreType.DMA((n * N_COPIES,)),
                        pltpu.SemaphoreType.DMA((n,))],
    )(*ins)


def _all_gather(shards, name):
    n = len(shards)

    def body(*refs):
        x_refs, out_refs = refs[:n], refs[n:2 * n]
        send_sems, recv_sems, local_sems = refs[2 * n:]
        x, y, c = _mesh_pos()
        me, sibling = (x, y, c), (x, y, 1 - c)
        chips = [(1 - x, y), (x, 1 - y), (1 - x, 1 - y)]

        def copy(w, k, block, to, own=False):
            px, py, pc = block
            dst = out_refs[w].at[4 * px + 2 * py + pc]
            return pltpu.make_async_remote_copy(
                src_ref=x_refs[w] if own else dst, dst_ref=dst, send_sem=send_sems.at[w * N_COPIES + k],
                recv_sem=recv_sems.at[w * N_COPIES + k], device_id=to, device_id_type=MESH)

        mine = [pltpu.make_async_copy(x_refs[w], out_refs[w].at[4 * x + 2 * y + c], local_sems.at[w]) for w in range(n)]
        for cp in mine:
            cp.start()
        first = [copy(w, 1 + j, me, (*chip, c), own=True) for j, chip in enumerate(chips) for w in range(n)]
        first += [copy(w, 0, me, sibling, own=True) for w in range(n)]
        for cp in first:
            cp.start()
        passed = []
        for j, chip in enumerate(chips):
            for w in range(n):
                copy(w, 1 + j, (*chip, c), me).wait_recv()
                passed.append(copy(w, 4 + j, (*chip, c), sibling))
                passed[-1].start()
        for w in range(n):
            copy(w, 0, sibling, me).wait_recv()
        for j, chip in enumerate(chips):
            for w in range(n):
                copy(w, 4 + j, (*chip, 1 - c), me).wait_recv()
        for cp in first + passed:
            cp.wait_send()
        for cp in mine:
            cp.wait()

    out_shape = [jax.ShapeDtypeStruct((N_DEV, *s.shape), s.dtype) for s in shards]
    return _comm_call(body, shards, out_shape, name)


HBM_SPEC = pl.BlockSpec(memory_space=pltpu.HBM)
SEM_SPEC = pl.BlockSpec(memory_space=pltpu.SEMAPHORE)
EFFECT = pltpu.SideEffectType.DATAFLOW_SIDE_EFFECTING
PEER_ORDER = (4, 2, 6, 5, 3, 7, 1)


def _peer_copies(src_refs, land_refs, send_sems, recv_sems, gather):
    x, y, c = _mesh_pos()
    me = 4 * x + 2 * y + c
    copies = []
    for k in PEER_ORDER:
        px = 1 - x if k & 4 else x
        py = 1 - y if k & 2 else y
        pc = 1 - c if k & 1 else c
        for w, (src, land) in enumerate(zip(src_refs, land_refs)):
            copies.append(pltpu.make_async_remote_copy(
                src_ref=src if gather else src.at[4 * px + 2 * py + pc],
                dst_ref=land.at[me] if gather else land.at[k - 1],
                send_sem=send_sems.at[w * N_COPIES + k - 1], recv_sem=recv_sems.at[w * N_COPIES + k - 1],
                device_id=(px, py, pc), device_id_type=MESH))
    return copies


def _all_gather_small(shard, name):
    def body(x_ref, out_ref, send_sems, recv_sems):
        x, y, c = _mesh_pos()
        out_ref[4 * x + 2 * y + c] = x_ref[...]
        copies = _peer_copies([x_ref], [out_ref], send_sems, recv_sems, True)
        for cp in copies:
            cp.start()
        for cp in copies:
            cp.wait()

    vmem = pl.BlockSpec(memory_space=pltpu.VMEM)
    return pl.pallas_call(
        body, name=name, out_shape=jax.ShapeDtypeStruct((N_DEV, *shard.shape), shard.dtype), in_specs=[vmem],
        out_specs=vmem, scratch_shapes=[pltpu.SemaphoreType.DMA((N_COPIES,)), pltpu.SemaphoreType.DMA((N_COPIES,))],
    )(shard)


def _copies_start(srcs, *, gather, after, name):
    n = len(srcs)
    lands = [lax.empty(((N_DEV,) + s.shape) if gather else ((N_COPIES,) + s.shape[1:]), s.dtype) for s in srcs]

    def body(*refs):
        src_refs, land_refs = refs[:n], refs[n:2 * n]
        send_sems, recv_sems = refs[-2 * n - 3], refs[-2 * n - 2]
        for cp in _peer_copies(src_refs, land_refs, send_sems, recv_sems, gather):
            cp.start()
        refs[-1][...] = jnp.zeros(refs[-1].shape, F32)

    ins = [pltpu.with_memory_space_constraint(a, pltpu.HBM) for a in srcs + lands]
    in_specs = [HBM_SPEC] * (2 * n)
    if after is not None:
        ins.append(after)
        in_specs.append(pl.BlockSpec(memory_space=pl.ANY))
    sems = pltpu.SemaphoreType.DMA((n * N_COPIES,))
    out = pl.pallas_call(
        body, name=name,
        out_shape=(sems, sems, *[pltpu.HBM(a.shape, a.dtype) for a in srcs + lands], jax.ShapeDtypeStruct((8, LANES), F32)),
        in_specs=in_specs,
        out_specs=(SEM_SPEC, SEM_SPEC, *[HBM_SPEC] * (2 * n), pl.BlockSpec(memory_space=pltpu.VMEM)),
        input_output_aliases={i: 2 + i for i in range(2 * n)},
        compiler_params=pltpu.CompilerParams(has_side_effects=EFFECT),
    )(*ins)
    return out[0], out[1], list(out[2:2 + n]), list(out[2 + n:2 + 2 * n]), out[-1]


def _copies_wait(started, *, gather, after, name):
    send_sems, recv_sems, srcs, lands, _ = started
    n = len(srcs)

    def body(*refs):
        src_refs, land_refs = refs[:n], refs[n:2 * n]
        for cp in _peer_copies(src_refs, land_refs, refs[2 * n], refs[2 * n + 1], gather):
            cp.wait_send()
            cp.wait_recv()

    out = pl.pallas_call(
        body, name=name, out_shape=tuple(pltpu.HBM(a.shape, a.dtype) for a in srcs + lands),
        in_specs=[HBM_SPEC] * (2 * n) + [SEM_SPEC, SEM_SPEC, pl.BlockSpec(memory_space=pl.ANY)],
        out_specs=tuple([HBM_SPEC] * (2 * n)), input_output_aliases={i: i for i in range(2 * n)},
        compiler_params=pltpu.CompilerParams(has_side_effects=EFFECT),
    )(*srcs, *lands, send_sems, recv_sems, after)
    return list(out[n:])


def _adamw(got, w, m, v, *, name, tile, own=None, layer=None, into=None):
    rows, width = w.shape[-2:]
    n_got = got.shape[0]
    c1 = 1.0 / (1.0 - ADAM_B1 ** ADAM_STEP)
    c2 = 1.0 / (1.0 - ADAM_B2 ** ADAM_STEP)

    def body(*refs):
        got_ref, w_ref, m_ref, v_ref = refs[:4]
        g_ref, d_ref, nm_ref, nv_ref = refs[-4:]
        g = got_ref[0] if own is None else refs[4][...] + got_ref[0]
        for s in range(1, n_got):
            g = g + got_ref[s]
        m_new = ADAM_B1 * m_ref[...] + (1.0 - ADAM_B1) * g
        v_new = ADAM_B2 * v_ref[...] + (1.0 - ADAM_B2) * (g * g)
        g_ref[...] = g
        nm_ref[...] = m_new
        nv_ref[...] = v_new
        d_ref[...] = -ADAM_LR * ((m_new * c1) / (jnp.sqrt(v_new * c2) + ADAM_EPS) + ADAM_WD * w_ref[...])

    spec = pl.BlockSpec((tile, width), lambda i: (i, 0))
    wspec = spec if layer is None else pl.BlockSpec((None, tile, width), lambda i: (layer, i, 0))
    args = [got, w, m, v] + ([] if own is None else [own])
    in_specs = [pl.BlockSpec((n_got, tile, width), lambda i: (0, i, 0))] + [wspec] * 3 + [spec] * (len(args) - 4)
    aliases = {}
    if into is not None:
        aliases = {len(args) + i: i for i in range(4)}
        args += list(into)
        in_specs += [pl.BlockSpec(memory_space=pl.ANY)] * 4
    return pl.pallas_call(
        body, name=name, grid=(rows // tile,), in_specs=in_specs, out_specs=[wspec] * 4,
        out_shape=[jax.ShapeDtypeStruct(w.shape, F32)] * 4, input_output_aliases=aliases,
        compiler_params=_cparams("parallel"),
    )(*args)


SHARDED = dict(a_w_in=(2, True), a_w_out=(1, True), b_w_in=(2, True), b_w_out=(1, True), ffn_w_in=(2, True),
               ffn_w_out=(1, True), b_conv_w=(2, False), b_conv_b=(1, False), b_norm_w=(1, False), ffn_conv_w=(2, False))
REPLICATED = ("norm1_w", "norm2_w", "a_lb_logits", "a_norm_w", "b_dt_bias", "b_a_log", "b_d_skip", "ffn_conv_b",
              "final_norm_w")
WEIGHTS = ("norm1_w", "norm2_w", "a_w_in", "a_lb_logits", "a_norm_w", "a_w_out", "b_w_in", "b_conv_w", "b_conv_b",
           "b_dt_bias", "b_a_log", "b_d_skip", "b_norm_w", "b_w_out", "ffn_w_in", "ffn_conv_w", "ffn_conv_b",
           "ffn_w_out", "final_norm_w")


def _pad_rows(flat, multiple):
    n = flat.shape[-1]
    per = PACK_W * multiple
    total = -(-n // per) * per
    flat = jnp.pad(flat, [(0, 0)] * (flat.ndim - 1) + [(0, total - n)])
    return flat.reshape(*flat.shape[:-1], total // PACK_W, PACK_W)


def _to_parts(full, axis, n=N_DEV):
    shp = full.shape
    t = full.reshape(*shp[:axis], n, shp[axis] // n, *shp[axis + 1:])
    return jnp.moveaxis(t, axis, 0)


def _from_parts(parts, axis):
    t = jnp.moveaxis(parts, 0, axis)
    shp = t.shape
    return t.reshape(*shp[:axis], shp[axis] * shp[axis + 1], *shp[axis + 2:])


BIG = tuple(n for n, (_, mm) in SHARDED.items() if mm)
SMALL = tuple(n for n, (_, mm) in SHARDED.items() if not mm)
SMALL_W = 512


def _small_rows(tree, lead):
    rows = []
    for n in SMALL:
        t = tree[n]
        t = t.reshape(*lead, -1, t.shape[-1])
        rows.append(jnp.pad(t, [(0, 0)] * (t.ndim - 1) + [(0, SMALL_W - t.shape[-1])]))
    buf = jnp.concatenate(rows, axis=-2)
    return jnp.pad(buf, [(0, 0)] * (buf.ndim - 2) + [(0, 16 - buf.shape[-2]), (0, 0)])


def _small_unrows(buf, like, lead):
    out, r = {}, 0
    for n in SMALL:
        shp = like[n].shape
        k = like[n].size // shp[-1]
        out[n] = buf[..., r:r + k, :shp[-1]].reshape(*lead, *shp)
        r += k
    return out


GROUPS = dict(hg=(("a_w_in", 0), ("a_w_out", 0)), f0=(("ffn_w_in", 0), ("ffn_w_out", 0)),
              mb=(("b_w_in", 0), ("b_w_out", 0)), f1=(("ffn_w_in", 1), ("ffn_w_out", 1)))


class _Comm:
    def __init__(self, local):
        x, y, c = _mesh_pos()
        self.me = 4 * x + 2 * y + c
        self.local = local
        self.shards = {g: [local[n][i].astype(BF16) for n, i in names] for g, names in GROUPS.items()}
        self.shards["f0"].append(_small_rows(local, ()))
        self.first = _all_gather(self.shards["hg"], "gather_hg")
        self.gathers, self.sent, token = {}, {}, None
        for g in ("f0", "mb", "f1"):
            self.gathers[g] = _copies_start(self.shards[g], gather=True, after=token, name=f"gather_{g}_start")
            token = self.gathers[g][-1]
        self.token = token

    def weights(self, group, after):
        if group == "hg":
            got = self.first
        else:
            lands = _copies_wait(self.gathers[group], gather=True, after=after, name=f"gather_{group}_wait")
            got = [lax.dynamic_update_index_in_dim(land, shard, self.me, 0)
                   for land, shard in zip(lands, self.shards[group])]
        out = dict(w_in=_from_parts(got[0], 1), w_out=_from_parts(got[1], 0))
        if group == "hg":
            out["token"] = self.token
        if group == "f0":
            small = _small_unrows(got[2], self.local, (N_DEV,))
            out["small"] = {n: _from_parts(small[n], SHARDED[n][0]) for n in SMALL}
        return out

    def send(self, group, grads):
        w_in = grads["w_in"]
        if isinstance(w_in, tuple):
            half = N_DEV // len(w_in)
            parts_in = jnp.concatenate([_to_parts(t, 1, half) for t in w_in], axis=0)
        else:
            parts_in = _to_parts(w_in, 1)
        parts = [parts_in, _to_parts(grads["w_out"], 0)]
        if "small" in grads:
            parts.append(_small_rows({n: _to_parts(grads["small"][n], SHARDED[n][0]) for n in SMALL}, (N_DEV,)))
        sent = [p.astype(BF16) for p in parts[:2]] + parts[2:] if group == "hg" else parts
        self.sent[group] = (parts, _copies_start(sent, gather=False, after=None, name=f"exchange_{group}_start"))
        return self.sent[group][1][-1]

    def finish(self, after, mom, var):
        res = {}
        for group in ("f1", "mb", "f0", "hg"):
            parts, started = self.sent[group]
            lands = _copies_wait(started, gather=False, after=after, name=f"exchange_{group}_wait")
            own = [lax.dynamic_index_in_dim(p, self.me, 0, keepdims=False) for p in parts]
            for (n, i), got, mine in zip(GROUPS[group], lands, own):
                res[n] = _adamw(got, self.local[n], mom[n], var[n], own=mine, layer=i, into=res.get(n),
                                name=f"adamw_{n}_{i}", tile=_pick(mine.shape[0], (256, 176, 128)))
                after = res[n][0]
        small = _adamw(lands[2], *[_small_rows(t, ()) for t in (self.local, mom, var)], own=own[2],
                       name="adamw_small", tile=16)
        return res, small


def _pack_small(tree, extra):
    flat = jnp.concatenate([tree[n].reshape(-1) for n in REPLICATED] + [extra.reshape(-1)])
    return _pad_rows(flat, 8)


def _unpack_small(pack, like):
    flat, out, off = pack.reshape(-1), {}, 0
    for n in REPLICATED:
        out[n] = flat[off:off + like[n].size].reshape(like[n].shape)
        off += like[n].size
    return out, flat[off]


def kernel(x, norm1_w, norm2_w, a_w_in, a_lb_logits, a_norm_w, a_w_out, b_w_in, b_conv_w, b_conv_b, b_dt_bias, b_a_log, b_d_skip, b_norm_w, b_w_out, ffn_w_in, ffn_conv_w, ffn_conv_b, ffn_w_out, final_norm_w, loss_target, m_norm1_w, m_norm2_w, m_a_w_in, m_a_lb_logits, m_a_norm_w, m_a_w_out, m_b_w_in, m_b_conv_w, m_b_conv_b, m_b_dt_bias, m_b_a_log, m_b_d_skip, m_b_norm_w, m_b_w_out, m_ffn_w_in, m_ffn_conv_w, m_ffn_conv_b, m_ffn_w_out, m_final_norm_w, v_norm1_w, v_norm2_w, v_a_w_in, v_a_lb_logits, v_a_norm_w, v_a_w_out, v_b_w_in, v_b_conv_w, v_b_conv_b, v_b_dt_bias, v_b_a_log, v_b_d_skip, v_b_norm_w, v_b_w_out, v_ffn_w_in, v_ffn_conv_w, v_ffn_conv_b, v_ffn_w_out, v_final_norm_w):
    given = dict(locals())
    local = {n: given[n] for n in WEIGHTS}
    mom = {n: given["m_" + n] for n in WEIGHTS}
    var = {n: given["v_" + n] for n in WEIGHTS}

    comm = _Comm(local)
    loss, grad_x, grads = _local_step(x[0], loss_target[0], local, comm)

    res, small_res = comm.finish(grad_x, mom, var)
    outs = ({}, {}, {}, {})
    for n in BIG:
        for out, r in zip(outs, res[n]):
            out[n] = r
    for out, r in zip(outs, small_res):
        out.update(_small_unrows(r, local, ()))
    out_g, out_d, out_m, out_v = outs

    small = _pack_small(grads, loss)
    rows = small.shape[0]
    got_s = _all_gather_small(small, "gather_small")
    zero = jnp.zeros((1,), F32)
    g, dlt, nm, nv = _adamw(got_s, _pack_small(local, zero), _pack_small(mom, zero), _pack_small(var, zero),
                            name="adamw_replicated", tile=rows)
    (rep_g, total), (rep_d, _), (rep_m, _), (rep_v, _) = (_unpack_small(t, local) for t in (g, dlt, nm, nv))
    out_g.update(rep_g)
    out_d.update(rep_d)
    out_m.update(rep_m)
    out_v.update(rep_v)

    return (total, grad_x[None], *[out_g[n] for n in WEIGHTS], *[out_d[n] for n in WEIGHTS],
            *[out_m[n] for n in WEIGHTS], *[out_v[n] for n in WEIGHTS])
```

```python
import functools

import jax
import jax.numpy as jnp
from jax import lax
from jax.experimental import pallas as pl
from jax.experimental.pallas import tpu as pltpu

F32 = jnp.float32
BF16 = jnp.bfloat16
HIGHEST = lax.Precision.HIGHEST
MESH = pl.DeviceIdType.MESH

N_DEV = 8
EPS = 1e-6
D_MODEL = 1024
HG_HEADS = 8
HG_HEAD_DIM = 128
D_INNER = 2048
SSM_HEAD_DIM = 64
SSM_GROUPS = 8
SSM_HPG = 4
SSM_STATE = 128
GN = SSM_GROUPS * SSM_STATE
CONV_DIM = D_INNER + 2 * GN
D_FF = 2816
SSM_CONV = 5
FFN_CONV = 3

ADAM_LR = 0.001
ADAM_B1 = 0.9
ADAM_B2 = 0.999
ADAM_EPS = 1e-08
ADAM_WD = 0.01
ADAM_STEP = 10

LANES = 128
ROW_TILE = 256
COL_TILE = 128
GLA_CHUNK = 16
GLA_BLOCK = 256
GLA_HEADS_PER_STEP = 8
SSD_CHUNK = 128
SSD_BLOCK = 512
PACK_W = 1024
VMEM_LIMIT = 56 * 1024 * 1024

NT_DIMS = (((1,), (1,)), ((), ()))
TN_DIMS = (((0,), (0,)), ((), ()))


def _cparams(*sem):
    return pltpu.CompilerParams(dimension_semantics=sem, vmem_limit_bytes=VMEM_LIMIT)


def _rms(x, w):
    return x * lax.rsqrt(jnp.mean(x * x, axis=-1, keepdims=True) + EPS) * w


def _row_kernel(body_fn, rows, params, row_outs, acc_outs, *, name, tile=ROW_TILE):
    L = rows[0][0].shape[0]
    tile = min(tile, L)
    n_in = len(rows) + len(params)
    n_ro = len(row_outs)

    def body(*refs):
        outs = body_fn(*[r[...] for r in refs[:n_in]])
        for ref, o in zip(refs[n_in:n_in + n_ro], outs[:n_ro]):
            ref[...] = o.astype(ref.dtype)
        first = pl.program_id(0) == 0
        for ref, o in zip(refs[n_in + n_ro:], outs[n_ro:]):
            @pl.when(first)
            def _(ref=ref):
                ref[...] = jnp.zeros(ref.shape, ref.dtype)
            ref[...] += o

    in_specs = [pl.BlockSpec((tile, w), lambda i, cb=cb: (i, cb)) for _, w, cb in rows]
    in_specs += [pl.BlockSpec(p.shape, lambda i: (0, 0)) for p in params]
    out_specs = [pl.BlockSpec((tile, w), lambda i: (i, 0)) for w, _ in row_outs]
    out_specs += [pl.BlockSpec(s, lambda i: (0, 0)) for s in acc_outs]
    out_shape = [jax.ShapeDtypeStruct((L, w), dt) for w, dt in row_outs]
    out_shape += [jax.ShapeDtypeStruct(s, F32) for s in acc_outs]
    return pl.pallas_call(
        body, name=name, grid=(L // tile,), in_specs=in_specs, out_specs=out_specs, out_shape=out_shape,
        compiler_params=_cparams("arbitrary" if acc_outs else "parallel"),
    )(*[a for a, _, _ in rows], *params)


def _col_kernel(body_fn, cols, params, col_outs, par_outs, *, name, n_tiles):
    L = cols[0][0].shape[0]
    n_in = len(cols) + len(params)
    width = n_tiles * COL_TILE

    def body(*refs):
        outs = body_fn(*[r[...] for r in refs[:n_in]])
        for ref, o in zip(refs[n_in:], outs):
            ref[...] = o.astype(ref.dtype)

    in_specs = [pl.BlockSpec((L, COL_TILE), lambda j, cb=cb: (0, cb + j)) for _, cb in cols]
    in_specs += [pl.BlockSpec((p.shape[0], COL_TILE), lambda j, cb=cb: (0, cb + j)) for p, cb in params]
    out_specs = [pl.BlockSpec((L, COL_TILE), lambda j: (0, j)) for _ in col_outs]
    out_specs += [pl.BlockSpec((k, COL_TILE), lambda j: (0, j)) for k in par_outs]
    out_shape = [jax.ShapeDtypeStruct((L, width), dt) for dt in col_outs]
    out_shape += [jax.ShapeDtypeStruct((k, width), F32) for k in par_outs]
    return pl.pallas_call(
        body, name=name, grid=(n_tiles,), in_specs=in_specs, out_specs=out_specs, out_shape=out_shape,
        compiler_params=_cparams("parallel"),
    )(*[a for a, _ in cols], *[p for p, _ in params])


def _pick(n, options):
    for t in options:
        if n % t == 0:
            return t
    return n


MATMUL_VMEM = 40 * 1024 * 1024


def _matmul_tiles(M, N, K, out_bytes):
    best = None
    for tm in (1024, 512, 256, 128, M):
        for tn in (1408, 1024, 512, 256, 128, N):
            if M % tm or N % tn:
                continue
            if 2 * (2 * K * (tm + tn) + out_bytes * tm * tn) > MATMUL_VMEM:
                continue
            if best is None or tm * tn > best[0] * best[1]:
                best = (tm, tn)
    return best


def _matmul(a, b, *, name, nt=False, ta=False, add=None, out_dtype=F32, after=None, b_cols=None):
    K, M = a.shape[::-1] if not ta else a.shape
    cb, width = b_cols if b_cols is not None else (0, b.shape[1])
    N = b.shape[0] if nt else width
    assert width == K or not nt
    tm, tn = _matmul_tiles(M, N, K, 4 * (1 + (add is not None)) if out_dtype == F32 else 2 + 4 * (add is not None))
    col0 = cb if nt else cb * (width // tn)

    def body(*refs):
        a_ref, b_ref = refs[0], refs[1]
        o_ref = refs[-1]
        dims = TN_DIMS if ta else NT_DIMS if nt else (((1,), (0,)), ((), ()))
        acc = lax.dot_general(a_ref[...], b_ref[...], dims, preferred_element_type=F32)
        if add is not None:
            acc = acc + refs[2][...]
        o_ref[...] = acc.astype(o_ref.dtype)

    in_specs = [pl.BlockSpec((K, tm), lambda i, j: (0, i)) if ta else pl.BlockSpec((tm, K), lambda i, j: (i, 0)),
                pl.BlockSpec((tn, K), lambda i, j: (j, col0)) if nt
                else pl.BlockSpec((K, tn), lambda i, j: (0, col0 + j))]
    args = [a, b]
    if add is not None:
        in_specs.append(pl.BlockSpec((tm, tn), lambda i, j: (i, j)))
        args.append(add)
    if after is not None:
        in_specs.append(pl.BlockSpec(memory_space=pl.ANY))
        args.append(after)
    return pl.pallas_call(
        body, name=name, grid=(M // tm, N // tn), in_specs=in_specs,
        out_specs=pl.BlockSpec((tm, tn), lambda i, j: (i, j)),
        out_shape=jax.ShapeDtypeStruct((M, N), out_dtype),
        compiler_params=_cparams("parallel", "parallel"),
    )(*args)


def _hgrn2_pre_fn(q_raw, ffw_raw, fbw_raw, lb_logits):
    lb = jax.nn.softmax(lb_logits, axis=0)[0:1]

    def gate(fr):
        f = lb + (1.0 - lb) * jax.nn.sigmoid(fr)
        return jnp.log(f), 1.0 - f

    lf_fw, k_fw = gate(ffw_raw)
    lf_bw, k_bw = gate(fbw_raw)
    return jax.nn.silu(q_raw), lf_fw, k_fw, lf_bw, k_bw


def _hgrn2_post_fn(o, g, norm_w):
    outs = []
    for h in range(HG_HEADS):
        sl = slice(h * HG_HEAD_DIM, (h + 1) * HG_HEAD_DIM)
        outs.append(_rms(o[:, sl], norm_w) * jax.nn.silu(g[:, sl]))
    return jnp.concatenate(outs, axis=1)


def _mamba_post_fn(y, z, norm_w):
    outs = []
    gw = D_INNER // SSM_GROUPS
    for gi in range(SSM_GROUPS):
        sl = slice(gi * gw, (gi + 1) * gw)
        outs.append(_rms(y[:, sl] * jax.nn.silu(z[:, sl]), norm_w[:, sl]))
    return jnp.concatenate(outs, axis=1)


def _shift_rows_impl(x, d):
    if d == 0:
        return x
    n, edge = x.shape[0], 8
    t = lax.broadcasted_iota(jnp.int32, (edge, x.shape[1]), 0)
    rolled = pltpu.roll(x, d % n, 0)
    if d > 0:
        return jnp.concatenate([jnp.where(t >= d, rolled[:edge], 0.0), rolled[edge:]], axis=0)
    return jnp.concatenate([rolled[:n - edge], jnp.where(t < edge + d, rolled[n - edge:], 0.0)], axis=0)


@functools.partial(jax.custom_vjp, nondiff_argnums=(1,))
def _shift_rows(x, d):
    return _shift_rows_impl(x, d)


_shift_rows.defvjp(lambda x, d: (_shift_rows_impl(x, d), None), lambda d, _, g: (_shift_rows_impl(g, -d),))


def _dwconv(x, w, b):
    taps = w.shape[0]
    c = (taps - 1) // 2
    y = b + w[0:1, :] * _shift_rows(x, c)
    for k in range(1, taps):
        y = y + w[k:k + 1, :] * _shift_rows(x, c - k)
    return y


def _ffn_mid_fn(gate, val, w, b):
    return jax.nn.silu(_dwconv(gate, w, b)) * val


def _mamba_conv_fn(xbc, w, b):
    return jax.nn.silu(_dwconv(xbc, w, b))


def _gla_consts(rev):
    return lax.broadcasted_iota(jnp.int32, (GLA_CHUNK, HG_HEAD_DIM), 0)


def _segment_cumsum(x, seg, rev):
    n = x.shape[0]
    t = lax.broadcasted_iota(jnp.int32, x.shape, 0) & (seg - 1)
    s = 1
    while s < seg:
        if rev:
            x = x + jnp.where(t < seg - s, pltpu.roll(x, n - s, 0), 0.0)
        else:
            x = x + jnp.where(t >= s, pltpu.roll(x, s, 0), 0.0)
        s *= 2
    return x


def _segment_cumsum_mxu(x, seg, rev):
    r = lax.broadcasted_iota(jnp.int32, (seg, seg), 0)
    c = lax.broadcasted_iota(jnp.int32, (seg, seg), 1)
    tri = ((r <= c) if rev else (r >= c)).astype(BF16)
    tri3 = jnp.concatenate([tri, tri, tri], axis=1)
    hi = x.astype(BF16)
    rest = x - hi.astype(F32)
    mid = rest.astype(BF16)
    lo = (rest - mid.astype(F32)).astype(BF16)
    outs = []
    for g in range(x.shape[0] // seg):
        rows = slice(g * seg, (g + 1) * seg)
        terms = jnp.concatenate([hi[rows], mid[rows], lo[rows]], axis=0)
        outs.append(jnp.dot(tri3, terms, preferred_element_type=F32))
    return jnp.concatenate(outs, axis=0)


@functools.partial(jax.custom_vjp, nondiff_argnums=(1, 2))
def _segment_cumsum_diff(x, seg, rev):
    return _segment_cumsum_mxu(x, seg, rev)


_segment_cumsum_diff.defvjp(lambda x, seg, rev: (_segment_cumsum_mxu(x, seg, rev), None),
                            lambda seg, rev, _, g: (_segment_cumsum_mxu(g, seg, not rev),))


def _gla_chunk(st, q, k, v, b, *, rev, consts, halves):
    row = consts
    c = q.shape[0]
    half = c // 2 if halves else c
    o = lax.dot_general((q * jnp.exp(b)).astype(BF16), st.astype(BF16), NT_DIMS, preferred_element_type=F32)
    spans = [slice(i, i + half) for i in range(0, c, half)]
    qs, bs, rows, os = [q[p] for p in spans], [b[p] for p in spans], [row[p] for p in spans], [o[p] for p in spans]

    def reach(s, h, masked):
        diff = bs[h] - b[s:s + 1]
        if masked:
            diff = jnp.where((rows[h] <= s) if rev else (rows[h] >= s), diff, -jnp.inf)
        return jnp.sum(qs[h] * (k[s:s + 1] * jnp.exp(diff)), axis=-1, keepdims=True) * v[s:s + 1]

    for s in range(c):
        own = s // half
        os[own] = os[own] + reach(s, own, True)
        if halves and own == (1 if rev else 0):
            os[1 - own] = os[1 - own] + reach(s, 1 - own, False)
    o = jnp.concatenate(os, axis=0) if halves else os[0]
    b_end = b[0:1] if rev else b[c - 1:c]
    kd = (k * jnp.exp(b_end - b)).astype(BF16)
    st_new = st * jnp.exp(b_end) + lax.dot_general(v.astype(BF16), kd, TN_DIMS, preferred_element_type=F32)
    return st_new, o


def _gla_fwd(q, k, v, v_cb, g, *, rev, add, name):
    L = q.shape[0]
    blk = min(GLA_BLOCK, L)
    nblk, nsub = L // blk, blk // GLA_CHUNK
    hd, hps = HG_HEAD_DIM, GLA_HEADS_PER_STEP
    wide = hd * hps

    def body(*refs):
        q_ref, k_ref, v_ref, g_ref = refs[:4]
        add_ref = refs[4] if add is not None else None
        o_ref, st_out, st_scr, b_scr = refs[-4:]
        consts = _gla_consts(rev)

        @pl.when(pl.program_id(1) == 0)
        def _():
            st_scr[...] = jnp.zeros(st_scr.shape, F32)

        b_scr[...] = _segment_cumsum(g_ref[...], GLA_CHUNK, rev)

        def step(i, carry):
            sl = pl.ds(pl.multiple_of((nsub - 1 - i if rev else i) * GLA_CHUNK, GLA_CHUNK), GLA_CHUNK)
            lanes = [slice(hi * hd, (hi + 1) * hd) for hi in range(hps)]
            ins = [(st_scr[hi], q_ref[sl, ln], k_ref[sl, ln], v_ref[sl, ln], b_scr[sl, ln])
                   for hi, ln in enumerate(lanes)]
            adds = [add_ref[sl, ln] for ln in lanes] if add_ref is not None else None
            outs = [_gla_chunk(*args, rev=rev, consts=consts, halves=True) for args in ins]
            for hi, ln in enumerate(lanes):
                st_out[hi, i] = ins[hi][0]
                st_scr[hi] = outs[hi][0]
                o_ref[sl, ln] = outs[hi][1] if adds is None else outs[hi][1] + adds[hi]
            return carry

        lax.fori_loop(0, nsub, step, 0)

    def pos(j):
        return nblk - 1 - j if rev else j

    spec = pl.BlockSpec((blk, wide), lambda h, j: (pos(j), h))
    in_specs = [spec, spec, pl.BlockSpec((blk, wide), lambda h, j: (pos(j), v_cb // hps + h)), spec]
    args = [q, k, v, g]
    if add is not None:
        in_specs.append(spec)
        args.append(add)
    return pl.pallas_call(
        body, name=name, grid=(HG_HEADS // hps, nblk), in_specs=in_specs,
        out_specs=[spec, pl.BlockSpec((hps, None, nsub, hd, hd), lambda h, j: (h, j, 0, 0, 0))],
        out_shape=[jax.ShapeDtypeStruct((L, HG_HEADS * hd), F32),
                   jax.ShapeDtypeStruct((HG_HEADS, nblk, nsub, hd, hd), F32)],
        scratch_shapes=[pltpu.VMEM((hps, hd, hd), F32), pltpu.VMEM((blk, wide), F32)],
        compiler_params=_cparams("parallel", "arbitrary"),
    )(*args)


def _gla_bwd(q, k, v, v_cb, g, do, states, *, rev, adds, name):
    L = q.shape[0]
    blk = min(GLA_BLOCK, L)
    nblk, nsub = L // blk, blk // GLA_CHUNK
    hd, hps = HG_HEAD_DIM, GLA_HEADS_PER_STEP
    wide = hd * hps
    n_add = 0 if adds is None else 2

    def body(*refs):
        q_ref, k_ref, v_ref, g_ref, do_ref, st_in = refs[:6]
        add_refs = refs[6:6 + n_add]
        dq_ref, dk_ref, dv_ref, dg_ref, dst_scr, b_scr, db_scr = refs[6 + n_add:]
        consts = _gla_consts(rev)
        chunk = functools.partial(_gla_chunk, rev=rev, consts=consts, halves=False)

        @pl.when(pl.program_id(1) == 0)
        def _():
            dst_scr[...] = jnp.zeros(dst_scr.shape, F32)

        b_scr[...] = _segment_cumsum(g_ref[...], GLA_CHUNK, rev)

        def rows(i):
            return pl.ds(pl.multiple_of((nsub - 1 - i if rev else i) * GLA_CHUNK, GLA_CHUNK), GLA_CHUNK)

        def step(n, carry):
            i = nsub - 1 - n
            sl = rows(i)
            lanes = [slice(hi * hd, (hi + 1) * hd) for hi in range(hps)]
            ins = [(st_in[hi, i], q_ref[sl, ln], k_ref[sl, ln], v_ref[sl, ln], b_scr[sl, ln])
                   for hi, ln in enumerate(lanes)]
            cts = [(dst_scr[hi], do_ref[sl, ln]) for hi, ln in enumerate(lanes)]
            adds = [(add_refs[0][sl, ln], add_refs[1][sl, ln]) for ln in lanes] if n_add else None
            outs = [jax.vjp(chunk, *args)[1](ct) for args, ct in zip(ins, cts)]
            for hi, ln in enumerate(lanes):
                dst, dq, dk, dv, db = outs[hi]
                dst_scr[hi] = dst
                dq_ref[sl, ln] = dq if adds is None else dq + adds[hi][0]
                dk_ref[sl, ln] = dk
                dv_ref[sl, ln] = dv if adds is None else dv + adds[hi][1]
                db_scr[sl, ln] = db
            return carry

        lax.fori_loop(0, nsub, step, 0)
        dg_ref[...] = _segment_cumsum(db_scr[...], GLA_CHUNK, not rev)

    def pos(j):
        p = nblk - 1 - j
        return nblk - 1 - p if rev else p

    spec = pl.BlockSpec((blk, wide), lambda h, j: (pos(j), h))
    in_specs = [spec, spec, pl.BlockSpec((blk, wide), lambda h, j: (pos(j), v_cb // hps + h)), spec, spec,
                pl.BlockSpec((hps, None, nsub, hd, hd), lambda h, j: (h, nblk - 1 - j, 0, 0, 0))]
    args = [q, k, v, g, do, states]
    if adds is not None:
        in_specs += [spec, spec]
        args += list(adds)
    out = jax.ShapeDtypeStruct((L, HG_HEADS * hd), F32)
    return pl.pallas_call(
        body, name=name, grid=(HG_HEADS // hps, nblk), in_specs=in_specs,
        out_specs=[spec] * 4, out_shape=[out] * 4,
        scratch_shapes=[pltpu.VMEM((hps, hd, hd), F32), pltpu.VMEM((blk, wide), F32), pltpu.VMEM((blk, wide), F32)],
        compiler_params=_cparams("parallel", "arbitrary"),
    )(*args)


def _ssd_consts(rev):
    c = SSD_CHUNK
    gw = SSM_HPG * SSM_HEAD_DIM
    r2 = lax.broadcasted_iota(jnp.int32, (c, c), 0)
    c2 = lax.broadcasted_iota(jnp.int32, (c, c), 1)
    low = (r2 <= c2) if rev else (r2 >= c2)
    lane = lax.broadcasted_iota(jnp.int32, (1, gw), 1)
    return low, r2 == c2, lane, lane >> 6


def _expand_heads(v4, head_of_lane):
    first = head_of_lane[:, :LANES] == 0
    cols = [v4[:, j:j + 1] for j in range(SSM_HPG)]
    return jnp.concatenate([jnp.where(first, cols[0], cols[1]), jnp.where(first, cols[2], cols[3])], axis=1)


def _ssd_prep(dtr, bias, alog, dsk, *, rev, consts):
    head_of_lane = consts[-1]
    dt_l = _expand_heads(jax.nn.softplus(dtr + bias), head_of_lane)
    a_l = _expand_heads(-jnp.exp(alog), head_of_lane)
    return dt_l, _segment_cumsum_diff(dt_l * a_l, SSD_CHUNK, rev), _expand_heads(dsk, head_of_lane)


def _ssd_chunk(st, x, bm, cm, dt_l, acum, dsk_l, *, rev, consts, skip):
    low, eye, lane, head_of_lane = consts
    c = x.shape[0]
    xd = x * dt_l
    cb = lax.dot_general(cm.astype(BF16), bm.astype(BF16), NT_DIMS, preferred_element_type=F32)
    y = jnp.dot(cm.astype(BF16), st.astype(BF16), preferred_element_type=F32) * jnp.exp(acum)
    for j in range(SSM_HPG):
        acol = jnp.sum(jnp.where(lane == j * SSM_HEAD_DIM, acum, 0.0), axis=-1, keepdims=True)
        ab = jnp.broadcast_to(acol, (c, c))
        arow = jnp.sum(jnp.where(eye, ab, 0.0), axis=0, keepdims=True)
        lmat = jnp.exp(jnp.where(low, ab - arow, -jnp.inf))
        xj = jnp.where(head_of_lane == j, xd, 0.0)
        y = y + jnp.dot((cb * lmat).astype(BF16), xj.astype(BF16), preferred_element_type=F32)
    a_end = acum[0:1] if rev else acum[c - 1:c]
    xdec = (xd * jnp.exp(a_end - acum)).astype(BF16)
    st_new = st * jnp.exp(a_end) + lax.dot_general(bm.astype(BF16), xdec, TN_DIMS, preferred_element_type=F32)
    if skip:
        y = y + x * dsk_l
    return st_new, y


def _ssd_specs(L, rev):
    blk = min(SSD_BLOCK, L)
    nblk = L // blk
    gw = SSM_HPG * SSM_HEAD_DIM

    def pos(j):
        return nblk - 1 - j if rev else j

    return blk, nblk, gw, pos


def _ssd_fwd(xbc, dtr, bias, alog, dsk, *, rev, skip, add, name):
    L = xbc.shape[0]
    blk, nblk, gw, pos = _ssd_specs(L, rev)
    nsub = blk // SSD_CHUNK
    n = SSM_STATE

    def body(*refs):
        x_ref, b_ref, c_ref, dt_ref, bias_ref, alog_ref, dsk_ref = refs[:7]
        add_ref = refs[7] if add is not None else None
        y_ref, st_out, st_scr, dt_scr, ac_scr = refs[-5:]
        consts = _ssd_consts(rev)

        @pl.when(pl.program_id(1) == 0)
        def _():
            st_scr[...] = jnp.zeros(st_scr.shape, F32)

        dt_scr[...], ac_scr[...], dsk_l = _ssd_prep(dt_ref[...], bias_ref[...], alog_ref[...], dsk_ref[...],
                                                   rev=rev, consts=consts)

        def step(i, carry):
            sl = pl.ds(pl.multiple_of((nsub - 1 - i if rev else i) * SSD_CHUNK, SSD_CHUNK), SSD_CHUNK)
            st = st_scr[...]
            st_out[i] = st
            st_new, y = _ssd_chunk(st, x_ref[sl, :], b_ref[sl, :], c_ref[sl, :], dt_scr[sl, :],
                                   ac_scr[sl, :], dsk_l, rev=rev, consts=consts, skip=skip)
            st_scr[...] = st_new
            if add_ref is not None:
                y = y + add_ref[sl, :]
            y_ref[sl, :] = y
            return carry

        lax.fori_loop(0, nsub, step, 0)

    b0 = D_INNER // n
    yspec = pl.BlockSpec((blk, gw), lambda g, j: (pos(j), g))
    pspec = pl.BlockSpec((None, 1, SSM_HPG), lambda g, j: (g, 0, 0))
    in_specs = [yspec,
                pl.BlockSpec((blk, n), lambda g, j: (pos(j), b0 + g)),
                pl.BlockSpec((blk, n), lambda g, j: (pos(j), b0 + SSM_GROUPS + g)),
                pl.BlockSpec((None, blk, SSM_HPG), lambda g, j: (g, pos(j), 0)),
                pspec, pspec, pspec]
    args = [xbc, xbc, xbc, dtr, bias, alog, dsk]
    if add is not None:
        in_specs.append(yspec)
        args.append(add)
    return pl.pallas_call(
        body, name=name, grid=(SSM_GROUPS, nblk), in_specs=in_specs,
        out_specs=[yspec, pl.BlockSpec((None, None, nsub, n, gw), lambda g, j: (g, j, 0, 0, 0))],
        out_shape=[jax.ShapeDtypeStruct((L, D_INNER), F32),
                   jax.ShapeDtypeStruct((SSM_GROUPS, nblk, nsub, n, gw), F32)],
        scratch_shapes=[pltpu.VMEM((n, gw), F32), pltpu.VMEM((blk, gw), F32), pltpu.VMEM((blk, gw), F32)],
        compiler_params=_cparams("parallel", "arbitrary"),
    )(*args)


def _ssd_bwd(xbc, dtr, bias, alog, dsk, dy, states, *, rev, skip, adds, name):
    L = xbc.shape[0]
    blk, nblk, gw, _ = _ssd_specs(L, rev)
    nsub = blk // SSD_CHUNK
    n = SSM_STATE
    n_add = 0 if adds is None else 3

    def body(*refs):
        x_ref, b_ref, c_ref, dt_ref, bias_ref, alog_ref, dsk_ref, dy_ref, st_in = refs[:9]
        add_refs = refs[9:9 + n_add]
        outs = refs[9 + n_add:]
        dx_ref, db_ref, dc_ref, ddt_ref, dbias_ref, dalog_ref, ddsk_ref = outs[:7]
        dst_scr, dt_scr, ac_scr, ddt_scr, dac_scr = outs[7:]
        consts = _ssd_consts(rev)
        chunk = functools.partial(_ssd_chunk, rev=rev, consts=consts, skip=skip)
        prep = functools.partial(_ssd_prep, rev=rev, consts=consts)

        @pl.when(pl.program_id(1) == 0)
        def _():
            dst_scr[...] = jnp.zeros(dst_scr.shape, F32)
            dbias_ref[...] = jnp.zeros(dbias_ref.shape, F32)
            dalog_ref[...] = jnp.zeros(dalog_ref.shape, F32)
            ddsk_ref[...] = jnp.zeros(ddsk_ref.shape, F32)

        narrow = (dt_ref[...], bias_ref[...], alog_ref[...], dsk_ref[...])
        (dt_scr[...], ac_scr[...], dsk_l), prep_vjp = jax.vjp(prep, *narrow)

        def rows(i):
            return pl.ds(pl.multiple_of((nsub - 1 - i if rev else i) * SSD_CHUNK, SSD_CHUNK), SSD_CHUNK)

        def operands(i):
            sl = rows(i)
            return (x_ref[sl, :], b_ref[sl, :], c_ref[sl, :], dt_scr[sl, :], ac_scr[sl, :], dsk_l)

        def step(k, ddsk_l):
            i = nsub - 1 - k
            sl = rows(i)
            _, vjp = jax.vjp(chunk, st_in[i], *operands(i))
            dst, dx, db, dc, ddt_l, dac, ddsk_k = vjp((dst_scr[...], dy_ref[sl, :]))
            dst_scr[...] = dst
            if n_add:
                dx = dx + add_refs[0][sl, :]
                db = db + add_refs[1][sl, :]
                dc = dc + add_refs[2][sl, :]
            dx_ref[sl, :] = dx
            db_ref[sl, :] = db
            dc_ref[sl, :] = dc
            ddt_scr[sl, :] = ddt_l
            dac_scr[sl, :] = dac
            return ddsk_l + ddsk_k

        ddsk_l = lax.fori_loop(0, nsub, step, jnp.zeros((1, gw), F32))
        ddt, dbias, dalog, ddsk = prep_vjp((ddt_scr[...], dac_scr[...], ddsk_l))
        ddt_ref[...] = ddt
        dbias_ref[...] += dbias
        dalog_ref[...] += dalog
        ddsk_ref[...] += ddsk

    def pos(j):
        p = nblk - 1 - j
        return nblk - 1 - p if rev else p

    b0 = D_INNER // n
    xspec = pl.BlockSpec((blk, gw), lambda g, j: (pos(j), g))
    nspec = pl.BlockSpec((blk, n), lambda g, j: (pos(j), g))
    dtspec = pl.BlockSpec((None, blk, SSM_HPG), lambda g, j: (g, pos(j), 0))
    pspec = pl.BlockSpec((None, 1, SSM_HPG), lambda g, j: (g, 0, 0))
    in_specs = [xspec,
                pl.BlockSpec((blk, n), lambda g, j: (pos(j), b0 + g)),
                pl.BlockSpec((blk, n), lambda g, j: (pos(j), b0 + SSM_GROUPS + g)),
                dtspec, pspec, pspec, pspec, xspec,
                pl.BlockSpec((None, None, nsub, n, gw), lambda g, j: (g, nblk - 1 - j, 0, 0, 0))]
    args = [xbc, xbc, xbc, dtr, bias, alog, dsk, dy, states]
    if adds is not None:
        in_specs += [xspec, nspec, nspec]
        args += list(adds)
    par = jax.ShapeDtypeStruct((SSM_GROUPS, 1, SSM_HPG), F32)
    return pl.pallas_call(
        body, name=name, grid=(SSM_GROUPS, nblk), in_specs=in_specs,
        out_specs=[xspec, nspec, nspec, dtspec, pspec, pspec, pspec],
        out_shape=[jax.ShapeDtypeStruct((L, D_INNER), F32), jax.ShapeDtypeStruct((L, GN), F32),
                   jax.ShapeDtypeStruct((L, GN), F32), jax.ShapeDtypeStruct((SSM_GROUPS, L, SSM_HPG), F32),
                   par, par, par],
        scratch_shapes=[pltpu.VMEM((n, gw), F32)] + [pltpu.VMEM((blk, gw), F32)] * 4,
        compiler_params=_cparams("parallel", "arbitrary"),
    )(*args)


def _out_proj(y, w_out, h, next_norm, name):
    if next_norm is None:
        return _matmul(y, w_out, add=h, name=name), None
    M, K = y.shape
    N = w_out.shape[1]
    tm = 512

    def body(y_ref, w_ref, h_ref, n_ref, o_ref, u_ref):
        acc = jnp.dot(y_ref[...], w_ref[...], preferred_element_type=F32) + h_ref[...]
        o_ref[...] = acc
        u_ref[...] = _rms(acc, n_ref[...]).astype(BF16)

    rows = pl.BlockSpec((tm, N), lambda i: (i, 0))
    return pl.pallas_call(
        body, name=name, grid=(M // tm,),
        in_specs=[pl.BlockSpec((tm, K), lambda i: (i, 0)), pl.BlockSpec((K, N), lambda i: (0, 0)), rows,
                  pl.BlockSpec((1, N), lambda i: (0, 0))],
        out_specs=[rows, rows], out_shape=[jax.ShapeDtypeStruct((M, N), F32), jax.ShapeDtypeStruct((M, N), BF16)],
        compiler_params=_cparams("parallel"),
    )(y, w_out, h, next_norm)


def _norm_fwd(h, w, name):
    d = h.shape[1]
    return _row_kernel(lambda hv, wv: (_rms(hv, wv),), [(h, d, 0)], [w], [(d, BF16)], [], name=name)[0]


def _matmul_norm_bwd(a, b, h, w, dh_in, *, name, add=None, b_cols=None, after=None):
    M, K = a.shape
    N = b.shape[0]
    cb = 0 if b_cols is None else b_cols[0]
    tm = 512 if K <= 4096 else 256

    def body(*refs):
        a_ref, b_ref, h_ref, w_ref, dh_ref = refs[:5]
        o32_ref, o16_ref, dw_ref = refs[-3:]
        du = lax.dot_general(a_ref[...], b_ref[...], NT_DIMS, preferred_element_type=F32)
        if add is not None:
            du = du + refs[5][...]
        _, vjp = jax.vjp(_rms, h_ref[...], w_ref[...])
        dh, dw = vjp(du)
        dh = dh + dh_ref[...]
        o32_ref[...] = dh
        o16_ref[...] = dh.astype(BF16)

        @pl.when(pl.program_id(0) == 0)
        def _():
            dw_ref[...] = jnp.zeros(dw_ref.shape, F32)

        dw_ref[...] += dw

    rows = pl.BlockSpec((tm, N), lambda i: (i, 0))
    in_specs = [pl.BlockSpec((tm, K), lambda i: (i, 0)), pl.BlockSpec((N, K), lambda i: (0, cb)), rows,
                pl.BlockSpec((1, N), lambda i: (0, 0)), rows]
    args = [a, b, h, w, dh_in]
    if add is not None:
        in_specs.append(rows)
        args.append(add)
    if after is not None:
        in_specs.append(pl.BlockSpec(memory_space=pl.ANY))
        args.append(after)
    return pl.pallas_call(
        body, name=name, grid=(M // tm,), in_specs=in_specs,
        out_specs=[rows, rows, pl.BlockSpec((1, N), lambda i: (0, 0))],
        out_shape=[jax.ShapeDtypeStruct((M, N), F32), jax.ShapeDtypeStruct((M, N), BF16),
                   jax.ShapeDtypeStruct((1, N), F32)],
        compiler_params=_cparams("arbitrary"),
    )(*args)


def _ffn_fwd(h, norm_w, w_in, w_out, conv_w, conv_b, tag, u, next_norm=None):
    pg = _matmul(u, w_in, b_cols=(0, D_FF), name=f"{tag}_in_gate")
    pv = _matmul(u, w_in, b_cols=(1, D_FF), name=f"{tag}_in_val")
    y = _col_kernel(lambda g, v, w, b: (_ffn_mid_fn(g, v, w, b),), [(pg, 0), (pv, 0)], [(conv_w, 0), (conv_b, 0)],
                    [BF16], [], name=f"{tag}_mid", n_tiles=D_FF // COL_TILE)[0]
    h_out, u_next = _out_proj(y, w_out, h, next_norm, f"{tag}_out")
    return h_out, (u, pg, pv, y), u_next


def _ffn_bwd(h, dh, dh16, saved, norm_w, w_in, w_out, conv_w, conv_b, tag, after=None):
    u, pg, pv, y = saved
    dy = _matmul(dh16, w_out, nt=True, after=after, name=f"{tag}_out_dx")
    dw_out = _matmul(y, dh16, ta=True, name=f"{tag}_out_dw")

    def fn(g, v, ct, w, b):
        _, vjp = jax.vjp(_ffn_mid_fn, g, v, w, b)
        return vjp(ct)

    dpg, dpv, dcw, dcb = _col_kernel(fn, [(pg, 0), (pv, 0), (dy, 0)], [(conv_w, 0), (conv_b, 0)], [BF16, BF16],
                                     [FFN_CONV, 1], name=f"{tag}_mid_bwd", n_tiles=D_FF // COL_TILE)
    du = _matmul(dpg, w_in, nt=True, b_cols=(0, D_FF), name=f"{tag}_gate_dx")
    dw_gate = _matmul(u, dpg, ta=True, name=f"{tag}_gate_dw")
    dw_val = _matmul(u, dpv, ta=True, name=f"{tag}_val_dw")
    dh, dh16, dnw = _matmul_norm_bwd(dpv, w_in, h, norm_w, dh, add=du, b_cols=(1, D_FF), name=f"{tag}_val_dx_norm")
    return dh, dh16, dict(w_in=(dw_gate, dw_val), w_out=dw_out, conv_w=dcw, conv_b=dcb, norm=dnw)


def _hgrn2_fwd(h, norm_w, w_in, lb_logits, a_norm_w, w_out, after=None, next_norm=None):
    d = D_MODEL
    u = _norm_fwd(h, norm_w, "hg_norm")
    pa = _matmul(u, w_in, after=after, name="hg_in")
    qs, lf_fw, k_fw, lf_bw, k_bw = _row_kernel(
        _hgrn2_pre_fn, [(pa, d, 0), (pa, d, 1), (pa, d, 2)], [lb_logits], [(d, F32)] * 5, [], name="hg_pre")
    o_fw, st_fw = _gla_fwd(qs, k_fw, pa, 3 * HG_HEADS, lf_fw, rev=False, add=None, name="hg_gla_fw")
    o, st_bw = _gla_fwd(qs, k_bw, pa, 3 * HG_HEADS, lf_bw, rev=True, add=o_fw, name="hg_gla_bw")
    y = _row_kernel(lambda ov, gv, wv: (_hgrn2_post_fn(ov, gv, wv),), [(o, d, 0), (pa, d, 4)], [a_norm_w],
                    [(d, BF16)], [], name="hg_post")[0]
    h_out, u_next = _out_proj(y, w_out, h, next_norm, "hg_out")
    return h_out, (u, pa, qs, lf_fw, k_fw, lf_bw, k_bw, st_fw, st_bw, o, y), u_next


def _hgrn2_bwd(h, dh, dh16, saved, norm_w, w_in, lb_logits, a_norm_w, w_out, send, after=None):
    d = D_MODEL
    u, pa, qs, lf_fw, k_fw, lf_bw, k_bw, st_fw, st_bw, o, y = saved
    dy = _matmul(dh16, w_out, nt=True, after=after, name="hg_out_dx")
    dw_out = _matmul(y, dh16, ta=True, name="hg_out_dw")

    def post_bwd(ov, gv, ct, wv):
        _, vjp = jax.vjp(_hgrn2_post_fn, ov, gv, wv)
        return vjp(ct)

    do, dg, dnw = _row_kernel(post_bwd, [(o, d, 0), (pa, d, 4), (dy, d, 0)], [a_norm_w], [(d, F32), (d, F32)],
                              [(1, HG_HEAD_DIM)], name="hg_post_bwd")
    dq1, dk_fw, dv1, dlf_fw = _gla_bwd(qs, k_fw, pa, 3 * HG_HEADS, lf_fw, do, st_fw, rev=False, adds=None,
                                       name="hg_gla_fw_bwd")
    dqs, dk_bw, dv, dlf_bw = _gla_bwd(qs, k_bw, pa, 3 * HG_HEADS, lf_bw, do, st_bw, rev=True, adds=(dq1, dv1),
                                      name="hg_gla_bw_bwd")

    def pre_bwd(qr, fr, br, c0, c1, c2, c3, c4, dvv, dgv, lbl):
        _, vjp = jax.vjp(_hgrn2_pre_fn, qr, fr, br, lbl)
        dq, df, db, dlbl = vjp((c0, c1, c2, c3, c4))
        return jnp.concatenate([dq, df, db, dvv, dgv], axis=1), dlbl

    rows = [(pa, d, 0), (pa, d, 1), (pa, d, 2), (dqs, d, 0), (dlf_fw, d, 0), (dk_fw, d, 0), (dlf_bw, d, 0),
            (dk_bw, d, 0), (dv, d, 0), (dg, d, 0)]
    dpa, dlbl = _row_kernel(pre_bwd, rows, [lb_logits], [(5 * d, BF16)], [lb_logits.shape], name="hg_pre_bwd")
    dw_in = _matmul(u, dpa, ta=True, name="hg_in_dw")
    token = send(dw_in, dw_out)
    dh, dh16, dn1 = _matmul_norm_bwd(dpa, w_in, h, norm_w, dh, after=token, name="hg_in_dx_norm")
    return dh, dh16, dict(lb=dlbl, a_norm=dnw, norm=dn1)


def _group_params(p):
    return p.reshape(SSM_GROUPS, 1, SSM_HPG)


def _mamba_fwd(h, norm_w, w_z, w_xbc, w_dt, conv_w, conv_b, dt_bias, a_log, d_skip, b_norm_w, w_out, u, next_norm):
    L = h.shape[0]
    z = _matmul(u, w_z, name="mb_in_z")
    xbc_raw = _matmul(u, w_xbc, name="mb_in_xbc")
    dt_raw = _matmul(u, w_dt, name="mb_in_dt")
    xbc = _col_kernel(lambda xv, w, b: (_mamba_conv_fn(xv, w, b),), [(xbc_raw, 0)], [(conv_w, 0), (conv_b, 0)],
                      [F32], [], name="mb_conv", n_tiles=CONV_DIM // COL_TILE)[0]
    dtr = dt_raw.reshape(L, 2, SSM_GROUPS, SSM_HPG).transpose(1, 2, 0, 3)
    bias, alog = dt_bias.reshape(2, -1), a_log.reshape(2, -1)
    dsk = _group_params(d_skip.reshape(-1))
    y_fw, st_fw = _ssd_fwd(xbc, dtr[0], _group_params(bias[0]), _group_params(alog[0]), dsk, rev=False, skip=True,
                           add=None, name="mb_ssd_fw")
    ysum, st_bw = _ssd_fwd(xbc, dtr[1], _group_params(bias[1]), _group_params(alog[1]), dsk, rev=True, skip=False,
                           add=y_fw, name="mb_ssd_bw")
    y = _row_kernel(lambda yv, zv, wv: (_mamba_post_fn(yv, zv, wv),), [(ysum, D_INNER, 0), (z, D_INNER, 0)],
                    [b_norm_w], [(D_INNER, BF16)], [], name="mb_post")[0]
    h_out, u_next = _out_proj(y, w_out, h, next_norm, "mb_out")
    return h_out, (u, z, xbc_raw, xbc, dtr, st_fw, st_bw, ysum, y), u_next


def _mamba_bwd(h, dh, dh16, saved, norm_w, w_z, w_xbc, w_dt, conv_w, conv_b, dt_bias, a_log, d_skip, b_norm_w, w_out,
               after=None):
    L = h.shape[0]
    u, z, xbc_raw, xbc, dtr, st_fw, st_bw, ysum, y = saved
    dy = _matmul(dh16, w_out, nt=True, after=after, name="mb_out_dx")
    dw_out = _matmul(y, dh16, ta=True, name="mb_out_dw")

    def post_bwd(yv, zv, ct, wv):
        _, vjp = jax.vjp(_mamba_post_fn, yv, zv, wv)
        return vjp(ct)

    dys, dz, dbn = _row_kernel(post_bwd, [(ysum, D_INNER, 0), (z, D_INNER, 0), (dy, D_INNER, 0)], [b_norm_w],
                               [(D_INNER, F32), (D_INNER, BF16)], [(1, D_INNER)], name="mb_post_bwd")
    bias, alog = dt_bias.reshape(2, -1), a_log.reshape(2, -1)
    dsk = _group_params(d_skip.reshape(-1))
    dx1, db1, dc1, ddt_fw, dbias_fw, dalog_fw, ddsk = _ssd_bwd(
        xbc, dtr[0], _group_params(bias[0]), _group_params(alog[0]), dsk, dys, st_fw, rev=False, skip=True,
        adds=None, name="mb_ssd_fw_bwd")
    dx, db, dc, ddt_bw, dbias_bw, dalog_bw, _ = _ssd_bwd(
        xbc, dtr[1], _group_params(bias[1]), _group_params(alog[1]), dsk, dys, st_bw, rev=True, skip=False,
        adds=(dx1, db1, dc1), name="mb_ssd_bw_bwd")

    def conv_bwd(n_tiles, ct, first):
        def fn(xv, ctv, w, b):
            _, vjp = jax.vjp(_mamba_conv_fn, xv, w, b)
            return vjp(ctv)
        return _col_kernel(fn, [(xbc_raw, first), (ct, 0)], [(conv_w, first), (conv_b, first)], [BF16],
                           [SSM_CONV, 1], name=f"mb_conv_bwd_{first}", n_tiles=n_tiles)

    nx, nb = D_INNER // COL_TILE, GN // COL_TILE
    parts = [conv_bwd(nx, dx, 0), conv_bwd(nb, db, nx), conv_bwd(nb, dc, nx + nb)]
    dxbc = jnp.concatenate([p[0] for p in parts], axis=1)
    dcw = jnp.concatenate([p[1] for p in parts], axis=1)
    dcb = jnp.concatenate([p[2] for p in parts], axis=1)
    ddt = jnp.stack([ddt_fw, ddt_bw]).transpose(2, 0, 1, 3).reshape(L, 2 * SSM_GROUPS * SSM_HPG).astype(BF16)
    du = _matmul(dz, w_z, nt=True, name="mb_z_dx")
    du = _matmul(dxbc, w_xbc, nt=True, add=du, name="mb_xbc_dx")
    dw_in = jnp.concatenate([_matmul(u, dz, ta=True, name="mb_z_dw"), _matmul(u, dxbc, ta=True, name="mb_xbc_dw"),
                             _matmul(u, ddt, ta=True, name="mb_dt_dw")], axis=1)
    dh, dh16, dn1 = _matmul_norm_bwd(ddt, w_dt, h, norm_w, dh, add=du, name="mb_dt_dx_norm")
    grads = dict(w_in=dw_in, w_out=dw_out, conv_w=dcw, conv_b=dcb, b_norm=dbn, norm=dn1,
                 dt_bias=jnp.stack([dbias_fw, dbias_bw]).reshape(1, 2, -1),
                 a_log=jnp.stack([dalog_fw, dalog_bw]).reshape(1, 2, -1), d_skip=ddsk.reshape(1, -1))
    return dh, dh16, grads


def _loss_head(h, target, w):
    d = h.shape[1]

    def fn(hv, tv, wv):
        def loss(hv, wv):
            err = _rms(hv, wv) - tv
            return 0.5 * jnp.sum(jnp.mean(err * err, axis=-1, keepdims=True), axis=0, keepdims=True)
        val, vjp = jax.vjp(loss, hv, wv)
        dh, dw = vjp(jnp.ones((1, 1), F32))
        return dh, dh, val, dw

    return _row_kernel(fn, [(h, d, 0), (target, d, 0)], [w], [(d, F32), (d, BF16)], [(1, 1), (1, d)], name="loss_head")


def _local_step(x, target, rep, comm):
    def ffn_args(i, w, small):
        return (rep["norm2_w"][i:i + 1], w["w_in"], w["w_out"], small["ffn_conv_w"][i], rep["ffn_conv_b"][i:i + 1])

    def mamba_args(w, small):
        w_in = w["w_in"]
        return (rep["norm1_w"][1:2], w_in[:, :D_INNER], w_in[:, D_INNER:D_INNER + CONV_DIM],
                w_in[:, D_INNER + CONV_DIM:], small["b_conv_w"][0], small["b_conv_b"], rep["b_dt_bias"],
                rep["b_a_log"], rep["b_d_skip"], small["b_norm_w"], w["w_out"])

    w_hg = comm.weights("hg", None)
    hg = (rep["norm1_w"][0:1], w_hg["w_in"], rep["a_lb_logits"], rep["a_norm_w"], w_hg["w_out"])
    h0 = x
    h1, s_hg, u1 = _hgrn2_fwd(h0, *hg, after=w_hg.get("token"), next_norm=rep["norm2_w"][0:1])
    w_f0 = comm.weights("f0", h1)
    small = w_f0["small"]
    f0 = ffn_args(0, w_f0, small)
    h2, s_f0, u2 = _ffn_fwd(h1, *f0, "ffn0", u1, next_norm=rep["norm1_w"][1:2])
    mb = mamba_args(comm.weights("mb", h2), small)
    h3, s_mb, u3 = _mamba_fwd(h2, *mb, u2, rep["norm2_w"][1:2])
    f1 = ffn_args(1, comm.weights("f1", h3), small)
    h4, s_f1, _ = _ffn_fwd(h3, *f1, "ffn1", u3)
    dh, dh16, loss, d_final = _loss_head(h4, target, rep["final_norm_w"].reshape(1, -1))

    dh, dh16, g_f1 = _ffn_bwd(h3, dh, dh16, s_f1, *f1, "ffn1")
    token = comm.send("f1", dict(w_in=g_f1["w_in"], w_out=g_f1["w_out"]))
    dh, dh16, g_mb = _mamba_bwd(h2, dh, dh16, s_mb, *mb, after=token)
    token = comm.send("mb", dict(w_in=g_mb["w_in"], w_out=g_mb["w_out"]))
    dh, dh16, g_f0 = _ffn_bwd(h1, dh, dh16, s_f0, *f0, "ffn0", after=token)
    token = comm.send("f0", dict(w_in=g_f0["w_in"], w_out=g_f0["w_out"]))
    small_grads = dict(b_conv_w=g_mb["conv_w"][None], b_conv_b=g_mb["conv_b"], b_norm_w=g_mb["b_norm"],
                       ffn_conv_w=jnp.stack([g_f0["conv_w"], g_f1["conv_w"]]))
    dh, dh16, g_hg = _hgrn2_bwd(
        h0, dh, dh16, s_hg, *hg, after=token,
        send=lambda dw_in, dw_out: comm.send("hg", dict(w_in=dw_in, w_out=dw_out, small=small_grads)))
    grads = dict(
        norm1_w=jnp.concatenate([g_hg["norm"], g_mb["norm"]], axis=0),
        norm2_w=jnp.concatenate([g_f0["norm"], g_f1["norm"]], axis=0),
        a_lb_logits=g_hg["lb"], a_norm_w=g_hg["a_norm"], b_dt_bias=g_mb["dt_bias"], b_a_log=g_mb["a_log"],
        b_d_skip=g_mb["d_skip"], ffn_conv_b=jnp.concatenate([g_f0["conv_b"], g_f1["conv_b"]], axis=0),
        final_norm_w=d_final.reshape(-1),
    )
    return loss, dh, grads


def _mesh_pos():
    return lax.axis_index("x"), lax.axis_index("y"), lax.axis_index("c")


N_COPIES = N_DEV - 1


def _comm_call(body, ins, out_shape, name):
    n = len(ins)
    hbm = pl.BlockSpec(memory_space=pl.ANY)
    return pl.pallas_call(
        body, name=name, out_shape=out_shape, in_specs=[hbm] * n, out_specs=[hbm] * n,
        scratch_shapes=[pltpu.SemaphoreType.DMA((n * N_COPIES,)), pltpu.SemaphoreType.DMA((n * N_COPIES,)),
                        pltpu.SemaphoreType.DMA((n,))],
    )(*ins)


def _all_gather(shards, name):
    n = len(shards)

    def body(*refs):
        x_refs, out_refs = refs[:n], refs[n:2 * n]
        send_sems, recv_sems, local_sems = refs[2 * n:]
        x, y, c = _mesh_pos()
        me, sibling = (x, y, c), (x, y, 1 - c)
        chips = [(1 - x, y), (x, 1 - y), (1 - x, 1 - y)]

        def copy(w, k, block, to, own=False):
            px, py, pc = block
            dst = out_refs[w].at[4 * px + 2 * py + pc]
            return pltpu.make_async_remote_copy(
                src_ref=x_refs[w] if own else dst, dst_ref=dst, send_sem=send_sems.at[w * N_COPIES + k],
                recv_sem=recv_sems.at[w * N_COPIES + k], device_id=to, device_id_type=MESH)

        mine = [pltpu.make_async_copy(x_refs[w], out_refs[w].at[4 * x + 2 * y + c], local_sems.at[w]) for w in range(n)]
        for cp in mine:
            cp.start()
        first = [copy(w, 1 + j, me, (*chip, c), own=True) for j, chip in enumerate(chips) for w in range(n)]
        first += [copy(w, 0, me, sibling, own=True) for w in range(n)]
        for cp in first:
            cp.start()
        passed = []
        for j, chip in enumerate(chips):
            for w in range(n):
                copy(w, 1 + j, (*chip, c), me).wait_recv()
                passed.append(copy(w, 4 + j, (*chip, c), sibling))
                passed[-1].start()
        for w in range(n):
            copy(w, 0, sibling, me).wait_recv()
        for j, chip in enumerate(chips):
            for w in range(n):
                copy(w, 4 + j, (*chip, 1 - c), me).wait_recv()
        for cp in first + passed:
            cp.wait_send()
        for cp in mine:
            cp.wait()

    out_shape = [jax.ShapeDtypeStruct((N_DEV, *s.shape), s.dtype) for s in shards]
    return _comm_call(body, shards, out_shape, name)


HBM_SPEC = pl.BlockSpec(memory_space=pltpu.HBM)
SEM_SPEC = pl.BlockSpec(memory_space=pltpu.SEMAPHORE)
EFFECT = pltpu.SideEffectType.DATAFLOW_SIDE_EFFECTING
PEER_ORDER = (4, 2, 6, 5, 3, 7, 1)


def _peer_copies(src_refs, land_refs, send_sems, recv_sems, gather):
    x, y, c = _mesh_pos()
    me = 4 * x + 2 * y + c
    copies = []
    for k in PEER_ORDER:
        px = 1 - x if k & 4 else x
        py = 1 - y if k & 2 else y
        pc = 1 - c if k & 1 else c
        for w, (src, land) in enumerate(zip(src_refs, land_refs)):
            copies.append(pltpu.make_async_remote_copy(
                src_ref=src if gather else src.at[4 * px + 2 * py + pc],
                dst_ref=land.at[me] if gather else land.at[k - 1],
                send_sem=send_sems.at[w * N_COPIES + k - 1], recv_sem=recv_sems.at[w * N_COPIES + k - 1],
                device_id=(px, py, pc), device_id_type=MESH))
    return copies


def _all_gather_small(shard, name):
    def body(x_ref, out_ref, send_sems, recv_sems):
        x, y, c = _mesh_pos()
        out_ref[4 * x + 2 * y + c] = x_ref[...]
        copies = _peer_copies([x_ref], [out_ref], send_sems, recv_sems, True)
        for cp in copies:
            cp.start()
        for cp in copies:
            cp.wait()

    vmem = pl.BlockSpec(memory_space=pltpu.VMEM)
    return pl.pallas_call(
        body, name=name, out_shape=jax.ShapeDtypeStruct((N_DEV, *shard.shape), shard.dtype), in_specs=[vmem],
        out_specs=vmem, scratch_shapes=[pltpu.SemaphoreType.DMA((N_COPIES,)), pltpu.SemaphoreType.DMA((N_COPIES,))],
    )(shard)


def _copies_start(srcs, *, gather, after, name):
    n = len(srcs)
    lands = [lax.empty(((N_DEV,) + s.shape) if gather else ((N_COPIES,) + s.shape[1:]), s.dtype) for s in srcs]

    def body(*refs):
        src_refs, land_refs = refs[:n], refs[n:2 * n]
        send_sems, recv_sems = refs[-2 * n - 3], refs[-2 * n - 2]
        for cp in _peer_copies(src_refs, land_refs, send_sems, recv_sems, gather):
            cp.start()
        refs[-1][...] = jnp.zeros(refs[-1].shape, F32)

    ins = [pltpu.with_memory_space_constraint(a, pltpu.HBM) for a in srcs + lands]
    in_specs = [HBM_SPEC] * (2 * n)
    if after is not None:
        ins.append(after)
        in_specs.append(pl.BlockSpec(memory_space=pl.ANY))
    sems = pltpu.SemaphoreType.DMA((n * N_COPIES,))
    out = pl.pallas_call(
        body, name=name,
        out_shape=(sems, sems, *[pltpu.HBM(a.shape, a.dtype) for a in srcs + lands], jax.ShapeDtypeStruct((8, LANES), F32)),
        in_specs=in_specs,
        out_specs=(SEM_SPEC, SEM_SPEC, *[HBM_SPEC] * (2 * n), pl.BlockSpec(memory_space=pltpu.VMEM)),
        input_output_aliases={i: 2 + i for i in range(2 * n)},
        compiler_params=pltpu.CompilerParams(has_side_effects=EFFECT),
    )(*ins)
    return out[0], out[1], list(out[2:2 + n]), list(out[2 + n:2 + 2 * n]), out[-1]


def _copies_wait(started, *, gather, after, name):
    send_sems, recv_sems, srcs, lands, _ = started
    n = len(srcs)

    def body(*refs):
        src_refs, land_refs = refs[:n], refs[n:2 * n]
        for cp in _peer_copies(src_refs, land_refs, refs[2 * n], refs[2 * n + 1], gather):
            cp.wait_send()
            cp.wait_recv()

    out = pl.pallas_call(
        body, name=name, out_shape=tuple(pltpu.HBM(a.shape, a.dtype) for a in srcs + lands),
        in_specs=[HBM_SPEC] * (2 * n) + [SEM_SPEC, SEM_SPEC, pl.BlockSpec(memory_space=pl.ANY)],
        out_specs=tuple([HBM_SPEC] * (2 * n)), input_output_aliases={i: i for i in range(2 * n)},
        compiler_params=pltpu.CompilerParams(has_side_effects=EFFECT),
    )(*srcs, *lands, send_sems, recv_sems, after)
    return list(out[n:])


def _adamw(got, w, m, v, *, name, tile, own=None, layer=None, into=None):
    rows, width = w.shape[-2:]
    n_got = got.shape[0]
    c1 = 1.0 / (1.0 - ADAM_B1 ** ADAM_STEP)
    c2 = 1.0 / (1.0 - ADAM_B2 ** ADAM_STEP)

    def body(*refs):
        got_ref, w_ref, m_ref, v_ref = refs[:4]
        g_ref, d_ref, nm_ref, nv_ref = refs[-4:]
        g = got_ref[0] if own is None else refs[4][...] + got_ref[0]
        for s in range(1, n_got):
            g = g + got_ref[s]
        m_new = ADAM_B1 * m_ref[...] + (1.0 - ADAM_B1) * g
        v_new = ADAM_B2 * v_ref[...] + (1.0 - ADAM_B2) * (g * g)
        g_ref[...] = g
        nm_ref[...] = m_new
        nv_ref[...] = v_new
        d_ref[...] = -ADAM_LR * ((m_new * c1) / (jnp.sqrt(v_new * c2) + ADAM_EPS) + ADAM_WD * w_ref[...])

    spec = pl.BlockSpec((tile, width), lambda i: (i, 0))
    wspec = spec if layer is None else pl.BlockSpec((None, tile, width), lambda i: (layer, i, 0))
    args = [got, w, m, v] + ([] if own is None else [own])
    in_specs = [pl.BlockSpec((n_got, tile, width), lambda i: (0, i, 0))] + [wspec] * 3 + [spec] * (len(args) - 4)
    aliases = {}
    if into is not None:
        aliases = {len(args) + i: i for i in range(4)}
        args += list(into)
        in_specs += [pl.BlockSpec(memory_space=pl.ANY)] * 4
    return pl.pallas_call(
        body, name=name, grid=(rows // tile,), in_specs=in_specs, out_specs=[wspec] * 4,
        out_shape=[jax.ShapeDtypeStruct(w.shape, F32)] * 4, input_output_aliases=aliases,
        compiler_params=_cparams("parallel"),
    )(*args)


SHARDED = dict(a_w_in=(2, True), a_w_out=(1, True), b_w_in=(2, True), b_w_out=(1, True), ffn_w_in=(2, True),
               ffn_w_out=(1, True), b_conv_w=(2, False), b_conv_b=(1, False), b_norm_w=(1, False), ffn_conv_w=(2, False))
REPLICATED = ("norm1_w", "norm2_w", "a_lb_logits", "a_norm_w", "b_dt_bias", "b_a_log", "b_d_skip", "ffn_conv_b",
              "final_norm_w")
WEIGHTS = ("norm1_w", "norm2_w", "a_w_in", "a_lb_logits", "a_norm_w", "a_w_out", "b_w_in", "b_conv_w", "b_conv_b",
           "b_dt_bias", "b_a_log", "b_d_skip", "b_norm_w", "b_w_out", "ffn_w_in", "ffn_conv_w", "ffn_conv_b",
           "ffn_w_out", "final_norm_w")


def _pad_rows(flat, multiple):
    n = flat.shape[-1]
    per = PACK_W * multiple
    total = -(-n // per) * per
    flat = jnp.pad(flat, [(0, 0)] * (flat.ndim - 1) + [(0, total - n)])
    return flat.reshape(*flat.shape[:-1], total // PACK_W, PACK_W)


def _to_parts(full, axis, n=N_DEV):
    shp = full.shape
    t = full.reshape(*shp[:axis], n, shp[axis] // n, *shp[axis + 1:])
    return jnp.moveaxis(t, axis, 0)


def _from_parts(parts, axis):
    t = jnp.moveaxis(parts, 0, axis)
    shp = t.shape
    return t.reshape(*shp[:axis], shp[axis] * shp[axis + 1], *shp[axis + 2:])


BIG = tuple(n for n, (_, mm) in SHARDED.items() if mm)
SMALL = tuple(n for n, (_, mm) in SHARDED.items() if not mm)
SMALL_W = 512


def _small_rows(tree, lead):
    rows = []
    for n in SMALL:
        t = tree[n]
        t = t.reshape(*lead, -1, t.shape[-1])
        rows.append(jnp.pad(t, [(0, 0)] * (t.ndim - 1) + [(0, SMALL_W - t.shape[-1])]))
    buf = jnp.concatenate(rows, axis=-2)
    return jnp.pad(buf, [(0, 0)] * (buf.ndim - 2) + [(0, 16 - buf.shape[-2]), (0, 0)])


def _small_unrows(buf, like, lead):
    out, r = {}, 0
    for n in SMALL:
        shp = like[n].shape
        k = like[n].size // shp[-1]
        out[n] = buf[..., r:r + k, :shp[-1]].reshape(*lead, *shp)
        r += k
    return out


GROUPS = dict(hg=(("a_w_in", 0), ("a_w_out", 0)), f0=(("ffn_w_in", 0), ("ffn_w_out", 0)),
              mb=(("b_w_in", 0), ("b_w_out", 0)), f1=(("ffn_w_in", 1), ("ffn_w_out", 1)))


class _Comm:
    def __init__(self, local):
        x, y, c = _mesh_pos()
        self.me = 4 * x + 2 * y + c
        self.local = local
        self.shards = {g: [local[n][i].astype(BF16) for n, i in names] for g, names in GROUPS.items()}
        self.shards["f0"].append(_small_rows(local, ()))
        self.first = _all_gather(self.shards["hg"], "gather_hg")
        self.gathers, self.sent, token = {}, {}, None
        for g in ("f0", "mb", "f1"):
            self.gathers[g] = _copies_start(self.shards[g], gather=True, after=token, name=f"gather_{g}_start")
            token = self.gathers[g][-1]
        self.token = token

    def weights(self, group, after):
        if group == "hg":
            got = self.first
        else:
            lands = _copies_wait(self.gathers[group], gather=True, after=after, name=f"gather_{group}_wait")
            got = [lax.dynamic_update_index_in_dim(land, shard, self.me, 0)
                   for land, shard in zip(lands, self.shards[group])]
        out = dict(w_in=_from_parts(got[0], 1), w_out=_from_parts(got[1], 0))
        if group == "hg":
            out["token"] = self.token
        if group == "f0":
            small = _small_unrows(got[2], self.local, (N_DEV,))
            out["small"] = {n: _from_parts(small[n], SHARDED[n][0]) for n in SMALL}
        return out

    def send(self, group, grads):
        w_in = grads["w_in"]
        if isinstance(w_in, tuple):
            half = N_DEV // len(w_in)
            parts_in = jnp.concatenate([_to_parts(t, 1, half) for t in w_in], axis=0)
        else:
            parts_in = _to_parts(w_in, 1)
        parts = [parts_in, _to_parts(grads["w_out"], 0)]
        if "small" in grads:
            parts.append(_small_rows({n: _to_parts(grads["small"][n], SHARDED[n][0]) for n in SMALL}, (N_DEV,)))
        sent = [p.astype(BF16) for p in parts[:2]] + parts[2:] if group == "hg" else parts
        self.sent[group] = (parts, _copies_start(sent, gather=False, after=None, name=f"exchange_{group}_start"))
        return self.sent[group][1][-1]

    def finish(self, after, mom, var):
        res = {}
        for group in ("f1", "mb", "f0", "hg"):
            parts, started = self.sent[group]
            lands = _copies_wait(started, gather=False, after=after, name=f"exchange_{group}_wait")
            own = [lax.dynamic_index_in_dim(p, self.me, 0, keepdims=False) for p in parts]
            for (n, i), got, mine in zip(GROUPS[group], lands, own):
                res[n] = _adamw(got, self.local[n], mom[n], var[n], own=mine, layer=i, into=res.get(n),
                                name=f"adamw_{n}_{i}", tile=_pick(mine.shape[0], (256, 176, 128)))
                after = res[n][0]
        small = _adamw(lands[2], *[_small_rows(t, ()) for t in (self.local, mom, var)], own=own[2],
                       name="adamw_small", tile=16)
        return res, small


def _pack_small(tree, extra):
    flat = jnp.concatenate([tree[n].reshape(-1) for n in REPLICATED] + [extra.reshape(-1)])
    return _pad_rows(flat, 8)


def _unpack_small(pack, like):
    flat, out, off = pack.reshape(-1), {}, 0
    for n in REPLICATED:
        out[n] = flat[off:off + like[n].size].reshape(like[n].shape)
        off += like[n].size
    return out, flat[off]


def kernel(x, norm1_w, norm2_w, a_w_in, a_lb_logits, a_norm_w, a_w_out, b_w_in, b_conv_w, b_conv_b, b_dt_bias, b_a_log, b_d_skip, b_norm_w, b_w_out, ffn_w_in, ffn_conv_w, ffn_conv_b, ffn_w_out, final_norm_w, loss_target, m_norm1_w, m_norm2_w, m_a_w_in, m_a_lb_logits, m_a_norm_w, m_a_w_out, m_b_w_in, m_b_conv_w, m_b_conv_b, m_b_dt_bias, m_b_a_log, m_b_d_skip, m_b_norm_w, m_b_w_out, m_ffn_w_in, m_ffn_conv_w, m_ffn_conv_b, m_ffn_w_out, m_final_norm_w, v_norm1_w, v_norm2_w, v_a_w_in, v_a_lb_logits, v_a_norm_w, v_a_w_out, v_b_w_in, v_b_conv_w, v_b_conv_b, v_b_dt_bias, v_b_a_log, v_b_d_skip, v_b_norm_w, v_b_w_out, v_ffn_w_in, v_ffn_conv_w, v_ffn_conv_b, v_ffn_w_out, v_final_norm_w):
    given = dict(locals())
    local = {n: given[n] for n in WEIGHTS}
    mom = {n: given["m_" + n] for n in WEIGHTS}
    var = {n: given["v_" + n] for n in WEIGHTS}

    comm = _Comm(local)
    loss, grad_x, grads = _local_step(x[0], loss_target[0], local, comm)

    res, small_res = comm.finish(grad_x, mom, var)
    outs = ({}, {}, {}, {})
    for n in BIG:
        for out, r in zip(outs, res[n]):
            out[n] = r
    for out, r in zip(outs, small_res):
        out.update(_small_unrows(r, local, ()))
    out_g, out_d, out_m, out_v = outs

    small = _pack_small(grads, loss)
    rows = small.shape[0]
    got_s = _all_gather_small(small, "gather_small")
    zero = jnp.zeros((1,), F32)
    g, dlt, nm, nv = _adamw(got_s, _pack_small(local, zero), _pack_small(mom, zero), _pack_small(var, zero),
                            name="adamw_replicated", tile=rows)
    (rep_g, total), (rep_d, _), (rep_m, _), (rep_v, _) = (_unpack_small(t, local) for t in (g, dlt, nm, nv))
    out_g.update(rep_g)
    out_d.update(rep_d)
    out_m.update(rep_m)
    out_v.update(rep_v)

    return (total, grad_x[None], *[out_g[n] for n in WEIGHTS], *[out_d[n] for n in WEIGHTS],
            *[out_m[n] for n in WEIGHTS], *[out_v[n] for n in WEIGHTS])
```

```python
import functools

import jax
import jax.numpy as jnp
from jax import lax
from jax.experimental import pallas as pl
from jax.experimental.pallas import tpu as pltpu

F32 = jnp.float32
BF16 = jnp.bfloat16
HIGHEST = lax.Precision.HIGHEST
MESH = pl.DeviceIdType.MESH

N_DEV = 8
EPS = 1e-6
D_MODEL = 1024
HG_HEADS = 8
HG_HEAD_DIM = 128
D_INNER = 2048
SSM_HEAD_DIM = 64
SSM_GROUPS = 8
SSM_HPG = 4
SSM_STATE = 128
GN = SSM_GROUPS * SSM_STATE
CONV_DIM = D_INNER + 2 * GN
D_FF = 2816
SSM_CONV = 5
FFN_CONV = 3

ADAM_LR = 0.001
ADAM_B1 = 0.9
ADAM_B2 = 0.999
ADAM_EPS = 1e-08
ADAM_WD = 0.01
ADAM_STEP = 10

LANES = 128
ROW_TILE = 256
COL_TILE = 128
GLA_CHUNK = 16
GLA_BLOCK = 256
GLA_HEADS_PER_STEP = 8
SSD_CHUNK = 128
SSD_FWD_GROUPS_PER_STEP = 4
SSD_BWD_GROUPS_PER_STEP = 1
SSD_BLOCK = 512
PACK_W = 1024
VMEM_LIMIT = 56 * 1024 * 1024

NT_DIMS = (((1,), (1,)), ((), ()))
TN_DIMS = (((0,), (0,)), ((), ()))


def _cparams(*sem):
    return pltpu.CompilerParams(dimension_semantics=sem, vmem_limit_bytes=VMEM_LIMIT)


def _rms(x, w):
    return x * lax.rsqrt(jnp.mean(x * x, axis=-1, keepdims=True) + EPS) * w


def _row_kernel(body_fn, rows, params, row_outs, acc_outs, *, name, tile=ROW_TILE):
    L = rows[0][0].shape[0]
    tile = min(tile, L)
    n_in = len(rows) + len(params)
    n_ro = len(row_outs)

    def body(*refs):
        outs = body_fn(*[r[...] for r in refs[:n_in]])
        for ref, o in zip(refs[n_in:n_in + n_ro], outs[:n_ro]):
            ref[...] = o.astype(ref.dtype)
        first = pl.program_id(0) == 0
        for ref, o in zip(refs[n_in + n_ro:], outs[n_ro:]):
            @pl.when(first)
            def _(ref=ref):
                ref[...] = jnp.zeros(ref.shape, ref.dtype)
            ref[...] += o

    in_specs = [pl.BlockSpec((tile, w), lambda i, cb=cb: (i, cb)) for _, w, cb in rows]
    in_specs += [pl.BlockSpec(p.shape, lambda i: (0, 0)) for p in params]
    out_specs = [pl.BlockSpec((tile, w), lambda i: (i, 0)) for w, _ in row_outs]
    out_specs += [pl.BlockSpec(s, lambda i: (0, 0)) for s in acc_outs]
    out_shape = [jax.ShapeDtypeStruct((L, w), dt) for w, dt in row_outs]
    out_shape += [jax.ShapeDtypeStruct(s, F32) for s in acc_outs]
    return pl.pallas_call(
        body, name=name, grid=(L // tile,), in_specs=in_specs, out_specs=out_specs, out_shape=out_shape,
        compiler_params=_cparams("arbitrary" if acc_outs else "parallel"),
    )(*[a for a, _, _ in rows], *params)


def _col_kernel(body_fn, cols, params, col_outs, par_outs, *, name, n_tiles):
    L = cols[0][0].shape[0]
    n_in = len(cols) + len(params)
    width = n_tiles * COL_TILE

    def body(*refs):
        outs = body_fn(*[r[...] for r in refs[:n_in]])
        for ref, o in zip(refs[n_in:], outs):
            ref[...] = o.astype(ref.dtype)

    in_specs = [pl.BlockSpec((L, COL_TILE), lambda j, cb=cb: (0, cb + j)) for _, cb in cols]
    in_specs += [pl.BlockSpec((p.shape[0], COL_TILE), lambda j, cb=cb: (0, cb + j)) for p, cb in params]
    out_specs = [pl.BlockSpec((L, COL_TILE), lambda j: (0, j)) for _ in col_outs]
    out_specs += [pl.BlockSpec((k, COL_TILE), lambda j: (0, j)) for k in par_outs]
    out_shape = [jax.ShapeDtypeStruct((L, width), dt) for dt in col_outs]
    out_shape += [jax.ShapeDtypeStruct((k, width), F32) for k in par_outs]
    return pl.pallas_call(
        body, name=name, grid=(n_tiles,), in_specs=in_specs, out_specs=out_specs, out_shape=out_shape,
        compiler_params=_cparams("parallel"),
    )(*[a for a, _ in cols], *[p for p, _ in params])


def _pick(n, options):
    for t in options:
        if n % t == 0:
            return t
    return n


MATMUL_VMEM = 40 * 1024 * 1024


def _matmul_tiles(M, N, K, out_bytes):
    best = None
    for tm in (1024, 512, 256, 128, M):
        for tn in (1408, 1024, 512, 256, 128, N):
            if M % tm or N % tn:
                continue
            if 2 * (2 * K * (tm + tn) + out_bytes * tm * tn) > MATMUL_VMEM:
                continue
            if best is None or tm * tn > best[0] * best[1]:
                best = (tm, tn)
    return best


def _matmul(a, b, *, name, nt=False, ta=False, add=None, out_dtype=F32, after=None, b_cols=None):
    K, M = a.shape[::-1] if not ta else a.shape
    cb, width = b_cols if b_cols is not None else (0, b.shape[1])
    N = b.shape[0] if nt else width
    assert width == K or not nt
    tm, tn = _matmul_tiles(M, N, K, 4 * (1 + (add is not None)) if out_dtype == F32 else 2 + 4 * (add is not None))
    col0 = cb if nt else cb * (width // tn)

    def body(*refs):
        a_ref, b_ref = refs[0], refs[1]
        o_ref = refs[-1]
        dims = TN_DIMS if ta else NT_DIMS if nt else (((1,), (0,)), ((), ()))
        acc = lax.dot_general(a_ref[...], b_ref[...], dims, preferred_element_type=F32)
        if add is not None:
            acc = acc + refs[2][...]
        o_ref[...] = acc.astype(o_ref.dtype)

    in_specs = [pl.BlockSpec((K, tm), lambda i, j: (0, i)) if ta else pl.BlockSpec((tm, K), lambda i, j: (i, 0)),
                pl.BlockSpec((tn, K), lambda i, j: (j, col0)) if nt
                else pl.BlockSpec((K, tn), lambda i, j: (0, col0 + j))]
    args = [a, b]
    if add is not None:
        in_specs.append(pl.BlockSpec((tm, tn), lambda i, j: (i, j)))
        args.append(add)
    if after is not None:
        in_specs.append(pl.BlockSpec(memory_space=pl.ANY))
        args.append(after)
    return pl.pallas_call(
        body, name=name, grid=(M // tm, N // tn), in_specs=in_specs,
        out_specs=pl.BlockSpec((tm, tn), lambda i, j: (i, j)),
        out_shape=jax.ShapeDtypeStruct((M, N), out_dtype),
        compiler_params=_cparams("parallel", "parallel"),
    )(*args)


def _hgrn2_pre_fn(q_raw, ffw_raw, fbw_raw, lb_logits):
    lb = jax.nn.softmax(lb_logits, axis=0)[0:1]

    def gate(fr):
        f = lb + (1.0 - lb) * jax.nn.sigmoid(fr)
        return jnp.log(f), 1.0 - f

    lf_fw, k_fw = gate(ffw_raw)
    lf_bw, k_bw = gate(fbw_raw)
    return jax.nn.silu(q_raw), lf_fw, k_fw, lf_bw, k_bw


def _hgrn2_post_fn(o, g, norm_w):
    outs = []
    for h in range(HG_HEADS):
        sl = slice(h * HG_HEAD_DIM, (h + 1) * HG_HEAD_DIM)
        outs.append(_rms(o[:, sl], norm_w) * jax.nn.silu(g[:, sl]))
    return jnp.concatenate(outs, axis=1)


def _mamba_post_fn(y, z, norm_w):
    outs = []
    gw = D_INNER // SSM_GROUPS
    for gi in range(SSM_GROUPS):
        sl = slice(gi * gw, (gi + 1) * gw)
        outs.append(_rms(y[:, sl] * jax.nn.silu(z[:, sl]), norm_w[:, sl]))
    return jnp.concatenate(outs, axis=1)


def _shift_rows_impl(x, d):
    if d == 0:
        return x
    n, edge = x.shape[0], 8
    t = lax.broadcasted_iota(jnp.int32, (edge, x.shape[1]), 0)
    rolled = pltpu.roll(x, d % n, 0)
    if d > 0:
        return jnp.concatenate([jnp.where(t >= d, rolled[:edge], 0.0), rolled[edge:]], axis=0)
    return jnp.concatenate([rolled[:n - edge], jnp.where(t < edge + d, rolled[n - edge:], 0.0)], axis=0)


@functools.partial(jax.custom_vjp, nondiff_argnums=(1,))
def _shift_rows(x, d):
    return _shift_rows_impl(x, d)


_shift_rows.defvjp(lambda x, d: (_shift_rows_impl(x, d), None), lambda d, _, g: (_shift_rows_impl(g, -d),))


def _dwconv(x, w, b):
    taps = w.shape[0]
    c = (taps - 1) // 2
    y = b + w[0:1, :] * _shift_rows(x, c)
    for k in range(1, taps):
        y = y + w[k:k + 1, :] * _shift_rows(x, c - k)
    return y


def _ffn_mid_fn(gate, val, w, b):
    return jax.nn.silu(_dwconv(gate, w, b)) * val


def _mamba_conv_fn(xbc, w, b):
    return jax.nn.silu(_dwconv(xbc, w, b))


def _gla_consts(rev):
    return lax.broadcasted_iota(jnp.int32, (GLA_CHUNK, HG_HEAD_DIM), 0)


def _segment_cumsum(x, seg, rev):
    n = x.shape[0]
    t = lax.broadcasted_iota(jnp.int32, x.shape, 0) & (seg - 1)
    s = 1
    while s < seg:
        if rev:
            x = x + jnp.where(t < seg - s, pltpu.roll(x, n - s, 0), 0.0)
        else:
            x = x + jnp.where(t >= s, pltpu.roll(x, s, 0), 0.0)
        s *= 2
    return x


def _segment_cumsum_mxu(x, seg, rev):
    r = lax.broadcasted_iota(jnp.int32, (seg, seg), 0)
    c = lax.broadcasted_iota(jnp.int32, (seg, seg), 1)
    tri = ((r <= c) if rev else (r >= c)).astype(BF16)
    tri3 = jnp.concatenate([tri, tri, tri], axis=1)
    hi = x.astype(BF16)
    rest = x - hi.astype(F32)
    mid = rest.astype(BF16)
    lo = (rest - mid.astype(F32)).astype(BF16)
    outs = []
    for g in range(x.shape[0] // seg):
        rows = slice(g * seg, (g + 1) * seg)
        terms = jnp.concatenate([hi[rows], mid[rows], lo[rows]], axis=0)
        outs.append(jnp.dot(tri3, terms, preferred_element_type=F32))
    return jnp.concatenate(outs, axis=0)


@functools.partial(jax.custom_vjp, nondiff_argnums=(1, 2))
def _segment_cumsum_diff(x, seg, rev):
    return _segment_cumsum_mxu(x, seg, rev)


_segment_cumsum_diff.defvjp(lambda x, seg, rev: (_segment_cumsum_mxu(x, seg, rev), None),
                            lambda seg, rev, _, g: (_segment_cumsum_mxu(g, seg, not rev),))


def _gla_chunk(st, q, k, v, b, *, rev, consts, halves):
    row = consts
    c = q.shape[0]
    half = c // 2 if halves else c
    o = lax.dot_general((q * jnp.exp(b)).astype(BF16), st.astype(BF16), NT_DIMS, preferred_element_type=F32)
    spans = [slice(i, i + half) for i in range(0, c, half)]
    qs, bs, rows, os = [q[p] for p in spans], [b[p] for p in spans], [row[p] for p in spans], [o[p] for p in spans]

    def reach(s, h, masked):
        diff = bs[h] - b[s:s + 1]
        if masked:
            diff = jnp.where((rows[h] <= s) if rev else (rows[h] >= s), diff, -jnp.inf)
        return jnp.sum(qs[h] * (k[s:s + 1] * jnp.exp(diff)), axis=-1, keepdims=True) * v[s:s + 1]

    for s in range(c):
        own = s // half
        os[own] = os[own] + reach(s, own, True)
        if halves and own == (1 if rev else 0):
            os[1 - own] = os[1 - own] + reach(s, 1 - own, False)
    o = jnp.concatenate(os, axis=0) if halves else os[0]
    b_end = b[0:1] if rev else b[c - 1:c]
    kd = (k * jnp.exp(b_end - b)).astype(BF16)
    st_new = st * jnp.exp(b_end) + lax.dot_general(v.astype(BF16), kd, TN_DIMS, preferred_element_type=F32)
    return st_new, o


def _gla_fwd(q, k, v, v_cb, g, *, rev, add, name):
    L = q.shape[0]
    blk = min(GLA_BLOCK, L)
    nblk, nsub = L // blk, blk // GLA_CHUNK
    hd, hps = HG_HEAD_DIM, GLA_HEADS_PER_STEP
    wide = hd * hps

    def body(*refs):
        q_ref, k_ref, v_ref, g_ref = refs[:4]
        add_ref = refs[4] if add is not None else None
        o_ref, st_out, st_scr, b_scr = refs[-4:]
        consts = _gla_consts(rev)

        @pl.when(pl.program_id(1) == 0)
        def _():
            st_scr[...] = jnp.zeros(st_scr.shape, F32)

        b_scr[...] = _segment_cumsum(g_ref[...], GLA_CHUNK, rev)

        def step(i, carry):
            sl = pl.ds(pl.multiple_of((nsub - 1 - i if rev else i) * GLA_CHUNK, GLA_CHUNK), GLA_CHUNK)
            lanes = [slice(hi * hd, (hi + 1) * hd) for hi in range(hps)]
            ins = [(st_scr[hi], q_ref[sl, ln], k_ref[sl, ln], v_ref[sl, ln], b_scr[sl, ln])
                   for hi, ln in enumerate(lanes)]
            adds = [add_ref[sl, ln] for ln in lanes] if add_ref is not None else None
            outs = [_gla_chunk(*args, rev=rev, consts=consts, halves=True) for args in ins]
            for hi, ln in enumerate(lanes):
                st_out[hi, i] = ins[hi][0]
                st_scr[hi] = outs[hi][0]
                o_ref[sl, ln] = outs[hi][1] if adds is None else outs[hi][1] + adds[hi]
            return carry

        lax.fori_loop(0, nsub, step, 0)

    def pos(j):
        return nblk - 1 - j if rev else j

    spec = pl.BlockSpec((blk, wide), lambda h, j: (pos(j), h))
    in_specs = [spec, spec, pl.BlockSpec((blk, wide), lambda h, j: (pos(j), v_cb // hps + h)), spec]
    args = [q, k, v, g]
    if add is not None:
        in_specs.append(spec)
        args.append(add)
    return pl.pallas_call(
        body, name=name, grid=(HG_HEADS // hps, nblk), in_specs=in_specs,
        out_specs=[spec, pl.BlockSpec((hps, None, nsub, hd, hd), lambda h, j: (h, j, 0, 0, 0))],
        out_shape=[jax.ShapeDtypeStruct((L, HG_HEADS * hd), F32),
                   jax.ShapeDtypeStruct((HG_HEADS, nblk, nsub, hd, hd), F32)],
        scratch_shapes=[pltpu.VMEM((hps, hd, hd), F32), pltpu.VMEM((blk, wide), F32)],
        compiler_params=_cparams("parallel", "arbitrary"),
    )(*args)


def _gla_bwd(q, k, v, v_cb, g, do, states, *, rev, adds, name):
    L = q.shape[0]
    blk = min(GLA_BLOCK, L)
    nblk, nsub = L // blk, blk // GLA_CHUNK
    hd, hps = HG_HEAD_DIM, GLA_HEADS_PER_STEP
    wide = hd * hps
    n_add = 0 if adds is None else 2

    def body(*refs):
        q_ref, k_ref, v_ref, g_ref, do_ref, st_in = refs[:6]
        add_refs = refs[6:6 + n_add]
        dq_ref, dk_ref, dv_ref, dg_ref, dst_scr, b_scr, db_scr = refs[6 + n_add:]
        consts = _gla_consts(rev)
        chunk = functools.partial(_gla_chunk, rev=rev, consts=consts, halves=False)

        @pl.when(pl.program_id(1) == 0)
        def _():
            dst_scr[...] = jnp.zeros(dst_scr.shape, F32)

        b_scr[...] = _segment_cumsum(g_ref[...], GLA_CHUNK, rev)

        def rows(i):
            return pl.ds(pl.multiple_of((nsub - 1 - i if rev else i) * GLA_CHUNK, GLA_CHUNK), GLA_CHUNK)

        def step(n, carry):
            i = nsub - 1 - n
            sl = rows(i)
            lanes = [slice(hi * hd, (hi + 1) * hd) for hi in range(hps)]
            ins = [(st_in[hi, i], q_ref[sl, ln], k_ref[sl, ln], v_ref[sl, ln], b_scr[sl, ln])
                   for hi, ln in enumerate(lanes)]
            cts = [(dst_scr[hi], do_ref[sl, ln]) for hi, ln in enumerate(lanes)]
            adds = [(add_refs[0][sl, ln], add_refs[1][sl, ln]) for ln in lanes] if n_add else None
            outs = [jax.vjp(chunk, *args)[1](ct) for args, ct in zip(ins, cts)]
            for hi, ln in enumerate(lanes):
                dst, dq, dk, dv, db = outs[hi]
                dst_scr[hi] = dst
                dq_ref[sl, ln] = dq if adds is None else dq + adds[hi][0]
                dk_ref[sl, ln] = dk
                dv_ref[sl, ln] = dv if adds is None else dv + adds[hi][1]
                db_scr[sl, ln] = db
            return carry

        lax.fori_loop(0, nsub, step, 0)
        dg_ref[...] = _segment_cumsum(db_scr[...], GLA_CHUNK, not rev)

    def pos(j):
        p = nblk - 1 - j
        return nblk - 1 - p if rev else p

    spec = pl.BlockSpec((blk, wide), lambda h, j: (pos(j), h))
    in_specs = [spec, spec, pl.BlockSpec((blk, wide), lambda h, j: (pos(j), v_cb // hps + h)), spec, spec,
                pl.BlockSpec((hps, None, nsub, hd, hd), lambda h, j: (h, nblk - 1 - j, 0, 0, 0))]
    args = [q, k, v, g, do, states]
    if adds is not None:
        in_specs += [spec, spec]
        args += list(adds)
    out = jax.ShapeDtypeStruct((L, HG_HEADS * hd), F32)
    return pl.pallas_call(
        body, name=name, grid=(HG_HEADS // hps, nblk), in_specs=in_specs,
        out_specs=[spec] * 4, out_shape=[out] * 4,
        scratch_shapes=[pltpu.VMEM((hps, hd, hd), F32), pltpu.VMEM((blk, wide), F32), pltpu.VMEM((blk, wide), F32)],
        compiler_params=_cparams("parallel", "arbitrary"),
    )(*args)


def _ssd_consts(rev):
    c = SSD_CHUNK
    gw = SSM_HPG * SSM_HEAD_DIM
    r2 = lax.broadcasted_iota(jnp.int32, (c, c), 0)
    c2 = lax.broadcasted_iota(jnp.int32, (c, c), 1)
    low = (r2 <= c2) if rev else (r2 >= c2)
    lane = lax.broadcasted_iota(jnp.int32, (1, gw), 1)
    return low, r2 == c2, lane, lane >> 6


def _expand_heads(v4, head_of_lane):
    first = head_of_lane[:, :LANES] == 0
    cols = [v4[:, j:j + 1] for j in range(SSM_HPG)]
    return jnp.concatenate([jnp.where(first, cols[0], cols[1]), jnp.where(first, cols[2], cols[3])], axis=1)


def _ssd_prep(dtr, bias, alog, dsk, *, rev, consts):
    head_of_lane = consts[-1]
    dt_l = _expand_heads(jax.nn.softplus(dtr + bias), head_of_lane)
    a_l = _expand_heads(-jnp.exp(alog), head_of_lane)
    return dt_l, _segment_cumsum_diff(dt_l * a_l, SSD_CHUNK, rev), _expand_heads(dsk, head_of_lane)


def _ssd_chunk(st, x, bm, cm, dt_l, acum, dsk_l, *, rev, consts, skip):
    low, eye, lane, head_of_lane = consts
    c = x.shape[0]
    xd = x * dt_l
    cb = lax.dot_general(cm.astype(BF16), bm.astype(BF16), NT_DIMS, preferred_element_type=F32)
    y = jnp.dot(cm.astype(BF16), st.astype(BF16), preferred_element_type=F32) * jnp.exp(acum)
    for j in range(SSM_HPG):
        acol = jnp.sum(jnp.where(lane == j * SSM_HEAD_DIM, acum, 0.0), axis=-1, keepdims=True)
        ab = jnp.broadcast_to(acol, (c, c))
        arow = jnp.sum(jnp.where(eye, ab, 0.0), axis=0, keepdims=True)
        lmat = jnp.exp(jnp.where(low, ab - arow, -jnp.inf))
        xj = jnp.where(head_of_lane == j, xd, 0.0)
        y = y + jnp.dot((cb * lmat).astype(BF16), xj.astype(BF16), preferred_element_type=F32)
    a_end = acum[0:1] if rev else acum[c - 1:c]
    xdec = (xd * jnp.exp(a_end - acum)).astype(BF16)
    st_new = st * jnp.exp(a_end) + lax.dot_general(bm.astype(BF16), xdec, TN_DIMS, preferred_element_type=F32)
    if skip:
        y = y + x * dsk_l
    return st_new, y


def _ssd_specs(L, rev):
    blk = min(SSD_BLOCK, L)
    nblk = L // blk
    gw = SSM_HPG * SSM_HEAD_DIM

    def pos(j):
        return nblk - 1 - j if rev else j

    return blk, nblk, gw, pos


def _ssd_fwd(xbc, dtr, bias, alog, dsk, *, rev, skip, add, name):
    L = xbc.shape[0]
    blk, nblk, gw, pos = _ssd_specs(L, rev)
    nsub = blk // SSD_CHUNK
    n = SSM_STATE

    gps = SSD_FWD_GROUPS_PER_STEP

    def body(*refs):
        x_ref, b_ref, c_ref, dt_ref, bias_ref, alog_ref, dsk_ref = refs[:7]
        add_ref = refs[7] if add is not None else None
        y_ref, st_out, st_scr, dt_scr, ac_scr = refs[-5:]
        consts = _ssd_consts(rev)
        xl = [slice(gi * gw, (gi + 1) * gw) for gi in range(gps)]
        nl = [slice(gi * n, (gi + 1) * n) for gi in range(gps)]

        @pl.when(pl.program_id(1) == 0)
        def _():
            st_scr[...] = jnp.zeros(st_scr.shape, F32)

        dsk_l = []
        for gi in range(gps):
            dt_scr[:, xl[gi]], ac_scr[:, xl[gi]], d = _ssd_prep(dt_ref[gi], bias_ref[gi], alog_ref[gi], dsk_ref[gi],
                                                              rev=rev, consts=consts)
            dsk_l.append(d)

        def step(i, carry):
            sl = pl.ds(pl.multiple_of((nsub - 1 - i if rev else i) * SSD_CHUNK, SSD_CHUNK), SSD_CHUNK)
            ins = [(st_scr[gi], x_ref[sl, xl[gi]], b_ref[sl, nl[gi]], c_ref[sl, nl[gi]], dt_scr[sl, xl[gi]],
                    ac_scr[sl, xl[gi]], dsk_l[gi]) for gi in range(gps)]
            adds = [add_ref[sl, xl[gi]] for gi in range(gps)] if add_ref is not None else None
            outs = [_ssd_chunk(*a, rev=rev, consts=consts, skip=skip) for a in ins]
            for gi in range(gps):
                st_out[gi, i] = ins[gi][0]
                st_scr[gi] = outs[gi][0]
                y_ref[sl, xl[gi]] = outs[gi][1] if adds is None else outs[gi][1] + adds[gi]
            return carry

        lax.fori_loop(0, nsub, step, 0)

    b0 = D_INNER // (n * gps)
    yspec = pl.BlockSpec((blk, gw * gps), lambda g, j: (pos(j), g))
    pspec = pl.BlockSpec((gps, 1, SSM_HPG), lambda g, j: (g, 0, 0))
    in_specs = [yspec,
                pl.BlockSpec((blk, n * gps), lambda g, j: (pos(j), b0 + g)),
                pl.BlockSpec((blk, n * gps), lambda g, j: (pos(j), b0 + SSM_GROUPS // gps + g)),
                pl.BlockSpec((gps, blk, SSM_HPG), lambda g, j: (g, pos(j), 0)),
                pspec, pspec, pspec]
    args = [xbc, xbc, xbc, dtr, bias, alog, dsk]
    if add is not None:
        in_specs.append(yspec)
        args.append(add)
    return pl.pallas_call(
        body, name=name, grid=(SSM_GROUPS // gps, nblk), in_specs=in_specs,
        out_specs=[yspec, pl.BlockSpec((gps, None, nsub, n, gw), lambda g, j: (g, j, 0, 0, 0))],
        out_shape=[jax.ShapeDtypeStruct((L, D_INNER), F32),
                   jax.ShapeDtypeStruct((SSM_GROUPS, nblk, nsub, n, gw), F32)],
        scratch_shapes=[pltpu.VMEM((gps, n, gw), F32), pltpu.VMEM((blk, gw * gps), F32),
                        pltpu.VMEM((blk, gw * gps), F32)],
        compiler_params=_cparams("parallel", "arbitrary"),
    )(*args)


def _ssd_bwd(xbc, dtr, bias, alog, dsk, dy, states, *, rev, skip, adds, name):
    L = xbc.shape[0]
    blk, nblk, gw, _ = _ssd_specs(L, rev)
    nsub = blk // SSD_CHUNK
    n = SSM_STATE
    n_add = 0 if adds is None else 3
    gps = SSD_BWD_GROUPS_PER_STEP

    def body(*refs):
        x_ref, b_ref, c_ref, dt_ref, bias_ref, alog_ref, dsk_ref, dy_ref, st_in = refs[:9]
        add_refs = refs[9:9 + n_add]
        outs = refs[9 + n_add:]
        dx_ref, db_ref, dc_ref, ddt_ref, dbias_ref, dalog_ref, ddsk_ref = outs[:7]
        dst_scr, dt_scr, ac_scr, ddt_scr, dac_scr = outs[7:]
        consts = _ssd_consts(rev)
        chunk = functools.partial(_ssd_chunk, rev=rev, consts=consts, skip=skip)
        prep = functools.partial(_ssd_prep, rev=rev, consts=consts)

        @pl.when(pl.program_id(1) == 0)
        def _():
            dst_scr[...] = jnp.zeros(dst_scr.shape, F32)
            dbias_ref[...] = jnp.zeros(dbias_ref.shape, F32)
            dalog_ref[...] = jnp.zeros(dalog_ref.shape, F32)
            ddsk_ref[...] = jnp.zeros(ddsk_ref.shape, F32)

        xl = [slice(gi * gw, (gi + 1) * gw) for gi in range(gps)]
        nl = [slice(gi * n, (gi + 1) * n) for gi in range(gps)]
        dsk_l, prep_vjp = [], []
        for gi in range(gps):
            narrow = (dt_ref[gi], bias_ref[gi], alog_ref[gi], dsk_ref[gi])
            (dt_scr[:, xl[gi]], ac_scr[:, xl[gi]], d), pv = jax.vjp(prep, *narrow)
            dsk_l.append(d)
            prep_vjp.append(pv)

        def step(k, ddsk_l):
            i = nsub - 1 - k
            sl = pl.ds(pl.multiple_of((nsub - 1 - i if rev else i) * SSD_CHUNK, SSD_CHUNK), SSD_CHUNK)
            ins = [(st_in[gi, i], x_ref[sl, xl[gi]], b_ref[sl, nl[gi]], c_ref[sl, nl[gi]], dt_scr[sl, xl[gi]],
                    ac_scr[sl, xl[gi]], dsk_l[gi]) for gi in range(gps)]
            cts = [(dst_scr[gi], dy_ref[sl, xl[gi]]) for gi in range(gps)]
            more = [(add_refs[0][sl, xl[gi]], add_refs[1][sl, nl[gi]], add_refs[2][sl, nl[gi]])
                    for gi in range(gps)] if n_add else None
            grads = [jax.vjp(chunk, *a)[1](ct) for a, ct in zip(ins, cts)]
            new = []
            for gi in range(gps):
                dst, dx, db, dc, ddt_l, dac, ddsk_k = grads[gi]
                dst_scr[gi] = dst
                dx_ref[sl, xl[gi]] = dx if more is None else dx + more[gi][0]
                db_ref[sl, nl[gi]] = db if more is None else db + more[gi][1]
                dc_ref[sl, nl[gi]] = dc if more is None else dc + more[gi][2]
                ddt_scr[sl, xl[gi]] = ddt_l
                dac_scr[sl, xl[gi]] = dac
                new.append(ddsk_l[gi] + ddsk_k)
            return tuple(new)

        ddsk_l = lax.fori_loop(0, nsub, step, tuple(jnp.zeros((1, gw), F32) for _ in range(gps)))
        for gi in range(gps):
            ddt, dbias, dalog, ddsk = prep_vjp[gi]((ddt_scr[:, xl[gi]], dac_scr[:, xl[gi]], ddsk_l[gi]))
            ddt_ref[gi] = ddt
            dbias_ref[gi] += dbias
            dalog_ref[gi] += dalog
            ddsk_ref[gi] += ddsk

    def pos(j):
        p = nblk - 1 - j
        return nblk - 1 - p if rev else p

    b0 = D_INNER // (n * gps)
    xspec = pl.BlockSpec((blk, gw * gps), lambda g, j: (pos(j), g))
    nspec = pl.BlockSpec((blk, n * gps), lambda g, j: (pos(j), g))
    dtspec = pl.BlockSpec((gps, blk, SSM_HPG), lambda g, j: (g, pos(j), 0))
    pspec = pl.BlockSpec((gps, 1, SSM_HPG), lambda g, j: (g, 0, 0))
    in_specs = [xspec,
                pl.BlockSpec((blk, n * gps), lambda g, j: (pos(j), b0 + g)),
                pl.BlockSpec((blk, n * gps), lambda g, j: (pos(j), b0 + SSM_GROUPS // gps + g)),
                dtspec, pspec, pspec, pspec, xspec,
                pl.BlockSpec((gps, None, nsub, n, gw), lambda g, j: (g, nblk - 1 - j, 0, 0, 0))]
    args = [xbc, xbc, xbc, dtr, bias, alog, dsk, dy, states]
    if adds is not None:
        in_specs += [xspec, nspec, nspec]
        args += list(adds)
    par = jax.ShapeDtypeStruct((SSM_GROUPS, 1, SSM_HPG), F32)
    return pl.pallas_call(
        body, name=name, grid=(SSM_GROUPS // gps, nblk), in_specs=in_specs,
        out_specs=[xspec, nspec, nspec, dtspec, pspec, pspec, pspec],
        out_shape=[jax.ShapeDtypeStruct((L, D_INNER), F32), jax.ShapeDtypeStruct((L, GN), F32),
                   jax.ShapeDtypeStruct((L, GN), F32), jax.ShapeDtypeStruct((SSM_GROUPS, L, SSM_HPG), F32),
                   par, par, par],
        scratch_shapes=[pltpu.VMEM((gps, n, gw), F32)] + [pltpu.VMEM((blk, gw * gps), F32)] * 4,
        compiler_params=_cparams("parallel", "arbitrary"),
    )(*args)


def _out_proj(y, w_out, h, next_norm, name):
    if next_norm is None:
        return _matmul(y, w_out, add=h, name=name), None
    M, K = y.shape
    N = w_out.shape[1]
    tm = 512

    def body(y_ref, w_ref, h_ref, n_ref, o_ref, u_ref):
        acc = jnp.dot(y_ref[...], w_ref[...], preferred_element_type=F32) + h_ref[...]
        o_ref[...] = acc
        u_ref[...] = _rms(acc, n_ref[...]).astype(BF16)

    rows = pl.BlockSpec((tm, N), lambda i: (i, 0))
    return pl.pallas_call(
        body, name=name, grid=(M // tm,),
        in_specs=[pl.BlockSpec((tm, K), lambda i: (i, 0)), pl.BlockSpec((K, N), lambda i: (0, 0)), rows,
                  pl.BlockSpec((1, N), lambda i: (0, 0))],
        out_specs=[rows, rows], out_shape=[jax.ShapeDtypeStruct((M, N), F32), jax.ShapeDtypeStruct((M, N), BF16)],
        compiler_params=_cparams("parallel"),
    )(y, w_out, h, next_norm)


def _norm_fwd(h, w, name):
    d = h.shape[1]
    return _row_kernel(lambda hv, wv: (_rms(hv, wv),), [(h, d, 0)], [w], [(d, BF16)], [], name=name)[0]


def _matmul_norm_bwd(a, b, h, w, dh_in, *, name, add=None, b_cols=None, after=None):
    M, K = a.shape
    N = b.shape[0]
    cb = 0 if b_cols is None else b_cols[0]
    tm = 512 if K <= 4096 else 256

    def body(*refs):
        a_ref, b_ref, h_ref, w_ref, dh_ref = refs[:5]
        o32_ref, o16_ref, dw_ref = refs[-3:]
        du = lax.dot_general(a_ref[...], b_ref[...], NT_DIMS, preferred_element_type=F32)
        if add is not None:
            du = du + refs[5][...]
        _, vjp = jax.vjp(_rms, h_ref[...], w_ref[...])
        dh, dw = vjp(du)
        dh = dh + dh_ref[...]
        o32_ref[...] = dh
        o16_ref[...] = dh.astype(BF16)

        @pl.when(pl.program_id(0) == 0)
        def _():
            dw_ref[...] = jnp.zeros(dw_ref.shape, F32)

        dw_ref[...] += dw

    rows = pl.BlockSpec((tm, N), lambda i: (i, 0))
    in_specs = [pl.BlockSpec((tm, K), lambda i: (i, 0)), pl.BlockSpec((N, K), lambda i: (0, cb)), rows,
                pl.BlockSpec((1, N), lambda i: (0, 0)), rows]
    args = [a, b, h, w, dh_in]
    if add is not None:
        in_specs.append(rows)
        args.append(add)
    if after is not None:
        in_specs.append(pl.BlockSpec(memory_space=pl.ANY))
        args.append(after)
    return pl.pallas_call(
        body, name=name, grid=(M // tm,), in_specs=in_specs,
        out_specs=[rows, rows, pl.BlockSpec((1, N), lambda i: (0, 0))],
        out_shape=[jax.ShapeDtypeStruct((M, N), F32), jax.ShapeDtypeStruct((M, N), BF16),
                   jax.ShapeDtypeStruct((1, N), F32)],
        compiler_params=_cparams("arbitrary"),
    )(*args)


def _ffn_fwd(h, norm_w, w_in, w_out, conv_w, conv_b, tag, u, next_norm=None):
    pg = _matmul(u, w_in, b_cols=(0, D_FF), name=f"{tag}_in_gate")
    pv = _matmul(u, w_in, b_cols=(1, D_FF), name=f"{tag}_in_val")
    y = _col_kernel(lambda g, v, w, b: (_ffn_mid_fn(g, v, w, b),), [(pg, 0), (pv, 0)], [(conv_w, 0), (conv_b, 0)],
                    [BF16], [], name=f"{tag}_mid", n_tiles=D_FF // COL_TILE)[0]
    h_out, u_next = _out_proj(y, w_out, h, next_norm, f"{tag}_out")
    return h_out, (u, pg, pv, y), u_next


def _ffn_bwd(h, dh, dh16, saved, norm_w, w_in, w_out, conv_w, conv_b, tag, after=None):
    u, pg, pv, y = saved
    dy = _matmul(dh16, w_out, nt=True, after=after, name=f"{tag}_out_dx")
    dw_out = _matmul(y, dh16, ta=True, name=f"{tag}_out_dw")

    def fn(g, v, ct, w, b):
        _, vjp = jax.vjp(_ffn_mid_fn, g, v, w, b)
        return vjp(ct)

    dpg, dpv, dcw, dcb = _col_kernel(fn, [(pg, 0), (pv, 0), (dy, 0)], [(conv_w, 0), (conv_b, 0)], [BF16, BF16],
                                     [FFN_CONV, 1], name=f"{tag}_mid_bwd", n_tiles=D_FF // COL_TILE)
    du = _matmul(dpg, w_in, nt=True, b_cols=(0, D_FF), name=f"{tag}_gate_dx")
    dw_gate = _matmul(u, dpg, ta=True, name=f"{tag}_gate_dw")
    dw_val = _matmul(u, dpv, ta=True, name=f"{tag}_val_dw")
    dh, dh16, dnw = _matmul_norm_bwd(dpv, w_in, h, norm_w, dh, add=du, b_cols=(1, D_FF), name=f"{tag}_val_dx_norm")
    return dh, dh16, dict(w_in=(dw_gate, dw_val), w_out=dw_out, conv_w=dcw, conv_b=dcb, norm=dnw)


def _hgrn2_fwd(h, norm_w, w_in, lb_logits, a_norm_w, w_out, after=None, next_norm=None):
    d = D_MODEL
    u = _norm_fwd(h, norm_w, "hg_norm")
    pa = _matmul(u, w_in, after=after, name="hg_in")
    qs, lf_fw, k_fw, lf_bw, k_bw = _row_kernel(
        _hgrn2_pre_fn, [(pa, d, 0), (pa, d, 1), (pa, d, 2)], [lb_logits], [(d, F32)] * 5, [], name="hg_pre")
    o_fw, st_fw = _gla_fwd(qs, k_fw, pa, 3 * HG_HEADS, lf_fw, rev=False, add=None, name="hg_gla_fw")
    o, st_bw = _gla_fwd(qs, k_bw, pa, 3 * HG_HEADS, lf_bw, rev=True, add=o_fw, name="hg_gla_bw")
    y = _row_kernel(lambda ov, gv, wv: (_hgrn2_post_fn(ov, gv, wv),), [(o, d, 0), (pa, d, 4)], [a_norm_w],
                    [(d, BF16)], [], name="hg_post")[0]
    h_out, u_next = _out_proj(y, w_out, h, next_norm, "hg_out")
    return h_out, (u, pa, qs, lf_fw, k_fw, lf_bw, k_bw, st_fw, st_bw, o, y), u_next


def _hgrn2_bwd(h, dh, dh16, saved, norm_w, w_in, lb_logits, a_norm_w, w_out, send, after=None):
    d = D_MODEL
    u, pa, qs, lf_fw, k_fw, lf_bw, k_bw, st_fw, st_bw, o, y = saved
    dy = _matmul(dh16, w_out, nt=True, after=after, name="hg_out_dx")
    dw_out = _matmul(y, dh16, ta=True, name="hg_out_dw")

    def post_bwd(ov, gv, ct, wv):
        _, vjp = jax.vjp(_hgrn2_post_fn, ov, gv, wv)
        return vjp(ct)

    do, dg, dnw = _row_kernel(post_bwd, [(o, d, 0), (pa, d, 4), (dy, d, 0)], [a_norm_w], [(d, F32), (d, F32)],
                              [(1, HG_HEAD_DIM)], name="hg_post_bwd")
    dq1, dk_fw, dv1, dlf_fw = _gla_bwd(qs, k_fw, pa, 3 * HG_HEADS, lf_fw, do, st_fw, rev=False, adds=None,
                                       name="hg_gla_fw_bwd")
    dqs, dk_bw, dv, dlf_bw = _gla_bwd(qs, k_bw, pa, 3 * HG_HEADS, lf_bw, do, st_bw, rev=True, adds=(dq1, dv1),
                                      name="hg_gla_bw_bwd")

    def pre_bwd(qr, fr, br, c0, c1, c2, c3, c4, dvv, dgv, lbl):
        _, vjp = jax.vjp(_hgrn2_pre_fn, qr, fr, br, lbl)
        dq, df, db, dlbl = vjp((c0, c1, c2, c3, c4))
        return jnp.concatenate([dq, df, db, dvv, dgv], axis=1), dlbl

    rows = [(pa, d, 0), (pa, d, 1), (pa, d, 2), (dqs, d, 0), (dlf_fw, d, 0), (dk_fw, d, 0), (dlf_bw, d, 0),
            (dk_bw, d, 0), (dv, d, 0), (dg, d, 0)]
    dpa, dlbl = _row_kernel(pre_bwd, rows, [lb_logits], [(5 * d, BF16)], [lb_logits.shape], name="hg_pre_bwd")
    dw_in = _matmul(u, dpa, ta=True, name="hg_in_dw")
    token = send(dw_in, dw_out)
    dh, dh16, dn1 = _matmul_norm_bwd(dpa, w_in, h, norm_w, dh, after=token, name="hg_in_dx_norm")
    return dh, dh16, dict(lb=dlbl, a_norm=dnw, norm=dn1)


def _group_params(p):
    return p.reshape(SSM_GROUPS, 1, SSM_HPG)


def _mamba_fwd(h, norm_w, w_z, w_xbc, w_dt, conv_w, conv_b, dt_bias, a_log, d_skip, b_norm_w, w_out, u, next_norm):
    L = h.shape[0]
    z = _matmul(u, w_z, name="mb_in_z")
    xbc_raw = _matmul(u, w_xbc, name="mb_in_xbc")
    dt_raw = _matmul(u, w_dt, name="mb_in_dt")
    xbc = _col_kernel(lambda xv, w, b: (_mamba_conv_fn(xv, w, b),), [(xbc_raw, 0)], [(conv_w, 0), (conv_b, 0)],
                      [F32], [], name="mb_conv", n_tiles=CONV_DIM // COL_TILE)[0]
    dtr = dt_raw.reshape(L, 2, SSM_GROUPS, SSM_HPG).transpose(1, 2, 0, 3)
    bias, alog = dt_bias.reshape(2, -1), a_log.reshape(2, -1)
    dsk = _group_params(d_skip.reshape(-1))
    y_fw, st_fw = _ssd_fwd(xbc, dtr[0], _group_params(bias[0]), _group_params(alog[0]), dsk, rev=False, skip=True,
                           add=None, name="mb_ssd_fw")
    ysum, st_bw = _ssd_fwd(xbc, dtr[1], _group_params(bias[1]), _group_params(alog[1]), dsk, rev=True, skip=False,
                           add=y_fw, name="mb_ssd_bw")
    y = _row_kernel(lambda yv, zv, wv: (_mamba_post_fn(yv, zv, wv),), [(ysum, D_INNER, 0), (z, D_INNER, 0)],
                    [b_norm_w], [(D_INNER, BF16)], [], name="mb_post")[0]
    h_out, u_next = _out_proj(y, w_out, h, next_norm, "mb_out")
    return h_out, (u, z, xbc_raw, xbc, dtr, st_fw, st_bw, ysum, y), u_next


def _mamba_bwd(h, dh, dh16, saved, norm_w, w_z, w_xbc, w_dt, conv_w, conv_b, dt_bias, a_log, d_skip, b_norm_w, w_out,
               after=None):
    L = h.shape[0]
    u, z, xbc_raw, xbc, dtr, st_fw, st_bw, ysum, y = saved
    dy = _matmul(dh16, w_out, nt=True, after=after, name="mb_out_dx")
    dw_out = _matmul(y, dh16, ta=True, name="mb_out_dw")

    def post_bwd(yv, zv, ct, wv):
        _, vjp = jax.vjp(_mamba_post_fn, yv, zv, wv)
        return vjp(ct)

    dys, dz, dbn = _row_kernel(post_bwd, [(ysum, D_INNER, 0), (z, D_INNER, 0), (dy, D_INNER, 0)], [b_norm_w],
                               [(D_INNER, F32), (D_INNER, BF16)], [(1, D_INNER)], name="mb_post_bwd")
    bias, alog = dt_bias.reshape(2, -1), a_log.reshape(2, -1)
    dsk = _group_params(d_skip.reshape(-1))
    dx1, db1, dc1, ddt_fw, dbias_fw, dalog_fw, ddsk = _ssd_bwd(
        xbc, dtr[0], _group_params(bias[0]), _group_params(alog[0]), dsk, dys, st_fw, rev=False, skip=True,
        adds=None, name="mb_ssd_fw_bwd")
    dx, db, dc, ddt_bw, dbias_bw, dalog_bw, _ = _ssd_bwd(
        xbc, dtr[1], _group_params(bias[1]), _group_params(alog[1]), dsk, dys, st_bw, rev=True, skip=False,
        adds=(dx1, db1, dc1), name="mb_ssd_bw_bwd")

    def conv_bwd(n_tiles, ct, first):
        def fn(xv, ctv, w, b):
            _, vjp = jax.vjp(_mamba_conv_fn, xv, w, b)
            return vjp(ctv)
        return _col_kernel(fn, [(xbc_raw, first), (ct, 0)], [(conv_w, first), (conv_b, first)], [BF16],
                           [SSM_CONV, 1], name=f"mb_conv_bwd_{first}", n_tiles=n_tiles)

    nx, nb = D_INNER // COL_TILE, GN // COL_TILE
    parts = [conv_bwd(nx, dx, 0), conv_bwd(nb, db, nx), conv_bwd(nb, dc, nx + nb)]
    dxbc = jnp.concatenate([p[0] for p in parts], axis=1)
    dcw = jnp.concatenate([p[1] for p in parts], axis=1)
    dcb = jnp.concatenate([p[2] for p in parts], axis=1)
    ddt = jnp.stack([ddt_fw, ddt_bw]).transpose(2, 0, 1, 3).reshape(L, 2 * SSM_GROUPS * SSM_HPG).astype(BF16)
    du = _matmul(dz, w_z, nt=True, name="mb_z_dx")
    du = _matmul(dxbc, w_xbc, nt=True, add=du, name="mb_xbc_dx")
    dw_in = jnp.concatenate([_matmul(u, dz, ta=True, name="mb_z_dw"), _matmul(u, dxbc, ta=True, name="mb_xbc_dw"),
                             _matmul(u, ddt, ta=True, name="mb_dt_dw")], axis=1)
    dh, dh16, dn1 = _matmul_norm_bwd(ddt, w_dt, h, norm_w, dh, add=du, name="mb_dt_dx_norm")
    grads = dict(w_in=dw_in, w_out=dw_out, conv_w=dcw, conv_b=dcb, b_norm=dbn, norm=dn1,
                 dt_bias=jnp.stack([dbias_fw, dbias_bw]).reshape(1, 2, -1),
                 a_log=jnp.stack([dalog_fw, dalog_bw]).reshape(1, 2, -1), d_skip=ddsk.reshape(1, -1))
    return dh, dh16, grads


def _loss_head(h, target, w):
    d = h.shape[1]

    def fn(hv, tv, wv):
        def loss(hv, wv):
            err = _rms(hv, wv) - tv
            return 0.5 * jnp.sum(jnp.mean(err * err, axis=-1, keepdims=True), axis=0, keepdims=True)
        val, vjp = jax.vjp(loss, hv, wv)
        dh, dw = vjp(jnp.ones((1, 1), F32))
        return dh, dh, val, dw

    return _row_kernel(fn, [(h, d, 0), (target, d, 0)], [w], [(d, F32), (d, BF16)], [(1, 1), (1, d)], name="loss_head")


def _local_step(x, target, rep, comm):
    def ffn_args(i, w, small):
        return (rep["norm2_w"][i:i + 1], w["w_in"], w["w_out"], small["ffn_conv_w"][i], rep["ffn_conv_b"][i:i + 1])

    def mamba_args(w, small):
        w_in = w["w_in"]
        return (rep["norm1_w"][1:2], w_in[:, :D_INNER], w_in[:, D_INNER:D_INNER + CONV_DIM],
                w_in[:, D_INNER + CONV_DIM:], small["b_conv_w"][0], small["b_conv_b"], rep["b_dt_bias"],
                rep["b_a_log"], rep["b_d_skip"], small["b_norm_w"], w["w_out"])

    w_hg = comm.weights("hg", None)
    hg = (rep["norm1_w"][0:1], w_hg["w_in"], rep["a_lb_logits"], rep["a_norm_w"], w_hg["w_out"])
    h0 = x
    h1, s_hg, u1 = _hgrn2_fwd(h0, *hg, after=w_hg.get("token"), next_norm=rep["norm2_w"][0:1])
    w_f0 = comm.weights("f0", h1)
    small = w_f0["small"]
    f0 = ffn_args(0, w_f0, small)
    h2, s_f0, u2 = _ffn_fwd(h1, *f0, "ffn0", u1, next_norm=rep["norm1_w"][1:2])
    mb = mamba_args(comm.weights("mb", h2), small)
    h3, s_mb, u3 = _mamba_fwd(h2, *mb, u2, rep["norm2_w"][1:2])
    f1 = ffn_args(1, comm.weights("f1", h3), small)
    h4, s_f1, _ = _ffn_fwd(h3, *f1, "ffn1", u3)
    dh, dh16, loss, d_final = _loss_head(h4, target, rep["final_norm_w"].reshape(1, -1))

    dh, dh16, g_f1 = _ffn_bwd(h3, dh, dh16, s_f1, *f1, "ffn1")
    token = comm.send("f1", dict(w_in=g_f1["w_in"], w_out=g_f1["w_out"]))
    dh, dh16, g_mb = _mamba_bwd(h2, dh, dh16, s_mb, *mb, after=token)
    token = comm.send("mb", dict(w_in=g_mb["w_in"], w_out=g_mb["w_out"]))
    dh, dh16, g_f0 = _ffn_bwd(h1, dh, dh16, s_f0, *f0, "ffn0", after=token)
    token = comm.send("f0", dict(w_in=g_f0["w_in"], w_out=g_f0["w_out"]))
    small_grads = dict(b_conv_w=g_mb["conv_w"][None], b_conv_b=g_mb["conv_b"], b_norm_w=g_mb["b_norm"],
                       ffn_conv_w=jnp.stack([g_f0["conv_w"], g_f1["conv_w"]]))
    dh, dh16, g_hg = _hgrn2_bwd(
        h0, dh, dh16, s_hg, *hg, after=token,
        send=lambda dw_in, dw_out: comm.send("hg", dict(w_in=dw_in, w_out=dw_out, small=small_grads)))
    grads = dict(
        norm1_w=jnp.concatenate([g_hg["norm"], g_mb["norm"]], axis=0),
        norm2_w=jnp.concatenate([g_f0["norm"], g_f1["norm"]], axis=0),
        a_lb_logits=g_hg["lb"], a_norm_w=g_hg["a_norm"], b_dt_bias=g_mb["dt_bias"], b_a_log=g_mb["a_log"],
        b_d_skip=g_mb["d_skip"], ffn_conv_b=jnp.concatenate([g_f0["conv_b"], g_f1["conv_b"]], axis=0),
        final_norm_w=d_final.reshape(-1),
    )
    return loss, dh, grads


def _mesh_pos():
    return lax.axis_index("x"), lax.axis_index("y"), lax.axis_index("c")


N_COPIES = N_DEV - 1


def _comm_call(body, ins, out_shape, name):
    n = len(ins)
    hbm = pl.BlockSpec(memory_space=pl.ANY)
    return pl.pallas_call(
        body, name=name, out_shape=out_shape, in_specs=[hbm] * n, out_specs=[hbm] * n,
        scratch_shapes=[pltpu.SemaphoreType.DMA((n * N_COPIES,)), pltpu.SemaphoreType.DMA((n * N_COPIES,)),
                        pltpu.SemaphoreType.DMA((n,))],
    )(*ins)


def _all_gather(shards, name):
    n = len(shards)

    def body(*refs):
        x_refs, out_refs = refs[:n], refs[n:2 * n]
        send_sems, recv_sems, local_sems = refs[2 * n:]
        x, y, c = _mesh_pos()
        me, sibling = (x, y, c), (x, y, 1 - c)
        chips = [(1 - x, y), (x, 1 - y), (1 - x, 1 - y)]

        def copy(w, k, block, to, own=False):
            px, py, pc = block
            dst = out_refs[w].at[4 * px + 2 * py + pc]
            return pltpu.make_async_remote_copy(
                src_ref=x_refs[w] if own else dst, dst_ref=dst, send_sem=send_sems.at[w * N_COPIES + k],
                recv_sem=recv_sems.at[w * N_COPIES + k], device_id=to, device_id_type=MESH)

        mine = [pltpu.make_async_copy(x_refs[w], out_refs[w].at[4 * x + 2 * y + c], local_sems.at[w]) for w in range(n)]
        for cp in mine:
            cp.start()
        first = [copy(w, 1 + j, me, (*chip, c), own=True) for j, chip in enumerate(chips) for w in range(n)]
        first += [copy(w, 0, me, sibling, own=True) for w in range(n)]
        for cp in first:
            cp.start()
        passed = []
        for j, chip in enumerate(chips):
            for w in range(n):
                copy(w, 1 + j, (*chip, c), me).wait_recv()
                passed.append(copy(w, 4 + j, (*chip, c), sibling))
                passed[-1].start()
        for w in range(n):
            copy(w, 0, sibling, me).wait_recv()
        for j, chip in enumerate(chips):
            for w in range(n):
                copy(w, 4 + j, (*chip, 1 - c), me).wait_recv()
        for cp in first + passed:
            cp.wait_send()
        for cp in mine:
            cp.wait()

    out_shape = [jax.ShapeDtypeStruct((N_DEV, *s.shape), s.dtype) for s in shards]
    return _comm_call(body, shards, out_shape, name)


HBM_SPEC = pl.BlockSpec(memory_space=pltpu.HBM)
SEM_SPEC = pl.BlockSpec(memory_space=pltpu.SEMAPHORE)
EFFECT = pltpu.SideEffectType.DATAFLOW_SIDE_EFFECTING
PEER_ORDER = (4, 2, 6, 5, 3, 7, 1)


def _peer_copies(src_refs, land_refs, send_sems, recv_sems, gather):
    x, y, c = _mesh_pos()
    me = 4 * x + 2 * y + c
    copies = []
    for k in PEER_ORDER:
        px = 1 - x if k & 4 else x
        py = 1 - y if k & 2 else y
        pc = 1 - c if k & 1 else c
        for w, (src, land) in enumerate(zip(src_refs, land_refs)):
            copies.append(pltpu.make_async_remote_copy(
                src_ref=src if gather else src.at[4 * px + 2 * py + pc],
                dst_ref=land.at[me] if gather else land.at[k - 1],
                send_sem=send_sems.at[w * N_COPIES + k - 1], recv_sem=recv_sems.at[w * N_COPIES + k - 1],
                device_id=(px, py, pc), device_id_type=MESH))
    return copies


def _all_gather_small(shard, name):
    def body(x_ref, out_ref, send_sems, recv_sems):
        x, y, c = _mesh_pos()
        out_ref[4 * x + 2 * y + c] = x_ref[...]
        copies = _peer_copies([x_ref], [out_ref], send_sems, recv_sems, True)
        for cp in copies:
            cp.start()
        for cp in copies:
            cp.wait()

    vmem = pl.BlockSpec(memory_space=pltpu.VMEM)
    return pl.pallas_call(
        body, name=name, out_shape=jax.ShapeDtypeStruct((N_DEV, *shard.shape), shard.dtype), in_specs=[vmem],
        out_specs=vmem, scratch_shapes=[pltpu.SemaphoreType.DMA((N_COPIES,)), pltpu.SemaphoreType.DMA((N_COPIES,))],
    )(shard)


def _copies_start(srcs, *, gather, after, name):
    n = len(srcs)
    lands = [lax.empty(((N_DEV,) + s.shape) if gather else ((N_COPIES,) + s.shape[1:]), s.dtype) for s in srcs]

    def body(*refs):
        src_refs, land_refs = refs[:n], refs[n:2 * n]
        send_sems, recv_sems = refs[-2 * n - 3], refs[-2 * n - 2]
        for cp in _peer_copies(src_refs, land_refs, send_sems, recv_sems, gather):
            cp.start()
        refs[-1][...] = jnp.zeros(refs[-1].shape, F32)

    ins = [pltpu.with_memory_space_constraint(a, pltpu.HBM) for a in srcs + lands]
    in_specs = [HBM_SPEC] * (2 * n)
    if after is not None:
        ins.append(after)
        in_specs.append(pl.BlockSpec(memory_space=pl.ANY))
    sems = pltpu.SemaphoreType.DMA((n * N_COPIES,))
    out = pl.pallas_call(
        body, name=name,
        out_shape=(sems, sems, *[pltpu.HBM(a.shape, a.dtype) for a in srcs + lands], jax.ShapeDtypeStruct((8, LANES), F32)),
        in_specs=in_specs,
        out_specs=(SEM_SPEC, SEM_SPEC, *[HBM_SPEC] * (2 * n), pl.BlockSpec(memory_space=pltpu.VMEM)),
        input_output_aliases={i: 2 + i for i in range(2 * n)},
        compiler_params=pltpu.CompilerParams(has_side_effects=EFFECT),
    )(*ins)
    return out[0], out[1], list(out[2:2 + n]), list(out[2 + n:2 + 2 * n]), out[-1]


def _copies_wait(started, *, gather, after, name):
    send_sems, recv_sems, srcs, lands, _ = started
    n = len(srcs)

    def body(*refs):
        src_refs, land_refs = refs[:n], refs[n:2 * n]
        for cp in _peer_copies(src_refs, land_refs, refs[2 * n], refs[2 * n + 1], gather):
            cp.wait_send()
            cp.wait_recv()

    out = pl.pallas_call(
        body, name=name, out_shape=tuple(pltpu.HBM(a.shape, a.dtype) for a in srcs + lands),
        in_specs=[HBM_SPEC] * (2 * n) + [SEM_SPEC, SEM_SPEC, pl.BlockSpec(memory_space=pl.ANY)],
        out_specs=tuple([HBM_SPEC] * (2 * n)), input_output_aliases={i: i for i in range(2 * n)},
        compiler_params=pltpu.CompilerParams(has_side_effects=EFFECT),
    )(*srcs, *lands, send_sems, recv_sems, after)
    return list(out[n:])


def _adamw(got, w, m, v, *, name, tile, own=None, layer=None, into=None):
    rows, width = w.shape[-2:]
    n_got = got.shape[0]
    c1 = 1.0 / (1.0 - ADAM_B1 ** ADAM_STEP)
    c2 = 1.0 / (1.0 - ADAM_B2 ** ADAM_STEP)

    def body(*refs):
        got_ref, w_ref, m_ref, v_ref = refs[:4]
        g_ref, d_ref, nm_ref, nv_ref = refs[-4:]
        g = got_ref[0] if own is None else refs[4][...] + got_ref[0]
        for s in range(1, n_got):
            g = g + got_ref[s]
        m_new = ADAM_B1 * m_ref[...] + (1.0 - ADAM_B1) * g
        v_new = ADAM_B2 * v_ref[...] + (1.0 - ADAM_B2) * (g * g)
        g_ref[...] = g
        nm_ref[...] = m_new
        nv_ref[...] = v_new
        d_ref[...] = -ADAM_LR * ((m_new * c1) / (jnp.sqrt(v_new * c2) + ADAM_EPS) + ADAM_WD * w_ref[...])

    spec = pl.BlockSpec((tile, width), lambda i: (i, 0))
    wspec = spec if layer is None else pl.BlockSpec((None, tile, width), lambda i: (layer, i, 0))
    args = [got, w, m, v] + ([] if own is None else [own])
    in_specs = [pl.BlockSpec((n_got, tile, width), lambda i: (0, i, 0))] + [wspec] * 3 + [spec] * (len(args) - 4)
    aliases = {}
    if into is not None:
        aliases = {len(args) + i: i for i in range(4)}
        args += list(into)
        in_specs += [pl.BlockSpec(memory_space=pl.ANY)] * 4
    return pl.pallas_call(
        body, name=name, grid=(rows // tile,), in_specs=in_specs, out_specs=[wspec] * 4,
        out_shape=[jax.ShapeDtypeStruct(w.shape, F32)] * 4, input_output_aliases=aliases,
        compiler_params=_cparams("parallel"),
    )(*args)


SHARDED = dict(a_w_in=(2, True), a_w_out=(1, True), b_w_in=(2, True), b_w_out=(1, True), ffn_w_in=(2, True),
               ffn_w_out=(1, True), b_conv_w=(2, False), b_conv_b=(1, False), b_norm_w=(1, False), ffn_conv_w=(2, False))
REPLICATED = ("norm1_w", "norm2_w", "a_lb_logits", "a_norm_w", "b_dt_bias", "b_a_log", "b_d_skip", "ffn_conv_b",
              "final_norm_w")
WEIGHTS = ("norm1_w", "norm2_w", "a_w_in", "a_lb_logits", "a_norm_w", "a_w_out", "b_w_in", "b_conv_w", "b_conv_b",
           "b_dt_bias", "b_a_log", "b_d_skip", "b_norm_w", "b_w_out", "ffn_w_in", "ffn_conv_w", "ffn_conv_b",
           "ffn_w_out", "final_norm_w")


def _pad_rows(flat, multiple):
    n = flat.shape[-1]
    per = PACK_W * multiple
    total = -(-n // per) * per
    flat = jnp.pad(flat, [(0, 0)] * (flat.ndim - 1) + [(0, total - n)])
    return flat.reshape(*flat.shape[:-1], total // PACK_W, PACK_W)


def _to_parts(full, axis, n=N_DEV):
    shp = full.shape
    t = full.reshape(*shp[:axis], n, shp[axis] // n, *shp[axis + 1:])
    return jnp.moveaxis(t, axis, 0)


def _from_parts(parts, axis):
    t = jnp.moveaxis(parts, 0, axis)
    shp = t.shape
    return t.reshape(*shp[:axis], shp[axis] * shp[axis + 1], *shp[axis + 2:])


BIG = tuple(n for n, (_, mm) in SHARDED.items() if mm)
SMALL = tuple(n for n, (_, mm) in SHARDED.items() if not mm)
SMALL_W = 512


def _small_rows(tree, lead):
    rows = []
    for n in SMALL:
        t = tree[n]
        t = t.reshape(*lead, -1, t.shape[-1])
        rows.append(jnp.pad(t, [(0, 0)] * (t.ndim - 1) + [(0, SMALL_W - t.shape[-1])]))
    buf = jnp.concatenate(rows, axis=-2)
    return jnp.pad(buf, [(0, 0)] * (buf.ndim - 2) + [(0, 16 - buf.shape[-2]), (0, 0)])


def _small_unrows(buf, like, lead):
    out, r = {}, 0
    for n in SMALL:
        shp = like[n].shape
        k = like[n].size // shp[-1]
        out[n] = buf[..., r:r + k, :shp[-1]].reshape(*lead, *shp)
        r += k
    return out


GROUPS = dict(hg=(("a_w_in", 0), ("a_w_out", 0)), f0=(("ffn_w_in", 0), ("ffn_w_out", 0)),
              mb=(("b_w_in", 0), ("b_w_out", 0)), f1=(("ffn_w_in", 1), ("ffn_w_out", 1)))


class _Comm:
    def __init__(self, local):
        x, y, c = _mesh_pos()
        self.me = 4 * x + 2 * y + c
        self.local = local
        self.shards = {g: [local[n][i].astype(BF16) for n, i in names] for g, names in GROUPS.items()}
        self.shards["f0"].append(_small_rows(local, ()))
        self.first = _all_gather(self.shards["hg"], "gather_hg")
        self.gathers, self.sent, token = {}, {}, None
        for g in ("f0", "mb", "f1"):
            self.gathers[g] = _copies_start(self.shards[g], gather=True, after=token, name=f"gather_{g}_start")
            token = self.gathers[g][-1]
        self.token = token

    def weights(self, group, after):
        if group == "hg":
            got = self.first
        else:
            lands = _copies_wait(self.gathers[group], gather=True, after=after, name=f"gather_{group}_wait")
            got = [lax.dynamic_update_index_in_dim(land, shard, self.me, 0)
                   for land, shard in zip(lands, self.shards[group])]
        out = dict(w_in=_from_parts(got[0], 1), w_out=_from_parts(got[1], 0))
        if group == "hg":
            out["token"] = self.token
        if group == "f0":
            small = _small_unrows(got[2], self.local, (N_DEV,))
            out["small"] = {n: _from_parts(small[n], SHARDED[n][0]) for n in SMALL}
        return out

    def send(self, group, grads):
        w_in = grads["w_in"]
        if isinstance(w_in, tuple):
            half = N_DEV // len(w_in)
            parts_in = jnp.concatenate([_to_parts(t, 1, half) for t in w_in], axis=0)
        else:
            parts_in = _to_parts(w_in, 1)
        parts = [parts_in, _to_parts(grads["w_out"], 0)]
        if "small" in grads:
            parts.append(_small_rows({n: _to_parts(grads["small"][n], SHARDED[n][0]) for n in SMALL}, (N_DEV,)))
        sent = [p.astype(BF16) for p in parts[:2]] + parts[2:] if group == "hg" else parts
        self.sent[group] = (parts, _copies_start(sent, gather=False, after=None, name=f"exchange_{group}_start"))
        return self.sent[group][1][-1]

    def finish(self, after, mom, var):
        res = {}
        for group in ("f1", "mb", "f0", "hg"):
            parts, started = self.sent[group]
            lands = _copies_wait(started, gather=False, after=after, name=f"exchange_{group}_wait")
            own = [lax.dynamic_index_in_dim(p, self.me, 0, keepdims=False) for p in parts]
            for (n, i), got, mine in zip(GROUPS[group], lands, own):
                res[n] = _adamw(got, self.local[n], mom[n], var[n], own=mine, layer=i, into=res.get(n),
                                name=f"adamw_{n}_{i}", tile=_pick(mine.shape[0], (256, 176, 128)))
                after = res[n][0]
        small = _adamw(lands[2], *[_small_rows(t, ()) for t in (self.local, mom, var)], own=own[2],
                       name="adamw_small", tile=16)
        return res, small


def _pack_small(tree, extra):
    flat = jnp.concatenate([tree[n].reshape(-1) for n in REPLICATED] + [extra.reshape(-1)])
    return _pad_rows(flat, 8)


def _unpack_small(pack, like):
    flat, out, off = pack.reshape(-1), {}, 0
    for n in REPLICATED:
        out[n] = flat[off:off + like[n].size].reshape(like[n].shape)
        off += like[n].size
    return out, flat[off]


def kernel(x, norm1_w, norm2_w, a_w_in, a_lb_logits, a_norm_w, a_w_out, b_w_in, b_conv_w, b_conv_b, b_dt_bias, b_a_log, b_d_skip, b_norm_w, b_w_out, ffn_w_in, ffn_conv_w, ffn_conv_b, ffn_w_out, final_norm_w, loss_target, m_norm1_w, m_norm2_w, m_a_w_in, m_a_lb_logits, m_a_norm_w, m_a_w_out, m_b_w_in, m_b_conv_w, m_b_conv_b, m_b_dt_bias, m_b_a_log, m_b_d_skip, m_b_norm_w, m_b_w_out, m_ffn_w_in, m_ffn_conv_w, m_ffn_conv_b, m_ffn_w_out, m_final_norm_w, v_norm1_w, v_norm2_w, v_a_w_in, v_a_lb_logits, v_a_norm_w, v_a_w_out, v_b_w_in, v_b_conv_w, v_b_conv_b, v_b_dt_bias, v_b_a_log, v_b_d_skip, v_b_norm_w, v_b_w_out, v_ffn_w_in, v_ffn_conv_w, v_ffn_conv_b, v_ffn_w_out, v_final_norm_w):
    given = dict(locals())
    local = {n: given[n] for n in WEIGHTS}
    mom = {n: given["m_" + n] for n in WEIGHTS}
    var = {n: given["v_" + n] for n in WEIGHTS}

    comm = _Comm(local)
    loss, grad_x, grads = _local_step(x[0], loss_target[0], local, comm)

    res, small_res = comm.finish(grad_x, mom, var)
    outs = ({}, {}, {}, {})
    for n in BIG:
        for out, r in zip(outs, res[n]):
            out[n] = r
    for out, r in zip(outs, small_res):
        out.update(_small_unrows(r, local, ()))
    out_g, out_d, out_m, out_v = outs

    small = _pack_small(grads, loss)
    rows = small.shape[0]
    got_s = _all_gather_small(small, "gather_small")
    zero = jnp.zeros((1,), F32)
    g, dlt, nm, nv = _adamw(got_s, _pack_small(local, zero), _pack_small(mom, zero), _pack_small(var, zero),
                            name="adamw_replicated", tile=rows)
    (rep_g, total), (rep_d, _), (rep_m, _), (rep_v, _) = (_unpack_small(t, local) for t in (g, dlt, nm, nv))
    out_g.update(rep_g)
    out_d.update(rep_d)
    out_m.update(rep_m)
    out_v.update(rep_v)

    return (total, grad_x[None], *[out_g[n] for n in WEIGHTS], *[out_d[n] for n in WEIGHTS],
            *[out_m[n] for n in WEIGHTS], *[out_v[n] for n in WEIGHTS])
```

```python
import functools

import jax
import jax.numpy as jnp
from jax import lax
from jax.experimental import pallas as pl
from jax.experimental.pallas import tpu as pltpu

F32 = jnp.float32
BF16 = jnp.bfloat16
MESH = pl.DeviceIdType.MESH

N_DEV = 8
EPS = 1e-6
D_MODEL = 1024
HG_HEADS = 8
HG_HEAD_DIM = 128
D_INNER = 2048
SSM_HEAD_DIM = 64
SSM_GROUPS = 8
SSM_HPG = 4
SSM_STATE = 128
GN = SSM_GROUPS * SSM_STATE
CONV_DIM = D_INNER + 2 * GN
D_FF = 2816
SSM_CONV = 5
FFN_CONV = 3

ADAM_LR = 0.001
ADAM_B1 = 0.9
ADAM_B2 = 0.999
ADAM_EPS = 1e-08
ADAM_WD = 0.01
ADAM_STEP = 10

LANES = 128
ROW_TILE = 256
COL_TILE = 128
GLA_CHUNK = 16
GLA_BLOCK = 256
GLA_HEADS_PER_STEP = 8
SSD_CHUNK = 128
SSD_FWD_GROUPS_PER_STEP = 4
SSD_BWD_GROUPS_PER_STEP = 1
SSD_BLOCK = 512
PACK_W = 1024
VMEM_LIMIT = 56 * 1024 * 1024

NT_DIMS = (((1,), (1,)), ((), ()))
TN_DIMS = (((0,), (0,)), ((), ()))


def _cparams(*sem):
    return pltpu.CompilerParams(dimension_semantics=sem, vmem_limit_bytes=VMEM_LIMIT)


def _rms(x, w):
    return x * lax.rsqrt(jnp.mean(x * x, axis=-1, keepdims=True) + EPS) * w


def _row_kernel(body_fn, rows, params, row_outs, acc_outs, *, name, tile=ROW_TILE):
    L = rows[0][0].shape[0]
    tile = min(tile, L)
    n_in = len(rows) + len(params)
    n_ro = len(row_outs)

    def body(*refs):
        outs = body_fn(*[r[...] for r in refs[:n_in]])
        for ref, o in zip(refs[n_in:n_in + n_ro], outs[:n_ro]):
            ref[...] = o.astype(ref.dtype)
        first = pl.program_id(0) == 0
        for ref, o in zip(refs[n_in + n_ro:], outs[n_ro:]):
            @pl.when(first)
            def _(ref=ref):
                ref[...] = jnp.zeros(ref.shape, ref.dtype)
            ref[...] += o

    in_specs = [pl.BlockSpec((tile, w), lambda i, cb=cb: (i, cb)) for _, w, cb in rows]
    in_specs += [pl.BlockSpec(p.shape, lambda i: (0, 0)) for p in params]
    out_specs = [pl.BlockSpec((tile, w), lambda i: (i, 0)) for w, _ in row_outs]
    out_specs += [pl.BlockSpec(s, lambda i: (0, 0)) for s in acc_outs]
    out_shape = [jax.ShapeDtypeStruct((L, w), dt) for w, dt in row_outs]
    out_shape += [jax.ShapeDtypeStruct(s, F32) for s in acc_outs]
    return pl.pallas_call(
        body, name=name, grid=(L // tile,), in_specs=in_specs, out_specs=out_specs, out_shape=out_shape,
        compiler_params=_cparams("arbitrary" if acc_outs else "parallel"),
    )(*[a for a, _, _ in rows], *params)


def _col_kernel(body_fn, cols, params, col_outs, par_outs, *, name, n_tiles):
    L = cols[0][0].shape[0]
    n_in = len(cols) + len(params)
    width = n_tiles * COL_TILE

    def body(*refs):
        outs = body_fn(*[r[...] for r in refs[:n_in]])
        for ref, o in zip(refs[n_in:], outs):
            ref[...] = o.astype(ref.dtype)

    in_specs = [pl.BlockSpec((L, COL_TILE), lambda j, cb=cb: (0, cb + j)) for _, cb in cols]
    in_specs += [pl.BlockSpec((p.shape[0], COL_TILE), lambda j, cb=cb: (0, cb + j)) for p, cb in params]
    out_specs = [pl.BlockSpec((L, COL_TILE), lambda j: (0, j)) for _ in col_outs]
    out_specs += [pl.BlockSpec((k, COL_TILE), lambda j: (0, j)) for k in par_outs]
    out_shape = [jax.ShapeDtypeStruct((L, width), dt) for dt in col_outs]
    out_shape += [jax.ShapeDtypeStruct((k, width), F32) for k in par_outs]
    return pl.pallas_call(
        body, name=name, grid=(n_tiles,), in_specs=in_specs, out_specs=out_specs, out_shape=out_shape,
        compiler_params=_cparams("parallel"),
    )(*[a for a, _ in cols], *[p for p, _ in params])


def _pick(n, options):
    for t in options:
        if n % t == 0:
            return t
    return n


MATMUL_VMEM = 40 * 1024 * 1024


def _matmul_tiles(M, N, K, out_bytes):
    best = None
    for tm in (1024, 512, 256, 128, M):
        for tn in (1408, 1024, 512, 256, 128, N):
            if M % tm or N % tn:
                continue
            if 2 * (2 * K * (tm + tn) + out_bytes * tm * tn) > MATMUL_VMEM:
                continue
            if best is None or tm * tn > best[0] * best[1]:
                best = (tm, tn)
    return best


def _matmul(a, b, *, name, nt=False, ta=False, add=None, out_dtype=F32, after=None, b_cols=None):
    K, M = a.shape[::-1] if not ta else a.shape
    cb, width = b_cols if b_cols is not None else (0, b.shape[1])
    N = b.shape[0] if nt else width
    assert width == K or not nt
    tm, tn = _matmul_tiles(M, N, K, 4 * (1 + (add is not None)) if out_dtype == F32 else 2 + 4 * (add is not None))
    col0 = cb if nt else cb * (width // tn)

    def body(*refs):
        a_ref, b_ref = refs[0], refs[1]
        o_ref = refs[-1]
        dims = TN_DIMS if ta else NT_DIMS if nt else (((1,), (0,)), ((), ()))
        acc = lax.dot_general(a_ref[...], b_ref[...], dims, preferred_element_type=F32)
        if add is not None:
            acc = acc + refs[2][...]
        o_ref[...] = acc.astype(o_ref.dtype)

    in_specs = [pl.BlockSpec((K, tm), lambda i, j: (0, i)) if ta else pl.BlockSpec((tm, K), lambda i, j: (i, 0)),
                pl.BlockSpec((tn, K), lambda i, j: (j, col0)) if nt
                else pl.BlockSpec((K, tn), lambda i, j: (0, col0 + j))]
    args = [a, b]
    if add is not None:
        in_specs.append(pl.BlockSpec((tm, tn), lambda i, j: (i, j)))
        args.append(add)
    if after is not None:
        in_specs.append(pl.BlockSpec(memory_space=pl.ANY))
        args.append(after)
    return pl.pallas_call(
        body, name=name, grid=(M // tm, N // tn), in_specs=in_specs,
        out_specs=pl.BlockSpec((tm, tn), lambda i, j: (i, j)),
        out_shape=jax.ShapeDtypeStruct((M, N), out_dtype),
        compiler_params=_cparams("parallel", "parallel"),
    )(*args)


def _hgrn2_pre_fn(q_raw, ffw_raw, fbw_raw, lb_logits):
    lb = jax.nn.softmax(lb_logits, axis=0)[0:1]

    def gate(fr):
        f = lb + (1.0 - lb) * jax.nn.sigmoid(fr)
        return jnp.log(f), 1.0 - f

    lf_fw, k_fw = gate(ffw_raw)
    lf_bw, k_bw = gate(fbw_raw)
    return jax.nn.silu(q_raw), lf_fw, k_fw, lf_bw, k_bw


def _hgrn2_post_fn(o, g, norm_w):
    outs = []
    for h in range(HG_HEADS):
        sl = slice(h * HG_HEAD_DIM, (h + 1) * HG_HEAD_DIM)
        outs.append(_rms(o[:, sl], norm_w) * jax.nn.silu(g[:, sl]))
    return jnp.concatenate(outs, axis=1)


def _mamba_post_fn(y, z, norm_w):
    outs = []
    gw = D_INNER // SSM_GROUPS
    for gi in range(SSM_GROUPS):
        sl = slice(gi * gw, (gi + 1) * gw)
        outs.append(_rms(y[:, sl] * jax.nn.silu(z[:, sl]), norm_w[:, sl]))
    return jnp.concatenate(outs, axis=1)


def _shift_rows_impl(x, d):
    if d == 0:
        return x
    n, edge = x.shape[0], 8
    t = lax.broadcasted_iota(jnp.int32, (edge, x.shape[1]), 0)
    rolled = pltpu.roll(x, d % n, 0)
    if d > 0:
        return jnp.concatenate([jnp.where(t >= d, rolled[:edge], 0.0), rolled[edge:]], axis=0)
    return jnp.concatenate([rolled[:n - edge], jnp.where(t < edge + d, rolled[n - edge:], 0.0)], axis=0)


@functools.partial(jax.custom_vjp, nondiff_argnums=(1,))
def _shift_rows(x, d):
    return _shift_rows_impl(x, d)


_shift_rows.defvjp(lambda x, d: (_shift_rows_impl(x, d), None), lambda d, _, g: (_shift_rows_impl(g, -d),))


def _dwconv(x, w, b):
    taps = w.shape[0]
    c = (taps - 1) // 2
    y = b + w[0:1, :] * _shift_rows(x, c)
    for k in range(1, taps):
        y = y + w[k:k + 1, :] * _shift_rows(x, c - k)
    return y


def _ffn_mid_fn(gate, val, w, b):
    return jax.nn.silu(_dwconv(gate, w, b)) * val


def _mamba_conv_fn(xbc, w, b):
    return jax.nn.silu(_dwconv(xbc, w, b))


def _gla_consts(rev):
    return lax.broadcasted_iota(jnp.int32, (GLA_CHUNK, HG_HEAD_DIM), 0)


def _segment_cumsum(x, seg, rev):
    n = x.shape[0]
    t = lax.broadcasted_iota(jnp.int32, x.shape, 0) & (seg - 1)
    s = 1
    while s < seg:
        if rev:
            x = x + jnp.where(t < seg - s, pltpu.roll(x, n - s, 0), 0.0)
        else:
            x = x + jnp.where(t >= s, pltpu.roll(x, s, 0), 0.0)
        s *= 2
    return x


def _segment_cumsum_mxu(x, seg, rev):
    r = lax.broadcasted_iota(jnp.int32, (seg, seg), 0)
    c = lax.broadcasted_iota(jnp.int32, (seg, seg), 1)
    tri = ((r <= c) if rev else (r >= c)).astype(BF16)
    tri3 = jnp.concatenate([tri, tri, tri], axis=1)
    hi = x.astype(BF16)
    rest = x - hi.astype(F32)
    mid = rest.astype(BF16)
    lo = (rest - mid.astype(F32)).astype(BF16)
    outs = []
    for g in range(x.shape[0] // seg):
        rows = slice(g * seg, (g + 1) * seg)
        terms = jnp.concatenate([hi[rows], mid[rows], lo[rows]], axis=0)
        outs.append(jnp.dot(tri3, terms, preferred_element_type=F32))
    return jnp.concatenate(outs, axis=0)


@functools.partial(jax.custom_vjp, nondiff_argnums=(1, 2))
def _segment_cumsum_diff(x, seg, rev):
    return _segment_cumsum_mxu(x, seg, rev)


_segment_cumsum_diff.defvjp(lambda x, seg, rev: (_segment_cumsum_mxu(x, seg, rev), None),
                            lambda seg, rev, _, g: (_segment_cumsum_mxu(g, seg, not rev),))


def _gla_chunk(st, q, k, v, b, *, rev, consts, halves):
    row = consts
    c = q.shape[0]
    half = c // 2 if halves else c
    o = lax.dot_general((q * jnp.exp(b)).astype(BF16), st.astype(BF16), NT_DIMS, preferred_element_type=F32)
    spans = [slice(i, i + half) for i in range(0, c, half)]
    qs, bs, rows, os = [q[p] for p in spans], [b[p] for p in spans], [row[p] for p in spans], [o[p] for p in spans]

    def reach(s, h, masked):
        diff = bs[h] - b[s:s + 1]
        if masked:
            diff = jnp.where((rows[h] <= s) if rev else (rows[h] >= s), diff, -jnp.inf)
        return jnp.sum(qs[h] * (k[s:s + 1] * jnp.exp(diff)), axis=-1, keepdims=True) * v[s:s + 1]

    for s in range(c):
        own = s // half
        os[own] = os[own] + reach(s, own, True)
        if halves and own == (1 if rev else 0):
            os[1 - own] = os[1 - own] + reach(s, 1 - own, False)
    o = jnp.concatenate(os, axis=0) if halves else os[0]
    b_end = b[0:1] if rev else b[c - 1:c]
    kd = (k * jnp.exp(b_end - b)).astype(BF16)
    st_new = st * jnp.exp(b_end) + lax.dot_general(v.astype(BF16), kd, TN_DIMS, preferred_element_type=F32)
    return st_new, o


def _gla_fwd(q, k, v, v_cb, g, *, rev, add, name):
    L = q.shape[0]
    blk = min(GLA_BLOCK, L)
    nblk, nsub = L // blk, blk // GLA_CHUNK
    hd, hps = HG_HEAD_DIM, GLA_HEADS_PER_STEP
    wide = hd * hps

    def body(*refs):
        q_ref, k_ref, v_ref, g_ref = refs[:4]
        add_ref = refs[4] if add is not None else None
        o_ref, st_out, st_scr, b_scr = refs[-4:]
        consts = _gla_consts(rev)

        @pl.when(pl.program_id(1) == 0)
        def _():
            st_scr[...] = jnp.zeros(st_scr.shape, F32)

        b_scr[...] = _segment_cumsum(g_ref[...], GLA_CHUNK, rev)

        def step(i, carry):
            sl = pl.ds(pl.multiple_of((nsub - 1 - i if rev else i) * GLA_CHUNK, GLA_CHUNK), GLA_CHUNK)
            lanes = [slice(hi * hd, (hi + 1) * hd) for hi in range(hps)]
            ins = [(st_scr[hi], q_ref[sl, ln], k_ref[sl, ln], v_ref[sl, ln], b_scr[sl, ln])
                   for hi, ln in enumerate(lanes)]
            adds = [add_ref[sl, ln] for ln in lanes] if add_ref is not None else None
            outs = [_gla_chunk(*args, rev=rev, consts=consts, halves=True) for args in ins]
            for hi, ln in enumerate(lanes):
                st_out[hi, i] = ins[hi][0]
                st_scr[hi] = outs[hi][0]
                o_ref[sl, ln] = outs[hi][1] if adds is None else outs[hi][1] + adds[hi]
            return carry

        lax.fori_loop(0, nsub, step, 0)

    def pos(j):
        return nblk - 1 - j if rev else j

    spec = pl.BlockSpec((blk, wide), lambda h, j: (pos(j), h))
    in_specs = [spec, spec, pl.BlockSpec((blk, wide), lambda h, j: (pos(j), v_cb // hps + h)), spec]
    args = [q, k, v, g]
    if add is not None:
        in_specs.append(spec)
        args.append(add)
    return pl.pallas_call(
        body, name=name, grid=(HG_HEADS // hps, nblk), in_specs=in_specs,
        out_specs=[spec, pl.BlockSpec((hps, None, nsub, hd, hd), lambda h, j: (h, j, 0, 0, 0))],
        out_shape=[jax.ShapeDtypeStruct((L, HG_HEADS * hd), F32),
                   jax.ShapeDtypeStruct((HG_HEADS, nblk, nsub, hd, hd), F32)],
        scratch_shapes=[pltpu.VMEM((hps, hd, hd), F32), pltpu.VMEM((blk, wide), F32)],
        compiler_params=_cparams("parallel", "arbitrary"),
    )(*args)


def _gla_bwd(q, k, v, v_cb, g, do, states, *, rev, adds, name):
    L = q.shape[0]
    blk = min(GLA_BLOCK, L)
    nblk, nsub = L // blk, blk // GLA_CHUNK
    hd, hps = HG_HEAD_DIM, GLA_HEADS_PER_STEP
    wide = hd * hps
    n_add = 0 if adds is None else 2

    def body(*refs):
        q_ref, k_ref, v_ref, g_ref, do_ref, st_in = refs[:6]
        add_refs = refs[6:6 + n_add]
        dq_ref, dk_ref, dv_ref, dg_ref, dst_scr, b_scr, db_scr = refs[6 + n_add:]
        consts = _gla_consts(rev)
        chunk = functools.partial(_gla_chunk, rev=rev, consts=consts, halves=False)

        @pl.when(pl.program_id(1) == 0)
        def _():
            dst_scr[...] = jnp.zeros(dst_scr.shape, F32)

        b_scr[...] = _segment_cumsum(g_ref[...], GLA_CHUNK, rev)

        def rows(i):
            return pl.ds(pl.multiple_of((nsub - 1 - i if rev else i) * GLA_CHUNK, GLA_CHUNK), GLA_CHUNK)

        def step(n, carry):
            i = nsub - 1 - n
            sl = rows(i)
            lanes = [slice(hi * hd, (hi + 1) * hd) for hi in range(hps)]
            ins = [(st_in[hi, i], q_ref[sl, ln], k_ref[sl, ln], v_ref[sl, ln], b_scr[sl, ln])
                   for hi, ln in enumerate(lanes)]
            cts = [(dst_scr[hi], do_ref[sl, ln]) for hi, ln in enumerate(lanes)]
            adds = [(add_refs[0][sl, ln], add_refs[1][sl, ln]) for ln in lanes] if n_add else None
            outs = [jax.vjp(chunk, *args)[1](ct) for args, ct in zip(ins, cts)]
            for hi, ln in enumerate(lanes):
                dst, dq, dk, dv, db = outs[hi]
                dst_scr[hi] = dst
                dq_ref[sl, ln] = dq if adds is None else dq + adds[hi][0]
                dk_ref[sl, ln] = dk
                dv_ref[sl, ln] = dv if adds is None else dv + adds[hi][1]
                db_scr[sl, ln] = db
            return carry

        lax.fori_loop(0, nsub, step, 0)
        dg_ref[...] = _segment_cumsum(db_scr[...], GLA_CHUNK, not rev)

    def pos(j):
        p = nblk - 1 - j
        return nblk - 1 - p if rev else p

    spec = pl.BlockSpec((blk, wide), lambda h, j: (pos(j), h))
    in_specs = [spec, spec, pl.BlockSpec((blk, wide), lambda h, j: (pos(j), v_cb // hps + h)), spec, spec,
                pl.BlockSpec((hps, None, nsub, hd, hd), lambda h, j: (h, nblk - 1 - j, 0, 0, 0))]
    args = [q, k, v, g, do, states]
    if adds is not None:
        in_specs += [spec, spec]
        args += list(adds)
    out = jax.ShapeDtypeStruct((L, HG_HEADS * hd), F32)
    return pl.pallas_call(
        body, name=name, grid=(HG_HEADS // hps, nblk), in_specs=in_specs,
        out_specs=[spec] * 4, out_shape=[out] * 4,
        scratch_shapes=[pltpu.VMEM((hps, hd, hd), F32), pltpu.VMEM((blk, wide), F32), pltpu.VMEM((blk, wide), F32)],
        compiler_params=_cparams("parallel", "arbitrary"),
    )(*args)


def _ssd_consts(rev):
    c = SSD_CHUNK
    gw = SSM_HPG * SSM_HEAD_DIM
    r2 = lax.broadcasted_iota(jnp.int32, (c, c), 0)
    c2 = lax.broadcasted_iota(jnp.int32, (c, c), 1)
    low = (r2 <= c2) if rev else (r2 >= c2)
    lane = lax.broadcasted_iota(jnp.int32, (1, gw), 1)
    return low, r2 == c2, lane, lane >> 6


def _expand_heads(v4, head_of_lane):
    first = head_of_lane[:, :LANES] == 0
    cols = [v4[:, j:j + 1] for j in range(SSM_HPG)]
    return jnp.concatenate([jnp.where(first, cols[0], cols[1]), jnp.where(first, cols[2], cols[3])], axis=1)


def _ssd_prep(dtr, bias, alog, dsk, *, rev, consts):
    head_of_lane = consts[-1]
    dt_l = _expand_heads(jax.nn.softplus(dtr + bias), head_of_lane)
    a_l = _expand_heads(-jnp.exp(alog), head_of_lane)
    return dt_l, _segment_cumsum_diff(dt_l * a_l, SSD_CHUNK, rev), _expand_heads(dsk, head_of_lane)


def _ssd_chunk(st, x, bm, cm, dt_l, acum, dsk_l, *, rev, consts, skip):
    low, eye, lane, head_of_lane = consts
    c = x.shape[0]
    xd = x * dt_l
    cb = lax.dot_general(cm.astype(BF16), bm.astype(BF16), NT_DIMS, preferred_element_type=F32)
    y = jnp.dot(cm.astype(BF16), st.astype(BF16), preferred_element_type=F32) * jnp.exp(acum)
    for j in range(SSM_HPG):
        acol = jnp.sum(jnp.where(lane == j * SSM_HEAD_DIM, acum, 0.0), axis=-1, keepdims=True)
        ab = jnp.broadcast_to(acol, (c, c))
        arow = jnp.sum(jnp.where(eye, ab, 0.0), axis=0, keepdims=True)
        lmat = jnp.exp(jnp.where(low, ab - arow, -jnp.inf))
        xj = jnp.where(head_of_lane == j, xd, 0.0)
        y = y + jnp.dot((cb * lmat).astype(BF16), xj.astype(BF16), preferred_element_type=F32)
    a_end = acum[0:1] if rev else acum[c - 1:c]
    xdec = (xd * jnp.exp(a_end - acum)).astype(BF16)
    st_new = st * jnp.exp(a_end) + lax.dot_general(bm.astype(BF16), xdec, TN_DIMS, preferred_element_type=F32)
    if skip:
        y = y + x * dsk_l
    return st_new, y


def _ssd_specs(L, rev):
    blk = min(SSD_BLOCK, L)
    nblk = L // blk
    gw = SSM_HPG * SSM_HEAD_DIM

    def pos(j):
        return nblk - 1 - j if rev else j

    return blk, nblk, gw, pos


def _ssd_fwd(xbc, dtr, bias, alog, dsk, *, rev, skip, add, name):
    L = xbc.shape[0]
    blk, nblk, gw, pos = _ssd_specs(L, rev)
    nsub = blk // SSD_CHUNK
    n = SSM_STATE

    gps = SSD_FWD_GROUPS_PER_STEP

    def body(*refs):
        x_ref, b_ref, c_ref, dt_ref, bias_ref, alog_ref, dsk_ref = refs[:7]
        add_ref = refs[7] if add is not None else None
        y_ref, st_out, st_scr, dt_scr, ac_scr = refs[-5:]
        consts = _ssd_consts(rev)
        xl = [slice(gi * gw, (gi + 1) * gw) for gi in range(gps)]
        nl = [slice(gi * n, (gi + 1) * n) for gi in range(gps)]

        @pl.when(pl.program_id(1) == 0)
        def _():
            st_scr[...] = jnp.zeros(st_scr.shape, F32)

        dsk_l = []
        for gi in range(gps):
            dt_scr[:, xl[gi]], ac_scr[:, xl[gi]], d = _ssd_prep(dt_ref[gi], bias_ref[gi], alog_ref[gi], dsk_ref[gi],
                                                              rev=rev, consts=consts)
            dsk_l.append(d)

        def step(i, carry):
            sl = pl.ds(pl.multiple_of((nsub - 1 - i if rev else i) * SSD_CHUNK, SSD_CHUNK), SSD_CHUNK)
            ins = [(st_scr[gi], x_ref[sl, xl[gi]], b_ref[sl, nl[gi]], c_ref[sl, nl[gi]], dt_scr[sl, xl[gi]],
                    ac_scr[sl, xl[gi]], dsk_l[gi]) for gi in range(gps)]
            adds = [add_ref[sl, xl[gi]] for gi in range(gps)] if add_ref is not None else None
            outs = [_ssd_chunk(*a, rev=rev, consts=consts, skip=skip) for a in ins]
            for gi in range(gps):
                st_out[gi, i] = ins[gi][0]
                st_scr[gi] = outs[gi][0]
                y_ref[sl, xl[gi]] = outs[gi][1] if adds is None else outs[gi][1] + adds[gi]
            return carry

        lax.fori_loop(0, nsub, step, 0)

    b0 = D_INNER // (n * gps)
    yspec = pl.BlockSpec((blk, gw * gps), lambda g, j: (pos(j), g))
    pspec = pl.BlockSpec((gps, 1, SSM_HPG), lambda g, j: (g, 0, 0))
    in_specs = [yspec,
                pl.BlockSpec((blk, n * gps), lambda g, j: (pos(j), b0 + g)),
                pl.BlockSpec((blk, n * gps), lambda g, j: (pos(j), b0 + SSM_GROUPS // gps + g)),
                pl.BlockSpec((gps, blk, SSM_HPG), lambda g, j: (g, pos(j), 0)),
                pspec, pspec, pspec]
    args = [xbc, xbc, xbc, dtr, bias, alog, dsk]
    if add is not None:
        in_specs.append(yspec)
        args.append(add)
    return pl.pallas_call(
        body, name=name, grid=(SSM_GROUPS // gps, nblk), in_specs=in_specs,
        out_specs=[yspec, pl.BlockSpec((gps, None, nsub, n, gw), lambda g, j: (g, j, 0, 0, 0))],
        out_shape=[jax.ShapeDtypeStruct((L, D_INNER), F32),
                   jax.ShapeDtypeStruct((SSM_GROUPS, nblk, nsub, n, gw), F32)],
        scratch_shapes=[pltpu.VMEM((gps, n, gw), F32), pltpu.VMEM((blk, gw * gps), F32),
                        pltpu.VMEM((blk, gw * gps), F32)],
        compiler_params=_cparams("parallel", "arbitrary"),
    )(*args)


def _ssd_bwd(xbc, dtr, bias, alog, dsk, dy, states, *, rev, skip, adds, name):
    L = xbc.shape[0]
    blk, nblk, gw, _ = _ssd_specs(L, rev)
    nsub = blk // SSD_CHUNK
    n = SSM_STATE
    n_add = 0 if adds is None else 3
    gps = SSD_BWD_GROUPS_PER_STEP

    def body(*refs):
        x_ref, b_ref, c_ref, dt_ref, bias_ref, alog_ref, dsk_ref, dy_ref, st_in = refs[:9]
        add_refs = refs[9:9 + n_add]
        outs = refs[9 + n_add:]
        dx_ref, db_ref, dc_ref, ddt_ref, dbias_ref, dalog_ref, ddsk_ref = outs[:7]
        dst_scr, dt_scr, ac_scr, ddt_scr, dac_scr = outs[7:]
        consts = _ssd_consts(rev)
        chunk = functools.partial(_ssd_chunk, rev=rev, consts=consts, skip=skip)
        prep = functools.partial(_ssd_prep, rev=rev, consts=consts)

        @pl.when(pl.program_id(1) == 0)
        def _():
            dst_scr[...] = jnp.zeros(dst_scr.shape, F32)
            dbias_ref[...] = jnp.zeros(dbias_ref.shape, F32)
            dalog_ref[...] = jnp.zeros(dalog_ref.shape, F32)
            ddsk_ref[...] = jnp.zeros(ddsk_ref.shape, F32)

        xl = [slice(gi * gw, (gi + 1) * gw) for gi in range(gps)]
        nl = [slice(gi * n, (gi + 1) * n) for gi in range(gps)]
        dsk_l, prep_vjp = [], []
        for gi in range(gps):
            narrow = (dt_ref[gi], bias_ref[gi], alog_ref[gi], dsk_ref[gi])
            (dt_scr[:, xl[gi]], ac_scr[:, xl[gi]], d), pv = jax.vjp(prep, *narrow)
            dsk_l.append(d)
            prep_vjp.append(pv)

        def step(k, ddsk_l):
            i = nsub - 1 - k
            sl = pl.ds(pl.multiple_of((nsub - 1 - i if rev else i) * SSD_CHUNK, SSD_CHUNK), SSD_CHUNK)
            ins = [(st_in[gi, i], x_ref[sl, xl[gi]], b_ref[sl, nl[gi]], c_ref[sl, nl[gi]], dt_scr[sl, xl[gi]],
                    ac_scr[sl, xl[gi]], dsk_l[gi]) for gi in range(gps)]
            cts = [(dst_scr[gi], dy_ref[sl, xl[gi]]) for gi in range(gps)]
            more = [(add_refs[0][sl, xl[gi]], add_refs[1][sl, nl[gi]], add_refs[2][sl, nl[gi]])
                    for gi in range(gps)] if n_add else None
            grads = [jax.vjp(chunk, *a)[1](ct) for a, ct in zip(ins, cts)]
            new = []
            for gi in range(gps):
                dst, dx, db, dc, ddt_l, dac, ddsk_k = grads[gi]
                dst_scr[gi] = dst
                dx_ref[sl, xl[gi]] = dx if more is None else dx + more[gi][0]
                db_ref[sl, nl[gi]] = db if more is None else db + more[gi][1]
                dc_ref[sl, nl[gi]] = dc if more is None else dc + more[gi][2]
                ddt_scr[sl, xl[gi]] = ddt_l
                dac_scr[sl, xl[gi]] = dac
                new.append(ddsk_l[gi] + ddsk_k)
            return tuple(new)

        ddsk_l = lax.fori_loop(0, nsub, step, tuple(jnp.zeros((1, gw), F32) for _ in range(gps)))
        for gi in range(gps):
            ddt, dbias, dalog, ddsk = prep_vjp[gi]((ddt_scr[:, xl[gi]], dac_scr[:, xl[gi]], ddsk_l[gi]))
            ddt_ref[gi] = ddt
            dbias_ref[gi] += dbias
            dalog_ref[gi] += dalog
            ddsk_ref[gi] += ddsk

    def pos(j):
        p = nblk - 1 - j
        return nblk - 1 - p if rev else p

    b0 = D_INNER // (n * gps)
    xspec = pl.BlockSpec((blk, gw * gps), lambda g, j: (pos(j), g))
    nspec = pl.BlockSpec((blk, n * gps), lambda g, j: (pos(j), g))
    dtspec = pl.BlockSpec((gps, blk, SSM_HPG), lambda g, j: (g, pos(j), 0))
    pspec = pl.BlockSpec((gps, 1, SSM_HPG), lambda g, j: (g, 0, 0))
    in_specs = [xspec,
                pl.BlockSpec((blk, n * gps), lambda g, j: (pos(j), b0 + g)),
                pl.BlockSpec((blk, n * gps), lambda g, j: (pos(j), b0 + SSM_GROUPS // gps + g)),
                dtspec, pspec, pspec, pspec, xspec,
                pl.BlockSpec((gps, None, nsub, n, gw), lambda g, j: (g, nblk - 1 - j, 0, 0, 0))]
    args = [xbc, xbc, xbc, dtr, bias, alog, dsk, dy, states]
    if adds is not None:
        in_specs += [xspec, nspec, nspec]
        args += list(adds)
    par = jax.ShapeDtypeStruct((SSM_GROUPS, 1, SSM_HPG), F32)
    return pl.pallas_call(
        body, name=name, grid=(SSM_GROUPS // gps, nblk), in_specs=in_specs,
        out_specs=[xspec, nspec, nspec, dtspec, pspec, pspec, pspec],
        out_shape=[jax.ShapeDtypeStruct((L, D_INNER), F32), jax.ShapeDtypeStruct((L, GN), F32),
                   jax.ShapeDtypeStruct((L, GN), F32), jax.ShapeDtypeStruct((SSM_GROUPS, L, SSM_HPG), F32),
                   par, par, par],
        scratch_shapes=[pltpu.VMEM((gps, n, gw), F32)] + [pltpu.VMEM((blk, gw * gps), F32)] * 4,
        compiler_params=_cparams("parallel", "arbitrary"),
    )(*args)


def _out_proj(y, w_out, h, next_norm, name):
    if next_norm is None:
        return _matmul(y, w_out, add=h, name=name), None
    M, K = y.shape
    N = w_out.shape[1]
    tm = 512

    def body(y_ref, w_ref, h_ref, n_ref, o_ref, u_ref):
        acc = jnp.dot(y_ref[...], w_ref[...], preferred_element_type=F32) + h_ref[...]
        o_ref[...] = acc
        u_ref[...] = _rms(acc, n_ref[...]).astype(BF16)

    rows = pl.BlockSpec((tm, N), lambda i: (i, 0))
    return pl.pallas_call(
        body, name=name, grid=(M // tm,),
        in_specs=[pl.BlockSpec((tm, K), lambda i: (i, 0)), pl.BlockSpec((K, N), lambda i: (0, 0)), rows,
                  pl.BlockSpec((1, N), lambda i: (0, 0))],
        out_specs=[rows, rows], out_shape=[jax.ShapeDtypeStruct((M, N), F32), jax.ShapeDtypeStruct((M, N), BF16)],
        compiler_params=_cparams("parallel"),
    )(y, w_out, h, next_norm)


def _norm_fwd(h, w, name):
    d = h.shape[1]
    return _row_kernel(lambda hv, wv: (_rms(hv, wv),), [(h, d, 0)], [w], [(d, BF16)], [], name=name)[0]


def _matmul_norm_bwd(a, b, h, w, dh_in, *, name, add=None, b_cols=None, after=None):
    M, K = a.shape
    N = b.shape[0]
    cb = 0 if b_cols is None else b_cols[0]
    tm = 512 if K <= 4096 else 256

    def body(*refs):
        a_ref, b_ref, h_ref, w_ref, dh_ref = refs[:5]
        o32_ref, o16_ref, dw_ref = refs[-3:]
        du = lax.dot_general(a_ref[...], b_ref[...], NT_DIMS, preferred_element_type=F32)
        if add is not None:
            du = du + refs[5][...]
        _, vjp = jax.vjp(_rms, h_ref[...], w_ref[...])
        dh, dw = vjp(du)
        dh = dh + dh_ref[...]
        o32_ref[...] = dh
        o16_ref[...] = dh.astype(BF16)

        @pl.when(pl.program_id(0) == 0)
        def _():
            dw_ref[...] = jnp.zeros(dw_ref.shape, F32)

        dw_ref[...] += dw

    rows = pl.BlockSpec((tm, N), lambda i: (i, 0))
    in_specs = [pl.BlockSpec((tm, K), lambda i: (i, 0)), pl.BlockSpec((N, K), lambda i: (0, cb)), rows,
                pl.BlockSpec((1, N), lambda i: (0, 0)), rows]
    args = [a, b, h, w, dh_in]
    if add is not None:
        in_specs.append(rows)
        args.append(add)
    if after is not None:
        in_specs.append(pl.BlockSpec(memory_space=pl.ANY))
        args.append(after)
    return pl.pallas_call(
        body, name=name, grid=(M // tm,), in_specs=in_specs,
        out_specs=[rows, rows, pl.BlockSpec((1, N), lambda i: (0, 0))],
        out_shape=[jax.ShapeDtypeStruct((M, N), F32), jax.ShapeDtypeStruct((M, N), BF16),
                   jax.ShapeDtypeStruct((1, N), F32)],
        compiler_params=_cparams("arbitrary"),
    )(*args)


def _ffn_fwd(h, norm_w, w_in, w_out, conv_w, conv_b, tag, u, next_norm=None, loss_with=None):
    pg = _matmul(u, w_in, b_cols=(0, D_FF), name=f"{tag}_in_gate")
    pv = _matmul(u, w_in, b_cols=(1, D_FF), name=f"{tag}_in_val")
    y = _col_kernel(lambda g, v, w, b: (_ffn_mid_fn(g, v, w, b),), [(pg, 0), (pv, 0)], [(conv_w, 0), (conv_b, 0)],
                    [BF16], [], name=f"{tag}_mid", n_tiles=D_FF // COL_TILE)[0]
    if loss_with is not None:
        return _out_proj_loss(y, w_out, h, *loss_with, f"{tag}_out_loss"), (u, pg, pv, y), None
    h_out, u_next = _out_proj(y, w_out, h, next_norm, f"{tag}_out")
    return h_out, (u, pg, pv, y), u_next


def _ffn_bwd(h, dh, dh16, saved, norm_w, w_in, w_out, conv_w, conv_b, tag, after=None):
    u, pg, pv, y = saved
    dy = _matmul(dh16, w_out, nt=True, after=after, name=f"{tag}_out_dx")
    dw_out = _matmul(y, dh16, ta=True, name=f"{tag}_out_dw")

    def fn(g, v, ct, w, b):
        _, vjp = jax.vjp(_ffn_mid_fn, g, v, w, b)
        return vjp(ct)

    dpg, dpv, dcw, dcb = _col_kernel(fn, [(pg, 0), (pv, 0), (dy, 0)], [(conv_w, 0), (conv_b, 0)], [BF16, BF16],
                                     [FFN_CONV, 1], name=f"{tag}_mid_bwd", n_tiles=D_FF // COL_TILE)
    du = _matmul(dpg, w_in, nt=True, b_cols=(0, D_FF), name=f"{tag}_gate_dx")
    dw_gate = _matmul(u, dpg, ta=True, name=f"{tag}_gate_dw")
    dw_val = _matmul(u, dpv, ta=True, name=f"{tag}_val_dw")
    dh, dh16, dnw = _matmul_norm_bwd(dpv, w_in, h, norm_w, dh, add=du, b_cols=(1, D_FF), name=f"{tag}_val_dx_norm")
    return dh, dh16, dict(w_in=(dw_gate, dw_val), w_out=dw_out, conv_w=dcw, conv_b=dcb, norm=dnw)


def _hgrn2_fwd(h, norm_w, w_in, lb_logits, a_norm_w, w_out, after=None, next_norm=None):
    d = D_MODEL
    u = _norm_fwd(h, norm_w, "hg_norm")
    pa = _matmul(u, w_in, after=after, name="hg_in")
    qs, lf_fw, k_fw, lf_bw, k_bw = _row_kernel(
        _hgrn2_pre_fn, [(pa, d, 0), (pa, d, 1), (pa, d, 2)], [lb_logits], [(d, F32)] * 5, [], name="hg_pre")
    o_fw, st_fw = _gla_fwd(qs, k_fw, pa, 3 * HG_HEADS, lf_fw, rev=False, add=None, name="hg_gla_fw")
    o, st_bw = _gla_fwd(qs, k_bw, pa, 3 * HG_HEADS, lf_bw, rev=True, add=o_fw, name="hg_gla_bw")
    y = _row_kernel(lambda ov, gv, wv: (_hgrn2_post_fn(ov, gv, wv),), [(o, d, 0), (pa, d, 4)], [a_norm_w],
                    [(d, BF16)], [], name="hg_post")[0]
    h_out, u_next = _out_proj(y, w_out, h, next_norm, "hg_out")
    return h_out, (u, pa, qs, lf_fw, k_fw, lf_bw, k_bw, st_fw, st_bw, o, y), u_next


def _hgrn2_bwd(h, dh, dh16, saved, norm_w, w_in, lb_logits, a_norm_w, w_out, send, after=None):
    d = D_MODEL
    u, pa, qs, lf_fw, k_fw, lf_bw, k_bw, st_fw, st_bw, o, y = saved
    dy = _matmul(dh16, w_out, nt=True, after=after, name="hg_out_dx")
    dw_out = _matmul(y, dh16, ta=True, name="hg_out_dw")

    def post_bwd(ov, gv, ct, wv):
        _, vjp = jax.vjp(_hgrn2_post_fn, ov, gv, wv)
        return vjp(ct)

    do, dg, dnw = _row_kernel(post_bwd, [(o, d, 0), (pa, d, 4), (dy, d, 0)], [a_norm_w], [(d, F32), (d, F32)],
                              [(1, HG_HEAD_DIM)], name="hg_post_bwd")
    dq1, dk_fw, dv1, dlf_fw = _gla_bwd(qs, k_fw, pa, 3 * HG_HEADS, lf_fw, do, st_fw, rev=False, adds=None,
                                       name="hg_gla_fw_bwd")
    dqs, dk_bw, dv, dlf_bw = _gla_bwd(qs, k_bw, pa, 3 * HG_HEADS, lf_bw, do, st_bw, rev=True, adds=(dq1, dv1),
                                      name="hg_gla_bw_bwd")

    def pre_bwd(qr, fr, br, c0, c1, c2, c3, c4, dvv, dgv, lbl):
        _, vjp = jax.vjp(_hgrn2_pre_fn, qr, fr, br, lbl)
        dq, df, db, dlbl = vjp((c0, c1, c2, c3, c4))
        return jnp.concatenate([dq, df, db, dvv, dgv], axis=1), dlbl

    rows = [(pa, d, 0), (pa, d, 1), (pa, d, 2), (dqs, d, 0), (dlf_fw, d, 0), (dk_fw, d, 0), (dlf_bw, d, 0),
            (dk_bw, d, 0), (dv, d, 0), (dg, d, 0)]
    dpa, dlbl = _row_kernel(pre_bwd, rows, [lb_logits], [(5 * d, BF16)], [lb_logits.shape], name="hg_pre_bwd")
    dw_in = _matmul(u, dpa, ta=True, name="hg_in_dw")
    token = send(dw_in, dw_out)
    dh, dh16, dn1 = _matmul_norm_bwd(dpa, w_in, h, norm_w, dh, after=token, name="hg_in_dx_norm")
    return dh, dh16, dict(lb=dlbl, a_norm=dnw, norm=dn1)


def _group_params(p):
    return p.reshape(SSM_GROUPS, 1, SSM_HPG)


def _mamba_fwd(h, norm_w, w_z, w_xbc, w_dt, conv_w, conv_b, dt_bias, a_log, d_skip, b_norm_w, w_out, u, next_norm):
    L = h.shape[0]
    z = _matmul(u, w_z, name="mb_in_z")
    xbc_raw = _matmul(u, w_xbc, name="mb_in_xbc")
    dt_raw = _matmul(u, w_dt, name="mb_in_dt")
    xbc = _col_kernel(lambda xv, w, b: (_mamba_conv_fn(xv, w, b),), [(xbc_raw, 0)], [(conv_w, 0), (conv_b, 0)],
                      [F32], [], name="mb_conv", n_tiles=CONV_DIM // COL_TILE)[0]
    dtr = dt_raw.reshape(L, 2, SSM_GROUPS, SSM_HPG).transpose(1, 2, 0, 3)
    bias, alog = dt_bias.reshape(2, -1), a_log.reshape(2, -1)
    dsk = _group_params(d_skip.reshape(-1))
    y_fw, st_fw = _ssd_fwd(xbc, dtr[0], _group_params(bias[0]), _group_params(alog[0]), dsk, rev=False, skip=True,
                           add=None, name="mb_ssd_fw")
    ysum, st_bw = _ssd_fwd(xbc, dtr[1], _group_params(bias[1]), _group_params(alog[1]), dsk, rev=True, skip=False,
                           add=y_fw, name="mb_ssd_bw")
    y = _row_kernel(lambda yv, zv, wv: (_mamba_post_fn(yv, zv, wv),), [(ysum, D_INNER, 0), (z, D_INNER, 0)],
                    [b_norm_w], [(D_INNER, BF16)], [], name="mb_post")[0]
    h_out, u_next = _out_proj(y, w_out, h, next_norm, "mb_out")
    return h_out, (u, z, xbc_raw, xbc, dtr, st_fw, st_bw, ysum, y), u_next


def _mamba_bwd(h, dh, dh16, saved, norm_w, w_z, w_xbc, w_dt, conv_w, conv_b, dt_bias, a_log, d_skip, b_norm_w, w_out,
               after=None):
    L = h.shape[0]
    u, z, xbc_raw, xbc, dtr, st_fw, st_bw, ysum, y = saved
    dy = _matmul(dh16, w_out, nt=True, after=after, name="mb_out_dx")
    dw_out = _matmul(y, dh16, ta=True, name="mb_out_dw")

    def post_bwd(yv, zv, ct, wv):
        _, vjp = jax.vjp(_mamba_post_fn, yv, zv, wv)
        return vjp(ct)

    dys, dz, dbn = _row_kernel(post_bwd, [(ysum, D_INNER, 0), (z, D_INNER, 0), (dy, D_INNER, 0)], [b_norm_w],
                               [(D_INNER, F32), (D_INNER, BF16)], [(1, D_INNER)], name="mb_post_bwd")
    bias, alog = dt_bias.reshape(2, -1), a_log.reshape(2, -1)
    dsk = _group_params(d_skip.reshape(-1))
    dx1, db1, dc1, ddt_fw, dbias_fw, dalog_fw, ddsk = _ssd_bwd(
        xbc, dtr[0], _group_params(bias[0]), _group_params(alog[0]), dsk, dys, st_fw, rev=False, skip=True,
        adds=None, name="mb_ssd_fw_bwd")
    dx, db, dc, ddt_bw, dbias_bw, dalog_bw, _ = _ssd_bwd(
        xbc, dtr[1], _group_params(bias[1]), _group_params(alog[1]), dsk, dys, st_bw, rev=True, skip=False,
        adds=(dx1, db1, dc1), name="mb_ssd_bw_bwd")

    def conv_bwd(n_tiles, ct, first):
        def fn(xv, ctv, w, b):
            _, vjp = jax.vjp(_mamba_conv_fn, xv, w, b)
            return vjp(ctv)
        return _col_kernel(fn, [(xbc_raw, first), (ct, 0)], [(conv_w, first), (conv_b, first)], [BF16],
                           [SSM_CONV, 1], name=f"mb_conv_bwd_{first}", n_tiles=n_tiles)

    nx, nb = D_INNER // COL_TILE, GN // COL_TILE
    parts = [conv_bwd(nx, dx, 0), conv_bwd(nb, db, nx), conv_bwd(nb, dc, nx + nb)]
    dxbc = jnp.concatenate([p[0] for p in parts], axis=1)
    dcw = jnp.concatenate([p[1] for p in parts], axis=1)
    dcb = jnp.concatenate([p[2] for p in parts], axis=1)
    ddt = jnp.stack([ddt_fw, ddt_bw]).transpose(2, 0, 1, 3).reshape(L, 2 * SSM_GROUPS * SSM_HPG).astype(BF16)
    du = _matmul(dz, w_z, nt=True, name="mb_z_dx")
    du = _matmul(dxbc, w_xbc, nt=True, add=du, name="mb_xbc_dx")
    dw_in = jnp.concatenate([_matmul(u, dz, ta=True, name="mb_z_dw"), _matmul(u, dxbc, ta=True, name="mb_xbc_dw"),
                             _matmul(u, ddt, ta=True, name="mb_dt_dw")], axis=1)
    dh, dh16, dn1 = _matmul_norm_bwd(ddt, w_dt, h, norm_w, dh, add=du, name="mb_dt_dx_norm")
    grads = dict(w_in=dw_in, w_out=dw_out, conv_w=dcw, conv_b=dcb, b_norm=dbn, norm=dn1,
                 dt_bias=jnp.stack([dbias_fw, dbias_bw]).reshape(1, 2, -1),
                 a_log=jnp.stack([dalog_fw, dalog_bw]).reshape(1, 2, -1), d_skip=ddsk.reshape(1, -1))
    return dh, dh16, grads


def _out_proj_loss(y, w_out, h, target, w, name):
    M, K = y.shape
    N = w_out.shape[1]
    tm = 512

    def body(y_ref, w_ref, h_ref, t_ref, n_ref, dh_ref, dh16_ref, loss_ref, dw_ref):
        out = jnp.dot(y_ref[...], w_ref[...], preferred_element_type=F32) + h_ref[...]
        target = t_ref[...]

        def loss(hv, wv):
            err = _rms(hv, wv) - target
            return 0.5 * jnp.sum(jnp.mean(err * err, axis=-1, keepdims=True), axis=0, keepdims=True)

        val, vjp = jax.vjp(loss, out, n_ref[...])
        dh, dw = vjp(jnp.ones((1, 1), F32))
        dh_ref[...] = dh
        dh16_ref[...] = dh.astype(BF16)

        @pl.when(pl.program_id(0) == 0)
        def _():
            loss_ref[...] = jnp.zeros(loss_ref.shape, F32)
            dw_ref[...] = jnp.zeros(dw_ref.shape, F32)

        loss_ref[...] += val
        dw_ref[...] += dw

    rows = pl.BlockSpec((tm, N), lambda i: (i, 0))
    return pl.pallas_call(
        body, name=name, grid=(M // tm,),
        in_specs=[pl.BlockSpec((tm, K), lambda i: (i, 0)), pl.BlockSpec((K, N), lambda i: (0, 0)), rows, rows,
                  pl.BlockSpec((1, N), lambda i: (0, 0))],
        out_specs=[rows, rows, pl.BlockSpec((1, 1), lambda i: (0, 0)), pl.BlockSpec((1, N), lambda i: (0, 0))],
        out_shape=[jax.ShapeDtypeStruct((M, N), F32), jax.ShapeDtypeStruct((M, N), BF16),
                   jax.ShapeDtypeStruct((1, 1), F32), jax.ShapeDtypeStruct((1, N), F32)],
        compiler_params=_cparams("arbitrary"),
    )(y, w_out, h, target, w)


def _local_step(x, target, rep, comm):
    def ffn_args(i, w, small):
        return (rep["norm2_w"][i:i + 1], w["w_in"], w["w_out"], small["ffn_conv_w"][i], rep["ffn_conv_b"][i:i + 1])

    def mamba_args(w, small):
        w_in = w["w_in"]
        return (rep["norm1_w"][1:2], w_in[:, :D_INNER], w_in[:, D_INNER:D_INNER + CONV_DIM],
                w_in[:, D_INNER + CONV_DIM:], small["b_conv_w"][0], small["b_conv_b"], rep["b_dt_bias"],
                rep["b_a_log"], rep["b_d_skip"], small["b_norm_w"], w["w_out"])

    w_hg = comm.weights("hg", None)
    hg = (rep["norm1_w"][0:1], w_hg["w_in"], rep["a_lb_logits"], rep["a_norm_w"], w_hg["w_out"])
    h0 = x
    h1, s_hg, u1 = _hgrn2_fwd(h0, *hg, after=w_hg.get("token"), next_norm=rep["norm2_w"][0:1])
    w_f0 = comm.weights("f0", h1)
    small = w_f0["small"]
    f0 = ffn_args(0, w_f0, small)
    h2, s_f0, u2 = _ffn_fwd(h1, *f0, "ffn0", u1, next_norm=rep["norm1_w"][1:2])
    mb = mamba_args(comm.weights("mb", h2), small)
    h3, s_mb, u3 = _mamba_fwd(h2, *mb, u2, rep["norm2_w"][1:2])
    f1 = ffn_args(1, comm.weights("f1", h3), small)
    (dh, dh16, loss, d_final), s_f1, _ = _ffn_fwd(h3, *f1, "ffn1", u3,
                                                  loss_with=(target, rep["final_norm_w"].reshape(1, -1)))

    dh, dh16, g_f1 = _ffn_bwd(h3, dh, dh16, s_f1, *f1, "ffn1")
    token = comm.send("f1", dict(w_in=g_f1["w_in"], w_out=g_f1["w_out"]))
    dh, dh16, g_mb = _mamba_bwd(h2, dh, dh16, s_mb, *mb, after=token)
    token = comm.send("mb", dict(w_in=g_mb["w_in"], w_out=g_mb["w_out"]))
    dh, dh16, g_f0 = _ffn_bwd(h1, dh, dh16, s_f0, *f0, "ffn0", after=token)
    token = comm.send("f0", dict(w_in=g_f0["w_in"], w_out=g_f0["w_out"]))
    small_grads = dict(b_conv_w=g_mb["conv_w"][None], b_conv_b=g_mb["conv_b"], b_norm_w=g_mb["b_norm"],
                       ffn_conv_w=jnp.stack([g_f0["conv_w"], g_f1["conv_w"]]))
    dh, dh16, g_hg = _hgrn2_bwd(
        h0, dh, dh16, s_hg, *hg, after=token,
        send=lambda dw_in, dw_out: comm.send("hg", dict(w_in=dw_in, w_out=dw_out, small=small_grads)))
    grads = dict(
        norm1_w=jnp.concatenate([g_hg["norm"], g_mb["norm"]], axis=0),
        norm2_w=jnp.concatenate([g_f0["norm"], g_f1["norm"]], axis=0),
        a_lb_logits=g_hg["lb"], a_norm_w=g_hg["a_norm"], b_dt_bias=g_mb["dt_bias"], b_a_log=g_mb["a_log"],
        b_d_skip=g_mb["d_skip"], ffn_conv_b=jnp.concatenate([g_f0["conv_b"], g_f1["conv_b"]], axis=0),
        final_norm_w=d_final.reshape(-1),
    )
    return loss, dh, grads


def _mesh_pos():
    return lax.axis_index("x"), lax.axis_index("y"), lax.axis_index("c")


N_COPIES = N_DEV - 1


def _comm_call(body, ins, out_shape, name):
    n = len(ins)
    hbm = pl.BlockSpec(memory_space=pl.ANY)
    return pl.pallas_call(
        body, name=name, out_shape=out_shape, in_specs=[hbm] * n, out_specs=[hbm] * n,
        scratch_shapes=[pltpu.SemaphoreType.DMA((n * N_COPIES,)), pltpu.SemaphoreType.DMA((n * N_COPIES,)),
                        pltpu.SemaphoreType.DMA((n,))],
    )(*ins)


def _all_gather(shards, name):
    n = len(shards)

    def body(*refs):
        x_refs, out_refs = refs[:n], refs[n:2 * n]
        send_sems, recv_sems, local_sems = refs[2 * n:]
        x, y, c = _mesh_pos()
        me, sibling = (x, y, c), (x, y, 1 - c)
        chips = [(1 - x, y), (x, 1 - y), (1 - x, 1 - y)]

        def copy(w, k, block, to, own=False):
            px, py, pc = block
            dst = out_refs[w].at[4 * px + 2 * py + pc]
            return pltpu.make_async_remote_copy(
                src_ref=x_refs[w] if own else dst, dst_ref=dst, send_sem=send_sems.at[w * N_COPIES + k],
                recv_sem=recv_sems.at[w * N_COPIES + k], device_id=to, device_id_type=MESH)

        mine = [pltpu.make_async_copy(x_refs[w], out_refs[w].at[4 * x + 2 * y + c], local_sems.at[w]) for w in range(n)]
        for cp in mine:
            cp.start()
        first = [copy(w, 1 + j, me, (*chip, c), own=True) for j, chip in enumerate(chips) for w in range(n)]
        first += [copy(w, 0, me, sibling, own=True) for w in range(n)]
        for cp in first:
            cp.start()
        passed = []
        for j, chip in enumerate(chips):
            for w in range(n):
                copy(w, 1 + j, (*chip, c), me).wait_recv()
                passed.append(copy(w, 4 + j, (*chip, c), sibling))
                passed[-1].start()
        for w in range(n):
            copy(w, 0, sibling, me).wait_recv()
        for j, chip in enumerate(chips):
            for w in range(n):
                copy(w, 4 + j, (*chip, 1 - c), me).wait_recv()
        for cp in first + passed:
            cp.wait_send()
        for cp in mine:
            cp.wait()

    out_shape = [jax.ShapeDtypeStruct((N_DEV, *s.shape), s.dtype) for s in shards]
    return _comm_call(body, shards, out_shape, name)


HBM_SPEC = pl.BlockSpec(memory_space=pltpu.HBM)
SEM_SPEC = pl.BlockSpec(memory_space=pltpu.SEMAPHORE)
EFFECT = pltpu.SideEffectType.DATAFLOW_SIDE_EFFECTING
PEER_ORDER = (4, 2, 6, 5, 3, 7, 1)


def _peer_copies(src_refs, land_refs, send_sems, recv_sems, gather):
    x, y, c = _mesh_pos()
    me = 4 * x + 2 * y + c
    copies = []
    for k in PEER_ORDER:
        px = 1 - x if k & 4 else x
        py = 1 - y if k & 2 else y
        pc = 1 - c if k & 1 else c
        for w, (src, land) in enumerate(zip(src_refs, land_refs)):
            copies.append(pltpu.make_async_remote_copy(
                src_ref=src if gather else src.at[4 * px + 2 * py + pc],
                dst_ref=land.at[me] if gather else land.at[k - 1],
                send_sem=send_sems.at[w * N_COPIES + k - 1], recv_sem=recv_sems.at[w * N_COPIES + k - 1],
                device_id=(px, py, pc), device_id_type=MESH))
    return copies


def _all_gather_small(shard, name):
    def body(x_ref, out_ref, send_sems, recv_sems):
        x, y, c = _mesh_pos()
        out_ref[4 * x + 2 * y + c] = x_ref[...]
        copies = _peer_copies([x_ref], [out_ref], send_sems, recv_sems, True)
        for cp in copies:
            cp.start()
        for cp in copies:
            cp.wait()

    vmem = pl.BlockSpec(memory_space=pltpu.VMEM)
    return pl.pallas_call(
        body, name=name, out_shape=jax.ShapeDtypeStruct((N_DEV, *shard.shape), shard.dtype), in_specs=[vmem],
        out_specs=vmem, scratch_shapes=[pltpu.SemaphoreType.DMA((N_COPIES,)), pltpu.SemaphoreType.DMA((N_COPIES,))],
    )(shard)


def _copies_start(srcs, *, gather, after, name):
    n = len(srcs)
    lands = [lax.empty(((N_DEV,) + s.shape) if gather else ((N_COPIES,) + s.shape[1:]), s.dtype) for s in srcs]

    def body(*refs):
        src_refs, land_refs = refs[:n], refs[n:2 * n]
        send_sems, recv_sems = refs[-2 * n - 3], refs[-2 * n - 2]
        for cp in _peer_copies(src_refs, land_refs, send_sems, recv_sems, gather):
            cp.start()
        refs[-1][...] = jnp.zeros(refs[-1].shape, F32)

    ins = [pltpu.with_memory_space_constraint(a, pltpu.HBM) for a in srcs + lands]
    in_specs = [HBM_SPEC] * (2 * n)
    if after is not None:
        ins.append(after)
        in_specs.append(pl.BlockSpec(memory_space=pl.ANY))
    sems = pltpu.SemaphoreType.DMA((n * N_COPIES,))
    out = pl.pallas_call(
        body, name=name,
        out_shape=(sems, sems, *[pltpu.HBM(a.shape, a.dtype) for a in srcs + lands], jax.ShapeDtypeStruct((8, LANES), F32)),
        in_specs=in_specs,
        out_specs=(SEM_SPEC, SEM_SPEC, *[HBM_SPEC] * (2 * n), pl.BlockSpec(memory_space=pltpu.VMEM)),
        input_output_aliases={i: 2 + i for i in range(2 * n)},
        compiler_params=pltpu.CompilerParams(has_side_effects=EFFECT),
    )(*ins)
    return out[0], out[1], list(out[2:2 + n]), list(out[2 + n:2 + 2 * n]), out[-1]


def _copies_wait(started, *, gather, after, name):
    send_sems, recv_sems, srcs, lands, _ = started
    n = len(srcs)

    def body(*refs):
        src_refs, land_refs = refs[:n], refs[n:2 * n]
        for cp in _peer_copies(src_refs, land_refs, refs[2 * n], refs[2 * n + 1], gather):
            cp.wait_send()
            cp.wait_recv()

    out = pl.pallas_call(
        body, name=name, out_shape=tuple(pltpu.HBM(a.shape, a.dtype) for a in srcs + lands),
        in_specs=[HBM_SPEC] * (2 * n) + [SEM_SPEC, SEM_SPEC, pl.BlockSpec(memory_space=pl.ANY)],
        out_specs=tuple([HBM_SPEC] * (2 * n)), input_output_aliases={i: i for i in range(2 * n)},
        compiler_params=pltpu.CompilerParams(has_side_effects=EFFECT),
    )(*srcs, *lands, send_sems, recv_sems, after)
    return list(out[n:])


def _adamw(got, w, m, v, *, name, tile, own=None, layer=None, into=None):
    rows, width = w.shape[-2:]
    n_got = got.shape[0]
    c1 = 1.0 / (1.0 - ADAM_B1 ** ADAM_STEP)
    c2 = 1.0 / (1.0 - ADAM_B2 ** ADAM_STEP)

    def body(*refs):
        got_ref, w_ref, m_ref, v_ref = refs[:4]
        g_ref, d_ref, nm_ref, nv_ref = refs[-4:]
        g = got_ref[0] if own is None else refs[4][...] + got_ref[0]
        for s in range(1, n_got):
            g = g + got_ref[s]
        m_new = ADAM_B1 * m_ref[...] + (1.0 - ADAM_B1) * g
        v_new = ADAM_B2 * v_ref[...] + (1.0 - ADAM_B2) * (g * g)
        g_ref[...] = g
        nm_ref[...] = m_new
        nv_ref[...] = v_new
        d_ref[...] = -ADAM_LR * ((m_new * c1) / (jnp.sqrt(v_new * c2) + ADAM_EPS) + ADAM_WD * w_ref[...])

    spec = pl.BlockSpec((tile, width), lambda i: (i, 0))
    wspec = spec if layer is None else pl.BlockSpec((None, tile, width), lambda i: (layer, i, 0))
    args = [got, w, m, v] + ([] if own is None else [own])
    in_specs = [pl.BlockSpec((n_got, tile, width), lambda i: (0, i, 0))] + [wspec] * 3 + [spec] * (len(args) - 4)
    aliases = {}
    if into is not None:
        aliases = {len(args) + i: i for i in range(4)}
        args += list(into)
        in_specs += [pl.BlockSpec(memory_space=pl.ANY)] * 4
    return pl.pallas_call(
        body, name=name, grid=(rows // tile,), in_specs=in_specs, out_specs=[wspec] * 4,
        out_shape=[jax.ShapeDtypeStruct(w.shape, F32)] * 4, input_output_aliases=aliases,
        compiler_params=_cparams("parallel"),
    )(*args)


SHARDED = dict(a_w_in=(2, True), a_w_out=(1, True), b_w_in=(2, True), b_w_out=(1, True), ffn_w_in=(2, True),
               ffn_w_out=(1, True), b_conv_w=(2, False), b_conv_b=(1, False), b_norm_w=(1, False), ffn_conv_w=(2, False))
REPLICATED = ("norm1_w", "norm2_w", "a_lb_logits", "a_norm_w", "b_dt_bias", "b_a_log", "b_d_skip", "ffn_conv_b",
              "final_norm_w")
WEIGHTS = ("norm1_w", "norm2_w", "a_w_in", "a_lb_logits", "a_norm_w", "a_w_out", "b_w_in", "b_conv_w", "b_conv_b",
           "b_dt_bias", "b_a_log", "b_d_skip", "b_norm_w", "b_w_out", "ffn_w_in", "ffn_conv_w", "ffn_conv_b",
           "ffn_w_out", "final_norm_w")


def _pad_rows(flat, multiple):
    n = flat.shape[-1]
    per = PACK_W * multiple
    total = -(-n // per) * per
    flat = jnp.pad(flat, [(0, 0)] * (flat.ndim - 1) + [(0, total - n)])
    return flat.reshape(*flat.shape[:-1], total // PACK_W, PACK_W)


def _to_parts(full, axis, n=N_DEV):
    shp = full.shape
    t = full.reshape(*shp[:axis], n, shp[axis] // n, *shp[axis + 1:])
    return jnp.moveaxis(t, axis, 0)


def _from_parts(parts, axis):
    t = jnp.moveaxis(parts, 0, axis)
    shp = t.shape
    return t.reshape(*shp[:axis], shp[axis] * shp[axis + 1], *shp[axis + 2:])


BIG = tuple(n for n, (_, mm) in SHARDED.items() if mm)
SMALL = tuple(n for n, (_, mm) in SHARDED.items() if not mm)
SMALL_W = 512


def _small_rows(tree, lead):
    rows = []
    for n in SMALL:
        t = tree[n]
        t = t.reshape(*lead, -1, t.shape[-1])
        rows.append(jnp.pad(t, [(0, 0)] * (t.ndim - 1) + [(0, SMALL_W - t.shape[-1])]))
    buf = jnp.concatenate(rows, axis=-2)
    return jnp.pad(buf, [(0, 0)] * (buf.ndim - 2) + [(0, 16 - buf.shape[-2]), (0, 0)])


def _small_unrows(buf, like, lead):
    out, r = {}, 0
    for n in SMALL:
        shp = like[n].shape
        k = like[n].size // shp[-1]
        out[n] = buf[..., r:r + k, :shp[-1]].reshape(*lead, *shp)
        r += k
    return out


GROUPS = dict(hg=(("a_w_in", 0), ("a_w_out", 0)), f0=(("ffn_w_in", 0), ("ffn_w_out", 0)),
              mb=(("b_w_in", 0), ("b_w_out", 0)), f1=(("ffn_w_in", 1), ("ffn_w_out", 1)))


class _Comm:
    def __init__(self, local):
        x, y, c = _mesh_pos()
        self.me = 4 * x + 2 * y + c
        self.local = local
        self.shards = {g: [local[n][i].astype(BF16) for n, i in names] for g, names in GROUPS.items()}
        self.shards["f0"].append(_small_rows(local, ()))
        self.first = _all_gather(self.shards["hg"], "gather_hg")
        self.gathers, self.sent, token = {}, {}, None
        for g in ("f0", "mb", "f1"):
            self.gathers[g] = _copies_start(self.shards[g], gather=True, after=token, name=f"gather_{g}_start")
            token = self.gathers[g][-1]
        self.token = token

    def weights(self, group, after):
        if group == "hg":
            got = self.first
        else:
            lands = _copies_wait(self.gathers[group], gather=True, after=after, name=f"gather_{group}_wait")
            got = [lax.dynamic_update_index_in_dim(land, shard, self.me, 0)
                   for land, shard in zip(lands, self.shards[group])]
        out = dict(w_in=_from_parts(got[0], 1), w_out=_from_parts(got[1], 0))
        if group == "hg":
            out["token"] = self.token
        if group == "f0":
            small = _small_unrows(got[2], self.local, (N_DEV,))
            out["small"] = {n: _from_parts(small[n], SHARDED[n][0]) for n in SMALL}
        return out

    def send(self, group, grads):
        w_in = grads["w_in"]
        if isinstance(w_in, tuple):
            half = N_DEV // len(w_in)
            parts_in = jnp.concatenate([_to_parts(t, 1, half) for t in w_in], axis=0)
        else:
            parts_in = _to_parts(w_in, 1)
        parts = [parts_in, _to_parts(grads["w_out"], 0)]
        if "small" in grads:
            parts.append(_small_rows({n: _to_parts(grads["small"][n], SHARDED[n][0]) for n in SMALL}, (N_DEV,)))
        sent = [p.astype(BF16) for p in parts[:2]] + parts[2:] if group == "hg" else parts
        self.sent[group] = (parts, _copies_start(sent, gather=False, after=None, name=f"exchange_{group}_start"))
        return self.sent[group][1][-1]

    def finish(self, after, mom, var):
        res = {}
        for group in ("f1", "mb", "f0", "hg"):
            parts, started = self.sent[group]
            lands = _copies_wait(started, gather=False, after=after, name=f"exchange_{group}_wait")
            own = [lax.dynamic_index_in_dim(p, self.me, 0, keepdims=False) for p in parts]
            for (n, i), got, mine in zip(GROUPS[group], lands, own):
                res[n] = _adamw(got, self.local[n], mom[n], var[n], own=mine, layer=i, into=res.get(n),
                                name=f"adamw_{n}_{i}", tile=_pick(mine.shape[0], (256, 176, 128)))
                after = res[n][0]
        small = _adamw(lands[2], *[_small_rows(t, ()) for t in (self.local, mom, var)], own=own[2],
                       name="adamw_small", tile=16)
        return res, small


def _pack_small(tree, extra):
    flat = jnp.concatenate([tree[n].reshape(-1) for n in REPLICATED] + [extra.reshape(-1)])
    return _pad_rows(flat, 8)


def _unpack_small(pack, like):
    flat, out, off = pack.reshape(-1), {}, 0
    for n in REPLICATED:
        out[n] = flat[off:off + like[n].size].reshape(like[n].shape)
        off += like[n].size
    return out, flat[off]


def kernel(x, norm1_w, norm2_w, a_w_in, a_lb_logits, a_norm_w, a_w_out, b_w_in, b_conv_w, b_conv_b, b_dt_bias, b_a_log, b_d_skip, b_norm_w, b_w_out, ffn_w_in, ffn_conv_w, ffn_conv_b, ffn_w_out, final_norm_w, loss_target, m_norm1_w, m_norm2_w, m_a_w_in, m_a_lb_logits, m_a_norm_w, m_a_w_out, m_b_w_in, m_b_conv_w, m_b_conv_b, m_b_dt_bias, m_b_a_log, m_b_d_skip, m_b_norm_w, m_b_w_out, m_ffn_w_in, m_ffn_conv_w, m_ffn_conv_b, m_ffn_w_out, m_final_norm_w, v_norm1_w, v_norm2_w, v_a_w_in, v_a_lb_logits, v_a_norm_w, v_a_w_out, v_b_w_in, v_b_conv_w, v_b_conv_b, v_b_dt_bias, v_b_a_log, v_b_d_skip, v_b_norm_w, v_b_w_out, v_ffn_w_in, v_ffn_conv_w, v_ffn_conv_b, v_ffn_w_out, v_final_norm_w):
    given = dict(locals())
    local = {n: given[n] for n in WEIGHTS}
    mom = {n: given["m_" + n] for n in WEIGHTS}
    var = {n: given["v_" + n] for n in WEIGHTS}

    comm = _Comm(local)
    loss, grad_x, grads = _local_step(x[0], loss_target[0], local, comm)

    res, small_res = comm.finish(grad_x, mom, var)
    outs = ({}, {}, {}, {})
    for n in BIG:
        for out, r in zip(outs, res[n]):
            out[n] = r
    for out, r in zip(outs, small_res):
        out.update(_small_unrows(r, local, ()))
    out_g, out_d, out_m, out_v = outs

    small = _pack_small(grads, loss)
    rows = small.shape[0]
    got_s = _all_gather_small(small, "gather_small")
    zero = jnp.zeros((1,), F32)
    g, dlt, nm, nv = _adamw(got_s, _pack_small(local, zero), _pack_small(mom, zero), _pack_small(var, zero),
                            name="adamw_replicated", tile=rows)
    (rep_g, total), (rep_d, _), (rep_m, _), (rep_v, _) = (_unpack_small(t, local) for t in (g, dlt, nm, nv))
    out_g.update(rep_g)
    out_d.update(rep_d)
    out_m.update(rep_m)
    out_v.update(rep_v)

    return (total, grad_x[None], *[out_g[n] for n in WEIGHTS], *[out_d[n] for n in WEIGHTS],
            *[out_m[n] for n in WEIGHTS], *[out_v[n] for n in WEIGHTS])
```

```python
import functools

import jax
import jax.numpy as jnp
from jax import lax
from jax.experimental import pallas as pl
from jax.experimental.pallas import tpu as pltpu

F32 = jnp.float32
BF16 = jnp.bfloat16
MESH = pl.DeviceIdType.MESH

N_DEV = 8
EPS = 1e-6
D_MODEL = 1024
HG_HEADS = 8
HG_HEAD_DIM = 128
D_INNER = 2048
SSM_HEAD_DIM = 64
SSM_GROUPS = 8
SSM_HPG = 4
SSM_STATE = 128
GN = SSM_GROUPS * SSM_STATE
CONV_DIM = D_INNER + 2 * GN
D_FF = 2816
SSM_CONV = 5
FFN_CONV = 3

ADAM_LR = 0.001
ADAM_B1 = 0.9
ADAM_B2 = 0.999
ADAM_EPS = 1e-08
ADAM_WD = 0.01
ADAM_STEP = 10

LANES = 128
ROW_TILE = 256
COL_TILE = 128
GLA_CHUNK = 16
GLA_BLOCK = 256
GLA_HEADS_PER_STEP = 8
SSD_CHUNK = 128
SSD_FWD_GROUPS_PER_STEP = 4
SSD_BWD_GROUPS_PER_STEP = 1
SSD_BLOCK = 512
PACK_W = 1024
VMEM_LIMIT = 56 * 1024 * 1024

NT_DIMS = (((1,), (1,)), ((), ()))
TN_DIMS = (((0,), (0,)), ((), ()))


def _cparams(*sem):
    return pltpu.CompilerParams(dimension_semantics=sem, vmem_limit_bytes=VMEM_LIMIT)


def _rms(x, w):
    return x * lax.rsqrt(jnp.mean(x * x, axis=-1, keepdims=True) + EPS) * w


def _row_kernel(body_fn, rows, params, row_outs, acc_outs, *, name, tile=ROW_TILE):
    L = rows[0][0].shape[0]
    tile = min(tile, L)
    n_in = len(rows) + len(params)
    n_ro = len(row_outs)

    def body(*refs):
        outs = body_fn(*[r[...] for r in refs[:n_in]])
        for ref, o in zip(refs[n_in:n_in + n_ro], outs[:n_ro]):
            ref[...] = o.astype(ref.dtype)
        first = pl.program_id(0) == 0
        for ref, o in zip(refs[n_in + n_ro:], outs[n_ro:]):
            @pl.when(first)
            def _(ref=ref):
                ref[...] = jnp.zeros(ref.shape, ref.dtype)
            ref[...] += o

    in_specs = [pl.BlockSpec((tile, w), lambda i, cb=cb: (i, cb)) for _, w, cb in rows]
    in_specs += [pl.BlockSpec(p.shape, lambda i: (0, 0)) for p in params]
    out_specs = [pl.BlockSpec((tile, w), lambda i: (i, 0)) for w, _ in row_outs]
    out_specs += [pl.BlockSpec(s, lambda i: (0, 0)) for s in acc_outs]
    out_shape = [jax.ShapeDtypeStruct((L, w), dt) for w, dt in row_outs]
    out_shape += [jax.ShapeDtypeStruct(s, F32) for s in acc_outs]
    return pl.pallas_call(
        body, name=name, grid=(L // tile,), in_specs=in_specs, out_specs=out_specs, out_shape=out_shape,
        compiler_params=_cparams("arbitrary" if acc_outs else "parallel"),
    )(*[a for a, _, _ in rows], *params)


def _col_kernel(body_fn, cols, params, col_outs, par_outs, *, name, n_tiles):
    L = cols[0][0].shape[0]
    n_in = len(cols) + len(params)
    width = n_tiles * COL_TILE

    def body(*refs):
        outs = body_fn(*[r[...] for r in refs[:n_in]])
        for ref, o in zip(refs[n_in:], outs):
            ref[...] = o.astype(ref.dtype)

    in_specs = [pl.BlockSpec((L, COL_TILE), lambda j, cb=cb: (0, cb(j) if callable(cb) else cb + j)) for _, cb in cols]
    in_specs += [pl.BlockSpec((p.shape[0], COL_TILE), lambda j, cb=cb: (0, cb + j)) for p, cb in params]
    out_specs = [pl.BlockSpec((L, COL_TILE), lambda j: (0, j)) for _ in col_outs]
    out_specs += [pl.BlockSpec((k, COL_TILE), lambda j: (0, j)) for k in par_outs]
    out_shape = [jax.ShapeDtypeStruct((L, width), dt) for dt in col_outs]
    out_shape += [jax.ShapeDtypeStruct((k, width), F32) for k in par_outs]
    return pl.pallas_call(
        body, name=name, grid=(n_tiles,), in_specs=in_specs, out_specs=out_specs, out_shape=out_shape,
        compiler_params=_cparams("parallel"),
    )(*[a for a, _ in cols], *[p for p, _ in params])


def _pick(n, options):
    for t in options:
        if n % t == 0:
            return t
    return n


MATMUL_VMEM = 40 * 1024 * 1024


def _matmul_tiles(M, N, K, out_bytes):
    best = None
    for tm in (1024, 512, 256, 128, M):
        for tn in (1408, 1024, 512, 256, 128, N):
            if M % tm or N % tn:
                continue
            if 2 * (2 * K * (tm + tn) + out_bytes * tm * tn) > MATMUL_VMEM:
                continue
            if best is None or tm * tn > best[0] * best[1]:
                best = (tm, tn)
    return best


def _matmul(a, b, *, name, nt=False, ta=False, add=None, out_dtype=F32, after=None, b_cols=None):
    K, M = a.shape[::-1] if not ta else a.shape
    cb, width = b_cols if b_cols is not None else (0, b.shape[1])
    N = b.shape[0] if nt else width
    assert width == K or not nt
    tm, tn = _matmul_tiles(M, N, K, 4 * (1 + (add is not None)) if out_dtype == F32 else 2 + 4 * (add is not None))
    col0 = cb if nt else cb * (width // tn)

    def body(*refs):
        a_ref, b_ref = refs[0], refs[1]
        o_ref = refs[-1]
        dims = TN_DIMS if ta else NT_DIMS if nt else (((1,), (0,)), ((), ()))
        acc = lax.dot_general(a_ref[...], b_ref[...], dims, preferred_element_type=F32)
        if add is not None:
            acc = acc + refs[2][...]
        o_ref[...] = acc.astype(o_ref.dtype)

    in_specs = [pl.BlockSpec((K, tm), lambda i, j: (0, i)) if ta else pl.BlockSpec((tm, K), lambda i, j: (i, 0)),
                pl.BlockSpec((tn, K), lambda i, j: (j, col0)) if nt
                else pl.BlockSpec((K, tn), lambda i, j: (0, col0 + j))]
    args = [a, b]
    if add is not None:
        in_specs.append(pl.BlockSpec((tm, tn), lambda i, j: (i, j)))
        args.append(add)
    if after is not None:
        in_specs.append(pl.BlockSpec(memory_space=pl.ANY))
        args.append(after)
    return pl.pallas_call(
        body, name=name, grid=(M // tm, N // tn), in_specs=in_specs,
        out_specs=pl.BlockSpec((tm, tn), lambda i, j: (i, j)),
        out_shape=jax.ShapeDtypeStruct((M, N), out_dtype),
        compiler_params=_cparams("parallel", "parallel"),
    )(*args)


def _hgrn2_pre_fn(q_raw, ffw_raw, fbw_raw, lb_logits):
    lb = jax.nn.softmax(lb_logits, axis=0)[0:1]

    def gate(fr):
        f = lb + (1.0 - lb) * jax.nn.sigmoid(fr)
        return jnp.log(f), 1.0 - f

    lf_fw, k_fw = gate(ffw_raw)
    lf_bw, k_bw = gate(fbw_raw)
    return jax.nn.silu(q_raw), lf_fw, k_fw, lf_bw, k_bw


def _hgrn2_post_fn(o, g, norm_w):
    outs = []
    for h in range(HG_HEADS):
        sl = slice(h * HG_HEAD_DIM, (h + 1) * HG_HEAD_DIM)
        outs.append(_rms(o[:, sl], norm_w) * jax.nn.silu(g[:, sl]))
    return jnp.concatenate(outs, axis=1)


def _mamba_post_fn(y, z, norm_w):
    outs = []
    gw = D_INNER // SSM_GROUPS
    for gi in range(SSM_GROUPS):
        sl = slice(gi * gw, (gi + 1) * gw)
        outs.append(_rms(y[:, sl] * jax.nn.silu(z[:, sl]), norm_w[:, sl]))
    return jnp.concatenate(outs, axis=1)


def _shift_rows_impl(x, d):
    if d == 0:
        return x
    n, edge = x.shape[0], 8
    t = lax.broadcasted_iota(jnp.int32, (edge, x.shape[1]), 0)
    rolled = pltpu.roll(x, d % n, 0)
    if d > 0:
        return jnp.concatenate([jnp.where(t >= d, rolled[:edge], 0.0), rolled[edge:]], axis=0)
    return jnp.concatenate([rolled[:n - edge], jnp.where(t < edge + d, rolled[n - edge:], 0.0)], axis=0)


@functools.partial(jax.custom_vjp, nondiff_argnums=(1,))
def _shift_rows(x, d):
    return _shift_rows_impl(x, d)


_shift_rows.defvjp(lambda x, d: (_shift_rows_impl(x, d), None), lambda d, _, g: (_shift_rows_impl(g, -d),))


def _dwconv(x, w, b):
    taps = w.shape[0]
    c = (taps - 1) // 2
    y = b + w[0:1, :] * _shift_rows(x, c)
    for k in range(1, taps):
        y = y + w[k:k + 1, :] * _shift_rows(x, c - k)
    return y


def _ffn_mid_fn(gate, val, w, b):
    return jax.nn.silu(_dwconv(gate, w, b)) * val


def _mamba_conv_fn(xbc, w, b):
    return jax.nn.silu(_dwconv(xbc, w, b))


def _gla_consts(rev):
    return lax.broadcasted_iota(jnp.int32, (GLA_CHUNK, HG_HEAD_DIM), 0)


def _segment_cumsum(x, seg, rev):
    n = x.shape[0]
    t = lax.broadcasted_iota(jnp.int32, x.shape, 0) & (seg - 1)
    s = 1
    while s < seg:
        if rev:
            x = x + jnp.where(t < seg - s, pltpu.roll(x, n - s, 0), 0.0)
        else:
            x = x + jnp.where(t >= s, pltpu.roll(x, s, 0), 0.0)
        s *= 2
    return x


def _segment_cumsum_mxu(x, seg, rev):
    r = lax.broadcasted_iota(jnp.int32, (seg, seg), 0)
    c = lax.broadcasted_iota(jnp.int32, (seg, seg), 1)
    tri = ((r <= c) if rev else (r >= c)).astype(BF16)
    tri3 = jnp.concatenate([tri, tri, tri], axis=1)
    hi = x.astype(BF16)
    rest = x - hi.astype(F32)
    mid = rest.astype(BF16)
    lo = (rest - mid.astype(F32)).astype(BF16)
    outs = []
    for g in range(x.shape[0] // seg):
        rows = slice(g * seg, (g + 1) * seg)
        terms = jnp.concatenate([hi[rows], mid[rows], lo[rows]], axis=0)
        outs.append(jnp.dot(tri3, terms, preferred_element_type=F32))
    return jnp.concatenate(outs, axis=0)


@functools.partial(jax.custom_vjp, nondiff_argnums=(1, 2))
def _segment_cumsum_diff(x, seg, rev):
    return _segment_cumsum_mxu(x, seg, rev)


_segment_cumsum_diff.defvjp(lambda x, seg, rev: (_segment_cumsum_mxu(x, seg, rev), None),
                            lambda seg, rev, _, g: (_segment_cumsum_mxu(g, seg, not rev),))


def _gla_chunk(st, q, k, v, b, *, rev, consts, halves):
    row = consts
    c = q.shape[0]
    half = c // 2 if halves else c
    o = lax.dot_general((q * jnp.exp(b)).astype(BF16), st.astype(BF16), NT_DIMS, preferred_element_type=F32)
    spans = [slice(i, i + half) for i in range(0, c, half)]
    qs, bs, rows, os = [q[p] for p in spans], [b[p] for p in spans], [row[p] for p in spans], [o[p] for p in spans]

    def reach(s, h, masked):
        diff = bs[h] - b[s:s + 1]
        if masked:
            diff = jnp.where((rows[h] <= s) if rev else (rows[h] >= s), diff, -jnp.inf)
        return jnp.sum(qs[h] * (k[s:s + 1] * jnp.exp(diff)), axis=-1, keepdims=True) * v[s:s + 1]

    for s in range(c):
        own = s // half
        os[own] = os[own] + reach(s, own, True)
        if halves and own == (1 if rev else 0):
            os[1 - own] = os[1 - own] + reach(s, 1 - own, False)
    o = jnp.concatenate(os, axis=0) if halves else os[0]
    b_end = b[0:1] if rev else b[c - 1:c]
    kd = (k * jnp.exp(b_end - b)).astype(BF16)
    st_new = st * jnp.exp(b_end) + lax.dot_general(v.astype(BF16), kd, TN_DIMS, preferred_element_type=F32)
    return st_new, o


def _gla_fwd(q, k, v, v_cb, g, *, rev, add, name):
    L = q.shape[0]
    blk = min(GLA_BLOCK, L)
    nblk, nsub = L // blk, blk // GLA_CHUNK
    hd, hps = HG_HEAD_DIM, GLA_HEADS_PER_STEP
    wide = hd * hps

    def body(*refs):
        q_ref, k_ref, v_ref, g_ref = refs[:4]
        add_ref = refs[4] if add is not None else None
        o_ref, st_out, st_scr, b_scr = refs[-4:]
        consts = _gla_consts(rev)

        @pl.when(pl.program_id(1) == 0)
        def _():
            st_scr[...] = jnp.zeros(st_scr.shape, F32)

        b_scr[...] = _segment_cumsum(g_ref[...], GLA_CHUNK, rev)

        def step(i, carry):
            sl = pl.ds(pl.multiple_of((nsub - 1 - i if rev else i) * GLA_CHUNK, GLA_CHUNK), GLA_CHUNK)
            lanes = [slice(hi * hd, (hi + 1) * hd) for hi in range(hps)]
            ins = [(st_scr[hi], q_ref[sl, ln], k_ref[sl, ln], v_ref[sl, ln], b_scr[sl, ln])
                   for hi, ln in enumerate(lanes)]
            adds = [add_ref[sl, ln] for ln in lanes] if add_ref is not None else None
            outs = [_gla_chunk(*args, rev=rev, consts=consts, halves=True) for args in ins]
            for hi, ln in enumerate(lanes):
                st_out[hi, i] = ins[hi][0]
                st_scr[hi] = outs[hi][0]
                o_ref[sl, ln] = outs[hi][1] if adds is None else outs[hi][1] + adds[hi]
            return carry

        lax.fori_loop(0, nsub, step, 0)

    def pos(j):
        return nblk - 1 - j if rev else j

    spec = pl.BlockSpec((blk, wide), lambda h, j: (pos(j), h))
    in_specs = [spec, spec, pl.BlockSpec((blk, wide), lambda h, j: (pos(j), v_cb // hps + h)), spec]
    args = [q, k, v, g]
    if add is not None:
        in_specs.append(spec)
        args.append(add)
    return pl.pallas_call(
        body, name=name, grid=(HG_HEADS // hps, nblk), in_specs=in_specs,
        out_specs=[spec, pl.BlockSpec((hps, None, nsub, hd, hd), lambda h, j: (h, j, 0, 0, 0))],
        out_shape=[jax.ShapeDtypeStruct((L, HG_HEADS * hd), F32),
                   jax.ShapeDtypeStruct((HG_HEADS, nblk, nsub, hd, hd), F32)],
        scratch_shapes=[pltpu.VMEM((hps, hd, hd), F32), pltpu.VMEM((blk, wide), F32)],
        compiler_params=_cparams("parallel", "arbitrary"),
    )(*args)


def _gla_bwd(q, k, v, v_cb, g, do, states, *, rev, adds, name):
    L = q.shape[0]
    blk = min(GLA_BLOCK, L)
    nblk, nsub = L // blk, blk // GLA_CHUNK
    hd, hps = HG_HEAD_DIM, GLA_HEADS_PER_STEP
    wide = hd * hps
    n_add = 0 if adds is None else 2

    def body(*refs):
        q_ref, k_ref, v_ref, g_ref, do_ref, st_in = refs[:6]
        add_refs = refs[6:6 + n_add]
        dq_ref, dk_ref, dv_ref, dg_ref, dst_scr, b_scr, db_scr = refs[6 + n_add:]
        consts = _gla_consts(rev)
        chunk = functools.partial(_gla_chunk, rev=rev, consts=consts, halves=False)

        @pl.when(pl.program_id(1) == 0)
        def _():
            dst_scr[...] = jnp.zeros(dst_scr.shape, F32)

        b_scr[...] = _segment_cumsum(g_ref[...], GLA_CHUNK, rev)

        def rows(i):
            return pl.ds(pl.multiple_of((nsub - 1 - i if rev else i) * GLA_CHUNK, GLA_CHUNK), GLA_CHUNK)

        def step(n, carry):
            i = nsub - 1 - n
            sl = rows(i)
            lanes = [slice(hi * hd, (hi + 1) * hd) for hi in range(hps)]
            ins = [(st_in[hi, i], q_ref[sl, ln], k_ref[sl, ln], v_ref[sl, ln], b_scr[sl, ln])
                   for hi, ln in enumerate(lanes)]
            cts = [(dst_scr[hi], do_ref[sl, ln]) for hi, ln in enumerate(lanes)]
            adds = [(add_refs[0][sl, ln], add_refs[1][sl, ln]) for ln in lanes] if n_add else None
            outs = [jax.vjp(chunk, *args)[1](ct) for args, ct in zip(ins, cts)]
            for hi, ln in enumerate(lanes):
                dst, dq, dk, dv, db = outs[hi]
                dst_scr[hi] = dst
                dq_ref[sl, ln] = dq if adds is None else dq + adds[hi][0]
                dk_ref[sl, ln] = dk
                dv_ref[sl, ln] = dv if adds is None else dv + adds[hi][1]
                db_scr[sl, ln] = db
            return carry

        lax.fori_loop(0, nsub, step, 0)
        dg_ref[...] = _segment_cumsum(db_scr[...], GLA_CHUNK, not rev)

    def pos(j):
        p = nblk - 1 - j
        return nblk - 1 - p if rev else p

    spec = pl.BlockSpec((blk, wide), lambda h, j: (pos(j), h))
    in_specs = [spec, spec, pl.BlockSpec((blk, wide), lambda h, j: (pos(j), v_cb // hps + h)), spec, spec,
                pl.BlockSpec((hps, None, nsub, hd, hd), lambda h, j: (h, nblk - 1 - j, 0, 0, 0))]
    args = [q, k, v, g, do, states]
    if adds is not None:
        in_specs += [spec, spec]
        args += list(adds)
    out = jax.ShapeDtypeStruct((L, HG_HEADS * hd), F32)
    return pl.pallas_call(
        body, name=name, grid=(HG_HEADS // hps, nblk), in_specs=in_specs,
        out_specs=[spec] * 4, out_shape=[out] * 4,
        scratch_shapes=[pltpu.VMEM((hps, hd, hd), F32), pltpu.VMEM((blk, wide), F32), pltpu.VMEM((blk, wide), F32)],
        compiler_params=_cparams("parallel", "arbitrary"),
    )(*args)


def _ssd_consts(rev):
    c = SSD_CHUNK
    gw = SSM_HPG * SSM_HEAD_DIM
    r2 = lax.broadcasted_iota(jnp.int32, (c, c), 0)
    c2 = lax.broadcasted_iota(jnp.int32, (c, c), 1)
    low = (r2 <= c2) if rev else (r2 >= c2)
    lane = lax.broadcasted_iota(jnp.int32, (1, gw), 1)
    return low, r2 == c2, lane, lane >> 6


def _expand_heads(v4, head_of_lane):
    first = head_of_lane[:, :LANES] == 0
    cols = [v4[:, j:j + 1] for j in range(SSM_HPG)]
    return jnp.concatenate([jnp.where(first, cols[0], cols[1]), jnp.where(first, cols[2], cols[3])], axis=1)


def _ssd_prep(dtr, bias, alog, dsk, *, rev, consts):
    head_of_lane = consts[-1]
    dt_l = _expand_heads(jax.nn.softplus(dtr + bias), head_of_lane)
    a_l = _expand_heads(-jnp.exp(alog), head_of_lane)
    return dt_l, _segment_cumsum_diff(dt_l * a_l, SSD_CHUNK, rev), _expand_heads(dsk, head_of_lane)


def _ssd_chunk(st, x, bm, cm, dt_l, acum, dsk_l, *, rev, consts, skip):
    low, eye, lane, head_of_lane = consts
    c = x.shape[0]
    xd = x * dt_l
    cb = lax.dot_general(cm.astype(BF16), bm.astype(BF16), NT_DIMS, preferred_element_type=F32)
    y = jnp.dot(cm.astype(BF16), st.astype(BF16), preferred_element_type=F32) * jnp.exp(acum)
    for j in range(SSM_HPG):
        acol = jnp.sum(jnp.where(lane == j * SSM_HEAD_DIM, acum, 0.0), axis=-1, keepdims=True)
        ab = jnp.broadcast_to(acol, (c, c))
        arow = jnp.sum(jnp.where(eye, ab, 0.0), axis=0, keepdims=True)
        lmat = jnp.exp(jnp.where(low, ab - arow, -jnp.inf))
        xj = jnp.where(head_of_lane == j, xd, 0.0)
        y = y + jnp.dot((cb * lmat).astype(BF16), xj.astype(BF16), preferred_element_type=F32)
    a_end = acum[0:1] if rev else acum[c - 1:c]
    xdec = (xd * jnp.exp(a_end - acum)).astype(BF16)
    st_new = st * jnp.exp(a_end) + lax.dot_general(bm.astype(BF16), xdec, TN_DIMS, preferred_element_type=F32)
    if skip:
        y = y + x * dsk_l
    return st_new, y


def _ssd_specs(L, rev):
    blk = min(SSD_BLOCK, L)
    nblk = L // blk
    gw = SSM_HPG * SSM_HEAD_DIM

    def pos(j):
        return nblk - 1 - j if rev else j

    return blk, nblk, gw, pos


def _ssd_fwd(xbc, dtr, bias, alog, dsk, *, rev, skip, add, name):
    L = xbc.shape[0]
    blk, nblk, gw, pos = _ssd_specs(L, rev)
    nsub = blk // SSD_CHUNK
    n = SSM_STATE

    gps = SSD_FWD_GROUPS_PER_STEP

    def body(*refs):
        x_ref, b_ref, c_ref, dt_ref, bias_ref, alog_ref, dsk_ref = refs[:7]
        add_ref = refs[7] if add is not None else None
        y_ref, st_out, st_scr, dt_scr, ac_scr = refs[-5:]
        consts = _ssd_consts(rev)
        xl = [slice(gi * gw, (gi + 1) * gw) for gi in range(gps)]
        nl = [slice(gi * n, (gi + 1) * n) for gi in range(gps)]

        @pl.when(pl.program_id(1) == 0)
        def _():
            st_scr[...] = jnp.zeros(st_scr.shape, F32)

        dsk_l = []
        for gi in range(gps):
            dt_scr[:, xl[gi]], ac_scr[:, xl[gi]], d = _ssd_prep(dt_ref[gi], bias_ref[gi], alog_ref[gi], dsk_ref[gi],
                                                              rev=rev, consts=consts)
            dsk_l.append(d)

        def step(i, carry):
            sl = pl.ds(pl.multiple_of((nsub - 1 - i if rev else i) * SSD_CHUNK, SSD_CHUNK), SSD_CHUNK)
            ins = [(st_scr[gi], x_ref[sl, xl[gi]], b_ref[sl, nl[gi]], c_ref[sl, nl[gi]], dt_scr[sl, xl[gi]],
                    ac_scr[sl, xl[gi]], dsk_l[gi]) for gi in range(gps)]
            adds = [add_ref[sl, xl[gi]] for gi in range(gps)] if add_ref is not None else None
            outs = [_ssd_chunk(*a, rev=rev, consts=consts, skip=skip) for a in ins]
            for gi in range(gps):
                st_out[gi, i] = ins[gi][0]
                st_scr[gi] = outs[gi][0]
                y_ref[sl, xl[gi]] = outs[gi][1] if adds is None else outs[gi][1] + adds[gi]
            return carry

        lax.fori_loop(0, nsub, step, 0)

    b0 = D_INNER // (n * gps)
    yspec = pl.BlockSpec((blk, gw * gps), lambda g, j: (pos(j), g))
    pspec = pl.BlockSpec((gps, 1, SSM_HPG), lambda g, j: (g, 0, 0))
    in_specs = [yspec,
                pl.BlockSpec((blk, n * gps), lambda g, j: (pos(j), b0 + g)),
                pl.BlockSpec((blk, n * gps), lambda g, j: (pos(j), b0 + SSM_GROUPS // gps + g)),
                pl.BlockSpec((gps, blk, SSM_HPG), lambda g, j: (g, pos(j), 0)),
                pspec, pspec, pspec]
    args = [xbc, xbc, xbc, dtr, bias, alog, dsk]
    if add is not None:
        in_specs.append(yspec)
        args.append(add)
    return pl.pallas_call(
        body, name=name, grid=(SSM_GROUPS // gps, nblk), in_specs=in_specs,
        out_specs=[yspec, pl.BlockSpec((gps, None, nsub, n, gw), lambda g, j: (g, j, 0, 0, 0))],
        out_shape=[jax.ShapeDtypeStruct((L, D_INNER), F32),
                   jax.ShapeDtypeStruct((SSM_GROUPS, nblk, nsub, n, gw), F32)],
        scratch_shapes=[pltpu.VMEM((gps, n, gw), F32), pltpu.VMEM((blk, gw * gps), F32),
                        pltpu.VMEM((blk, gw * gps), F32)],
        compiler_params=_cparams("parallel", "arbitrary"),
    )(*args)


def _ssd_bwd(xbc, dtr, bias, alog, dsk, dy, states, *, rev, skip, adds, name):
    L = xbc.shape[0]
    blk, nblk, gw, _ = _ssd_specs(L, rev)
    nsub = blk // SSD_CHUNK
    n = SSM_STATE
    n_add = 0 if adds is None else 3
    gps = SSD_BWD_GROUPS_PER_STEP

    def body(*refs):
        x_ref, b_ref, c_ref, dt_ref, bias_ref, alog_ref, dsk_ref, dy_ref, st_in = refs[:9]
        add_refs = refs[9:9 + n_add]
        outs = refs[9 + n_add:]
        dx_ref, db_ref, dc_ref, ddt_ref, dbias_ref, dalog_ref, ddsk_ref = outs[:7]
        dst_scr, dt_scr, ac_scr, ddt_scr, dac_scr = outs[7:]
        consts = _ssd_consts(rev)
        chunk = functools.partial(_ssd_chunk, rev=rev, consts=consts, skip=skip)
        prep = functools.partial(_ssd_prep, rev=rev, consts=consts)

        @pl.when(pl.program_id(1) == 0)
        def _():
            dst_scr[...] = jnp.zeros(dst_scr.shape, F32)
            dbias_ref[...] = jnp.zeros(dbias_ref.shape, F32)
            dalog_ref[...] = jnp.zeros(dalog_ref.shape, F32)
            ddsk_ref[...] = jnp.zeros(ddsk_ref.shape, F32)

        xl = [slice(gi * gw, (gi + 1) * gw) for gi in range(gps)]
        nl = [slice(gi * n, (gi + 1) * n) for gi in range(gps)]
        dsk_l, prep_vjp = [], []
        for gi in range(gps):
            narrow = (dt_ref[gi], bias_ref[gi], alog_ref[gi], dsk_ref[gi])
            (dt_scr[:, xl[gi]], ac_scr[:, xl[gi]], d), pv = jax.vjp(prep, *narrow)
            dsk_l.append(d)
            prep_vjp.append(pv)

        def step(k, ddsk_l):
            i = nsub - 1 - k
            sl = pl.ds(pl.multiple_of((nsub - 1 - i if rev else i) * SSD_CHUNK, SSD_CHUNK), SSD_CHUNK)
            ins = [(st_in[gi, i], x_ref[sl, xl[gi]], b_ref[sl, nl[gi]], c_ref[sl, nl[gi]], dt_scr[sl, xl[gi]],
                    ac_scr[sl, xl[gi]], dsk_l[gi]) for gi in range(gps)]
            cts = [(dst_scr[gi], dy_ref[sl, xl[gi]]) for gi in range(gps)]
            more = [(add_refs[0][sl, xl[gi]], add_refs[1][sl, nl[gi]], add_refs[2][sl, nl[gi]])
                    for gi in range(gps)] if n_add else None
            grads = [jax.vjp(chunk, *a)[1](ct) for a, ct in zip(ins, cts)]
            new = []
            for gi in range(gps):
                dst, dx, db, dc, ddt_l, dac, ddsk_k = grads[gi]
                dst_scr[gi] = dst
                dx_ref[sl, xl[gi]] = dx if more is None else dx + more[gi][0]
                db_ref[sl, nl[gi]] = db if more is None else db + more[gi][1]
                dc_ref[sl, nl[gi]] = dc if more is None else dc + more[gi][2]
                ddt_scr[sl, xl[gi]] = ddt_l
                dac_scr[sl, xl[gi]] = dac
                new.append(ddsk_l[gi] + ddsk_k)
            return tuple(new)

        ddsk_l = lax.fori_loop(0, nsub, step, tuple(jnp.zeros((1, gw), F32) for _ in range(gps)))
        for gi in range(gps):
            ddt, dbias, dalog, ddsk = prep_vjp[gi]((ddt_scr[:, xl[gi]], dac_scr[:, xl[gi]], ddsk_l[gi]))
            ddt_ref[gi] = ddt
            dbias_ref[gi] += dbias
            dalog_ref[gi] += dalog
            ddsk_ref[gi] += ddsk

    def pos(j):
        p = nblk - 1 - j
        return nblk - 1 - p if rev else p

    b0 = D_INNER // (n * gps)
    xspec = pl.BlockSpec((blk, gw * gps), lambda g, j: (pos(j), g))
    nspec = pl.BlockSpec((blk, n * gps), lambda g, j: (pos(j), g))
    dtspec = pl.BlockSpec((gps, blk, SSM_HPG), lambda g, j: (g, pos(j), 0))
    pspec = pl.BlockSpec((gps, 1, SSM_HPG), lambda g, j: (g, 0, 0))
    in_specs = [xspec,
                pl.BlockSpec((blk, n * gps), lambda g, j: (pos(j), b0 + g)),
                pl.BlockSpec((blk, n * gps), lambda g, j: (pos(j), b0 + SSM_GROUPS // gps + g)),
                dtspec, pspec, pspec, pspec, xspec,
                pl.BlockSpec((gps, None, nsub, n, gw), lambda g, j: (g, nblk - 1 - j, 0, 0, 0))]
    args = [xbc, xbc, xbc, dtr, bias, alog, dsk, dy, states]
    if adds is not None:
        in_specs += [xspec, nspec, nspec]
        args += list(adds)
    par = jax.ShapeDtypeStruct((SSM_GROUPS, 1, SSM_HPG), F32)
    return pl.pallas_call(
        body, name=name, grid=(SSM_GROUPS // gps, nblk), in_specs=in_specs,
        out_specs=[xspec, nspec, nspec, dtspec, pspec, pspec, pspec],
        out_shape=[jax.ShapeDtypeStruct((L, D_INNER), F32), jax.ShapeDtypeStruct((L, GN), F32),
                   jax.ShapeDtypeStruct((L, GN), F32), jax.ShapeDtypeStruct((SSM_GROUPS, L, SSM_HPG), F32),
                   par, par, par],
        scratch_shapes=[pltpu.VMEM((gps, n, gw), F32)] + [pltpu.VMEM((blk, gw * gps), F32)] * 4,
        compiler_params=_cparams("parallel", "arbitrary"),
    )(*args)


def _out_proj(y, w_out, h, next_norm, name):
    if next_norm is None:
        return _matmul(y, w_out, add=h, name=name), None
    M, K = y.shape
    N = w_out.shape[1]
    tm = 512

    def body(y_ref, w_ref, h_ref, n_ref, o_ref, u_ref):
        acc = jnp.dot(y_ref[...], w_ref[...], preferred_element_type=F32) + h_ref[...]
        o_ref[...] = acc
        u_ref[...] = _rms(acc, n_ref[...]).astype(BF16)

    rows = pl.BlockSpec((tm, N), lambda i: (i, 0))
    return pl.pallas_call(
        body, name=name, grid=(M // tm,),
        in_specs=[pl.BlockSpec((tm, K), lambda i: (i, 0)), pl.BlockSpec((K, N), lambda i: (0, 0)), rows,
                  pl.BlockSpec((1, N), lambda i: (0, 0))],
        out_specs=[rows, rows], out_shape=[jax.ShapeDtypeStruct((M, N), F32), jax.ShapeDtypeStruct((M, N), BF16)],
        compiler_params=_cparams("parallel"),
    )(y, w_out, h, next_norm)


def _norm_fwd(h, w, name):
    d = h.shape[1]
    return _row_kernel(lambda hv, wv: (_rms(hv, wv),), [(h, d, 0)], [w], [(d, BF16)], [], name=name)[0]


def _matmul_norm_bwd(a, b, h, w, dh_in, *, name, add=None, b_cols=None, after=None):
    M, K = a.shape
    N = b.shape[0]
    cb = 0 if b_cols is None else b_cols[0]
    tm = 512 if K <= 4096 else 256

    def body(*refs):
        a_ref, b_ref, h_ref, w_ref, dh_ref = refs[:5]
        o32_ref, o16_ref, dw_ref = refs[-3:]
        du = lax.dot_general(a_ref[...], b_ref[...], NT_DIMS, preferred_element_type=F32)
        if add is not None:
            du = du + refs[5][...]
        _, vjp = jax.vjp(_rms, h_ref[...], w_ref[...])
        dh, dw = vjp(du)
        dh = dh + dh_ref[...]
        o32_ref[...] = dh
        o16_ref[...] = dh.astype(BF16)

        @pl.when(pl.program_id(0) == 0)
        def _():
            dw_ref[...] = jnp.zeros(dw_ref.shape, F32)

        dw_ref[...] += dw

    rows = pl.BlockSpec((tm, N), lambda i: (i, 0))
    in_specs = [pl.BlockSpec((tm, K), lambda i: (i, 0)), pl.BlockSpec((N, K), lambda i: (0, cb)), rows,
                pl.BlockSpec((1, N), lambda i: (0, 0)), rows]
    args = [a, b, h, w, dh_in]
    if add is not None:
        in_specs.append(rows)
        args.append(add)
    if after is not None:
        in_specs.append(pl.BlockSpec(memory_space=pl.ANY))
        args.append(after)
    return pl.pallas_call(
        body, name=name, grid=(M // tm,), in_specs=in_specs,
        out_specs=[rows, rows, pl.BlockSpec((1, N), lambda i: (0, 0))],
        out_shape=[jax.ShapeDtypeStruct((M, N), F32), jax.ShapeDtypeStruct((M, N), BF16),
                   jax.ShapeDtypeStruct((1, N), F32)],
        compiler_params=_cparams("arbitrary"),
    )(*args)


def _ffn_fwd(h, norm_w, w_in, w_out, conv_w, conv_b, tag, u, next_norm=None, loss_with=None):
    pg = _matmul(u, w_in, b_cols=(0, D_FF), name=f"{tag}_in_gate")
    pv = _matmul(u, w_in, b_cols=(1, D_FF), name=f"{tag}_in_val")
    y = _col_kernel(lambda g, v, w, b: (_ffn_mid_fn(g, v, w, b),), [(pg, 0), (pv, 0)], [(conv_w, 0), (conv_b, 0)],
                    [BF16], [], name=f"{tag}_mid", n_tiles=D_FF // COL_TILE)[0]
    if loss_with is not None:
        return _out_proj_loss(y, w_out, h, *loss_with, f"{tag}_out_loss"), (u, pg, pv, y), None
    h_out, u_next = _out_proj(y, w_out, h, next_norm, f"{tag}_out")
    return h_out, (u, pg, pv, y), u_next


def _ffn_bwd(h, dh, dh16, saved, norm_w, w_in, w_out, conv_w, conv_b, tag, after=None):
    u, pg, pv, y = saved
    dy = _matmul(dh16, w_out, nt=True, after=after, name=f"{tag}_out_dx")
    dw_out = _matmul(y, dh16, ta=True, name=f"{tag}_out_dw")

    def fn(g, v, ct, w, b):
        _, vjp = jax.vjp(_ffn_mid_fn, g, v, w, b)
        return vjp(ct)

    dpg, dpv, dcw, dcb = _col_kernel(fn, [(pg, 0), (pv, 0), (dy, 0)], [(conv_w, 0), (conv_b, 0)], [BF16, BF16],
                                     [FFN_CONV, 1], name=f"{tag}_mid_bwd", n_tiles=D_FF // COL_TILE)
    du = _matmul(dpg, w_in, nt=True, b_cols=(0, D_FF), name=f"{tag}_gate_dx")
    dw_gate = _matmul(u, dpg, ta=True, name=f"{tag}_gate_dw")
    dw_val = _matmul(u, dpv, ta=True, name=f"{tag}_val_dw")
    dh, dh16, dnw = _matmul_norm_bwd(dpv, w_in, h, norm_w, dh, add=du, b_cols=(1, D_FF), name=f"{tag}_val_dx_norm")
    return dh, dh16, dict(w_in=(dw_gate, dw_val), w_out=dw_out, conv_w=dcw, conv_b=dcb, norm=dnw)


def _hgrn2_fwd(h, norm_w, w_in, lb_logits, a_norm_w, w_out, after=None, next_norm=None):
    d = D_MODEL
    u = _norm_fwd(h, norm_w, "hg_norm")
    pa = _matmul(u, w_in, after=after, name="hg_in")
    qs, lf_fw, k_fw, lf_bw, k_bw = _row_kernel(
        _hgrn2_pre_fn, [(pa, d, 0), (pa, d, 1), (pa, d, 2)], [lb_logits], [(d, F32)] * 5, [], name="hg_pre")
    o_fw, st_fw = _gla_fwd(qs, k_fw, pa, 3 * HG_HEADS, lf_fw, rev=False, add=None, name="hg_gla_fw")
    o, st_bw = _gla_fwd(qs, k_bw, pa, 3 * HG_HEADS, lf_bw, rev=True, add=o_fw, name="hg_gla_bw")
    y = _row_kernel(lambda ov, gv, wv: (_hgrn2_post_fn(ov, gv, wv),), [(o, d, 0), (pa, d, 4)], [a_norm_w],
                    [(d, BF16)], [], name="hg_post")[0]
    h_out, u_next = _out_proj(y, w_out, h, next_norm, "hg_out")
    return h_out, (u, pa, qs, lf_fw, k_fw, lf_bw, k_bw, st_fw, st_bw, o, y), u_next


def _hgrn2_bwd(h, dh, dh16, saved, norm_w, w_in, lb_logits, a_norm_w, w_out, send, after=None):
    d = D_MODEL
    u, pa, qs, lf_fw, k_fw, lf_bw, k_bw, st_fw, st_bw, o, y = saved
    dy = _matmul(dh16, w_out, nt=True, after=after, name="hg_out_dx")
    dw_out = _matmul(y, dh16, ta=True, name="hg_out_dw")

    def post_bwd(ov, gv, ct, wv):
        _, vjp = jax.vjp(_hgrn2_post_fn, ov, gv, wv)
        return vjp(ct)

    do, dg, dnw = _row_kernel(post_bwd, [(o, d, 0), (pa, d, 4), (dy, d, 0)], [a_norm_w], [(d, F32), (d, F32)],
                              [(1, HG_HEAD_DIM)], name="hg_post_bwd")
    dq1, dk_fw, dv1, dlf_fw = _gla_bwd(qs, k_fw, pa, 3 * HG_HEADS, lf_fw, do, st_fw, rev=False, adds=None,
                                       name="hg_gla_fw_bwd")
    dqs, dk_bw, dv, dlf_bw = _gla_bwd(qs, k_bw, pa, 3 * HG_HEADS, lf_bw, do, st_bw, rev=True, adds=(dq1, dv1),
                                      name="hg_gla_bw_bwd")

    def pre_bwd(qr, fr, br, c0, c1, c2, c3, c4, dvv, dgv, lbl):
        _, vjp = jax.vjp(_hgrn2_pre_fn, qr, fr, br, lbl)
        dq, df, db, dlbl = vjp((c0, c1, c2, c3, c4))
        return jnp.concatenate([dq, df, db, dvv, dgv], axis=1), dlbl

    rows = [(pa, d, 0), (pa, d, 1), (pa, d, 2), (dqs, d, 0), (dlf_fw, d, 0), (dk_fw, d, 0), (dlf_bw, d, 0),
            (dk_bw, d, 0), (dv, d, 0), (dg, d, 0)]
    dpa, dlbl = _row_kernel(pre_bwd, rows, [lb_logits], [(5 * d, BF16)], [lb_logits.shape], name="hg_pre_bwd")
    dw_in = _matmul(u, dpa, ta=True, name="hg_in_dw")
    token = send(dw_in, dw_out)
    dh, dh16, dn1 = _matmul_norm_bwd(dpa, w_in, h, norm_w, dh, after=token, name="hg_in_dx_norm")
    return dh, dh16, dict(lb=dlbl, a_norm=dnw, norm=dn1)


def _group_params(p):
    return p.reshape(SSM_GROUPS, 1, SSM_HPG)


def _mamba_fwd(h, norm_w, w_z, w_xbc, w_dt, conv_w, conv_b, dt_bias, a_log, d_skip, b_norm_w, w_out, u, next_norm):
    L = h.shape[0]
    z = _matmul(u, w_z, name="mb_in_z")
    xbc_raw = _matmul(u, w_xbc, name="mb_in_xbc")
    dt_raw = _matmul(u, w_dt, name="mb_in_dt")
    xbc = _col_kernel(lambda xv, w, b: (_mamba_conv_fn(xv, w, b),), [(xbc_raw, 0)], [(conv_w, 0), (conv_b, 0)],
                      [F32], [], name="mb_conv", n_tiles=CONV_DIM // COL_TILE)[0]
    dtr = dt_raw.reshape(L, 2, SSM_GROUPS, SSM_HPG).transpose(1, 2, 0, 3)
    bias, alog = dt_bias.reshape(2, -1), a_log.reshape(2, -1)
    dsk = _group_params(d_skip.reshape(-1))
    y_fw, st_fw = _ssd_fwd(xbc, dtr[0], _group_params(bias[0]), _group_params(alog[0]), dsk, rev=False, skip=True,
                           add=None, name="mb_ssd_fw")
    ysum, st_bw = _ssd_fwd(xbc, dtr[1], _group_params(bias[1]), _group_params(alog[1]), dsk, rev=True, skip=False,
                           add=y_fw, name="mb_ssd_bw")
    y = _row_kernel(lambda yv, zv, wv: (_mamba_post_fn(yv, zv, wv),), [(ysum, D_INNER, 0), (z, D_INNER, 0)],
                    [b_norm_w], [(D_INNER, BF16)], [], name="mb_post")[0]
    h_out, u_next = _out_proj(y, w_out, h, next_norm, "mb_out")
    return h_out, (u, z, xbc_raw, xbc, dtr, st_fw, st_bw, ysum, y), u_next


def _mamba_bwd(h, dh, dh16, saved, norm_w, w_z, w_xbc, w_dt, conv_w, conv_b, dt_bias, a_log, d_skip, b_norm_w, w_out,
               after=None):
    L = h.shape[0]
    u, z, xbc_raw, xbc, dtr, st_fw, st_bw, ysum, y = saved
    dy = _matmul(dh16, w_out, nt=True, after=after, name="mb_out_dx")
    dw_out = _matmul(y, dh16, ta=True, name="mb_out_dw")

    def post_bwd(yv, zv, ct, wv):
        _, vjp = jax.vjp(_mamba_post_fn, yv, zv, wv)
        return vjp(ct)

    dys, dz, dbn = _row_kernel(post_bwd, [(ysum, D_INNER, 0), (z, D_INNER, 0), (dy, D_INNER, 0)], [b_norm_w],
                               [(D_INNER, F32), (D_INNER, BF16)], [(1, D_INNER)], name="mb_post_bwd")
    bias, alog = dt_bias.reshape(2, -1), a_log.reshape(2, -1)
    dsk = _group_params(d_skip.reshape(-1))
    dx1, db1, dc1, ddt_fw, dbias_fw, dalog_fw, ddsk = _ssd_bwd(
        xbc, dtr[0], _group_params(bias[0]), _group_params(alog[0]), dsk, dys, st_fw, rev=False, skip=True,
        adds=None, name="mb_ssd_fw_bwd")
    dx, db, dc, ddt_bw, dbias_bw, dalog_bw, _ = _ssd_bwd(
        xbc, dtr[1], _group_params(bias[1]), _group_params(alog[1]), dsk, dys, st_bw, rev=True, skip=False,
        adds=(dx1, db1, dc1), name="mb_ssd_bw_bwd")

    nx, nb = D_INNER // COL_TILE, GN // COL_TILE

    def conv_bwd(xv, cx, cbm, ccm, w, b):
        j = pl.program_id(0)
        ct = jnp.where(j < nx, cx, jnp.where(j < nx + nb, cbm, ccm))
        _, vjp = jax.vjp(_mamba_conv_fn, xv, w, b)
        return vjp(ct)

    cols = [(xbc_raw, 0), (dx, lambda j: jnp.minimum(j, nx - 1)), (db, lambda j: jnp.clip(j - nx, 0, nb - 1)),
            (dc, lambda j: jnp.clip(j - nx - nb, 0, nb - 1))]
    dxbc, dcw, dcb = _col_kernel(conv_bwd, cols, [(conv_w, 0), (conv_b, 0)], [BF16], [SSM_CONV, 1],
                                 name="mb_conv_bwd", n_tiles=CONV_DIM // COL_TILE)
    ddt = jnp.stack([ddt_fw, ddt_bw]).transpose(2, 0, 1, 3).reshape(L, 2 * SSM_GROUPS * SSM_HPG).astype(BF16)
    du = _matmul(dz, w_z, nt=True, name="mb_z_dx")
    du = _matmul(dxbc, w_xbc, nt=True, add=du, name="mb_xbc_dx")
    dw_in = jnp.concatenate([_matmul(u, dz, ta=True, name="mb_z_dw"), _matmul(u, dxbc, ta=True, name="mb_xbc_dw"),
                             _matmul(u, ddt, ta=True, name="mb_dt_dw")], axis=1)
    dh, dh16, dn1 = _matmul_norm_bwd(ddt, w_dt, h, norm_w, dh, add=du, name="mb_dt_dx_norm")
    grads = dict(w_in=dw_in, w_out=dw_out, conv_w=dcw, conv_b=dcb, b_norm=dbn, norm=dn1,
                 dt_bias=jnp.stack([dbias_fw, dbias_bw]).reshape(1, 2, -1),
                 a_log=jnp.stack([dalog_fw, dalog_bw]).reshape(1, 2, -1), d_skip=ddsk.reshape(1, -1))
    return dh, dh16, grads


def _out_proj_loss(y, w_out, h, target, w, name):
    M, K = y.shape
    N = w_out.shape[1]
    tm = 512

    def body(y_ref, w_ref, h_ref, t_ref, n_ref, dh_ref, dh16_ref, loss_ref, dw_ref):
        out = jnp.dot(y_ref[...], w_ref[...], preferred_element_type=F32) + h_ref[...]
        target = t_ref[...]

        def loss(hv, wv):
            err = _rms(hv, wv) - target
            return 0.5 * jnp.sum(jnp.mean(err * err, axis=-1, keepdims=True), axis=0, keepdims=True)

        val, vjp = jax.vjp(loss, out, n_ref[...])
        dh, dw = vjp(jnp.ones((1, 1), F32))
        dh_ref[...] = dh
        dh16_ref[...] = dh.astype(BF16)

        @pl.when(pl.program_id(0) == 0)
        def _():
            loss_ref[...] = jnp.zeros(loss_ref.shape, F32)
            dw_ref[...] = jnp.zeros(dw_ref.shape, F32)

        loss_ref[...] += val
        dw_ref[...] += dw

    rows = pl.BlockSpec((tm, N), lambda i: (i, 0))
    return pl.pallas_call(
        body, name=name, grid=(M // tm,),
        in_specs=[pl.BlockSpec((tm, K), lambda i: (i, 0)), pl.BlockSpec((K, N), lambda i: (0, 0)), rows, rows,
                  pl.BlockSpec((1, N), lambda i: (0, 0))],
        out_specs=[rows, rows, pl.BlockSpec((1, 1), lambda i: (0, 0)), pl.BlockSpec((1, N), lambda i: (0, 0))],
        out_shape=[jax.ShapeDtypeStruct((M, N), F32), jax.ShapeDtypeStruct((M, N), BF16),
                   jax.ShapeDtypeStruct((1, 1), F32), jax.ShapeDtypeStruct((1, N), F32)],
        compiler_params=_cparams("arbitrary"),
    )(y, w_out, h, target, w)


def _local_step(x, target, rep, comm):
    def ffn_args(i, w, small):
        return (rep["norm2_w"][i:i + 1], w["w_in"], w["w_out"], small["ffn_conv_w"][i], rep["ffn_conv_b"][i:i + 1])

    def mamba_args(w, small):
        w_in = w["w_in"]
        return (rep["norm1_w"][1:2], w_in[:, :D_INNER], w_in[:, D_INNER:D_INNER + CONV_DIM],
                w_in[:, D_INNER + CONV_DIM:], small["b_conv_w"][0], small["b_conv_b"], rep["b_dt_bias"],
                rep["b_a_log"], rep["b_d_skip"], small["b_norm_w"], w["w_out"])

    w_hg = comm.weights("hg", None)
    hg = (rep["norm1_w"][0:1], w_hg["w_in"], rep["a_lb_logits"], rep["a_norm_w"], w_hg["w_out"])
    h0 = x
    h1, s_hg, u1 = _hgrn2_fwd(h0, *hg, after=w_hg.get("token"), next_norm=rep["norm2_w"][0:1])
    w_f0 = comm.weights("f0", h1)
    small = w_f0["small"]
    f0 = ffn_args(0, w_f0, small)
    h2, s_f0, u2 = _ffn_fwd(h1, *f0, "ffn0", u1, next_norm=rep["norm1_w"][1:2])
    mb = mamba_args(comm.weights("mb", h2), small)
    h3, s_mb, u3 = _mamba_fwd(h2, *mb, u2, rep["norm2_w"][1:2])
    f1 = ffn_args(1, comm.weights("f1", h3), small)
    (dh, dh16, loss, d_final), s_f1, _ = _ffn_fwd(h3, *f1, "ffn1", u3,
                                                  loss_with=(target, rep["final_norm_w"].reshape(1, -1)))

    dh, dh16, g_f1 = _ffn_bwd(h3, dh, dh16, s_f1, *f1, "ffn1")
    token = comm.send("f1", dict(w_in=g_f1["w_in"], w_out=g_f1["w_out"]))
    dh, dh16, g_mb = _mamba_bwd(h2, dh, dh16, s_mb, *mb, after=token)
    token = comm.send("mb", dict(w_in=g_mb["w_in"], w_out=g_mb["w_out"]))
    dh, dh16, g_f0 = _ffn_bwd(h1, dh, dh16, s_f0, *f0, "ffn0", after=token)
    token = comm.send("f0", dict(w_in=g_f0["w_in"], w_out=g_f0["w_out"]))
    small_grads = dict(b_conv_w=g_mb["conv_w"][None], b_conv_b=g_mb["conv_b"], b_norm_w=g_mb["b_norm"],
                       ffn_conv_w=jnp.stack([g_f0["conv_w"], g_f1["conv_w"]]))
    dh, dh16, g_hg = _hgrn2_bwd(
        h0, dh, dh16, s_hg, *hg, after=token,
        send=lambda dw_in, dw_out: comm.send("hg", dict(w_in=dw_in, w_out=dw_out, small=small_grads)))
    grads = dict(
        norm1_w=jnp.concatenate([g_hg["norm"], g_mb["norm"]], axis=0),
        norm2_w=jnp.concatenate([g_f0["norm"], g_f1["norm"]], axis=0),
        a_lb_logits=g_hg["lb"], a_norm_w=g_hg["a_norm"], b_dt_bias=g_mb["dt_bias"], b_a_log=g_mb["a_log"],
        b_d_skip=g_mb["d_skip"], ffn_conv_b=jnp.concatenate([g_f0["conv_b"], g_f1["conv_b"]], axis=0),
        final_norm_w=d_final.reshape(-1),
    )
    return loss, dh, grads


def _mesh_pos():
    return lax.axis_index("x"), lax.axis_index("y"), lax.axis_index("c")


N_COPIES = N_DEV - 1


def _comm_call(body, ins, out_shape, name):
    n = len(ins)
    hbm = pl.BlockSpec(memory_space=pl.ANY)
    return pl.pallas_call(
        body, name=name, out_shape=out_shape, in_specs=[hbm] * n, out_specs=[hbm] * n,
        scratch_shapes=[pltpu.SemaphoreType.DMA((n * N_COPIES,)), pltpu.SemaphoreType.DMA((n * N_COPIES,)),
                        pltpu.SemaphoreType.DMA((n,))],
    )(*ins)


def _all_gather(shards, name):
    n = len(shards)

    def body(*refs):
        x_refs, out_refs = refs[:n], refs[n:2 * n]
        send_sems, recv_sems, local_sems = refs[2 * n:]
        x, y, c = _mesh_pos()
        me, sibling = (x, y, c), (x, y, 1 - c)
        chips = [(1 - x, y), (x, 1 - y), (1 - x, 1 - y)]

        def copy(w, k, block, to, own=False):
            px, py, pc = block
            dst = out_refs[w].at[4 * px + 2 * py + pc]
            return pltpu.make_async_remote_copy(
                src_ref=x_refs[w] if own else dst, dst_ref=dst, send_sem=send_sems.at[w * N_COPIES + k],
                recv_sem=recv_sems.at[w * N_COPIES + k], device_id=to, device_id_type=MESH)

        mine = [pltpu.make_async_copy(x_refs[w], out_refs[w].at[4 * x + 2 * y + c], local_sems.at[w]) for w in range(n)]
        for cp in mine:
            cp.start()
        first = [copy(w, 1 + j, me, (*chip, c), own=True) for j, chip in enumerate(chips) for w in range(n)]
        first += [copy(w, 0, me, sibling, own=True) for w in range(n)]
        for cp in first:
            cp.start()
        passed = []
        for j, chip in enumerate(chips):
            for w in range(n):
                copy(w, 1 + j, (*chip, c), me).wait_recv()
                passed.append(copy(w, 4 + j, (*chip, c), sibling))
                passed[-1].start()
        for w in range(n):
            copy(w, 0, sibling, me).wait_recv()
        for j, chip in enumerate(chips):
            for w in range(n):
                copy(w, 4 + j, (*chip, 1 - c), me).wait_recv()
        for cp in first + passed:
            cp.wait_send()
        for cp in mine:
            cp.wait()

    out_shape = [jax.ShapeDtypeStruct((N_DEV, *s.shape), s.dtype) for s in shards]
    return _comm_call(body, shards, out_shape, name)


HBM_SPEC = pl.BlockSpec(memory_space=pltpu.HBM)
SEM_SPEC = pl.BlockSpec(memory_space=pltpu.SEMAPHORE)
EFFECT = pltpu.SideEffectType.DATAFLOW_SIDE_EFFECTING
PEER_ORDER = (4, 2, 6, 5, 3, 7, 1)


def _peer_copies(src_refs, land_refs, send_sems, recv_sems, gather):
    x, y, c = _mesh_pos()
    me = 4 * x + 2 * y + c
    copies = []
    for k in PEER_ORDER:
        px = 1 - x if k & 4 else x
        py = 1 - y if k & 2 else y
        pc = 1 - c if k & 1 else c
        for w, (src, land) in enumerate(zip(src_refs, land_refs)):
            copies.append(pltpu.make_async_remote_copy(
                src_ref=src if gather else src.at[4 * px + 2 * py + pc],
                dst_ref=land.at[me] if gather else land.at[k - 1],
                send_sem=send_sems.at[w * N_COPIES + k - 1], recv_sem=recv_sems.at[w * N_COPIES + k - 1],
                device_id=(px, py, pc), device_id_type=MESH))
    return copies


def _all_gather_small(shard, name):
    def body(x_ref, out_ref, send_sems, recv_sems):
        x, y, c = _mesh_pos()
        out_ref[4 * x + 2 * y + c] = x_ref[...]
        copies = _peer_copies([x_ref], [out_ref], send_sems, recv_sems, True)
        for cp in copies:
            cp.start()
        for cp in copies:
            cp.wait()

    vmem = pl.BlockSpec(memory_space=pltpu.VMEM)
    return pl.pallas_call(
        body, name=name, out_shape=jax.ShapeDtypeStruct((N_DEV, *shard.shape), shard.dtype), in_specs=[vmem],
        out_specs=vmem, scratch_shapes=[pltpu.SemaphoreType.DMA((N_COPIES,)), pltpu.SemaphoreType.DMA((N_COPIES,))],
    )(shard)


def _copies_start(srcs, *, gather, after, name):
    n = len(srcs)
    lands = [lax.empty(((N_DEV,) + s.shape) if gather else ((N_COPIES,) + s.shape[1:]), s.dtype) for s in srcs]

    def body(*refs):
        src_refs, land_refs = refs[:n], refs[n:2 * n]
        send_sems, recv_sems = refs[-2 * n - 3], refs[-2 * n - 2]
        for cp in _peer_copies(src_refs, land_refs, send_sems, recv_sems, gather):
            cp.start()
        refs[-1][...] = jnp.zeros(refs[-1].shape, F32)

    ins = [pltpu.with_memory_space_constraint(a, pltpu.HBM) for a in srcs + lands]
    in_specs = [HBM_SPEC] * (2 * n)
    if after is not None:
        ins.append(after)
        in_specs.append(pl.BlockSpec(memory_space=pl.ANY))
    sems = pltpu.SemaphoreType.DMA((n * N_COPIES,))
    out = pl.pallas_call(
        body, name=name,
        out_shape=(sems, sems, *[pltpu.HBM(a.shape, a.dtype) for a in srcs + lands], jax.ShapeDtypeStruct((8, LANES), F32)),
        in_specs=in_specs,
        out_specs=(SEM_SPEC, SEM_SPEC, *[HBM_SPEC] * (2 * n), pl.BlockSpec(memory_space=pltpu.VMEM)),
        input_output_aliases={i: 2 + i for i in range(2 * n)},
        compiler_params=pltpu.CompilerParams(has_side_effects=EFFECT),
    )(*ins)
    return out[0], out[1], list(out[2:2 + n]), list(out[2 + n:2 + 2 * n]), out[-1]


def _copies_wait(started, *, gather, after, name):
    send_sems, recv_sems, srcs, lands, _ = started
    n = len(srcs)

    def body(*refs):
        src_refs, land_refs = refs[:n], refs[n:2 * n]
        for cp in _peer_copies(src_refs, land_refs, refs[2 * n], refs[2 * n + 1], gather):
            cp.wait_send()
            cp.wait_recv()

    out = pl.pallas_call(
        body, name=name, out_shape=tuple(pltpu.HBM(a.shape, a.dtype) for a in srcs + lands),
        in_specs=[HBM_SPEC] * (2 * n) + [SEM_SPEC, SEM_SPEC, pl.BlockSpec(memory_space=pl.ANY)],
        out_specs=tuple([HBM_SPEC] * (2 * n)), input_output_aliases={i: i for i in range(2 * n)},
        compiler_params=pltpu.CompilerParams(has_side_effects=EFFECT),
    )(*srcs, *lands, send_sems, recv_sems, after)
    return list(out[n:])


def _adamw(got, w, m, v, *, name, tile, own=None, layer=None, into=None):
    rows, width = w.shape[-2:]
    n_got = got.shape[0]
    c1 = 1.0 / (1.0 - ADAM_B1 ** ADAM_STEP)
    c2 = 1.0 / (1.0 - ADAM_B2 ** ADAM_STEP)

    def body(*refs):
        got_ref, w_ref, m_ref, v_ref = refs[:4]
        g_ref, d_ref, nm_ref, nv_ref = refs[-4:]
        g = got_ref[0] if own is None else refs[4][...] + got_ref[0]
        for s in range(1, n_got):
            g = g + got_ref[s]
        m_new = ADAM_B1 * m_ref[...] + (1.0 - ADAM_B1) * g
        v_new = ADAM_B2 * v_ref[...] + (1.0 - ADAM_B2) * (g * g)
        g_ref[...] = g
        nm_ref[...] = m_new
        nv_ref[...] = v_new
        d_ref[...] = -ADAM_LR * ((m_new * c1) / (jnp.sqrt(v_new * c2) + ADAM_EPS) + ADAM_WD * w_ref[...])

    spec = pl.BlockSpec((tile, width), lambda i: (i, 0))
    wspec = spec if layer is None else pl.BlockSpec((None, tile, width), lambda i: (layer, i, 0))
    args = [got, w, m, v] + ([] if own is None else [own])
    in_specs = [pl.BlockSpec((n_got, tile, width), lambda i: (0, i, 0))] + [wspec] * 3 + [spec] * (len(args) - 4)
    aliases = {}
    if into is not None:
        aliases = {len(args) + i: i for i in range(4)}
        args += list(into)
        in_specs += [pl.BlockSpec(memory_space=pl.ANY)] * 4
    return pl.pallas_call(
        body, name=name, grid=(rows // tile,), in_specs=in_specs, out_specs=[wspec] * 4,
        out_shape=[jax.ShapeDtypeStruct(w.shape, F32)] * 4, input_output_aliases=aliases,
        compiler_params=_cparams("parallel"),
    )(*args)


SHARDED = dict(a_w_in=(2, True), a_w_out=(1, True), b_w_in=(2, True), b_w_out=(1, True), ffn_w_in=(2, True),
               ffn_w_out=(1, True), b_conv_w=(2, False), b_conv_b=(1, False), b_norm_w=(1, False), ffn_conv_w=(2, False))
REPLICATED = ("norm1_w", "norm2_w", "a_lb_logits", "a_norm_w", "b_dt_bias", "b_a_log", "b_d_skip", "ffn_conv_b",
              "final_norm_w")
WEIGHTS = ("norm1_w", "norm2_w", "a_w_in", "a_lb_logits", "a_norm_w", "a_w_out", "b_w_in", "b_conv_w", "b_conv_b",
           "b_dt_bias", "b_a_log", "b_d_skip", "b_norm_w", "b_w_out", "ffn_w_in", "ffn_conv_w", "ffn_conv_b",
           "ffn_w_out", "final_norm_w")


def _pad_rows(flat, multiple):
    n = flat.shape[-1]
    per = PACK_W * multiple
    total = -(-n // per) * per
    flat = jnp.pad(flat, [(0, 0)] * (flat.ndim - 1) + [(0, total - n)])
    return flat.reshape(*flat.shape[:-1], total // PACK_W, PACK_W)


def _to_parts(full, axis, n=N_DEV):
    shp = full.shape
    t = full.reshape(*shp[:axis], n, shp[axis] // n, *shp[axis + 1:])
    return jnp.moveaxis(t, axis, 0)


def _from_parts(parts, axis):
    t = jnp.moveaxis(parts, 0, axis)
    shp = t.shape
    return t.reshape(*shp[:axis], shp[axis] * shp[axis + 1], *shp[axis + 2:])


BIG = tuple(n for n, (_, mm) in SHARDED.items() if mm)
SMALL = tuple(n for n, (_, mm) in SHARDED.items() if not mm)
SMALL_W = 512


def _small_rows(tree, lead):
    rows = []
    for n in SMALL:
        t = tree[n]
        t = t.reshape(*lead, -1, t.shape[-1])
        rows.append(jnp.pad(t, [(0, 0)] * (t.ndim - 1) + [(0, SMALL_W - t.shape[-1])]))
    buf = jnp.concatenate(rows, axis=-2)
    return jnp.pad(buf, [(0, 0)] * (buf.ndim - 2) + [(0, 16 - buf.shape[-2]), (0, 0)])


def _small_unrows(buf, like, lead):
    out, r = {}, 0
    for n in SMALL:
        shp = like[n].shape
        k = like[n].size // shp[-1]
        out[n] = buf[..., r:r + k, :shp[-1]].reshape(*lead, *shp)
        r += k
    return out


GROUPS = dict(hg=(("a_w_in", 0), ("a_w_out", 0)), f0=(("ffn_w_in", 0), ("ffn_w_out", 0)),
              mb=(("b_w_in", 0), ("b_w_out", 0)), f1=(("ffn_w_in", 1), ("ffn_w_out", 1)))


class _Comm:
    def __init__(self, local):
        x, y, c = _mesh_pos()
        self.me = 4 * x + 2 * y + c
        self.local = local
        self.shards = {g: [local[n][i].astype(BF16) for n, i in names] for g, names in GROUPS.items()}
        self.shards["f0"].append(_small_rows(local, ()))
        self.first = _all_gather(self.shards["hg"], "gather_hg")
        self.gathers, self.sent, token = {}, {}, None
        for g in ("f0", "mb", "f1"):
            self.gathers[g] = _copies_start(self.shards[g], gather=True, after=token, name=f"gather_{g}_start")
            token = self.gathers[g][-1]
        self.token = token

    def weights(self, group, after):
        if group == "hg":
            got = self.first
        else:
            lands = _copies_wait(self.gathers[group], gather=True, after=after, name=f"gather_{group}_wait")
            got = [lax.dynamic_update_index_in_dim(land, shard, self.me, 0)
                   for land, shard in zip(lands, self.shards[group])]
        out = dict(w_in=_from_parts(got[0], 1), w_out=_from_parts(got[1], 0))
        if group == "hg":
            out["token"] = self.token
        if group == "f0":
            small = _small_unrows(got[2], self.local, (N_DEV,))
            out["small"] = {n: _from_parts(small[n], SHARDED[n][0]) for n in SMALL}
        return out

    def send(self, group, grads):
        w_in = grads["w_in"]
        if isinstance(w_in, tuple):
            half = N_DEV // len(w_in)
            parts_in = jnp.concatenate([_to_parts(t, 1, half) for t in w_in], axis=0)
        else:
            parts_in = _to_parts(w_in, 1)
        parts = [parts_in, _to_parts(grads["w_out"], 0)]
        if "small" in grads:
            parts.append(_small_rows({n: _to_parts(grads["small"][n], SHARDED[n][0]) for n in SMALL}, (N_DEV,)))
        sent = [p.astype(BF16) for p in parts[:2]] + parts[2:] if group == "hg" else parts
        self.sent[group] = (parts, _copies_start(sent, gather=False, after=None, name=f"exchange_{group}_start"))
        return self.sent[group][1][-1]

    def finish(self, after, mom, var):
        res = {}
        for group in ("f1", "mb", "f0", "hg"):
            parts, started = self.sent[group]
            lands = _copies_wait(started, gather=False, after=after, name=f"exchange_{group}_wait")
            own = [lax.dynamic_index_in_dim(p, self.me, 0, keepdims=False) for p in parts]
            for (n, i), got, mine in zip(GROUPS[group], lands, own):
                res[n] = _adamw(got, self.local[n], mom[n], var[n], own=mine, layer=i, into=res.get(n),
                                name=f"adamw_{n}_{i}", tile=_pick(mine.shape[0], (256, 176, 128)))
                after = res[n][0]
        small = _adamw(lands[2], *[_small_rows(t, ()) for t in (self.local, mom, var)], own=own[2],
                       name="adamw_small", tile=16)
        return res, small


def _pack_small(tree, extra):
    flat = jnp.concatenate([tree[n].reshape(-1) for n in REPLICATED] + [extra.reshape(-1)])
    return _pad_rows(flat, 8)


def _unpack_small(pack, like):
    flat, out, off = pack.reshape(-1), {}, 0
    for n in REPLICATED:
        out[n] = flat[off:off + like[n].size].reshape(like[n].shape)
        off += like[n].size
    return out, flat[off]


def kernel(x, norm1_w, norm2_w, a_w_in, a_lb_logits, a_norm_w, a_w_out, b_w_in, b_conv_w, b_conv_b, b_dt_bias, b_a_log, b_d_skip, b_norm_w, b_w_out, ffn_w_in, ffn_conv_w, ffn_conv_b, ffn_w_out, final_norm_w, loss_target, m_norm1_w, m_norm2_w, m_a_w_in, m_a_lb_logits, m_a_norm_w, m_a_w_out, m_b_w_in, m_b_conv_w, m_b_conv_b, m_b_dt_bias, m_b_a_log, m_b_d_skip, m_b_norm_w, m_b_w_out, m_ffn_w_in, m_ffn_conv_w, m_ffn_conv_b, m_ffn_w_out, m_final_norm_w, v_norm1_w, v_norm2_w, v_a_w_in, v_a_lb_logits, v_a_norm_w, v_a_w_out, v_b_w_in, v_b_conv_w, v_b_conv_b, v_b_dt_bias, v_b_a_log, v_b_d_skip, v_b_norm_w, v_b_w_out, v_ffn_w_in, v_ffn_conv_w, v_ffn_conv_b, v_ffn_w_out, v_final_norm_w):
    given = dict(locals())
    local = {n: given[n] for n in WEIGHTS}
    mom = {n: given["m_" + n] for n in WEIGHTS}
    var = {n: given["v_" + n] for n in WEIGHTS}

    comm = _Comm(local)
    loss, grad_x, grads = _local_step(x[0], loss_target[0], local, comm)

    res, small_res = comm.finish(grad_x, mom, var)
    outs = ({}, {}, {}, {})
    for n in BIG:
        for out, r in zip(outs, res[n]):
            out[n] = r
    for out, r in zip(outs, small_res):
        out.update(_small_unrows(r, local, ()))
    out_g, out_d, out_m, out_v = outs

    small = _pack_small(grads, loss)
    rows = small.shape[0]
    got_s = _all_gather_small(small, "gather_small")
    zero = jnp.zeros((1,), F32)
    g, dlt, nm, nv = _adamw(got_s, _pack_small(local, zero), _pack_small(mom, zero), _pack_small(var, zero),
                            name="adamw_replicated", tile=rows)
    (rep_g, total), (rep_d, _), (rep_m, _), (rep_v, _) = (_unpack_small(t, local) for t in (g, dlt, nm, nv))
    out_g.update(rep_g)
    out_d.update(rep_d)
    out_m.update(rep_m)
    out_v.update(rep_v)

    return (total, grad_x[None], *[out_g[n] for n in WEIGHTS], *[out_d[n] for n in WEIGHTS],
            *[out_m[n] for n in WEIGHTS], *[out_v[n] for n in WEIGHTS])
```

```python
import functools

import jax
import jax.numpy as jnp
from jax import lax
from jax.experimental import pallas as pl
from jax.experimental.pallas import tpu as pltpu

F32 = jnp.float32
BF16 = jnp.bfloat16
MESH = pl.DeviceIdType.MESH

N_DEV = 8
EPS = 1e-6
D_MODEL = 1024
HG_HEADS = 8
HG_HEAD_DIM = 128
D_INNER = 2048
SSM_HEAD_DIM = 64
SSM_GROUPS = 8
SSM_HPG = 4
SSM_STATE = 128
GN = SSM_GROUPS * SSM_STATE
CONV_DIM = D_INNER + 2 * GN
D_FF = 2816
SSM_CONV = 5
FFN_CONV = 3

ADAM_LR = 0.001
ADAM_B1 = 0.9
ADAM_B2 = 0.999
ADAM_EPS = 1e-08
ADAM_WD = 0.01
ADAM_STEP = 10

LANES = 128
ROW_TILE = 256
COL_TILE = 128
GLA_CHUNK = 16
GLA_BLOCK = 256
GLA_HEADS_PER_STEP = 8
SSD_CHUNK = 128
SSD_FWD_GROUPS_PER_STEP = 4
SSD_BWD_GROUPS_PER_STEP = 1
SSD_BLOCK = 512
PACK_W = 1024
VMEM_LIMIT = 56 * 1024 * 1024

NT_DIMS = (((1,), (1,)), ((), ()))
TN_DIMS = (((0,), (0,)), ((), ()))


def _cparams(*sem):
    return pltpu.CompilerParams(dimension_semantics=sem, vmem_limit_bytes=VMEM_LIMIT)


def _rms(x, w):
    return x * lax.rsqrt(jnp.mean(x * x, axis=-1, keepdims=True) + EPS) * w


def _row_kernel(body_fn, rows, params, row_outs, acc_outs, *, name, tile=ROW_TILE):
    L = rows[0][0].shape[0]
    tile = min(tile, L)
    n_in = len(rows) + len(params)
    n_ro = len(row_outs)

    def body(*refs):
        outs = body_fn(*[r[...] for r in refs[:n_in]])
        for ref, o in zip(refs[n_in:n_in + n_ro], outs[:n_ro]):
            ref[...] = o.astype(ref.dtype)
        first = pl.program_id(0) == 0
        for ref, o in zip(refs[n_in + n_ro:], outs[n_ro:]):
            @pl.when(first)
            def _(ref=ref):
                ref[...] = jnp.zeros(ref.shape, ref.dtype)
            ref[...] += o

    in_specs = [pl.BlockSpec((tile, w), lambda i, cb=cb: (i, cb)) for _, w, cb in rows]
    in_specs += [pl.BlockSpec(p.shape, lambda i: (0, 0)) for p in params]
    out_specs = [pl.BlockSpec((tile, w), lambda i: (i, 0)) for w, _ in row_outs]
    out_specs += [pl.BlockSpec(s, lambda i: (0, 0)) for s in acc_outs]
    out_shape = [jax.ShapeDtypeStruct((L, w), dt) for w, dt in row_outs]
    out_shape += [jax.ShapeDtypeStruct(s, F32) for s in acc_outs]
    return pl.pallas_call(
        body, name=name, grid=(L // tile,), in_specs=in_specs, out_specs=out_specs, out_shape=out_shape,
        compiler_params=_cparams("arbitrary" if acc_outs else "parallel"),
    )(*[a for a, _, _ in rows], *params)


def _col_kernel(body_fn, cols, params, col_outs, par_outs, *, name, n_tiles):
    L = cols[0][0].shape[0]
    n_in = len(cols) + len(params)
    width = n_tiles * COL_TILE

    def body(*refs):
        outs = body_fn(*[r[...] for r in refs[:n_in]])
        for ref, o in zip(refs[n_in:], outs):
            ref[...] = o.astype(ref.dtype)

    in_specs = [pl.BlockSpec((L, COL_TILE), lambda j, cb=cb: (0, cb(j) if callable(cb) else cb + j)) for _, cb in cols]
    in_specs += [pl.BlockSpec((p.shape[0], COL_TILE), lambda j, cb=cb: (0, cb + j)) for p, cb in params]
    out_specs = [pl.BlockSpec((L, COL_TILE), lambda j: (0, j)) for _ in col_outs]
    out_specs += [pl.BlockSpec((k, COL_TILE), lambda j: (0, j)) for k in par_outs]
    out_shape = [jax.ShapeDtypeStruct((L, width), dt) for dt in col_outs]
    out_shape += [jax.ShapeDtypeStruct((k, width), F32) for k in par_outs]
    return pl.pallas_call(
        body, name=name, grid=(n_tiles,), in_specs=in_specs, out_specs=out_specs, out_shape=out_shape,
        compiler_params=_cparams("parallel"),
    )(*[a for a, _ in cols], *[p for p, _ in params])


def _pick(n, options):
    for t in options:
        if n % t == 0:
            return t
    return n


MATMUL_VMEM = 40 * 1024 * 1024


def _matmul_tiles(M, N, K, out_bytes):
    best = None
    for tm in (1024, 512, 256, 128, M):
        for tn in (1408, 1024, 512, 256, 128, N):
            if M % tm or N % tn:
                continue
            if 2 * (2 * K * (tm + tn) + out_bytes * tm * tn) > MATMUL_VMEM:
                continue
            if best is None or tm * tn > best[0] * best[1]:
                best = (tm, tn)
    return best


def _matmul(a, b, *, name, nt=False, ta=False, add=None, out_dtype=F32, after=None, b_cols=None, col_parts=None):
    K, M = a.shape[::-1] if not ta else a.shape
    cb, width = b_cols if b_cols is not None else (0, b.shape[1])
    N = b.shape[0] if nt else width
    assert width == K or not nt
    tm, tn = _matmul_tiles(M, N, K, 4 * (1 + (add is not None)) if out_dtype == F32 else 2 + 4 * (add is not None))
    if col_parts is not None:
        tn = N // col_parts
        assert tn % LANES == 0 and add is None
    col0 = cb if nt else cb * (width // tn)

    def body(*refs):
        a_ref, b_ref = refs[0], refs[1]
        o_ref = refs[-1]
        dims = TN_DIMS if ta else NT_DIMS if nt else (((1,), (0,)), ((), ()))
        acc = lax.dot_general(a_ref[...], b_ref[...], dims, preferred_element_type=F32)
        if add is not None:
            acc = acc + refs[2][...]
        o_ref[...] = acc.astype(o_ref.dtype)

    in_specs = [pl.BlockSpec((K, tm), lambda i, j: (0, i)) if ta else pl.BlockSpec((tm, K), lambda i, j: (i, 0)),
                pl.BlockSpec((tn, K), lambda i, j: (j, col0)) if nt
                else pl.BlockSpec((K, tn), lambda i, j: (0, col0 + j))]
    args = [a, b]
    if add is not None:
        in_specs.append(pl.BlockSpec((tm, tn), lambda i, j: (i, j)))
        args.append(add)
    if after is not None:
        in_specs.append(pl.BlockSpec(memory_space=pl.ANY))
        args.append(after)
    if col_parts is None:
        out_spec, out_shape = pl.BlockSpec((tm, tn), lambda i, j: (i, j)), (M, N)
    else:
        out_spec, out_shape = pl.BlockSpec((None, tm, tn), lambda i, j: (j, i, 0)), (col_parts, M, tn)
    return pl.pallas_call(
        body, name=name, grid=(M // tm, N // tn), in_specs=in_specs, out_specs=out_spec,
        out_shape=jax.ShapeDtypeStruct(out_shape, out_dtype), compiler_params=_cparams("parallel", "parallel"),
    )(*args)


def _hgrn2_pre_fn(q_raw, ffw_raw, fbw_raw, lb_logits):
    lb = jax.nn.softmax(lb_logits, axis=0)[0:1]

    def gate(fr):
        f = lb + (1.0 - lb) * jax.nn.sigmoid(fr)
        return jnp.log(f), 1.0 - f

    lf_fw, k_fw = gate(ffw_raw)
    lf_bw, k_bw = gate(fbw_raw)
    return jax.nn.silu(q_raw), lf_fw, k_fw, lf_bw, k_bw


def _hgrn2_post_fn(o, g, norm_w):
    outs = []
    for h in range(HG_HEADS):
        sl = slice(h * HG_HEAD_DIM, (h + 1) * HG_HEAD_DIM)
        outs.append(_rms(o[:, sl], norm_w) * jax.nn.silu(g[:, sl]))
    return jnp.concatenate(outs, axis=1)


def _mamba_post_fn(y, z, norm_w):
    outs = []
    gw = D_INNER // SSM_GROUPS
    for gi in range(SSM_GROUPS):
        sl = slice(gi * gw, (gi + 1) * gw)
        outs.append(_rms(y[:, sl] * jax.nn.silu(z[:, sl]), norm_w[:, sl]))
    return jnp.concatenate(outs, axis=1)


def _shift_rows_impl(x, d):
    if d == 0:
        return x
    n, edge = x.shape[0], 8
    t = lax.broadcasted_iota(jnp.int32, (edge, x.shape[1]), 0)
    rolled = pltpu.roll(x, d % n, 0)
    if d > 0:
        return jnp.concatenate([jnp.where(t >= d, rolled[:edge], 0.0), rolled[edge:]], axis=0)
    return jnp.concatenate([rolled[:n - edge], jnp.where(t < edge + d, rolled[n - edge:], 0.0)], axis=0)


@functools.partial(jax.custom_vjp, nondiff_argnums=(1,))
def _shift_rows(x, d):
    return _shift_rows_impl(x, d)


_shift_rows.defvjp(lambda x, d: (_shift_rows_impl(x, d), None), lambda d, _, g: (_shift_rows_impl(g, -d),))


def _dwconv(x, w, b):
    taps = w.shape[0]
    c = (taps - 1) // 2
    y = b + w[0:1, :] * _shift_rows(x, c)
    for k in range(1, taps):
        y = y + w[k:k + 1, :] * _shift_rows(x, c - k)
    return y


def _ffn_mid_fn(gate, val, w, b):
    return jax.nn.silu(_dwconv(gate, w, b)) * val


def _mamba_conv_fn(xbc, w, b):
    return jax.nn.silu(_dwconv(xbc, w, b))


def _gla_consts(rev):
    return lax.broadcasted_iota(jnp.int32, (GLA_CHUNK, HG_HEAD_DIM), 0)


def _segment_cumsum(x, seg, rev):
    n = x.shape[0]
    t = lax.broadcasted_iota(jnp.int32, x.shape, 0) & (seg - 1)
    s = 1
    while s < seg:
        if rev:
            x = x + jnp.where(t < seg - s, pltpu.roll(x, n - s, 0), 0.0)
        else:
            x = x + jnp.where(t >= s, pltpu.roll(x, s, 0), 0.0)
        s *= 2
    return x


def _segment_cumsum_mxu(x, seg, rev):
    r = lax.broadcasted_iota(jnp.int32, (seg, seg), 0)
    c = lax.broadcasted_iota(jnp.int32, (seg, seg), 1)
    tri = ((r <= c) if rev else (r >= c)).astype(BF16)
    tri3 = jnp.concatenate([tri, tri, tri], axis=1)
    hi = x.astype(BF16)
    rest = x - hi.astype(F32)
    mid = rest.astype(BF16)
    lo = (rest - mid.astype(F32)).astype(BF16)
    outs = []
    for g in range(x.shape[0] // seg):
        rows = slice(g * seg, (g + 1) * seg)
        terms = jnp.concatenate([hi[rows], mid[rows], lo[rows]], axis=0)
        outs.append(jnp.dot(tri3, terms, preferred_element_type=F32))
    return jnp.concatenate(outs, axis=0)


@functools.partial(jax.custom_vjp, nondiff_argnums=(1, 2))
def _segment_cumsum_diff(x, seg, rev):
    return _segment_cumsum_mxu(x, seg, rev)


_segment_cumsum_diff.defvjp(lambda x, seg, rev: (_segment_cumsum_mxu(x, seg, rev), None),
                            lambda seg, rev, _, g: (_segment_cumsum_mxu(g, seg, not rev),))


def _gla_chunk(st, q, k, v, b, *, rev, consts, halves):
    row = consts
    c = q.shape[0]
    half = c // 2 if halves else c
    o = lax.dot_general((q * jnp.exp(b)).astype(BF16), st.astype(BF16), NT_DIMS, preferred_element_type=F32)
    spans = [slice(i, i + half) for i in range(0, c, half)]
    qs, bs, rows, os = [q[p] for p in spans], [b[p] for p in spans], [row[p] for p in spans], [o[p] for p in spans]

    def reach(s, h, masked):
        diff = bs[h] - b[s:s + 1]
        if masked:
            diff = jnp.where((rows[h] <= s) if rev else (rows[h] >= s), diff, -jnp.inf)
        return jnp.sum(qs[h] * (k[s:s + 1] * jnp.exp(diff)), axis=-1, keepdims=True) * v[s:s + 1]

    for s in range(c):
        own = s // half
        os[own] = os[own] + reach(s, own, True)
        if halves and own == (1 if rev else 0):
            os[1 - own] = os[1 - own] + reach(s, 1 - own, False)
    o = jnp.concatenate(os, axis=0) if halves else os[0]
    b_end = b[0:1] if rev else b[c - 1:c]
    kd = (k * jnp.exp(b_end - b)).astype(BF16)
    st_new = st * jnp.exp(b_end) + lax.dot_general(v.astype(BF16), kd, TN_DIMS, preferred_element_type=F32)
    return st_new, o


def _gla_fwd(q, k, v, v_cb, g, *, rev, add, name):
    L = q.shape[0]
    blk = min(GLA_BLOCK, L)
    nblk, nsub = L // blk, blk // GLA_CHUNK
    hd, hps = HG_HEAD_DIM, GLA_HEADS_PER_STEP
    wide = hd * hps

    def body(*refs):
        q_ref, k_ref, v_ref, g_ref = refs[:4]
        add_ref = refs[4] if add is not None else None
        o_ref, st_out, st_scr, b_scr = refs[-4:]
        consts = _gla_consts(rev)

        @pl.when(pl.program_id(1) == 0)
        def _():
            st_scr[...] = jnp.zeros(st_scr.shape, F32)

        b_scr[...] = _segment_cumsum(g_ref[...], GLA_CHUNK, rev)

        def step(i, carry):
            sl = pl.ds(pl.multiple_of((nsub - 1 - i if rev else i) * GLA_CHUNK, GLA_CHUNK), GLA_CHUNK)
            lanes = [slice(hi * hd, (hi + 1) * hd) for hi in range(hps)]
            ins = [(st_scr[hi], q_ref[sl, ln], k_ref[sl, ln], v_ref[sl, ln], b_scr[sl, ln])
                   for hi, ln in enumerate(lanes)]
            adds = [add_ref[sl, ln] for ln in lanes] if add_ref is not None else None
            outs = [_gla_chunk(*args, rev=rev, consts=consts, halves=True) for args in ins]
            for hi, ln in enumerate(lanes):
                st_out[hi, i] = ins[hi][0]
                st_scr[hi] = outs[hi][0]
                o_ref[sl, ln] = outs[hi][1] if adds is None else outs[hi][1] + adds[hi]
            return carry

        lax.fori_loop(0, nsub, step, 0)

    def pos(j):
        return nblk - 1 - j if rev else j

    spec = pl.BlockSpec((blk, wide), lambda h, j: (pos(j), h))
    in_specs = [spec, spec, pl.BlockSpec((blk, wide), lambda h, j: (pos(j), v_cb // hps + h)), spec]
    args = [q, k, v, g]
    if add is not None:
        in_specs.append(spec)
        args.append(add)
    return pl.pallas_call(
        body, name=name, grid=(HG_HEADS // hps, nblk), in_specs=in_specs,
        out_specs=[spec, pl.BlockSpec((hps, None, nsub, hd, hd), lambda h, j: (h, j, 0, 0, 0))],
        out_shape=[jax.ShapeDtypeStruct((L, HG_HEADS * hd), F32),
                   jax.ShapeDtypeStruct((HG_HEADS, nblk, nsub, hd, hd), F32)],
        scratch_shapes=[pltpu.VMEM((hps, hd, hd), F32), pltpu.VMEM((blk, wide), F32)],
        compiler_params=_cparams("parallel", "arbitrary"),
    )(*args)


def _gla_bwd(q, k, v, v_cb, g, do, states, *, rev, adds, name):
    L = q.shape[0]
    blk = min(GLA_BLOCK, L)
    nblk, nsub = L // blk, blk // GLA_CHUNK
    hd, hps = HG_HEAD_DIM, GLA_HEADS_PER_STEP
    wide = hd * hps
    n_add = 0 if adds is None else 2

    def body(*refs):
        q_ref, k_ref, v_ref, g_ref, do_ref, st_in = refs[:6]
        add_refs = refs[6:6 + n_add]
        dq_ref, dk_ref, dv_ref, dg_ref, dst_scr, b_scr, db_scr = refs[6 + n_add:]
        consts = _gla_consts(rev)
        chunk = functools.partial(_gla_chunk, rev=rev, consts=consts, halves=False)

        @pl.when(pl.program_id(1) == 0)
        def _():
            dst_scr[...] = jnp.zeros(dst_scr.shape, F32)

        b_scr[...] = _segment_cumsum(g_ref[...], GLA_CHUNK, rev)

        def rows(i):
            return pl.ds(pl.multiple_of((nsub - 1 - i if rev else i) * GLA_CHUNK, GLA_CHUNK), GLA_CHUNK)

        def step(n, carry):
            i = nsub - 1 - n
            sl = rows(i)
            lanes = [slice(hi * hd, (hi + 1) * hd) for hi in range(hps)]
            ins = [(st_in[hi, i], q_ref[sl, ln], k_ref[sl, ln], v_ref[sl, ln], b_scr[sl, ln])
                   for hi, ln in enumerate(lanes)]
            cts = [(dst_scr[hi], do_ref[sl, ln]) for hi, ln in enumerate(lanes)]
            adds = [(add_refs[0][sl, ln], add_refs[1][sl, ln]) for ln in lanes] if n_add else None
            outs = [jax.vjp(chunk, *args)[1](ct) for args, ct in zip(ins, cts)]
            for hi, ln in enumerate(lanes):
                dst, dq, dk, dv, db = outs[hi]
                dst_scr[hi] = dst
                dq_ref[sl, ln] = dq if adds is None else dq + adds[hi][0]
                dk_ref[sl, ln] = dk
                dv_ref[sl, ln] = dv if adds is None else dv + adds[hi][1]
                db_scr[sl, ln] = db
            return carry

        lax.fori_loop(0, nsub, step, 0)
        dg_ref[...] = _segment_cumsum(db_scr[...], GLA_CHUNK, not rev)

    def pos(j):
        p = nblk - 1 - j
        return nblk - 1 - p if rev else p

    spec = pl.BlockSpec((blk, wide), lambda h, j: (pos(j), h))
    in_specs = [spec, spec, pl.BlockSpec((blk, wide), lambda h, j: (pos(j), v_cb // hps + h)), spec, spec,
                pl.BlockSpec((hps, None, nsub, hd, hd), lambda h, j: (h, nblk - 1 - j, 0, 0, 0))]
    args = [q, k, v, g, do, states]
    if adds is not None:
        in_specs += [spec, spec]
        args += list(adds)
    out = jax.ShapeDtypeStruct((L, HG_HEADS * hd), F32)
    return pl.pallas_call(
        body, name=name, grid=(HG_HEADS // hps, nblk), in_specs=in_specs,
        out_specs=[spec] * 4, out_shape=[out] * 4,
        scratch_shapes=[pltpu.VMEM((hps, hd, hd), F32), pltpu.VMEM((blk, wide), F32), pltpu.VMEM((blk, wide), F32)],
        compiler_params=_cparams("parallel", "arbitrary"),
    )(*args)


def _ssd_consts(rev):
    c = SSD_CHUNK
    gw = SSM_HPG * SSM_HEAD_DIM
    r2 = lax.broadcasted_iota(jnp.int32, (c, c), 0)
    c2 = lax.broadcasted_iota(jnp.int32, (c, c), 1)
    low = (r2 <= c2) if rev else (r2 >= c2)
    lane = lax.broadcasted_iota(jnp.int32, (1, gw), 1)
    return low, r2 == c2, lane, lane >> 6


def _expand_heads(v4, head_of_lane):
    first = head_of_lane[:, :LANES] == 0
    cols = [v4[:, j:j + 1] for j in range(SSM_HPG)]
    return jnp.concatenate([jnp.where(first, cols[0], cols[1]), jnp.where(first, cols[2], cols[3])], axis=1)


def _ssd_prep(dtr, bias, alog, dsk, *, rev, consts):
    head_of_lane = consts[-1]
    dt_l = _expand_heads(jax.nn.softplus(dtr + bias), head_of_lane)
    a_l = _expand_heads(-jnp.exp(alog), head_of_lane)
    return dt_l, _segment_cumsum_diff(dt_l * a_l, SSD_CHUNK, rev), _expand_heads(dsk, head_of_lane)


def _ssd_chunk(st, x, bm, cm, dt_l, acum, dsk_l, *, rev, consts, skip):
    low, eye, lane, head_of_lane = consts
    c = x.shape[0]
    xd = x * dt_l
    cb = lax.dot_general(cm.astype(BF16), bm.astype(BF16), NT_DIMS, preferred_element_type=F32)
    y = jnp.dot(cm.astype(BF16), st.astype(BF16), preferred_element_type=F32) * jnp.exp(acum)
    for j in range(SSM_HPG):
        acol = jnp.sum(jnp.where(lane == j * SSM_HEAD_DIM, acum, 0.0), axis=-1, keepdims=True)
        ab = jnp.broadcast_to(acol, (c, c))
        arow = jnp.sum(jnp.where(eye, ab, 0.0), axis=0, keepdims=True)
        lmat = jnp.exp(jnp.where(low, ab - arow, -jnp.inf))
        xj = jnp.where(head_of_lane == j, xd, 0.0)
        y = y + jnp.dot((cb * lmat).astype(BF16), xj.astype(BF16), preferred_element_type=F32)
    a_end = acum[0:1] if rev else acum[c - 1:c]
    xdec = (xd * jnp.exp(a_end - acum)).astype(BF16)
    st_new = st * jnp.exp(a_end) + lax.dot_general(bm.astype(BF16), xdec, TN_DIMS, preferred_element_type=F32)
    if skip:
        y = y + x * dsk_l
    return st_new, y


def _ssd_specs(L, rev):
    blk = min(SSD_BLOCK, L)
    nblk = L // blk
    gw = SSM_HPG * SSM_HEAD_DIM

    def pos(j):
        return nblk - 1 - j if rev else j

    return blk, nblk, gw, pos


def _ssd_fwd(xbc, dtr, bias, alog, dsk, *, rev, skip, add, name):
    L = xbc.shape[0]
    blk, nblk, gw, pos = _ssd_specs(L, rev)
    nsub = blk // SSD_CHUNK
    n = SSM_STATE

    gps = SSD_FWD_GROUPS_PER_STEP

    def body(*refs):
        x_ref, b_ref, c_ref, dt_ref, bias_ref, alog_ref, dsk_ref = refs[:7]
        add_ref = refs[7] if add is not None else None
        y_ref, st_out, st_scr, dt_scr, ac_scr = refs[-5:]
        consts = _ssd_consts(rev)
        xl = [slice(gi * gw, (gi + 1) * gw) for gi in range(gps)]
        nl = [slice(gi * n, (gi + 1) * n) for gi in range(gps)]

        @pl.when(pl.program_id(1) == 0)
        def _():
            st_scr[...] = jnp.zeros(st_scr.shape, F32)

        dsk_l = []
        for gi in range(gps):
            dt_scr[:, xl[gi]], ac_scr[:, xl[gi]], d = _ssd_prep(dt_ref[gi], bias_ref[gi], alog_ref[gi], dsk_ref[gi],
                                                              rev=rev, consts=consts)
            dsk_l.append(d)

        def step(i, carry):
            sl = pl.ds(pl.multiple_of((nsub - 1 - i if rev else i) * SSD_CHUNK, SSD_CHUNK), SSD_CHUNK)
            ins = [(st_scr[gi], x_ref[sl, xl[gi]], b_ref[sl, nl[gi]], c_ref[sl, nl[gi]], dt_scr[sl, xl[gi]],
                    ac_scr[sl, xl[gi]], dsk_l[gi]) for gi in range(gps)]
            adds = [add_ref[sl, xl[gi]] for gi in range(gps)] if add_ref is not None else None
            outs = [_ssd_chunk(*a, rev=rev, consts=consts, skip=skip) for a in ins]
            for gi in range(gps):
                st_out[gi, i] = ins[gi][0]
                st_scr[gi] = outs[gi][0]
                y_ref[sl, xl[gi]] = outs[gi][1] if adds is None else outs[gi][1] + adds[gi]
            return carry

        lax.fori_loop(0, nsub, step, 0)

    b0 = D_INNER // (n * gps)
    yspec = pl.BlockSpec((blk, gw * gps), lambda g, j: (pos(j), g))
    pspec = pl.BlockSpec((gps, 1, SSM_HPG), lambda g, j: (g, 0, 0))
    in_specs = [yspec,
                pl.BlockSpec((blk, n * gps), lambda g, j: (pos(j), b0 + g)),
                pl.BlockSpec((blk, n * gps), lambda g, j: (pos(j), b0 + SSM_GROUPS // gps + g)),
                pl.BlockSpec((gps, blk, SSM_HPG), lambda g, j: (g, pos(j), 0)),
                pspec, pspec, pspec]
    args = [xbc, xbc, xbc, dtr, bias, alog, dsk]
    if add is not None:
        in_specs.append(yspec)
        args.append(add)
    return pl.pallas_call(
        body, name=name, grid=(SSM_GROUPS // gps, nblk), in_specs=in_specs,
        out_specs=[yspec, pl.BlockSpec((gps, None, nsub, n, gw), lambda g, j: (g, j, 0, 0, 0))],
        out_shape=[jax.ShapeDtypeStruct((L, D_INNER), F32),
                   jax.ShapeDtypeStruct((SSM_GROUPS, nblk, nsub, n, gw), F32)],
        scratch_shapes=[pltpu.VMEM((gps, n, gw), F32), pltpu.VMEM((blk, gw * gps), F32),
                        pltpu.VMEM((blk, gw * gps), F32)],
        compiler_params=_cparams("parallel", "arbitrary"),
    )(*args)


def _ssd_bwd(xbc, dtr, bias, alog, dsk, dy, states, *, rev, skip, adds, name):
    L = xbc.shape[0]
    blk, nblk, gw, _ = _ssd_specs(L, rev)
    nsub = blk // SSD_CHUNK
    n = SSM_STATE
    n_add = 0 if adds is None else 3
    gps = SSD_BWD_GROUPS_PER_STEP

    def body(*refs):
        x_ref, b_ref, c_ref, dt_ref, bias_ref, alog_ref, dsk_ref, dy_ref, st_in = refs[:9]
        add_refs = refs[9:9 + n_add]
        outs = refs[9 + n_add:]
        dx_ref, db_ref, dc_ref, ddt_ref, dbias_ref, dalog_ref, ddsk_ref = outs[:7]
        dst_scr, dt_scr, ac_scr, ddt_scr, dac_scr = outs[7:]
        consts = _ssd_consts(rev)
        chunk = functools.partial(_ssd_chunk, rev=rev, consts=consts, skip=skip)
        prep = functools.partial(_ssd_prep, rev=rev, consts=consts)

        @pl.when(pl.program_id(1) == 0)
        def _():
            dst_scr[...] = jnp.zeros(dst_scr.shape, F32)
            dbias_ref[...] = jnp.zeros(dbias_ref.shape, F32)
            dalog_ref[...] = jnp.zeros(dalog_ref.shape, F32)
            ddsk_ref[...] = jnp.zeros(ddsk_ref.shape, F32)

        xl = [slice(gi * gw, (gi + 1) * gw) for gi in range(gps)]
        nl = [slice(gi * n, (gi + 1) * n) for gi in range(gps)]
        dsk_l, prep_vjp = [], []
        for gi in range(gps):
            narrow = (dt_ref[gi], bias_ref[gi], alog_ref[gi], dsk_ref[gi])
            (dt_scr[:, xl[gi]], ac_scr[:, xl[gi]], d), pv = jax.vjp(prep, *narrow)
            dsk_l.append(d)
            prep_vjp.append(pv)

        def step(k, ddsk_l):
            i = nsub - 1 - k
            sl = pl.ds(pl.multiple_of((nsub - 1 - i if rev else i) * SSD_CHUNK, SSD_CHUNK), SSD_CHUNK)
            ins = [(st_in[gi, i], x_ref[sl, xl[gi]], b_ref[sl, nl[gi]], c_ref[sl, nl[gi]], dt_scr[sl, xl[gi]],
                    ac_scr[sl, xl[gi]], dsk_l[gi]) for gi in range(gps)]
            cts = [(dst_scr[gi], dy_ref[sl, xl[gi]]) for gi in range(gps)]
            more = [(add_refs[0][sl, xl[gi]], add_refs[1][sl, nl[gi]], add_refs[2][sl, nl[gi]])
                    for gi in range(gps)] if n_add else None
            grads = [jax.vjp(chunk, *a)[1](ct) for a, ct in zip(ins, cts)]
            new = []
            for gi in range(gps):
                dst, dx, db, dc, ddt_l, dac, ddsk_k = grads[gi]
                dst_scr[gi] = dst
                dx_ref[sl, xl[gi]] = dx if more is None else dx + more[gi][0]
                db_ref[sl, nl[gi]] = db if more is None else db + more[gi][1]
                dc_ref[sl, nl[gi]] = dc if more is None else dc + more[gi][2]
                ddt_scr[sl, xl[gi]] = ddt_l
                dac_scr[sl, xl[gi]] = dac
                new.append(ddsk_l[gi] + ddsk_k)
            return tuple(new)

        ddsk_l = lax.fori_loop(0, nsub, step, tuple(jnp.zeros((1, gw), F32) for _ in range(gps)))
        for gi in range(gps):
            ddt, dbias, dalog, ddsk = prep_vjp[gi]((ddt_scr[:, xl[gi]], dac_scr[:, xl[gi]], ddsk_l[gi]))
            ddt_ref[gi] = ddt
            dbias_ref[gi] += dbias
            dalog_ref[gi] += dalog
            ddsk_ref[gi] += ddsk

    def pos(j):
        p = nblk - 1 - j
        return nblk - 1 - p if rev else p

    b0 = D_INNER // (n * gps)
    xspec = pl.BlockSpec((blk, gw * gps), lambda g, j: (pos(j), g))
    nspec = pl.BlockSpec((blk, n * gps), lambda g, j: (pos(j), g))
    dtspec = pl.BlockSpec((gps, blk, SSM_HPG), lambda g, j: (g, pos(j), 0))
    pspec = pl.BlockSpec((gps, 1, SSM_HPG), lambda g, j: (g, 0, 0))
    in_specs = [xspec,
                pl.BlockSpec((blk, n * gps), lambda g, j: (pos(j), b0 + g)),
                pl.BlockSpec((blk, n * gps), lambda g, j: (pos(j), b0 + SSM_GROUPS // gps + g)),
                dtspec, pspec, pspec, pspec, xspec,
                pl.BlockSpec((gps, None, nsub, n, gw), lambda g, j: (g, nblk - 1 - j, 0, 0, 0))]
    args = [xbc, xbc, xbc, dtr, bias, alog, dsk, dy, states]
    if adds is not None:
        in_specs += [xspec, nspec, nspec]
        args += list(adds)
    par = jax.ShapeDtypeStruct((SSM_GROUPS, 1, SSM_HPG), F32)
    return pl.pallas_call(
        body, name=name, grid=(SSM_GROUPS // gps, nblk), in_specs=in_specs,
        out_specs=[xspec, nspec, nspec, dtspec, pspec, pspec, pspec],
        out_shape=[jax.ShapeDtypeStruct((L, D_INNER), F32), jax.ShapeDtypeStruct((L, GN), F32),
                   jax.ShapeDtypeStruct((L, GN), F32), jax.ShapeDtypeStruct((SSM_GROUPS, L, SSM_HPG), F32),
                   par, par, par],
        scratch_shapes=[pltpu.VMEM((gps, n, gw), F32)] + [pltpu.VMEM((blk, gw * gps), F32)] * 4,
        compiler_params=_cparams("parallel", "arbitrary"),
    )(*args)


def _out_proj(y, w_out, h, next_norm, name):
    if next_norm is None:
        return _matmul(y, w_out, add=h, name=name), None
    M, K = y.shape
    N = w_out.shape[1]
    tm = 512

    def body(y_ref, w_ref, h_ref, n_ref, o_ref, u_ref):
        acc = jnp.dot(y_ref[...], w_ref[...], preferred_element_type=F32) + h_ref[...]
        o_ref[...] = acc
        u_ref[...] = _rms(acc, n_ref[...]).astype(BF16)

    rows = pl.BlockSpec((tm, N), lambda i: (i, 0))
    return pl.pallas_call(
        body, name=name, grid=(M // tm,),
        in_specs=[pl.BlockSpec((tm, K), lambda i: (i, 0)), pl.BlockSpec((K, N), lambda i: (0, 0)), rows,
                  pl.BlockSpec((1, N), lambda i: (0, 0))],
        out_specs=[rows, rows], out_shape=[jax.ShapeDtypeStruct((M, N), F32), jax.ShapeDtypeStruct((M, N), BF16)],
        compiler_params=_cparams("parallel"),
    )(y, w_out, h, next_norm)


def _norm_fwd(h, w, name):
    d = h.shape[1]
    return _row_kernel(lambda hv, wv: (_rms(hv, wv),), [(h, d, 0)], [w], [(d, BF16)], [], name=name)[0]


def _matmul_norm_bwd(a, b, h, w, dh_in, *, name, add=None, b_cols=None, after=None):
    M, K = a.shape
    N = b.shape[0]
    cb = 0 if b_cols is None else b_cols[0]
    tm = 512 if K <= 4096 else 256

    def body(*refs):
        a_ref, b_ref, h_ref, w_ref, dh_ref = refs[:5]
        o32_ref, o16_ref, dw_ref = refs[-3:]
        du = lax.dot_general(a_ref[...], b_ref[...], NT_DIMS, preferred_element_type=F32)
        if add is not None:
            du = du + refs[5][...]
        _, vjp = jax.vjp(_rms, h_ref[...], w_ref[...])
        dh, dw = vjp(du)
        dh = dh + dh_ref[...]
        o32_ref[...] = dh
        o16_ref[...] = dh.astype(BF16)

        @pl.when(pl.program_id(0) == 0)
        def _():
            dw_ref[...] = jnp.zeros(dw_ref.shape, F32)

        dw_ref[...] += dw

    rows = pl.BlockSpec((tm, N), lambda i: (i, 0))
    in_specs = [pl.BlockSpec((tm, K), lambda i: (i, 0)), pl.BlockSpec((N, K), lambda i: (0, cb)), rows,
                pl.BlockSpec((1, N), lambda i: (0, 0)), rows]
    args = [a, b, h, w, dh_in]
    if add is not None:
        in_specs.append(rows)
        args.append(add)
    if after is not None:
        in_specs.append(pl.BlockSpec(memory_space=pl.ANY))
        args.append(after)
    return pl.pallas_call(
        body, name=name, grid=(M // tm,), in_specs=in_specs,
        out_specs=[rows, rows, pl.BlockSpec((1, N), lambda i: (0, 0))],
        out_shape=[jax.ShapeDtypeStruct((M, N), F32), jax.ShapeDtypeStruct((M, N), BF16),
                   jax.ShapeDtypeStruct((1, N), F32)],
        compiler_params=_cparams("arbitrary"),
    )(*args)


def _ffn_fwd(h, norm_w, w_in, w_out, conv_w, conv_b, tag, u, next_norm=None, loss_with=None):
    pg = _matmul(u, w_in, b_cols=(0, D_FF), name=f"{tag}_in_gate")
    pv = _matmul(u, w_in, b_cols=(1, D_FF), name=f"{tag}_in_val")
    y = _col_kernel(lambda g, v, w, b: (_ffn_mid_fn(g, v, w, b),), [(pg, 0), (pv, 0)], [(conv_w, 0), (conv_b, 0)],
                    [BF16], [], name=f"{tag}_mid", n_tiles=D_FF // COL_TILE)[0]
    if loss_with is not None:
        return _out_proj_loss(y, w_out, h, *loss_with, f"{tag}_out_loss"), (u, pg, pv, y), None
    h_out, u_next = _out_proj(y, w_out, h, next_norm, f"{tag}_out")
    return h_out, (u, pg, pv, y), u_next


def _ffn_bwd(h, dh, dh16, saved, norm_w, w_in, w_out, conv_w, conv_b, tag, after=None):
    u, pg, pv, y = saved
    dy = _matmul(dh16, w_out, nt=True, after=after, name=f"{tag}_out_dx")
    dw_out = _matmul(y, dh16, ta=True, name=f"{tag}_out_dw")

    def fn(g, v, ct, w, b):
        _, vjp = jax.vjp(_ffn_mid_fn, g, v, w, b)
        return vjp(ct)

    dpg, dpv, dcw, dcb = _col_kernel(fn, [(pg, 0), (pv, 0), (dy, 0)], [(conv_w, 0), (conv_b, 0)], [BF16, BF16],
                                     [FFN_CONV, 1], name=f"{tag}_mid_bwd", n_tiles=D_FF // COL_TILE)
    du = _matmul(dpg, w_in, nt=True, b_cols=(0, D_FF), name=f"{tag}_gate_dx")
    dw_gate = _matmul(u, dpg, ta=True, name=f"{tag}_gate_dw")
    dw_val = _matmul(u, dpv, ta=True, name=f"{tag}_val_dw")
    dh, dh16, dnw = _matmul_norm_bwd(dpv, w_in, h, norm_w, dh, add=du, b_cols=(1, D_FF), name=f"{tag}_val_dx_norm")
    return dh, dh16, dict(w_in=(dw_gate, dw_val), w_out=dw_out, conv_w=dcw, conv_b=dcb, norm=dnw)


def _hgrn2_fwd(h, norm_w, w_in, lb_logits, a_norm_w, w_out, after=None, next_norm=None):
    d = D_MODEL
    u = _norm_fwd(h, norm_w, "hg_norm")
    pa = _matmul(u, w_in, after=after, name="hg_in")
    qs, lf_fw, k_fw, lf_bw, k_bw = _row_kernel(
        _hgrn2_pre_fn, [(pa, d, 0), (pa, d, 1), (pa, d, 2)], [lb_logits], [(d, F32)] * 5, [], name="hg_pre")
    o_fw, st_fw = _gla_fwd(qs, k_fw, pa, 3 * HG_HEADS, lf_fw, rev=False, add=None, name="hg_gla_fw")
    o, st_bw = _gla_fwd(qs, k_bw, pa, 3 * HG_HEADS, lf_bw, rev=True, add=o_fw, name="hg_gla_bw")
    y = _row_kernel(lambda ov, gv, wv: (_hgrn2_post_fn(ov, gv, wv),), [(o, d, 0), (pa, d, 4)], [a_norm_w],
                    [(d, BF16)], [], name="hg_post")[0]
    h_out, u_next = _out_proj(y, w_out, h, next_norm, "hg_out")
    return h_out, (u, pa, qs, lf_fw, k_fw, lf_bw, k_bw, st_fw, st_bw, o, y), u_next


def _hgrn2_bwd(h, dh, dh16, saved, norm_w, w_in, lb_logits, a_norm_w, w_out, send, after=None):
    d = D_MODEL
    u, pa, qs, lf_fw, k_fw, lf_bw, k_bw, st_fw, st_bw, o, y = saved
    dy = _matmul(dh16, w_out, nt=True, after=after, name="hg_out_dx")
    dw_out = _matmul(y, dh16, ta=True, name="hg_out_dw")

    def post_bwd(ov, gv, ct, wv):
        _, vjp = jax.vjp(_hgrn2_post_fn, ov, gv, wv)
        return vjp(ct)

    do, dg, dnw = _row_kernel(post_bwd, [(o, d, 0), (pa, d, 4), (dy, d, 0)], [a_norm_w], [(d, F32), (d, F32)],
                              [(1, HG_HEAD_DIM)], name="hg_post_bwd")
    dq1, dk_fw, dv1, dlf_fw = _gla_bwd(qs, k_fw, pa, 3 * HG_HEADS, lf_fw, do, st_fw, rev=False, adds=None,
                                       name="hg_gla_fw_bwd")
    dqs, dk_bw, dv, dlf_bw = _gla_bwd(qs, k_bw, pa, 3 * HG_HEADS, lf_bw, do, st_bw, rev=True, adds=(dq1, dv1),
                                      name="hg_gla_bw_bwd")

    def pre_bwd(qr, fr, br, c0, c1, c2, c3, c4, dvv, dgv, lbl):
        _, vjp = jax.vjp(_hgrn2_pre_fn, qr, fr, br, lbl)
        dq, df, db, dlbl = vjp((c0, c1, c2, c3, c4))
        return jnp.concatenate([dq, df, db, dvv, dgv], axis=1), dlbl

    rows = [(pa, d, 0), (pa, d, 1), (pa, d, 2), (dqs, d, 0), (dlf_fw, d, 0), (dk_fw, d, 0), (dlf_bw, d, 0),
            (dk_bw, d, 0), (dv, d, 0), (dg, d, 0)]
    dpa, dlbl = _row_kernel(pre_bwd, rows, [lb_logits], [(5 * d, BF16)], [lb_logits.shape], name="hg_pre_bwd")
    dw_in = _matmul(u, dpa, ta=True, col_parts=N_DEV, name="hg_in_dw")
    token = send(dw_in, dw_out)
    dh, dh16, dn1 = _matmul_norm_bwd(dpa, w_in, h, norm_w, dh, after=token, name="hg_in_dx_norm")
    return dh, dh16, dict(lb=dlbl, a_norm=dnw, norm=dn1)


def _group_params(p):
    return p.reshape(SSM_GROUPS, 1, SSM_HPG)


def _mamba_fwd(h, norm_w, w_z, w_xbc, w_dt, conv_w, conv_b, dt_bias, a_log, d_skip, b_norm_w, w_out, u, next_norm):
    L = h.shape[0]
    z = _matmul(u, w_z, name="mb_in_z")
    xbc_raw = _matmul(u, w_xbc, name="mb_in_xbc")
    dt_raw = _matmul(u, w_dt, name="mb_in_dt")
    xbc = _col_kernel(lambda xv, w, b: (_mamba_conv_fn(xv, w, b),), [(xbc_raw, 0)], [(conv_w, 0), (conv_b, 0)],
                      [F32], [], name="mb_conv", n_tiles=CONV_DIM // COL_TILE)[0]
    dtr = dt_raw.reshape(L, 2, SSM_GROUPS, SSM_HPG).transpose(1, 2, 0, 3)
    bias, alog = dt_bias.reshape(2, -1), a_log.reshape(2, -1)
    dsk = _group_params(d_skip.reshape(-1))
    y_fw, st_fw = _ssd_fwd(xbc, dtr[0], _group_params(bias[0]), _group_params(alog[0]), dsk, rev=False, skip=True,
                           add=None, name="mb_ssd_fw")
    ysum, st_bw = _ssd_fwd(xbc, dtr[1], _group_params(bias[1]), _group_params(alog[1]), dsk, rev=True, skip=False,
                           add=y_fw, name="mb_ssd_bw")
    y = _row_kernel(lambda yv, zv, wv: (_mamba_post_fn(yv, zv, wv),), [(ysum, D_INNER, 0), (z, D_INNER, 0)],
                    [b_norm_w], [(D_INNER, BF16)], [], name="mb_post")[0]
    h_out, u_next = _out_proj(y, w_out, h, next_norm, "mb_out")
    return h_out, (u, z, xbc_raw, xbc, dtr, st_fw, st_bw, ysum, y), u_next


def _mamba_bwd(h, dh, dh16, saved, norm_w, w_z, w_xbc, w_dt, conv_w, conv_b, dt_bias, a_log, d_skip, b_norm_w, w_out,
               after=None):
    L = h.shape[0]
    u, z, xbc_raw, xbc, dtr, st_fw, st_bw, ysum, y = saved
    dy = _matmul(dh16, w_out, nt=True, after=after, name="mb_out_dx")
    dw_out = _matmul(y, dh16, ta=True, name="mb_out_dw")

    def post_bwd(yv, zv, ct, wv):
        _, vjp = jax.vjp(_mamba_post_fn, yv, zv, wv)
        return vjp(ct)

    dys, dz, dbn = _row_kernel(post_bwd, [(ysum, D_INNER, 0), (z, D_INNER, 0), (dy, D_INNER, 0)], [b_norm_w],
                               [(D_INNER, F32), (D_INNER, BF16)], [(1, D_INNER)], name="mb_post_bwd")
    bias, alog = dt_bias.reshape(2, -1), a_log.reshape(2, -1)
    dsk = _group_params(d_skip.reshape(-1))
    dx1, db1, dc1, ddt_fw, dbias_fw, dalog_fw, ddsk = _ssd_bwd(
        xbc, dtr[0], _group_params(bias[0]), _group_params(alog[0]), dsk, dys, st_fw, rev=False, skip=True,
        adds=None, name="mb_ssd_fw_bwd")
    dx, db, dc, ddt_bw, dbias_bw, dalog_bw, _ = _ssd_bwd(
        xbc, dtr[1], _group_params(bias[1]), _group_params(alog[1]), dsk, dys, st_bw, rev=True, skip=False,
        adds=(dx1, db1, dc1), name="mb_ssd_bw_bwd")

    nx, nb = D_INNER // COL_TILE, GN // COL_TILE

    def conv_bwd(xv, cx, cbm, ccm, w, b):
        j = pl.program_id(0)
        ct = jnp.where(j < nx, cx, jnp.where(j < nx + nb, cbm, ccm))
        _, vjp = jax.vjp(_mamba_conv_fn, xv, w, b)
        return vjp(ct)

    cols = [(xbc_raw, 0), (dx, lambda j: jnp.minimum(j, nx - 1)), (db, lambda j: jnp.clip(j - nx, 0, nb - 1)),
            (dc, lambda j: jnp.clip(j - nx - nb, 0, nb - 1))]
    dxbc, dcw, dcb = _col_kernel(conv_bwd, cols, [(conv_w, 0), (conv_b, 0)], [BF16], [SSM_CONV, 1],
                                 name="mb_conv_bwd", n_tiles=CONV_DIM // COL_TILE)
    ddt = jnp.stack([ddt_fw, ddt_bw]).transpose(2, 0, 1, 3).reshape(L, 2 * SSM_GROUPS * SSM_HPG).astype(BF16)
    du = _matmul(dz, w_z, nt=True, name="mb_z_dx")
    du = _matmul(dxbc, w_xbc, nt=True, add=du, name="mb_xbc_dx")
    dw_in = jnp.concatenate([_matmul(u, dz, ta=True, name="mb_z_dw"), _matmul(u, dxbc, ta=True, name="mb_xbc_dw"),
                             _matmul(u, ddt, ta=True, name="mb_dt_dw")], axis=1)
    dh, dh16, dn1 = _matmul_norm_bwd(ddt, w_dt, h, norm_w, dh, add=du, name="mb_dt_dx_norm")
    grads = dict(w_in=dw_in, w_out=dw_out, conv_w=dcw, conv_b=dcb, b_norm=dbn, norm=dn1,
                 dt_bias=jnp.stack([dbias_fw, dbias_bw]).reshape(1, 2, -1),
                 a_log=jnp.stack([dalog_fw, dalog_bw]).reshape(1, 2, -1), d_skip=ddsk.reshape(1, -1))
    return dh, dh16, grads


def _out_proj_loss(y, w_out, h, target, w, name):
    M, K = y.shape
    N = w_out.shape[1]
    tm = 512

    def body(y_ref, w_ref, h_ref, t_ref, n_ref, dh_ref, dh16_ref, loss_ref, dw_ref):
        out = jnp.dot(y_ref[...], w_ref[...], preferred_element_type=F32) + h_ref[...]
        target = t_ref[...]

        def loss(hv, wv):
            err = _rms(hv, wv) - target
            return 0.5 * jnp.sum(jnp.mean(err * err, axis=-1, keepdims=True), axis=0, keepdims=True)

        val, vjp = jax.vjp(loss, out, n_ref[...])
        dh, dw = vjp(jnp.ones((1, 1), F32))
        dh_ref[...] = dh
        dh16_ref[...] = dh.astype(BF16)

        @pl.when(pl.program_id(0) == 0)
        def _():
            loss_ref[...] = jnp.zeros(loss_ref.shape, F32)
            dw_ref[...] = jnp.zeros(dw_ref.shape, F32)

        loss_ref[...] += val
        dw_ref[...] += dw

    rows = pl.BlockSpec((tm, N), lambda i: (i, 0))
    return pl.pallas_call(
        body, name=name, grid=(M // tm,),
        in_specs=[pl.BlockSpec((tm, K), lambda i: (i, 0)), pl.BlockSpec((K, N), lambda i: (0, 0)), rows, rows,
                  pl.BlockSpec((1, N), lambda i: (0, 0))],
        out_specs=[rows, rows, pl.BlockSpec((1, 1), lambda i: (0, 0)), pl.BlockSpec((1, N), lambda i: (0, 0))],
        out_shape=[jax.ShapeDtypeStruct((M, N), F32), jax.ShapeDtypeStruct((M, N), BF16),
                   jax.ShapeDtypeStruct((1, 1), F32), jax.ShapeDtypeStruct((1, N), F32)],
        compiler_params=_cparams("arbitrary"),
    )(y, w_out, h, target, w)


def _local_step(x, target, rep, comm):
    def ffn_args(i, w, small):
        return (rep["norm2_w"][i:i + 1], w["w_in"], w["w_out"], small["ffn_conv_w"][i], rep["ffn_conv_b"][i:i + 1])

    def mamba_args(w, small):
        w_in = w["w_in"]
        return (rep["norm1_w"][1:2], w_in[:, :D_INNER], w_in[:, D_INNER:D_INNER + CONV_DIM],
                w_in[:, D_INNER + CONV_DIM:], small["b_conv_w"][0], small["b_conv_b"], rep["b_dt_bias"],
                rep["b_a_log"], rep["b_d_skip"], small["b_norm_w"], w["w_out"])

    w_hg = comm.weights("hg", None)
    hg = (rep["norm1_w"][0:1], w_hg["w_in"], rep["a_lb_logits"], rep["a_norm_w"], w_hg["w_out"])
    h0 = x
    h1, s_hg, u1 = _hgrn2_fwd(h0, *hg, after=w_hg.get("token"), next_norm=rep["norm2_w"][0:1])
    w_f0 = comm.weights("f0", h1)
    small = w_f0["small"]
    f0 = ffn_args(0, w_f0, small)
    h2, s_f0, u2 = _ffn_fwd(h1, *f0, "ffn0", u1, next_norm=rep["norm1_w"][1:2])
    mb = mamba_args(comm.weights("mb", h2), small)
    h3, s_mb, u3 = _mamba_fwd(h2, *mb, u2, rep["norm2_w"][1:2])
    f1 = ffn_args(1, comm.weights("f1", h3), small)
    (dh, dh16, loss, d_final), s_f1, _ = _ffn_fwd(h3, *f1, "ffn1", u3,
                                                  loss_with=(target, rep["final_norm_w"].reshape(1, -1)))

    dh, dh16, g_f1 = _ffn_bwd(h3, dh, dh16, s_f1, *f1, "ffn1")
    token = comm.send("f1", dict(w_in=g_f1["w_in"], w_out=g_f1["w_out"]))
    dh, dh16, g_mb = _mamba_bwd(h2, dh, dh16, s_mb, *mb, after=token)
    token = comm.send("mb", dict(w_in=g_mb["w_in"], w_out=g_mb["w_out"]))
    dh, dh16, g_f0 = _ffn_bwd(h1, dh, dh16, s_f0, *f0, "ffn0", after=token)
    token = comm.send("f0", dict(w_in=g_f0["w_in"], w_out=g_f0["w_out"]))
    small_grads = dict(b_conv_w=g_mb["conv_w"][None], b_conv_b=g_mb["conv_b"], b_norm_w=g_mb["b_norm"],
                       ffn_conv_w=jnp.stack([g_f0["conv_w"], g_f1["conv_w"]]))
    dh, dh16, g_hg = _hgrn2_bwd(
        h0, dh, dh16, s_hg, *hg, after=token,
        send=lambda dw_in, dw_out: comm.send("hg", dict(w_in=dw_in, w_out=dw_out, small=small_grads)))
    grads = dict(
        norm1_w=jnp.concatenate([g_hg["norm"], g_mb["norm"]], axis=0),
        norm2_w=jnp.concatenate([g_f0["norm"], g_f1["norm"]], axis=0),
        a_lb_logits=g_hg["lb"], a_norm_w=g_hg["a_norm"], b_dt_bias=g_mb["dt_bias"], b_a_log=g_mb["a_log"],
        b_d_skip=g_mb["d_skip"], ffn_conv_b=jnp.concatenate([g_f0["conv_b"], g_f1["conv_b"]], axis=0),
        final_norm_w=d_final.reshape(-1),
    )
    return loss, dh, grads


def _mesh_pos():
    return lax.axis_index("x"), lax.axis_index("y"), lax.axis_index("c")


N_COPIES = N_DEV - 1


def _comm_call(body, ins, out_shape, name):
    n = len(ins)
    hbm = pl.BlockSpec(memory_space=pl.ANY)
    return pl.pallas_call(
        body, name=name, out_shape=out_shape, in_specs=[hbm] * n, out_specs=[hbm] * n,
        scratch_shapes=[pltpu.SemaphoreType.DMA((n * N_COPIES,)), pltpu.SemaphoreType.DMA((n * N_COPIES,)),
                        pltpu.SemaphoreType.DMA((n,))],
    )(*ins)


def _all_gather(shards, name):
    n = len(shards)

    def body(*refs):
        x_refs, out_refs = refs[:n], refs[n:2 * n]
        send_sems, recv_sems, local_sems = refs[2 * n:]
        x, y, c = _mesh_pos()
        me, sibling = (x, y, c), (x, y, 1 - c)
        chips = [(1 - x, y), (x, 1 - y), (1 - x, 1 - y)]

        def copy(w, k, block, to, own=False):
            px, py, pc = block
            dst = out_refs[w].at[4 * px + 2 * py + pc]
            return pltpu.make_async_remote_copy(
                src_ref=x_refs[w] if own else dst, dst_ref=dst, send_sem=send_sems.at[w * N_COPIES + k],
                recv_sem=recv_sems.at[w * N_COPIES + k], device_id=to, device_id_type=MESH)

        mine = [pltpu.make_async_copy(x_refs[w], out_refs[w].at[4 * x + 2 * y + c], local_sems.at[w]) for w in range(n)]
        for cp in mine:
            cp.start()
        first = [copy(w, 1 + j, me, (*chip, c), own=True) for j, chip in enumerate(chips) for w in range(n)]
        first += [copy(w, 0, me, sibling, own=True) for w in range(n)]
        for cp in first:
            cp.start()
        passed = []
        for j, chip in enumerate(chips):
            for w in range(n):
                copy(w, 1 + j, (*chip, c), me).wait_recv()
                passed.append(copy(w, 4 + j, (*chip, c), sibling))
                passed[-1].start()
        for w in range(n):
            copy(w, 0, sibling, me).wait_recv()
        for j, chip in enumerate(chips):
            for w in range(n):
                copy(w, 4 + j, (*chip, 1 - c), me).wait_recv()
        for cp in first + passed:
            cp.wait_send()
        for cp in mine:
            cp.wait()

    out_shape = [jax.ShapeDtypeStruct((N_DEV, *s.shape), s.dtype) for s in shards]
    return _comm_call(body, shards, out_shape, name)


HBM_SPEC = pl.BlockSpec(memory_space=pltpu.HBM)
SEM_SPEC = pl.BlockSpec(memory_space=pltpu.SEMAPHORE)
EFFECT = pltpu.SideEffectType.DATAFLOW_SIDE_EFFECTING
PEER_ORDER = (4, 2, 6, 5, 3, 7, 1)


def _peer_copies(src_refs, land_refs, send_sems, recv_sems, gather):
    x, y, c = _mesh_pos()
    me = 4 * x + 2 * y + c
    copies = []
    for k in PEER_ORDER:
        px = 1 - x if k & 4 else x
        py = 1 - y if k & 2 else y
        pc = 1 - c if k & 1 else c
        for w, (src, land) in enumerate(zip(src_refs, land_refs)):
            copies.append(pltpu.make_async_remote_copy(
                src_ref=src if gather else src.at[4 * px + 2 * py + pc],
                dst_ref=land.at[me] if gather else land.at[k - 1],
                send_sem=send_sems.at[w * N_COPIES + k - 1], recv_sem=recv_sems.at[w * N_COPIES + k - 1],
                device_id=(px, py, pc), device_id_type=MESH))
    return copies


def _all_gather_small(shard, name):
    def body(x_ref, out_ref, send_sems, recv_sems):
        x, y, c = _mesh_pos()
        out_ref[4 * x + 2 * y + c] = x_ref[...]
        copies = _peer_copies([x_ref], [out_ref], send_sems, recv_sems, True)
        for cp in copies:
            cp.start()
        for cp in copies:
            cp.wait()

    vmem = pl.BlockSpec(memory_space=pltpu.VMEM)
    return pl.pallas_call(
        body, name=name, out_shape=jax.ShapeDtypeStruct((N_DEV, *shard.shape), shard.dtype), in_specs=[vmem],
        out_specs=vmem, scratch_shapes=[pltpu.SemaphoreType.DMA((N_COPIES,)), pltpu.SemaphoreType.DMA((N_COPIES,))],
    )(shard)


def _copies_start(srcs, *, gather, after, name):
    n = len(srcs)
    lands = [lax.empty(((N_DEV,) + s.shape) if gather else ((N_COPIES,) + s.shape[1:]), s.dtype) for s in srcs]

    def body(*refs):
        src_refs, land_refs = refs[:n], refs[n:2 * n]
        send_sems, recv_sems = refs[-2 * n - 3], refs[-2 * n - 2]
        for cp in _peer_copies(src_refs, land_refs, send_sems, recv_sems, gather):
            cp.start()
        refs[-1][...] = jnp.zeros(refs[-1].shape, F32)

    ins = [pltpu.with_memory_space_constraint(a, pltpu.HBM) for a in srcs + lands]
    in_specs = [HBM_SPEC] * (2 * n)
    if after is not None:
        ins.append(after)
        in_specs.append(pl.BlockSpec(memory_space=pl.ANY))
    sems = pltpu.SemaphoreType.DMA((n * N_COPIES,))
    out = pl.pallas_call(
        body, name=name,
        out_shape=(sems, sems, *[pltpu.HBM(a.shape, a.dtype) for a in srcs + lands], jax.ShapeDtypeStruct((8, LANES), F32)),
        in_specs=in_specs,
        out_specs=(SEM_SPEC, SEM_SPEC, *[HBM_SPEC] * (2 * n), pl.BlockSpec(memory_space=pltpu.VMEM)),
        input_output_aliases={i: 2 + i for i in range(2 * n)},
        compiler_params=pltpu.CompilerParams(has_side_effects=EFFECT),
    )(*ins)
    return out[0], out[1], list(out[2:2 + n]), list(out[2 + n:2 + 2 * n]), out[-1]


def _copies_wait(started, *, gather, after, name):
    send_sems, recv_sems, srcs, lands, _ = started
    n = len(srcs)

    def body(*refs):
        src_refs, land_refs = refs[:n], refs[n:2 * n]
        for cp in _peer_copies(src_refs, land_refs, refs[2 * n], refs[2 * n + 1], gather):
            cp.wait_send()
            cp.wait_recv()

    out = pl.pallas_call(
        body, name=name, out_shape=tuple(pltpu.HBM(a.shape, a.dtype) for a in srcs + lands),
        in_specs=[HBM_SPEC] * (2 * n) + [SEM_SPEC, SEM_SPEC, pl.BlockSpec(memory_space=pl.ANY)],
        out_specs=tuple([HBM_SPEC] * (2 * n)), input_output_aliases={i: i for i in range(2 * n)},
        compiler_params=pltpu.CompilerParams(has_side_effects=EFFECT),
    )(*srcs, *lands, send_sems, recv_sems, after)
    return list(out[n:])


def _adamw(got, w, m, v, *, name, tile, own=None, layer=None, into=None):
    rows, width = w.shape[-2:]
    n_got = got.shape[0]
    c1 = 1.0 / (1.0 - ADAM_B1 ** ADAM_STEP)
    c2 = 1.0 / (1.0 - ADAM_B2 ** ADAM_STEP)

    def body(*refs):
        got_ref, w_ref, m_ref, v_ref = refs[:4]
        g_ref, d_ref, nm_ref, nv_ref = refs[-4:]
        g = got_ref[0] if own is None else refs[4][...] + got_ref[0]
        for s in range(1, n_got):
            g = g + got_ref[s]
        m_new = ADAM_B1 * m_ref[...] + (1.0 - ADAM_B1) * g
        v_new = ADAM_B2 * v_ref[...] + (1.0 - ADAM_B2) * (g * g)
        g_ref[...] = g
        nm_ref[...] = m_new
        nv_ref[...] = v_new
        d_ref[...] = -ADAM_LR * ((m_new * c1) / (jnp.sqrt(v_new * c2) + ADAM_EPS) + ADAM_WD * w_ref[...])

    spec = pl.BlockSpec((tile, width), lambda i: (i, 0))
    wspec = spec if layer is None else pl.BlockSpec((None, tile, width), lambda i: (layer, i, 0))
    args = [got, w, m, v] + ([] if own is None else [own])
    in_specs = [pl.BlockSpec((n_got, tile, width), lambda i: (0, i, 0))] + [wspec] * 3 + [spec] * (len(args) - 4)
    aliases = {}
    if into is not None:
        aliases = {len(args) + i: i for i in range(4)}
        args += list(into)
        in_specs += [pl.BlockSpec(memory_space=pl.ANY)] * 4
    return pl.pallas_call(
        body, name=name, grid=(rows // tile,), in_specs=in_specs, out_specs=[wspec] * 4,
        out_shape=[jax.ShapeDtypeStruct(w.shape, F32)] * 4, input_output_aliases=aliases,
        compiler_params=_cparams("parallel"),
    )(*args)


SHARDED = dict(a_w_in=(2, True), a_w_out=(1, True), b_w_in=(2, True), b_w_out=(1, True), ffn_w_in=(2, True),
               ffn_w_out=(1, True), b_conv_w=(2, False), b_conv_b=(1, False), b_norm_w=(1, False), ffn_conv_w=(2, False))
REPLICATED = ("norm1_w", "norm2_w", "a_lb_logits", "a_norm_w", "b_dt_bias", "b_a_log", "b_d_skip", "ffn_conv_b",
              "final_norm_w")
WEIGHTS = ("norm1_w", "norm2_w", "a_w_in", "a_lb_logits", "a_norm_w", "a_w_out", "b_w_in", "b_conv_w", "b_conv_b",
           "b_dt_bias", "b_a_log", "b_d_skip", "b_norm_w", "b_w_out", "ffn_w_in", "ffn_conv_w", "ffn_conv_b",
           "ffn_w_out", "final_norm_w")


def _pad_rows(flat, multiple):
    n = flat.shape[-1]
    per = PACK_W * multiple
    total = -(-n // per) * per
    flat = jnp.pad(flat, [(0, 0)] * (flat.ndim - 1) + [(0, total - n)])
    return flat.reshape(*flat.shape[:-1], total // PACK_W, PACK_W)


def _to_parts(full, axis, n=N_DEV):
    shp = full.shape
    t = full.reshape(*shp[:axis], n, shp[axis] // n, *shp[axis + 1:])
    return jnp.moveaxis(t, axis, 0)


def _from_parts(parts, axis):
    t = jnp.moveaxis(parts, 0, axis)
    shp = t.shape
    return t.reshape(*shp[:axis], shp[axis] * shp[axis + 1], *shp[axis + 2:])


BIG = tuple(n for n, (_, mm) in SHARDED.items() if mm)
SMALL = tuple(n for n, (_, mm) in SHARDED.items() if not mm)
SMALL_W = 512


def _small_rows(tree, lead):
    rows = []
    for n in SMALL:
        t = tree[n]
        t = t.reshape(*lead, -1, t.shape[-1])
        rows.append(jnp.pad(t, [(0, 0)] * (t.ndim - 1) + [(0, SMALL_W - t.shape[-1])]))
    buf = jnp.concatenate(rows, axis=-2)
    return jnp.pad(buf, [(0, 0)] * (buf.ndim - 2) + [(0, 16 - buf.shape[-2]), (0, 0)])


def _small_unrows(buf, like, lead):
    out, r = {}, 0
    for n in SMALL:
        shp = like[n].shape
        k = like[n].size // shp[-1]
        out[n] = buf[..., r:r + k, :shp[-1]].reshape(*lead, *shp)
        r += k
    return out


GROUPS = dict(hg=(("a_w_in", 0), ("a_w_out", 0)), f0=(("ffn_w_in", 0), ("ffn_w_out", 0)),
              mb=(("b_w_in", 0), ("b_w_out", 0)), f1=(("ffn_w_in", 1), ("ffn_w_out", 1)))


class _Comm:
    def __init__(self, local):
        x, y, c = _mesh_pos()
        self.me = 4 * x + 2 * y + c
        self.local = local
        self.shards = {g: [local[n][i].astype(BF16) for n, i in names] for g, names in GROUPS.items()}
        self.shards["f0"].append(_small_rows(local, ()))
        self.first = _all_gather(self.shards["hg"], "gather_hg")
        self.gathers, self.sent, token = {}, {}, None
        for g in ("f0", "mb", "f1"):
            self.gathers[g] = _copies_start(self.shards[g], gather=True, after=token, name=f"gather_{g}_start")
            token = self.gathers[g][-1]
        self.token = token

    def weights(self, group, after):
        if group == "hg":
            got = self.first
        else:
            lands = _copies_wait(self.gathers[group], gather=True, after=after, name=f"gather_{group}_wait")
            got = [lax.dynamic_update_index_in_dim(land, shard, self.me, 0)
                   for land, shard in zip(lands, self.shards[group])]
        out = dict(w_in=_from_parts(got[0], 1), w_out=_from_parts(got[1], 0))
        if group == "hg":
            out["token"] = self.token
        if group == "f0":
            small = _small_unrows(got[2], self.local, (N_DEV,))
            out["small"] = {n: _from_parts(small[n], SHARDED[n][0]) for n in SMALL}
        return out

    def send(self, group, grads):
        w_in = grads["w_in"]
        if isinstance(w_in, tuple):
            half = N_DEV // len(w_in)
            parts_in = jnp.concatenate([_to_parts(t, 1, half) for t in w_in], axis=0)
        else:
            parts_in = w_in if w_in.ndim == 3 else _to_parts(w_in, 1)
        parts = [parts_in, _to_parts(grads["w_out"], 0)]
        if "small" in grads:
            parts.append(_small_rows({n: _to_parts(grads["small"][n], SHARDED[n][0]) for n in SMALL}, (N_DEV,)))
        sent = [p.astype(BF16) for p in parts[:2]] + parts[2:] if group == "hg" else parts
        self.sent[group] = (parts, _copies_start(sent, gather=False, after=None, name=f"exchange_{group}_start"))
        return self.sent[group][1][-1]

    def finish(self, after, mom, var):
        res = {}
        for group in ("f1", "mb", "f0", "hg"):
            parts, started = self.sent[group]
            lands = _copies_wait(started, gather=False, after=after, name=f"exchange_{group}_wait")
            own = [lax.dynamic_index_in_dim(p, self.me, 0, keepdims=False) for p in parts]
            for (n, i), got, mine in zip(GROUPS[group], lands, own):
                res[n] = _adamw(got, self.local[n], mom[n], var[n], own=mine, layer=i, into=res.get(n),
                                name=f"adamw_{n}_{i}", tile=_pick(mine.shape[0], (256, 176, 128)))
                after = res[n][0]
        small = _adamw(lands[2], *[_small_rows(t, ()) for t in (self.local, mom, var)], own=own[2],
                       name="adamw_small", tile=16)
        return res, small


def _pack_small(tree, extra):
    flat = jnp.concatenate([tree[n].reshape(-1) for n in REPLICATED] + [extra.reshape(-1)])
    return _pad_rows(flat, 8)


def _unpack_small(pack, like):
    flat, out, off = pack.reshape(-1), {}, 0
    for n in REPLICATED:
        out[n] = flat[off:off + like[n].size].reshape(like[n].shape)
        off += like[n].size
    return out, flat[off]


def kernel(x, norm1_w, norm2_w, a_w_in, a_lb_logits, a_norm_w, a_w_out, b_w_in, b_conv_w, b_conv_b, b_dt_bias, b_a_log, b_d_skip, b_norm_w, b_w_out, ffn_w_in, ffn_conv_w, ffn_conv_b, ffn_w_out, final_norm_w, loss_target, m_norm1_w, m_norm2_w, m_a_w_in, m_a_lb_logits, m_a_norm_w, m_a_w_out, m_b_w_in, m_b_conv_w, m_b_conv_b, m_b_dt_bias, m_b_a_log, m_b_d_skip, m_b_norm_w, m_b_w_out, m_ffn_w_in, m_ffn_conv_w, m_ffn_conv_b, m_ffn_w_out, m_final_norm_w, v_norm1_w, v_norm2_w, v_a_w_in, v_a_lb_logits, v_a_norm_w, v_a_w_out, v_b_w_in, v_b_conv_w, v_b_conv_b, v_b_dt_bias, v_b_a_log, v_b_d_skip, v_b_norm_w, v_b_w_out, v_ffn_w_in, v_ffn_conv_w, v_ffn_conv_b, v_ffn_w_out, v_final_norm_w):
    given = dict(locals())
    local = {n: given[n] for n in WEIGHTS}
    mom = {n: given["m_" + n] for n in WEIGHTS}
    var = {n: given["v_" + n] for n in WEIGHTS}

    comm = _Comm(local)
    loss, grad_x, grads = _local_step(x[0], loss_target[0], local, comm)

    res, small_res = comm.finish(grad_x, mom, var)
    outs = ({}, {}, {}, {})
    for n in BIG:
        for out, r in zip(outs, res[n]):
            out[n] = r
    for out, r in zip(outs, small_res):
        out.update(_small_unrows(r, local, ()))
    out_g, out_d, out_m, out_v = outs

    small = _pack_small(grads, loss)
    rows = small.shape[0]
    got_s = _all_gather_small(small, "gather_small")
    zero = jnp.zeros((1,), F32)
    g, dlt, nm, nv = _adamw(got_s, _pack_small(local, zero), _pack_small(mom, zero), _pack_small(var, zero),
                            name="adamw_replicated", tile=rows)
    (rep_g, total), (rep_d, _), (rep_m, _), (rep_v, _) = (_unpack_small(t, local) for t in (g, dlt, nm, nv))
    out_g.update(rep_g)
    out_d.update(rep_d)
    out_m.update(rep_m)
    out_v.update(rep_v)

    return (total, grad_x[None], *[out_g[n] for n in WEIGHTS], *[out_d[n] for n in WEIGHTS],
            *[out_m[n] for n in WEIGHTS], *[out_v[n] for n in WEIGHTS])
```

```python
import functools

import jax
import jax.numpy as jnp
from jax import lax
from jax.experimental import pallas as pl
from jax.experimental.pallas import tpu as pltpu

F32 = jnp.float32
BF16 = jnp.bfloat16
MESH = pl.DeviceIdType.MESH

N_DEV = 8
EPS = 1e-6
D_MODEL = 1024
HG_HEADS = 8
HG_HEAD_DIM = 128
D_INNER = 2048
SSM_HEAD_DIM = 64
SSM_GROUPS = 8
SSM_HPG = 4
SSM_STATE = 128
GN = SSM_GROUPS * SSM_STATE
CONV_DIM = D_INNER + 2 * GN
D_FF = 2816
SSM_CONV = 5
FFN_CONV = 3

ADAM_LR = 0.001
ADAM_B1 = 0.9
ADAM_B2 = 0.999
ADAM_EPS = 1e-08
ADAM_WD = 0.01
ADAM_STEP = 10

LANES = 128
ROW_TILE = 256
COL_TILE = 128
GLA_CHUNK = 16
GLA_BLOCK = 256
GLA_HEADS_PER_STEP = 8
SSD_CHUNK = 128
SSD_FWD_GROUPS_PER_STEP = 4
SSD_BWD_GROUPS_PER_STEP = 1
SSD_BLOCK = 512
PACK_W = 1024
VMEM_LIMIT = 56 * 1024 * 1024

NT_DIMS = (((1,), (1,)), ((), ()))
TN_DIMS = (((0,), (0,)), ((), ()))


def _cparams(*sem):
    return pltpu.CompilerParams(dimension_semantics=sem, vmem_limit_bytes=VMEM_LIMIT)


def _rms(x, w):
    return x * lax.rsqrt(jnp.mean(x * x, axis=-1, keepdims=True) + EPS) * w


def _row_kernel(body_fn, rows, params, row_outs, acc_outs, *, name, tile=ROW_TILE):
    L = rows[0][0].shape[0]
    tile = min(tile, L)
    n_in = len(rows) + len(params)
    n_ro = len(row_outs)

    def body(*refs):
        outs = body_fn(*[r[...] for r in refs[:n_in]])
        for ref, o in zip(refs[n_in:n_in + n_ro], outs[:n_ro]):
            ref[...] = o.astype(ref.dtype)
        first = pl.program_id(0) == 0
        for ref, o in zip(refs[n_in + n_ro:], outs[n_ro:]):
            @pl.when(first)
            def _(ref=ref):
                ref[...] = jnp.zeros(ref.shape, ref.dtype)
            ref[...] += o

    in_specs = [pl.BlockSpec((tile, w), lambda i, cb=cb: (i, cb)) for _, w, cb in rows]
    in_specs += [pl.BlockSpec(p.shape, lambda i: (0, 0)) for p in params]
    out_specs = [pl.BlockSpec((tile, w), lambda i: (i, 0)) for w, _ in row_outs]
    out_specs += [pl.BlockSpec(s, lambda i: (0, 0)) for s in acc_outs]
    out_shape = [jax.ShapeDtypeStruct((L, w), dt) for w, dt in row_outs]
    out_shape += [jax.ShapeDtypeStruct(s, F32) for s in acc_outs]
    return pl.pallas_call(
        body, name=name, grid=(L // tile,), in_specs=in_specs, out_specs=out_specs, out_shape=out_shape,
        compiler_params=_cparams("arbitrary" if acc_outs else "parallel"),
    )(*[a for a, _, _ in rows], *params)


def _col_kernel(body_fn, cols, params, col_outs, par_outs, *, name, n_tiles):
    L = cols[0][0].shape[0]
    n_in = len(cols) + len(params)
    width = n_tiles * COL_TILE

    def body(*refs):
        outs = body_fn(*[r[...] for r in refs[:n_in]])
        for ref, o in zip(refs[n_in:], outs):
            ref[...] = o.astype(ref.dtype)

    in_specs = [pl.BlockSpec((L, COL_TILE), lambda j, cb=cb: (0, cb(j) if callable(cb) else cb + j)) for _, cb in cols]
    in_specs += [pl.BlockSpec((p.shape[0], COL_TILE), lambda j, cb=cb: (0, cb + j)) for p, cb in params]
    out_specs = [pl.BlockSpec((L, COL_TILE), lambda j: (0, j)) for _ in col_outs]
    out_specs += [pl.BlockSpec((k, COL_TILE), lambda j: (0, j)) for k in par_outs]
    out_shape = [jax.ShapeDtypeStruct((L, width), dt) for dt in col_outs]
    out_shape += [jax.ShapeDtypeStruct((k, width), F32) for k in par_outs]
    return pl.pallas_call(
        body, name=name, grid=(n_tiles,), in_specs=in_specs, out_specs=out_specs, out_shape=out_shape,
        compiler_params=_cparams("parallel"),
    )(*[a for a, _ in cols], *[p for p, _ in params])


def _pick(n, options):
    for t in options:
        if n % t == 0:
            return t
    return n


MATMUL_VMEM = 40 * 1024 * 1024


def _matmul_tiles(M, N, K, out_bytes):
    best = None
    for tm in (1024, 512, 256, 128, M):
        for tn in (1408, 1024, 512, 256, 128, N):
            if M % tm or N % tn:
                continue
            if 2 * (2 * K * (tm + tn) + out_bytes * tm * tn) > MATMUL_VMEM:
                continue
            if best is None or tm * tn > best[0] * best[1]:
                best = (tm, tn)
    return best


def _matmul(a, b, *, name, nt=False, ta=False, add=None, out_dtype=F32, after=None, b_cols=None, col_parts=None):
    K, M = a.shape[::-1] if not ta else a.shape
    cb, width = b_cols if b_cols is not None else (0, b.shape[1])
    N = b.shape[0] if nt else width
    assert width == K or not nt
    tm, tn = _matmul_tiles(M, N, K, 4 * (1 + (add is not None)) if out_dtype == F32 else 2 + 4 * (add is not None))
    if col_parts is not None:
        tn = N // col_parts
        assert tn % LANES == 0 and add is None
    col0 = cb if nt else cb * (width // tn)

    def body(*refs):
        a_ref, b_ref = refs[0], refs[1]
        o_ref = refs[-1]
        dims = TN_DIMS if ta else NT_DIMS if nt else (((1,), (0,)), ((), ()))
        acc = lax.dot_general(a_ref[...], b_ref[...], dims, preferred_element_type=F32)
        if add is not None:
            acc = acc + refs[2][...]
        o_ref[...] = acc.astype(o_ref.dtype)

    in_specs = [pl.BlockSpec((K, tm), lambda i, j: (0, i)) if ta else pl.BlockSpec((tm, K), lambda i, j: (i, 0)),
                pl.BlockSpec((tn, K), lambda i, j: (j, col0)) if nt
                else pl.BlockSpec((K, tn), lambda i, j: (0, col0 + j))]
    args = [a, b]
    if add is not None:
        in_specs.append(pl.BlockSpec((tm, tn), lambda i, j: (i, j)))
        args.append(add)
    if after is not None:
        in_specs.append(pl.BlockSpec(memory_space=pl.ANY))
        args.append(after)
    if col_parts is None:
        out_spec, out_shape = pl.BlockSpec((tm, tn), lambda i, j: (i, j)), (M, N)
    else:
        out_spec, out_shape = pl.BlockSpec((None, tm, tn), lambda i, j: (j, i, 0)), (col_parts, M, tn)
    return pl.pallas_call(
        body, name=name, grid=(M // tm, N // tn), in_specs=in_specs, out_specs=out_spec,
        out_shape=jax.ShapeDtypeStruct(out_shape, out_dtype), compiler_params=_cparams("parallel", "parallel"),
    )(*args)


def _hgrn2_pre_fn(q_raw, ffw_raw, fbw_raw, lb_logits):
    lb = jax.nn.softmax(lb_logits, axis=0)[0:1]

    def gate(fr):
        f = lb + (1.0 - lb) * jax.nn.sigmoid(fr)
        return jnp.log(f), 1.0 - f

    lf_fw, k_fw = gate(ffw_raw)
    lf_bw, k_bw = gate(fbw_raw)
    return jax.nn.silu(q_raw), lf_fw, k_fw, lf_bw, k_bw


def _hgrn2_post_fn(o, g, norm_w):
    outs = []
    for h in range(HG_HEADS):
        sl = slice(h * HG_HEAD_DIM, (h + 1) * HG_HEAD_DIM)
        outs.append(_rms(o[:, sl], norm_w) * jax.nn.silu(g[:, sl]))
    return jnp.concatenate(outs, axis=1)


def _mamba_post_fn(y, z, norm_w):
    outs = []
    gw = D_INNER // SSM_GROUPS
    for gi in range(SSM_GROUPS):
        sl = slice(gi * gw, (gi + 1) * gw)
        outs.append(_rms(y[:, sl] * jax.nn.silu(z[:, sl]), norm_w[:, sl]))
    return jnp.concatenate(outs, axis=1)


def _shift_rows_impl(x, d):
    if d == 0:
        return x
    n, edge = x.shape[0], 8
    t = lax.broadcasted_iota(jnp.int32, (edge, x.shape[1]), 0)
    rolled = pltpu.roll(x, d % n, 0)
    if d > 0:
        return jnp.concatenate([jnp.where(t >= d, rolled[:edge], 0.0), rolled[edge:]], axis=0)
    return jnp.concatenate([rolled[:n - edge], jnp.where(t < edge + d, rolled[n - edge:], 0.0)], axis=0)


@functools.partial(jax.custom_vjp, nondiff_argnums=(1,))
def _shift_rows(x, d):
    return _shift_rows_impl(x, d)


_shift_rows.defvjp(lambda x, d: (_shift_rows_impl(x, d), None), lambda d, _, g: (_shift_rows_impl(g, -d),))


def _dwconv(x, w, b):
    taps = w.shape[0]
    c = (taps - 1) // 2
    y = b + w[0:1, :] * _shift_rows(x, c)
    for k in range(1, taps):
        y = y + w[k:k + 1, :] * _shift_rows(x, c - k)
    return y


def _ffn_mid_fn(gate, val, w, b):
    return jax.nn.silu(_dwconv(gate, w, b)) * val


def _mamba_conv_fn(xbc, w, b):
    return jax.nn.silu(_dwconv(xbc, w, b))


def _gla_consts(rev):
    return lax.broadcasted_iota(jnp.int32, (GLA_CHUNK, HG_HEAD_DIM), 0)


def _segment_cumsum(x, seg, rev):
    n = x.shape[0]
    t = lax.broadcasted_iota(jnp.int32, x.shape, 0) & (seg - 1)
    s = 1
    while s < seg:
        if rev:
            x = x + jnp.where(t < seg - s, pltpu.roll(x, n - s, 0), 0.0)
        else:
            x = x + jnp.where(t >= s, pltpu.roll(x, s, 0), 0.0)
        s *= 2
    return x


def _segment_cumsum_mxu(x, seg, rev):
    r = lax.broadcasted_iota(jnp.int32, (seg, seg), 0)
    c = lax.broadcasted_iota(jnp.int32, (seg, seg), 1)
    tri = ((r <= c) if rev else (r >= c)).astype(BF16)
    tri3 = jnp.concatenate([tri, tri, tri], axis=1)
    hi = x.astype(BF16)
    rest = x - hi.astype(F32)
    mid = rest.astype(BF16)
    lo = (rest - mid.astype(F32)).astype(BF16)
    outs = []
    for g in range(x.shape[0] // seg):
        rows = slice(g * seg, (g + 1) * seg)
        terms = jnp.concatenate([hi[rows], mid[rows], lo[rows]], axis=0)
        outs.append(jnp.dot(tri3, terms, preferred_element_type=F32))
    return jnp.concatenate(outs, axis=0)


@functools.partial(jax.custom_vjp, nondiff_argnums=(1, 2))
def _segment_cumsum_diff(x, seg, rev):
    return _segment_cumsum_mxu(x, seg, rev)


_segment_cumsum_diff.defvjp(lambda x, seg, rev: (_segment_cumsum_mxu(x, seg, rev), None),
                            lambda seg, rev, _, g: (_segment_cumsum_mxu(g, seg, not rev),))


def _gla_chunk(st, q, k, v, b, *, rev, consts, halves):
    row = consts
    c = q.shape[0]
    half = c // 2 if halves else c
    o = lax.dot_general((q * jnp.exp(b)).astype(BF16), st.astype(BF16), NT_DIMS, preferred_element_type=F32)
    spans = [slice(i, i + half) for i in range(0, c, half)]
    qs, bs, rows, os = [q[p] for p in spans], [b[p] for p in spans], [row[p] for p in spans], [o[p] for p in spans]

    def reach(s, h, masked):
        diff = bs[h] - b[s:s + 1]
        if masked:
            diff = jnp.where((rows[h] <= s) if rev else (rows[h] >= s), diff, -jnp.inf)
        return jnp.sum(qs[h] * (k[s:s + 1] * jnp.exp(diff)), axis=-1, keepdims=True) * v[s:s + 1]

    for s in range(c):
        own = s // half
        os[own] = os[own] + reach(s, own, True)
        if halves and own == (1 if rev else 0):
            os[1 - own] = os[1 - own] + reach(s, 1 - own, False)
    o = jnp.concatenate(os, axis=0) if halves else os[0]
    b_end = b[0:1] if rev else b[c - 1:c]
    kd = (k * jnp.exp(b_end - b)).astype(BF16)
    st_new = st * jnp.exp(b_end) + lax.dot_general(v.astype(BF16), kd, TN_DIMS, preferred_element_type=F32)
    return st_new, o


def _gla_fwd(q, k, v, v_cb, g, *, rev, add, name):
    L = q.shape[0]
    blk = min(GLA_BLOCK, L)
    nblk, nsub = L // blk, blk // GLA_CHUNK
    hd, hps = HG_HEAD_DIM, GLA_HEADS_PER_STEP
    wide = hd * hps

    def body(*refs):
        q_ref, k_ref, v_ref, g_ref = refs[:4]
        add_ref = refs[4] if add is not None else None
        o_ref, st_out, st_scr, b_scr = refs[-4:]
        consts = _gla_consts(rev)

        @pl.when(pl.program_id(1) == 0)
        def _():
            st_scr[...] = jnp.zeros(st_scr.shape, F32)

        b_scr[...] = _segment_cumsum(g_ref[...], GLA_CHUNK, rev)

        def step(i, carry):
            sl = pl.ds(pl.multiple_of((nsub - 1 - i if rev else i) * GLA_CHUNK, GLA_CHUNK), GLA_CHUNK)
            lanes = [slice(hi * hd, (hi + 1) * hd) for hi in range(hps)]
            ins = [(st_scr[hi], q_ref[sl, ln], k_ref[sl, ln], v_ref[sl, ln], b_scr[sl, ln])
                   for hi, ln in enumerate(lanes)]
            adds = [add_ref[sl, ln] for ln in lanes] if add_ref is not None else None
            outs = [_gla_chunk(*args, rev=rev, consts=consts, halves=True) for args in ins]
            for hi, ln in enumerate(lanes):
                st_out[hi, i] = ins[hi][0]
                st_scr[hi] = outs[hi][0]
                o_ref[sl, ln] = outs[hi][1] if adds is None else outs[hi][1] + adds[hi]
            return carry

        lax.fori_loop(0, nsub, step, 0)

    def pos(j):
        return nblk - 1 - j if rev else j

    spec = pl.BlockSpec((blk, wide), lambda h, j: (pos(j), h))
    in_specs = [spec, spec, pl.BlockSpec((blk, wide), lambda h, j: (pos(j), v_cb // hps + h)), spec]
    args = [q, k, v, g]
    if add is not None:
        in_specs.append(spec)
        args.append(add)
    return pl.pallas_call(
        body, name=name, grid=(HG_HEADS // hps, nblk), in_specs=in_specs,
        out_specs=[spec, pl.BlockSpec((hps, None, nsub, hd, hd), lambda h, j: (h, j, 0, 0, 0))],
        out_shape=[jax.ShapeDtypeStruct((L, HG_HEADS * hd), F32),
                   jax.ShapeDtypeStruct((HG_HEADS, nblk, nsub, hd, hd), F32)],
        scratch_shapes=[pltpu.VMEM((hps, hd, hd), F32), pltpu.VMEM((blk, wide), F32)],
        compiler_params=_cparams("parallel", "arbitrary"),
    )(*args)


def _gla_bwd(q, k, v, v_cb, g, do, states, *, rev, adds, name):
    L = q.shape[0]
    blk = min(GLA_BLOCK, L)
    nblk, nsub = L // blk, blk // GLA_CHUNK
    hd, hps = HG_HEAD_DIM, GLA_HEADS_PER_STEP
    wide = hd * hps
    n_add = 0 if adds is None else 2

    def body(*refs):
        q_ref, k_ref, v_ref, g_ref, do_ref, st_in = refs[:6]
        add_refs = refs[6:6 + n_add]
        dq_ref, dk_ref, dv_ref, dg_ref, dst_scr, b_scr, db_scr = refs[6 + n_add:]
        consts = _gla_consts(rev)
        chunk = functools.partial(_gla_chunk, rev=rev, consts=consts, halves=False)

        @pl.when(pl.program_id(1) == 0)
        def _():
            dst_scr[...] = jnp.zeros(dst_scr.shape, F32)

        b_scr[...] = _segment_cumsum(g_ref[...], GLA_CHUNK, rev)

        def rows(i):
            return pl.ds(pl.multiple_of((nsub - 1 - i if rev else i) * GLA_CHUNK, GLA_CHUNK), GLA_CHUNK)

        def step(n, carry):
            i = nsub - 1 - n
            sl = rows(i)
            lanes = [slice(hi * hd, (hi + 1) * hd) for hi in range(hps)]
            ins = [(st_in[hi, i], q_ref[sl, ln], k_ref[sl, ln], v_ref[sl, ln], b_scr[sl, ln])
                   for hi, ln in enumerate(lanes)]
            cts = [(dst_scr[hi], do_ref[sl, ln]) for hi, ln in enumerate(lanes)]
            adds = [(add_refs[0][sl, ln], add_refs[1][sl, ln]) for ln in lanes] if n_add else None
            outs = [jax.vjp(chunk, *args)[1](ct) for args, ct in zip(ins, cts)]
            for hi, ln in enumerate(lanes):
                dst, dq, dk, dv, db = outs[hi]
                dst_scr[hi] = dst
                dq_ref[sl, ln] = dq if adds is None else dq + adds[hi][0]
                dk_ref[sl, ln] = dk
                dv_ref[sl, ln] = dv if adds is None else dv + adds[hi][1]
                db_scr[sl, ln] = db
            return carry

        lax.fori_loop(0, nsub, step, 0)
        dg_ref[...] = _segment_cumsum(db_scr[...], GLA_CHUNK, not rev)

    def pos(j):
        p = nblk - 1 - j
        return nblk - 1 - p if rev else p

    spec = pl.BlockSpec((blk, wide), lambda h, j: (pos(j), h))
    in_specs = [spec, spec, pl.BlockSpec((blk, wide), lambda h, j: (pos(j), v_cb // hps + h)), spec, spec,
                pl.BlockSpec((hps, None, nsub, hd, hd), lambda h, j: (h, nblk - 1 - j, 0, 0, 0))]
    args = [q, k, v, g, do, states]
    if adds is not None:
        in_specs += [spec, spec]
        args += list(adds)
    out = jax.ShapeDtypeStruct((L, HG_HEADS * hd), F32)
    return pl.pallas_call(
        body, name=name, grid=(HG_HEADS // hps, nblk), in_specs=in_specs,
        out_specs=[spec] * 4, out_shape=[out] * 4,
        scratch_shapes=[pltpu.VMEM((hps, hd, hd), F32), pltpu.VMEM((blk, wide), F32), pltpu.VMEM((blk, wide), F32)],
        compiler_params=_cparams("parallel", "arbitrary"),
    )(*args)


def _ssd_consts(rev):
    c = SSD_CHUNK
    gw = SSM_HPG * SSM_HEAD_DIM
    r2 = lax.broadcasted_iota(jnp.int32, (c, c), 0)
    c2 = lax.broadcasted_iota(jnp.int32, (c, c), 1)
    low = (r2 <= c2) if rev else (r2 >= c2)
    lane = lax.broadcasted_iota(jnp.int32, (1, gw), 1)
    return low, r2 == c2, lane, lane >> 6


def _expand_heads(v4, head_of_lane):
    first = head_of_lane[:, :LANES] == 0
    cols = [v4[:, j:j + 1] for j in range(SSM_HPG)]
    return jnp.concatenate([jnp.where(first, cols[0], cols[1]), jnp.where(first, cols[2], cols[3])], axis=1)


def _ssd_prep(dtr, bias, alog, dsk, *, rev, consts):
    head_of_lane = consts[-1]
    dt_l = _expand_heads(jax.nn.softplus(dtr + bias), head_of_lane)
    a_l = _expand_heads(-jnp.exp(alog), head_of_lane)
    return dt_l, _segment_cumsum_diff(dt_l * a_l, SSD_CHUNK, rev), _expand_heads(dsk, head_of_lane)


def _ssd_chunk(st, x, bm, cm, dt_l, acum, dsk_l, *, rev, consts, skip):
    low, eye, lane, head_of_lane = consts
    c = x.shape[0]
    xd = x * dt_l
    cb = lax.dot_general(cm.astype(BF16), bm.astype(BF16), NT_DIMS, preferred_element_type=F32)
    y = jnp.dot(cm.astype(BF16), st.astype(BF16), preferred_element_type=F32) * jnp.exp(acum)
    for j in range(SSM_HPG):
        acol = jnp.sum(jnp.where(lane == j * SSM_HEAD_DIM, acum, 0.0), axis=-1, keepdims=True)
        ab = jnp.broadcast_to(acol, (c, c))
        arow = jnp.sum(jnp.where(eye, ab, 0.0), axis=0, keepdims=True)
        lmat = jnp.exp(jnp.where(low, ab - arow, -jnp.inf))
        xj = jnp.where(head_of_lane == j, xd, 0.0)
        y = y + jnp.dot((cb * lmat).astype(BF16), xj.astype(BF16), preferred_element_type=F32)
    a_end = acum[0:1] if rev else acum[c - 1:c]
    xdec = (xd * jnp.exp(a_end - acum)).astype(BF16)
    st_new = st * jnp.exp(a_end) + lax.dot_general(bm.astype(BF16), xdec, TN_DIMS, preferred_element_type=F32)
    if skip:
        y = y + x * dsk_l
    return st_new, y


def _ssd_specs(L, rev):
    blk = min(SSD_BLOCK, L)
    nblk = L // blk
    gw = SSM_HPG * SSM_HEAD_DIM

    def pos(j):
        return nblk - 1 - j if rev else j

    return blk, nblk, gw, pos


def _ssd_fwd(xbc, dtr, bias, alog, dsk, *, rev, skip, add, name):
    L = xbc.shape[0]
    blk, nblk, gw, pos = _ssd_specs(L, rev)
    nsub = blk // SSD_CHUNK
    n = SSM_STATE

    gps = SSD_FWD_GROUPS_PER_STEP

    def body(*refs):
        x_ref, b_ref, c_ref, dt_ref, bias_ref, alog_ref, dsk_ref = refs[:7]
        add_ref = refs[7] if add is not None else None
        y_ref, st_out, st_scr, dt_scr, ac_scr = refs[-5:]
        consts = _ssd_consts(rev)
        xl = [slice(gi * gw, (gi + 1) * gw) for gi in range(gps)]
        nl = [slice(gi * n, (gi + 1) * n) for gi in range(gps)]

        @pl.when(pl.program_id(1) == 0)
        def _():
            st_scr[...] = jnp.zeros(st_scr.shape, F32)

        dsk_l = []
        for gi in range(gps):
            dt_scr[:, xl[gi]], ac_scr[:, xl[gi]], d = _ssd_prep(dt_ref[gi], bias_ref[gi], alog_ref[gi], dsk_ref[gi],
                                                              rev=rev, consts=consts)
            dsk_l.append(d)

        def step(i, carry):
            sl = pl.ds(pl.multiple_of((nsub - 1 - i if rev else i) * SSD_CHUNK, SSD_CHUNK), SSD_CHUNK)
            ins = [(st_scr[gi], x_ref[sl, xl[gi]], b_ref[sl, nl[gi]], c_ref[sl, nl[gi]], dt_scr[sl, xl[gi]],
                    ac_scr[sl, xl[gi]], dsk_l[gi]) for gi in range(gps)]
            adds = [add_ref[sl, xl[gi]] for gi in range(gps)] if add_ref is not None else None
            outs = [_ssd_chunk(*a, rev=rev, consts=consts, skip=skip) for a in ins]
            for gi in range(gps):
                st_out[gi, i] = ins[gi][0]
                st_scr[gi] = outs[gi][0]
                y_ref[sl, xl[gi]] = outs[gi][1] if adds is None else outs[gi][1] + adds[gi]
            return carry

        lax.fori_loop(0, nsub, step, 0)

    b0 = D_INNER // (n * gps)
    yspec = pl.BlockSpec((blk, gw * gps), lambda g, j: (pos(j), g))
    pspec = pl.BlockSpec((gps, 1, SSM_HPG), lambda g, j: (g, 0, 0))
    in_specs = [yspec,
                pl.BlockSpec((blk, n * gps), lambda g, j: (pos(j), b0 + g)),
                pl.BlockSpec((blk, n * gps), lambda g, j: (pos(j), b0 + SSM_GROUPS // gps + g)),
                pl.BlockSpec((gps, blk, SSM_HPG), lambda g, j: (g, pos(j), 0)),
                pspec, pspec, pspec]
    args = [xbc, xbc, xbc, dtr, bias, alog, dsk]
    if add is not None:
        in_specs.append(yspec)
        args.append(add)
    return pl.pallas_call(
        body, name=name, grid=(SSM_GROUPS // gps, nblk), in_specs=in_specs,
        out_specs=[yspec, pl.BlockSpec((gps, None, nsub, n, gw), lambda g, j: (g, j, 0, 0, 0))],
        out_shape=[jax.ShapeDtypeStruct((L, D_INNER), F32),
                   jax.ShapeDtypeStruct((SSM_GROUPS, nblk, nsub, n, gw), F32)],
        scratch_shapes=[pltpu.VMEM((gps, n, gw), F32), pltpu.VMEM((blk, gw * gps), F32),
                        pltpu.VMEM((blk, gw * gps), F32)],
        compiler_params=_cparams("parallel", "arbitrary"),
    )(*args)


def _ssd_bwd(xbc, dtr, bias, alog, dsk, dy, states, *, rev, skip, adds, name):
    L = xbc.shape[0]
    blk, nblk, gw, _ = _ssd_specs(L, rev)
    nsub = blk // SSD_CHUNK
    n = SSM_STATE
    n_add = 0 if adds is None else 3
    gps = SSD_BWD_GROUPS_PER_STEP

    def body(*refs):
        x_ref, b_ref, c_ref, dt_ref, bias_ref, alog_ref, dsk_ref, dy_ref, st_in = refs[:9]
        add_refs = refs[9:9 + n_add]
        outs = refs[9 + n_add:]
        dx_ref, db_ref, dc_ref, ddt_ref, dbias_ref, dalog_ref, ddsk_ref = outs[:7]
        dst_scr, dt_scr, ac_scr, ddt_scr, dac_scr = outs[7:]
        consts = _ssd_consts(rev)
        chunk = functools.partial(_ssd_chunk, rev=rev, consts=consts, skip=skip)
        prep = functools.partial(_ssd_prep, rev=rev, consts=consts)

        @pl.when(pl.program_id(1) == 0)
        def _():
            dst_scr[...] = jnp.zeros(dst_scr.shape, F32)
            dbias_ref[...] = jnp.zeros(dbias_ref.shape, F32)
            dalog_ref[...] = jnp.zeros(dalog_ref.shape, F32)
            ddsk_ref[...] = jnp.zeros(ddsk_ref.shape, F32)

        xl = [slice(gi * gw, (gi + 1) * gw) for gi in range(gps)]
        nl = [slice(gi * n, (gi + 1) * n) for gi in range(gps)]
        dsk_l, prep_vjp = [], []
        for gi in range(gps):
            narrow = (dt_ref[gi], bias_ref[gi], alog_ref[gi], dsk_ref[gi])
            (dt_scr[:, xl[gi]], ac_scr[:, xl[gi]], d), pv = jax.vjp(prep, *narrow)
            dsk_l.append(d)
            prep_vjp.append(pv)

        def step(k, ddsk_l):
            i = nsub - 1 - k
            sl = pl.ds(pl.multiple_of((nsub - 1 - i if rev else i) * SSD_CHUNK, SSD_CHUNK), SSD_CHUNK)
            ins = [(st_in[gi, i], x_ref[sl, xl[gi]], b_ref[sl, nl[gi]], c_ref[sl, nl[gi]], dt_scr[sl, xl[gi]],
                    ac_scr[sl, xl[gi]], dsk_l[gi]) for gi in range(gps)]
            cts = [(dst_scr[gi], dy_ref[sl, xl[gi]]) for gi in range(gps)]
            more = [(add_refs[0][sl, xl[gi]], add_refs[1][sl, nl[gi]], add_refs[2][sl, nl[gi]])
                    for gi in range(gps)] if n_add else None
            grads = [jax.vjp(chunk, *a)[1](ct) for a, ct in zip(ins, cts)]
            new = []
            for gi in range(gps):
                dst, dx, db, dc, ddt_l, dac, ddsk_k = grads[gi]
                dst_scr[gi] = dst
                dx_ref[sl, xl[gi]] = dx if more is None else dx + more[gi][0]
                db_ref[sl, nl[gi]] = db if more is None else db + more[gi][1]
                dc_ref[sl, nl[gi]] = dc if more is None else dc + more[gi][2]
                ddt_scr[sl, xl[gi]] = ddt_l
                dac_scr[sl, xl[gi]] = dac
                new.append(ddsk_l[gi] + ddsk_k)
            return tuple(new)

        ddsk_l = lax.fori_loop(0, nsub, step, tuple(jnp.zeros((1, gw), F32) for _ in range(gps)))
        for gi in range(gps):
            ddt, dbias, dalog, ddsk = prep_vjp[gi]((ddt_scr[:, xl[gi]], dac_scr[:, xl[gi]], ddsk_l[gi]))
            ddt_ref[gi] = ddt
            dbias_ref[gi] += dbias
            dalog_ref[gi] += dalog
            ddsk_ref[gi] += ddsk

    def pos(j):
        p = nblk - 1 - j
        return nblk - 1 - p if rev else p

    b0 = D_INNER // (n * gps)
    xspec = pl.BlockSpec((blk, gw * gps), lambda g, j: (pos(j), g))
    nspec = pl.BlockSpec((blk, n * gps), lambda g, j: (pos(j), g))
    dtspec = pl.BlockSpec((gps, blk, SSM_HPG), lambda g, j: (g, pos(j), 0))
    pspec = pl.BlockSpec((gps, 1, SSM_HPG), lambda g, j: (g, 0, 0))
    in_specs = [xspec,
                pl.BlockSpec((blk, n * gps), lambda g, j: (pos(j), b0 + g)),
                pl.BlockSpec((blk, n * gps), lambda g, j: (pos(j), b0 + SSM_GROUPS // gps + g)),
                dtspec, pspec, pspec, pspec, xspec,
                pl.BlockSpec((gps, None, nsub, n, gw), lambda g, j: (g, nblk - 1 - j, 0, 0, 0))]
    args = [xbc, xbc, xbc, dtr, bias, alog, dsk, dy, states]
    if adds is not None:
        in_specs += [xspec, nspec, nspec]
        args += list(adds)
    par = jax.ShapeDtypeStruct((SSM_GROUPS, 1, SSM_HPG), F32)
    return pl.pallas_call(
        body, name=name, grid=(SSM_GROUPS // gps, nblk), in_specs=in_specs,
        out_specs=[xspec, nspec, nspec, dtspec, pspec, pspec, pspec],
        out_shape=[jax.ShapeDtypeStruct((L, D_INNER), F32), jax.ShapeDtypeStruct((L, GN), F32),
                   jax.ShapeDtypeStruct((L, GN), F32), jax.ShapeDtypeStruct((SSM_GROUPS, L, SSM_HPG), F32),
                   par, par, par],
        scratch_shapes=[pltpu.VMEM((gps, n, gw), F32)] + [pltpu.VMEM((blk, gw * gps), F32)] * 4,
        compiler_params=_cparams("parallel", "arbitrary"),
    )(*args)


def _out_proj(y, w_out, h, next_norm, name):
    if next_norm is None:
        return _matmul(y, w_out, add=h, name=name), None
    M, K = y.shape
    N = w_out.shape[1]
    tm = 512

    def body(y_ref, w_ref, h_ref, n_ref, o_ref, u_ref):
        acc = jnp.dot(y_ref[...], w_ref[...], preferred_element_type=F32) + h_ref[...]
        o_ref[...] = acc
        u_ref[...] = _rms(acc, n_ref[...]).astype(BF16)

    rows = pl.BlockSpec((tm, N), lambda i: (i, 0))
    return pl.pallas_call(
        body, name=name, grid=(M // tm,),
        in_specs=[pl.BlockSpec((tm, K), lambda i: (i, 0)), pl.BlockSpec((K, N), lambda i: (0, 0)), rows,
                  pl.BlockSpec((1, N), lambda i: (0, 0))],
        out_specs=[rows, rows], out_shape=[jax.ShapeDtypeStruct((M, N), F32), jax.ShapeDtypeStruct((M, N), BF16)],
        compiler_params=_cparams("parallel"),
    )(y, w_out, h, next_norm)


def _norm_fwd(h, w, name):
    d = h.shape[1]
    return _row_kernel(lambda hv, wv: (_rms(hv, wv),), [(h, d, 0)], [w], [(d, BF16)], [], name=name)[0]


def _matmul_norm_bwd(a, b, h, w, dh_in, *, name, add=None, b_cols=None, after=None):
    M, K = a.shape
    N = b.shape[0]
    cb = 0 if b_cols is None else b_cols[0]
    tm = 512 if K <= 4096 else 256

    def body(*refs):
        a_ref, b_ref, h_ref, w_ref, dh_ref = refs[:5]
        o32_ref, o16_ref, dw_ref = refs[-3:]
        du = lax.dot_general(a_ref[...], b_ref[...], NT_DIMS, preferred_element_type=F32)
        if add is not None:
            du = du + refs[5][...]
        _, vjp = jax.vjp(_rms, h_ref[...], w_ref[...])
        dh, dw = vjp(du)
        dh = dh + dh_ref[...]
        o32_ref[...] = dh
        o16_ref[...] = dh.astype(BF16)

        @pl.when(pl.program_id(0) == 0)
        def _():
            dw_ref[...] = jnp.zeros(dw_ref.shape, F32)

        dw_ref[...] += dw

    rows = pl.BlockSpec((tm, N), lambda i: (i, 0))
    in_specs = [pl.BlockSpec((tm, K), lambda i: (i, 0)), pl.BlockSpec((N, K), lambda i: (0, cb)), rows,
                pl.BlockSpec((1, N), lambda i: (0, 0)), rows]
    args = [a, b, h, w, dh_in]
    if add is not None:
        in_specs.append(rows)
        args.append(add)
    if after is not None:
        in_specs.append(pl.BlockSpec(memory_space=pl.ANY))
        args.append(after)
    return pl.pallas_call(
        body, name=name, grid=(M // tm,), in_specs=in_specs,
        out_specs=[rows, rows, pl.BlockSpec((1, N), lambda i: (0, 0))],
        out_shape=[jax.ShapeDtypeStruct((M, N), F32), jax.ShapeDtypeStruct((M, N), BF16),
                   jax.ShapeDtypeStruct((1, N), F32)],
        compiler_params=_cparams("arbitrary"),
    )(*args)


def _ffn_fwd(h, norm_w, w_in, w_out, conv_w, conv_b, tag, u, next_norm=None, loss_with=None):
    pg = _matmul(u, w_in, b_cols=(0, D_FF), name=f"{tag}_in_gate")
    pv = _matmul(u, w_in, b_cols=(1, D_FF), name=f"{tag}_in_val")
    y = _col_kernel(lambda g, v, w, b: (_ffn_mid_fn(g, v, w, b),), [(pg, 0), (pv, 0)], [(conv_w, 0), (conv_b, 0)],
                    [BF16], [], name=f"{tag}_mid", n_tiles=D_FF // COL_TILE)[0]
    if loss_with is not None:
        return _out_proj_loss(y, w_out, h, *loss_with, f"{tag}_out_loss"), (u, pg, pv, y), None
    h_out, u_next = _out_proj(y, w_out, h, next_norm, f"{tag}_out")
    return h_out, (u, pg, pv, y), u_next


def _ffn_bwd(h, dh, dh16, saved, norm_w, w_in, w_out, conv_w, conv_b, tag, after=None):
    u, pg, pv, y = saved
    dy = _matmul(dh16, w_out, nt=True, after=after, name=f"{tag}_out_dx")
    dw_out = _matmul(y, dh16, ta=True, name=f"{tag}_out_dw")

    def fn(g, v, ct, w, b):
        _, vjp = jax.vjp(_ffn_mid_fn, g, v, w, b)
        return vjp(ct)

    dpg, dpv, dcw, dcb = _col_kernel(fn, [(pg, 0), (pv, 0), (dy, 0)], [(conv_w, 0), (conv_b, 0)], [BF16, BF16],
                                     [FFN_CONV, 1], name=f"{tag}_mid_bwd", n_tiles=D_FF // COL_TILE)
    du = _matmul(dpg, w_in, nt=True, b_cols=(0, D_FF), name=f"{tag}_gate_dx")
    dw_gate = _matmul(u, dpg, ta=True, name=f"{tag}_gate_dw")
    dw_val = _matmul(u, dpv, ta=True, name=f"{tag}_val_dw")
    dh, dh16, dnw = _matmul_norm_bwd(dpv, w_in, h, norm_w, dh, add=du, b_cols=(1, D_FF), name=f"{tag}_val_dx_norm")
    return dh, dh16, dict(w_in=(dw_gate, dw_val), w_out=dw_out, conv_w=dcw, conv_b=dcb, norm=dnw)


def _hgrn2_fwd(h, norm_w, w_in, lb_logits, a_norm_w, w_out, after=None, next_norm=None):
    d = D_MODEL
    u = _norm_fwd(h, norm_w, "hg_norm")
    pa = _matmul(u, w_in, after=after, name="hg_in")
    qs, lf_fw, k_fw, lf_bw, k_bw = _row_kernel(
        _hgrn2_pre_fn, [(pa, d, 0), (pa, d, 1), (pa, d, 2)], [lb_logits], [(d, F32)] * 5, [], name="hg_pre",
        tile=2 * ROW_TILE)
    o_fw, st_fw = _gla_fwd(qs, k_fw, pa, 3 * HG_HEADS, lf_fw, rev=False, add=None, name="hg_gla_fw")
    o, st_bw = _gla_fwd(qs, k_bw, pa, 3 * HG_HEADS, lf_bw, rev=True, add=o_fw, name="hg_gla_bw")
    y = _row_kernel(lambda ov, gv, wv: (_hgrn2_post_fn(ov, gv, wv),), [(o, d, 0), (pa, d, 4)], [a_norm_w],
                    [(d, BF16)], [], name="hg_post", tile=2 * ROW_TILE)[0]
    h_out, u_next = _out_proj(y, w_out, h, next_norm, "hg_out")
    return h_out, (u, pa, qs, lf_fw, k_fw, lf_bw, k_bw, st_fw, st_bw, o, y), u_next


def _hgrn2_bwd(h, dh, dh16, saved, norm_w, w_in, lb_logits, a_norm_w, w_out, send, after=None):
    d = D_MODEL
    u, pa, qs, lf_fw, k_fw, lf_bw, k_bw, st_fw, st_bw, o, y = saved
    dy = _matmul(dh16, w_out, nt=True, after=after, name="hg_out_dx")
    dw_out = _matmul(y, dh16, ta=True, name="hg_out_dw")

    def post_bwd(ov, gv, ct, wv):
        _, vjp = jax.vjp(_hgrn2_post_fn, ov, gv, wv)
        return vjp(ct)

    do, dg, dnw = _row_kernel(post_bwd, [(o, d, 0), (pa, d, 4), (dy, d, 0)], [a_norm_w], [(d, F32), (d, F32)],
                              [(1, HG_HEAD_DIM)], name="hg_post_bwd", tile=2 * ROW_TILE)
    dq1, dk_fw, dv1, dlf_fw = _gla_bwd(qs, k_fw, pa, 3 * HG_HEADS, lf_fw, do, st_fw, rev=False, adds=None,
                                       name="hg_gla_fw_bwd")
    dqs, dk_bw, dv, dlf_bw = _gla_bwd(qs, k_bw, pa, 3 * HG_HEADS, lf_bw, do, st_bw, rev=True, adds=(dq1, dv1),
                                      name="hg_gla_bw_bwd")

    def pre_bwd(qr, fr, br, c0, c1, c2, c3, c4, dvv, dgv, lbl):
        _, vjp = jax.vjp(_hgrn2_pre_fn, qr, fr, br, lbl)
        dq, df, db, dlbl = vjp((c0, c1, c2, c3, c4))
        return jnp.concatenate([dq, df, db, dvv, dgv], axis=1), dlbl

    rows = [(pa, d, 0), (pa, d, 1), (pa, d, 2), (dqs, d, 0), (dlf_fw, d, 0), (dk_fw, d, 0), (dlf_bw, d, 0),
            (dk_bw, d, 0), (dv, d, 0), (dg, d, 0)]
    dpa, dlbl = _row_kernel(pre_bwd, rows, [lb_logits], [(5 * d, BF16)], [lb_logits.shape], name="hg_pre_bwd")
    dw_in = _matmul(u, dpa, ta=True, col_parts=N_DEV, name="hg_in_dw")
    token = send(dw_in, dw_out)
    dh, dh16, dn1 = _matmul_norm_bwd(dpa, w_in, h, norm_w, dh, after=token, name="hg_in_dx_norm")
    return dh, dh16, dict(lb=dlbl, a_norm=dnw, norm=dn1)


def _group_params(p):
    return p.reshape(SSM_GROUPS, 1, SSM_HPG)


def _mamba_fwd(h, norm_w, w_z, w_xbc, w_dt, conv_w, conv_b, dt_bias, a_log, d_skip, b_norm_w, w_out, u, next_norm):
    L = h.shape[0]
    z = _matmul(u, w_z, name="mb_in_z")
    xbc_raw = _matmul(u, w_xbc, name="mb_in_xbc")
    dt_raw = _matmul(u, w_dt, name="mb_in_dt")
    xbc = _col_kernel(lambda xv, w, b: (_mamba_conv_fn(xv, w, b),), [(xbc_raw, 0)], [(conv_w, 0), (conv_b, 0)],
                      [F32], [], name="mb_conv", n_tiles=CONV_DIM // COL_TILE)[0]
    dtr = dt_raw.reshape(L, 2, SSM_GROUPS, SSM_HPG).transpose(1, 2, 0, 3)
    bias, alog = dt_bias.reshape(2, -1), a_log.reshape(2, -1)
    dsk = _group_params(d_skip.reshape(-1))
    y_fw, st_fw = _ssd_fwd(xbc, dtr[0], _group_params(bias[0]), _group_params(alog[0]), dsk, rev=False, skip=True,
                           add=None, name="mb_ssd_fw")
    ysum, st_bw = _ssd_fwd(xbc, dtr[1], _group_params(bias[1]), _group_params(alog[1]), dsk, rev=True, skip=False,
                           add=y_fw, name="mb_ssd_bw")
    y = _row_kernel(lambda yv, zv, wv: (_mamba_post_fn(yv, zv, wv),), [(ysum, D_INNER, 0), (z, D_INNER, 0)],
                    [b_norm_w], [(D_INNER, BF16)], [], name="mb_post", tile=2 * ROW_TILE)[0]
    h_out, u_next = _out_proj(y, w_out, h, next_norm, "mb_out")
    return h_out, (u, z, xbc_raw, xbc, dtr, st_fw, st_bw, ysum, y), u_next


def _mamba_bwd(h, dh, dh16, saved, norm_w, w_z, w_xbc, w_dt, conv_w, conv_b, dt_bias, a_log, d_skip, b_norm_w, w_out,
               after=None):
    L = h.shape[0]
    u, z, xbc_raw, xbc, dtr, st_fw, st_bw, ysum, y = saved
    dy = _matmul(dh16, w_out, nt=True, after=after, name="mb_out_dx")
    dw_out = _matmul(y, dh16, ta=True, name="mb_out_dw")

    def post_bwd(yv, zv, ct, wv):
        _, vjp = jax.vjp(_mamba_post_fn, yv, zv, wv)
        return vjp(ct)

    dys, dz, dbn = _row_kernel(post_bwd, [(ysum, D_INNER, 0), (z, D_INNER, 0), (dy, D_INNER, 0)], [b_norm_w],
                               [(D_INNER, F32), (D_INNER, BF16)], [(1, D_INNER)], name="mb_post_bwd", tile=2 * ROW_TILE)
    bias, alog = dt_bias.reshape(2, -1), a_log.reshape(2, -1)
    dsk = _group_params(d_skip.reshape(-1))
    dx1, db1, dc1, ddt_fw, dbias_fw, dalog_fw, ddsk = _ssd_bwd(
        xbc, dtr[0], _group_params(bias[0]), _group_params(alog[0]), dsk, dys, st_fw, rev=False, skip=True,
        adds=None, name="mb_ssd_fw_bwd")
    dx, db, dc, ddt_bw, dbias_bw, dalog_bw, _ = _ssd_bwd(
        xbc, dtr[1], _group_params(bias[1]), _group_params(alog[1]), dsk, dys, st_bw, rev=True, skip=False,
        adds=(dx1, db1, dc1), name="mb_ssd_bw_bwd")

    nx, nb = D_INNER // COL_TILE, GN // COL_TILE

    def conv_bwd(xv, cx, cbm, ccm, w, b):
        j = pl.program_id(0)
        ct = jnp.where(j < nx, cx, jnp.where(j < nx + nb, cbm, ccm))
        _, vjp = jax.vjp(_mamba_conv_fn, xv, w, b)
        return vjp(ct)

    cols = [(xbc_raw, 0), (dx, lambda j: jnp.minimum(j, nx - 1)), (db, lambda j: jnp.clip(j - nx, 0, nb - 1)),
            (dc, lambda j: jnp.clip(j - nx - nb, 0, nb - 1))]
    dxbc, dcw, dcb = _col_kernel(conv_bwd, cols, [(conv_w, 0), (conv_b, 0)], [BF16], [SSM_CONV, 1],
                                 name="mb_conv_bwd", n_tiles=CONV_DIM // COL_TILE)
    ddt = jnp.stack([ddt_fw, ddt_bw]).transpose(2, 0, 1, 3).reshape(L, 2 * SSM_GROUPS * SSM_HPG).astype(BF16)
    du = _matmul(dz, w_z, nt=True, name="mb_z_dx")
    du = _matmul(dxbc, w_xbc, nt=True, add=du, name="mb_xbc_dx")
    dw_in = jnp.concatenate([_matmul(u, dz, ta=True, name="mb_z_dw"), _matmul(u, dxbc, ta=True, name="mb_xbc_dw"),
                             _matmul(u, ddt, ta=True, name="mb_dt_dw")], axis=1)
    dh, dh16, dn1 = _matmul_norm_bwd(ddt, w_dt, h, norm_w, dh, add=du, name="mb_dt_dx_norm")
    grads = dict(w_in=dw_in, w_out=dw_out, conv_w=dcw, conv_b=dcb, b_norm=dbn, norm=dn1,
                 dt_bias=jnp.stack([dbias_fw, dbias_bw]).reshape(1, 2, -1),
                 a_log=jnp.stack([dalog_fw, dalog_bw]).reshape(1, 2, -1), d_skip=ddsk.reshape(1, -1))
    return dh, dh16, grads


def _out_proj_loss(y, w_out, h, target, w, name):
    M, K = y.shape
    N = w_out.shape[1]
    tm = 512

    def body(y_ref, w_ref, h_ref, t_ref, n_ref, dh_ref, dh16_ref, loss_ref, dw_ref):
        out = jnp.dot(y_ref[...], w_ref[...], preferred_element_type=F32) + h_ref[...]
        target = t_ref[...]

        def loss(hv, wv):
            err = _rms(hv, wv) - target
            return 0.5 * jnp.sum(jnp.mean(err * err, axis=-1, keepdims=True), axis=0, keepdims=True)

        val, vjp = jax.vjp(loss, out, n_ref[...])
        dh, dw = vjp(jnp.ones((1, 1), F32))
        dh_ref[...] = dh
        dh16_ref[...] = dh.astype(BF16)

        @pl.when(pl.program_id(0) == 0)
        def _():
            loss_ref[...] = jnp.zeros(loss_ref.shape, F32)
            dw_ref[...] = jnp.zeros(dw_ref.shape, F32)

        loss_ref[...] += val
        dw_ref[...] += dw

    rows = pl.BlockSpec((tm, N), lambda i: (i, 0))
    return pl.pallas_call(
        body, name=name, grid=(M // tm,),
        in_specs=[pl.BlockSpec((tm, K), lambda i: (i, 0)), pl.BlockSpec((K, N), lambda i: (0, 0)), rows, rows,
                  pl.BlockSpec((1, N), lambda i: (0, 0))],
        out_specs=[rows, rows, pl.BlockSpec((1, 1), lambda i: (0, 0)), pl.BlockSpec((1, N), lambda i: (0, 0))],
        out_shape=[jax.ShapeDtypeStruct((M, N), F32), jax.ShapeDtypeStruct((M, N), BF16),
                   jax.ShapeDtypeStruct((1, 1), F32), jax.ShapeDtypeStruct((1, N), F32)],
        compiler_params=_cparams("arbitrary"),
    )(y, w_out, h, target, w)


def _local_step(x, target, rep, comm):
    def ffn_args(i, w, small):
        return (rep["norm2_w"][i:i + 1], w["w_in"], w["w_out"], small["ffn_conv_w"][i], rep["ffn_conv_b"][i:i + 1])

    def mamba_args(w, small):
        w_in = w["w_in"]
        return (rep["norm1_w"][1:2], w_in[:, :D_INNER], w_in[:, D_INNER:D_INNER + CONV_DIM],
                w_in[:, D_INNER + CONV_DIM:], small["b_conv_w"][0], small["b_conv_b"], rep["b_dt_bias"],
                rep["b_a_log"], rep["b_d_skip"], small["b_norm_w"], w["w_out"])

    w_hg = comm.weights("hg", None)
    hg = (rep["norm1_w"][0:1], w_hg["w_in"], rep["a_lb_logits"], rep["a_norm_w"], w_hg["w_out"])
    h0 = x
    h1, s_hg, u1 = _hgrn2_fwd(h0, *hg, after=w_hg.get("token"), next_norm=rep["norm2_w"][0:1])
    w_f0 = comm.weights("f0", h1)
    small = w_f0["small"]
    f0 = ffn_args(0, w_f0, small)
    h2, s_f0, u2 = _ffn_fwd(h1, *f0, "ffn0", u1, next_norm=rep["norm1_w"][1:2])
    mb = mamba_args(comm.weights("mb", h2), small)
    h3, s_mb, u3 = _mamba_fwd(h2, *mb, u2, rep["norm2_w"][1:2])
    f1 = ffn_args(1, comm.weights("f1", h3), small)
    (dh, dh16, loss, d_final), s_f1, _ = _ffn_fwd(h3, *f1, "ffn1", u3,
                                                  loss_with=(target, rep["final_norm_w"].reshape(1, -1)))

    dh, dh16, g_f1 = _ffn_bwd(h3, dh, dh16, s_f1, *f1, "ffn1")
    token = comm.send("f1", dict(w_in=g_f1["w_in"], w_out=g_f1["w_out"]))
    dh, dh16, g_mb = _mamba_bwd(h2, dh, dh16, s_mb, *mb, after=token)
    token = comm.send("mb", dict(w_in=g_mb["w_in"], w_out=g_mb["w_out"]))
    dh, dh16, g_f0 = _ffn_bwd(h1, dh, dh16, s_f0, *f0, "ffn0", after=token)
    token = comm.send("f0", dict(w_in=g_f0["w_in"], w_out=g_f0["w_out"]))
    small_grads = dict(b_conv_w=g_mb["conv_w"][None], b_conv_b=g_mb["conv_b"], b_norm_w=g_mb["b_norm"],
                       ffn_conv_w=jnp.stack([g_f0["conv_w"], g_f1["conv_w"]]))
    dh, dh16, g_hg = _hgrn2_bwd(
        h0, dh, dh16, s_hg, *hg, after=token,
        send=lambda dw_in, dw_out: comm.send("hg", dict(w_in=dw_in, w_out=dw_out, small=small_grads)))
    grads = dict(
        norm1_w=jnp.concatenate([g_hg["norm"], g_mb["norm"]], axis=0),
        norm2_w=jnp.concatenate([g_f0["norm"], g_f1["norm"]], axis=0),
        a_lb_logits=g_hg["lb"], a_norm_w=g_hg["a_norm"], b_dt_bias=g_mb["dt_bias"], b_a_log=g_mb["a_log"],
        b_d_skip=g_mb["d_skip"], ffn_conv_b=jnp.concatenate([g_f0["conv_b"], g_f1["conv_b"]], axis=0),
        final_norm_w=d_final.reshape(-1),
    )
    return loss, dh, grads


def _mesh_pos():
    return lax.axis_index("x"), lax.axis_index("y"), lax.axis_index("c")


N_COPIES = N_DEV - 1


def _comm_call(body, ins, out_shape, name):
    n = len(ins)
    hbm = pl.BlockSpec(memory_space=pl.ANY)
    return pl.pallas_call(
        body, name=name, out_shape=out_shape, in_specs=[hbm] * n, out_specs=[hbm] * n,
        scratch_shapes=[pltpu.SemaphoreType.DMA((n * N_COPIES,)), pltpu.SemaphoreType.DMA((n * N_COPIES,)),
                        pltpu.SemaphoreType.DMA((n,))],
    )(*ins)


def _all_gather(shards, name):
    n = len(shards)

    def body(*refs):
        x_refs, out_refs = refs[:n], refs[n:2 * n]
        send_sems, recv_sems, local_sems = refs[2 * n:]
        x, y, c = _mesh_pos()
        me, sibling = (x, y, c), (x, y, 1 - c)
        chips = [(1 - x, y), (x, 1 - y), (1 - x, 1 - y)]

        def copy(w, k, block, to, own=False):
            px, py, pc = block
            dst = out_refs[w].at[4 * px + 2 * py + pc]
            return pltpu.make_async_remote_copy(
                src_ref=x_refs[w] if own else dst, dst_ref=dst, send_sem=send_sems.at[w * N_COPIES + k],
                recv_sem=recv_sems.at[w * N_COPIES + k], device_id=to, device_id_type=MESH)

        mine = [pltpu.make_async_copy(x_refs[w], out_refs[w].at[4 * x + 2 * y + c], local_sems.at[w]) for w in range(n)]
        for cp in mine:
            cp.start()
        first = [copy(w, 1 + j, me, (*chip, c), own=True) for j, chip in enumerate(chips) for w in range(n)]
        first += [copy(w, 0, me, sibling, own=True) for w in range(n)]
        for cp in first:
            cp.start()
        passed = []
        for j, chip in enumerate(chips):
            for w in range(n):
                copy(w, 1 + j, (*chip, c), me).wait_recv()
                passed.append(copy(w, 4 + j, (*chip, c), sibling))
                passed[-1].start()
        for w in range(n):
            copy(w, 0, sibling, me).wait_recv()
        for j, chip in enumerate(chips):
            for w in range(n):
                copy(w, 4 + j, (*chip, 1 - c), me).wait_recv()
        for cp in first + passed:
            cp.wait_send()
        for cp in mine:
            cp.wait()

    out_shape = [jax.ShapeDtypeStruct((N_DEV, *s.shape), s.dtype) for s in shards]
    return _comm_call(body, shards, out_shape, name)


HBM_SPEC = pl.BlockSpec(memory_space=pltpu.HBM)
SEM_SPEC = pl.BlockSpec(memory_space=pltpu.SEMAPHORE)
EFFECT = pltpu.SideEffectType.DATAFLOW_SIDE_EFFECTING
PEER_ORDER = (4, 2, 6, 5, 3, 7, 1)


def _peer_copies(src_refs, land_refs, send_sems, recv_sems, gather):
    x, y, c = _mesh_pos()
    me = 4 * x + 2 * y + c
    copies = []
    for k in PEER_ORDER:
        px = 1 - x if k & 4 else x
        py = 1 - y if k & 2 else y
        pc = 1 - c if k & 1 else c
        for w, (src, land) in enumerate(zip(src_refs, land_refs)):
            copies.append(pltpu.make_async_remote_copy(
                src_ref=src if gather else src.at[4 * px + 2 * py + pc],
                dst_ref=land.at[me] if gather else land.at[k - 1],
                send_sem=send_sems.at[w * N_COPIES + k - 1], recv_sem=recv_sems.at[w * N_COPIES + k - 1],
                device_id=(px, py, pc), device_id_type=MESH))
    return copies


def _all_gather_small(shard, name):
    def body(x_ref, out_ref, send_sems, recv_sems):
        x, y, c = _mesh_pos()
        out_ref[4 * x + 2 * y + c] = x_ref[...]
        copies = _peer_copies([x_ref], [out_ref], send_sems, recv_sems, True)
        for cp in copies:
            cp.start()
        for cp in copies:
            cp.wait()

    vmem = pl.BlockSpec(memory_space=pltpu.VMEM)
    return pl.pallas_call(
        body, name=name, out_shape=jax.ShapeDtypeStruct((N_DEV, *shard.shape), shard.dtype), in_specs=[vmem],
        out_specs=vmem, scratch_shapes=[pltpu.SemaphoreType.DMA((N_COPIES,)), pltpu.SemaphoreType.DMA((N_COPIES,))],
    )(shard)


def _copies_start(srcs, *, gather, after, name):
    n = len(srcs)
    lands = [lax.empty(((N_DEV,) + s.shape) if gather else ((N_COPIES,) + s.shape[1:]), s.dtype) for s in srcs]

    def body(*refs):
        src_refs, land_refs = refs[:n], refs[n:2 * n]
        send_sems, recv_sems = refs[-2 * n - 3], refs[-2 * n - 2]
        for cp in _peer_copies(src_refs, land_refs, send_sems, recv_sems, gather):
            cp.start()
        refs[-1][...] = jnp.zeros(refs[-1].shape, F32)

    ins = [pltpu.with_memory_space_constraint(a, pltpu.HBM) for a in srcs + lands]
    in_specs = [HBM_SPEC] * (2 * n)
    if after is not None:
        ins.append(after)
        in_specs.append(pl.BlockSpec(memory_space=pl.ANY))
    sems = pltpu.SemaphoreType.DMA((n * N_COPIES,))
    out = pl.pallas_call(
        body, name=name,
        out_shape=(sems, sems, *[pltpu.HBM(a.shape, a.dtype) for a in srcs + lands], jax.ShapeDtypeStruct((8, LANES), F32)),
        in_specs=in_specs,
        out_specs=(SEM_SPEC, SEM_SPEC, *[HBM_SPEC] * (2 * n), pl.BlockSpec(memory_space=pltpu.VMEM)),
        input_output_aliases={i: 2 + i for i in range(2 * n)},
        compiler_params=pltpu.CompilerParams(has_side_effects=EFFECT),
    )(*ins)
    return out[0], out[1], list(out[2:2 + n]), list(out[2 + n:2 + 2 * n]), out[-1]


def _copies_wait(started, *, gather, after, name):
    send_sems, recv_sems, srcs, lands, _ = started
    n = len(srcs)

    def body(*refs):
        src_refs, land_refs = refs[:n], refs[n:2 * n]
        for cp in _peer_copies(src_refs, land_refs, refs[2 * n], refs[2 * n + 1], gather):
            cp.wait_send()
            cp.wait_recv()

    out = pl.pallas_call(
        body, name=name, out_shape=tuple(pltpu.HBM(a.shape, a.dtype) for a in srcs + lands),
        in_specs=[HBM_SPEC] * (2 * n) + [SEM_SPEC, SEM_SPEC, pl.BlockSpec(memory_space=pl.ANY)],
        out_specs=tuple([HBM_SPEC] * (2 * n)), input_output_aliases={i: i for i in range(2 * n)},
        compiler_params=pltpu.CompilerParams(has_side_effects=EFFECT),
    )(*srcs, *lands, send_sems, recv_sems, after)
    return list(out[n:])


def _adamw(got, w, m, v, *, name, tile, own=None, layer=None, into=None):
    rows, width = w.shape[-2:]
    n_got = got.shape[0]
    c1 = 1.0 / (1.0 - ADAM_B1 ** ADAM_STEP)
    c2 = 1.0 / (1.0 - ADAM_B2 ** ADAM_STEP)

    def body(*refs):
        got_ref, w_ref, m_ref, v_ref = refs[:4]
        g_ref, d_ref, nm_ref, nv_ref = refs[-4:]
        g = got_ref[0] if own is None else refs[4][...] + got_ref[0]
        for s in range(1, n_got):
            g = g + got_ref[s]
        m_new = ADAM_B1 * m_ref[...] + (1.0 - ADAM_B1) * g
        v_new = ADAM_B2 * v_ref[...] + (1.0 - ADAM_B2) * (g * g)
        g_ref[...] = g
        nm_ref[...] = m_new
        nv_ref[...] = v_new
        d_ref[...] = -ADAM_LR * ((m_new * c1) / (jnp.sqrt(v_new * c2) + ADAM_EPS) + ADAM_WD * w_ref[...])

    spec = pl.BlockSpec((tile, width), lambda i: (i, 0))
    wspec = spec if layer is None else pl.BlockSpec((None, tile, width), lambda i: (layer, i, 0))
    args = [got, w, m, v] + ([] if own is None else [own])
    in_specs = [pl.BlockSpec((n_got, tile, width), lambda i: (0, i, 0))] + [wspec] * 3 + [spec] * (len(args) - 4)
    aliases = {}
    if into is not None:
        aliases = {len(args) + i: i for i in range(4)}
        args += list(into)
        in_specs += [pl.BlockSpec(memory_space=pl.ANY)] * 4
    return pl.pallas_call(
        body, name=name, grid=(rows // tile,), in_specs=in_specs, out_specs=[wspec] * 4,
        out_shape=[jax.ShapeDtypeStruct(w.shape, F32)] * 4, input_output_aliases=aliases,
        compiler_params=_cparams("parallel"),
    )(*args)


SHARDED = dict(a_w_in=(2, True), a_w_out=(1, True), b_w_in=(2, True), b_w_out=(1, True), ffn_w_in=(2, True),
               ffn_w_out=(1, True), b_conv_w=(2, False), b_conv_b=(1, False), b_norm_w=(1, False), ffn_conv_w=(2, False))
REPLICATED = ("norm1_w", "norm2_w", "a_lb_logits", "a_norm_w", "b_dt_bias", "b_a_log", "b_d_skip", "ffn_conv_b",
              "final_norm_w")
WEIGHTS = ("norm1_w", "norm2_w", "a_w_in", "a_lb_logits", "a_norm_w", "a_w_out", "b_w_in", "b_conv_w", "b_conv_b",
           "b_dt_bias", "b_a_log", "b_d_skip", "b_norm_w", "b_w_out", "ffn_w_in", "ffn_conv_w", "ffn_conv_b",
           "ffn_w_out", "final_norm_w")


def _pad_rows(flat, multiple):
    n = flat.shape[-1]
    per = PACK_W * multiple
    total = -(-n // per) * per
    flat = jnp.pad(flat, [(0, 0)] * (flat.ndim - 1) + [(0, total - n)])
    return flat.reshape(*flat.shape[:-1], total // PACK_W, PACK_W)


def _to_parts(full, axis, n=N_DEV):
    shp = full.shape
    t = full.reshape(*shp[:axis], n, shp[axis] // n, *shp[axis + 1:])
    return jnp.moveaxis(t, axis, 0)


def _from_parts(parts, axis):
    t = jnp.moveaxis(parts, 0, axis)
    shp = t.shape
    return t.reshape(*shp[:axis], shp[axis] * shp[axis + 1], *shp[axis + 2:])


BIG = tuple(n for n, (_, mm) in SHARDED.items() if mm)
SMALL = tuple(n for n, (_, mm) in SHARDED.items() if not mm)
SMALL_W = 512


def _small_rows(tree, lead):
    rows = []
    for n in SMALL:
        t = tree[n]
        t = t.reshape(*lead, -1, t.shape[-1])
        rows.append(jnp.pad(t, [(0, 0)] * (t.ndim - 1) + [(0, SMALL_W - t.shape[-1])]))
    buf = jnp.concatenate(rows, axis=-2)
    return jnp.pad(buf, [(0, 0)] * (buf.ndim - 2) + [(0, 16 - buf.shape[-2]), (0, 0)])


def _small_unrows(buf, like, lead):
    out, r = {}, 0
    for n in SMALL:
        shp = like[n].shape
        k = like[n].size // shp[-1]
        out[n] = buf[..., r:r + k, :shp[-1]].reshape(*lead, *shp)
        r += k
    return out


GROUPS = dict(hg=(("a_w_in", 0), ("a_w_out", 0)), f0=(("ffn_w_in", 0), ("ffn_w_out", 0)),
              mb=(("b_w_in", 0), ("b_w_out", 0)), f1=(("ffn_w_in", 1), ("ffn_w_out", 1)))


class _Comm:
    def __init__(self, local):
        x, y, c = _mesh_pos()
        self.me = 4 * x + 2 * y + c
        self.local = local
        self.shards = {g: [local[n][i].astype(BF16) for n, i in names] for g, names in GROUPS.items()}
        self.shards["f0"].append(_small_rows(local, ()))
        self.first = _all_gather(self.shards["hg"], "gather_hg")
        self.gathers, self.sent, token = {}, {}, None
        for g in ("f0", "mb", "f1"):
            self.gathers[g] = _copies_start(self.shards[g], gather=True, after=token, name=f"gather_{g}_start")
            token = self.gathers[g][-1]
        self.token = token

    def weights(self, group, after):
        if group == "hg":
            got = self.first
        else:
            lands = _copies_wait(self.gathers[group], gather=True, after=after, name=f"gather_{group}_wait")
            got = [lax.dynamic_update_index_in_dim(land, shard, self.me, 0)
                   for land, shard in zip(lands, self.shards[group])]
        out = dict(w_in=_from_parts(got[0], 1), w_out=_from_parts(got[1], 0))
        if group == "hg":
            out["token"] = self.token
        if group == "f0":
            small = _small_unrows(got[2], self.local, (N_DEV,))
            out["small"] = {n: _from_parts(small[n], SHARDED[n][0]) for n in SMALL}
        return out

    def send(self, group, grads):
        w_in = grads["w_in"]
        if isinstance(w_in, tuple):
            half = N_DEV // len(w_in)
            parts_in = jnp.concatenate([_to_parts(t, 1, half) for t in w_in], axis=0)
        else:
            parts_in = w_in if w_in.ndim == 3 else _to_parts(w_in, 1)
        parts = [parts_in, _to_parts(grads["w_out"], 0)]
        if "small" in grads:
            parts.append(_small_rows({n: _to_parts(grads["small"][n], SHARDED[n][0]) for n in SMALL}, (N_DEV,)))
        sent = [p.astype(BF16) for p in parts[:2]] + parts[2:] if group == "hg" else parts
        self.sent[group] = (parts, _copies_start(sent, gather=False, after=None, name=f"exchange_{group}_start"))
        return self.sent[group][1][-1]

    def finish(self, after, mom, var):
        res = {}
        for group in ("f1", "mb", "f0", "hg"):
            parts, started = self.sent[group]
            lands = _copies_wait(started, gather=False, after=after, name=f"exchange_{group}_wait")
            own = [lax.dynamic_index_in_dim(p, self.me, 0, keepdims=False) for p in parts]
            for (n, i), got, mine in zip(GROUPS[group], lands, own):
                res[n] = _adamw(got, self.local[n], mom[n], var[n], own=mine, layer=i, into=res.get(n),
                                name=f"adamw_{n}_{i}", tile=_pick(mine.shape[0], (256, 176, 128)))
                after = res[n][0]
        small = _adamw(lands[2], *[_small_rows(t, ()) for t in (self.local, mom, var)], own=own[2],
                       name="adamw_small", tile=16)
        return res, small


def _pack_small(tree, extra):
    flat = jnp.concatenate([tree[n].reshape(-1) for n in REPLICATED] + [extra.reshape(-1)])
    return _pad_rows(flat, 8)


def _unpack_small(pack, like):
    flat, out, off = pack.reshape(-1), {}, 0
    for n in REPLICATED:
        out[n] = flat[off:off + like[n].size].reshape(like[n].shape)
        off += like[n].size
    return out, flat[off]


def kernel(x, norm1_w, norm2_w, a_w_in, a_lb_logits, a_norm_w, a_w_out, b_w_in, b_conv_w, b_conv_b, b_dt_bias, b_a_log, b_d_skip, b_norm_w, b_w_out, ffn_w_in, ffn_conv_w, ffn_conv_b, ffn_w_out, final_norm_w, loss_target, m_norm1_w, m_norm2_w, m_a_w_in, m_a_lb_logits, m_a_norm_w, m_a_w_out, m_b_w_in, m_b_conv_w, m_b_conv_b, m_b_dt_bias, m_b_a_log, m_b_d_skip, m_b_norm_w, m_b_w_out, m_ffn_w_in, m_ffn_conv_w, m_ffn_conv_b, m_ffn_w_out, m_final_norm_w, v_norm1_w, v_norm2_w, v_a_w_in, v_a_lb_logits, v_a_norm_w, v_a_w_out, v_b_w_in, v_b_conv_w, v_b_conv_b, v_b_dt_bias, v_b_a_log, v_b_d_skip, v_b_norm_w, v_b_w_out, v_ffn_w_in, v_ffn_conv_w, v_ffn_conv_b, v_ffn_w_out, v_final_norm_w):
    given = dict(locals())
    local = {n: given[n] for n in WEIGHTS}
    mom = {n: given["m_" + n] for n in WEIGHTS}
    var = {n: given["v_" + n] for n in WEIGHTS}

    comm = _Comm(local)
    loss, grad_x, grads = _local_step(x[0], loss_target[0], local, comm)

    res, small_res = comm.finish(grad_x, mom, var)
    outs = ({}, {}, {}, {})
    for n in BIG:
        for out, r in zip(outs, res[n]):
            out[n] = r
    for out, r in zip(outs, small_res):
        out.update(_small_unrows(r, local, ()))
    out_g, out_d, out_m, out_v = outs

    small = _pack_small(grads, loss)
    rows = small.shape[0]
    got_s = _all_gather_small(small, "gather_small")
    zero = jnp.zeros((1,), F32)
    g, dlt, nm, nv = _adamw(got_s, _pack_small(local, zero), _pack_small(mom, zero), _pack_small(var, zero),
                            name="adamw_replicated", tile=rows)
    (rep_g, total), (rep_d, _), (rep_m, _), (rep_v, _) = (_unpack_small(t, local) for t in (g, dlt, nm, nv))
    out_g.update(rep_g)
    out_d.update(rep_d)
    out_m.update(rep_m)
    out_v.update(rep_v)

    return (total, grad_x[None], *[out_g[n] for n in WEIGHTS], *[out_d[n] for n in WEIGHTS],
            *[out_m[n] for n in WEIGHTS], *[out_v[n] for n in WEIGHTS])
```

```python
import functools

import jax
import jax.numpy as jnp
from jax import lax
from jax.experimental import pallas as pl
from jax.experimental.pallas import tpu as pltpu

F32 = jnp.float32
BF16 = jnp.bfloat16
MESH = pl.DeviceIdType.MESH

N_DEV = 8
EPS = 1e-6
D_MODEL = 1024
HG_HEADS = 8
HG_HEAD_DIM = 128
D_INNER = 2048
SSM_HEAD_DIM = 64
SSM_GROUPS = 8
SSM_HPG = 4
SSM_STATE = 128
GN = SSM_GROUPS * SSM_STATE
CONV_DIM = D_INNER + 2 * GN
D_FF = 2816
SSM_CONV = 5
FFN_CONV = 3

ADAM_LR = 0.001
ADAM_B1 = 0.9
ADAM_B2 = 0.999
ADAM_EPS = 1e-08
ADAM_WD = 0.01
ADAM_STEP = 10

LANES = 128
ROW_TILE = 256
COL_TILE = 128
GLA_CHUNK = 16
GLA_BLOCK = 256
GLA_HEADS_PER_STEP = 8
SSD_CHUNK = 128
SSD_FWD_GROUPS_PER_STEP = 4
SSD_BWD_GROUPS_PER_STEP = 1
SSD_BLOCK = 512
PACK_W = 1024
VMEM_LIMIT = 56 * 1024 * 1024

NT_DIMS = (((1,), (1,)), ((), ()))
TN_DIMS = (((0,), (0,)), ((), ()))


def _cparams(*sem):
    return pltpu.CompilerParams(dimension_semantics=sem, vmem_limit_bytes=VMEM_LIMIT)


def _rms(x, w):
    return x * lax.rsqrt(jnp.mean(x * x, axis=-1, keepdims=True) + EPS) * w


def _row_kernel(body_fn, rows, params, row_outs, acc_outs, *, name, tile=ROW_TILE):
    L = rows[0][0].shape[0]
    tile = min(tile, L)
    n_in = len(rows) + len(params)
    n_ro = len(row_outs)

    def body(*refs):
        outs = body_fn(*[r[...] for r in refs[:n_in]])
        for ref, o in zip(refs[n_in:n_in + n_ro], outs[:n_ro]):
            ref[...] = o.astype(ref.dtype)
        first = pl.program_id(0) == 0
        for ref, o in zip(refs[n_in + n_ro:], outs[n_ro:]):
            @pl.when(first)
            def _(ref=ref):
                ref[...] = jnp.zeros(ref.shape, ref.dtype)
            ref[...] += o

    in_specs = [pl.BlockSpec((tile, w), lambda i, cb=cb: (i, cb)) for _, w, cb in rows]
    in_specs += [pl.BlockSpec(p.shape, lambda i: (0, 0)) for p in params]
    out_specs = [pl.BlockSpec((tile, w), lambda i: (i, 0)) for w, _ in row_outs]
    out_specs += [pl.BlockSpec(s, lambda i: (0, 0)) for s in acc_outs]
    out_shape = [jax.ShapeDtypeStruct((L, w), dt) for w, dt in row_outs]
    out_shape += [jax.ShapeDtypeStruct(s, F32) for s in acc_outs]
    return pl.pallas_call(
        body, name=name, grid=(L // tile,), in_specs=in_specs, out_specs=out_specs, out_shape=out_shape,
        compiler_params=_cparams("arbitrary" if acc_outs else "parallel"),
    )(*[a for a, _, _ in rows], *params)


def _col_kernel(body_fn, cols, params, col_outs, par_outs, *, name, n_tiles):
    L = cols[0][0].shape[0]
    n_in = len(cols) + len(params)
    width = n_tiles * COL_TILE

    def body(*refs):
        outs = body_fn(*[r[...] for r in refs[:n_in]])
        for ref, o in zip(refs[n_in:], outs):
            ref[...] = o.astype(ref.dtype)

    in_specs = [pl.BlockSpec((L, COL_TILE), lambda j, cb=cb: (0, cb(j) if callable(cb) else cb + j)) for _, cb in cols]
    in_specs += [pl.BlockSpec((p.shape[0], COL_TILE), lambda j, cb=cb: (0, cb + j)) for p, cb in params]
    out_specs = [pl.BlockSpec((L, COL_TILE), lambda j: (0, j)) for _ in col_outs]
    out_specs += [pl.BlockSpec((k, COL_TILE), lambda j: (0, j)) for k in par_outs]
    out_shape = [jax.ShapeDtypeStruct((L, width), dt) for dt in col_outs]
    out_shape += [jax.ShapeDtypeStruct((k, width), F32) for k in par_outs]
    return pl.pallas_call(
        body, name=name, grid=(n_tiles,), in_specs=in_specs, out_specs=out_specs, out_shape=out_shape,
        compiler_params=_cparams("parallel"),
    )(*[a for a, _ in cols], *[p for p, _ in params])


def _pick(n, options):
    for t in options:
        if n % t == 0:
            return t
    return n


MATMUL_VMEM = 40 * 1024 * 1024


def _matmul_tiles(M, N, K, out_bytes):
    best = None
    for tm in (1024, 512, 256, 128, M):
        for tn in (1408, 1024, 512, 256, 128, N):
            if M % tm or N % tn:
                continue
            if 2 * (2 * K * (tm + tn) + out_bytes * tm * tn) > MATMUL_VMEM:
                continue
            if best is None or tm * tn > best[0] * best[1]:
                best = (tm, tn)
    return best


def _matmul(a, b, *, name, nt=False, ta=False, add=None, out_dtype=F32, after=None, b_cols=None, col_parts=None):
    K, M = a.shape[::-1] if not ta else a.shape
    cb, width = b_cols if b_cols is not None else (0, b.shape[1])
    N = b.shape[0] if nt else width
    assert width == K or not nt
    tm, tn = _matmul_tiles(M, N, K, 4 * (1 + (add is not None)) if out_dtype == F32 else 2 + 4 * (add is not None))
    if col_parts is not None:
        tn = N // col_parts
        assert tn % LANES == 0 and add is None
    col0 = cb if nt else cb * (width // tn)

    def body(*refs):
        a_ref, b_ref = refs[0], refs[1]
        o_ref = refs[-1]
        dims = TN_DIMS if ta else NT_DIMS if nt else (((1,), (0,)), ((), ()))
        acc = lax.dot_general(a_ref[...], b_ref[...], dims, preferred_element_type=F32)
        if add is not None:
            acc = acc + refs[2][...]
        o_ref[...] = acc.astype(o_ref.dtype)

    in_specs = [pl.BlockSpec((K, tm), lambda i, j: (0, i)) if ta else pl.BlockSpec((tm, K), lambda i, j: (i, 0)),
                pl.BlockSpec((tn, K), lambda i, j: (j, col0)) if nt
                else pl.BlockSpec((K, tn), lambda i, j: (0, col0 + j))]
    args = [a, b]
    if add is not None:
        in_specs.append(pl.BlockSpec((tm, tn), lambda i, j: (i, j)))
        args.append(add)
    if after is not None:
        in_specs.append(pl.BlockSpec(memory_space=pl.ANY))
        args.append(after)
    if col_parts is None:
        out_spec, out_shape = pl.BlockSpec((tm, tn), lambda i, j: (i, j)), (M, N)
    else:
        out_spec, out_shape = pl.BlockSpec((None, tm, tn), lambda i, j: (j, i, 0)), (col_parts, M, tn)
    return pl.pallas_call(
        body, name=name, grid=(M // tm, N // tn), in_specs=in_specs, out_specs=out_spec,
        out_shape=jax.ShapeDtypeStruct(out_shape, out_dtype), compiler_params=_cparams("parallel", "parallel"),
    )(*args)


def _hgrn2_pre_fn(q_raw, ffw_raw, fbw_raw, lb_logits):
    lb = jax.nn.softmax(lb_logits, axis=0)[0:1]

    def gate(fr):
        f = lb + (1.0 - lb) * jax.nn.sigmoid(fr)
        return jnp.log(f), 1.0 - f

    lf_fw, k_fw = gate(ffw_raw)
    lf_bw, k_bw = gate(fbw_raw)
    return jax.nn.silu(q_raw), lf_fw, k_fw, lf_bw, k_bw


def _hgrn2_post_fn(o, g, norm_w):
    outs = []
    for h in range(HG_HEADS):
        sl = slice(h * HG_HEAD_DIM, (h + 1) * HG_HEAD_DIM)
        outs.append(_rms(o[:, sl], norm_w) * jax.nn.silu(g[:, sl]))
    return jnp.concatenate(outs, axis=1)


def _mamba_post_fn(y, z, norm_w):
    outs = []
    gw = D_INNER // SSM_GROUPS
    for gi in range(SSM_GROUPS):
        sl = slice(gi * gw, (gi + 1) * gw)
        outs.append(_rms(y[:, sl] * jax.nn.silu(z[:, sl]), norm_w[:, sl]))
    return jnp.concatenate(outs, axis=1)


def _shift_rows_impl(x, d):
    if d == 0:
        return x
    n, edge = x.shape[0], 8
    t = lax.broadcasted_iota(jnp.int32, (edge, x.shape[1]), 0)
    rolled = pltpu.roll(x, d % n, 0)
    if d > 0:
        return jnp.concatenate([jnp.where(t >= d, rolled[:edge], 0.0), rolled[edge:]], axis=0)
    return jnp.concatenate([rolled[:n - edge], jnp.where(t < edge + d, rolled[n - edge:], 0.0)], axis=0)


@functools.partial(jax.custom_vjp, nondiff_argnums=(1,))
def _shift_rows(x, d):
    return _shift_rows_impl(x, d)


_shift_rows.defvjp(lambda x, d: (_shift_rows_impl(x, d), None), lambda d, _, g: (_shift_rows_impl(g, -d),))


def _dwconv(x, w, b):
    taps = w.shape[0]
    c = (taps - 1) // 2
    y = b + w[0:1, :] * _shift_rows(x, c)
    for k in range(1, taps):
        y = y + w[k:k + 1, :] * _shift_rows(x, c - k)
    return y


def _ffn_mid_fn(gate, val, w, b):
    return jax.nn.silu(_dwconv(gate, w, b)) * val


def _mamba_conv_fn(xbc, w, b):
    return jax.nn.silu(_dwconv(xbc, w, b))


def _gla_consts(rev):
    return lax.broadcasted_iota(jnp.int32, (GLA_CHUNK, HG_HEAD_DIM), 0)


def _segment_cumsum(x, seg, rev):
    n = x.shape[0]
    t = lax.broadcasted_iota(jnp.int32, x.shape, 0) & (seg - 1)
    s = 1
    while s < seg:
        if rev:
            x = x + jnp.where(t < seg - s, pltpu.roll(x, n - s, 0), 0.0)
        else:
            x = x + jnp.where(t >= s, pltpu.roll(x, s, 0), 0.0)
        s *= 2
    return x


def _segment_cumsum_mxu(x, seg, rev):
    r = lax.broadcasted_iota(jnp.int32, (seg, seg), 0)
    c = lax.broadcasted_iota(jnp.int32, (seg, seg), 1)
    tri = ((r <= c) if rev else (r >= c)).astype(BF16)
    tri3 = jnp.concatenate([tri, tri, tri], axis=1)
    hi = x.astype(BF16)
    rest = x - hi.astype(F32)
    mid = rest.astype(BF16)
    lo = (rest - mid.astype(F32)).astype(BF16)
    outs = []
    for g in range(x.shape[0] // seg):
        rows = slice(g * seg, (g + 1) * seg)
        terms = jnp.concatenate([hi[rows], mid[rows], lo[rows]], axis=0)
        outs.append(jnp.dot(tri3, terms, preferred_element_type=F32))
    return jnp.concatenate(outs, axis=0)


@functools.partial(jax.custom_vjp, nondiff_argnums=(1, 2))
def _segment_cumsum_diff(x, seg, rev):
    return _segment_cumsum_mxu(x, seg, rev)


_segment_cumsum_diff.defvjp(lambda x, seg, rev: (_segment_cumsum_mxu(x, seg, rev), None),
                            lambda seg, rev, _, g: (_segment_cumsum_mxu(g, seg, not rev),))


def _gla_chunk(st, q, k, v, b, *, rev, consts, halves):
    row = consts
    c = q.shape[0]
    half = c // 2 if halves else c
    o = lax.dot_general((q * jnp.exp(b)).astype(BF16), st.astype(BF16), NT_DIMS, preferred_element_type=F32)
    spans = [slice(i, i + half) for i in range(0, c, half)]
    qs, bs, rows, os = [q[p] for p in spans], [b[p] for p in spans], [row[p] for p in spans], [o[p] for p in spans]

    def reach(s, h, masked):
        diff = bs[h] - b[s:s + 1]
        if masked:
            diff = jnp.where((rows[h] <= s) if rev else (rows[h] >= s), diff, -jnp.inf)
        return jnp.sum(qs[h] * (k[s:s + 1] * jnp.exp(diff)), axis=-1, keepdims=True) * v[s:s + 1]

    for s in range(c):
        own = s // half
        os[own] = os[own] + reach(s, own, True)
        if halves and own == (1 if rev else 0):
            os[1 - own] = os[1 - own] + reach(s, 1 - own, False)
    o = jnp.concatenate(os, axis=0) if halves else os[0]
    b_end = b[0:1] if rev else b[c - 1:c]
    kd = (k * jnp.exp(b_end - b)).astype(BF16)
    st_new = st * jnp.exp(b_end) + lax.dot_general(v.astype(BF16), kd, TN_DIMS, preferred_element_type=F32)
    return st_new, o


def _gla_fwd(q, k, v, v_cb, g, *, rev, add, name):
    L = q.shape[0]
    blk = min(GLA_BLOCK, L)
    nblk, nsub = L // blk, blk // GLA_CHUNK
    hd, hps = HG_HEAD_DIM, GLA_HEADS_PER_STEP
    wide = hd * hps

    def body(*refs):
        q_ref, k_ref, v_ref, g_ref = refs[:4]
        add_ref = refs[4] if add is not None else None
        o_ref, st_out, st_scr, b_scr = refs[-4:]
        consts = _gla_consts(rev)

        @pl.when(pl.program_id(1) == 0)
        def _():
            st_scr[...] = jnp.zeros(st_scr.shape, F32)

        b_scr[...] = _segment_cumsum(g_ref[...], GLA_CHUNK, rev)

        def step(i, carry):
            sl = pl.ds(pl.multiple_of((nsub - 1 - i if rev else i) * GLA_CHUNK, GLA_CHUNK), GLA_CHUNK)
            lanes = [slice(hi * hd, (hi + 1) * hd) for hi in range(hps)]
            ins = [(st_scr[hi], q_ref[sl, ln], k_ref[sl, ln], v_ref[sl, ln], b_scr[sl, ln])
                   for hi, ln in enumerate(lanes)]
            adds = [add_ref[sl, ln] for ln in lanes] if add_ref is not None else None
            outs = [_gla_chunk(*args, rev=rev, consts=consts, halves=True) for args in ins]
            for hi, ln in enumerate(lanes):
                st_out[hi, i] = ins[hi][0]
                st_scr[hi] = outs[hi][0]
                o_ref[sl, ln] = outs[hi][1] if adds is None else outs[hi][1] + adds[hi]
            return carry

        lax.fori_loop(0, nsub, step, 0)

    def pos(j):
        return nblk - 1 - j if rev else j

    spec = pl.BlockSpec((blk, wide), lambda h, j: (pos(j), h))
    in_specs = [spec, spec, pl.BlockSpec((blk, wide), lambda h, j: (pos(j), v_cb // hps + h)), spec]
    args = [q, k, v, g]
    if add is not None:
        in_specs.append(spec)
        args.append(add)
    return pl.pallas_call(
        body, name=name, grid=(HG_HEADS // hps, nblk), in_specs=in_specs,
        out_specs=[spec, pl.BlockSpec((hps, None, nsub, hd, hd), lambda h, j: (h, j, 0, 0, 0))],
        out_shape=[jax.ShapeDtypeStruct((L, HG_HEADS * hd), F32),
                   jax.ShapeDtypeStruct((HG_HEADS, nblk, nsub, hd, hd), F32)],
        scratch_shapes=[pltpu.VMEM((hps, hd, hd), F32), pltpu.VMEM((blk, wide), F32)],
        compiler_params=_cparams("parallel", "arbitrary"),
    )(*args)


def _gla_bwd(q, k, v, v_cb, g, do, states, *, rev, adds, name):
    L = q.shape[0]
    blk = min(GLA_BLOCK, L)
    nblk, nsub = L // blk, blk // GLA_CHUNK
    hd, hps = HG_HEAD_DIM, GLA_HEADS_PER_STEP
    wide = hd * hps
    n_add = 0 if adds is None else 2

    def body(*refs):
        q_ref, k_ref, v_ref, g_ref, do_ref, st_in = refs[:6]
        add_refs = refs[6:6 + n_add]
        dq_ref, dk_ref, dv_ref, dg_ref, dst_scr, b_scr, db_scr = refs[6 + n_add:]
        consts = _gla_consts(rev)
        chunk = functools.partial(_gla_chunk, rev=rev, consts=consts, halves=False)

        @pl.when(pl.program_id(1) == 0)
        def _():
            dst_scr[...] = jnp.zeros(dst_scr.shape, F32)

        b_scr[...] = _segment_cumsum(g_ref[...], GLA_CHUNK, rev)

        def rows(i):
            return pl.ds(pl.multiple_of((nsub - 1 - i if rev else i) * GLA_CHUNK, GLA_CHUNK), GLA_CHUNK)

        def step(n, carry):
            i = nsub - 1 - n
            sl = rows(i)
            lanes = [slice(hi * hd, (hi + 1) * hd) for hi in range(hps)]
            ins = [(st_in[hi, i], q_ref[sl, ln], k_ref[sl, ln], v_ref[sl, ln], b_scr[sl, ln])
                   for hi, ln in enumerate(lanes)]
            cts = [(dst_scr[hi], do_ref[sl, ln]) for hi, ln in enumerate(lanes)]
            adds = [(add_refs[0][sl, ln], add_refs[1][sl, ln]) for ln in lanes] if n_add else None
            outs = [jax.vjp(chunk, *args)[1](ct) for args, ct in zip(ins, cts)]
            for hi, ln in enumerate(lanes):
                dst, dq, dk, dv, db = outs[hi]
                dst_scr[hi] = dst
                dq_ref[sl, ln] = dq if adds is None else dq + adds[hi][0]
                dk_ref[sl, ln] = dk
                dv_ref[sl, ln] = dv if adds is None else dv + adds[hi][1]
                db_scr[sl, ln] = db
            return carry

        lax.fori_loop(0, nsub, step, 0)
        dg_ref[...] = _segment_cumsum(db_scr[...], GLA_CHUNK, not rev)

    def pos(j):
        p = nblk - 1 - j
        return nblk - 1 - p if rev else p

    spec = pl.BlockSpec((blk, wide), lambda h, j: (pos(j), h))
    in_specs = [spec, spec, pl.BlockSpec((blk, wide), lambda h, j: (pos(j), v_cb // hps + h)), spec, spec,
                pl.BlockSpec((hps, None, nsub, hd, hd), lambda h, j: (h, nblk - 1 - j, 0, 0, 0))]
    args = [q, k, v, g, do, states]
    if adds is not None:
        in_specs += [spec, spec]
        args += list(adds)
    out = jax.ShapeDtypeStruct((L, HG_HEADS * hd), F32)
    return pl.pallas_call(
        body, name=name, grid=(HG_HEADS // hps, nblk), in_specs=in_specs,
        out_specs=[spec] * 4, out_shape=[out] * 4,
        scratch_shapes=[pltpu.VMEM((hps, hd, hd), F32), pltpu.VMEM((blk, wide), F32), pltpu.VMEM((blk, wide), F32)],
        compiler_params=_cparams("parallel", "arbitrary"),
    )(*args)


def _ssd_consts(rev):
    c = SSD_CHUNK
    gw = SSM_HPG * SSM_HEAD_DIM
    r2 = lax.broadcasted_iota(jnp.int32, (c, c), 0)
    c2 = lax.broadcasted_iota(jnp.int32, (c, c), 1)
    low = (r2 <= c2) if rev else (r2 >= c2)
    lane = lax.broadcasted_iota(jnp.int32, (1, gw), 1)
    return low, r2 == c2, lane, lane >> 6


def _expand_heads(v4, head_of_lane):
    first = head_of_lane[:, :LANES] == 0
    cols = [v4[:, j:j + 1] for j in range(SSM_HPG)]
    return jnp.concatenate([jnp.where(first, cols[0], cols[1]), jnp.where(first, cols[2], cols[3])], axis=1)


def _ssd_prep(dtr, bias, alog, dsk, *, rev, consts):
    head_of_lane = consts[-1]
    dt_l = _expand_heads(jax.nn.softplus(dtr + bias), head_of_lane)
    a_l = _expand_heads(-jnp.exp(alog), head_of_lane)
    return dt_l, _segment_cumsum_diff(dt_l * a_l, SSD_CHUNK, rev), _expand_heads(dsk, head_of_lane)


def _ssd_chunk(st, x, bm, cm, dt_l, acum, dsk_l, *, rev, consts, skip):
    low, eye, lane, head_of_lane = consts
    c = x.shape[0]
    xd = x * dt_l
    cb = lax.dot_general(cm.astype(BF16), bm.astype(BF16), NT_DIMS, preferred_element_type=F32)
    y = jnp.dot(cm.astype(BF16), st.astype(BF16), preferred_element_type=F32) * jnp.exp(acum)
    for j in range(SSM_HPG):
        acol = jnp.sum(jnp.where(lane == j * SSM_HEAD_DIM, acum, 0.0), axis=-1, keepdims=True)
        ab = jnp.broadcast_to(acol, (c, c))
        arow = jnp.sum(jnp.where(eye, ab, 0.0), axis=0, keepdims=True)
        lmat = jnp.exp(jnp.where(low, ab - arow, -jnp.inf))
        xj = jnp.where(head_of_lane == j, xd, 0.0)
        y = y + jnp.dot((cb * lmat).astype(BF16), xj.astype(BF16), preferred_element_type=F32)
    a_end = acum[0:1] if rev else acum[c - 1:c]
    xdec = (xd * jnp.exp(a_end - acum)).astype(BF16)
    st_new = st * jnp.exp(a_end) + lax.dot_general(bm.astype(BF16), xdec, TN_DIMS, preferred_element_type=F32)
    if skip:
        y = y + x * dsk_l
    return st_new, y


def _ssd_specs(L, rev):
    blk = min(SSD_BLOCK, L)
    nblk = L // blk
    gw = SSM_HPG * SSM_HEAD_DIM

    def pos(j):
        return nblk - 1 - j if rev else j

    return blk, nblk, gw, pos


def _ssd_fwd(xbc, dtr, bias, alog, dsk, *, rev, skip, add, name):
    L = xbc.shape[0]
    blk, nblk, gw, pos = _ssd_specs(L, rev)
    nsub = blk // SSD_CHUNK
    n = SSM_STATE

    gps = SSD_FWD_GROUPS_PER_STEP

    def body(*refs):
        x_ref, b_ref, c_ref, dt_ref, bias_ref, alog_ref, dsk_ref = refs[:7]
        add_ref = refs[7] if add is not None else None
        y_ref, st_out, st_scr, dt_scr, ac_scr = refs[-5:]
        consts = _ssd_consts(rev)
        xl = [slice(gi * gw, (gi + 1) * gw) for gi in range(gps)]
        nl = [slice(gi * n, (gi + 1) * n) for gi in range(gps)]

        @pl.when(pl.program_id(1) == 0)
        def _():
            st_scr[...] = jnp.zeros(st_scr.shape, F32)

        dsk_l = []
        for gi in range(gps):
            for r in range(nsub):
                rs = slice(r * SSD_CHUNK, (r + 1) * SSD_CHUNK)
                dt_scr[rs, xl[gi]], ac_scr[rs, xl[gi]], d = _ssd_prep(
                    dt_ref[gi, rs], bias_ref[gi], alog_ref[gi], dsk_ref[gi], rev=rev, consts=consts)
            dsk_l.append(d)

        def step(i, carry):
            sl = pl.ds(pl.multiple_of((nsub - 1 - i if rev else i) * SSD_CHUNK, SSD_CHUNK), SSD_CHUNK)
            ins = [(st_scr[gi], x_ref[sl, xl[gi]], b_ref[sl, nl[gi]], c_ref[sl, nl[gi]], dt_scr[sl, xl[gi]],
                    ac_scr[sl, xl[gi]], dsk_l[gi]) for gi in range(gps)]
            adds = [add_ref[sl, xl[gi]] for gi in range(gps)] if add_ref is not None else None
            outs = [_ssd_chunk(*a, rev=rev, consts=consts, skip=skip) for a in ins]
            for gi in range(gps):
                st_out[gi, i] = ins[gi][0]
                st_scr[gi] = outs[gi][0]
                y_ref[sl, xl[gi]] = outs[gi][1] if adds is None else outs[gi][1] + adds[gi]
            return carry

        lax.fori_loop(0, nsub, step, 0)

    b0 = D_INNER // (n * gps)
    yspec = pl.BlockSpec((blk, gw * gps), lambda g, j: (pos(j), g))
    pspec = pl.BlockSpec((gps, 1, SSM_HPG), lambda g, j: (g, 0, 0))
    in_specs = [yspec,
                pl.BlockSpec((blk, n * gps), lambda g, j: (pos(j), b0 + g)),
                pl.BlockSpec((blk, n * gps), lambda g, j: (pos(j), b0 + SSM_GROUPS // gps + g)),
                pl.BlockSpec((gps, blk, SSM_HPG), lambda g, j: (g, pos(j), 0)),
                pspec, pspec, pspec]
    args = [xbc, xbc, xbc, dtr, bias, alog, dsk]
    if add is not None:
        in_specs.append(yspec)
        args.append(add)
    return pl.pallas_call(
        body, name=name, grid=(SSM_GROUPS // gps, nblk), in_specs=in_specs,
        out_specs=[yspec, pl.BlockSpec((gps, None, nsub, n, gw), lambda g, j: (g, j, 0, 0, 0))],
        out_shape=[jax.ShapeDtypeStruct((L, D_INNER), F32),
                   jax.ShapeDtypeStruct((SSM_GROUPS, nblk, nsub, n, gw), F32)],
        scratch_shapes=[pltpu.VMEM((gps, n, gw), F32), pltpu.VMEM((blk, gw * gps), F32),
                        pltpu.VMEM((blk, gw * gps), F32)],
        compiler_params=_cparams("parallel", "arbitrary"),
    )(*args)


def _ssd_bwd(xbc, dtr, bias, alog, dsk, dy, states, *, rev, skip, adds, name):
    L = xbc.shape[0]
    blk, nblk, gw, _ = _ssd_specs(L, rev)
    nsub = blk // SSD_CHUNK
    n = SSM_STATE
    n_add = 0 if adds is None else 3
    gps = SSD_BWD_GROUPS_PER_STEP

    def body(*refs):
        x_ref, b_ref, c_ref, dt_ref, bias_ref, alog_ref, dsk_ref, dy_ref, st_in = refs[:9]
        add_refs = refs[9:9 + n_add]
        outs = refs[9 + n_add:]
        dx_ref, db_ref, dc_ref, ddt_ref, dbias_ref, dalog_ref, ddsk_ref = outs[:7]
        dst_scr, dt_scr, ac_scr, ddt_scr, dac_scr = outs[7:]
        consts = _ssd_consts(rev)
        chunk = functools.partial(_ssd_chunk, rev=rev, consts=consts, skip=skip)
        prep = functools.partial(_ssd_prep, rev=rev, consts=consts)

        @pl.when(pl.program_id(1) == 0)
        def _():
            dst_scr[...] = jnp.zeros(dst_scr.shape, F32)
            dbias_ref[...] = jnp.zeros(dbias_ref.shape, F32)
            dalog_ref[...] = jnp.zeros(dalog_ref.shape, F32)
            ddsk_ref[...] = jnp.zeros(ddsk_ref.shape, F32)

        xl = [slice(gi * gw, (gi + 1) * gw) for gi in range(gps)]
        nl = [slice(gi * n, (gi + 1) * n) for gi in range(gps)]
        chunks = [slice(r * SSD_CHUNK, (r + 1) * SSD_CHUNK) for r in range(nsub)]
        dsk_l, prep_vjp = [], []
        for gi in range(gps):
            for rs in chunks:
                narrow = (dt_ref[gi, rs], bias_ref[gi], alog_ref[gi], dsk_ref[gi])
                (dt_scr[rs, xl[gi]], ac_scr[rs, xl[gi]], d), pv = jax.vjp(prep, *narrow)
                prep_vjp.append(pv)
            dsk_l.append(d)

        def step(k, ddsk_l):
            i = nsub - 1 - k
            sl = pl.ds(pl.multiple_of((nsub - 1 - i if rev else i) * SSD_CHUNK, SSD_CHUNK), SSD_CHUNK)
            ins = [(st_in[gi, i], x_ref[sl, xl[gi]], b_ref[sl, nl[gi]], c_ref[sl, nl[gi]], dt_scr[sl, xl[gi]],
                    ac_scr[sl, xl[gi]], dsk_l[gi]) for gi in range(gps)]
            cts = [(dst_scr[gi], dy_ref[sl, xl[gi]]) for gi in range(gps)]
            more = [(add_refs[0][sl, xl[gi]], add_refs[1][sl, nl[gi]], add_refs[2][sl, nl[gi]])
                    for gi in range(gps)] if n_add else None
            grads = [jax.vjp(chunk, *a)[1](ct) for a, ct in zip(ins, cts)]
            new = []
            for gi in range(gps):
                dst, dx, db, dc, ddt_l, dac, ddsk_k = grads[gi]
                dst_scr[gi] = dst
                dx_ref[sl, xl[gi]] = dx if more is None else dx + more[gi][0]
                db_ref[sl, nl[gi]] = db if more is None else db + more[gi][1]
                dc_ref[sl, nl[gi]] = dc if more is None else dc + more[gi][2]
                ddt_scr[sl, xl[gi]] = ddt_l
                dac_scr[sl, xl[gi]] = dac
                new.append(ddsk_l[gi] + ddsk_k)
            return tuple(new)

        ddsk_l = lax.fori_loop(0, nsub, step, tuple(jnp.zeros((1, gw), F32) for _ in range(gps)))
        for gi in range(gps):
            for r, rs in enumerate(chunks):
                ddt, dbias, dalog, ddsk = prep_vjp[gi * nsub + r](
                    (ddt_scr[rs, xl[gi]], dac_scr[rs, xl[gi]], ddsk_l[gi] if r == 0 else jnp.zeros((1, gw), F32)))
                ddt_ref[gi, rs] = ddt
                dbias_ref[gi] += dbias
                dalog_ref[gi] += dalog
                ddsk_ref[gi] += ddsk

    def pos(j):
        p = nblk - 1 - j
        return nblk - 1 - p if rev else p

    b0 = D_INNER // (n * gps)
    xspec = pl.BlockSpec((blk, gw * gps), lambda g, j: (pos(j), g))
    nspec = pl.BlockSpec((blk, n * gps), lambda g, j: (pos(j), g))
    dtspec = pl.BlockSpec((gps, blk, SSM_HPG), lambda g, j: (g, pos(j), 0))
    pspec = pl.BlockSpec((gps, 1, SSM_HPG), lambda g, j: (g, 0, 0))
    in_specs = [xspec,
                pl.BlockSpec((blk, n * gps), lambda g, j: (pos(j), b0 + g)),
                pl.BlockSpec((blk, n * gps), lambda g, j: (pos(j), b0 + SSM_GROUPS // gps + g)),
                dtspec, pspec, pspec, pspec, xspec,
                pl.BlockSpec((gps, None, nsub, n, gw), lambda g, j: (g, nblk - 1 - j, 0, 0, 0))]
    args = [xbc, xbc, xbc, dtr, bias, alog, dsk, dy, states]
    if adds is not None:
        in_specs += [xspec, nspec, nspec]
        args += list(adds)
    par = jax.ShapeDtypeStruct((SSM_GROUPS, 1, SSM_HPG), F32)
    return pl.pallas_call(
        body, name=name, grid=(SSM_GROUPS // gps, nblk), in_specs=in_specs,
        out_specs=[xspec, nspec, nspec, dtspec, pspec, pspec, pspec],
        out_shape=[jax.ShapeDtypeStruct((L, D_INNER), F32), jax.ShapeDtypeStruct((L, GN), F32),
                   jax.ShapeDtypeStruct((L, GN), F32), jax.ShapeDtypeStruct((SSM_GROUPS, L, SSM_HPG), F32),
                   par, par, par],
        scratch_shapes=[pltpu.VMEM((gps, n, gw), F32)] + [pltpu.VMEM((blk, gw * gps), F32)] * 4,
        compiler_params=_cparams("parallel", "arbitrary"),
    )(*args)


def _out_proj(y, w_out, h, next_norm, name):
    if next_norm is None:
        return _matmul(y, w_out, add=h, name=name), None
    M, K = y.shape
    N = w_out.shape[1]
    tm = 512

    def body(y_ref, w_ref, h_ref, n_ref, o_ref, u_ref):
        acc = jnp.dot(y_ref[...], w_ref[...], preferred_element_type=F32) + h_ref[...]
        o_ref[...] = acc
        u_ref[...] = _rms(acc, n_ref[...]).astype(BF16)

    rows = pl.BlockSpec((tm, N), lambda i: (i, 0))
    return pl.pallas_call(
        body, name=name, grid=(M // tm,),
        in_specs=[pl.BlockSpec((tm, K), lambda i: (i, 0)), pl.BlockSpec((K, N), lambda i: (0, 0)), rows,
                  pl.BlockSpec((1, N), lambda i: (0, 0))],
        out_specs=[rows, rows], out_shape=[jax.ShapeDtypeStruct((M, N), F32), jax.ShapeDtypeStruct((M, N), BF16)],
        compiler_params=_cparams("parallel"),
    )(y, w_out, h, next_norm)


def _norm_fwd(h, w, name):
    d = h.shape[1]
    return _row_kernel(lambda hv, wv: (_rms(hv, wv),), [(h, d, 0)], [w], [(d, BF16)], [], name=name)[0]


def _matmul_norm_bwd(a, b, h, w, dh_in, *, name, add=None, b_cols=None, after=None):
    M, K = a.shape
    N = b.shape[0]
    cb = 0 if b_cols is None else b_cols[0]
    tm = 512 if K <= 4096 else 256

    def body(*refs):
        a_ref, b_ref, h_ref, w_ref, dh_ref = refs[:5]
        o32_ref, o16_ref, dw_ref = refs[-3:]
        du = lax.dot_general(a_ref[...], b_ref[...], NT_DIMS, preferred_element_type=F32)
        if add is not None:
            du = du + refs[5][...]
        _, vjp = jax.vjp(_rms, h_ref[...], w_ref[...])
        dh, dw = vjp(du)
        dh = dh + dh_ref[...]
        o32_ref[...] = dh
        o16_ref[...] = dh.astype(BF16)

        @pl.when(pl.program_id(0) == 0)
        def _():
            dw_ref[...] = jnp.zeros(dw_ref.shape, F32)

        dw_ref[...] += dw

    rows = pl.BlockSpec((tm, N), lambda i: (i, 0))
    in_specs = [pl.BlockSpec((tm, K), lambda i: (i, 0)), pl.BlockSpec((N, K), lambda i: (0, cb)), rows,
                pl.BlockSpec((1, N), lambda i: (0, 0)), rows]
    args = [a, b, h, w, dh_in]
    if add is not None:
        in_specs.append(rows)
        args.append(add)
    if after is not None:
        in_specs.append(pl.BlockSpec(memory_space=pl.ANY))
        args.append(after)
    return pl.pallas_call(
        body, name=name, grid=(M // tm,), in_specs=in_specs,
        out_specs=[rows, rows, pl.BlockSpec((1, N), lambda i: (0, 0))],
        out_shape=[jax.ShapeDtypeStruct((M, N), F32), jax.ShapeDtypeStruct((M, N), BF16),
                   jax.ShapeDtypeStruct((1, N), F32)],
        compiler_params=_cparams("arbitrary"),
    )(*args)


def _ffn_fwd(h, norm_w, w_in, w_out, conv_w, conv_b, tag, u, next_norm=None, loss_with=None):
    pg = _matmul(u, w_in, b_cols=(0, D_FF), name=f"{tag}_in_gate")
    pv = _matmul(u, w_in, b_cols=(1, D_FF), name=f"{tag}_in_val")
    y = _col_kernel(lambda g, v, w, b: (_ffn_mid_fn(g, v, w, b),), [(pg, 0), (pv, 0)], [(conv_w, 0), (conv_b, 0)],
                    [BF16], [], name=f"{tag}_mid", n_tiles=D_FF // COL_TILE)[0]
    if loss_with is not None:
        return _out_proj_loss(y, w_out, h, *loss_with, f"{tag}_out_loss"), (u, pg, pv, y), None
    h_out, u_next = _out_proj(y, w_out, h, next_norm, f"{tag}_out")
    return h_out, (u, pg, pv, y), u_next


def _ffn_bwd(h, dh, dh16, saved, norm_w, w_in, w_out, conv_w, conv_b, tag, after=None):
    u, pg, pv, y = saved
    dy = _matmul(dh16, w_out, nt=True, after=after, name=f"{tag}_out_dx")
    dw_out = _matmul(y, dh16, ta=True, name=f"{tag}_out_dw")

    def fn(g, v, ct, w, b):
        _, vjp = jax.vjp(_ffn_mid_fn, g, v, w, b)
        return vjp(ct)

    dpg, dpv, dcw, dcb = _col_kernel(fn, [(pg, 0), (pv, 0), (dy, 0)], [(conv_w, 0), (conv_b, 0)], [BF16, BF16],
                                     [FFN_CONV, 1], name=f"{tag}_mid_bwd", n_tiles=D_FF // COL_TILE)
    du = _matmul(dpg, w_in, nt=True, b_cols=(0, D_FF), name=f"{tag}_gate_dx")
    dw_gate = _matmul(u, dpg, ta=True, name=f"{tag}_gate_dw")
    dw_val = _matmul(u, dpv, ta=True, name=f"{tag}_val_dw")
    dh, dh16, dnw = _matmul_norm_bwd(dpv, w_in, h, norm_w, dh, add=du, b_cols=(1, D_FF), name=f"{tag}_val_dx_norm")
    return dh, dh16, dict(w_in=(dw_gate, dw_val), w_out=dw_out, conv_w=dcw, conv_b=dcb, norm=dnw)


def _hgrn2_fwd(h, norm_w, w_in, lb_logits, a_norm_w, w_out, after=None, next_norm=None):
    d = D_MODEL
    u = _norm_fwd(h, norm_w, "hg_norm")
    pa = _matmul(u, w_in, after=after, name="hg_in")
    qs, lf_fw, k_fw, lf_bw, k_bw = _row_kernel(
        _hgrn2_pre_fn, [(pa, d, 0), (pa, d, 1), (pa, d, 2)], [lb_logits], [(d, F32)] * 5, [], name="hg_pre")
    o_fw, st_fw = _gla_fwd(qs, k_fw, pa, 3 * HG_HEADS, lf_fw, rev=False, add=None, name="hg_gla_fw")
    o, st_bw = _gla_fwd(qs, k_bw, pa, 3 * HG_HEADS, lf_bw, rev=True, add=o_fw, name="hg_gla_bw")
    y = _row_kernel(lambda ov, gv, wv: (_hgrn2_post_fn(ov, gv, wv),), [(o, d, 0), (pa, d, 4)], [a_norm_w],
                    [(d, BF16)], [], name="hg_post")[0]
    h_out, u_next = _out_proj(y, w_out, h, next_norm, "hg_out")
    return h_out, (u, pa, qs, lf_fw, k_fw, lf_bw, k_bw, st_fw, st_bw, o, y), u_next


def _hgrn2_bwd(h, dh, dh16, saved, norm_w, w_in, lb_logits, a_norm_w, w_out, send, after=None):
    d = D_MODEL
    u, pa, qs, lf_fw, k_fw, lf_bw, k_bw, st_fw, st_bw, o, y = saved
    dy = _matmul(dh16, w_out, nt=True, after=after, name="hg_out_dx")
    dw_out = _matmul(y, dh16, ta=True, name="hg_out_dw")

    def post_bwd(ov, gv, ct, wv):
        _, vjp = jax.vjp(_hgrn2_post_fn, ov, gv, wv)
        return vjp(ct)

    do, dg, dnw = _row_kernel(post_bwd, [(o, d, 0), (pa, d, 4), (dy, d, 0)], [a_norm_w], [(d, F32), (d, F32)],
                              [(1, HG_HEAD_DIM)], name="hg_post_bwd")
    dq1, dk_fw, dv1, dlf_fw = _gla_bwd(qs, k_fw, pa, 3 * HG_HEADS, lf_fw, do, st_fw, rev=False, adds=None,
                                       name="hg_gla_fw_bwd")
    dqs, dk_bw, dv, dlf_bw = _gla_bwd(qs, k_bw, pa, 3 * HG_HEADS, lf_bw, do, st_bw, rev=True, adds=(dq1, dv1),
                                      name="hg_gla_bw_bwd")

    def pre_bwd(qr, fr, br, c0, c1, c2, c3, c4, dvv, dgv, lbl):
        _, vjp = jax.vjp(_hgrn2_pre_fn, qr, fr, br, lbl)
        dq, df, db, dlbl = vjp((c0, c1, c2, c3, c4))
        return jnp.concatenate([dq, df, db, dvv, dgv], axis=1), dlbl

    rows = [(pa, d, 0), (pa, d, 1), (pa, d, 2), (dqs, d, 0), (dlf_fw, d, 0), (dk_fw, d, 0), (dlf_bw, d, 0),
            (dk_bw, d, 0), (dv, d, 0), (dg, d, 0)]
    dpa, dlbl = _row_kernel(pre_bwd, rows, [lb_logits], [(5 * d, BF16)], [lb_logits.shape], name="hg_pre_bwd")
    dw_in = _matmul(u, dpa, ta=True, col_parts=N_DEV, name="hg_in_dw")
    token = send(dw_in, dw_out)
    dh, dh16, dn1 = _matmul_norm_bwd(dpa, w_in, h, norm_w, dh, after=token, name="hg_in_dx_norm")
    return dh, dh16, dict(lb=dlbl, a_norm=dnw, norm=dn1)


def _group_params(p):
    return p.reshape(SSM_GROUPS, 1, SSM_HPG)


def _mamba_fwd(h, norm_w, w_z, w_xbc, w_dt, conv_w, conv_b, dt_bias, a_log, d_skip, b_norm_w, w_out, u, next_norm):
    L = h.shape[0]
    z = _matmul(u, w_z, name="mb_in_z")
    xbc_raw = _matmul(u, w_xbc, name="mb_in_xbc")
    dt_raw = _matmul(u, w_dt, name="mb_in_dt")
    xbc = _col_kernel(lambda xv, w, b: (_mamba_conv_fn(xv, w, b),), [(xbc_raw, 0)], [(conv_w, 0), (conv_b, 0)],
                      [F32], [], name="mb_conv", n_tiles=CONV_DIM // COL_TILE)[0]
    dtr = dt_raw.reshape(L, 2, SSM_GROUPS, SSM_HPG).transpose(1, 2, 0, 3)
    bias, alog = dt_bias.reshape(2, -1), a_log.reshape(2, -1)
    dsk = _group_params(d_skip.reshape(-1))
    y_fw, st_fw = _ssd_fwd(xbc, dtr[0], _group_params(bias[0]), _group_params(alog[0]), dsk, rev=False, skip=True,
                           add=None, name="mb_ssd_fw")
    ysum, st_bw = _ssd_fwd(xbc, dtr[1], _group_params(bias[1]), _group_params(alog[1]), dsk, rev=True, skip=False,
                           add=y_fw, name="mb_ssd_bw")
    y = _row_kernel(lambda yv, zv, wv: (_mamba_post_fn(yv, zv, wv),), [(ysum, D_INNER, 0), (z, D_INNER, 0)],
                    [b_norm_w], [(D_INNER, BF16)], [], name="mb_post")[0]
    h_out, u_next = _out_proj(y, w_out, h, next_norm, "mb_out")
    return h_out, (u, z, xbc_raw, xbc, dtr, st_fw, st_bw, ysum, y), u_next


def _mamba_bwd(h, dh, dh16, saved, norm_w, w_z, w_xbc, w_dt, conv_w, conv_b, dt_bias, a_log, d_skip, b_norm_w, w_out,
               after=None):
    L = h.shape[0]
    u, z, xbc_raw, xbc, dtr, st_fw, st_bw, ysum, y = saved
    dy = _matmul(dh16, w_out, nt=True, after=after, name="mb_out_dx")
    dw_out = _matmul(y, dh16, ta=True, name="mb_out_dw")

    def post_bwd(yv, zv, ct, wv):
        _, vjp = jax.vjp(_mamba_post_fn, yv, zv, wv)
        return vjp(ct)

    dys, dz, dbn = _row_kernel(post_bwd, [(ysum, D_INNER, 0), (z, D_INNER, 0), (dy, D_INNER, 0)], [b_norm_w],
                               [(D_INNER, F32), (D_INNER, BF16)], [(1, D_INNER)], name="mb_post_bwd")
    bias, alog = dt_bias.reshape(2, -1), a_log.reshape(2, -1)
    dsk = _group_params(d_skip.reshape(-1))
    dx1, db1, dc1, ddt_fw, dbias_fw, dalog_fw, ddsk = _ssd_bwd(
        xbc, dtr[0], _group_params(bias[0]), _group_params(alog[0]), dsk, dys, st_fw, rev=False, skip=True,
        adds=None, name="mb_ssd_fw_bwd")
    dx, db, dc, ddt_bw, dbias_bw, dalog_bw, _ = _ssd_bwd(
        xbc, dtr[1], _group_params(bias[1]), _group_params(alog[1]), dsk, dys, st_bw, rev=True, skip=False,
        adds=(dx1, db1, dc1), name="mb_ssd_bw_bwd")

    nx, nb = D_INNER // COL_TILE, GN // COL_TILE

    def conv_bwd(xv, cx, cbm, ccm, w, b):
        j = pl.program_id(0)
        ct = jnp.where(j < nx, cx, jnp.where(j < nx + nb, cbm, ccm))
        _, vjp = jax.vjp(_mamba_conv_fn, xv, w, b)
        return vjp(ct)

    cols = [(xbc_raw, 0), (dx, lambda j: jnp.minimum(j, nx - 1)), (db, lambda j: jnp.clip(j - nx, 0, nb - 1)),
            (dc, lambda j: jnp.clip(j - nx - nb, 0, nb - 1))]
    dxbc, dcw, dcb = _col_kernel(conv_bwd, cols, [(conv_w, 0), (conv_b, 0)], [BF16], [SSM_CONV, 1],
                                 name="mb_conv_bwd", n_tiles=CONV_DIM // COL_TILE)
    ddt = jnp.stack([ddt_fw, ddt_bw]).transpose(2, 0, 1, 3).reshape(L, 2 * SSM_GROUPS * SSM_HPG).astype(BF16)
    du = _matmul(dz, w_z, nt=True, name="mb_z_dx")
    du = _matmul(dxbc, w_xbc, nt=True, add=du, name="mb_xbc_dx")
    dw_in = jnp.concatenate([_matmul(u, dz, ta=True, name="mb_z_dw"), _matmul(u, dxbc, ta=True, name="mb_xbc_dw"),
                             _matmul(u, ddt, ta=True, name="mb_dt_dw")], axis=1)
    dh, dh16, dn1 = _matmul_norm_bwd(ddt, w_dt, h, norm_w, dh, add=du, name="mb_dt_dx_norm")
    grads = dict(w_in=dw_in, w_out=dw_out, conv_w=dcw, conv_b=dcb, b_norm=dbn, norm=dn1,
                 dt_bias=jnp.stack([dbias_fw, dbias_bw]).reshape(1, 2, -1),
                 a_log=jnp.stack([dalog_fw, dalog_bw]).reshape(1, 2, -1), d_skip=ddsk.reshape(1, -1))
    return dh, dh16, grads


def _out_proj_loss(y, w_out, h, target, w, name):
    M, K = y.shape
    N = w_out.shape[1]
    tm = 512

    def body(y_ref, w_ref, h_ref, t_ref, n_ref, dh_ref, dh16_ref, loss_ref, dw_ref):
        out = jnp.dot(y_ref[...], w_ref[...], preferred_element_type=F32) + h_ref[...]
        target = t_ref[...]

        def loss(hv, wv):
            err = _rms(hv, wv) - target
            return 0.5 * jnp.sum(jnp.mean(err * err, axis=-1, keepdims=True), axis=0, keepdims=True)

        val, vjp = jax.vjp(loss, out, n_ref[...])
        dh, dw = vjp(jnp.ones((1, 1), F32))
        dh_ref[...] = dh
        dh16_ref[...] = dh.astype(BF16)

        @pl.when(pl.program_id(0) == 0)
        def _():
            loss_ref[...] = jnp.zeros(loss_ref.shape, F32)
            dw_ref[...] = jnp.zeros(dw_ref.shape, F32)

        loss_ref[...] += val
        dw_ref[...] += dw

    rows = pl.BlockSpec((tm, N), lambda i: (i, 0))
    return pl.pallas_call(
        body, name=name, grid=(M // tm,),
        in_specs=[pl.BlockSpec((tm, K), lambda i: (i, 0)), pl.BlockSpec((K, N), lambda i: (0, 0)), rows, rows,
                  pl.BlockSpec((1, N), lambda i: (0, 0))],
        out_specs=[rows, rows, pl.BlockSpec((1, 1), lambda i: (0, 0)), pl.BlockSpec((1, N), lambda i: (0, 0))],
        out_shape=[jax.ShapeDtypeStruct((M, N), F32), jax.ShapeDtypeStruct((M, N), BF16),
                   jax.ShapeDtypeStruct((1, 1), F32), jax.ShapeDtypeStruct((1, N), F32)],
        compiler_params=_cparams("arbitrary"),
    )(y, w_out, h, target, w)


def _local_step(x, target, rep, comm):
    def ffn_args(i, w, small):
        return (rep["norm2_w"][i:i + 1], w["w_in"], w["w_out"], small["ffn_conv_w"][i], rep["ffn_conv_b"][i:i + 1])

    def mamba_args(w, small):
        w_in = w["w_in"]
        return (rep["norm1_w"][1:2], w_in[:, :D_INNER], w_in[:, D_INNER:D_INNER + CONV_DIM],
                w_in[:, D_INNER + CONV_DIM:], small["b_conv_w"][0], small["b_conv_b"], rep["b_dt_bias"],
                rep["b_a_log"], rep["b_d_skip"], small["b_norm_w"], w["w_out"])

    w_hg = comm.weights("hg", None)
    hg = (rep["norm1_w"][0:1], w_hg["w_in"], rep["a_lb_logits"], rep["a_norm_w"], w_hg["w_out"])
    h0 = x
    h1, s_hg, u1 = _hgrn2_fwd(h0, *hg, after=w_hg.get("token"), next_norm=rep["norm2_w"][0:1])
    w_f0 = comm.weights("f0", h1)
    small = w_f0["small"]
    f0 = ffn_args(0, w_f0, small)
    h2, s_f0, u2 = _ffn_fwd(h1, *f0, "ffn0", u1, next_norm=rep["norm1_w"][1:2])
    mb = mamba_args(comm.weights("mb", h2), small)
    h3, s_mb, u3 = _mamba_fwd(h2, *mb, u2, rep["norm2_w"][1:2])
    f1 = ffn_args(1, comm.weights("f1", h3), small)
    (dh, dh16, loss, d_final), s_f1, _ = _ffn_fwd(h3, *f1, "ffn1", u3,
                                                  loss_with=(target, rep["final_norm_w"].reshape(1, -1)))

    dh, dh16, g_f1 = _ffn_bwd(h3, dh, dh16, s_f1, *f1, "ffn1")
    token = comm.send("f1", dict(w_in=g_f1["w_in"], w_out=g_f1["w_out"]))
    dh, dh16, g_mb = _mamba_bwd(h2, dh, dh16, s_mb, *mb, after=token)
    token = comm.send("mb", dict(w_in=g_mb["w_in"], w_out=g_mb["w_out"]))
    dh, dh16, g_f0 = _ffn_bwd(h1, dh, dh16, s_f0, *f0, "ffn0", after=token)
    token = comm.send("f0", dict(w_in=g_f0["w_in"], w_out=g_f0["w_out"]))
    small_grads = dict(b_conv_w=g_mb["conv_w"][None], b_conv_b=g_mb["conv_b"], b_norm_w=g_mb["b_norm"],
                       ffn_conv_w=jnp.stack([g_f0["conv_w"], g_f1["conv_w"]]))
    dh, dh16, g_hg = _hgrn2_bwd(
        h0, dh, dh16, s_hg, *hg, after=token,
        send=lambda dw_in, dw_out: comm.send("hg", dict(w_in=dw_in, w_out=dw_out, small=small_grads)))
    grads = dict(
        norm1_w=jnp.concatenate([g_hg["norm"], g_mb["norm"]], axis=0),
        norm2_w=jnp.concatenate([g_f0["norm"], g_f1["norm"]], axis=0),
        a_lb_logits=g_hg["lb"], a_norm_w=g_hg["a_norm"], b_dt_bias=g_mb["dt_bias"], b_a_log=g_mb["a_log"],
        b_d_skip=g_mb["d_skip"], ffn_conv_b=jnp.concatenate([g_f0["conv_b"], g_f1["conv_b"]], axis=0),
        final_norm_w=d_final.reshape(-1),
    )
    return loss, dh, grads


def _mesh_pos():
    return lax.axis_index("x"), lax.axis_index("y"), lax.axis_index("c")


N_COPIES = N_DEV - 1


def _comm_call(body, ins, out_shape, name):
    n = len(ins)
    hbm = pl.BlockSpec(memory_space=pl.ANY)
    return pl.pallas_call(
        body, name=name, out_shape=out_shape, in_specs=[hbm] * n, out_specs=[hbm] * n,
        scratch_shapes=[pltpu.SemaphoreType.DMA((n * N_COPIES,)), pltpu.SemaphoreType.DMA((n * N_COPIES,)),
                        pltpu.SemaphoreType.DMA((n,))],
    )(*ins)


def _all_gather(shards, name):
    n = len(shards)

    def body(*refs):
        x_refs, out_refs = refs[:n], refs[n:2 * n]
        send_sems, recv_sems, local_sems = refs[2 * n:]
        x, y, c = _mesh_pos()
        me, sibling = (x, y, c), (x, y, 1 - c)
        chips = [(1 - x, y), (x, 1 - y), (1 - x, 1 - y)]

        def copy(w, k, block, to, own=False):
            px, py, pc = block
            dst = out_refs[w].at[4 * px + 2 * py + pc]
            return pltpu.make_async_remote_copy(
                src_ref=x_refs[w] if own else dst, dst_ref=dst, send_sem=send_sems.at[w * N_COPIES + k],
                recv_sem=recv_sems.at[w * N_COPIES + k], device_id=to, device_id_type=MESH)

        mine = [pltpu.make_async_copy(x_refs[w], out_refs[w].at[4 * x + 2 * y + c], local_sems.at[w]) for w in range(n)]
        for cp in mine:
            cp.start()
        first = [copy(w, 1 + j, me, (*chip, c), own=True) for j, chip in enumerate(chips) for w in range(n)]
        first += [copy(w, 0, me, sibling, own=True) for w in range(n)]
        for cp in first:
            cp.start()
        passed = []
        for j, chip in enumerate(chips):
            for w in range(n):
                copy(w, 1 + j, (*chip, c), me).wait_recv()
                passed.append(copy(w, 4 + j, (*chip, c), sibling))
                passed[-1].start()
        for w in range(n):
            copy(w, 0, sibling, me).wait_recv()
        for j, chip in enumerate(chips):
            for w in range(n):
                copy(w, 4 + j, (*chip, 1 - c), me).wait_recv()
        for cp in first + passed:
            cp.wait_send()
        for cp in mine:
            cp.wait()

    out_shape = [jax.ShapeDtypeStruct((N_DEV, *s.shape), s.dtype) for s in shards]
    return _comm_call(body, shards, out_shape, name)


HBM_SPEC = pl.BlockSpec(memory_space=pltpu.HBM)
SEM_SPEC = pl.BlockSpec(memory_space=pltpu.SEMAPHORE)
EFFECT = pltpu.SideEffectType.DATAFLOW_SIDE_EFFECTING
PEER_ORDER = (4, 2, 6, 5, 3, 7, 1)


def _peer_copies(src_refs, land_refs, send_sems, recv_sems, gather):
    x, y, c = _mesh_pos()
    me = 4 * x + 2 * y + c
    copies = []
    for k in PEER_ORDER:
        px = 1 - x if k & 4 else x
        py = 1 - y if k & 2 else y
        pc = 1 - c if k & 1 else c
        for w, (src, land) in enumerate(zip(src_refs, land_refs)):
            copies.append(pltpu.make_async_remote_copy(
                src_ref=src if gather else src.at[4 * px + 2 * py + pc],
                dst_ref=land.at[me] if gather else land.at[k - 1],
                send_sem=send_sems.at[w * N_COPIES + k - 1], recv_sem=recv_sems.at[w * N_COPIES + k - 1],
                device_id=(px, py, pc), device_id_type=MESH))
    return copies


def _all_gather_small(shard, name):
    def body(x_ref, out_ref, send_sems, recv_sems):
        x, y, c = _mesh_pos()
        out_ref[4 * x + 2 * y + c] = x_ref[...]
        copies = _peer_copies([x_ref], [out_ref], send_sems, recv_sems, True)
        for cp in copies:
            cp.start()
        for cp in copies:
            cp.wait()

    vmem = pl.BlockSpec(memory_space=pltpu.VMEM)
    return pl.pallas_call(
        body, name=name, out_shape=jax.ShapeDtypeStruct((N_DEV, *shard.shape), shard.dtype), in_specs=[vmem],
        out_specs=vmem, scratch_shapes=[pltpu.SemaphoreType.DMA((N_COPIES,)), pltpu.SemaphoreType.DMA((N_COPIES,))],
    )(shard)


def _copies_start(srcs, *, gather, after, name):
    n = len(srcs)
    lands = [lax.empty(((N_DEV,) + s.shape) if gather else ((N_COPIES,) + s.shape[1:]), s.dtype) for s in srcs]

    def body(*refs):
        src_refs, land_refs = refs[:n], refs[n:2 * n]
        send_sems, recv_sems = refs[-2 * n - 3], refs[-2 * n - 2]
        for cp in _peer_copies(src_refs, land_refs, send_sems, recv_sems, gather):
            cp.start()
        refs[-1][...] = jnp.zeros(refs[-1].shape, F32)

    ins = [pltpu.with_memory_space_constraint(a, pltpu.HBM) for a in srcs + lands]
    in_specs = [HBM_SPEC] * (2 * n)
    if after is not None:
        ins.append(after)
        in_specs.append(pl.BlockSpec(memory_space=pl.ANY))
    sems = pltpu.SemaphoreType.DMA((n * N_COPIES,))
    out = pl.pallas_call(
        body, name=name,
        out_shape=(sems, sems, *[pltpu.HBM(a.shape, a.dtype) for a in srcs + lands], jax.ShapeDtypeStruct((8, LANES), F32)),
        in_specs=in_specs,
        out_specs=(SEM_SPEC, SEM_SPEC, *[HBM_SPEC] * (2 * n), pl.BlockSpec(memory_space=pltpu.VMEM)),
        input_output_aliases={i: 2 + i for i in range(2 * n)},
        compiler_params=pltpu.CompilerParams(has_side_effects=EFFECT),
    )(*ins)
    return out[0], out[1], list(out[2:2 + n]), list(out[2 + n:2 + 2 * n]), out[-1]


def _copies_wait(started, *, gather, after, name):
    send_sems, recv_sems, srcs, lands, _ = started
    n = len(srcs)

    def body(*refs):
        src_refs, land_refs = refs[:n], refs[n:2 * n]
        for cp in _peer_copies(src_refs, land_refs, refs[2 * n], refs[2 * n + 1], gather):
            cp.wait_send()
            cp.wait_recv()

    out = pl.pallas_call(
        body, name=name, out_shape=tuple(pltpu.HBM(a.shape, a.dtype) for a in srcs + lands),
        in_specs=[HBM_SPEC] * (2 * n) + [SEM_SPEC, SEM_SPEC, pl.BlockSpec(memory_space=pl.ANY)],
        out_specs=tuple([HBM_SPEC] * (2 * n)), input_output_aliases={i: i for i in range(2 * n)},
        compiler_params=pltpu.CompilerParams(has_side_effects=EFFECT),
    )(*srcs, *lands, send_sems, recv_sems, after)
    return list(out[n:])


def _adamw(got, w, m, v, *, name, tile, own=None, layer=None, into=None):
    rows, width = w.shape[-2:]
    n_got = got.shape[0]
    c1 = 1.0 / (1.0 - ADAM_B1 ** ADAM_STEP)
    c2 = 1.0 / (1.0 - ADAM_B2 ** ADAM_STEP)

    def body(*refs):
        got_ref, w_ref, m_ref, v_ref = refs[:4]
        g_ref, d_ref, nm_ref, nv_ref = refs[-4:]
        g = got_ref[0] if own is None else refs[4][...] + got_ref[0]
        for s in range(1, n_got):
            g = g + got_ref[s]
        m_new = ADAM_B1 * m_ref[...] + (1.0 - ADAM_B1) * g
        v_new = ADAM_B2 * v_ref[...] + (1.0 - ADAM_B2) * (g * g)
        g_ref[...] = g
        nm_ref[...] = m_new
        nv_ref[...] = v_new
        d_ref[...] = -ADAM_LR * ((m_new * c1) / (jnp.sqrt(v_new * c2) + ADAM_EPS) + ADAM_WD * w_ref[...])

    spec = pl.BlockSpec((tile, width), lambda i: (i, 0))
    wspec = spec if layer is None else pl.BlockSpec((None, tile, width), lambda i: (layer, i, 0))
    args = [got, w, m, v] + ([] if own is None else [own])
    in_specs = [pl.BlockSpec((n_got, tile, width), lambda i: (0, i, 0))] + [wspec] * 3 + [spec] * (len(args) - 4)
    aliases = {}
    if into is not None:
        aliases = {len(args) + i: i for i in range(4)}
        args += list(into)
        in_specs += [pl.BlockSpec(memory_space=pl.ANY)] * 4
    return pl.pallas_call(
        body, name=name, grid=(rows // tile,), in_specs=in_specs, out_specs=[wspec] * 4,
        out_shape=[jax.ShapeDtypeStruct(w.shape, F32)] * 4, input_output_aliases=aliases,
        compiler_params=_cparams("parallel"),
    )(*args)


SHARDED = dict(a_w_in=(2, True), a_w_out=(1, True), b_w_in=(2, True), b_w_out=(1, True), ffn_w_in=(2, True),
               ffn_w_out=(1, True), b_conv_w=(2, False), b_conv_b=(1, False), b_norm_w=(1, False), ffn_conv_w=(2, False))
REPLICATED = ("norm1_w", "norm2_w", "a_lb_logits", "a_norm_w", "b_dt_bias", "b_a_log", "b_d_skip", "ffn_conv_b",
              "final_norm_w")
WEIGHTS = ("norm1_w", "norm2_w", "a_w_in", "a_lb_logits", "a_norm_w", "a_w_out", "b_w_in", "b_conv_w", "b_conv_b",
           "b_dt_bias", "b_a_log", "b_d_skip", "b_norm_w", "b_w_out", "ffn_w_in", "ffn_conv_w", "ffn_conv_b",
           "ffn_w_out", "final_norm_w")


def _pad_rows(flat, multiple):
    n = flat.shape[-1]
    per = PACK_W * multiple
    total = -(-n // per) * per
    flat = jnp.pad(flat, [(0, 0)] * (flat.ndim - 1) + [(0, total - n)])
    return flat.reshape(*flat.shape[:-1], total // PACK_W, PACK_W)


def _to_parts(full, axis, n=N_DEV):
    shp = full.shape
    t = full.reshape(*shp[:axis], n, shp[axis] // n, *shp[axis + 1:])
    return jnp.moveaxis(t, axis, 0)


def _from_parts(parts, axis):
    t = jnp.moveaxis(parts, 0, axis)
    shp = t.shape
    return t.reshape(*shp[:axis], shp[axis] * shp[axis + 1], *shp[axis + 2:])


BIG = tuple(n for n, (_, mm) in SHARDED.items() if mm)
SMALL = tuple(n for n, (_, mm) in SHARDED.items() if not mm)
SMALL_W = 512


def _small_rows(tree, lead):
    rows = []
    for n in SMALL:
        t = tree[n]
        t = t.reshape(*lead, -1, t.shape[-1])
        rows.append(jnp.pad(t, [(0, 0)] * (t.ndim - 1) + [(0, SMALL_W - t.shape[-1])]))
    buf = jnp.concatenate(rows, axis=-2)
    return jnp.pad(buf, [(0, 0)] * (buf.ndim - 2) + [(0, 16 - buf.shape[-2]), (0, 0)])


def _small_unrows(buf, like, lead):
    out, r = {}, 0
    for n in SMALL:
        shp = like[n].shape
        k = like[n].size // shp[-1]
        out[n] = buf[..., r:r + k, :shp[-1]].reshape(*lead, *shp)
        r += k
    return out


GROUPS = dict(hg=(("a_w_in", 0), ("a_w_out", 0)), f0=(("ffn_w_in", 0), ("ffn_w_out", 0)),
              mb=(("b_w_in", 0), ("b_w_out", 0)), f1=(("ffn_w_in", 1), ("ffn_w_out", 1)))


class _Comm:
    def __init__(self, local):
        x, y, c = _mesh_pos()
        self.me = 4 * x + 2 * y + c
        self.local = local
        self.shards = {g: [local[n][i].astype(BF16) for n, i in names] for g, names in GROUPS.items()}
        self.shards["f0"].append(_small_rows(local, ()))
        self.first = _all_gather(self.shards["hg"], "gather_hg")
        self.gathers, self.sent, token = {}, {}, None
        for g in ("f0", "mb", "f1"):
            self.gathers[g] = _copies_start(self.shards[g], gather=True, after=token, name=f"gather_{g}_start")
            token = self.gathers[g][-1]
        self.token = token

    def weights(self, group, after):
        if group == "hg":
            got = self.first
        else:
            lands = _copies_wait(self.gathers[group], gather=True, after=after, name=f"gather_{group}_wait")
            got = [lax.dynamic_update_index_in_dim(land, shard, self.me, 0)
                   for land, shard in zip(lands, self.shards[group])]
        out = dict(w_in=_from_parts(got[0], 1), w_out=_from_parts(got[1], 0))
        if group == "hg":
            out["token"] = self.token
        if group == "f0":
            small = _small_unrows(got[2], self.local, (N_DEV,))
            out["small"] = {n: _from_parts(small[n], SHARDED[n][0]) for n in SMALL}
        return out

    def send(self, group, grads):
        w_in = grads["w_in"]
        if isinstance(w_in, tuple):
            half = N_DEV // len(w_in)
            parts_in = jnp.concatenate([_to_parts(t, 1, half) for t in w_in], axis=0)
        else:
            parts_in = w_in if w_in.ndim == 3 else _to_parts(w_in, 1)
        parts = [parts_in, _to_parts(grads["w_out"], 0)]
        if "small" in grads:
            parts.append(_small_rows({n: _to_parts(grads["small"][n], SHARDED[n][0]) for n in SMALL}, (N_DEV,)))
        sent = [p.astype(BF16) for p in parts[:2]] + parts[2:] if group == "hg" else parts
        self.sent[group] = (parts, _copies_start(sent, gather=False, after=None, name=f"exchange_{group}_start"))
        return self.sent[group][1][-1]

    def finish(self, after, mom, var):
        res = {}
        for group in ("f1", "mb", "f0", "hg"):
            parts, started = self.sent[group]
            lands = _copies_wait(started, gather=False, after=after, name=f"exchange_{group}_wait")
            own = [lax.dynamic_index_in_dim(p, self.me, 0, keepdims=False) for p in parts]
            for (n, i), got, mine in zip(GROUPS[group], lands, own):
                res[n] = _adamw(got, self.local[n], mom[n], var[n], own=mine, layer=i, into=res.get(n),
                                name=f"adamw_{n}_{i}", tile=_pick(mine.shape[0], (256, 176, 128)))
                after = res[n][0]
        small = _adamw(lands[2], *[_small_rows(t, ()) for t in (self.local, mom, var)], own=own[2],
                       name="adamw_small", tile=16)
        return res, small


def _pack_small(tree, extra):
    flat = jnp.concatenate([tree[n].reshape(-1) for n in REPLICATED] + [extra.reshape(-1)])
    return _pad_rows(flat, 8)


def _unpack_small(pack, like):
    flat, out, off = pack.reshape(-1), {}, 0
    for n in REPLICATED:
        out[n] = flat[off:off + like[n].size].reshape(like[n].shape)
        off += like[n].size
    return out, flat[off]


def kernel(x, norm1_w, norm2_w, a_w_in, a_lb_logits, a_norm_w, a_w_out, b_w_in, b_conv_w, b_conv_b, b_dt_bias, b_a_log, b_d_skip, b_norm_w, b_w_out, ffn_w_in, ffn_conv_w, ffn_conv_b, ffn_w_out, final_norm_w, loss_target, m_norm1_w, m_norm2_w, m_a_w_in, m_a_lb_logits, m_a_norm_w, m_a_w_out, m_b_w_in, m_b_conv_w, m_b_conv_b, m_b_dt_bias, m_b_a_log, m_b_d_skip, m_b_norm_w, m_b_w_out, m_ffn_w_in, m_ffn_conv_w, m_ffn_conv_b, m_ffn_w_out, m_final_norm_w, v_norm1_w, v_norm2_w, v_a_w_in, v_a_lb_logits, v_a_norm_w, v_a_w_out, v_b_w_in, v_b_conv_w, v_b_conv_b, v_b_dt_bias, v_b_a_log, v_b_d_skip, v_b_norm_w, v_b_w_out, v_ffn_w_in, v_ffn_conv_w, v_ffn_conv_b, v_ffn_w_out, v_final_norm_w):
    given = dict(locals())
    local = {n: given[n] for n in WEIGHTS}
    mom = {n: given["m_" + n] for n in WEIGHTS}
    var = {n: given["v_" + n] for n in WEIGHTS}

    comm = _Comm(local)
    loss, grad_x, grads = _local_step(x[0], loss_target[0], local, comm)

    res, small_res = comm.finish(grad_x, mom, var)
    outs = ({}, {}, {}, {})
    for n in BIG:
        for out, r in zip(outs, res[n]):
            out[n] = r
    for out, r in zip(outs, small_res):
        out.update(_small_unrows(r, local, ()))
    out_g, out_d, out_m, out_v = outs

    small = _pack_small(grads, loss)
    rows = small.shape[0]
    got_s = _all_gather_small(small, "gather_small")
    zero = jnp.zeros((1,), F32)
    g, dlt, nm, nv = _adamw(got_s, _pack_small(local, zero), _pack_small(mom, zero), _pack_small(var, zero),
                            name="adamw_replicated", tile=rows)
    (rep_g, total), (rep_d, _), (rep_m, _), (rep_v, _) = (_unpack_small(t, local) for t in (g, dlt, nm, nv))
    out_g.update(rep_g)
    out_d.update(rep_d)
    out_m.update(rep_m)
    out_v.update(rep_v)

    return (total, grad_x[None], *[out_g[n] for n in WEIGHTS], *[out_d[n] for n in WEIGHTS],
            *[out_m[n] for n in WEIGHTS], *[out_v[n] for n in WEIGHTS])
```
